```python
import math
import jax, jax.numpy as jnp
from jax import lax
import numpy as np

D_MODEL = 1024
BATCH = 4
SEQ = 8192
DEPTH = 4

CHUNK = 64
N_META = 16
PAD_FRONT = 128 - N_META
Q_BLOCK = 128
NORM_EPS = 1e-6
NEG_INF = -1e30
A_HEADS = 8
A_DIM = 64
IDX_HEADS = 8
IDX_DIM = 64
IDX_SCALE = (IDX_DIM ** -0.5) * (IDX_HEADS ** -0.5)
TOPK_MAX = 256
B_HEADS = 4
B_DK = 64
B_DV = 128
GLA_GATE_RANK = 16
GLA_TAU = 16.0
C_HEADS = 4
C_DK = 64
C_DV = 128
ROPE_BASE = 10000.0
D_HEADS = 8
D_DIM = 64
REL_BUCKETS = 32
REL_MAX_DIST = 128
SPLIT_AB = (A_HEADS * A_DIM, A_HEADS * A_DIM, A_HEADS * A_DIM, A_HEADS * A_DIM,
            IDX_HEADS * IDX_DIM, IDX_DIM, IDX_HEADS,
            B_HEADS * B_DK, B_HEADS * B_DK, B_HEADS * B_DV, B_HEADS * B_DV, GLA_GATE_RANK)
SPLIT_CD = (C_HEADS * C_DK, C_HEADS * C_DK, C_HEADS * C_DV, C_HEADS * C_DV,
            D_HEADS * D_DIM, D_HEADS * D_DIM, D_HEADS * D_DIM, D_HEADS * D_DIM)
W_AB = sum(SPLIT_AB)
W_CD = sum(SPLIT_CD)
MIX_AB = A_HEADS * A_DIM + B_HEADS * B_DV
MIX_CD = C_HEADS * C_DV + D_HEADS * D_DIM

kernel_name = 'hybrid_dsa_gla_retnet_stickbreak'


def rms_norm(x, g):
    xf = x.astype(jnp.float32)
    y = xf * lax.rsqrt(jnp.mean(xf * xf, -1, keepdims=True) + NORM_EPS)
    return (y * g.astype(jnp.float32)).astype(x.dtype)


def head_rms(o):
    return o * lax.rsqrt(jnp.mean(o * o, -1, keepdims=True) + NORM_EPS)


def head_layer_norm(o):
    o = o - jnp.mean(o, -1, keepdims=True)
    return o * lax.rsqrt(jnp.mean(o * o, -1, keepdims=True) + NORM_EPS)


def split_cols(t, sizes):
    return jnp.split(t, np.cumsum(sizes)[:-1].tolist(), axis=-1)


def rel_bucket(rel):
    half = REL_BUCKETS // 2
    max_exact = half // 2
    n = -rel
    ret = jnp.where(n < 0, half, 0)
    n = jnp.abs(n)
    nf = jnp.maximum(n, 1).astype(jnp.float32)
    large = max_exact + (jnp.log(nf / max_exact) / math.log(REL_MAX_DIST / max_exact)
                         * (half - max_exact)).astype(jnp.int32)
    large = jnp.minimum(large, half - 1)
    return ret + jnp.where(n < max_exact, n, large)


def rotary(x, pos):
    half = x.shape[-1] // 2
    inv = ROPE_BASE ** (-jnp.arange(half, dtype=jnp.float32) / half)
    ang = pos.astype(jnp.float32)[:, None] * inv[None, :]
    cos = jnp.cos(ang)[None, :, None, :]
    sin = jnp.sin(ang)[None, :, None, :]
    x1, x2 = x[..., :half], x[..., half:]
    return jnp.concatenate([x1 * cos - x2 * sin, x1 * sin + x2 * cos], -1)


def to_chunks(t):
    b, p, h, d = t.shape
    return t.reshape(b, p // CHUNK, CHUNK, h, d).transpose(0, 3, 1, 2, 4)


def from_chunks(t):
    b, h, n, c, e = t.shape
    return t.transpose(0, 2, 3, 1, 4).reshape(b, n * c, h, e)


def dsa_attention(q, k, v, iq, ik, iw, rel_bias, chunk, valid, topk):
    bsz, P, H, dh = q.shape
    take = jax.vmap(lambda t, ii: t[ii])

    def block(i):
        s0 = i * Q_BLOCK
        qb = lax.dynamic_slice_in_dim(q, s0, Q_BLOCK, axis=1)
        iqb = lax.dynamic_slice_in_dim(iq, s0, Q_BLOCK, axis=1)
        iwb = lax.dynamic_slice_in_dim(iw, s0, Q_BLOCK, axis=1)
        qpos = s0 + jnp.arange(Q_BLOCK, dtype=jnp.int32)
        qchunk = qpos // CHUNK
        adm = valid[None, :] & (chunk[None, :] <= qchunk[:, None])
        sc = jax.nn.relu(jnp.einsum('bqhd,bsd->bqhs', iqb, ik).astype(jnp.float32))
        score = jnp.einsum('bqhs,bqh->bqs', sc, iwb.astype(jnp.float32)) * IDX_SCALE
        score = jnp.where(adm[None], score, NEG_INF)
        _, idx = lax.top_k(score, topk)
        kg = take(k, idx)
        vg = take(v, idx)
        logits = jnp.einsum('bqhd,bqkhd->bhqk', qb, kg).astype(jnp.float32) * dh ** -0.5
        bias = rel_bias[rel_bucket(idx - qpos[None, :, None])]
        logits = logits + jnp.transpose(bias, (0, 3, 1, 2)).astype(jnp.float32)
        ok = valid[idx] & (chunk[idx] <= qchunk[None, :, None])
        logits = jnp.where(ok[:, None], logits, NEG_INF)
        p = jax.nn.softmax(logits, axis=-1).astype(v.dtype)
        return jnp.einsum('bhqk,bqkhd->bqhd', p, vg)

    out = lax.map(block, jnp.arange(P // Q_BLOCK))
    return jnp.transpose(out, (1, 0, 2, 3, 4)).reshape(bsz, P, H, dh)


def gla_chunked(q, k, v, log_a):
    dk = q.shape[-1]
    q, k, v, log_a = to_chunks(q) * dk ** -0.5, to_chunks(k), to_chunks(v), to_chunks(log_a)
    bcum = jnp.cumsum(log_a, axis=3)
    b_last = bcum[:, :, :, -1:, :]
    q_t = q * jnp.exp(bcum)
    k_t = k * jnp.exp(-bcum)
    causal = jnp.tril(jnp.ones((CHUNK, CHUNK), dtype=bool))
    att = jnp.where(causal, jnp.einsum('bhncd,bhnsd->bhncs', q_t, k_t), 0.0)
    o_intra = jnp.einsum('bhncs,bhnse->bhnce', att, v)
    contrib = jnp.einsum('bhncd,bhnce->bhnde', k * jnp.exp(b_last - bcum), v)
    decay = jnp.exp(b_last[:, :, :, 0, :])

    def step(S, inp):
        dec, con = inp
        return S * dec[..., None] + con, S

    S0 = jnp.zeros(contrib.shape[:2] + contrib.shape[3:], jnp.float32)
    _, S_before = lax.scan(step, S0, (jnp.moveaxis(decay, 2, 0), jnp.moveaxis(contrib, 2, 0)))
    S_before = jnp.moveaxis(S_before, 0, 2)
    o_inter = jnp.einsum('bhncd,bhnde->bhnce', q_t, S_before)
    return from_chunks(o_intra + o_inter)


def retention_chunked(q, k, v, log_gamma):
    dk = q.shape[-1]
    q, k, v = to_chunks(q), to_chunks(k) * dk ** -0.5, to_chunks(v)
    i = jnp.arange(CHUNK, dtype=jnp.float32)
    diff = i[:, None] - i[None, :]
    dmat = jnp.where(diff >= 0, jnp.exp(log_gamma[:, None, None] * jnp.maximum(diff, 0.0)), 0.0)
    att = jnp.einsum('bhncd,bhnsd->bhncs', q, k) * dmat[None, :, None]
    o_intra = jnp.einsum('bhncs,bhnse->bhnce', att, v)
    zeta = jnp.exp(log_gamma[:, None] * (CHUNK - 1 - i))
    xi = jnp.exp(log_gamma[:, None] * (i + 1))
    contrib = jnp.einsum('bhncd,bhnce->bhnde', k * zeta[None, :, None, :, None], v)
    chunk_decay = jnp.exp(log_gamma * CHUNK)[None, :, None, None]

    def step(S, con):
        return S * chunk_decay + con, S

    S0 = jnp.zeros(contrib.shape[:2] + contrib.shape[3:], jnp.float32)
    _, S_before = lax.scan(step, S0, jnp.moveaxis(contrib, 2, 0))
    S_before = jnp.moveaxis(S_before, 0, 2)
    o_inter = jnp.einsum('bhncd,bhnde->bhnce', q * xi[None, :, None, :, None], S_before)
    return from_chunks(o_intra + o_inter)


def stick_breaking(q, k, v, valid):
    bsz, P, H, dh = q.shape
    kpos = jnp.arange(P, dtype=jnp.int32)

    def block(i):
        s0 = i * Q_BLOCK
        qb = lax.dynamic_slice_in_dim(q, s0, Q_BLOCK, axis=1)
        qpos = s0 + jnp.arange(Q_BLOCK, dtype=jnp.int32)
        z = jnp.einsum('bqhd,bshd->bhqs', qb, k).astype(jnp.float32) * dh ** -0.5
        ok = ((kpos[None, :] < qpos[:, None]) & valid[None, :])[None, None]
        log_1m = jnp.where(ok, -jax.nn.softplus(z), 0.0)
        after = lax.cumsum(log_1m, axis=3, reverse=True) - log_1m
        w = jnp.where(ok, jnp.exp(jax.nn.log_sigmoid(z) + after), 0.0)
        return jnp.einsum('bhqs,bshd->bqhd', w.astype(v.dtype), v)

    out = lax.map(block, jnp.arange(P // Q_BLOCK))
    return jnp.transpose(out, (1, 0, 2, 3, 4)).reshape(bsz, P, H, dh)


def layer_ab(h, w_in, gate_w2, gate_b, w_out, rel_bias, chunk, valid, topk):
    bsz, P, _ = h.shape
    aq, ak, av, ag, iq, ik, iw, bq, bk, bv, bg, ba = split_cols(h @ w_in, SPLIT_AB)
    hd = lambda t, nh: t.reshape(bsz, P, nh, -1)
    oa = dsa_attention(hd(aq, A_HEADS), hd(ak, A_HEADS), hd(av, A_HEADS),
                       hd(iq, IDX_HEADS), ik, iw, rel_bias, chunk, valid, topk)
    oa = oa.reshape(bsz, P, -1) * jax.nn.silu(ag)
    log_a = jax.nn.log_sigmoid((ba @ gate_w2 + gate_b).astype(jnp.float32)) / GLA_TAU
    f = lambda t, nh: hd(t, nh).astype(jnp.float32)
    ob = gla_chunked(f(bq, B_HEADS), f(bk, B_HEADS), f(bv, B_HEADS), hd(log_a, B_HEADS))
    ob = head_rms(ob).reshape(bsz, P, -1).astype(h.dtype) * jax.nn.silu(bg)
    return jnp.concatenate([oa, ob], axis=-1) @ w_out


def layer_cd(h, w_in, w_out, log_gamma, pos, valid):
    bsz, P, _ = h.shape
    cq, ck, cv, cg, dq, dk, dv, dg = split_cols(h @ w_in, SPLIT_CD)
    hd = lambda t, nh: t.reshape(bsz, P, nh, -1)
    f = lambda t, nh: hd(t, nh).astype(jnp.float32)
    oc = retention_chunked(rotary(f(cq, C_HEADS), pos), rotary(f(ck, C_HEADS), pos),
                           f(cv, C_HEADS), log_gamma)
    oc = head_layer_norm(oc).reshape(bsz, P, -1).astype(h.dtype) * jax.nn.silu(cg)
    od = stick_breaking(hd(dq, D_HEADS), hd(dk, D_HEADS), hd(dv, D_HEADS), valid)
    od = od.reshape(bsz, P, -1) * jax.nn.silu(dg)
    return jnp.concatenate([oc, od], axis=-1) @ w_out


def setup_inputs(seed: int = 0) -> dict:
    key = jax.random.key(seed)
    ks = jax.random.split(key, 11)
    n_even = (DEPTH + 1) // 2
    n_odd = DEPTH // 2
    f32 = jnp.float32
    nrm = lambda k, shape, fan: jax.random.normal(k, shape, f32) * fan ** -0.5
    return {
        'x': jax.random.normal(ks[0], (BATCH, SEQ, D_MODEL), f32),
        'meta_tokens': jax.random.normal(ks[1], (N_META, D_MODEL), f32),
        'rel_bias': 0.1 * jax.random.normal(ks[2], (REL_BUCKETS, A_HEADS), f32),
        'norm_g': 1.0 + 0.02 * jax.random.normal(ks[3], (DEPTH, D_MODEL), f32),
        'final_g': 1.0 + 0.02 * jax.random.normal(ks[4], (D_MODEL,), f32),
        'w_in_ab': nrm(ks[5], (n_even, D_MODEL, W_AB), D_MODEL),
        'gla_gate_w2': nrm(ks[6], (n_even, GLA_GATE_RANK, B_HEADS * B_DK), GLA_GATE_RANK),
        'gla_gate_b': 0.1 * jax.random.normal(ks[7], (n_even, B_HEADS * B_DK), f32),
        'w_out_ab': nrm(ks[8], (n_even, MIX_AB, D_MODEL), MIX_AB),
        'w_in_cd': nrm(ks[9], (n_odd, D_MODEL, W_CD), D_MODEL),
        'w_out_cd': nrm(ks[10], (n_odd, MIX_CD, D_MODEL), MIX_CD),
    }


def reference(x, meta_tokens, rel_bias, norm_g, final_g, w_in_ab, gla_gate_w2, gla_gate_b,
              w_out_ab, w_in_cd, w_out_cd):
    bsz, seq, d = x.shape
    P = seq + PAD_FRONT + N_META
    h = jnp.concatenate([jnp.zeros((bsz, PAD_FRONT, d), x.dtype),
                         jnp.broadcast_to(meta_tokens.astype(x.dtype)[None], (bsz, N_META, d)),
                         x], axis=1)
    pos = jnp.arange(P, dtype=jnp.int32)
    chunk = pos // CHUNK
    valid = pos >= PAD_FRONT
    topk = min(TOPK_MAX, seq // 4)
    log_gamma = jnp.log(1.0 - jnp.exp2(-5.0 - jnp.arange(C_HEADS, dtype=jnp.float32)))
    for layer in range(DEPTH):
        hn = rms_norm(h, norm_g[layer])
        j = layer // 2
        if layer % 2 == 0:
            out = layer_ab(hn, w_in_ab[j], gla_gate_w2[j], gla_gate_b[j], w_out_ab[j],
                           rel_bias, chunk, valid, topk)
        else:
            out = layer_cd(hn, w_in_cd[j], w_out_cd[j], log_gamma, pos, valid)
        h = h + jnp.where(valid[None, :, None], out, 0.0).astype(h.dtype)
    return rms_norm(h[:, PAD_FRONT + N_META:], final_g)
```

```python
import functools
import math

import numpy as np
import jax
import jax.numpy as jnp
from jax import lax
from jax.experimental import pallas as pl
from jax.experimental.pallas import tpu as pltpu

F32 = jnp.float32
BF16 = jnp.bfloat16
I32 = jnp.int32

D_MODEL = 1024
CHUNK = 64
N_META = 16
PAD_FRONT = 128 - N_META
NORM_EPS = 1e-6
NEG_INF = -1e30
A_HEADS, A_DIM = 8, 64
IDX_HEADS, IDX_DIM = 8, 64
IDX_SCALE = (IDX_DIM ** -0.5) * (IDX_HEADS ** -0.5)
TOPK_MAX = 256
B_HEADS, B_DK, B_DV = 4, 64, 128
GLA_GATE_RANK = 16
GLA_TAU = 16.0
C_HEADS, C_DK, C_DV = 4, 64, 128
ROPE_BASE = 10000.0
D_HEADS, D_DIM = 8, 64
REL_BUCKETS = 32
REL_MAX_DIST = 128
SPLIT_AB = (512, 512, 512, 512, 512, 64, 8, 256, 256, 512, 512, 16)
SPLIT_CD = (256, 256, 512, 512, 512, 512, 512, 512)

LANES = 128
W_AB_PAD = 4480
W_CD = 3584
INT_MIN = -2 ** 31
IDX_BITS = 14
VMEM_LIMIT = 56 * 1024 * 1024

NT_DIMS = (((1,), (1,)), ((), ()))
TN_DIMS = (((0,), (0,)), ((), ()))


def _pick_tile(n, candidates):
    for c in candidates:
        if n % c == 0:
            return c
    raise ValueError(f"no tile for {n}")


def _params(sem):
    return pltpu.CompilerParams(dimension_semantics=sem, vmem_limit_bytes=VMEM_LIMIT)


def _silu(x):
    return x / (1.0 + jnp.exp(-x))


def _softplus_parts(z):
    t = jnp.log1p(jnp.exp(-jnp.abs(z)))
    return jnp.maximum(z, 0.0) + t, jnp.minimum(z, 0.0) - t


def _split3(x):
    a = x.astype(BF16)
    r = x - a.astype(F32)
    b = r.astype(BF16)
    c = (r - b.astype(F32)).astype(BF16)
    return a, b, c


def _inproj_kernel(x_ref, g_ref, w_ref, o_ref, hn_ref):
    @pl.when(pl.program_id(2) == 0)
    def _():
        x = x_ref[...]
        ms = jnp.mean(x * x, axis=-1, keepdims=True)
        hn_ref[...] = (x * lax.rsqrt(ms + NORM_EPS) * g_ref[...]).astype(BF16)

    o_ref[...] = jnp.dot(hn_ref[...], w_ref[...], preferred_element_type=F32).astype(o_ref.dtype)


def _inproj(h, g, w):
    bsz, p, d = h.shape
    n = w.shape[1]
    tp = _pick_tile(p, (1664, 1280, 640, 128))
    tn = _pick_tile(n, (896, 512, 128))
    return pl.pallas_call(
        _inproj_kernel,
        grid=(bsz, p // tp, n // tn),
        in_specs=[
            pl.BlockSpec((None, tp, d), lambda b, i, j: (b, i, 0)),
            pl.BlockSpec((1, d), lambda b, i, j: (0, 0)),
            pl.BlockSpec((d, tn), lambda b, i, j: (0, j)),
        ],
        out_specs=pl.BlockSpec((None, tp, tn), lambda b, i, j: (b, i, j)),
        out_shape=jax.ShapeDtypeStruct((bsz, p, n), BF16),
        scratch_shapes=[pltpu.VMEM((tp, d), BF16)],
        compiler_params=_params(("parallel", "parallel", "arbitrary")),
        name="inproj",
    )(h, g.reshape(1, d), w)


def _outproj_kernel(m1_ref, m2_ref, w1_ref, w2_ref, h_ref, o_ref, *, tp):
    y = jnp.dot(m1_ref[...], w1_ref[...], preferred_element_type=F32)
    y = y + jnp.dot(m2_ref[...], w2_ref[...], preferred_element_type=F32)
    pos = pl.program_id(1) * tp + lax.broadcasted_iota(I32, y.shape, 0)
    o_ref[...] = h_ref[...] + jnp.where(pos >= PAD_FRONT, y, 0.0)


def _outproj(m1, m2, w_out, h):
    bsz, p, d = h.shape
    k1 = m1.shape[-1]
    k2 = m2.shape[-1]
    tp = _pick_tile(p, (832, 640, 128))
    w = w_out.astype(BF16)
    return pl.pallas_call(
        functools.partial(_outproj_kernel, tp=tp),
        grid=(bsz, p // tp),
        in_specs=[
            pl.BlockSpec((None, tp, k1), lambda b, i: (b, i, 0)),
            pl.BlockSpec((None, tp, k2), lambda b, i: (b, i, 0)),
            pl.BlockSpec((k1, d), lambda b, i: (0, 0)),
            pl.BlockSpec((k2, d), lambda b, i: (0, 0)),
            pl.BlockSpec((None, tp, d), lambda b, i: (b, i, 0)),
        ],
        out_specs=pl.BlockSpec((None, tp, d), lambda b, i: (b, i, 0)),
        out_shape=jax.ShapeDtypeStruct((bsz, p, d), F32),
        compiler_params=_params(("parallel", "parallel")),
        name="outproj",
    )(m1, m2, w[:k1], w[k1:], h)


def _final_norm_kernel(x_ref, g_ref, o_ref):
    x = x_ref[...]
    ms = jnp.mean(x * x, axis=-1, keepdims=True)
    o_ref[...] = x * lax.rsqrt(ms + NORM_EPS) * g_ref[...]


def _final_norm(h, g, seq):
    bsz, p, d = h.shape
    skip = (p - seq) // LANES
    return pl.pallas_call(
        _final_norm_kernel,
        grid=(bsz, seq // LANES),
        in_specs=[
            pl.BlockSpec((None, LANES, d), lambda b, i: (b, i + skip, 0)),
            pl.BlockSpec((1, d), lambda b, i: (0, 0)),
        ],
        out_specs=pl.BlockSpec((None, LANES, d), lambda b, i: (b, i, 0)),
        out_shape=jax.ShapeDtypeStruct((bsz, seq, d), F32),
        compiler_params=_params(("parallel", "parallel")),
        name="final_norm",
    )(h, g.reshape(1, d))


def _dsa_kernel(aq_ref, ag_ref, iq_ref, iw_ref, k_ref, v_ref, ik_ref, bias_ref, o_ref,
                key_ref, iqs_ref, wts_ref, qm_ref, thr_ref, jdx_ref, m_ref, l_ref, acc_ref,
                *, topk):
    T = LANES
    i = pl.program_id(1)
    nkt = i + 1
    row = lax.broadcasted_iota(I32, (T, T), 0)
    col = lax.broadcasted_iota(I32, (T, T), 1)
    qchunk = (i * T + row) >> 6
    low = col < 64

    def admissible(kt):
        kpos = kt * T + col
        return (kpos >= PAD_FRONT) & ((kpos >> 6) <= qchunk)

    iw = iw_ref[...].astype(F32)
    for h in range(IDX_HEADS):
        pair = iq_ref[:, (h // 2) * T:(h // 2 + 1) * T]
        mine = low if h % 2 == 0 else ~low
        iqs_ref[h * T:(h + 1) * T, :] = jnp.where(mine, pair, jnp.zeros_like(pair))
        wts_ref[h] = jnp.broadcast_to(iw[:, h:h + 1], (T, T))
    for h in range(A_HEADS):
        pair = aq_ref[:, (h // 2) * T:(h // 2 + 1) * T]
        mine = low if h % 2 == 0 else ~low
        qm_ref[h] = jnp.where(mine, pair * (A_DIM ** -0.5), jnp.zeros_like(pair))

    def score_tile(kt, carry):
        ikt = ik_ref[pl.ds(pl.multiple_of(kt * T, T), T), :]
        s = lax.dot_general(iqs_ref[...], ikt, NT_DIMS, preferred_element_type=F32)
        tot = jnp.zeros((T, T), F32)
        for h in range(IDX_HEADS):
            tot = tot + jnp.maximum(s[h * T:(h + 1) * T], 0.0) * wts_ref[h]
        score = jnp.where(admissible(kt), tot * IDX_SCALE, NEG_INF)
        bits = lax.bitcast_convert_type(score, I32)
        key_ref[kt] = bits ^ ((bits >> 31) & 0x7FFFFFFF)
        return carry

    lax.fori_loop(0, nkt, score_tile, 0)

    def row_count(pred):
        def body(kt, acc):
            return acc + jnp.where(pred(kt), 1.0, 0.0)
        acc = lax.fori_loop(0, nkt, body, jnp.zeros((T, T), F32))
        return jnp.sum(acc, axis=-1, keepdims=True)

    def value_bit(it, t):
        cand = t + (jnp.int32(1) << (31 - it))
        cnt = row_count(lambda kt: key_ref[kt] >= cand)
        return jnp.where(cnt >= topk, cand, t)

    thr = lax.fori_loop(0, 32, value_bit, jnp.full((T, 1), INT_MIN, I32))
    thr_ref[...] = jnp.broadcast_to(thr, (T, T))
    jdx_ref[...] = jnp.full((T, T), 2 ** IDX_BITS, I32)

    cnt_gt = row_count(lambda kt: key_ref[kt] > thr)
    cnt_eq = row_count(lambda kt: key_ref[kt] == thr)
    need = topk - cnt_gt
    tied = jnp.max(jnp.where(cnt_eq > need, 1.0, 0.0)) > 0.0

    @pl.when(tied)
    def _():
        def index_bit(it, j):
            cand = j + (jnp.int32(1) << (IDX_BITS - 1 - it))
            cnt = row_count(lambda kt: (key_ref[kt] == thr) & (kt * T + col < cand))
            return jnp.where(cnt < need, cand, j)
        j = lax.fori_loop(0, IDX_BITS, index_bit, jnp.zeros((T, 1), I32))
        jdx_ref[...] = jnp.broadcast_to(j, (T, T))

    m_ref[...] = jnp.full(m_ref.shape, NEG_INF, F32)
    l_ref[...] = jnp.zeros(l_ref.shape, F32)
    acc_ref[...] = jnp.zeros(acc_ref.shape, F32)

    def attend_tile(kt, carry):
        key = key_ref[kt]
        thr_t = thr_ref[...]
        sel = (key > thr_t) | ((key == thr_t) & (kt * T + col <= jdx_ref[...]))
        ok = sel & admissible(kt)
        ks = pl.ds(pl.multiple_of(kt * T, T), T)
        d = jnp.minimum(i - kt, 2)
        for h in range(A_HEADS):
            hp = h // 2
            kt_tile = k_ref[ks, hp * T:(hp + 1) * T]
            vt_tile = v_ref[ks, hp * T:(hp + 1) * T]
            s = lax.dot_general(qm_ref[h], kt_tile, NT_DIMS, preferred_element_type=F32)
            s = jnp.where(ok, s + bias_ref[h, d], NEG_INF)
            m_old = m_ref[h]
            m_new = jnp.maximum(m_old, jnp.max(s, axis=-1, keepdims=True))
            alpha = jnp.exp(m_old - m_new)
            p = jnp.exp(s - m_new)
            l_ref[h] = alpha * l_ref[h] + jnp.sum(p, axis=-1, keepdims=True)
            acc_ref[h] = alpha * acc_ref[h] + jnp.dot(p.astype(BF16), vt_tile,
                                                      preferred_element_type=F32)
            m_ref[h] = m_new
        return carry

    lax.fori_loop(0, nkt, attend_tile, 0)

    for hp in range(A_HEADS // 2):
        o0 = acc_ref[2 * hp] / l_ref[2 * hp]
        o1 = acc_ref[2 * hp + 1] / l_ref[2 * hp + 1]
        g = ag_ref[:, hp * T:(hp + 1) * T].astype(F32)
        o_ref[:, hp * T:(hp + 1) * T] = (jnp.where(low, o0, o1) * _silu(g)).astype(o_ref.dtype)


def _dsa(pab, bias_tiles, topk):
    bsz, p, _ = pab.shape
    T = LANES
    nq = p // T
    hw = A_HEADS * A_DIM
    one = pl.Buffered(1)
    return pl.pallas_call(
        functools.partial(_dsa_kernel, topk=float(topk)),
        grid=(bsz, nq),
        in_specs=[
            pl.BlockSpec((None, T, hw), lambda b, i: (b, i, 0)),
            pl.BlockSpec((None, T, hw), lambda b, i: (b, i, 3)),
            pl.BlockSpec((None, T, hw), lambda b, i: (b, i, 4)),
            pl.BlockSpec((None, T, T), lambda b, i: (b, i, 33)),
            pl.BlockSpec((None, p, hw), lambda b, i: (b, 0, 1), pipeline_mode=one),
            pl.BlockSpec((None, p, hw), lambda b, i: (b, 0, 2), pipeline_mode=one),
            pl.BlockSpec((None, p, T), lambda b, i: (b, 0, 32), pipeline_mode=one),
            pl.BlockSpec((A_HEADS, 3, T, T), lambda b, i: (0, 0, 0, 0), pipeline_mode=one),
        ],
        out_specs=pl.BlockSpec((None, T, hw), lambda b, i: (b, i, 0)),
        out_shape=jax.ShapeDtypeStruct((bsz, p, hw), BF16),
        scratch_shapes=[
            pltpu.VMEM((nq, T, T), I32),
            pltpu.VMEM((IDX_HEADS * T, T), BF16),
            pltpu.VMEM((IDX_HEADS, T, T), F32),
            pltpu.VMEM((A_HEADS, T, T), BF16),
            pltpu.VMEM((T, T), I32),
            pltpu.VMEM((T, T), I32),
            pltpu.VMEM((A_HEADS, T, 1), F32),
            pltpu.VMEM((A_HEADS, T, 1), F32),
            pltpu.VMEM((A_HEADS, T, T), F32),
        ],
        compiler_params=_params(("parallel", "arbitrary")),
        name="dsa",
    )(pab, pab, pab, pab, pab, pab, pab, bias_tiles)


def _gla_kernel(q_ref, k_ref, v_ref, g_ref, a_ref, w2_ref, gb_ref, o_ref, st_ref, *, nchunk):
    C = CHUNK
    T = LANES

    @pl.when(pl.program_id(1) == 0)
    def _():
        st_ref[...] = jnp.zeros(st_ref.shape, F32)

    r_i = lax.broadcasted_iota(I32, (C, C), 0)
    c_i = lax.broadcasted_iota(I32, (C, C), 1)
    causal = c_i <= r_i
    tri = jnp.where(causal, 1.0, 0.0).astype(BF16)
    lane = lax.broadcasted_iota(I32, (C, T), 1)
    sd_r = lax.broadcasted_iota(I32, (2 * B_DV, T), 0)
    sd_c = lax.broadcasted_iota(I32, (2 * B_DV, T), 1)
    blockdiag = (sd_r >= B_DV) == (sd_c >= B_DK)

    def chunk(c, carry):
        r = pl.ds(pl.multiple_of(c * C, C), C)
        x = jnp.dot(a_ref[r, :], w2_ref[...], preferred_element_type=F32) + gb_ref[...]
        log_a = _softplus_parts(x)[1] * (1.0 / GLA_TAU)
        a1, a2, a3 = _split3(log_a)
        bcum = (jnp.dot(tri, a1, preferred_element_type=F32)
                + jnp.dot(tri, a2, preferred_element_type=F32)
                + jnp.dot(tri, a3, preferred_element_type=F32))
        b_last = bcum[C - 1:C, :]
        q = q_ref[r, :].astype(F32) * (B_DK ** -0.5)
        k = k_ref[r, :].astype(F32)
        q_t = (q * jnp.exp(bcum)).astype(BF16)
        k_t = (k * jnp.exp(-bcum)).astype(BF16)
        k_d = (k * jnp.exp(b_last - bcum)).astype(BF16)
        decay = jnp.exp(b_last)
        for hp in range(B_HEADS // 2):
            ls = slice(hp * T, (hp + 1) * T)
            st = st_ref[hp]
            o_inter = lax.dot_general(q_t[:, ls], st.astype(BF16), NT_DIMS,
                                      preferred_element_type=F32)
            for hh in range(2):
                h = 2 * hp + hh
                mine = (lane < B_DK) if hh == 0 else (lane >= B_DK)
                qm = jnp.where(mine, q_t[:, ls], jnp.zeros((C, T), BF16))
                att = lax.dot_general(qm, k_t[:, ls], NT_DIMS, preferred_element_type=F32)
                att = jnp.where(causal, att, 0.0)
                vs = slice(h * B_DV, (h + 1) * B_DV)
                o = jnp.dot(att.astype(BF16), v_ref[r, vs], preferred_element_type=F32)
                o = o + o_inter[:, hh * B_DV:(hh + 1) * B_DV]
                o = o * lax.rsqrt(jnp.mean(o * o, axis=-1, keepdims=True) + NORM_EPS)
                o = o * _silu(g_ref[r, vs].astype(F32))
                o_ref[r, vs] = o.astype(o_ref.dtype)
            vp = v_ref[r, hp * 2 * B_DV:(hp + 1) * 2 * B_DV]
            contrib = lax.dot_general(vp, k_d[:, ls], TN_DIMS, preferred_element_type=F32)
            st_ref[hp] = st * decay[:, ls] + jnp.where(blockdiag, contrib, 0.0)
        return carry

    lax.fori_loop(0, nchunk, chunk, 0)


def _gla(pab, w2, gb):
    bsz, p, _ = pab.shape
    tc = _pick_tile(p, (640, 128))
    hv = B_HEADS * B_DV
    hk = B_HEADS * B_DK
    return pl.pallas_call(
        functools.partial(_gla_kernel, nchunk=tc // CHUNK),
        grid=(bsz, p // tc),
        in_specs=[
            pl.BlockSpec((None, tc, hk), lambda b, c: (b, c, 14)),
            pl.BlockSpec((None, tc, hk), lambda b, c: (b, c, 15)),
            pl.BlockSpec((None, tc, hv), lambda b, c: (b, c, 5)),
            pl.BlockSpec((None, tc, hv), lambda b, c: (b, c, 6)),
            pl.BlockSpec((None, tc, LANES), lambda b, c: (b, c, 34)),
            pl.BlockSpec((LANES, hk), lambda b, c: (0, 0)),
            pl.BlockSpec((1, hk), lambda b, c: (0, 0)),
        ],
        out_specs=pl.BlockSpec((None, tc, hv), lambda b, c: (b, c, 0)),
        out_shape=jax.ShapeDtypeStruct((bsz, p, hv), BF16),
        scratch_shapes=[pltpu.VMEM((B_HEADS // 2, 2 * B_DV, 2 * B_DK), F32)],
        compiler_params=_params(("parallel", "arbitrary")),
        name="gla",
    )(pab, pab, pab, pab, pab, w2, gb)


def _ret_kernel(q_ref, k_ref, v_ref, g_ref, cos_ref, sin_ref, dmat_ref, zeta_ref, xi_ref,
                cdec_ref, o_ref, st_ref, *, nchunk):
    C = CHUNK
    T = LANES
    W = C_HEADS * C_DK

    @pl.when(pl.program_id(1) == 0)
    def _():
        st_ref[...] = jnp.zeros(st_ref.shape, F32)

    lane_w = lax.broadcasted_iota(I32, (C, W), 1)
    first_half = (lane_w & (C_DK - 1)) < (C_DK // 2)
    lane = lax.broadcasted_iota(I32, (C, T), 1)
    sd_r = lax.broadcasted_iota(I32, (2 * C_DV, T), 0)
    sd_c = lax.broadcasted_iota(I32, (2 * C_DV, T), 1)
    blockdiag = (sd_r >= C_DV) == (sd_c >= C_DK)

    def rotate(x, cos, sin_signed):
        swapped = jnp.where(first_half, pltpu.roll(x, W - C_DK // 2, 1), pltpu.roll(x, C_DK // 2, 1))
        return x * cos + swapped * sin_signed

    def chunk(c, carry):
        r = pl.ds(pl.multiple_of(c * C, C), C)
        cos = cos_ref[r, :]
        sin = sin_ref[r, :]
        q = rotate(q_ref[r, :].astype(F32), cos, sin)
        k = rotate(k_ref[r, :].astype(F32), cos, sin) * (C_DK ** -0.5)
        q_b = q.astype(BF16)
        k_b = k.astype(BF16)
        q_x = (q * xi_ref[...]).astype(BF16)
        k_z = (k * zeta_ref[...]).astype(BF16)
        for hp in range(C_HEADS // 2):
            ls = slice(hp * T, (hp + 1) * T)
            st = st_ref[hp]
            o_inter = lax.dot_general(q_x[:, ls], st.astype(BF16), NT_DIMS,
                                      preferred_element_type=F32)
            for hh in range(2):
                h = 2 * hp + hh
                mine = (lane < C_DK) if hh == 0 else (lane >= C_DK)
                qm = jnp.where(mine, q_b[:, ls], jnp.zeros((C, T), BF16))
                att = lax.dot_general(qm, k_b[:, ls], NT_DIMS, preferred_element_type=F32)
                att = att * dmat_ref[h]
                vs = slice(h * C_DV, (h + 1) * C_DV)
                o = jnp.dot(att.astype(BF16), v_ref[r, vs], preferred_element_type=F32)
                o = o + o_inter[:, hh * C_DV:(hh + 1) * C_DV]
                o = o - jnp.mean(o, axis=-1, keepdims=True)
                o = o * lax.rsqrt(jnp.mean(o * o, axis=-1, keepdims=True) + NORM_EPS)
                o = o * _silu(g_ref[r, vs].astype(F32))
                o_ref[r, vs] = o.astype(o_ref.dtype)
            vp = v_ref[r, hp * 2 * C_DV:(hp + 1) * 2 * C_DV]
            contrib = lax.dot_general(vp, k_z[:, ls], TN_DIMS, preferred_element_type=F32)
            st_ref[hp] = st * cdec_ref[:, ls] + jnp.where(blockdiag, contrib, 0.0)
        return carry

    lax.fori_loop(0, nchunk, chunk, 0)


def _retention_tables(p):
    log_gamma = np.log(1.0 - np.exp2(-5.0 - np.arange(C_HEADS, dtype=np.float64)))
    i = np.arange(CHUNK, dtype=np.float64)
    diff = i[:, None] - i[None, :]
    dmat = np.where(diff >= 0, np.exp(log_gamma[:, None, None] * np.maximum(diff, 0.0)), 0.0)
    zeta = np.exp(log_gamma[:, None] * (CHUNK - 1 - i))
    xi = np.exp(log_gamma[:, None] * (i + 1))
    cdec = np.exp(log_gamma * CHUNK)
    widen = lambda t: np.repeat(t.T[:, :, None], C_DK, axis=2).reshape(CHUNK, C_HEADS * C_DK)
    half = C_DK // 2
    inv = jnp.asarray(ROPE_BASE, F32) ** (-jnp.arange(half, dtype=F32) / half)
    ang = jnp.arange(p, dtype=jnp.int32).astype(F32)[:, None] * inv[None, :]
    cos = jnp.tile(jnp.cos(ang), (1, 2 * C_HEADS))
    sin = jnp.sin(ang)
    sin_signed = jnp.tile(jnp.concatenate([-sin, sin], axis=1), (1, C_HEADS))
    return (cos, sin_signed, jnp.asarray(dmat, F32), jnp.asarray(widen(zeta), F32),
            jnp.asarray(widen(xi), F32),
            jnp.asarray(np.repeat(cdec, C_DK)[None, :], F32))


def _retention(pcd, tables):
    bsz, p, _ = pcd.shape
    cos, sin, dmat, zeta, xi, cdec = tables
    tc = _pick_tile(p, (640, 128))
    hv = C_HEADS * C_DV
    hk = C_HEADS * C_DK
    full = lambda shape: pl.BlockSpec(shape, lambda b, c: (0,) * len(shape))
    return pl.pallas_call(
        functools.partial(_ret_kernel, nchunk=tc // CHUNK),
        grid=(bsz, p // tc),
        in_specs=[
            pl.BlockSpec((None, tc, hk), lambda b, c: (b, c, 0)),
            pl.BlockSpec((None, tc, hk), lambda b, c: (b, c, 1)),
            pl.BlockSpec((None, tc, hv), lambda b, c: (b, c, 1)),
            pl.BlockSpec((None, tc, hv), lambda b, c: (b, c, 2)),
            pl.BlockSpec((tc, hk), lambda b, c: (c, 0)),
            pl.BlockSpec((tc, hk), lambda b, c: (c, 0)),
            full((C_HEADS, CHUNK, CHUNK)),
            full((CHUNK, hk)),
            full((CHUNK, hk)),
            full((1, hk)),
        ],
        out_specs=pl.BlockSpec((None, tc, hv), lambda b, c: (b, c, 0)),
        out_shape=jax.ShapeDtypeStruct((bsz, p, hv), BF16),
        scratch_shapes=[pltpu.VMEM((C_HEADS // 2, 2 * C_DV, 2 * C_DK), F32)],
        compiler_params=_params(("parallel", "arbitrary")),
        name="retention",
    )(pcd, pcd, pcd, pcd, cos, sin, dmat, zeta, xi, cdec)


def _sb_kernel(q_ref, g_ref, k_ref, v_ref, o_ref):
    T = LANES
    i = pl.program_id(2)
    row = lax.broadcasted_iota(I32, (T, T), 0)
    col = lax.broadcasted_iota(I32, (T, T), 1)
    qpos = i * T + row
    low = col < D_DIM
    u_r = lax.broadcasted_iota(I32, (T, 2 * T), 0)
    u_c = lax.broadcasted_iota(I32, (T, 2 * T), 1)
    suffix = jnp.where((u_c >= T) | (u_r > u_c), 1.0, 0.0).astype(BF16)

    q = q_ref[...]
    outs = []
    for hh in range(2):
        qm = jnp.where(low if hh == 0 else ~low, q * (D_DIM ** -0.5), jnp.zeros_like(q))

        def tile(t, state, qm=qm):
            run, acc = state
            kt = i - t
            ks = pl.ds(pl.multiple_of(kt * T, T), T)
            kpos = kt * T + col
            ok = (kpos < qpos) & (kpos >= PAD_FRONT)
            z = lax.dot_general(qm, k_ref[ks, :], NT_DIMS, preferred_element_type=F32)
            sp, logsig = _softplus_parts(z)
            log_1m = jnp.where(ok, -sp, 0.0)
            hi = log_1m.astype(BF16)
            lo = (log_1m - hi.astype(F32)).astype(BF16)
            sums = (jnp.dot(hi, suffix, preferred_element_type=F32)
                    + jnp.dot(lo, suffix, preferred_element_type=F32))
            w = jnp.where(ok, jnp.exp(logsig + run + sums[:, :T]), 0.0)
            acc = acc + jnp.dot(w.astype(BF16), v_ref[ks, :], preferred_element_type=F32)
            return run + sums[:, T:], acc

        zero = jnp.zeros((T, T), F32)
        _, acc = lax.fori_loop(0, i + 1, tile, (zero, zero))
        outs.append(acc)

    o = jnp.where(low, outs[0], outs[1]) * _silu(g_ref[...].astype(F32))
    o_ref[...] = o.astype(o_ref.dtype)


def _stick_breaking(pcd):
    bsz, p, _ = pcd.shape
    T = LANES
    npair = D_HEADS // 2
    return pl.pallas_call(
        _sb_kernel,
        grid=(bsz, npair, p // T),
        in_specs=[
            pl.BlockSpec((None, T, T), lambda b, h, i: (b, i, 12 + h)),
            pl.BlockSpec((None, T, T), lambda b, h, i: (b, i, 24 + h)),
            pl.BlockSpec((None, p, T), lambda b, h, i: (b, 0, 16 + h)),
            pl.BlockSpec((None, p, T), lambda b, h, i: (b, 0, 20 + h)),
        ],
        out_specs=pl.BlockSpec((None, T, T), lambda b, h, i: (b, i, h)),
        out_shape=jax.ShapeDtypeStruct((bsz, p, D_HEADS * D_DIM), BF16),
        compiler_params=_params(("parallel", "parallel", "arbitrary")),
        name="stick_breaking",
    )(pcd, pcd, pcd, pcd)


def _rel_bucket_np(rel):
    half = REL_BUCKETS // 2
    max_exact = half // 2
    n = -rel
    ret = np.where(n < 0, half, 0)
    n = np.abs(n)
    edges = [math.ceil(max_exact * (REL_MAX_DIST / max_exact) ** (j / (half - max_exact)) - 1e-9)
             for j in range(1, half - max_exact)]
    large = max_exact + sum((n >= e).astype(np.int64) for e in edges)
    return ret + np.where(n < max_exact, n, large)


def _bias_tiles(rel_bias):
    i = np.arange(LANES)[:, None]
    j = np.arange(LANES)[None, :]
    idx = np.stack([_rel_bucket_np(j - i - LANES * d) for d in range(3)])
    return jnp.transpose(rel_bias.astype(F32)[idx], (3, 0, 1, 2))


def _layout_ab(w):
    aq, ak, av, ag, iq, ik, iw, bq, bk, bv, bg, ba = jnp.split(w, np.cumsum(SPLIT_AB)[:-1].tolist(), axis=1)
    pad = lambda t: jnp.pad(t, ((0, 0), (0, LANES - t.shape[1])))
    cols = [aq, ak, av, ag, iq, bv, bg, bq, bk, jnp.concatenate([ik, ik], axis=1), pad(iw), pad(ba)]
    return jnp.concatenate(cols, axis=1).astype(BF16)


def kernel(x, meta_tokens, rel_bias, norm_g, final_g, w_in_ab, gla_gate_w2, gla_gate_b, w_out_ab,
           w_in_cd, w_out_cd):
    bsz, seq, d = x.shape
    p = seq + PAD_FRONT + N_META
    depth = norm_g.shape[0]
    topk = min(TOPK_MAX, seq // 4)
    h = jnp.concatenate([jnp.zeros((bsz, PAD_FRONT, d), x.dtype),
                         jnp.broadcast_to(meta_tokens.astype(x.dtype)[None], (bsz, N_META, d)),
                         x], axis=1)
    bias_tiles = _bias_tiles(rel_bias)
    tables = _retention_tables(p)
    for layer in range(depth):
        j = layer // 2
        if layer % 2 == 0:
            pab = _inproj(h, norm_g[layer], _layout_ab(w_in_ab[j]))
            oa = _dsa(pab, bias_tiles, topk)
            w2 = jnp.pad(gla_gate_w2[j], ((0, LANES - GLA_GATE_RANK), (0, 0))).astype(BF16)
            ob = _gla(pab, w2, gla_gate_b[j].reshape(1, -1).astype(F32))
            h = _outproj(oa, ob, w_out_ab[j], h)
        else:
            pcd = _inproj(h, norm_g[layer], w_in_cd[j].astype(BF16))
            oc = _retention(pcd, tables)
            od = _stick_breaking(pcd)
            h = _outproj(oc, od, w_out_cd[j], h)
    return _final_norm(h, final_g, seq)
```

```python
import functools
import math

import numpy as np
import jax
import jax.numpy as jnp
from jax import lax
from jax.experimental import pallas as pl
from jax.experimental.pallas import tpu as pltpu

F32 = jnp.float32
BF16 = jnp.bfloat16
I32 = jnp.int32

D_MODEL = 1024
CHUNK = 64
N_META = 16
PAD_FRONT = 128 - N_META
NORM_EPS = 1e-6
NEG_INF = -1e30
A_HEADS, A_DIM = 8, 64
IDX_HEADS, IDX_DIM = 8, 64
IDX_SCALE = (IDX_DIM ** -0.5) * (IDX_HEADS ** -0.5)
TOPK_MAX = 256
B_HEADS, B_DK, B_DV = 4, 64, 128
GLA_GATE_RANK = 16
GLA_TAU = 16.0
C_HEADS, C_DK, C_DV = 4, 64, 128
ROPE_BASE = 10000.0
D_HEADS, D_DIM = 8, 64
REL_BUCKETS = 32
REL_MAX_DIST = 128
SPLIT_AB = (512, 512, 512, 512, 512, 64, 8, 256, 256, 512, 512, 16)
SPLIT_CD = (256, 256, 512, 512, 512, 512, 512, 512)

LANES = 128
W_AB_PAD = 4480
W_CD = 3584
INT_MIN = -2 ** 31
IDX_BITS = 14
LOG_F32_UNDERFLOW = -104.0
VMEM_LIMIT = 56 * 1024 * 1024

NT_DIMS = (((1,), (1,)), ((), ()))
TN_DIMS = (((0,), (0,)), ((), ()))


def _pick_tile(n, candidates):
    for c in candidates:
        if n % c == 0:
            return c
    raise ValueError(f"no tile for {n}")


def _params(sem):
    return pltpu.CompilerParams(dimension_semantics=sem, vmem_limit_bytes=VMEM_LIMIT)


def _silu(x):
    return x / (1.0 + jnp.exp(-x))


def _softplus_parts(z):
    t = jnp.log1p(jnp.exp(-jnp.abs(z)))
    return jnp.maximum(z, 0.0) + t, jnp.minimum(z, 0.0) - t


def _split3(x):
    a = x.astype(BF16)
    r = x - a.astype(F32)
    b = r.astype(BF16)
    c = (r - b.astype(F32)).astype(BF16)
    return a, b, c


def _inproj_kernel(x_ref, g_ref, w_ref, o_ref, hn_ref):
    @pl.when(pl.program_id(2) == 0)
    def _():
        x = x_ref[...]
        ms = jnp.mean(x * x, axis=-1, keepdims=True)
        hn_ref[...] = (x * lax.rsqrt(ms + NORM_EPS) * g_ref[...]).astype(BF16)

    o_ref[...] = jnp.dot(hn_ref[...], w_ref[...], preferred_element_type=F32).astype(o_ref.dtype)


def _inproj(h, g, w):
    bsz, p, d = h.shape
    n = w.shape[1]
    tp = _pick_tile(p, (1664, 1280, 640, 128))
    tn = _pick_tile(n, (896, 512, 128))
    return pl.pallas_call(
        _inproj_kernel,
        grid=(bsz, p // tp, n // tn),
        in_specs=[
            pl.BlockSpec((None, tp, d), lambda b, i, j: (b, i, 0)),
            pl.BlockSpec((1, d), lambda b, i, j: (0, 0)),
            pl.BlockSpec((d, tn), lambda b, i, j: (0, j)),
        ],
        out_specs=pl.BlockSpec((None, tp, tn), lambda b, i, j: (b, i, j)),
        out_shape=jax.ShapeDtypeStruct((bsz, p, n), BF16),
        scratch_shapes=[pltpu.VMEM((tp, d), BF16)],
        compiler_params=_params(("parallel", "parallel", "arbitrary")),
        name="inproj",
    )(h, g.reshape(1, d), w)


def _outproj_kernel(m1_ref, m2_ref, w1_ref, w2_ref, h_ref, o_ref, *, tp):
    y = jnp.dot(m1_ref[...], w1_ref[...], preferred_element_type=F32)
    y = y + jnp.dot(m2_ref[...], w2_ref[...], preferred_element_type=F32)
    pos = pl.program_id(1) * tp + lax.broadcasted_iota(I32, y.shape, 0)
    o_ref[...] = h_ref[...] + jnp.where(pos >= PAD_FRONT, y, 0.0)


def _outproj(m1, m2, w_out, h):
    bsz, p, d = h.shape
    k1 = m1.shape[-1]
    k2 = m2.shape[-1]
    tp = _pick_tile(p, (832, 640, 128))
    w = w_out.astype(BF16)
    return pl.pallas_call(
        functools.partial(_outproj_kernel, tp=tp),
        grid=(bsz, p // tp),
        in_specs=[
            pl.BlockSpec((None, tp, k1), lambda b, i: (b, i, 0)),
            pl.BlockSpec((None, tp, k2), lambda b, i: (b, i, 0)),
            pl.BlockSpec((k1, d), lambda b, i: (0, 0)),
            pl.BlockSpec((k2, d), lambda b, i: (0, 0)),
            pl.BlockSpec((None, tp, d), lambda b, i: (b, i, 0)),
        ],
        out_specs=pl.BlockSpec((None, tp, d), lambda b, i: (b, i, 0)),
        out_shape=jax.ShapeDtypeStruct((bsz, p, d), F32),
        compiler_params=_params(("parallel", "parallel")),
        name="outproj",
    )(m1, m2, w[:k1], w[k1:], h)


def _final_norm_kernel(x_ref, g_ref, o_ref):
    x = x_ref[...]
    ms = jnp.mean(x * x, axis=-1, keepdims=True)
    o_ref[...] = x * lax.rsqrt(ms + NORM_EPS) * g_ref[...]


def _final_norm(h, g, seq):
    bsz, p, d = h.shape
    skip = (p - seq) // LANES
    return pl.pallas_call(
        _final_norm_kernel,
        grid=(bsz, seq // LANES),
        in_specs=[
            pl.BlockSpec((None, LANES, d), lambda b, i: (b, i + skip, 0)),
            pl.BlockSpec((1, d), lambda b, i: (0, 0)),
        ],
        out_specs=pl.BlockSpec((None, LANES, d), lambda b, i: (b, i, 0)),
        out_shape=jax.ShapeDtypeStruct((bsz, seq, d), F32),
        compiler_params=_params(("parallel", "parallel")),
        name="final_norm",
    )(h, g.reshape(1, d))


def _dsa_kernel(aq_ref, ag_ref, iq_ref, iw_ref, k_ref, v_ref, ik_ref, bias_ref, o_ref,
                key_ref, iqs_ref, wts_ref, qm_ref, thr_ref, jdx_ref, mx_ref, ls_ref, acc_ref,
                *, topk):
    T = LANES
    i = pl.program_id(1)
    nkt = i + 1
    row = lax.broadcasted_iota(I32, (T, T), 0)
    col = lax.broadcasted_iota(I32, (T, T), 1)
    qchunk = (i * T + row) >> 6
    low = col < 64

    def admissible(kt):
        kpos = kt * T + col
        return (kpos >= PAD_FRONT) & ((kpos >> 6) <= qchunk)

    iw = iw_ref[...].astype(F32)
    for h in range(IDX_HEADS):
        pair = iq_ref[:, (h // 2) * T:(h // 2 + 1) * T]
        mine = low if h % 2 == 0 else ~low
        iqs_ref[h] = jnp.where(mine, pair, jnp.zeros_like(pair))
        wts_ref[h] = jnp.broadcast_to(iw[:, h:h + 1], (T, T))
    for h in range(A_HEADS):
        pair = aq_ref[:, (h // 2) * T:(h // 2 + 1) * T]
        mine = low if h % 2 == 0 else ~low
        qm_ref[h] = jnp.where(mine, pair * (A_DIM ** -0.5), jnp.zeros_like(pair))

    def score_tile(kt, carry):
        ikt = ik_ref[pl.ds(pl.multiple_of(kt * T, T), T), :]
        tot = jnp.zeros((T, T), F32)
        for h in range(IDX_HEADS):
            s = lax.dot_general(iqs_ref[h], ikt, NT_DIMS, preferred_element_type=F32)
            tot = tot + jnp.maximum(s, 0.0) * wts_ref[h]
        score = jnp.where(admissible(kt), tot * IDX_SCALE, NEG_INF)
        bits = lax.bitcast_convert_type(score, I32)
        key_ref[kt] = bits ^ ((bits >> 31) & 0x7FFFFFFF)
        return carry

    lax.fori_loop(0, nkt, score_tile, 0)

    def row_count(pred):
        def body(kt, acc):
            return acc + jnp.where(pred(kt), 1.0, 0.0)
        acc = lax.fori_loop(0, nkt, body, jnp.zeros((T, T), F32))
        return jnp.sum(acc, axis=-1, keepdims=True)

    def value_bit(it, t):
        cand = t + (jnp.int32(1) << (31 - it))
        cnt = row_count(lambda kt: key_ref[kt] >= cand)
        return jnp.where(cnt >= topk, cand, t)

    thr = lax.fori_loop(0, 32, value_bit, jnp.full((T, 1), INT_MIN, I32))
    thr_ref[...] = jnp.broadcast_to(thr, (T, T))
    jdx_ref[...] = jnp.full((T, T), 2 ** IDX_BITS, I32)

    cnt_gt = row_count(lambda kt: key_ref[kt] > thr)
    cnt_eq = row_count(lambda kt: key_ref[kt] == thr)
    need = topk - cnt_gt
    tied = jnp.max(jnp.where(cnt_eq > need, 1.0, 0.0)) > 0.0

    @pl.when(tied)
    def _():
        def index_bit(it, j):
            cand = j + (jnp.int32(1) << (IDX_BITS - 1 - it))
            cnt = row_count(lambda kt: (key_ref[kt] == thr) & (kt * T + col < cand))
            return jnp.where(cnt < need, cand, j)
        j = lax.fori_loop(0, IDX_BITS, index_bit, jnp.zeros((T, 1), I32))
        jdx_ref[...] = jnp.broadcast_to(j, (T, T))

    def to_mask(kt, carry):
        key = key_ref[kt]
        thr_t = thr_ref[...]
        sel = (key > thr_t) | ((key == thr_t) & (kt * T + col <= jdx_ref[...]))
        madd = jnp.where(sel & admissible(kt), 0.0, NEG_INF)
        key_ref[kt] = lax.bitcast_convert_type(madd, I32)
        return carry

    lax.fori_loop(0, nkt, to_mask, 0)

    mx_ref[...] = jnp.full(mx_ref.shape, NEG_INF, F32)
    ls_ref[...] = jnp.zeros(ls_ref.shape, F32)
    acc_ref[...] = jnp.zeros(acc_ref.shape, F32)

    def attend(kt0, ntile, near, final):
        ks = pl.ds(pl.multiple_of(kt0 * T, T), ntile * T)
        madd = [lax.bitcast_convert_type(key_ref[kt0 + u], F32) for u in range(ntile)]
        for h in range(A_HEADS):
            ps = slice((h // 2) * T, (h // 2 + 1) * T)
            s = lax.dot_general(qm_ref[h], k_ref[ks, ps], NT_DIMS, preferred_element_type=F32)
            parts = [s[:, u * T:(u + 1) * T] + madd[u] for u in range(ntile)]
            if near:
                parts[0] = parts[0] + bias_ref[h, jnp.minimum(i - kt0, 2)]
            if not final:
                mx_ref[h] = functools.reduce(jnp.maximum, parts, mx_ref[h])
            else:
                es = [jnp.exp(part - mx_ref[h]) for part in parts]
                ls_ref[h] = functools.reduce(jnp.add, es, ls_ref[h])
                p = es[0] if ntile == 1 else jnp.concatenate(es, axis=1)
                acc_ref[h] = acc_ref[h] + jnp.dot(p.astype(BF16), v_ref[ks, ps],
                                                  preferred_element_type=F32)

    nwin = jnp.maximum(i - 1, 0) // 2
    for final in (False, True):
        def far(w, carry, final=final):
            attend(2 * w, 2, False, final)
            return carry

        def near(kt, carry, final=final):
            attend(kt, 1, True, final)
            return carry

        lax.fori_loop(0, nwin, far, 0)
        lax.fori_loop(2 * nwin, nkt, near, 0)
        if not final:
            for h in range(A_HEADS):
                mx_ref[h] = jnp.broadcast_to(jnp.max(mx_ref[h], axis=-1, keepdims=True), (T, T))

    for hp in range(A_HEADS // 2):
        o0 = acc_ref[2 * hp] / jnp.sum(ls_ref[2 * hp], axis=-1, keepdims=True)
        o1 = acc_ref[2 * hp + 1] / jnp.sum(ls_ref[2 * hp + 1], axis=-1, keepdims=True)
        g = ag_ref[:, hp * T:(hp + 1) * T].astype(F32)
        o_ref[:, hp * T:(hp + 1) * T] = (jnp.where(low, o0, o1) * _silu(g)).astype(o_ref.dtype)


def _dsa(pab, bias_tiles, topk):
    bsz, p, _ = pab.shape
    T = LANES
    nq = p // T
    hw = A_HEADS * A_DIM
    one = pl.Buffered(1)
    return pl.pallas_call(
        functools.partial(_dsa_kernel, topk=float(topk)),
        grid=(bsz, nq),
        in_specs=[
            pl.BlockSpec((None, T, hw), lambda b, i: (b, i, 0)),
            pl.BlockSpec((None, T, hw), lambda b, i: (b, i, 3)),
            pl.BlockSpec((None, T, hw), lambda b, i: (b, i, 4)),
            pl.BlockSpec((None, T, T), lambda b, i: (b, i, 33)),
            pl.BlockSpec((None, p, hw), lambda b, i: (b, 0, 1), pipeline_mode=one),
            pl.BlockSpec((None, p, hw), lambda b, i: (b, 0, 2), pipeline_mode=one),
            pl.BlockSpec((None, p, T), lambda b, i: (b, 0, 32), pipeline_mode=one),
            pl.BlockSpec((A_HEADS, 3, T, T), lambda b, i: (0, 0, 0, 0), pipeline_mode=one),
        ],
        out_specs=pl.BlockSpec((None, T, hw), lambda b, i: (b, i, 0)),
        out_shape=jax.ShapeDtypeStruct((bsz, p, hw), BF16),
        scratch_shapes=[
            pltpu.VMEM((nq, T, T), I32),
            pltpu.VMEM((IDX_HEADS, T, T), BF16),
            pltpu.VMEM((IDX_HEADS, T, T), F32),
            pltpu.VMEM((A_HEADS, T, T), BF16),
            pltpu.VMEM((T, T), I32),
            pltpu.VMEM((T, T), I32),
            pltpu.VMEM((A_HEADS, T, T), F32),
            pltpu.VMEM((A_HEADS, T, T), F32),
            pltpu.VMEM((A_HEADS, T, T), F32),
        ],
        compiler_params=_params(("parallel", "arbitrary")),
        name="dsa",
    )(pab, pab, pab, pab, pab, pab, pab, bias_tiles)


def _gla_kernel(q_ref, k_ref, v_ref, g_ref, a_ref, w2_ref, gb_ref, o_ref, st_ref, *, nchunk):
    C = CHUNK
    T = LANES

    @pl.when(pl.program_id(1) == 0)
    def _():
        st_ref[...] = jnp.zeros(st_ref.shape, F32)

    r_i = lax.broadcasted_iota(I32, (C, C), 0)
    c_i = lax.broadcasted_iota(I32, (C, C), 1)
    causal = c_i <= r_i
    tri = jnp.where(causal, 1.0, 0.0).astype(BF16)
    lane = lax.broadcasted_iota(I32, (C, T), 1)
    sd_r = lax.broadcasted_iota(I32, (2 * B_DV, T), 0)
    sd_c = lax.broadcasted_iota(I32, (2 * B_DV, T), 1)
    blockdiag = (sd_r >= B_DV) == (sd_c >= B_DK)

    def chunk(c, carry):
        r = pl.ds(pl.multiple_of(c * C, C), C)
        x = jnp.dot(a_ref[r, :], w2_ref[...], preferred_element_type=F32) + gb_ref[...]
        log_a = _softplus_parts(x)[1] * (1.0 / GLA_TAU)
        a1, a2, a3 = _split3(log_a)
        bcum = (jnp.dot(tri, a1, preferred_element_type=F32)
                + jnp.dot(tri, a2, preferred_element_type=F32)
                + jnp.dot(tri, a3, preferred_element_type=F32))
        b_last = bcum[C - 1:C, :]
        q = q_ref[r, :].astype(F32) * (B_DK ** -0.5)
        k = k_ref[r, :].astype(F32)
        q_t = (q * jnp.exp(bcum)).astype(BF16)
        k_t = (k * jnp.exp(-bcum)).astype(BF16)
        k_d = (k * jnp.exp(b_last - bcum)).astype(BF16)
        decay = jnp.exp(b_last)
        for hp in range(B_HEADS // 2):
            ls = slice(hp * T, (hp + 1) * T)
            st = st_ref[hp]
            o_inter = lax.dot_general(q_t[:, ls], st.astype(BF16), NT_DIMS,
                                      preferred_element_type=F32)
            for hh in range(2):
                h = 2 * hp + hh
                mine = (lane < B_DK) if hh == 0 else (lane >= B_DK)
                qm = jnp.where(mine, q_t[:, ls], jnp.zeros((C, T), BF16))
                att = lax.dot_general(qm, k_t[:, ls], NT_DIMS, preferred_element_type=F32)
                att = jnp.where(causal, att, 0.0)
                vs = slice(h * B_DV, (h + 1) * B_DV)
                o = jnp.dot(att.astype(BF16), v_ref[r, vs], preferred_element_type=F32)
                o = o + o_inter[:, hh * B_DV:(hh + 1) * B_DV]
                o = o * lax.rsqrt(jnp.mean(o * o, axis=-1, keepdims=True) + NORM_EPS)
                o = o * _silu(g_ref[r, vs].astype(F32))
                o_ref[r, vs] = o.astype(o_ref.dtype)
            vp = v_ref[r, hp * 2 * B_DV:(hp + 1) * 2 * B_DV]
            contrib = lax.dot_general(vp, k_d[:, ls], TN_DIMS, preferred_element_type=F32)
            st_ref[hp] = st * decay[:, ls] + jnp.where(blockdiag, contrib, 0.0)
        return carry

    lax.fori_loop(0, nchunk, chunk, 0)


def _gla(pab, w2, gb):
    bsz, p, _ = pab.shape
    tc = _pick_tile(p, (640, 128))
    hv = B_HEADS * B_DV
    hk = B_HEADS * B_DK
    return pl.pallas_call(
        functools.partial(_gla_kernel, nchunk=tc // CHUNK),
        grid=(bsz, p // tc),
        in_specs=[
            pl.BlockSpec((None, tc, hk), lambda b, c: (b, c, 14)),
            pl.BlockSpec((None, tc, hk), lambda b, c: (b, c, 15)),
            pl.BlockSpec((None, tc, hv), lambda b, c: (b, c, 5)),
            pl.BlockSpec((None, tc, hv), lambda b, c: (b, c, 6)),
            pl.BlockSpec((None, tc, LANES), lambda b, c: (b, c, 34)),
            pl.BlockSpec((LANES, hk), lambda b, c: (0, 0)),
            pl.BlockSpec((1, hk), lambda b, c: (0, 0)),
        ],
        out_specs=pl.BlockSpec((None, tc, hv), lambda b, c: (b, c, 0)),
        out_shape=jax.ShapeDtypeStruct((bsz, p, hv), BF16),
        scratch_shapes=[pltpu.VMEM((B_HEADS // 2, 2 * B_DV, 2 * B_DK), F32)],
        compiler_params=_params(("parallel", "arbitrary")),
        name="gla",
    )(pab, pab, pab, pab, pab, w2, gb)


def _ret_kernel(q_ref, k_ref, v_ref, g_ref, cos_ref, sin_ref, dmat_ref, zeta_ref, xi_ref,
                cdec_ref, o_ref, st_ref, *, nchunk):
    C = CHUNK
    T = LANES
    W = C_HEADS * C_DK

    @pl.when(pl.program_id(1) == 0)
    def _():
        st_ref[...] = jnp.zeros(st_ref.shape, F32)

    lane_w = lax.broadcasted_iota(I32, (C, W), 1)
    first_half = (lane_w & (C_DK - 1)) < (C_DK // 2)
    lane = lax.broadcasted_iota(I32, (C, T), 1)
    sd_r = lax.broadcasted_iota(I32, (2 * C_DV, T), 0)
    sd_c = lax.broadcasted_iota(I32, (2 * C_DV, T), 1)
    blockdiag = (sd_r >= C_DV) == (sd_c >= C_DK)

    def rotate(x, cos, sin_signed):
        swapped = jnp.where(first_half, pltpu.roll(x, W - C_DK // 2, 1), pltpu.roll(x, C_DK // 2, 1))
        return x * cos + swapped * sin_signed

    def chunk(c, carry):
        r = pl.ds(pl.multiple_of(c * C, C), C)
        cos = cos_ref[r, :]
        sin = sin_ref[r, :]
        q = rotate(q_ref[r, :].astype(F32), cos, sin)
        k = rotate(k_ref[r, :].astype(F32), cos, sin) * (C_DK ** -0.5)
        q_b = q.astype(BF16)
        k_b = k.astype(BF16)
        q_x = (q * xi_ref[...]).astype(BF16)
        k_z = (k * zeta_ref[...]).astype(BF16)
        for hp in range(C_HEADS // 2):
            ls = slice(hp * T, (hp + 1) * T)
            st = st_ref[hp]
            o_inter = lax.dot_general(q_x[:, ls], st.astype(BF16), NT_DIMS,
                                      preferred_element_type=F32)
            for hh in range(2):
                h = 2 * hp + hh
                mine = (lane < C_DK) if hh == 0 else (lane >= C_DK)
                qm = jnp.where(mine, q_b[:, ls], jnp.zeros((C, T), BF16))
                att = lax.dot_general(qm, k_b[:, ls], NT_DIMS, preferred_element_type=F32)
                att = att * dmat_ref[h]
                vs = slice(h * C_DV, (h + 1) * C_DV)
                o = jnp.dot(att.astype(BF16), v_ref[r, vs], preferred_element_type=F32)
                o = o + o_inter[:, hh * C_DV:(hh + 1) * C_DV]
                o = o - jnp.mean(o, axis=-1, keepdims=True)
                o = o * lax.rsqrt(jnp.mean(o * o, axis=-1, keepdims=True) + NORM_EPS)
                o = o * _silu(g_ref[r, vs].astype(F32))
                o_ref[r, vs] = o.astype(o_ref.dtype)
            vp = v_ref[r, hp * 2 * C_DV:(hp + 1) * 2 * C_DV]
            contrib = lax.dot_general(vp, k_z[:, ls], TN_DIMS, preferred_element_type=F32)
            st_ref[hp] = st * cdec_ref[:, ls] + jnp.where(blockdiag, contrib, 0.0)
        return carry

    lax.fori_loop(0, nchunk, chunk, 0)


def _retention_tables(p):
    log_gamma = np.log(1.0 - np.exp2(-5.0 - np.arange(C_HEADS, dtype=np.float64)))
    i = np.arange(CHUNK, dtype=np.float64)
    diff = i[:, None] - i[None, :]
    dmat = np.where(diff >= 0, np.exp(log_gamma[:, None, None] * np.maximum(diff, 0.0)), 0.0)
    zeta = np.exp(log_gamma[:, None] * (CHUNK - 1 - i))
    xi = np.exp(log_gamma[:, None] * (i + 1))
    cdec = np.exp(log_gamma * CHUNK)
    widen = lambda t: np.repeat(t.T[:, :, None], C_DK, axis=2).reshape(CHUNK, C_HEADS * C_DK)
    half = C_DK // 2
    inv = jnp.asarray(ROPE_BASE, F32) ** (-jnp.arange(half, dtype=F32) / half)
    ang = jnp.arange(p, dtype=jnp.int32).astype(F32)[:, None] * inv[None, :]
    cos = jnp.tile(jnp.cos(ang), (1, 2 * C_HEADS))
    sin = jnp.sin(ang)
    sin_signed = jnp.tile(jnp.concatenate([-sin, sin], axis=1), (1, C_HEADS))
    return (cos, sin_signed, jnp.asarray(dmat, F32), jnp.asarray(widen(zeta), F32),
            jnp.asarray(widen(xi), F32),
            jnp.asarray(np.repeat(cdec, C_DK)[None, :], F32))


def _retention(pcd, tables):
    bsz, p, _ = pcd.shape
    cos, sin, dmat, zeta, xi, cdec = tables
    tc = _pick_tile(p, (640, 128))
    hv = C_HEADS * C_DV
    hk = C_HEADS * C_DK
    full = lambda shape: pl.BlockSpec(shape, lambda b, c: (0,) * len(shape))
    return pl.pallas_call(
        functools.partial(_ret_kernel, nchunk=tc // CHUNK),
        grid=(bsz, p // tc),
        in_specs=[
            pl.BlockSpec((None, tc, hk), lambda b, c: (b, c, 0)),
            pl.BlockSpec((None, tc, hk), lambda b, c: (b, c, 1)),
            pl.BlockSpec((None, tc, hv), lambda b, c: (b, c, 1)),
            pl.BlockSpec((None, tc, hv), lambda b, c: (b, c, 2)),
            pl.BlockSpec((tc, hk), lambda b, c: (c, 0)),
            pl.BlockSpec((tc, hk), lambda b, c: (c, 0)),
            full((C_HEADS, CHUNK, CHUNK)),
            full((CHUNK, hk)),
            full((CHUNK, hk)),
            full((1, hk)),
        ],
        out_specs=pl.BlockSpec((None, tc, hv), lambda b, c: (b, c, 0)),
        out_shape=jax.ShapeDtypeStruct((bsz, p, hv), BF16),
        scratch_shapes=[pltpu.VMEM((C_HEADS // 2, 2 * C_DV, 2 * C_DK), F32)],
        compiler_params=_params(("parallel", "arbitrary")),
        name="retention",
    )(pcd, pcd, pcd, pcd, cos, sin, dmat, zeta, xi, cdec)


def _sb_kernel(q_ref, g_ref, k_ref, v_ref, o_ref, qm_ref, run_ref, acc_ref):
    T = LANES
    i = pl.program_id(1)
    row = lax.broadcasted_iota(I32, (T, T), 0)
    col = lax.broadcasted_iota(I32, (T, T), 1)
    qpos = i * T + row
    low = col < D_DIM
    u_r = lax.broadcasted_iota(I32, (T, 2 * T), 0)
    u_c = lax.broadcasted_iota(I32, (T, 2 * T), 1)
    suffix = jnp.where((u_c >= T) | (u_r > u_c), 1.0, 0.0).astype(BF16)

    for h in range(D_HEADS):
        pair = q_ref[:, (h // 2) * T:(h // 2 + 1) * T]
        mine = low if h % 2 == 0 else ~low
        qm_ref[h] = jnp.where(mine, pair * (D_DIM ** -0.5), jnp.zeros_like(pair))
    run_ref[...] = jnp.zeros(run_ref.shape, F32)
    acc_ref[...] = jnp.zeros(acc_ref.shape, F32)

    def tile(state):
        t, _ = state
        kt = i - t
        ks = pl.ds(pl.multiple_of(kt * T, T), T)
        kpos = kt * T + col
        ok = (kpos < qpos) & (kpos >= PAD_FRONT)
        slowest = None
        for h in range(D_HEADS):
            ps = slice((h // 2) * T, (h // 2 + 1) * T)
            z = lax.dot_general(qm_ref[h], k_ref[ks, ps], NT_DIMS, preferred_element_type=F32)
            sp, logsig = _softplus_parts(z)
            log_1m = jnp.where(ok, -sp, 0.0)
            hi = log_1m.astype(BF16)
            lo = (log_1m - hi.astype(F32)).astype(BF16)
            sums = (jnp.dot(hi, suffix, preferred_element_type=F32)
                    + jnp.dot(lo, suffix, preferred_element_type=F32))
            run = run_ref[h]
            w = jnp.where(ok, jnp.exp(logsig + run + sums[:, :T]), 0.0)
            acc_ref[h] = acc_ref[h] + jnp.dot(w.astype(BF16), v_ref[ks, ps],
                                              preferred_element_type=F32)
            run = run + sums[:, T:]
            run_ref[h] = run
            slowest = run if slowest is None else jnp.maximum(slowest, run)
        return t + 1, (jnp.max(slowest) > LOG_F32_UNDERFLOW).astype(I32)

    lax.while_loop(lambda s: (s[0] <= i) & (s[1] > 0), tile, (jnp.int32(0), jnp.int32(1)))

    for hp in range(D_HEADS // 2):
        ps = slice(hp * T, (hp + 1) * T)
        o = jnp.where(low, acc_ref[2 * hp], acc_ref[2 * hp + 1]) * _silu(g_ref[:, ps].astype(F32))
        o_ref[:, ps] = o.astype(o_ref.dtype)


def _stick_breaking(pcd):
    bsz, p, _ = pcd.shape
    T = LANES
    hw = D_HEADS * D_DIM
    one = pl.Buffered(1)
    return pl.pallas_call(
        _sb_kernel,
        grid=(bsz, p // T),
        in_specs=[
            pl.BlockSpec((None, T, hw), lambda b, i: (b, i, 3)),
            pl.BlockSpec((None, T, hw), lambda b, i: (b, i, 6)),
            pl.BlockSpec((None, p, hw), lambda b, i: (b, 0, 4), pipeline_mode=one),
            pl.BlockSpec((None, p, hw), lambda b, i: (b, 0, 5), pipeline_mode=one),
        ],
        out_specs=pl.BlockSpec((None, T, hw), lambda b, i: (b, i, 0)),
        out_shape=jax.ShapeDtypeStruct((bsz, p, hw), BF16),
        scratch_shapes=[
            pltpu.VMEM((D_HEADS, T, T), BF16),
            pltpu.VMEM((D_HEADS, T, T), F32),
            pltpu.VMEM((D_HEADS, T, T), F32),
        ],
        compiler_params=_params(("parallel", "arbitrary")),
        name="stick_breaking",
    )(pcd, pcd, pcd, pcd)


def _rel_bucket_np(rel):
    half = REL_BUCKETS // 2
    max_exact = half // 2
    n = -rel
    ret = np.where(n < 0, half, 0)
    n = np.abs(n)
    edges = [math.ceil(max_exact * (REL_MAX_DIST / max_exact) ** (j / (half - max_exact)) - 1e-9)
             for j in range(1, half - max_exact)]
    large = max_exact + sum((n >= e).astype(np.int64) for e in edges)
    return ret + np.where(n < max_exact, n, large)


def _bias_tiles(rel_bias):
    i = np.arange(LANES)[:, None]
    j = np.arange(LANES)[None, :]
    idx = np.stack([_rel_bucket_np(j - i - LANES * d) for d in range(3)])
    assert (idx[2] == idx[2, 0, 0]).all()
    table = rel_bias.astype(F32)
    tiles = table[idx] - table[idx[2, 0, 0]]
    return jnp.transpose(tiles, (3, 0, 1, 2))


def _layout_ab(w):
    aq, ak, av, ag, iq, ik, iw, bq, bk, bv, bg, ba = jnp.split(w, np.cumsum(SPLIT_AB)[:-1].tolist(), axis=1)
    pad = lambda t: jnp.pad(t, ((0, 0), (0, LANES - t.shape[1])))
    cols = [aq, ak, av, ag, iq, bv, bg, bq, bk, jnp.concatenate([ik, ik], axis=1), pad(iw), pad(ba)]
    return jnp.concatenate(cols, axis=1).astype(BF16)


def kernel(x, meta_tokens, rel_bias, norm_g, final_g, w_in_ab, gla_gate_w2, gla_gate_b, w_out_ab,
           w_in_cd, w_out_cd):
    bsz, seq, d = x.shape
    p = seq + PAD_FRONT + N_META
    depth = norm_g.shape[0]
    topk = min(TOPK_MAX, seq // 4)
    h = jnp.concatenate([jnp.zeros((bsz, PAD_FRONT, d), x.dtype),
                         jnp.broadcast_to(meta_tokens.astype(x.dtype)[None], (bsz, N_META, d)),
                         x], axis=1)
    bias_tiles = _bias_tiles(rel_bias)
    tables = _retention_tables(p)
    for layer in range(depth):
        j = layer // 2
        if layer % 2 == 0:
            pab = _inproj(h, norm_g[layer], _layout_ab(w_in_ab[j]))
            oa = _dsa(pab, bias_tiles, topk)
            w2 = jnp.pad(gla_gate_w2[j], ((0, LANES - GLA_GATE_RANK), (0, 0))).astype(BF16)
            ob = _gla(pab, w2, gla_gate_b[j].reshape(1, -1).astype(F32))
            h = _outproj(oa, ob, w_out_ab[j], h)
        else:
            pcd = _inproj(h, norm_g[layer], w_in_cd[j].astype(BF16))
            oc = _retention(pcd, tables)
            od = _stick_breaking(pcd)
            h = _outproj(oc, od, w_out_cd[j], h)
    return _final_norm(h, final_g, seq)
```

```python
import functools
import math

import numpy as np
import jax
import jax.numpy as jnp
from jax import lax
from jax.experimental import pallas as pl
from jax.experimental.pallas import tpu as pltpu

F32 = jnp.float32
BF16 = jnp.bfloat16
I32 = jnp.int32
I16 = jnp.int16

D_MODEL = 1024
CHUNK = 64
N_META = 16
PAD_FRONT = 128 - N_META
NORM_EPS = 1e-6
NEG_INF = -1e30
A_HEADS, A_DIM = 8, 64
IDX_HEADS, IDX_DIM = 8, 64
IDX_SCALE = (IDX_DIM ** -0.5) * (IDX_HEADS ** -0.5)
TOPK_MAX = 256
B_HEADS, B_DK, B_DV = 4, 64, 128
GLA_GATE_RANK = 16
GLA_TAU = 16.0
C_HEADS, C_DK, C_DV = 4, 64, 128
ROPE_BASE = 10000.0
D_HEADS, D_DIM = 8, 64
REL_BUCKETS = 32
REL_MAX_DIST = 128
SPLIT_AB = (512, 512, 512, 512, 512, 64, 8, 256, 256, 512, 512, 16)
SPLIT_CD = (256, 256, 512, 512, 512, 512, 512, 512)

LANES = 128
W_AB_PAD = 4480
W_CD = 3584
I16_MIN = -2 ** 15
IDX_BITS = 14
LOG_F32_UNDERFLOW = -104.0
VMEM_LIMIT = 56 * 1024 * 1024

NT_DIMS = (((1,), (1,)), ((), ()))
TN_DIMS = (((0,), (0,)), ((), ()))


def _pick_tile(n, candidates):
    for c in candidates:
        if n % c == 0:
            return c
    raise ValueError(f"no tile for {n}")


def _params(sem):
    return pltpu.CompilerParams(dimension_semantics=sem, vmem_limit_bytes=VMEM_LIMIT)


def _silu(x):
    return x / (1.0 + jnp.exp(-x))


def _softplus_parts(z):
    t = jnp.log1p(jnp.exp(-jnp.abs(z)))
    return jnp.maximum(z, 0.0) + t, jnp.minimum(z, 0.0) - t


def _split3(x):
    a = x.astype(BF16)
    r = x - a.astype(F32)
    b = r.astype(BF16)
    c = (r - b.astype(F32)).astype(BF16)
    return a, b, c


def _inproj_kernel(x_ref, g_ref, w_ref, o_ref, hn_ref):
    @pl.when(pl.program_id(2) == 0)
    def _():
        x = x_ref[...]
        ms = jnp.mean(x * x, axis=-1, keepdims=True)
        hn_ref[...] = (x * lax.rsqrt(ms + NORM_EPS) * g_ref[...]).astype(BF16)

    o_ref[...] = jnp.dot(hn_ref[...], w_ref[...], preferred_element_type=F32).astype(o_ref.dtype)


def _inproj(h, g, w):
    bsz, p, d = h.shape
    n = w.shape[1]
    tp = _pick_tile(p, (1664, 1280, 640, 128))
    tn = _pick_tile(n, (896, 512, 128))
    return pl.pallas_call(
        _inproj_kernel,
        grid=(bsz, p // tp, n // tn),
        in_specs=[
            pl.BlockSpec((None, tp, d), lambda b, i, j: (b, i, 0)),
            pl.BlockSpec((1, d), lambda b, i, j: (0, 0)),
            pl.BlockSpec((d, tn), lambda b, i, j: (0, j)),
        ],
        out_specs=pl.BlockSpec((None, tp, tn), lambda b, i, j: (b, i, j)),
        out_shape=jax.ShapeDtypeStruct((bsz, p, n), BF16),
        scratch_shapes=[pltpu.VMEM((tp, d), BF16)],
        compiler_params=_params(("parallel", "parallel", "arbitrary")),
        name="inproj",
    )(h, g.reshape(1, d), w)


def _outproj_kernel(m1_ref, m2_ref, w1_ref, w2_ref, h_ref, o_ref, *, tp):
    y = jnp.dot(m1_ref[...], w1_ref[...], preferred_element_type=F32)
    y = y + jnp.dot(m2_ref[...], w2_ref[...], preferred_element_type=F32)
    pos = pl.program_id(1) * tp + lax.broadcasted_iota(I32, y.shape, 0)
    o_ref[...] = h_ref[...] + jnp.where(pos >= PAD_FRONT, y, 0.0)


def _outproj(m1, m2, w_out, h):
    bsz, p, d = h.shape
    k1 = m1.shape[-1]
    k2 = m2.shape[-1]
    tp = _pick_tile(p, (832, 640, 128))
    w = w_out.astype(BF16)
    return pl.pallas_call(
        functools.partial(_outproj_kernel, tp=tp),
        grid=(bsz, p // tp),
        in_specs=[
            pl.BlockSpec((None, tp, k1), lambda b, i: (b, i, 0)),
            pl.BlockSpec((None, tp, k2), lambda b, i: (b, i, 0)),
            pl.BlockSpec((k1, d), lambda b, i: (0, 0)),
            pl.BlockSpec((k2, d), lambda b, i: (0, 0)),
            pl.BlockSpec((None, tp, d), lambda b, i: (b, i, 0)),
        ],
        out_specs=pl.BlockSpec((None, tp, d), lambda b, i: (b, i, 0)),
        out_shape=jax.ShapeDtypeStruct((bsz, p, d), F32),
        compiler_params=_params(("parallel", "parallel")),
        name="outproj",
    )(m1, m2, w[:k1], w[k1:], h)


def _final_norm_kernel(x_ref, g_ref, o_ref):
    x = x_ref[...]
    ms = jnp.mean(x * x, axis=-1, keepdims=True)
    o_ref[...] = x * lax.rsqrt(ms + NORM_EPS) * g_ref[...]


def _final_norm(h, g, seq):
    bsz, p, d = h.shape
    skip = (p - seq) // LANES
    return pl.pallas_call(
        _final_norm_kernel,
        grid=(bsz, seq // LANES),
        in_specs=[
            pl.BlockSpec((None, LANES, d), lambda b, i: (b, i + skip, 0)),
            pl.BlockSpec((1, d), lambda b, i: (0, 0)),
        ],
        out_specs=pl.BlockSpec((None, LANES, d), lambda b, i: (b, i, 0)),
        out_shape=jax.ShapeDtypeStruct((bsz, seq, d), F32),
        compiler_params=_params(("parallel", "parallel")),
        name="final_norm",
    )(h, g.reshape(1, d))


def _dsa_kernel(aq_ref, ag_ref, iq_ref, iw_ref, k_ref, v_ref, ik_ref, bias_ref, o_ref,
                hi_ref, lo_ref, mask_ref, iqs_ref, wts_ref, qm_ref, jdx_ref, mx_ref, ls_ref,
                acc_ref, *, topk):
    T = LANES
    i = pl.program_id(1)
    base = i & 1
    npair = i // 2 + 1
    row = lax.broadcasted_iota(I32, (T, T), 0)
    col = lax.broadcasted_iota(I32, (T, T), 1)
    qchunk = (i * T + row) >> 6
    low = col < 64

    def admissible(kt):
        kpos = kt * T + col
        return (kpos >= PAD_FRONT) & ((kpos >> 6) <= qchunk)

    def tile_rows(slot):
        return pl.ds(pl.multiple_of(jnp.maximum(slot - 1, 0) * T, T), T)

    iw = iw_ref[...].astype(F32)
    for h in range(IDX_HEADS):
        pair = iq_ref[:, (h // 2) * T:(h // 2 + 1) * T]
        mine = low if h % 2 == 0 else ~low
        iqs_ref[h] = jnp.where(mine, pair, jnp.zeros_like(pair))
        wts_ref[h] = jnp.broadcast_to(iw[:, h:h + 1], (T, T))
    for h in range(A_HEADS):
        pair = aq_ref[:, (h // 2) * T:(h // 2 + 1) * T]
        mine = low if h % 2 == 0 else ~low
        qm_ref[h] = jnp.where(mine, pair * (A_DIM ** -0.5), jnp.zeros_like(pair))

    def score_pair(p, carry):
        s0 = base + 2 * p
        ikw = jnp.concatenate([ik_ref[tile_rows(s0), :], ik_ref[tile_rows(s0 + 1), :]], axis=0)
        tot = [jnp.zeros((T, T), F32), jnp.zeros((T, T), F32)]
        for h in range(IDX_HEADS):
            s = lax.dot_general(iqs_ref[h], ikw, NT_DIMS, preferred_element_type=F32)
            for u in range(2):
                tot[u] = tot[u] + jnp.maximum(s[:, u * T:(u + 1) * T], 0.0) * wts_ref[h]
        for u in range(2):
            score = jnp.where(admissible(s0 + u - 1), tot[u] * IDX_SCALE, NEG_INF)
            bits = lax.bitcast_convert_type(score, I32)
            key = bits ^ ((bits >> 31) & 0x7FFFFFFF)
            hi_ref[s0 + u] = (key >> 16).astype(I16)
            lo_ref[s0 + u] = ((key & 0xFFFF) + I16_MIN).astype(I16)
        return carry

    lax.fori_loop(0, npair, score_pair, 0)
    hi_ref[0] = jnp.full((T, T), I16_MIN, I16)
    lo_ref[0] = jnp.full((T, T), I16_MIN, I16)

    def row_count(pred, dtype=I16):
        one = jnp.ones((T, T), dtype)
        zero = jnp.zeros((T, T), dtype)

        def body(p, acc):
            s0 = base + 2 * p
            return acc + jnp.where(pred(s0), one, zero) + jnp.where(pred(s0 + 1), one, zero)
        acc = lax.fori_loop(0, npair, body, zero)
        return jnp.sum(acc.astype(F32), axis=-1, keepdims=True)

    def wide16(x):
        return jnp.broadcast_to(x, (T, T)).astype(I16)

    def largest_with(ref, want):
        def bit(it, t):
            cand = t + (jnp.int32(1) << (15 - it))
            c16 = wide16(cand)
            cnt = row_count(lambda s: ref[s] >= c16)
            return jnp.where(cnt >= want, cand, t)
        return lax.fori_loop(0, 16, bit, jnp.full((T, 1), I16_MIN, I32))

    t_hi = largest_with(hi_ref, topk)
    t_hi16 = wide16(t_hi)
    above = row_count(lambda s: hi_ref[s] > t_hi16)

    def to_bucket(p, carry):
        for u in range(2):
            s = base + 2 * p + u
            lo_ref[s] = jnp.where(hi_ref[s] == t_hi16, lo_ref[s], jnp.full((T, T), I16_MIN, I16))
        return carry

    lax.fori_loop(0, npair, to_bucket, 0)
    t_lo = largest_with(lo_ref, topk - above)
    t_lo16 = wide16(t_lo)

    def tie(s):
        return (hi_ref[s].astype(I32) == t_hi) & (lo_ref[s].astype(I32) == t_lo)

    need = topk - above - row_count(lambda s: lo_ref[s] > t_lo16)
    ties = row_count(tie, F32)
    jdx_ref[...] = jnp.full((T, T), 2 ** IDX_BITS, I32)

    @pl.when(jnp.max(jnp.where(ties > need, 1.0, 0.0)) > 0.0)
    def _():
        def index_bit(it, j):
            cand = j + (jnp.int32(1) << (IDX_BITS - 1 - it))
            cnt = row_count(lambda s: tie(s) & ((s - 1) * T + col < cand), F32)
            return jnp.where(cnt < need, cand, j)
        j = lax.fori_loop(0, IDX_BITS, index_bit, jnp.zeros((T, 1), I32))
        jdx_ref[...] = jnp.broadcast_to(j, (T, T))

    jdx = jdx_ref[...]

    def to_mask(p, carry):
        for u in range(2):
            s = base + 2 * p + u
            hi = hi_ref[s].astype(I32)
            lo = lo_ref[s].astype(I32)
            sel = (hi > t_hi) | ((hi == t_hi) & ((lo > t_lo) | ((lo == t_lo) & ((s - 1) * T + col <= jdx))))
            mask_ref[s] = jnp.where(sel & admissible(s - 1), 0.0, NEG_INF)
        return carry

    lax.fori_loop(0, npair, to_mask, 0)
    mask_ref[0] = jnp.full((T, T), NEG_INF, F32)

    mx_ref[...] = jnp.full(mx_ref.shape, NEG_INF, F32)
    ls_ref[...] = jnp.zeros(ls_ref.shape, F32)
    acc_ref[...] = jnp.zeros(acc_ref.shape, F32)

    def attend(s0, near, final):
        rows = [tile_rows(s0), tile_rows(s0 + 1)]
        madd = [mask_ref[s0], mask_ref[s0 + 1]]
        for h in range(A_HEADS):
            ps = slice((h // 2) * T, (h // 2 + 1) * T)
            kw = jnp.concatenate([k_ref[rows[0], ps], k_ref[rows[1], ps]], axis=0)
            s = lax.dot_general(qm_ref[h], kw, NT_DIMS, preferred_element_type=F32)
            parts = [s[:, u * T:(u + 1) * T] + madd[u] for u in range(2)]
            if near:
                parts = [parts[u] + bias_ref[h, 1 - u] for u in range(2)]
            if not final:
                mx_ref[h] = jnp.maximum(mx_ref[h], jnp.maximum(parts[0], parts[1]))
            else:
                es = [jnp.exp(part - mx_ref[h]) for part in parts]
                ls_ref[h] = ls_ref[h] + (es[0] + es[1])
                vw = jnp.concatenate([v_ref[rows[0], ps], v_ref[rows[1], ps]], axis=0)
                acc_ref[h] = acc_ref[h] + jnp.dot(jnp.concatenate(es, axis=1).astype(BF16), vw,
                                                  preferred_element_type=F32)

    for final in (False, True):
        def far(p, carry, final=final):
            attend(base + 2 * p, False, final)
            return carry

        lax.fori_loop(0, npair - 1, far, 0)
        attend(i, True, final)
        if not final:
            for h in range(A_HEADS):
                mx_ref[h] = jnp.broadcast_to(jnp.max(mx_ref[h], axis=-1, keepdims=True), (T, T))

    for hp in range(A_HEADS // 2):
        o0 = acc_ref[2 * hp] / jnp.sum(ls_ref[2 * hp], axis=-1, keepdims=True)
        o1 = acc_ref[2 * hp + 1] / jnp.sum(ls_ref[2 * hp + 1], axis=-1, keepdims=True)
        g = ag_ref[:, hp * T:(hp + 1) * T].astype(F32)
        o_ref[:, hp * T:(hp + 1) * T] = (jnp.where(low, o0, o1) * _silu(g)).astype(o_ref.dtype)


def _dsa(pab, bias_tiles, topk):
    bsz, p, _ = pab.shape
    T = LANES
    nq = p // T
    hw = A_HEADS * A_DIM
    one = pl.Buffered(1)
    return pl.pallas_call(
        functools.partial(_dsa_kernel, topk=float(topk)),
        grid=(bsz, nq),
        in_specs=[
            pl.BlockSpec((None, T, hw), lambda b, i: (b, i, 0)),
            pl.BlockSpec((None, T, hw), lambda b, i: (b, i, 3)),
            pl.BlockSpec((None, T, hw), lambda b, i: (b, i, 4)),
            pl.BlockSpec((None, T, T), lambda b, i: (b, i, 33)),
            pl.BlockSpec((None, p, hw), lambda b, i: (b, 0, 1), pipeline_mode=one),
            pl.BlockSpec((None, p, hw), lambda b, i: (b, 0, 2), pipeline_mode=one),
            pl.BlockSpec((None, p, T), lambda b, i: (b, 0, 32), pipeline_mode=one),
            pl.BlockSpec((A_HEADS, 2, T, T), lambda b, i: (0, 0, 0, 0), pipeline_mode=one),
        ],
        out_specs=pl.BlockSpec((None, T, hw), lambda b, i: (b, i, 0)),
        out_shape=jax.ShapeDtypeStruct((bsz, p, hw), BF16),
        scratch_shapes=[
            pltpu.VMEM((nq + 1, T, T), I16),
            pltpu.VMEM((nq + 1, T, T), I16),
            pltpu.VMEM((nq + 1, T, T), F32),
            pltpu.VMEM((IDX_HEADS, T, T), BF16),
            pltpu.VMEM((IDX_HEADS, T, T), F32),
            pltpu.VMEM((A_HEADS, T, T), BF16),
            pltpu.VMEM((T, T), I32),
            pltpu.VMEM((A_HEADS, T, T), F32),
            pltpu.VMEM((A_HEADS, T, T), F32),
            pltpu.VMEM((A_HEADS, T, T), F32),
        ],
        compiler_params=_params(("parallel", "arbitrary")),
        name="dsa",
    )(pab, pab, pab, pab, pab, pab, pab, bias_tiles[:, :2])


def _gla_kernel(q_ref, k_ref, v_ref, g_ref, a_ref, w2_ref, gb_ref, o_ref, st_ref, *, nchunk):
    C = CHUNK
    T = LANES

    @pl.when(pl.program_id(1) == 0)
    def _():
        st_ref[...] = jnp.zeros(st_ref.shape, F32)

    r_i = lax.broadcasted_iota(I32, (C, C), 0)
    c_i = lax.broadcasted_iota(I32, (C, C), 1)
    causal = c_i <= r_i
    tri = jnp.where(causal, 1.0, 0.0).astype(BF16)
    lane = lax.broadcasted_iota(I32, (C, T), 1)
    sd_r = lax.broadcasted_iota(I32, (2 * B_DV, T), 0)
    sd_c = lax.broadcasted_iota(I32, (2 * B_DV, T), 1)
    blockdiag = (sd_r >= B_DV) == (sd_c >= B_DK)

    def chunk(c, carry):
        r = pl.ds(pl.multiple_of(c * C, C), C)
        x = jnp.dot(a_ref[r, :], w2_ref[...], preferred_element_type=F32) + gb_ref[...]
        log_a = _softplus_parts(x)[1] * (1.0 / GLA_TAU)
        a1, a2, a3 = _split3(log_a)
        bcum = (jnp.dot(tri, a1, preferred_element_type=F32)
                + jnp.dot(tri, a2, preferred_element_type=F32)
                + jnp.dot(tri, a3, preferred_element_type=F32))
        b_last = bcum[C - 1:C, :]
        q = q_ref[r, :].astype(F32) * (B_DK ** -0.5)
        k = k_ref[r, :].astype(F32)
        q_t = (q * jnp.exp(bcum)).astype(BF16)
        k_t = (k * jnp.exp(-bcum)).astype(BF16)
        k_d = (k * jnp.exp(b_last - bcum)).astype(BF16)
        decay = jnp.exp(b_last)
        for hp in range(B_HEADS // 2):
            ls = slice(hp * T, (hp + 1) * T)
            st = st_ref[hp]
            o_inter = lax.dot_general(q_t[:, ls], st.astype(BF16), NT_DIMS,
                                      preferred_element_type=F32)
            for hh in range(2):
                h = 2 * hp + hh
                mine = (lane < B_DK) if hh == 0 else (lane >= B_DK)
                qm = jnp.where(mine, q_t[:, ls], jnp.zeros((C, T), BF16))
                att = lax.dot_general(qm, k_t[:, ls], NT_DIMS, preferred_element_type=F32)
                att = jnp.where(causal, att, 0.0)
                vs = slice(h * B_DV, (h + 1) * B_DV)
                o = jnp.dot(att.astype(BF16), v_ref[r, vs], preferred_element_type=F32)
                o = o + o_inter[:, hh * B_DV:(hh + 1) * B_DV]
                o = o * lax.rsqrt(jnp.mean(o * o, axis=-1, keepdims=True) + NORM_EPS)
                o = o * _silu(g_ref[r, vs].astype(F32))
                o_ref[r, vs] = o.astype(o_ref.dtype)
            vp = v_ref[r, hp * 2 * B_DV:(hp + 1) * 2 * B_DV]
            contrib = lax.dot_general(vp, k_d[:, ls], TN_DIMS, preferred_element_type=F32)
            st_ref[hp] = st * decay[:, ls] + jnp.where(blockdiag, contrib, 0.0)
        return carry

    lax.fori_loop(0, nchunk, chunk, 0)


def _gla(pab, w2, gb):
    bsz, p, _ = pab.shape
    tc = _pick_tile(p, (640, 128))
    hv = B_HEADS * B_DV
    hk = B_HEADS * B_DK
    return pl.pallas_call(
        functools.partial(_gla_kernel, nchunk=tc // CHUNK),
        grid=(bsz, p // tc),
        in_specs=[
            pl.BlockSpec((None, tc, hk), lambda b, c: (b, c, 14)),
            pl.BlockSpec((None, tc, hk), lambda b, c: (b, c, 15)),
            pl.BlockSpec((None, tc, hv), lambda b, c: (b, c, 5)),
            pl.BlockSpec((None, tc, hv), lambda b, c: (b, c, 6)),
            pl.BlockSpec((None, tc, LANES), lambda b, c: (b, c, 34)),
            pl.BlockSpec((LANES, hk), lambda b, c: (0, 0)),
            pl.BlockSpec((1, hk), lambda b, c: (0, 0)),
        ],
        out_specs=pl.BlockSpec((None, tc, hv), lambda b, c: (b, c, 0)),
        out_shape=jax.ShapeDtypeStruct((bsz, p, hv), BF16),
        scratch_shapes=[pltpu.VMEM((B_HEADS // 2, 2 * B_DV, 2 * B_DK), F32)],
        compiler_params=_params(("parallel", "arbitrary")),
        name="gla",
    )(pab, pab, pab, pab, pab, w2, gb)


def _ret_kernel(q_ref, k_ref, v_ref, g_ref, cos_ref, sin_ref, dmat_ref, zeta_ref, xi_ref,
                cdec_ref, o_ref, st_ref, *, nchunk):
    C = CHUNK
    T = LANES
    W = C_HEADS * C_DK

    @pl.when(pl.program_id(1) == 0)
    def _():
        st_ref[...] = jnp.zeros(st_ref.shape, F32)

    lane_w = lax.broadcasted_iota(I32, (C, W), 1)
    first_half = (lane_w & (C_DK - 1)) < (C_DK // 2)
    lane = lax.broadcasted_iota(I32, (C, T), 1)
    sd_r = lax.broadcasted_iota(I32, (2 * C_DV, T), 0)
    sd_c = lax.broadcasted_iota(I32, (2 * C_DV, T), 1)
    blockdiag = (sd_r >= C_DV) == (sd_c >= C_DK)

    def rotate(x, cos, sin_signed):
        swapped = jnp.where(first_half, pltpu.roll(x, W - C_DK // 2, 1), pltpu.roll(x, C_DK // 2, 1))
        return x * cos + swapped * sin_signed

    def chunk(c, carry):
        r = pl.ds(pl.multiple_of(c * C, C), C)
        cos = cos_ref[r, :]
        sin = sin_ref[r, :]
        q = rotate(q_ref[r, :].astype(F32), cos, sin)
        k = rotate(k_ref[r, :].astype(F32), cos, sin) * (C_DK ** -0.5)
        q_b = q.astype(BF16)
        k_b = k.astype(BF16)
        q_x = (q * xi_ref[...]).astype(BF16)
        k_z = (k * zeta_ref[...]).astype(BF16)
        for hp in range(C_HEADS // 2):
            ls = slice(hp * T, (hp + 1) * T)
            st = st_ref[hp]
            o_inter = lax.dot_general(q_x[:, ls], st.astype(BF16), NT_DIMS,
                                      preferred_element_type=F32)
            for hh in range(2):
                h = 2 * hp + hh
                mine = (lane < C_DK) if hh == 0 else (lane >= C_DK)
                qm = jnp.where(mine, q_b[:, ls], jnp.zeros((C, T), BF16))
                att = lax.dot_general(qm, k_b[:, ls], NT_DIMS, preferred_element_type=F32)
                att = att * dmat_ref[h]
                vs = slice(h * C_DV, (h + 1) * C_DV)
                o = jnp.dot(att.astype(BF16), v_ref[r, vs], preferred_element_type=F32)
                o = o + o_inter[:, hh * C_DV:(hh + 1) * C_DV]
                o = o - jnp.mean(o, axis=-1, keepdims=True)
                o = o * lax.rsqrt(jnp.mean(o * o, axis=-1, keepdims=True) + NORM_EPS)
                o = o * _silu(g_ref[r, vs].astype(F32))
                o_ref[r, vs] = o.astype(o_ref.dtype)
            vp = v_ref[r, hp * 2 * C_DV:(hp + 1) * 2 * C_DV]
            contrib = lax.dot_general(vp, k_z[:, ls], TN_DIMS, preferred_element_type=F32)
            st_ref[hp] = st * cdec_ref[:, ls] + jnp.where(blockdiag, contrib, 0.0)
        return carry

    lax.fori_loop(0, nchunk, chunk, 0)


def _retention_tables(p):
    log_gamma = np.log(1.0 - np.exp2(-5.0 - np.arange(C_HEADS, dtype=np.float64)))
    i = np.arange(CHUNK, dtype=np.float64)
    diff = i[:, None] - i[None, :]
    dmat = np.where(diff >= 0, np.exp(log_gamma[:, None, None] * np.maximum(diff, 0.0)), 0.0)
    zeta = np.exp(log_gamma[:, None] * (CHUNK - 1 - i))
    xi = np.exp(log_gamma[:, None] * (i + 1))
    cdec = np.exp(log_gamma * CHUNK)
    widen = lambda t: np.repeat(t.T[:, :, None], C_DK, axis=2).reshape(CHUNK, C_HEADS * C_DK)
    half = C_DK // 2
    inv = jnp.asarray(ROPE_BASE, F32) ** (-jnp.arange(half, dtype=F32) / half)
    ang = jnp.arange(p, dtype=jnp.int32).astype(F32)[:, None] * inv[None, :]
    cos = jnp.tile(jnp.cos(ang), (1, 2 * C_HEADS))
    sin = jnp.sin(ang)
    sin_signed = jnp.tile(jnp.concatenate([-sin, sin], axis=1), (1, C_HEADS))
    return (cos, sin_signed, jnp.asarray(dmat, F32), jnp.asarray(widen(zeta), F32),
            jnp.asarray(widen(xi), F32),
            jnp.asarray(np.repeat(cdec, C_DK)[None, :], F32))


def _retention(pcd, tables):
    bsz, p, _ = pcd.shape
    cos, sin, dmat, zeta, xi, cdec = tables
    tc = _pick_tile(p, (640, 128))
    hv = C_HEADS * C_DV
    hk = C_HEADS * C_DK
    full = lambda shape: pl.BlockSpec(shape, lambda b, c: (0,) * len(shape))
    return pl.pallas_call(
        functools.partial(_ret_kernel, nchunk=tc // CHUNK),
        grid=(bsz, p // tc),
        in_specs=[
            pl.BlockSpec((None, tc, hk), lambda b, c: (b, c, 0)),
            pl.BlockSpec((None, tc, hk), lambda b, c: (b, c, 1)),
            pl.BlockSpec((None, tc, hv), lambda b, c: (b, c, 1)),
            pl.BlockSpec((None, tc, hv), lambda b, c: (b, c, 2)),
            pl.BlockSpec((tc, hk), lambda b, c: (c, 0)),
            pl.BlockSpec((tc, hk), lambda b, c: (c, 0)),
            full((C_HEADS, CHUNK, CHUNK)),
            full((CHUNK, hk)),
            full((CHUNK, hk)),
            full((1, hk)),
        ],
        out_specs=pl.BlockSpec((None, tc, hv), lambda b, c: (b, c, 0)),
        out_shape=jax.ShapeDtypeStruct((bsz, p, hv), BF16),
        scratch_shapes=[pltpu.VMEM((C_HEADS // 2, 2 * C_DV, 2 * C_DK), F32)],
        compiler_params=_params(("parallel", "arbitrary")),
        name="retention",
    )(pcd, pcd, pcd, pcd, cos, sin, dmat, zeta, xi, cdec)


def _sb_kernel(q_ref, g_ref, k_ref, v_ref, o_ref, qm_ref, run_ref, acc_ref, hi_ref, lo_ref, ls_ref,
               w_ref):
    T = LANES
    i = pl.program_id(1)
    row = lax.broadcasted_iota(I32, (T, T), 0)
    col = lax.broadcasted_iota(I32, (T, T), 1)
    qpos = i * T + row
    low = col < D_DIM
    u_r = lax.broadcasted_iota(I32, (T, 2 * T), 0)
    u_c = lax.broadcasted_iota(I32, (T, 2 * T), 1)
    suffix = jnp.where((u_c >= T) | (u_r > u_c), 1.0, 0.0).astype(BF16)

    for h in range(D_HEADS):
        pair = q_ref[:, (h // 2) * T:(h // 2 + 1) * T]
        mine = low if h % 2 == 0 else ~low
        qm_ref[h] = jnp.where(mine, pair * (D_DIM ** -0.5), jnp.zeros_like(pair))
    run_ref[...] = jnp.zeros(run_ref.shape, F32)
    acc_ref[...] = jnp.zeros(acc_ref.shape, F32)

    def tile(state):
        t, _ = state
        kt = i - t
        ks = pl.ds(pl.multiple_of(kt * T, T), T)
        kpos = kt * T + col
        ok = (kpos < qpos) & (kpos >= PAD_FRONT)
        for h in range(D_HEADS):
            ps = slice((h // 2) * T, (h // 2 + 1) * T)
            z = lax.dot_general(qm_ref[h], k_ref[ks, ps], NT_DIMS, preferred_element_type=F32)
            sp, logsig = _softplus_parts(z)
            log_1m = jnp.where(ok, -sp, 0.0)
            hi = log_1m.astype(BF16)
            hi_ref[h] = hi
            lo_ref[h] = (log_1m - hi.astype(F32)).astype(BF16)
            ls_ref[h] = jnp.where(ok, logsig, NEG_INF)
        slowest = None
        for h in range(D_HEADS):
            sums = (jnp.dot(hi_ref[h], suffix, preferred_element_type=F32)
                    + jnp.dot(lo_ref[h], suffix, preferred_element_type=F32))
            run = run_ref[h]
            w_ref[h] = jnp.exp(ls_ref[h] + run + sums[:, :T]).astype(BF16)
            run = run + sums[:, T:]
            run_ref[h] = run
            slowest = run if slowest is None else jnp.maximum(slowest, run)
        for h in range(D_HEADS):
            ps = slice((h // 2) * T, (h // 2 + 1) * T)
            acc_ref[h] = acc_ref[h] + jnp.dot(w_ref[h], v_ref[ks, ps],
                                              preferred_element_type=F32)
        return t + 1, (jnp.max(slowest) > LOG_F32_UNDERFLOW).astype(I32)

    lax.while_loop(lambda s: (s[0] <= i) & (s[1] > 0), tile, (jnp.int32(0), jnp.int32(1)))

    for hp in range(D_HEADS // 2):
        ps = slice(hp * T, (hp + 1) * T)
        o = jnp.where(low, acc_ref[2 * hp], acc_ref[2 * hp + 1]) * _silu(g_ref[:, ps].astype(F32))
        o_ref[:, ps] = o.astype(o_ref.dtype)


def _stick_breaking(pcd):
    bsz, p, _ = pcd.shape
    T = LANES
    hw = D_HEADS * D_DIM
    one = pl.Buffered(1)
    return pl.pallas_call(
        _sb_kernel,
        grid=(bsz, p // T),
        in_specs=[
            pl.BlockSpec((None, T, hw), lambda b, i: (b, i, 3)),
            pl.BlockSpec((None, T, hw), lambda b, i: (b, i, 6)),
            pl.BlockSpec((None, p, hw), lambda b, i: (b, 0, 4), pipeline_mode=one),
            pl.BlockSpec((None, p, hw), lambda b, i: (b, 0, 5), pipeline_mode=one),
        ],
        out_specs=pl.BlockSpec((None, T, hw), lambda b, i: (b, i, 0)),
        out_shape=jax.ShapeDtypeStruct((bsz, p, hw), BF16),
        scratch_shapes=[
            pltpu.VMEM((D_HEADS, T, T), BF16),
            pltpu.VMEM((D_HEADS, T, T), F32),
            pltpu.VMEM((D_HEADS, T, T), F32),
            pltpu.VMEM((D_HEADS, T, T), BF16),
            pltpu.VMEM((D_HEADS, T, T), BF16),
            pltpu.VMEM((D_HEADS, T, T), F32),
            pltpu.VMEM((D_HEADS, T, T), BF16),
        ],
        compiler_params=_params(("parallel", "arbitrary")),
        name="stick_breaking",
    )(pcd, pcd, pcd, pcd)


def _rel_bucket_np(rel):
    half = REL_BUCKETS // 2
    max_exact = half // 2
    n = -rel
    ret = np.where(n < 0, half, 0)
    n = np.abs(n)
    edges = [math.ceil(max_exact * (REL_MAX_DIST / max_exact) ** (j / (half - max_exact)) - 1e-9)
             for j in range(1, half - max_exact)]
    large = max_exact + sum((n >= e).astype(np.int64) for e in edges)
    return ret + np.where(n < max_exact, n, large)


def _bias_tiles(rel_bias):
    i = np.arange(LANES)[:, None]
    j = np.arange(LANES)[None, :]
    idx = np.stack([_rel_bucket_np(j - i - LANES * d) for d in range(3)])
    assert (idx[2] == idx[2, 0, 0]).all()
    table = rel_bias.astype(F32)
    tiles = table[idx] - table[idx[2, 0, 0]]
    return jnp.transpose(tiles, (3, 0, 1, 2))


def _layout_ab(w):
    aq, ak, av, ag, iq, ik, iw, bq, bk, bv, bg, ba = jnp.split(w, np.cumsum(SPLIT_AB)[:-1].tolist(), axis=1)
    pad = lambda t: jnp.pad(t, ((0, 0), (0, LANES - t.shape[1])))
    cols = [aq, ak, av, ag, iq, bv, bg, bq, bk, jnp.concatenate([ik, ik], axis=1), pad(iw), pad(ba)]
    return jnp.concatenate(cols, axis=1).astype(BF16)


def kernel(x, meta_tokens, rel_bias, norm_g, final_g, w_in_ab, gla_gate_w2, gla_gate_b, w_out_ab,
           w_in_cd, w_out_cd):
    bsz, seq, d = x.shape
    p = seq + PAD_FRONT + N_META
    depth = norm_g.shape[0]
    topk = min(TOPK_MAX, seq // 4)
    h = jnp.concatenate([jnp.zeros((bsz, PAD_FRONT, d), x.dtype),
                         jnp.broadcast_to(meta_tokens.astype(x.dtype)[None], (bsz, N_META, d)),
                         x], axis=1)
    bias_tiles = _bias_tiles(rel_bias)
    tables = _retention_tables(p)
    for layer in range(depth):
        j = layer // 2
        if layer % 2 == 0:
            pab = _inproj(h, norm_g[layer], _layout_ab(w_in_ab[j]))
            oa = _dsa(pab, bias_tiles, topk)
            w2 = jnp.pad(gla_gate_w2[j], ((0, LANES - GLA_GATE_RANK), (0, 0))).astype(BF16)
            ob = _gla(pab, w2, gla_gate_b[j].reshape(1, -1).astype(F32))
            h = _outproj(oa, ob, w_out_ab[j], h)
        else:
            pcd = _inproj(h, norm_g[layer], w_in_cd[j].astype(BF16))
            oc = _retention(pcd, tables)
            od = _stick_breaking(pcd)
            h = _outproj(oc, od, w_out_cd[j], h)
    return _final_norm(h, final_g, seq)
```

```python
import functools
import math

import numpy as np
import jax
import jax.numpy as jnp
from jax import lax
from jax.experimental import pallas as pl
from jax.experimental.pallas import tpu as pltpu

F32 = jnp.float32
BF16 = jnp.bfloat16
I32 = jnp.int32
I16 = jnp.int16

D_MODEL = 1024
CHUNK = 64
N_META = 16
PAD_FRONT = 128 - N_META
NORM_EPS = 1e-6
NEG_INF = -1e30
A_HEADS, A_DIM = 8, 64
IDX_HEADS, IDX_DIM = 8, 64
IDX_SCALE = (IDX_DIM ** -0.5) * (IDX_HEADS ** -0.5)
TOPK_MAX = 256
B_HEADS, B_DK, B_DV = 4, 64, 128
GLA_GATE_RANK = 16
GLA_TAU = 16.0
C_HEADS, C_DK, C_DV = 4, 64, 128
ROPE_BASE = 10000.0
D_HEADS, D_DIM = 8, 64
REL_BUCKETS = 32
REL_MAX_DIST = 128
SPLIT_AB = (512, 512, 512, 512, 512, 64, 8, 256, 256, 512, 512, 16)
SPLIT_CD = (256, 256, 512, 512, 512, 512, 512, 512)

LANES = 128
W_AB_PAD = 4480
W_CD = 3584
I16_MIN = -2 ** 15
I16_MAX = 2 ** 15 - 1
IDX_BITS = 14
LOG_F32_UNDERFLOW = -104.0
VMEM_LIMIT = 56 * 1024 * 1024

NT_DIMS = (((1,), (1,)), ((), ()))
TN_DIMS = (((0,), (0,)), ((), ()))


def _pick_tile(n, candidates):
    for c in candidates:
        if n % c == 0:
            return c
    raise ValueError(f"no tile for {n}")


def _params(sem):
    return pltpu.CompilerParams(dimension_semantics=sem, vmem_limit_bytes=VMEM_LIMIT)


def _silu(x):
    return x / (1.0 + jnp.exp(-x))


def _softplus_parts(z):
    t = jnp.log1p(jnp.exp(-jnp.abs(z)))
    return jnp.maximum(z, 0.0) + t, jnp.minimum(z, 0.0) - t


def _split3(x):
    a = x.astype(BF16)
    r = x - a.astype(F32)
    b = r.astype(BF16)
    c = (r - b.astype(F32)).astype(BF16)
    return a, b, c


def _inproj_kernel(x_ref, g_ref, w_ref, o_ref, hn_ref):
    @pl.when(pl.program_id(2) == 0)
    def _():
        x = x_ref[...]
        ms = jnp.mean(x * x, axis=-1, keepdims=True)
        hn_ref[...] = (x * lax.rsqrt(ms + NORM_EPS) * g_ref[...]).astype(BF16)

    o_ref[...] = jnp.dot(hn_ref[...], w_ref[...], preferred_element_type=F32).astype(o_ref.dtype)


def _inproj(h, g, w):
    bsz, p, d = h.shape
    n = w.shape[1]
    tp = _pick_tile(p, (1664, 1280, 640, 128))
    tn = _pick_tile(n, (896, 512, 128))
    return pl.pallas_call(
        _inproj_kernel,
        grid=(bsz, p // tp, n // tn),
        in_specs=[
            pl.BlockSpec((None, tp, d), lambda b, i, j: (b, i, 0)),
            pl.BlockSpec((1, d), lambda b, i, j: (0, 0)),
            pl.BlockSpec((d, tn), lambda b, i, j: (0, j)),
        ],
        out_specs=pl.BlockSpec((None, tp, tn), lambda b, i, j: (b, i, j)),
        out_shape=jax.ShapeDtypeStruct((bsz, p, n), BF16),
        scratch_shapes=[pltpu.VMEM((tp, d), BF16)],
        compiler_params=_params(("parallel", "parallel", "arbitrary")),
        name="inproj",
    )(h, g.reshape(1, d), w)


def _outproj_kernel(m1_ref, m2_ref, w1_ref, w2_ref, h_ref, o_ref, *, tp):
    y = jnp.dot(m1_ref[...], w1_ref[...], preferred_element_type=F32)
    y = y + jnp.dot(m2_ref[...], w2_ref[...], preferred_element_type=F32)
    pos = pl.program_id(1) * tp + lax.broadcasted_iota(I32, y.shape, 0)
    o_ref[...] = h_ref[...] + jnp.where(pos >= PAD_FRONT, y, 0.0)


def _outproj(m1, m2, w_out, h):
    bsz, p, d = h.shape
    k1 = m1.shape[-1]
    k2 = m2.shape[-1]
    tp = _pick_tile(p, (832, 640, 128))
    w = w_out.astype(BF16)
    return pl.pallas_call(
        functools.partial(_outproj_kernel, tp=tp),
        grid=(bsz, p // tp),
        in_specs=[
            pl.BlockSpec((None, tp, k1), lambda b, i: (b, i, 0)),
            pl.BlockSpec((None, tp, k2), lambda b, i: (b, i, 0)),
            pl.BlockSpec((k1, d), lambda b, i: (0, 0)),
            pl.BlockSpec((k2, d), lambda b, i: (0, 0)),
            pl.BlockSpec((None, tp, d), lambda b, i: (b, i, 0)),
        ],
        out_specs=pl.BlockSpec((None, tp, d), lambda b, i: (b, i, 0)),
        out_shape=jax.ShapeDtypeStruct((bsz, p, d), F32),
        compiler_params=_params(("parallel", "parallel")),
        name="outproj",
    )(m1, m2, w[:k1], w[k1:], h)


def _final_norm_kernel(x_ref, g_ref, o_ref):
    x = x_ref[...]
    ms = jnp.mean(x * x, axis=-1, keepdims=True)
    o_ref[...] = x * lax.rsqrt(ms + NORM_EPS) * g_ref[...]


def _final_norm(h, g, seq):
    bsz, p, d = h.shape
    skip = (p - seq) // LANES
    return pl.pallas_call(
        _final_norm_kernel,
        grid=(bsz, seq // LANES),
        in_specs=[
            pl.BlockSpec((None, LANES, d), lambda b, i: (b, i + skip, 0)),
            pl.BlockSpec((1, d), lambda b, i: (0, 0)),
        ],
        out_specs=pl.BlockSpec((None, LANES, d), lambda b, i: (b, i, 0)),
        out_shape=jax.ShapeDtypeStruct((bsz, seq, d), F32),
        compiler_params=_params(("parallel", "parallel")),
        name="final_norm",
    )(h, g.reshape(1, d))


def _dsa_kernel(aq_ref, ag_ref, iq_ref, iw_ref, k_ref, v_ref, ik_ref, bias_ref, o_ref,
                hi_ref, lo_ref, pos_ref, mask_ref, iqs_ref, wts_ref, qm_ref, jdx_ref,
                mx_ref, ls_ref, acc_ref, *, topk):
    T = LANES
    i = pl.program_id(1)
    base = i & 1
    npair = i // 2 + 1
    row = lax.broadcasted_iota(I32, (T, T), 0)
    col = lax.broadcasted_iota(I32, (T, T), 1)
    qchunk = (i * T + row) >> 6
    low = col < 64

    def admissible(kt):
        kpos = kt * T + col
        return (kpos >= PAD_FRONT) & ((kpos >> 6) <= qchunk)

    def tile_rows(slot):
        return pl.ds(pl.multiple_of(jnp.maximum(slot - 1, 0) * T, T), T)

    iw = iw_ref[...].astype(F32)
    for h in range(IDX_HEADS):
        pair = iq_ref[:, (h // 2) * T:(h // 2 + 1) * T]
        mine = low if h % 2 == 0 else ~low
        iqs_ref[h] = jnp.where(mine, pair, jnp.zeros_like(pair))
        wts_ref[h] = jnp.broadcast_to(iw[:, h:h + 1], (T, T))
    for h in range(A_HEADS):
        pair = aq_ref[:, (h // 2) * T:(h // 2 + 1) * T]
        mine = low if h % 2 == 0 else ~low
        qm_ref[h] = jnp.where(mine, pair * (A_DIM ** -0.5), jnp.zeros_like(pair))

    def sweep(issue, consume, consume_last):
        def body(p, carry):
            consume(p, issue(p))
            return carry

        lax.fori_loop(0, npair - 1, body, 0)
        consume_last(npair - 1, issue(npair - 1))

    def issue_products(lhs_ref, rhs_ref, paired):
        def issue(p):
            s0 = base + 2 * p
            rows = [tile_rows(s0), tile_rows(s0 + 1)]
            staged = []
            for h in range(lhs_ref.shape[0]):
                ps = slice((h // 2) * T, (h // 2 + 1) * T) if paired else slice(0, T)
                kw = jnp.concatenate([rhs_ref[rows[0], ps], rhs_ref[rows[1], ps]], axis=0)
                staged.append(lax.dot_general(lhs_ref[h], kw, NT_DIMS, preferred_element_type=F32))
            return staged
        return issue

    def score_pair(p, staged):
        s0 = base + 2 * p
        for u in range(2):
            tot = jnp.zeros((T, T), F32)
            for h in range(IDX_HEADS):
                tot = tot + jnp.maximum(staged[h][:, u * T:(u + 1) * T], 0.0) * wts_ref[h]
            score = jnp.where(admissible(s0 + u - 1), tot * IDX_SCALE, NEG_INF)
            bits = lax.bitcast_convert_type(score, I32)
            key = bits ^ ((bits >> 31) & 0x7FFFFFFF)
            hi_ref[s0 + u] = (key >> 16).astype(I16)
            lo_ref[s0 + u] = ((key & 0xFFFF) + I16_MIN).astype(I16)

    sweep(issue_products(iqs_ref, ik_ref, False), score_pair, score_pair)
    hi_ref[0] = jnp.full((T, T), I16_MIN, I16)
    lo_ref[0] = jnp.full((T, T), I16_MIN, I16)

    def row_count(pred, dtype=I16):
        one = jnp.ones((T, T), dtype)
        zero = jnp.zeros((T, T), dtype)

        def body(p, acc):
            s0 = base + 2 * p
            return acc + jnp.where(pred(s0), one, zero) + jnp.where(pred(s0 + 1), one, zero)
        acc = lax.fori_loop(0, npair, body, zero)
        return jnp.sum(acc.astype(F32), axis=-1, keepdims=True)

    def wide16(x):
        return jnp.broadcast_to(x, (T, T)).astype(I16)

    def largest_with(ref, want):
        def bit(it, t):
            cand = t + (jnp.int32(1) << (15 - it))
            c16 = wide16(cand)
            cnt = row_count(lambda s: ref[s] >= c16)
            return jnp.where(cnt >= want, cand, t)
        return lax.fori_loop(0, 16, bit, jnp.full((T, 1), I16_MIN, I32))

    t_hi = largest_with(hi_ref, topk)
    t_hi16 = wide16(t_hi)
    above = row_count(lambda s: hi_ref[s] > t_hi16)

    def to_bucket(p, carry):
        for u in range(2):
            s = base + 2 * p + u
            lo_ref[s] = jnp.where(hi_ref[s] == t_hi16, lo_ref[s], jnp.full((T, T), I16_MIN, I16))
        return carry

    lax.fori_loop(0, npair, to_bucket, 0)
    t_lo = largest_with(lo_ref, topk - above)
    t_lo16 = wide16(t_lo)

    def tie(s):
        return (hi_ref[s].astype(I32) == t_hi) & (lo_ref[s].astype(I32) == t_lo)

    need = topk - above - row_count(lambda s: lo_ref[s] > t_lo16)
    ties = row_count(tie, F32)
    jdx_ref[...] = jnp.full((T, T), 2 ** IDX_BITS, I32)

    @pl.when(jnp.max(jnp.where(ties > need, 1.0, 0.0)) > 0.0)
    def _():
        def tie_positions(p, carry):
            for u in range(2):
                s = base + 2 * p + u
                pos_ref[s] = jnp.where(tie(s), (s - 1) * T + col, I16_MAX).astype(I16)
            return carry

        lax.fori_loop(0, npair, tie_positions, 0)

        def index_bit(it, j):
            cand = j + (jnp.int32(1) << (IDX_BITS - 1 - it))
            c16 = wide16(cand)
            cnt = row_count(lambda s: pos_ref[s] < c16)
            return jnp.where(cnt < need, cand, j)
        j = lax.fori_loop(0, IDX_BITS, index_bit, jnp.zeros((T, 1), I32))
        jdx_ref[...] = jnp.broadcast_to(j, (T, T))

    jdx = jdx_ref[...]

    def to_mask(p, carry):
        for u in range(2):
            s = base + 2 * p + u
            hi = hi_ref[s].astype(I32)
            lo = lo_ref[s].astype(I32)
            sel = (hi > t_hi) | ((hi == t_hi) & ((lo > t_lo) | ((lo == t_lo) & ((s - 1) * T + col <= jdx))))
            mask_ref[s] = jnp.where(sel & admissible(s - 1), 0.0, NEG_INF)
        return carry

    lax.fori_loop(0, npair, to_mask, 0)
    mask_ref[0] = jnp.full((T, T), NEG_INF, F32)

    mx_ref[...] = jnp.full(mx_ref.shape, NEG_INF, F32)
    ls_ref[...] = jnp.zeros(ls_ref.shape, F32)
    acc_ref[...] = jnp.zeros(acc_ref.shape, F32)

    def attend(p, staged, near, final):
        s0 = base + 2 * p
        madd = [mask_ref[s0], mask_ref[s0 + 1]]
        for h in range(A_HEADS):
            parts = [staged[h][:, u * T:(u + 1) * T] + madd[u] for u in range(2)]
            if near:
                parts = [parts[u] + bias_ref[h, 1 - u] for u in range(2)]
            if not final:
                mx_ref[h] = jnp.maximum(mx_ref[h], jnp.maximum(parts[0], parts[1]))
            else:
                ps = slice((h // 2) * T, (h // 2 + 1) * T)
                es = [jnp.exp(part - mx_ref[h]) for part in parts]
                ls_ref[h] = ls_ref[h] + (es[0] + es[1])
                vw = jnp.concatenate([v_ref[tile_rows(s0), ps], v_ref[tile_rows(s0 + 1), ps]], axis=0)
                acc_ref[h] = acc_ref[h] + jnp.dot(jnp.concatenate(es, axis=1).astype(BF16), vw,
                                                  preferred_element_type=F32)

    for final in (False, True):
        sweep(issue_products(qm_ref, k_ref, True),
              functools.partial(attend, near=False, final=final),
              functools.partial(attend, near=True, final=final))
        if not final:
            for h in range(A_HEADS):
                mx_ref[h] = jnp.broadcast_to(jnp.max(mx_ref[h], axis=-1, keepdims=True), (T, T))

    for hp in range(A_HEADS // 2):
        o0 = acc_ref[2 * hp] / jnp.sum(ls_ref[2 * hp], axis=-1, keepdims=True)
        o1 = acc_ref[2 * hp + 1] / jnp.sum(ls_ref[2 * hp + 1], axis=-1, keepdims=True)
        g = ag_ref[:, hp * T:(hp + 1) * T].astype(F32)
        o_ref[:, hp * T:(hp + 1) * T] = (jnp.where(low, o0, o1) * _silu(g)).astype(o_ref.dtype)


def _dsa(pab, bias_tiles, topk):
    bsz, p, _ = pab.shape
    T = LANES
    nq = p // T
    hw = A_HEADS * A_DIM
    one = pl.Buffered(1)
    return pl.pallas_call(
        functools.partial(_dsa_kernel, topk=float(topk)),
        grid=(bsz, nq),
        in_specs=[
            pl.BlockSpec((None, T, hw), lambda b, i: (b, i, 0)),
            pl.BlockSpec((None, T, hw), lambda b, i: (b, i, 3)),
            pl.BlockSpec((None, T, hw), lambda b, i: (b, i, 4)),
            pl.BlockSpec((None, T, T), lambda b, i: (b, i, 33)),
            pl.BlockSpec((None, p, hw), lambda b, i: (b, 0, 1), pipeline_mode=one),
            pl.BlockSpec((None, p, hw), lambda b, i: (b, 0, 2), pipeline_mode=one),
            pl.BlockSpec((None, p, T), lambda b, i: (b, 0, 32), pipeline_mode=one),
            pl.BlockSpec((A_HEADS, 2, T, T), lambda b, i: (0, 0, 0, 0), pipeline_mode=one),
        ],
        out_specs=pl.BlockSpec((None, T, hw), lambda b, i: (b, i, 0)),
        out_shape=jax.ShapeDtypeStruct((bsz, p, hw), BF16),
        scratch_shapes=[
            pltpu.VMEM((nq + 1, T, T), I16),
            pltpu.VMEM((nq + 1, T, T), I16),
            pltpu.VMEM((nq + 1, T, T), I16),
            pltpu.VMEM((nq + 1, T, T), F32),
            pltpu.VMEM((IDX_HEADS, T, T), BF16),
            pltpu.VMEM((IDX_HEADS, T, T), F32),
            pltpu.VMEM((A_HEADS, T, T), BF16),
            pltpu.VMEM((T, T), I32),
            pltpu.VMEM((A_HEADS, T, T), F32),
            pltpu.VMEM((A_HEADS, T, T), F32),
            pltpu.VMEM((A_HEADS, T, T), F32),
        ],
        compiler_params=_params(("parallel", "arbitrary")),
        name="dsa",
    )(pab, pab, pab, pab, pab, pab, pab, bias_tiles[:, :2])


def _gla_kernel(q_ref, k_ref, v_ref, g_ref, a_ref, w2_ref, gb_ref, o_ref, st_ref, *, nchunk):
    C = CHUNK
    T = LANES

    @pl.when(pl.program_id(1) == 0)
    def _():
        st_ref[...] = jnp.zeros(st_ref.shape, F32)

    r_i = lax.broadcasted_iota(I32, (C, C), 0)
    c_i = lax.broadcasted_iota(I32, (C, C), 1)
    causal = c_i <= r_i
    tri = jnp.where(causal, 1.0, 0.0).astype(BF16)
    lane = lax.broadcasted_iota(I32, (C, T), 1)
    sd_r = lax.broadcasted_iota(I32, (2 * B_DV, T), 0)
    sd_c = lax.broadcasted_iota(I32, (2 * B_DV, T), 1)
    blockdiag = (sd_r >= B_DV) == (sd_c >= B_DK)

    def chunk(c, carry):
        r = pl.ds(pl.multiple_of(c * C, C), C)
        x = jnp.dot(a_ref[r, :], w2_ref[...], preferred_element_type=F32) + gb_ref[...]
        log_a = _softplus_parts(x)[1] * (1.0 / GLA_TAU)
        a1, a2, a3 = _split3(log_a)
        bcum = (jnp.dot(tri, a1, preferred_element_type=F32)
                + jnp.dot(tri, a2, preferred_element_type=F32)
                + jnp.dot(tri, a3, preferred_element_type=F32))
        b_last = bcum[C - 1:C, :]
        q = q_ref[r, :].astype(F32) * (B_DK ** -0.5)
        k = k_ref[r, :].astype(F32)
        q_t = (q * jnp.exp(bcum)).astype(BF16)
        k_t = (k * jnp.exp(-bcum)).astype(BF16)
        k_d = (k * jnp.exp(b_last - bcum)).astype(BF16)
        decay = jnp.exp(b_last)
        for hp in range(B_HEADS // 2):
            ls = slice(hp * T, (hp + 1) * T)
            st = st_ref[hp]
            o_inter = lax.dot_general(q_t[:, ls], st.astype(BF16), NT_DIMS,
                                      preferred_element_type=F32)
            for hh in range(2):
                h = 2 * hp + hh
                mine = (lane < B_DK) if hh == 0 else (lane >= B_DK)
                qm = jnp.where(mine, q_t[:, ls], jnp.zeros((C, T), BF16))
                att = lax.dot_general(qm, k_t[:, ls], NT_DIMS, preferred_element_type=F32)
                att = jnp.where(causal, att, 0.0)
                vs = slice(h * B_DV, (h + 1) * B_DV)
                o = jnp.dot(att.astype(BF16), v_ref[r, vs], preferred_element_type=F32)
                o = o + o_inter[:, hh * B_DV:(hh + 1) * B_DV]
                o = o * lax.rsqrt(jnp.mean(o * o, axis=-1, keepdims=True) + NORM_EPS)
                o = o * _silu(g_ref[r, vs].astype(F32))
                o_ref[r, vs] = o.astype(o_ref.dtype)
            vp = v_ref[r, hp * 2 * B_DV:(hp + 1) * 2 * B_DV]
            contrib = lax.dot_general(vp, k_d[:, ls], TN_DIMS, preferred_element_type=F32)
            st_ref[hp] = st * decay[:, ls] + jnp.where(blockdiag, contrib, 0.0)
        return carry

    lax.fori_loop(0, nchunk, chunk, 0)


def _gla(pab, w2, gb):
    bsz, p, _ = pab.shape
    tc = _pick_tile(p, (640, 128))
    hv = B_HEADS * B_DV
    hk = B_HEADS * B_DK
    return pl.pallas_call(
        functools.partial(_gla_kernel, nchunk=tc // CHUNK),
        grid=(bsz, p // tc),
        in_specs=[
            pl.BlockSpec((None, tc, hk), lambda b, c: (b, c, 14)),
            pl.BlockSpec((None, tc, hk), lambda b, c: (b, c, 15)),
            pl.BlockSpec((None, tc, hv), lambda b, c: (b, c, 5)),
            pl.BlockSpec((None, tc, hv), lambda b, c: (b, c, 6)),
            pl.BlockSpec((None, tc, LANES), lambda b, c: (b, c, 34)),
            pl.BlockSpec((LANES, hk), lambda b, c: (0, 0)),
            pl.BlockSpec((1, hk), lambda b, c: (0, 0)),
        ],
        out_specs=pl.BlockSpec((None, tc, hv), lambda b, c: (b, c, 0)),
        out_shape=jax.ShapeDtypeStruct((bsz, p, hv), BF16),
        scratch_shapes=[pltpu.VMEM((B_HEADS // 2, 2 * B_DV, 2 * B_DK), F32)],
        compiler_params=_params(("parallel", "arbitrary")),
        name="gla",
    )(pab, pab, pab, pab, pab, w2, gb)


def _ret_kernel(q_ref, k_ref, v_ref, g_ref, cos_ref, sin_ref, dmat_ref, zeta_ref, xi_ref,
                cdec_ref, o_ref, st_ref, *, nchunk):
    C = CHUNK
    T = LANES
    W = C_HEADS * C_DK

    @pl.when(pl.program_id(1) == 0)
    def _():
        st_ref[...] = jnp.zeros(st_ref.shape, F32)

    lane_w = lax.broadcasted_iota(I32, (C, W), 1)
    first_half = (lane_w & (C_DK - 1)) < (C_DK // 2)
    lane = lax.broadcasted_iota(I32, (C, T), 1)
    sd_r = lax.broadcasted_iota(I32, (2 * C_DV, T), 0)
    sd_c = lax.broadcasted_iota(I32, (2 * C_DV, T), 1)
    blockdiag = (sd_r >= C_DV) == (sd_c >= C_DK)

    def rotate(x, cos, sin_signed):
        swapped = jnp.where(first_half, pltpu.roll(x, W - C_DK // 2, 1), pltpu.roll(x, C_DK // 2, 1))
        return x * cos + swapped * sin_signed

    def chunk(c, carry):
        r = pl.ds(pl.multiple_of(c * C, C), C)
        cos = cos_ref[r, :]
        sin = sin_ref[r, :]
        q = rotate(q_ref[r, :].astype(F32), cos, sin)
        k = rotate(k_ref[r, :].astype(F32), cos, sin) * (C_DK ** -0.5)
        q_b = q.astype(BF16)
        k_b = k.astype(BF16)
        q_x = (q * xi_ref[...]).astype(BF16)
        k_z = (k * zeta_ref[...]).astype(BF16)
        for hp in range(C_HEADS // 2):
            ls = slice(hp * T, (hp + 1) * T)
            st = st_ref[hp]
            o_inter = lax.dot_general(q_x[:, ls], st.astype(BF16), NT_DIMS,
                                      preferred_element_type=F32)
            for hh in range(2):
                h = 2 * hp + hh
                mine = (lane < C_DK) if hh == 0 else (lane >= C_DK)
                qm = jnp.where(mine, q_b[:, ls], jnp.zeros((C, T), BF16))
                att = lax.dot_general(qm, k_b[:, ls], NT_DIMS, preferred_element_type=F32)
                att = att * dmat_ref[h]
                vs = slice(h * C_DV, (h + 1) * C_DV)
                o = jnp.dot(att.astype(BF16), v_ref[r, vs], preferred_element_type=F32)
                o = o + o_inter[:, hh * C_DV:(hh + 1) * C_DV]
                o = o - jnp.mean(o, axis=-1, keepdims=True)
                o = o * lax.rsqrt(jnp.mean(o * o, axis=-1, keepdims=True) + NORM_EPS)
                o = o * _silu(g_ref[r, vs].astype(F32))
                o_ref[r, vs] = o.astype(o_ref.dtype)
            vp = v_ref[r, hp * 2 * C_DV:(hp + 1) * 2 * C_DV]
            contrib = lax.dot_general(vp, k_z[:, ls], TN_DIMS, preferred_element_type=F32)
            st_ref[hp] = st * cdec_ref[:, ls] + jnp.where(blockdiag, contrib, 0.0)
        return carry

    lax.fori_loop(0, nchunk, chunk, 0)


def _retention_tables(p):
    log_gamma = np.log(1.0 - np.exp2(-5.0 - np.arange(C_HEADS, dtype=np.float64)))
    i = np.arange(CHUNK, dtype=np.float64)
    diff = i[:, None] - i[None, :]
    dmat = np.where(diff >= 0, np.exp(log_gamma[:, None, None] * np.maximum(diff, 0.0)), 0.0)
    zeta = np.exp(log_gamma[:, None] * (CHUNK - 1 - i))
    xi = np.exp(log_gamma[:, None] * (i + 1))
    cdec = np.exp(log_gamma * CHUNK)
    widen = lambda t: np.repeat(t.T[:, :, None], C_DK, axis=2).reshape(CHUNK, C_HEADS * C_DK)
    half = C_DK // 2
    inv = jnp.asarray(ROPE_BASE, F32) ** (-jnp.arange(half, dtype=F32) / half)
    ang = jnp.arange(p, dtype=jnp.int32).astype(F32)[:, None] * inv[None, :]
    cos = jnp.tile(jnp.cos(ang), (1, 2 * C_HEADS))
    sin = jnp.sin(ang)
    sin_signed = jnp.tile(jnp.concatenate([-sin, sin], axis=1), (1, C_HEADS))
    return (cos, sin_signed, jnp.asarray(dmat, F32), jnp.asarray(widen(zeta), F32),
            jnp.asarray(widen(xi), F32),
            jnp.asarray(np.repeat(cdec, C_DK)[None, :], F32))


def _retention(pcd, tables):
    bsz, p, _ = pcd.shape
    cos, sin, dmat, zeta, xi, cdec = tables
    tc = _pick_tile(p, (640, 128))
    hv = C_HEADS * C_DV
    hk = C_HEADS * C_DK
    full = lambda shape: pl.BlockSpec(shape, lambda b, c: (0,) * len(shape))
    return pl.pallas_call(
        functools.partial(_ret_kernel, nchunk=tc // CHUNK),
        grid=(bsz, p // tc),
        in_specs=[
            pl.BlockSpec((None, tc, hk), lambda b, c: (b, c, 0)),
            pl.BlockSpec((None, tc, hk), lambda b, c: (b, c, 1)),
            pl.BlockSpec((None, tc, hv), lambda b, c: (b, c, 1)),
            pl.BlockSpec((None, tc, hv), lambda b, c: (b, c, 2)),
            pl.BlockSpec((tc, hk), lambda b, c: (c, 0)),
            pl.BlockSpec((tc, hk), lambda b, c: (c, 0)),
            full((C_HEADS, CHUNK, CHUNK)),
            full((CHUNK, hk)),
            full((CHUNK, hk)),
            full((1, hk)),
        ],
        out_specs=pl.BlockSpec((None, tc, hv), lambda b, c: (b, c, 0)),
        out_shape=jax.ShapeDtypeStruct((bsz, p, hv), BF16),
        scratch_shapes=[pltpu.VMEM((C_HEADS // 2, 2 * C_DV, 2 * C_DK), F32)],
        compiler_params=_params(("parallel", "arbitrary")),
        name="retention",
    )(pcd, pcd, pcd, pcd, cos, sin, dmat, zeta, xi, cdec)


def _sb_kernel(q_ref, g_ref, k_ref, v_ref, o_ref, qm_ref, run_ref, acc_ref, hi_ref, lo_ref, ls_ref,
               w_ref):
    T = LANES
    i = pl.program_id(1)
    row = lax.broadcasted_iota(I32, (T, T), 0)
    col = lax.broadcasted_iota(I32, (T, T), 1)
    qpos = i * T + row
    low = col < D_DIM
    u_r = lax.broadcasted_iota(I32, (T, 2 * T), 0)
    u_c = lax.broadcasted_iota(I32, (T, 2 * T), 1)
    suffix = jnp.where((u_c >= T) | (u_r > u_c), 1.0, 0.0).astype(BF16)

    for h in range(D_HEADS):
        pair = q_ref[:, (h // 2) * T:(h // 2 + 1) * T]
        mine = low if h % 2 == 0 else ~low
        qm_ref[h] = jnp.where(mine, pair * (D_DIM ** -0.5), jnp.zeros_like(pair))
    run_ref[...] = jnp.zeros(run_ref.shape, F32)
    acc_ref[...] = jnp.zeros(acc_ref.shape, F32)

    def tile(state):
        t, _ = state
        kt = i - t
        ks = pl.ds(pl.multiple_of(kt * T, T), T)
        kpos = kt * T + col
        ok = (kpos < qpos) & (kpos >= PAD_FRONT)
        for h in range(D_HEADS):
            ps = slice((h // 2) * T, (h // 2 + 1) * T)
            z = lax.dot_general(qm_ref[h], k_ref[ks, ps], NT_DIMS, preferred_element_type=F32)
            sp, logsig = _softplus_parts(z)
            log_1m = jnp.where(ok, -sp, 0.0)
            hi = log_1m.astype(BF16)
            hi_ref[h] = hi
            lo_ref[h] = (log_1m - hi.astype(F32)).astype(BF16)
            ls_ref[h] = jnp.where(ok, logsig, NEG_INF)
        slowest = None
        for h in range(D_HEADS):
            sums = (jnp.dot(hi_ref[h], suffix, preferred_element_type=F32)
                    + jnp.dot(lo_ref[h], suffix, preferred_element_type=F32))
            run = run_ref[h]
            w_ref[h] = jnp.exp(ls_ref[h] + run + sums[:, :T]).astype(BF16)
            run = run + sums[:, T:]
            run_ref[h] = run
            slowest = run if slowest is None else jnp.maximum(slowest, run)
        for h in range(D_HEADS):
            ps = slice((h // 2) * T, (h // 2 + 1) * T)
            acc_ref[h] = acc_ref[h] + jnp.dot(w_ref[h], v_ref[ks, ps],
                                              preferred_element_type=F32)
        return t + 1, (jnp.max(slowest) > LOG_F32_UNDERFLOW).astype(I32)

    lax.while_loop(lambda s: (s[0] <= i) & (s[1] > 0), tile, (jnp.int32(0), jnp.int32(1)))

    for hp in range(D_HEADS // 2):
        ps = slice(hp * T, (hp + 1) * T)
        o = jnp.where(low, acc_ref[2 * hp], acc_ref[2 * hp + 1]) * _silu(g_ref[:, ps].astype(F32))
        o_ref[:, ps] = o.astype(o_ref.dtype)


def _stick_breaking(pcd):
    bsz, p, _ = pcd.shape
    T = LANES
    hw = D_HEADS * D_DIM
    one = pl.Buffered(1)
    return pl.pallas_call(
        _sb_kernel,
        grid=(bsz, p // T),
        in_specs=[
            pl.BlockSpec((None, T, hw), lambda b, i: (b, i, 3)),
            pl.BlockSpec((None, T, hw), lambda b, i: (b, i, 6)),
            pl.BlockSpec((None, p, hw), lambda b, i: (b, 0, 4), pipeline_mode=one),
            pl.BlockSpec((None, p, hw), lambda b, i: (b, 0, 5), pipeline_mode=one),
        ],
        out_specs=pl.BlockSpec((None, T, hw), lambda b, i: (b, i, 0)),
        out_shape=jax.ShapeDtypeStruct((bsz, p, hw), BF16),
        scratch_shapes=[
            pltpu.VMEM((D_HEADS, T, T), BF16),
            pltpu.VMEM((D_HEADS, T, T), F32),
            pltpu.VMEM((D_HEADS, T, T), F32),
            pltpu.VMEM((D_HEADS, T, T), BF16),
            pltpu.VMEM((D_HEADS, T, T), BF16),
            pltpu.VMEM((D_HEADS, T, T), F32),
            pltpu.VMEM((D_HEADS, T, T), BF16),
        ],
        compiler_params=_params(("parallel", "arbitrary")),
        name="stick_breaking",
    )(pcd, pcd, pcd, pcd)


def _rel_bucket_np(rel):
    half = REL_BUCKETS // 2
    max_exact = half // 2
    n = -rel
    ret = np.where(n < 0, half, 0)
    n = np.abs(n)
    edges = [math.ceil(max_exact * (REL_MAX_DIST / max_exact) ** (j / (half - max_exact)) - 1e-9)
             for j in range(1, half - max_exact)]
    large = max_exact + sum((n >= e).astype(np.int64) for e in edges)
    return ret + np.where(n < max_exact, n, large)


def _bias_tiles(rel_bias):
    i = np.arange(LANES)[:, None]
    j = np.arange(LANES)[None, :]
    idx = np.stack([_rel_bucket_np(j - i - LANES * d) for d in range(3)])
    assert (idx[2] == idx[2, 0, 0]).all()
    table = rel_bias.astype(F32)
    tiles = table[idx] - table[idx[2, 0, 0]]
    return jnp.transpose(tiles, (3, 0, 1, 2))


def _layout_ab(w):
    aq, ak, av, ag, iq, ik, iw, bq, bk, bv, bg, ba = jnp.split(w, np.cumsum(SPLIT_AB)[:-1].tolist(), axis=1)
    pad = lambda t: jnp.pad(t, ((0, 0), (0, LANES - t.shape[1])))
    cols = [aq, ak, av, ag, iq, bv, bg, bq, bk, jnp.concatenate([ik, ik], axis=1), pad(iw), pad(ba)]
    return jnp.concatenate(cols, axis=1).astype(BF16)


def kernel(x, meta_tokens, rel_bias, norm_g, final_g, w_in_ab, gla_gate_w2, gla_gate_b, w_out_ab,
           w_in_cd, w_out_cd):
    bsz, seq, d = x.shape
    p = seq + PAD_FRONT + N_META
    depth = norm_g.shape[0]
    topk = min(TOPK_MAX, seq // 4)
    h = jnp.concatenate([jnp.zeros((bsz, PAD_FRONT, d), x.dtype),
                         jnp.broadcast_to(meta_tokens.astype(x.dtype)[None], (bsz, N_META, d)),
                         x], axis=1)
    bias_tiles = _bias_tiles(rel_bias)
    tables = _retention_tables(p)
    for layer in range(depth):
        j = layer // 2
        if layer % 2 == 0:
            pab = _inproj(h, norm_g[layer], _layout_ab(w_in_ab[j]))
            oa = _dsa(pab, bias_tiles, topk)
            w2 = jnp.pad(gla_gate_w2[j], ((0, LANES - GLA_GATE_RANK), (0, 0))).astype(BF16)
            ob = _gla(pab, w2, gla_gate_b[j].reshape(1, -1).astype(F32))
            h = _outproj(oa, ob, w_out_ab[j], h)
        else:
            pcd = _inproj(h, norm_g[layer], w_in_cd[j].astype(BF16))
            oc = _retention(pcd, tables)
            od = _stick_breaking(pcd)
            h = _outproj(oc, od, w_out_cd[j], h)
    return _final_norm(h, final_g, seq)
```

```python
import functools
import math

import numpy as np
import jax
import jax.numpy as jnp
from jax import lax
from jax.experimental import pallas as pl
from jax.experimental.pallas import tpu as pltpu

F32 = jnp.float32
BF16 = jnp.bfloat16
I32 = jnp.int32
I16 = jnp.int16

D_MODEL = 1024
CHUNK = 64
N_META = 16
PAD_FRONT = 128 - N_META
NORM_EPS = 1e-6
NEG_INF = -1e30
A_HEADS, A_DIM = 8, 64
IDX_HEADS, IDX_DIM = 8, 64
IDX_SCALE = (IDX_DIM ** -0.5) * (IDX_HEADS ** -0.5)
TOPK_MAX = 256
B_HEADS, B_DK, B_DV = 4, 64, 128
GLA_GATE_RANK = 16
GLA_TAU = 16.0
C_HEADS, C_DK, C_DV = 4, 64, 128
ROPE_BASE = 10000.0
D_HEADS, D_DIM = 8, 64
REL_BUCKETS = 32
REL_MAX_DIST = 128
SPLIT_AB = (512, 512, 512, 512, 512, 64, 8, 256, 256, 512, 512, 16)
SPLIT_CD = (256, 256, 512, 512, 512, 512, 512, 512)

LANES = 128
W_AB_PAD = 4480
W_CD = 3584
I16_MIN = -2 ** 15
I16_MAX = 2 ** 15 - 1
DUMMY = 7
IDX_BITS = 14
LOG_F32_UNDERFLOW = -104.0
VMEM_LIMIT = 56 * 1024 * 1024

NT_DIMS = (((1,), (1,)), ((), ()))
TN_DIMS = (((0,), (0,)), ((), ()))


def _pick_tile(n, candidates):
    for c in candidates:
        if n % c == 0:
            return c
    raise ValueError(f"no tile for {n}")


def _params(sem):
    return pltpu.CompilerParams(dimension_semantics=sem, vmem_limit_bytes=VMEM_LIMIT)


def _silu(x):
    return x / (1.0 + jnp.exp(-x))


def _softplus_parts(z):
    t = jnp.log1p(jnp.exp(-jnp.abs(z)))
    return jnp.maximum(z, 0.0) + t, jnp.minimum(z, 0.0) - t


def _split3(x):
    a = x.astype(BF16)
    r = x - a.astype(F32)
    b = r.astype(BF16)
    c = (r - b.astype(F32)).astype(BF16)
    return a, b, c


def _inproj_kernel(x_ref, g_ref, w_ref, o_ref, hn_ref):
    @pl.when(pl.program_id(2) == 0)
    def _():
        x = x_ref[...]
        ms = jnp.mean(x * x, axis=-1, keepdims=True)
        hn_ref[...] = (x * lax.rsqrt(ms + NORM_EPS) * g_ref[...]).astype(BF16)

    o_ref[...] = jnp.dot(hn_ref[...], w_ref[...], preferred_element_type=F32).astype(o_ref.dtype)


def _inproj(h, g, w):
    bsz, p, d = h.shape
    n = w.shape[1]
    tp = _pick_tile(p, (1664, 1280, 640, 128))
    tn = _pick_tile(n, (896, 512, 128))
    return pl.pallas_call(
        _inproj_kernel,
        grid=(bsz, p // tp, n // tn),
        in_specs=[
            pl.BlockSpec((None, tp, d), lambda b, i, j: (b, i, 0)),
            pl.BlockSpec((1, d), lambda b, i, j: (0, 0)),
            pl.BlockSpec((d, tn), lambda b, i, j: (0, j)),
        ],
        out_specs=pl.BlockSpec((None, tp, tn), lambda b, i, j: (b, i, j)),
        out_shape=jax.ShapeDtypeStruct((bsz, p, n), BF16),
        scratch_shapes=[pltpu.VMEM((tp, d), BF16)],
        compiler_params=_params(("parallel", "parallel", "arbitrary")),
        name="inproj",
    )(h, g.reshape(1, d), w)


def _outproj_kernel(m1_ref, m2_ref, w1_ref, w2_ref, h_ref, o_ref, *, tp):
    y = jnp.dot(m1_ref[...], w1_ref[...], preferred_element_type=F32)
    y = y + jnp.dot(m2_ref[...], w2_ref[...], preferred_element_type=F32)
    pos = pl.program_id(1) * tp + lax.broadcasted_iota(I32, y.shape, 0)
    o_ref[...] = h_ref[...] + jnp.where(pos >= PAD_FRONT, y, 0.0)


def _outproj(m1, m2, w_out, h):
    bsz, p, d = h.shape
    k1 = m1.shape[-1]
    k2 = m2.shape[-1]
    tp = _pick_tile(p, (832, 640, 128))
    w = w_out.astype(BF16)
    return pl.pallas_call(
        functools.partial(_outproj_kernel, tp=tp),
        grid=(bsz, p // tp),
        in_specs=[
            pl.BlockSpec((None, tp, k1), lambda b, i: (b, i, 0)),
            pl.BlockSpec((None, tp, k2), lambda b, i: (b, i, 0)),
            pl.BlockSpec((k1, d), lambda b, i: (0, 0)),
            pl.BlockSpec((k2, d), lambda b, i: (0, 0)),
            pl.BlockSpec((None, tp, d), lambda b, i: (b, i, 0)),
        ],
        out_specs=pl.BlockSpec((None, tp, d), lambda b, i: (b, i, 0)),
        out_shape=jax.ShapeDtypeStruct((bsz, p, d), F32),
        compiler_params=_params(("parallel", "parallel")),
        name="outproj",
    )(m1, m2, w[:k1], w[k1:], h)


def _final_norm_kernel(x_ref, g_ref, o_ref):
    x = x_ref[...]
    ms = jnp.mean(x * x, axis=-1, keepdims=True)
    o_ref[...] = x * lax.rsqrt(ms + NORM_EPS) * g_ref[...]


def _final_norm(h, g, seq):
    bsz, p, d = h.shape
    skip = (p - seq) // LANES
    return pl.pallas_call(
        _final_norm_kernel,
        grid=(bsz, seq // LANES),
        in_specs=[
            pl.BlockSpec((None, LANES, d), lambda b, i: (b, i + skip, 0)),
            pl.BlockSpec((1, d), lambda b, i: (0, 0)),
        ],
        out_specs=pl.BlockSpec((None, LANES, d), lambda b, i: (b, i, 0)),
        out_shape=jax.ShapeDtypeStruct((bsz, seq, d), F32),
        compiler_params=_params(("parallel", "parallel")),
        name="final_norm",
    )(h, g.reshape(1, d))


def _dsa_kernel(aq_ref, ag_ref, iq_ref, iw_ref, k_ref, v_ref, ik_ref, bias_ref, o_ref,
                hi_ref, lo_ref, pos_ref, mask_ref, sa_ref, sb_ref, iqs_ref, wts_ref, qm_ref,
                jdx_ref, mx_ref, ls_ref, acc_ref, *, topk):
    T = LANES
    i = pl.program_id(1)
    last = i + DUMMY

    def num_groups(group):
        return (i + group) // group

    def first_slot(group):
        return last + 1 - group * num_groups(group)

    row = lax.broadcasted_iota(I32, (T, T), 0)
    col = lax.broadcasted_iota(I32, (T, T), 1)
    qchunk = (i * T + row) >> 6
    low = col < 64

    def admissible(kt):
        kpos = kt * T + col
        return (kpos >= PAD_FRONT) & ((kpos >> 6) <= qchunk)

    def tile_rows(slot):
        return pl.ds(pl.multiple_of(jnp.maximum(slot - DUMMY, 0) * T, T), T)

    def for_groups(group, body):
        first = first_slot(group)

        def step(g, carry):
            body(first + group * g)
            return carry

        lax.fori_loop(0, num_groups(group), step, 0)

    iw = iw_ref[...].astype(F32)
    for h in range(IDX_HEADS):
        pair = iq_ref[:, (h // 2) * T:(h // 2 + 1) * T]
        mine = low if h % 2 == 0 else ~low
        iqs_ref[h] = jnp.where(mine, pair, jnp.zeros_like(pair))
        wts_ref[h] = jnp.broadcast_to(iw[:, h:h + 1], (T, T))
    for h in range(A_HEADS):
        pair = aq_ref[:, (h // 2) * T:(h // 2 + 1) * T]
        mine = low if h % 2 == 0 else ~low
        qm_ref[h] = jnp.where(mine, pair * (A_DIM ** -0.5), jnp.zeros_like(pair))

    def sweep(lhs_ref, rhs_ref, paired, consume, consume_last):
        def issue(s0, stage):
            rows = [tile_rows(s0), tile_rows(s0 + 1)]
            for h in range(lhs_ref.shape[0]):
                ps = slice((h // 2) * T, (h // 2 + 1) * T) if paired else slice(0, T)
                kw = jnp.concatenate([rhs_ref[rows[0], ps], rhs_ref[rows[1], ps]], axis=0)
                stage[h] = lax.dot_general(lhs_ref[h], kw, NT_DIMS, preferred_element_type=F32)

        first = first_slot(4)
        issue(first, sa_ref)

        def body(q, carry):
            s = first + 4 * q
            issue(s + 2, sb_ref)
            consume(s, sa_ref)
            issue(s + 4, sa_ref)
            consume(s + 2, sb_ref)
            return carry

        lax.fori_loop(0, num_groups(4) - 1, body, 0)
        issue(last - 1, sb_ref)
        consume(last - 3, sa_ref)
        consume_last(last - 1, sb_ref)

    def score_pair(s0, stage):
        for u in range(2):
            tot = jnp.zeros((T, T), F32)
            for h in range(IDX_HEADS):
                tot = tot + jnp.maximum(stage[h, :, u * T:(u + 1) * T], 0.0) * wts_ref[h]
            score = jnp.where(admissible(s0 + u - DUMMY), tot * IDX_SCALE, NEG_INF)
            bits = lax.bitcast_convert_type(score, I32)
            key = bits ^ ((bits >> 31) & 0x7FFFFFFF)
            hi_ref[s0 + u] = (key >> 16).astype(I16)
            lo_ref[s0 + u] = ((key & 0xFFFF) + I16_MIN).astype(I16)

    sweep(iqs_ref, ik_ref, False, score_pair, score_pair)
    for d in range(DUMMY):
        hi_ref[d] = jnp.full((T, T), I16_MIN, I16)
        lo_ref[d] = jnp.full((T, T), I16_MIN, I16)

    def row_count(pred):
        one = jnp.ones((T, T), I16)
        zero = jnp.zeros((T, T), I16)
        first = first_slot(8)

        def body(g, acc):
            hits = [jnp.where(pred(first + 8 * g + u), one, zero) for u in range(8)]
            while len(hits) > 1:
                hits = [a + b for a, b in zip(hits[::2], hits[1::2])]
            return acc + hits[0]

        acc = lax.fori_loop(0, num_groups(8), body, zero)
        return jnp.sum(acc.astype(F32), axis=-1, keepdims=True)

    def wide16(x):
        return jnp.broadcast_to(x, (T, T)).astype(I16)

    def largest_with(ref, want):
        def bit(it, t):
            cand = t + (jnp.int32(1) << (15 - it))
            c16 = wide16(cand)
            cnt = row_count(lambda s: ref[s] >= c16)
            return jnp.where(cnt >= want, cand, t)
        return lax.fori_loop(0, 16, bit, jnp.full((T, 1), I16_MIN, I32))

    t_hi = largest_with(hi_ref, topk)
    t_hi16 = wide16(t_hi)
    above = row_count(lambda s: hi_ref[s] > t_hi16)

    def to_bucket(s0):
        for s in [s0 + u for u in range(8)]:
            lo_ref[s] = jnp.where(hi_ref[s] == t_hi16, lo_ref[s], jnp.full((T, T), I16_MIN, I16))

    for_groups(8, to_bucket)
    t_lo = largest_with(lo_ref, topk - above)
    t_lo16 = wide16(t_lo)

    need = topk - above - row_count(lambda s: lo_ref[s] > t_lo16)
    ties = row_count(lambda s: (hi_ref[s] == t_hi16) & (lo_ref[s] == t_lo16))
    jdx_ref[...] = jnp.full((T, T), 2 ** IDX_BITS, I32)

    @pl.when(jnp.max(jnp.where(ties > need, 1.0, 0.0)) > 0.0)
    def _():
        def tie_positions(s0):
            for s in [s0 + u for u in range(4)]:
                tie = (hi_ref[s].astype(I32) == t_hi) & (lo_ref[s].astype(I32) == t_lo)
                pos_ref[s] = jnp.where(tie, (s - DUMMY) * T + col, I16_MAX).astype(I16)

        for_groups(4, tie_positions)
        for d in range(DUMMY):
            pos_ref[d] = jnp.full((T, T), I16_MAX, I16)

        def index_bit(bit, j):
            cand = j + (jnp.int32(1) << bit)
            c16 = wide16(cand)
            cnt = row_count(lambda s: pos_ref[s] < c16)
            return jnp.where(cnt < need, cand, j)

        nbits = 7 + sum((i >= (1 << k)).astype(I32) for k in range(7))
        j = lax.fori_loop(0, nbits, lambda it, j: index_bit(nbits - 1 - it, j),
                          jnp.zeros((T, 1), I32))
        jdx_ref[...] = jnp.broadcast_to(j, (T, T))

    jdx = jdx_ref[...]

    def to_mask(s0):
        for s in [s0 + u for u in range(4)]:
            hi = hi_ref[s].astype(I32)
            lo = lo_ref[s].astype(I32)
            pos = (s - DUMMY) * T + col
            sel = (hi > t_hi) | ((hi == t_hi) & ((lo > t_lo) | ((lo == t_lo) & (pos <= jdx))))
            mask_ref[s] = jnp.where(sel & admissible(s - DUMMY), 0.0, NEG_INF)

    for_groups(4, to_mask)

    mx_ref[...] = jnp.full(mx_ref.shape, NEG_INF, F32)
    ls_ref[...] = jnp.zeros(ls_ref.shape, F32)
    acc_ref[...] = jnp.zeros(acc_ref.shape, F32)

    def attend(s0, stage, near, final):
        madd = [mask_ref[s0], mask_ref[s0 + 1]]
        for h in range(A_HEADS):
            parts = [stage[h, :, u * T:(u + 1) * T] + madd[u] for u in range(2)]
            if near:
                parts = [parts[u] + bias_ref[h, 1 - u] for u in range(2)]
            if not final:
                mx_ref[h] = jnp.maximum(mx_ref[h], jnp.maximum(parts[0], parts[1]))
            else:
                ps = slice((h // 2) * T, (h // 2 + 1) * T)
                es = [jnp.exp(part - mx_ref[h]) for part in parts]
                ls_ref[h] = ls_ref[h] + (es[0] + es[1])
                vw = jnp.concatenate([v_ref[tile_rows(s0), ps], v_ref[tile_rows(s0 + 1), ps]], axis=0)
                acc_ref[h] = acc_ref[h] + jnp.dot(jnp.concatenate(es, axis=1).astype(BF16), vw,
                                                  preferred_element_type=F32)

    for final in (False, True):
        sweep(qm_ref, k_ref, True,
              functools.partial(attend, near=False, final=final),
              functools.partial(attend, near=True, final=final))
        if not final:
            for h in range(A_HEADS):
                mx_ref[h] = jnp.broadcast_to(jnp.max(mx_ref[h], axis=-1, keepdims=True), (T, T))

    for hp in range(A_HEADS // 2):
        o0 = acc_ref[2 * hp] / jnp.sum(ls_ref[2 * hp], axis=-1, keepdims=True)
        o1 = acc_ref[2 * hp + 1] / jnp.sum(ls_ref[2 * hp + 1], axis=-1, keepdims=True)
        g = ag_ref[:, hp * T:(hp + 1) * T].astype(F32)
        o_ref[:, hp * T:(hp + 1) * T] = (jnp.where(low, o0, o1) * _silu(g)).astype(o_ref.dtype)


def _dsa(pab, bias_tiles, topk):
    bsz, p, _ = pab.shape
    T = LANES
    nq = p // T
    hw = A_HEADS * A_DIM
    one = pl.Buffered(1)
    return pl.pallas_call(
        functools.partial(_dsa_kernel, topk=float(topk)),
        grid=(bsz, nq),
        in_specs=[
            pl.BlockSpec((None, T, hw), lambda b, i: (b, i, 0)),
            pl.BlockSpec((None, T, hw), lambda b, i: (b, i, 3)),
            pl.BlockSpec((None, T, hw), lambda b, i: (b, i, 4)),
            pl.BlockSpec((None, T, T), lambda b, i: (b, i, 33)),
            pl.BlockSpec((None, p, hw), lambda b, i: (b, 0, 1), pipeline_mode=one),
            pl.BlockSpec((None, p, hw), lambda b, i: (b, 0, 2), pipeline_mode=one),
            pl.BlockSpec((None, p, T), lambda b, i: (b, 0, 32), pipeline_mode=one),
            pl.BlockSpec((A_HEADS, 2, T, T), lambda b, i: (0, 0, 0, 0), pipeline_mode=one),
        ],
        out_specs=pl.BlockSpec((None, T, hw), lambda b, i: (b, i, 0)),
        out_shape=jax.ShapeDtypeStruct((bsz, p, hw), BF16),
        scratch_shapes=[
            pltpu.VMEM((nq + DUMMY, T, T), I16),
            pltpu.VMEM((nq + DUMMY, T, T), I16),
            pltpu.VMEM((nq + DUMMY, T, T), I16),
            pltpu.VMEM((nq + DUMMY, T, T), F32),
            pltpu.VMEM((A_HEADS, T, 2 * T), F32),
            pltpu.VMEM((A_HEADS, T, 2 * T), F32),
            pltpu.VMEM((IDX_HEADS, T, T), BF16),
            pltpu.VMEM((IDX_HEADS, T, T), F32),
            pltpu.VMEM((A_HEADS, T, T), BF16),
            pltpu.VMEM((T, T), I32),
            pltpu.VMEM((A_HEADS, T, T), F32),
            pltpu.VMEM((A_HEADS, T, T), F32),
            pltpu.VMEM((A_HEADS, T, T), F32),
        ],
        compiler_params=_params(("parallel", "arbitrary")),
        name="dsa",
    )(pab, pab, pab, pab, pab, pab, pab, bias_tiles[:, :2])


def _gla_kernel(q_ref, k_ref, v_ref, g_ref, a_ref, w2_ref, gb_ref, o_ref, st_ref, *, nchunk):
    C = CHUNK
    T = LANES

    @pl.when(pl.program_id(1) == 0)
    def _():
        st_ref[...] = jnp.zeros(st_ref.shape, F32)

    r_i = lax.broadcasted_iota(I32, (C, C), 0)
    c_i = lax.broadcasted_iota(I32, (C, C), 1)
    causal = c_i <= r_i
    tri = jnp.where(causal, 1.0, 0.0).astype(BF16)
    lane = lax.broadcasted_iota(I32, (C, T), 1)
    sd_r = lax.broadcasted_iota(I32, (2 * B_DV, T), 0)
    sd_c = lax.broadcasted_iota(I32, (2 * B_DV, T), 1)
    blockdiag = (sd_r >= B_DV) == (sd_c >= B_DK)

    def chunk(c, carry):
        r = pl.ds(pl.multiple_of(c * C, C), C)
        x = jnp.dot(a_ref[r, :], w2_ref[...], preferred_element_type=F32) + gb_ref[...]
        log_a = _softplus_parts(x)[1] * (1.0 / GLA_TAU)
        a1, a2, a3 = _split3(log_a)
        bcum = (jnp.dot(tri, a1, preferred_element_type=F32)
                + jnp.dot(tri, a2, preferred_element_type=F32)
                + jnp.dot(tri, a3, preferred_element_type=F32))
        b_last = bcum[C - 1:C, :]
        q = q_ref[r, :].astype(F32) * (B_DK ** -0.5)
        k = k_ref[r, :].astype(F32)
        q_t = (q * jnp.exp(bcum)).astype(BF16)
        k_t = (k * jnp.exp(-bcum)).astype(BF16)
        k_d = (k * jnp.exp(b_last - bcum)).astype(BF16)
        decay = jnp.exp(b_last)
        for hp in range(B_HEADS // 2):
            ls = slice(hp * T, (hp + 1) * T)
            st = st_ref[hp]
            o_inter = lax.dot_general(q_t[:, ls], st.astype(BF16), NT_DIMS,
                                      preferred_element_type=F32)
            for hh in range(2):
                h = 2 * hp + hh
                mine = (lane < B_DK) if hh == 0 else (lane >= B_DK)
                qm = jnp.where(mine, q_t[:, ls], jnp.zeros((C, T), BF16))
                att = lax.dot_general(qm, k_t[:, ls], NT_DIMS, preferred_element_type=F32)
                att = jnp.where(causal, att, 0.0)
                vs = slice(h * B_DV, (h + 1) * B_DV)
                o = jnp.dot(att.astype(BF16), v_ref[r, vs], preferred_element_type=F32)
                o = o + o_inter[:, hh * B_DV:(hh + 1) * B_DV]
                o = o * lax.rsqrt(jnp.mean(o * o, axis=-1, keepdims=True) + NORM_EPS)
                o = o * _silu(g_ref[r, vs].astype(F32))
                o_ref[r, vs] = o.astype(o_ref.dtype)
            vp = v_ref[r, hp * 2 * B_DV:(hp + 1) * 2 * B_DV]
            contrib = lax.dot_general(vp, k_d[:, ls], TN_DIMS, preferred_element_type=F32)
            st_ref[hp] = st * decay[:, ls] + jnp.where(blockdiag, contrib, 0.0)
        return carry

    lax.fori_loop(0, nchunk, chunk, 0)


def _gla(pab, w2, gb):
    bsz, p, _ = pab.shape
    tc = _pick_tile(p, (640, 128))
    hv = B_HEADS * B_DV
    hk = B_HEADS * B_DK
    return pl.pallas_call(
        functools.partial(_gla_kernel, nchunk=tc // CHUNK),
        grid=(bsz, p // tc),
        in_specs=[
            pl.BlockSpec((None, tc, hk), lambda b, c: (b, c, 14)),
            pl.BlockSpec((None, tc, hk), lambda b, c: (b, c, 15)),
            pl.BlockSpec((None, tc, hv), lambda b, c: (b, c, 5)),
            pl.BlockSpec((None, tc, hv), lambda b, c: (b, c, 6)),
            pl.BlockSpec((None, tc, LANES), lambda b, c: (b, c, 34)),
            pl.BlockSpec((LANES, hk), lambda b, c: (0, 0)),
            pl.BlockSpec((1, hk), lambda b, c: (0, 0)),
        ],
        out_specs=pl.BlockSpec((None, tc, hv), lambda b, c: (b, c, 0)),
        out_shape=jax.ShapeDtypeStruct((bsz, p, hv), BF16),
        scratch_shapes=[pltpu.VMEM((B_HEADS // 2, 2 * B_DV, 2 * B_DK), F32)],
        compiler_params=_params(("parallel", "arbitrary")),
        name="gla",
    )(pab, pab, pab, pab, pab, w2, gb)


def _ret_kernel(q_ref, k_ref, v_ref, g_ref, cos_ref, sin_ref, dmat_ref, zeta_ref, xi_ref,
                cdec_ref, o_ref, st_ref, *, nchunk):
    C = CHUNK
    T = LANES
    W = C_HEADS * C_DK

    @pl.when(pl.program_id(1) == 0)
    def _():
        st_ref[...] = jnp.zeros(st_ref.shape, F32)

    lane_w = lax.broadcasted_iota(I32, (C, W), 1)
    first_half = (lane_w & (C_DK - 1)) < (C_DK // 2)
    lane = lax.broadcasted_iota(I32, (C, T), 1)
    sd_r = lax.broadcasted_iota(I32, (2 * C_DV, T), 0)
    sd_c = lax.broadcasted_iota(I32, (2 * C_DV, T), 1)
    blockdiag = (sd_r >= C_DV) == (sd_c >= C_DK)

    def rotate(x, cos, sin_signed):
        swapped = jnp.where(first_half, pltpu.roll(x, W - C_DK // 2, 1), pltpu.roll(x, C_DK // 2, 1))
        return x * cos + swapped * sin_signed

    def chunk(c, carry):
        r = pl.ds(pl.multiple_of(c * C, C), C)
        cos = cos_ref[r, :]
        sin = sin_ref[r, :]
        q = rotate(q_ref[r, :].astype(F32), cos, sin)
        k = rotate(k_ref[r, :].astype(F32), cos, sin) * (C_DK ** -0.5)
        q_b = q.astype(BF16)
        k_b = k.astype(BF16)
        q_x = (q * xi_ref[...]).astype(BF16)
        k_z = (k * zeta_ref[...]).astype(BF16)
        for hp in range(C_HEADS // 2):
            ls = slice(hp * T, (hp + 1) * T)
            st = st_ref[hp]
            o_inter = lax.dot_general(q_x[:, ls], st.astype(BF16), NT_DIMS,
                                      preferred_element_type=F32)
            for hh in range(2):
                h = 2 * hp + hh
                mine = (lane < C_DK) if hh == 0 else (lane >= C_DK)
                qm = jnp.where(mine, q_b[:, ls], jnp.zeros((C, T), BF16))
                att = lax.dot_general(qm, k_b[:, ls], NT_DIMS, preferred_element_type=F32)
                att = att * dmat_ref[h]
                vs = slice(h * C_DV, (h + 1) * C_DV)
                o = jnp.dot(att.astype(BF16), v_ref[r, vs], preferred_element_type=F32)
                o = o + o_inter[:, hh * C_DV:(hh + 1) * C_DV]
                o = o - jnp.mean(o, axis=-1, keepdims=True)
                o = o * lax.rsqrt(jnp.mean(o * o, axis=-1, keepdims=True) + NORM_EPS)
                o = o * _silu(g_ref[r, vs].astype(F32))
                o_ref[r, vs] = o.astype(o_ref.dtype)
            vp = v_ref[r, hp * 2 * C_DV:(hp + 1) * 2 * C_DV]
            contrib = lax.dot_general(vp, k_z[:, ls], TN_DIMS, preferred_element_type=F32)
            st_ref[hp] = st * cdec_ref[:, ls] + jnp.where(blockdiag, contrib, 0.0)
        return carry

    lax.fori_loop(0, nchunk, chunk, 0)


def _retention_tables(p):
    log_gamma = np.log(1.0 - np.exp2(-5.0 - np.arange(C_HEADS, dtype=np.float64)))
    i = np.arange(CHUNK, dtype=np.float64)
    diff = i[:, None] - i[None, :]
    dmat = np.where(diff >= 0, np.exp(log_gamma[:, None, None] * np.maximum(diff, 0.0)), 0.0)
    zeta = np.exp(log_gamma[:, None] * (CHUNK - 1 - i))
    xi = np.exp(log_gamma[:, None] * (i + 1))
    cdec = np.exp(log_gamma * CHUNK)
    widen = lambda t: np.repeat(t.T[:, :, None], C_DK, axis=2).reshape(CHUNK, C_HEADS * C_DK)
    half = C_DK // 2
    inv = jnp.asarray(ROPE_BASE, F32) ** (-jnp.arange(half, dtype=F32) / half)
    ang = jnp.arange(p, dtype=jnp.int32).astype(F32)[:, None] * inv[None, :]
    cos = jnp.tile(jnp.cos(ang), (1, 2 * C_HEADS))
    sin = jnp.sin(ang)
    sin_signed = jnp.tile(jnp.concatenate([-sin, sin], axis=1), (1, C_HEADS))
    return (cos, sin_signed, jnp.asarray(dmat, F32), jnp.asarray(widen(zeta), F32),
            jnp.asarray(widen(xi), F32),
            jnp.asarray(np.repeat(cdec, C_DK)[None, :], F32))


def _retention(pcd, tables):
    bsz, p, _ = pcd.shape
    cos, sin, dmat, zeta, xi, cdec = tables
    tc = _pick_tile(p, (640, 128))
    hv = C_HEADS * C_DV
    hk = C_HEADS * C_DK
    full = lambda shape: pl.BlockSpec(shape, lambda b, c: (0,) * len(shape))
    return pl.pallas_call(
        functools.partial(_ret_kernel, nchunk=tc // CHUNK),
        grid=(bsz, p // tc),
        in_specs=[
            pl.BlockSpec((None, tc, hk), lambda b, c: (b, c, 0)),
            pl.BlockSpec((None, tc, hk), lambda b, c: (b, c, 1)),
            pl.BlockSpec((None, tc, hv), lambda b, c: (b, c, 1)),
            pl.BlockSpec((None, tc, hv), lambda b, c: (b, c, 2)),
            pl.BlockSpec((tc, hk), lambda b, c: (c, 0)),
            pl.BlockSpec((tc, hk), lambda b, c: (c, 0)),
            full((C_HEADS, CHUNK, CHUNK)),
            full((CHUNK, hk)),
            full((CHUNK, hk)),
            full((1, hk)),
        ],
        out_specs=pl.BlockSpec((None, tc, hv), lambda b, c: (b, c, 0)),
        out_shape=jax.ShapeDtypeStruct((bsz, p, hv), BF16),
        scratch_shapes=[pltpu.VMEM((C_HEADS // 2, 2 * C_DV, 2 * C_DK), F32)],
        compiler_params=_params(("parallel", "arbitrary")),
        name="retention",
    )(pcd, pcd, pcd, pcd, cos, sin, dmat, zeta, xi, cdec)


def _sb_kernel(q_ref, g_ref, k_ref, v_ref, o_ref, qm_ref, run_ref, acc_ref, hi_ref, lo_ref, ls_ref,
               w_ref):
    T = LANES
    i = pl.program_id(1)
    row = lax.broadcasted_iota(I32, (T, T), 0)
    col = lax.broadcasted_iota(I32, (T, T), 1)
    qpos = i * T + row
    low = col < D_DIM
    u_r = lax.broadcasted_iota(I32, (T, 2 * T), 0)
    u_c = lax.broadcasted_iota(I32, (T, 2 * T), 1)
    suffix = jnp.where((u_c >= T) | (u_r > u_c), 1.0, 0.0).astype(BF16)

    for h in range(D_HEADS):
        pair = q_ref[:, (h // 2) * T:(h // 2 + 1) * T]
        mine = low if h % 2 == 0 else ~low
        qm_ref[h] = jnp.where(mine, pair * (D_DIM ** -0.5), jnp.zeros_like(pair))
    run_ref[...] = jnp.zeros(run_ref.shape, F32)
    acc_ref[...] = jnp.zeros(acc_ref.shape, F32)

    def tile(state):
        t, _ = state
        kt = i - t
        ks = pl.ds(pl.multiple_of(kt * T, T), T)
        kpos = kt * T + col
        ok = (kpos < qpos) & (kpos >= PAD_FRONT)
        for h in range(D_HEADS):
            ps = slice((h // 2) * T, (h // 2 + 1) * T)
            z = lax.dot_general(qm_ref[h], k_ref[ks, ps], NT_DIMS, preferred_element_type=F32)
            sp, logsig = _softplus_parts(z)
            log_1m = jnp.where(ok, -sp, 0.0)
            hi = log_1m.astype(BF16)
            hi_ref[h] = hi
            lo_ref[h] = (log_1m - hi.astype(F32)).astype(BF16)
            ls_ref[h] = jnp.where(ok, logsig, NEG_INF)
        slowest = None
        for h in range(D_HEADS):
            sums = (jnp.dot(hi_ref[h], suffix, preferred_element_type=F32)
                    + jnp.dot(lo_ref[h], suffix, preferred_element_type=F32))
            run = run_ref[h]
            w_ref[h] = jnp.exp(ls_ref[h] + run + sums[:, :T]).astype(BF16)
            run = run + sums[:, T:]
            run_ref[h] = run
            slowest = run if slowest is None else jnp.maximum(slowest, run)
        for h in range(D_HEADS):
            ps = slice((h // 2) * T, (h // 2 + 1) * T)
            acc_ref[h] = acc_ref[h] + jnp.dot(w_ref[h], v_ref[ks, ps],
                                              preferred_element_type=F32)
        return t + 1, (jnp.max(slowest) > LOG_F32_UNDERFLOW).astype(I32)

    lax.while_loop(lambda s: (s[0] <= i) & (s[1] > 0), tile, (jnp.int32(0), jnp.int32(1)))

    for hp in range(D_HEADS // 2):
        ps = slice(hp * T, (hp + 1) * T)
        o = jnp.where(low, acc_ref[2 * hp], acc_ref[2 * hp + 1]) * _silu(g_ref[:, ps].astype(F32))
        o_ref[:, ps] = o.astype(o_ref.dtype)


def _stick_breaking(pcd):
    bsz, p, _ = pcd.shape
    T = LANES
    hw = D_HEADS * D_DIM
    one = pl.Buffered(1)
    return pl.pallas_call(
        _sb_kernel,
        grid=(bsz, p // T),
        in_specs=[
            pl.BlockSpec((None, T, hw), lambda b, i: (b, i, 3)),
            pl.BlockSpec((None, T, hw), lambda b, i: (b, i, 6)),
            pl.BlockSpec((None, p, hw), lambda b, i: (b, 0, 4), pipeline_mode=one),
            pl.BlockSpec((None, p, hw), lambda b, i: (b, 0, 5), pipeline_mode=one),
        ],
        out_specs=pl.BlockSpec((None, T, hw), lambda b, i: (b, i, 0)),
        out_shape=jax.ShapeDtypeStruct((bsz, p, hw), BF16),
        scratch_shapes=[
            pltpu.VMEM((D_HEADS, T, T), BF16),
            pltpu.VMEM((D_HEADS, T, T), F32),
            pltpu.VMEM((D_HEADS, T, T), F32),
            pltpu.VMEM((D_HEADS, T, T), BF16),
            pltpu.VMEM((D_HEADS, T, T), BF16),
            pltpu.VMEM((D_HEADS, T, T), F32),
            pltpu.VMEM((D_HEADS, T, T), BF16),
        ],
        compiler_params=_params(("parallel", "arbitrary")),
        name="stick_breaking",
    )(pcd, pcd, pcd, pcd)


def _rel_bucket_np(rel):
    half = REL_BUCKETS // 2
    max_exact = half // 2
    n = -rel
    ret = np.where(n < 0, half, 0)
    n = np.abs(n)
    edges = [math.ceil(max_exact * (REL_MAX_DIST / max_exact) ** (j / (half - max_exact)) - 1e-9)
             for j in range(1, half - max_exact)]
    large = max_exact + sum((n >= e).astype(np.int64) for e in edges)
    return ret + np.where(n < max_exact, n, large)


def _bias_tiles(rel_bias):
    i = np.arange(LANES)[:, None]
    j = np.arange(LANES)[None, :]
    idx = np.stack([_rel_bucket_np(j - i - LANES * d) for d in range(3)])
    assert (idx[2] == idx[2, 0, 0]).all()
    table = rel_bias.astype(F32)
    tiles = table[idx] - table[idx[2, 0, 0]]
    return jnp.transpose(tiles, (3, 0, 1, 2))


def _layout_ab(w):
    aq, ak, av, ag, iq, ik, iw, bq, bk, bv, bg, ba = jnp.split(w, np.cumsum(SPLIT_AB)[:-1].tolist(), axis=1)
    pad = lambda t: jnp.pad(t, ((0, 0), (0, LANES - t.shape[1])))
    cols = [aq, ak, av, ag, iq, bv, bg, bq, bk, jnp.concatenate([ik, ik], axis=1), pad(iw), pad(ba)]
    return jnp.concatenate(cols, axis=1).astype(BF16)


def kernel(x, meta_tokens, rel_bias, norm_g, final_g, w_in_ab, gla_gate_w2, gla_gate_b, w_out_ab,
           w_in_cd, w_out_cd):
    bsz, seq, d = x.shape
    p = seq + PAD_FRONT + N_META
    depth = norm_g.shape[0]
    topk = min(TOPK_MAX, seq // 4)
    h = jnp.concatenate([jnp.zeros((bsz, PAD_FRONT, d), x.dtype),
                         jnp.broadcast_to(meta_tokens.astype(x.dtype)[None], (bsz, N_META, d)),
                         x], axis=1)
    bias_tiles = _bias_tiles(rel_bias)
    tables = _retention_tables(p)
    for layer in range(depth):
        j = layer // 2
        if layer % 2 == 0:
            pab = _inproj(h, norm_g[layer], _layout_ab(w_in_ab[j]))
            oa = _dsa(pab, bias_tiles, topk)
            w2 = jnp.pad(gla_gate_w2[j], ((0, LANES - GLA_GATE_RANK), (0, 0))).astype(BF16)
            ob = _gla(pab, w2, gla_gate_b[j].reshape(1, -1).astype(F32))
            h = _outproj(oa, ob, w_out_ab[j], h)
        else:
            pcd = _inproj(h, norm_g[layer], w_in_cd[j].astype(BF16))
            oc = _retention(pcd, tables)
            od = _stick_breaking(pcd)
            h = _outproj(oc, od, w_out_cd[j], h)
    return _final_norm(h, final_g, seq)
```

```python
import functools
import math

import numpy as np
import jax
import jax.numpy as jnp
from jax import lax
from jax.experimental import pallas as pl
from jax.experimental.pallas import tpu as pltpu

F32 = jnp.float32
BF16 = jnp.bfloat16
I32 = jnp.int32
I16 = jnp.int16

D_MODEL = 1024
CHUNK = 64
N_META = 16
PAD_FRONT = 128 - N_META
NORM_EPS = 1e-6
NEG_INF = -1e30
A_HEADS, A_DIM = 8, 64
IDX_HEADS, IDX_DIM = 8, 64
IDX_SCALE = (IDX_DIM ** -0.5) * (IDX_HEADS ** -0.5)
TOPK_MAX = 256
B_HEADS, B_DK, B_DV = 4, 64, 128
GLA_GATE_RANK = 16
GLA_TAU = 16.0
C_HEADS, C_DK, C_DV = 4, 64, 128
ROPE_BASE = 10000.0
D_HEADS, D_DIM = 8, 64
REL_BUCKETS = 32
REL_MAX_DIST = 128
SPLIT_AB = (512, 512, 512, 512, 512, 64, 8, 256, 256, 512, 512, 16)
SPLIT_CD = (256, 256, 512, 512, 512, 512, 512, 512)

LANES = 128
W_AB_PAD = 4480
W_CD = 3584
I16_MIN = -2 ** 15
I16_MAX = 2 ** 15 - 1
DUMMY = 7
IDX_BITS = 14
LOG_F32_UNDERFLOW = -104.0
VMEM_LIMIT = 56 * 1024 * 1024

NT_DIMS = (((1,), (1,)), ((), ()))
TN_DIMS = (((0,), (0,)), ((), ()))


def _pick_tile(n, candidates):
    for c in candidates:
        if n % c == 0:
            return c
    raise ValueError(f"no tile for {n}")


def _params(sem):
    return pltpu.CompilerParams(dimension_semantics=sem, vmem_limit_bytes=VMEM_LIMIT)


def _silu(x):
    return x / (1.0 + jnp.exp(-x))


def _softplus_parts(z):
    t = jnp.log1p(jnp.exp(-jnp.abs(z)))
    return jnp.maximum(z, 0.0) + t, jnp.minimum(z, 0.0) - t


def _split3(x):
    a = x.astype(BF16)
    r = x - a.astype(F32)
    b = r.astype(BF16)
    c = (r - b.astype(F32)).astype(BF16)
    return a, b, c


def _inproj_kernel(x_ref, g_ref, w_ref, o_ref, hn_ref):
    @pl.when(pl.program_id(2) == 0)
    def _():
        x = x_ref[...]
        ms = jnp.mean(x * x, axis=-1, keepdims=True)
        hn_ref[...] = (x * lax.rsqrt(ms + NORM_EPS) * g_ref[...]).astype(BF16)

    o_ref[...] = jnp.dot(hn_ref[...], w_ref[...], preferred_element_type=F32).astype(o_ref.dtype)


def _inproj(h, g, w):
    bsz, p, d = h.shape
    n = w.shape[1]
    tp = _pick_tile(p, (1664, 1280, 640, 128))
    tn = _pick_tile(n, (896, 512, 128))
    return pl.pallas_call(
        _inproj_kernel,
        grid=(bsz, p // tp, n // tn),
        in_specs=[
            pl.BlockSpec((None, tp, d), lambda b, i, j: (b, i, 0)),
            pl.BlockSpec((1, d), lambda b, i, j: (0, 0)),
            pl.BlockSpec((d, tn), lambda b, i, j: (0, j)),
        ],
        out_specs=pl.BlockSpec((None, tp, tn), lambda b, i, j: (b, i, j)),
        out_shape=jax.ShapeDtypeStruct((bsz, p, n), BF16),
        scratch_shapes=[pltpu.VMEM((tp, d), BF16)],
        compiler_params=_params(("parallel", "parallel", "arbitrary")),
        name="inproj",
    )(h, g.reshape(1, d), w)


def _outproj_kernel(m1_ref, m2_ref, w1_ref, w2_ref, h_ref, o_ref, *, tp):
    y = jnp.dot(m1_ref[...], w1_ref[...], preferred_element_type=F32)
    y = y + jnp.dot(m2_ref[...], w2_ref[...], preferred_element_type=F32)
    pos = pl.program_id(1) * tp + lax.broadcasted_iota(I32, y.shape, 0)
    o_ref[...] = h_ref[...] + jnp.where(pos >= PAD_FRONT, y, 0.0)


def _outproj(m1, m2, w_out, h):
    bsz, p, d = h.shape
    k1 = m1.shape[-1]
    k2 = m2.shape[-1]
    tp = _pick_tile(p, (832, 640, 128))
    w = w_out.astype(BF16)
    return pl.pallas_call(
        functools.partial(_outproj_kernel, tp=tp),
        grid=(bsz, p // tp),
        in_specs=[
            pl.BlockSpec((None, tp, k1), lambda b, i: (b, i, 0)),
            pl.BlockSpec((None, tp, k2), lambda b, i: (b, i, 0)),
            pl.BlockSpec((k1, d), lambda b, i: (0, 0)),
            pl.BlockSpec((k2, d), lambda b, i: (0, 0)),
            pl.BlockSpec((None, tp, d), lambda b, i: (b, i, 0)),
        ],
        out_specs=pl.BlockSpec((None, tp, d), lambda b, i: (b, i, 0)),
        out_shape=jax.ShapeDtypeStruct((bsz, p, d), F32),
        compiler_params=_params(("parallel", "parallel")),
        name="outproj",
    )(m1, m2, w[:k1], w[k1:], h)


def _final_norm_kernel(x_ref, g_ref, o_ref):
    x = x_ref[...]
    ms = jnp.mean(x * x, axis=-1, keepdims=True)
    o_ref[...] = x * lax.rsqrt(ms + NORM_EPS) * g_ref[...]


def _final_norm(h, g, seq):
    bsz, p, d = h.shape
    skip = (p - seq) // LANES
    return pl.pallas_call(
        _final_norm_kernel,
        grid=(bsz, seq // LANES),
        in_specs=[
            pl.BlockSpec((None, LANES, d), lambda b, i: (b, i + skip, 0)),
            pl.BlockSpec((1, d), lambda b, i: (0, 0)),
        ],
        out_specs=pl.BlockSpec((None, LANES, d), lambda b, i: (b, i, 0)),
        out_shape=jax.ShapeDtypeStruct((bsz, seq, d), F32),
        compiler_params=_params(("parallel", "parallel")),
        name="final_norm",
    )(h, g.reshape(1, d))


def _dsa_kernel(aq_ref, ag_ref, iq_ref, iw_ref, k_ref, v_ref, ik_ref, bias_ref, o_ref,
                hi_ref, lo_ref, pos_ref, mask_ref, sa_ref, sb_ref, iqs_ref, wts_ref, qm_ref,
                jdx_ref, mx_ref, ls_ref, acc_ref, *, topk):
    T = LANES
    i = pl.program_id(1)
    last = i + DUMMY

    def num_groups(group):
        return (i + group) // group

    def first_slot(group):
        return last + 1 - group * num_groups(group)

    row = lax.broadcasted_iota(I32, (T, T), 0)
    col = lax.broadcasted_iota(I32, (T, T), 1)
    qchunk = (i * T + row) >> 6
    low = col < 64

    def admissible(kt):
        kpos = kt * T + col
        return (kpos >= PAD_FRONT) & ((kpos >> 6) <= qchunk)

    def tile_rows(slot):
        return pl.ds(pl.multiple_of(jnp.maximum(slot - DUMMY, 0) * T, T), T)

    def for_groups(group, body):
        first = first_slot(group)

        def step(g, carry):
            body(first + group * g)
            return carry

        lax.fori_loop(0, num_groups(group), step, 0)

    iw = iw_ref[...].astype(F32)
    for h in range(IDX_HEADS):
        pair = iq_ref[:, (h // 2) * T:(h // 2 + 1) * T]
        mine = low if h % 2 == 0 else ~low
        iqs_ref[0, h * T:(h + 1) * T, :] = jnp.where(mine, pair, jnp.zeros_like(pair))
        wts_ref[h] = jnp.broadcast_to(iw[:, h:h + 1], (T, T))
    for h in range(A_HEADS):
        pair = aq_ref[:, (h // 2) * T:(h // 2 + 1) * T]
        mine = low if h % 2 == 0 else ~low
        qm_ref[h // 2, (h % 2) * T:(h % 2 + 1) * T, :] = jnp.where(
            mine, pair * (A_DIM ** -0.5), jnp.zeros_like(pair))

    def sweep(lhs_ref, rhs_ref, consume, consume_last):
        group_rows = lhs_ref.shape[1]

        def issue(s0, stage):
            rows = [tile_rows(s0), tile_rows(s0 + 1)]
            for g in range(lhs_ref.shape[0]):
                ps = slice(g * T, (g + 1) * T)
                kw = jnp.concatenate([rhs_ref[rows[0], ps], rhs_ref[rows[1], ps]], axis=0)
                stage[g * group_rows:(g + 1) * group_rows, :] = lax.dot_general(
                    lhs_ref[g], kw, NT_DIMS, preferred_element_type=F32)

        first = first_slot(4)
        issue(first, sa_ref)

        def body(q, carry):
            s = first + 4 * q
            issue(s + 2, sb_ref)
            consume(s, sa_ref)
            issue(s + 4, sa_ref)
            consume(s + 2, sb_ref)
            return carry

        lax.fori_loop(0, num_groups(4) - 1, body, 0)
        issue(last - 1, sb_ref)
        consume(last - 3, sa_ref)
        consume_last(last - 1, sb_ref)

    def score_pair(s0, stage):
        for u in range(2):
            tot = jnp.zeros((T, T), F32)
            for h in range(IDX_HEADS):
                tot = tot + jnp.maximum(stage[h * T:(h + 1) * T, u * T:(u + 1) * T], 0.0) * wts_ref[h]
            score = jnp.where(admissible(s0 + u - DUMMY), tot * IDX_SCALE, NEG_INF)
            bits = lax.bitcast_convert_type(score, I32)
            key = bits ^ ((bits >> 31) & 0x7FFFFFFF)
            hi_ref[s0 + u] = (key >> 16).astype(I16)
            lo_ref[s0 + u] = ((key & 0xFFFF) + I16_MIN).astype(I16)

    sweep(iqs_ref, ik_ref, score_pair, score_pair)
    for d in range(DUMMY):
        hi_ref[d] = jnp.full((T, T), I16_MIN, I16)
        lo_ref[d] = jnp.full((T, T), I16_MIN, I16)

    def row_count(pred):
        one = jnp.ones((T, T), I16)
        zero = jnp.zeros((T, T), I16)
        first = first_slot(8)

        def body(g, acc):
            hits = [jnp.where(pred(first + 8 * g + u), one, zero) for u in range(8)]
            while len(hits) > 1:
                hits = [a + b for a, b in zip(hits[::2], hits[1::2])]
            return acc + hits[0]

        acc = lax.fori_loop(0, num_groups(8), body, zero)
        return jnp.sum(acc.astype(F32), axis=-1, keepdims=True)

    def wide16(x):
        return jnp.broadcast_to(x, (T, T)).astype(I16)

    def largest_with(ref, want):
        def bit(it, t):
            cand = t + (jnp.int32(1) << (15 - it))
            c16 = wide16(cand)
            cnt = row_count(lambda s: ref[s] >= c16)
            return jnp.where(cnt >= want, cand, t)
        return lax.fori_loop(0, 16, bit, jnp.full((T, 1), I16_MIN, I32))

    t_hi = largest_with(hi_ref, topk)
    t_hi16 = wide16(t_hi)
    above = row_count(lambda s: hi_ref[s] > t_hi16)

    def to_bucket(s0):
        for s in [s0 + u for u in range(8)]:
            lo_ref[s] = jnp.where(hi_ref[s] == t_hi16, lo_ref[s], jnp.full((T, T), I16_MIN, I16))

    for_groups(8, to_bucket)
    t_lo = largest_with(lo_ref, topk - above)
    t_lo16 = wide16(t_lo)

    need = topk - above - row_count(lambda s: lo_ref[s] > t_lo16)
    ties = row_count(lambda s: (hi_ref[s] == t_hi16) & (lo_ref[s] == t_lo16))
    jdx_ref[...] = jnp.full((T, T), 2 ** IDX_BITS, I32)

    @pl.when(jnp.max(jnp.where(ties > need, 1.0, 0.0)) > 0.0)
    def _():
        def tie_positions(s0):
            for s in [s0 + u for u in range(4)]:
                tie = (hi_ref[s].astype(I32) == t_hi) & (lo_ref[s].astype(I32) == t_lo)
                pos_ref[s] = jnp.where(tie, (s - DUMMY) * T + col, I16_MAX).astype(I16)

        for_groups(4, tie_positions)
        for d in range(DUMMY):
            pos_ref[d] = jnp.full((T, T), I16_MAX, I16)

        def index_bit(bit, j):
            cand = j + (jnp.int32(1) << bit)
            c16 = wide16(cand)
            cnt = row_count(lambda s: pos_ref[s] < c16)
            return jnp.where(cnt < need, cand, j)

        nbits = 7 + sum((i >= (1 << k)).astype(I32) for k in range(7))
        j = lax.fori_loop(0, nbits, lambda it, j: index_bit(nbits - 1 - it, j),
                          jnp.zeros((T, 1), I32))
        jdx_ref[...] = jnp.broadcast_to(j, (T, T))

    jdx = jdx_ref[...]

    def to_mask(s0):
        for s in [s0 + u for u in range(4)]:
            hi = hi_ref[s].astype(I32)
            lo = lo_ref[s].astype(I32)
            pos = (s - DUMMY) * T + col
            sel = (hi > t_hi) | ((hi == t_hi) & ((lo > t_lo) | ((lo == t_lo) & (pos <= jdx))))
            mask_ref[s] = jnp.where(sel & admissible(s - DUMMY), 0.0, NEG_INF)

    for_groups(4, to_mask)

    mx_ref[...] = jnp.full(mx_ref.shape, NEG_INF, F32)
    ls_ref[...] = jnp.zeros(ls_ref.shape, F32)
    acc_ref[...] = jnp.zeros(acc_ref.shape, F32)

    def attend(s0, stage, near, final):
        madd = [mask_ref[s0], mask_ref[s0 + 1]]
        for hp in range(A_HEADS // 2):
            weights = []
            for h in (2 * hp, 2 * hp + 1):
                parts = [stage[h * T:(h + 1) * T, u * T:(u + 1) * T] + madd[u] for u in range(2)]
                if near:
                    parts = [parts[u] + bias_ref[h, 1 - u] for u in range(2)]
                if not final:
                    mx_ref[h] = jnp.maximum(mx_ref[h], jnp.maximum(parts[0], parts[1]))
                else:
                    es = [jnp.exp(part - mx_ref[h]) for part in parts]
                    ls_ref[h] = ls_ref[h] + (es[0] + es[1])
                    weights.append(jnp.concatenate(es, axis=1).astype(BF16))
            if final:
                ps = slice(hp * T, (hp + 1) * T)
                vw = jnp.concatenate([v_ref[tile_rows(s0), ps], v_ref[tile_rows(s0 + 1), ps]], axis=0)
                acc_ref[hp] = acc_ref[hp] + jnp.dot(jnp.concatenate(weights, axis=0), vw,
                                                    preferred_element_type=F32)

    for final in (False, True):
        sweep(qm_ref, k_ref,
              functools.partial(attend, near=False, final=final),
              functools.partial(attend, near=True, final=final))
        if not final:
            for h in range(A_HEADS):
                mx_ref[h] = jnp.broadcast_to(jnp.max(mx_ref[h], axis=-1, keepdims=True), (T, T))

    for hp in range(A_HEADS // 2):
        o0 = acc_ref[hp, :T, :] / jnp.sum(ls_ref[2 * hp], axis=-1, keepdims=True)
        o1 = acc_ref[hp, T:, :] / jnp.sum(ls_ref[2 * hp + 1], axis=-1, keepdims=True)
        g = ag_ref[:, hp * T:(hp + 1) * T].astype(F32)
        o_ref[:, hp * T:(hp + 1) * T] = (jnp.where(low, o0, o1) * _silu(g)).astype(o_ref.dtype)


def _dsa(pab, bias_tiles, topk):
    bsz, p, _ = pab.shape
    T = LANES
    nq = p // T
    hw = A_HEADS * A_DIM
    one = pl.Buffered(1)
    return pl.pallas_call(
        functools.partial(_dsa_kernel, topk=float(topk)),
        grid=(bsz, nq),
        in_specs=[
            pl.BlockSpec((None, T, hw), lambda b, i: (b, i, 0)),
            pl.BlockSpec((None, T, hw), lambda b, i: (b, i, 3)),
            pl.BlockSpec((None, T, hw), lambda b, i: (b, i, 4)),
            pl.BlockSpec((None, T, T), lambda b, i: (b, i, 33)),
            pl.BlockSpec((None, p, hw), lambda b, i: (b, 0, 1), pipeline_mode=one),
            pl.BlockSpec((None, p, hw), lambda b, i: (b, 0, 2), pipeline_mode=one),
            pl.BlockSpec((None, p, T), lambda b, i: (b, 0, 32), pipeline_mode=one),
            pl.BlockSpec((A_HEADS, 2, T, T), lambda b, i: (0, 0, 0, 0), pipeline_mode=one),
        ],
        out_specs=pl.BlockSpec((None, T, hw), lambda b, i: (b, i, 0)),
        out_shape=jax.ShapeDtypeStruct((bsz, p, hw), BF16),
        scratch_shapes=[
            pltpu.VMEM((nq + DUMMY, T, T), I16),
            pltpu.VMEM((nq + DUMMY, T, T), I16),
            pltpu.VMEM((nq + DUMMY, T, T), I16),
            pltpu.VMEM((nq + DUMMY, T, T), F32),
            pltpu.VMEM((A_HEADS * T, 2 * T), F32),
            pltpu.VMEM((A_HEADS * T, 2 * T), F32),
            pltpu.VMEM((1, IDX_HEADS * T, T), BF16),
            pltpu.VMEM((IDX_HEADS, T, T), F32),
            pltpu.VMEM((A_HEADS // 2, 2 * T, T), BF16),
            pltpu.VMEM((T, T), I32),
            pltpu.VMEM((A_HEADS, T, T), F32),
            pltpu.VMEM((A_HEADS, T, T), F32),
            pltpu.VMEM((A_HEADS // 2, 2 * T, T), F32),
        ],
        compiler_params=_params(("parallel", "arbitrary")),
        name="dsa",
    )(pab, pab, pab, pab, pab, pab, pab, bias_tiles[:, :2])


def _gla_kernel(q_ref, k_ref, v_ref, g_ref, a_ref, w2_ref, gb_ref, o_ref, st_ref, *, nchunk):
    C = CHUNK
    T = LANES

    @pl.when(pl.program_id(1) == 0)
    def _():
        st_ref[...] = jnp.zeros(st_ref.shape, F32)

    r_i = lax.broadcasted_iota(I32, (C, C), 0)
    c_i = lax.broadcasted_iota(I32, (C, C), 1)
    causal = c_i <= r_i
    tri = jnp.where(causal, 1.0, 0.0).astype(BF16)
    lane = lax.broadcasted_iota(I32, (C, T), 1)
    sd_r = lax.broadcasted_iota(I32, (2 * B_DV, T), 0)
    sd_c = lax.broadcasted_iota(I32, (2 * B_DV, T), 1)
    blockdiag = (sd_r >= B_DV) == (sd_c >= B_DK)

    def chunk(c, carry):
        r = pl.ds(pl.multiple_of(c * C, C), C)
        x = jnp.dot(a_ref[r, :], w2_ref[...], preferred_element_type=F32) + gb_ref[...]
        log_a = _softplus_parts(x)[1] * (1.0 / GLA_TAU)
        a1, a2, a3 = _split3(log_a)
        bcum = (jnp.dot(tri, a1, preferred_element_type=F32)
                + jnp.dot(tri, a2, preferred_element_type=F32)
                + jnp.dot(tri, a3, preferred_element_type=F32))
        b_last = bcum[C - 1:C, :]
        q = q_ref[r, :].astype(F32) * (B_DK ** -0.5)
        k = k_ref[r, :].astype(F32)
        q_t = (q * jnp.exp(bcum)).astype(BF16)
        k_t = (k * jnp.exp(-bcum)).astype(BF16)
        k_d = (k * jnp.exp(b_last - bcum)).astype(BF16)
        decay = jnp.exp(b_last)
        for hp in range(B_HEADS // 2):
            ls = slice(hp * T, (hp + 1) * T)
            st = st_ref[hp]
            o_inter = lax.dot_general(q_t[:, ls], st.astype(BF16), NT_DIMS,
                                      preferred_element_type=F32)
            for hh in range(2):
                h = 2 * hp + hh
                mine = (lane < B_DK) if hh == 0 else (lane >= B_DK)
                qm = jnp.where(mine, q_t[:, ls], jnp.zeros((C, T), BF16))
                att = lax.dot_general(qm, k_t[:, ls], NT_DIMS, preferred_element_type=F32)
                att = jnp.where(causal, att, 0.0)
                vs = slice(h * B_DV, (h + 1) * B_DV)
                o = jnp.dot(att.astype(BF16), v_ref[r, vs], preferred_element_type=F32)
                o = o + o_inter[:, hh * B_DV:(hh + 1) * B_DV]
                o = o * lax.rsqrt(jnp.mean(o * o, axis=-1, keepdims=True) + NORM_EPS)
                o = o * _silu(g_ref[r, vs].astype(F32))
                o_ref[r, vs] = o.astype(o_ref.dtype)
            vp = v_ref[r, hp * 2 * B_DV:(hp + 1) * 2 * B_DV]
            contrib = lax.dot_general(vp, k_d[:, ls], TN_DIMS, preferred_element_type=F32)
            st_ref[hp] = st * decay[:, ls] + jnp.where(blockdiag, contrib, 0.0)
        return carry

    lax.fori_loop(0, nchunk, chunk, 0)


def _gla(pab, w2, gb):
    bsz, p, _ = pab.shape
    tc = _pick_tile(p, (640, 128))
    hv = B_HEADS * B_DV
    hk = B_HEADS * B_DK
    return pl.pallas_call(
        functools.partial(_gla_kernel, nchunk=tc // CHUNK),
        grid=(bsz, p // tc),
        in_specs=[
            pl.BlockSpec((None, tc, hk), lambda b, c: (b, c, 14)),
            pl.BlockSpec((None, tc, hk), lambda b, c: (b, c, 15)),
            pl.BlockSpec((None, tc, hv), lambda b, c: (b, c, 5)),
            pl.BlockSpec((None, tc, hv), lambda b, c: (b, c, 6)),
            pl.BlockSpec((None, tc, LANES), lambda b, c: (b, c, 34)),
            pl.BlockSpec((LANES, hk), lambda b, c: (0, 0)),
            pl.BlockSpec((1, hk), lambda b, c: (0, 0)),
        ],
        out_specs=pl.BlockSpec((None, tc, hv), lambda b, c: (b, c, 0)),
        out_shape=jax.ShapeDtypeStruct((bsz, p, hv), BF16),
        scratch_shapes=[pltpu.VMEM((B_HEADS // 2, 2 * B_DV, 2 * B_DK), F32)],
        compiler_params=_params(("parallel", "arbitrary")),
        name="gla",
    )(pab, pab, pab, pab, pab, w2, gb)


def _ret_kernel(q_ref, k_ref, v_ref, g_ref, cos_ref, sin_ref, dmat_ref, zeta_ref, xi_ref,
                cdec_ref, o_ref, st_ref, *, nchunk):
    C = CHUNK
    T = LANES
    W = C_HEADS * C_DK

    @pl.when(pl.program_id(1) == 0)
    def _():
        st_ref[...] = jnp.zeros(st_ref.shape, F32)

    lane_w = lax.broadcasted_iota(I32, (C, W), 1)
    first_half = (lane_w & (C_DK - 1)) < (C_DK // 2)
    lane = lax.broadcasted_iota(I32, (C, T), 1)
    sd_r = lax.broadcasted_iota(I32, (2 * C_DV, T), 0)
    sd_c = lax.broadcasted_iota(I32, (2 * C_DV, T), 1)
    blockdiag = (sd_r >= C_DV) == (sd_c >= C_DK)

    def rotate(x, cos, sin_signed):
        swapped = jnp.where(first_half, pltpu.roll(x, W - C_DK // 2, 1), pltpu.roll(x, C_DK // 2, 1))
        return x * cos + swapped * sin_signed

    def chunk(c, carry):
        r = pl.ds(pl.multiple_of(c * C, C), C)
        cos = cos_ref[r, :]
        sin = sin_ref[r, :]
        q = rotate(q_ref[r, :].astype(F32), cos, sin)
        k = rotate(k_ref[r, :].astype(F32), cos, sin) * (C_DK ** -0.5)
        q_b = q.astype(BF16)
        k_b = k.astype(BF16)
        q_x = (q * xi_ref[...]).astype(BF16)
        k_z = (k * zeta_ref[...]).astype(BF16)
        for hp in range(C_HEADS // 2):
            ls = slice(hp * T, (hp + 1) * T)
            st = st_ref[hp]
            o_inter = lax.dot_general(q_x[:, ls], st.astype(BF16), NT_DIMS,
                                      preferred_element_type=F32)
            for hh in range(2):
                h = 2 * hp + hh
                mine = (lane < C_DK) if hh == 0 else (lane >= C_DK)
                qm = jnp.where(mine, q_b[:, ls], jnp.zeros((C, T), BF16))
                att = lax.dot_general(qm, k_b[:, ls], NT_DIMS, preferred_element_type=F32)
                att = att * dmat_ref[h]
                vs = slice(h * C_DV, (h + 1) * C_DV)
                o = jnp.dot(att.astype(BF16), v_ref[r, vs], preferred_element_type=F32)
                o = o + o_inter[:, hh * C_DV:(hh + 1) * C_DV]
                o = o - jnp.mean(o, axis=-1, keepdims=True)
                o = o * lax.rsqrt(jnp.mean(o * o, axis=-1, keepdims=True) + NORM_EPS)
                o = o * _silu(g_ref[r, vs].astype(F32))
                o_ref[r, vs] = o.astype(o_ref.dtype)
            vp = v_ref[r, hp * 2 * C_DV:(hp + 1) * 2 * C_DV]
            contrib = lax.dot_general(vp, k_z[:, ls], TN_DIMS, preferred_element_type=F32)
            st_ref[hp] = st * cdec_ref[:, ls] + jnp.where(blockdiag, contrib, 0.0)
        return carry

    lax.fori_loop(0, nchunk, chunk, 0)


def _retention_tables(p):
    log_gamma = np.log(1.0 - np.exp2(-5.0 - np.arange(C_HEADS, dtype=np.float64)))
    i = np.arange(CHUNK, dtype=np.float64)
    diff = i[:, None] - i[None, :]
    dmat = np.where(diff >= 0, np.exp(log_gamma[:, None, None] * np.maximum(diff, 0.0)), 0.0)
    zeta = np.exp(log_gamma[:, None] * (CHUNK - 1 - i))
    xi = np.exp(log_gamma[:, None] * (i + 1))
    cdec = np.exp(log_gamma * CHUNK)
    widen = lambda t: np.repeat(t.T[:, :, None], C_DK, axis=2).reshape(CHUNK, C_HEADS * C_DK)
    half = C_DK // 2
    inv = jnp.asarray(ROPE_BASE, F32) ** (-jnp.arange(half, dtype=F32) / half)
    ang = jnp.arange(p, dtype=jnp.int32).astype(F32)[:, None] * inv[None, :]
    cos = jnp.tile(jnp.cos(ang), (1, 2 * C_HEADS))
    sin = jnp.sin(ang)
    sin_signed = jnp.tile(jnp.concatenate([-sin, sin], axis=1), (1, C_HEADS))
    return (cos, sin_signed, jnp.asarray(dmat, F32), jnp.asarray(widen(zeta), F32),
            jnp.asarray(widen(xi), F32),
            jnp.asarray(np.repeat(cdec, C_DK)[None, :], F32))


def _retention(pcd, tables):
    bsz, p, _ = pcd.shape
    cos, sin, dmat, zeta, xi, cdec = tables
    tc = _pick_tile(p, (640, 128))
    hv = C_HEADS * C_DV
    hk = C_HEADS * C_DK
    full = lambda shape: pl.BlockSpec(shape, lambda b, c: (0,) * len(shape))
    return pl.pallas_call(
        functools.partial(_ret_kernel, nchunk=tc // CHUNK),
        grid=(bsz, p // tc),
        in_specs=[
            pl.BlockSpec((None, tc, hk), lambda b, c: (b, c, 0)),
            pl.BlockSpec((None, tc, hk), lambda b, c: (b, c, 1)),
            pl.BlockSpec((None, tc, hv), lambda b, c: (b, c, 1)),
            pl.BlockSpec((None, tc, hv), lambda b, c: (b, c, 2)),
            pl.BlockSpec((tc, hk), lambda b, c: (c, 0)),
            pl.BlockSpec((tc, hk), lambda b, c: (c, 0)),
            full((C_HEADS, CHUNK, CHUNK)),
            full((CHUNK, hk)),
            full((CHUNK, hk)),
            full((1, hk)),
        ],
        out_specs=pl.BlockSpec((None, tc, hv), lambda b, c: (b, c, 0)),
        out_shape=jax.ShapeDtypeStruct((bsz, p, hv), BF16),
        scratch_shapes=[pltpu.VMEM((C_HEADS // 2, 2 * C_DV, 2 * C_DK), F32)],
        compiler_params=_params(("parallel", "arbitrary")),
        name="retention",
    )(pcd, pcd, pcd, pcd, cos, sin, dmat, zeta, xi, cdec)


def _sb_kernel(q_ref, g_ref, k_ref, v_ref, o_ref, qm_ref, run_ref, acc_ref, hi_ref, lo_ref, ls_ref,
               w_ref):
    T = LANES
    i = pl.program_id(1)
    row = lax.broadcasted_iota(I32, (T, T), 0)
    col = lax.broadcasted_iota(I32, (T, T), 1)
    qpos = i * T + row
    low = col < D_DIM
    u_r = lax.broadcasted_iota(I32, (T, 2 * T), 0)
    u_c = lax.broadcasted_iota(I32, (T, 2 * T), 1)
    suffix = jnp.where((u_c >= T) | (u_r > u_c), 1.0, 0.0).astype(BF16)

    for h in range(D_HEADS):
        pair = q_ref[:, (h // 2) * T:(h // 2 + 1) * T]
        mine = low if h % 2 == 0 else ~low
        qm_ref[h] = jnp.where(mine, pair * (D_DIM ** -0.5), jnp.zeros_like(pair))
    run_ref[...] = jnp.zeros(run_ref.shape, F32)
    acc_ref[...] = jnp.zeros(acc_ref.shape, F32)

    def tile(state):
        t, _ = state
        kt = i - t
        ks = pl.ds(pl.multiple_of(kt * T, T), T)
        kpos = kt * T + col
        ok = (kpos < qpos) & (kpos >= PAD_FRONT)
        for h in range(D_HEADS):
            ps = slice((h // 2) * T, (h // 2 + 1) * T)
            z = lax.dot_general(qm_ref[h], k_ref[ks, ps], NT_DIMS, preferred_element_type=F32)
            sp, logsig = _softplus_parts(z)
            log_1m = jnp.where(ok, -sp, 0.0)
            hi = log_1m.astype(BF16)
            hi_ref[h] = hi
            lo_ref[h] = (log_1m - hi.astype(F32)).astype(BF16)
            ls_ref[h] = jnp.where(ok, logsig, NEG_INF)
        slowest = None
        for h in range(D_HEADS):
            sums = (jnp.dot(hi_ref[h], suffix, preferred_element_type=F32)
                    + jnp.dot(lo_ref[h], suffix, preferred_element_type=F32))
            run = run_ref[h]
            w_ref[h] = jnp.exp(ls_ref[h] + run + sums[:, :T]).astype(BF16)
            run = run + sums[:, T:]
            run_ref[h] = run
            slowest = run if slowest is None else jnp.maximum(slowest, run)
        for h in range(D_HEADS):
            ps = slice((h // 2) * T, (h // 2 + 1) * T)
            acc_ref[h] = acc_ref[h] + jnp.dot(w_ref[h], v_ref[ks, ps],
                                              preferred_element_type=F32)
        return t + 1, (jnp.max(slowest) > LOG_F32_UNDERFLOW).astype(I32)

    lax.while_loop(lambda s: (s[0] <= i) & (s[1] > 0), tile, (jnp.int32(0), jnp.int32(1)))

    for hp in range(D_HEADS // 2):
        ps = slice(hp * T, (hp + 1) * T)
        o = jnp.where(low, acc_ref[2 * hp], acc_ref[2 * hp + 1]) * _silu(g_ref[:, ps].astype(F32))
        o_ref[:, ps] = o.astype(o_ref.dtype)


def _stick_breaking(pcd):
    bsz, p, _ = pcd.shape
    T = LANES
    hw = D_HEADS * D_DIM
    one = pl.Buffered(1)
    return pl.pallas_call(
        _sb_kernel,
        grid=(bsz, p // T),
        in_specs=[
            pl.BlockSpec((None, T, hw), lambda b, i: (b, i, 3)),
            pl.BlockSpec((None, T, hw), lambda b, i: (b, i, 6)),
            pl.BlockSpec((None, p, hw), lambda b, i: (b, 0, 4), pipeline_mode=one),
            pl.BlockSpec((None, p, hw), lambda b, i: (b, 0, 5), pipeline_mode=one),
        ],
        out_specs=pl.BlockSpec((None, T, hw), lambda b, i: (b, i, 0)),
        out_shape=jax.ShapeDtypeStruct((bsz, p, hw), BF16),
        scratch_shapes=[
            pltpu.VMEM((D_HEADS, T, T), BF16),
            pltpu.VMEM((D_HEADS, T, T), F32),
            pltpu.VMEM((D_HEADS, T, T), F32),
            pltpu.VMEM((D_HEADS, T, T), BF16),
            pltpu.VMEM((D_HEADS, T, T), BF16),
            pltpu.VMEM((D_HEADS, T, T), F32),
            pltpu.VMEM((D_HEADS, T, T), BF16),
        ],
        compiler_params=_params(("parallel", "arbitrary")),
        name="stick_breaking",
    )(pcd, pcd, pcd, pcd)


def _rel_bucket_np(rel):
    half = REL_BUCKETS // 2
    max_exact = half // 2
    n = -rel
    ret = np.where(n < 0, half, 0)
    n = np.abs(n)
    edges = [math.ceil(max_exact * (REL_MAX_DIST / max_exact) ** (j / (half - max_exact)) - 1e-9)
             for j in range(1, half - max_exact)]
    large = max_exact + sum((n >= e).astype(np.int64) for e in edges)
    return ret + np.where(n < max_exact, n, large)


def _bias_tiles(rel_bias):
    i = np.arange(LANES)[:, None]
    j = np.arange(LANES)[None, :]
    idx = np.stack([_rel_bucket_np(j - i - LANES * d) for d in range(3)])
    assert (idx[2] == idx[2, 0, 0]).all()
    table = rel_bias.astype(F32)
    tiles = table[idx] - table[idx[2, 0, 0]]
    return jnp.transpose(tiles, (3, 0, 1, 2))


def _layout_ab(w):
    aq, ak, av, ag, iq, ik, iw, bq, bk, bv, bg, ba = jnp.split(w, np.cumsum(SPLIT_AB)[:-1].tolist(), axis=1)
    pad = lambda t: jnp.pad(t, ((0, 0), (0, LANES - t.shape[1])))
    cols = [aq, ak, av, ag, iq, bv, bg, bq, bk, jnp.concatenate([ik, ik], axis=1), pad(iw), pad(ba)]
    return jnp.concatenate(cols, axis=1).astype(BF16)


def kernel(x, meta_tokens, rel_bias, norm_g, final_g, w_in_ab, gla_gate_w2, gla_gate_b, w_out_ab,
           w_in_cd, w_out_cd):
    bsz, seq, d = x.shape
    p = seq + PAD_FRONT + N_META
    depth = norm_g.shape[0]
    topk = min(TOPK_MAX, seq // 4)
    h = jnp.concatenate([jnp.zeros((bsz, PAD_FRONT, d), x.dtype),
                         jnp.broadcast_to(meta_tokens.astype(x.dtype)[None], (bsz, N_META, d)),
                         x], axis=1)
    bias_tiles = _bias_tiles(rel_bias)
    tables = _retention_tables(p)
    for layer in range(depth):
        j = layer // 2
        if layer % 2 == 0:
            pab = _inproj(h, norm_g[layer], _layout_ab(w_in_ab[j]))
            oa = _dsa(pab, bias_tiles, topk)
            w2 = jnp.pad(gla_gate_w2[j], ((0, LANES - GLA_GATE_RANK), (0, 0))).astype(BF16)
            ob = _gla(pab, w2, gla_gate_b[j].reshape(1, -1).astype(F32))
            h = _outproj(oa, ob, w_out_ab[j], h)
        else:
            pcd = _inproj(h, norm_g[layer], w_in_cd[j].astype(BF16))
            oc = _retention(pcd, tables)
            od = _stick_breaking(pcd)
            h = _outproj(oc, od, w_out_cd[j], h)
    return _final_norm(h, final_g, seq)
```

```python
import functools
import math

import numpy as np
import jax
import jax.numpy as jnp
from jax import lax
from jax.experimental import pallas as pl
from jax.experimental.pallas import tpu as pltpu

F32 = jnp.float32
BF16 = jnp.bfloat16
I32 = jnp.int32

D_MODEL = 1024
CHUNK = 64
N_META = 16
PAD_FRONT = 128 - N_META
NORM_EPS = 1e-6
NEG_INF = -1e30
A_HEADS, A_DIM = 8, 64
IDX_HEADS, IDX_DIM = 8, 64
IDX_SCALE = (IDX_DIM ** -0.5) * (IDX_HEADS ** -0.5)
TOPK_MAX = 256
B_HEADS, B_DK, B_DV = 4, 64, 128
GLA_GATE_RANK = 16
GLA_TAU = 16.0
C_HEADS, C_DK, C_DV = 4, 64, 128
ROPE_BASE = 10000.0
D_HEADS, D_DIM = 8, 64
REL_BUCKETS = 32
REL_MAX_DIST = 128
SPLIT_AB = (512, 512, 512, 512, 512, 64, 8, 256, 256, 512, 512, 16)
SPLIT_CD = (256, 256, 512, 512, 512, 512, 512, 512)

LANES = 128
W_AB_PAD = 4480
W_CD = 3584
I32_MIN = -2 ** 31
I32_MAX = 2 ** 31 - 1
DUMMY = 7
IDX_BITS = 14
LOG_F32_UNDERFLOW = -104.0
VMEM_LIMIT = 56 * 1024 * 1024

NT_DIMS = (((1,), (1,)), ((), ()))
TN_DIMS = (((0,), (0,)), ((), ()))


def _pick_tile(n, candidates):
    for c in candidates:
        if n % c == 0:
            return c
    raise ValueError(f"no tile for {n}")


def _params(sem):
    return pltpu.CompilerParams(dimension_semantics=sem, vmem_limit_bytes=VMEM_LIMIT)


def _silu(x):
    return x / (1.0 + jnp.exp(-x))


def _softplus_parts(z):
    t = jnp.log1p(jnp.exp(-jnp.abs(z)))
    return jnp.maximum(z, 0.0) + t, jnp.minimum(z, 0.0) - t


def _split3(x):
    a = x.astype(BF16)
    r = x - a.astype(F32)
    b = r.astype(BF16)
    c = (r - b.astype(F32)).astype(BF16)
    return a, b, c


def _inproj_kernel(x_ref, g_ref, w_ref, o_ref, hn_ref):
    @pl.when(pl.program_id(2) == 0)
    def _():
        x = x_ref[...]
        ms = jnp.mean(x * x, axis=-1, keepdims=True)
        hn_ref[...] = (x * lax.rsqrt(ms + NORM_EPS) * g_ref[...]).astype(BF16)

    o_ref[...] = jnp.dot(hn_ref[...], w_ref[...], preferred_element_type=F32).astype(o_ref.dtype)


def _inproj(h, g, w):
    bsz, p, d = h.shape
    n = w.shape[1]
    tp = _pick_tile(p, (1664, 1280, 640, 128))
    tn = _pick_tile(n, (896, 512, 128))
    return pl.pallas_call(
        _inproj_kernel,
        grid=(bsz, p // tp, n // tn),
        in_specs=[
            pl.BlockSpec((None, tp, d), lambda b, i, j: (b, i, 0)),
            pl.BlockSpec((1, d), lambda b, i, j: (0, 0)),
            pl.BlockSpec((d, tn), lambda b, i, j: (0, j)),
        ],
        out_specs=pl.BlockSpec((None, tp, tn), lambda b, i, j: (b, i, j)),
        out_shape=jax.ShapeDtypeStruct((bsz, p, n), BF16),
        scratch_shapes=[pltpu.VMEM((tp, d), BF16)],
        compiler_params=_params(("parallel", "parallel", "arbitrary")),
        name="inproj",
    )(h, g.reshape(1, d), w)


def _outproj_kernel(m1_ref, m2_ref, w1_ref, w2_ref, h_ref, o_ref, *, tp):
    y = jnp.dot(m1_ref[...], w1_ref[...], preferred_element_type=F32)
    y = y + jnp.dot(m2_ref[...], w2_ref[...], preferred_element_type=F32)
    pos = pl.program_id(1) * tp + lax.broadcasted_iota(I32, y.shape, 0)
    o_ref[...] = h_ref[...] + jnp.where(pos >= PAD_FRONT, y, 0.0)


def _outproj(m1, m2, w_out, h):
    bsz, p, d = h.shape
    k1 = m1.shape[-1]
    k2 = m2.shape[-1]
    tp = _pick_tile(p, (832, 640, 128))
    w = w_out.astype(BF16)
    return pl.pallas_call(
        functools.partial(_outproj_kernel, tp=tp),
        grid=(bsz, p // tp),
        in_specs=[
            pl.BlockSpec((None, tp, k1), lambda b, i: (b, i, 0)),
            pl.BlockSpec((None, tp, k2), lambda b, i: (b, i, 0)),
            pl.BlockSpec((k1, d), lambda b, i: (0, 0)),
            pl.BlockSpec((k2, d), lambda b, i: (0, 0)),
            pl.BlockSpec((None, tp, d), lambda b, i: (b, i, 0)),
        ],
        out_specs=pl.BlockSpec((None, tp, d), lambda b, i: (b, i, 0)),
        out_shape=jax.ShapeDtypeStruct((bsz, p, d), F32),
        compiler_params=_params(("parallel", "parallel")),
        name="outproj",
    )(m1, m2, w[:k1], w[k1:], h)


def _final_norm_kernel(x_ref, g_ref, o_ref):
    x = x_ref[...]
    ms = jnp.mean(x * x, axis=-1, keepdims=True)
    o_ref[...] = x * lax.rsqrt(ms + NORM_EPS) * g_ref[...]


def _final_norm(h, g, seq):
    bsz, p, d = h.shape
    skip = (p - seq) // LANES
    return pl.pallas_call(
        _final_norm_kernel,
        grid=(bsz, seq // LANES),
        in_specs=[
            pl.BlockSpec((None, LANES, d), lambda b, i: (b, i + skip, 0)),
            pl.BlockSpec((1, d), lambda b, i: (0, 0)),
        ],
        out_specs=pl.BlockSpec((None, LANES, d), lambda b, i: (b, i, 0)),
        out_shape=jax.ShapeDtypeStruct((bsz, seq, d), F32),
        compiler_params=_params(("parallel", "parallel")),
        name="final_norm",
    )(h, g.reshape(1, d))


def _dsa_kernel(aq_ref, ag_ref, iq_ref, iw_ref, k_ref, v_ref, ik_ref, bias_ref, o_ref,
                key_ref, pos_ref, mask_ref, sa_ref, sb_ref, iqs_ref, wts_ref, qm_ref,
                jdx_ref, mx_ref, ls_ref, acc_ref, *, topk):
    T = LANES
    i = pl.program_id(1)
    last = i + DUMMY

    def num_groups(group):
        return (i + group) // group

    def first_slot(group):
        return last + 1 - group * num_groups(group)

    row = lax.broadcasted_iota(I32, (T, T), 0)
    col = lax.broadcasted_iota(I32, (T, T), 1)
    qchunk = (i * T + row) >> 6
    low = col < 64

    def admissible(kt):
        kpos = kt * T + col
        return (kpos >= PAD_FRONT) & ((kpos >> 6) <= qchunk)

    def tile_rows(slot):
        return pl.ds(pl.multiple_of(jnp.maximum(slot - DUMMY, 0) * T, T), T)

    def for_groups(group, body):
        first = first_slot(group)

        def step(g, carry):
            body(first + group * g)
            return carry

        lax.fori_loop(0, num_groups(group), step, 0)

    iw = iw_ref[...].astype(F32)
    for h in range(IDX_HEADS):
        pair = iq_ref[:, (h // 2) * T:(h // 2 + 1) * T]
        mine = low if h % 2 == 0 else ~low
        iqs_ref[0, h * T:(h + 1) * T, :] = jnp.where(mine, pair, jnp.zeros_like(pair))
        wts_ref[h] = jnp.broadcast_to(iw[:, h:h + 1], (T, T))
    for h in range(A_HEADS):
        pair = aq_ref[:, (h // 2) * T:(h // 2 + 1) * T]
        mine = low if h % 2 == 0 else ~low
        qm_ref[h // 2, (h % 2) * T:(h % 2 + 1) * T, :] = jnp.where(
            mine, pair * (A_DIM ** -0.5), jnp.zeros_like(pair))

    def sweep(lhs_ref, rhs_ref, consume, consume_last):
        group_rows = lhs_ref.shape[1]

        def issue(s0, stage):
            rows = [tile_rows(s0), tile_rows(s0 + 1)]
            for g in range(lhs_ref.shape[0]):
                ps = slice(g * T, (g + 1) * T)
                kw = jnp.concatenate([rhs_ref[rows[0], ps], rhs_ref[rows[1], ps]], axis=0)
                stage[g * group_rows:(g + 1) * group_rows, :] = lax.dot_general(
                    lhs_ref[g], kw, NT_DIMS, preferred_element_type=F32)

        first = first_slot(4)
        issue(first, sa_ref)

        def body(q, carry):
            s = first + 4 * q
            issue(s + 2, sb_ref)
            consume(s, sa_ref)
            issue(s + 4, sa_ref)
            consume(s + 2, sb_ref)
            return carry

        lax.fori_loop(0, num_groups(4) - 1, body, 0)
        issue(last - 1, sb_ref)
        consume(last - 3, sa_ref)
        consume_last(last - 1, sb_ref)

    def score_pair(s0, stage):
        for u in range(2):
            tot = jnp.zeros((T, T), F32)
            for h in range(IDX_HEADS):
                tot = tot + jnp.maximum(stage[h * T:(h + 1) * T, u * T:(u + 1) * T], 0.0) * wts_ref[h]
            score = jnp.where(admissible(s0 + u - DUMMY), tot * IDX_SCALE, NEG_INF)
            bits = lax.bitcast_convert_type(score, I32)
            key_ref[s0 + u] = bits ^ ((bits >> 31) & 0x7FFFFFFF)

    sweep(iqs_ref, ik_ref, score_pair, score_pair)
    for d in range(DUMMY):
        key_ref[d] = jnp.full((T, T), I32_MIN, I32)

    def row_count(pred):
        one = jnp.ones((T, T), I32)
        zero = jnp.zeros((T, T), I32)
        first = first_slot(8)

        def body(g, acc):
            hits = [jnp.where(pred(first + 8 * g + u), one, zero) for u in range(8)]
            while len(hits) > 1:
                hits = [a + b for a, b in zip(hits[::2], hits[1::2])]
            return acc + hits[0]

        acc = lax.fori_loop(0, num_groups(8), body, zero)
        return jnp.sum(acc.astype(F32), axis=-1, keepdims=True)

    def value_bit(it, t):
        cand = t + (jnp.int32(1) << (31 - it))
        wide = jnp.broadcast_to(cand, (T, T))
        cnt = row_count(lambda s: key_ref[s] >= wide)
        return jnp.where(cnt >= topk, cand, t)

    thr = jnp.broadcast_to(lax.fori_loop(0, 32, value_bit, jnp.full((T, 1), I32_MIN, I32)), (T, T))

    need = topk - row_count(lambda s: key_ref[s] > thr)
    ties = row_count(lambda s: key_ref[s] == thr)
    jdx_ref[...] = jnp.full((T, T), 2 ** IDX_BITS, I32)

    @pl.when(jnp.max(jnp.where(ties > need, 1.0, 0.0)) > 0.0)
    def _():
        def tie_positions(s0):
            for s in [s0 + u for u in range(4)]:
                pos_ref[s] = jnp.where(key_ref[s] == thr, (s - DUMMY) * T + col, I32_MAX)

        for_groups(4, tie_positions)
        for d in range(DUMMY):
            pos_ref[d] = jnp.full((T, T), I32_MAX, I32)

        def index_bit(bit, j):
            cand = j + (jnp.int32(1) << bit)
            wide = jnp.broadcast_to(cand, (T, T))
            cnt = row_count(lambda s: pos_ref[s] < wide)
            return jnp.where(cnt < need, cand, j)

        nbits = 7 + sum((i >= (1 << k)).astype(I32) for k in range(7))
        j = lax.fori_loop(0, nbits, lambda it, j: index_bit(nbits - 1 - it, j),
                          jnp.zeros((T, 1), I32))
        jdx_ref[...] = jnp.broadcast_to(j, (T, T))

    jdx = jdx_ref[...]

    def to_mask(s0):
        for s in [s0 + u for u in range(4)]:
            key = key_ref[s]
            sel = (key > thr) | ((key == thr) & ((s - DUMMY) * T + col <= jdx))
            mask_ref[s] = jnp.where(sel & admissible(s - DUMMY), 0.0, NEG_INF)

    for_groups(4, to_mask)

    mx_ref[...] = jnp.full(mx_ref.shape, NEG_INF, F32)
    ls_ref[...] = jnp.zeros(ls_ref.shape, F32)
    acc_ref[...] = jnp.zeros(acc_ref.shape, F32)

    def attend(s0, stage, near, final):
        madd = [mask_ref[s0], mask_ref[s0 + 1]]
        for hp in range(A_HEADS // 2):
            weights = []
            for h in (2 * hp, 2 * hp + 1):
                parts = [stage[h * T:(h + 1) * T, u * T:(u + 1) * T] + madd[u] for u in range(2)]
                if near:
                    parts = [parts[u] + bias_ref[h, 1 - u] for u in range(2)]
                if not final:
                    mx_ref[h] = jnp.maximum(mx_ref[h], jnp.maximum(parts[0], parts[1]))
                else:
                    es = [jnp.exp(part - mx_ref[h]) for part in parts]
                    ls_ref[h] = ls_ref[h] + (es[0] + es[1])
                    weights.append(jnp.concatenate(es, axis=1).astype(BF16))
            if final:
                ps = slice(hp * T, (hp + 1) * T)
                vw = jnp.concatenate([v_ref[tile_rows(s0), ps], v_ref[tile_rows(s0 + 1), ps]], axis=0)
                acc_ref[hp] = acc_ref[hp] + jnp.dot(jnp.concatenate(weights, axis=0), vw,
                                                    preferred_element_type=F32)

    for final in (False, True):
        sweep(qm_ref, k_ref,
              functools.partial(attend, near=False, final=final),
              functools.partial(attend, near=True, final=final))
        if not final:
            for h in range(A_HEADS):
                mx_ref[h] = jnp.broadcast_to(jnp.max(mx_ref[h], axis=-1, keepdims=True), (T, T))

    for hp in range(A_HEADS // 2):
        o0 = acc_ref[hp, :T, :] / jnp.sum(ls_ref[2 * hp], axis=-1, keepdims=True)
        o1 = acc_ref[hp, T:, :] / jnp.sum(ls_ref[2 * hp + 1], axis=-1, keepdims=True)
        g = ag_ref[:, hp * T:(hp + 1) * T].astype(F32)
        o_ref[:, hp * T:(hp + 1) * T] = (jnp.where(low, o0, o1) * _silu(g)).astype(o_ref.dtype)


def _dsa(pab, bias_tiles, topk):
    bsz, p, _ = pab.shape
    T = LANES
    nq = p // T
    hw = A_HEADS * A_DIM
    one = pl.Buffered(1)
    return pl.pallas_call(
        functools.partial(_dsa_kernel, topk=float(topk)),
        grid=(bsz, nq),
        in_specs=[
            pl.BlockSpec((None, T, hw), lambda b, i: (b, i, 0)),
            pl.BlockSpec((None, T, hw), lambda b, i: (b, i, 3)),
            pl.BlockSpec((None, T, hw), lambda b, i: (b, i, 4)),
            pl.BlockSpec((None, T, T), lambda b, i: (b, i, 33)),
            pl.BlockSpec((None, p, hw), lambda b, i: (b, 0, 1), pipeline_mode=one),
            pl.BlockSpec((None, p, hw), lambda b, i: (b, 0, 2), pipeline_mode=one),
            pl.BlockSpec((None, p, T), lambda b, i: (b, 0, 32), pipeline_mode=one),
            pl.BlockSpec((A_HEADS, 2, T, T), lambda b, i: (0, 0, 0, 0), pipeline_mode=one),
        ],
        out_specs=pl.BlockSpec((None, T, hw), lambda b, i: (b, i, 0)),
        out_shape=jax.ShapeDtypeStruct((bsz, p, hw), BF16),
        scratch_shapes=[
            pltpu.VMEM((nq + DUMMY, T, T), I32),
            pltpu.VMEM((nq + DUMMY, T, T), I32),
            pltpu.VMEM((nq + DUMMY, T, T), F32),
            pltpu.VMEM((A_HEADS * T, 2 * T), F32),
            pltpu.VMEM((A_HEADS * T, 2 * T), F32),
            pltpu.VMEM((1, IDX_HEADS * T, T), BF16),
            pltpu.VMEM((IDX_HEADS, T, T), F32),
            pltpu.VMEM((A_HEADS // 2, 2 * T, T), BF16),
            pltpu.VMEM((T, T), I32),
            pltpu.VMEM((A_HEADS, T, T), F32),
            pltpu.VMEM((A_HEADS, T, T), F32),
            pltpu.VMEM((A_HEADS // 2, 2 * T, T), F32),
        ],
        compiler_params=_params(("parallel", "arbitrary")),
        name="dsa",
    )(pab, pab, pab, pab, pab, pab, pab, bias_tiles[:, :2])


def _gla_kernel(q_ref, k_ref, v_ref, g_ref, a_ref, w2_ref, gb_ref, o_ref, st_ref, *, nchunk):
    C = CHUNK
    T = LANES

    @pl.when(pl.program_id(1) == 0)
    def _():
        st_ref[...] = jnp.zeros(st_ref.shape, F32)

    r_i = lax.broadcasted_iota(I32, (C, C), 0)
    c_i = lax.broadcasted_iota(I32, (C, C), 1)
    causal = c_i <= r_i
    tri = jnp.where(causal, 1.0, 0.0).astype(BF16)
    lane = lax.broadcasted_iota(I32, (C, T), 1)
    sd_r = lax.broadcasted_iota(I32, (2 * B_DV, T), 0)
    sd_c = lax.broadcasted_iota(I32, (2 * B_DV, T), 1)
    blockdiag = (sd_r >= B_DV) == (sd_c >= B_DK)

    def chunk(c, carry):
        r = pl.ds(pl.multiple_of(c * C, C), C)
        x = jnp.dot(a_ref[r, :], w2_ref[...], preferred_element_type=F32) + gb_ref[...]
        log_a = _softplus_parts(x)[1] * (1.0 / GLA_TAU)
        a1, a2, a3 = _split3(log_a)
        bcum = (jnp.dot(tri, a1, preferred_element_type=F32)
                + jnp.dot(tri, a2, preferred_element_type=F32)
                + jnp.dot(tri, a3, preferred_element_type=F32))
        b_last = bcum[C - 1:C, :]
        q = q_ref[r, :].astype(F32) * (B_DK ** -0.5)
        k = k_ref[r, :].astype(F32)
        q_t = (q * jnp.exp(bcum)).astype(BF16)
        k_t = (k * jnp.exp(-bcum)).astype(BF16)
        k_d = (k * jnp.exp(b_last - bcum)).astype(BF16)
        decay = jnp.exp(b_last)
        for hp in range(B_HEADS // 2):
            ls = slice(hp * T, (hp + 1) * T)
            st = st_ref[hp]
            o_inter = lax.dot_general(q_t[:, ls], st.astype(BF16), NT_DIMS,
                                      preferred_element_type=F32)
            for hh in range(2):
                h = 2 * hp + hh
                mine = (lane < B_DK) if hh == 0 else (lane >= B_DK)
                qm = jnp.where(mine, q_t[:, ls], jnp.zeros((C, T), BF16))
                att = lax.dot_general(qm, k_t[:, ls], NT_DIMS, preferred_element_type=F32)
                att = jnp.where(causal, att, 0.0)
                vs = slice(h * B_DV, (h + 1) * B_DV)
                o = jnp.dot(att.astype(BF16), v_ref[r, vs], preferred_element_type=F32)
                o = o + o_inter[:, hh * B_DV:(hh + 1) * B_DV]
                o = o * lax.rsqrt(jnp.mean(o * o, axis=-1, keepdims=True) + NORM_EPS)
                o = o * _silu(g_ref[r, vs].astype(F32))
                o_ref[r, vs] = o.astype(o_ref.dtype)
            vp = v_ref[r, hp * 2 * B_DV:(hp + 1) * 2 * B_DV]
            contrib = lax.dot_general(vp, k_d[:, ls], TN_DIMS, preferred_element_type=F32)
            st_ref[hp] = st * decay[:, ls] + jnp.where(blockdiag, contrib, 0.0)
        return carry

    lax.fori_loop(0, nchunk, chunk, 0)


def _gla(pab, w2, gb):
    bsz, p, _ = pab.shape
    tc = _pick_tile(p, (640, 128))
    hv = B_HEADS * B_DV
    hk = B_HEADS * B_DK
    return pl.pallas_call(
        functools.partial(_gla_kernel, nchunk=tc // CHUNK),
        grid=(bsz, p // tc),
        in_specs=[
            pl.BlockSpec((None, tc, hk), lambda b, c: (b, c, 14)),
            pl.BlockSpec((None, tc, hk), lambda b, c: (b, c, 15)),
            pl.BlockSpec((None, tc, hv), lambda b, c: (b, c, 5)),
            pl.BlockSpec((None, tc, hv), lambda b, c: (b, c, 6)),
            pl.BlockSpec((None, tc, LANES), lambda b, c: (b, c, 34)),
            pl.BlockSpec((LANES, hk), lambda b, c: (0, 0)),
            pl.BlockSpec((1, hk), lambda b, c: (0, 0)),
        ],
        out_specs=pl.BlockSpec((None, tc, hv), lambda b, c: (b, c, 0)),
        out_shape=jax.ShapeDtypeStruct((bsz, p, hv), BF16),
        scratch_shapes=[pltpu.VMEM((B_HEADS // 2, 2 * B_DV, 2 * B_DK), F32)],
        compiler_params=_params(("parallel", "arbitrary")),
        name="gla",
    )(pab, pab, pab, pab, pab, w2, gb)


def _ret_kernel(q_ref, k_ref, v_ref, g_ref, cos_ref, sin_ref, dmat_ref, zeta_ref, xi_ref,
                cdec_ref, o_ref, st_ref, *, nchunk):
    C = CHUNK
    T = LANES
    W = C_HEADS * C_DK

    @pl.when(pl.program_id(1) == 0)
    def _():
        st_ref[...] = jnp.zeros(st_ref.shape, F32)

    lane_w = lax.broadcasted_iota(I32, (C, W), 1)
    first_half = (lane_w & (C_DK - 1)) < (C_DK // 2)
    lane = lax.broadcasted_iota(I32, (C, T), 1)
    sd_r = lax.broadcasted_iota(I32, (2 * C_DV, T), 0)
    sd_c = lax.broadcasted_iota(I32, (2 * C_DV, T), 1)
    blockdiag = (sd_r >= C_DV) == (sd_c >= C_DK)

    def rotate(x, cos, sin_signed):
        swapped = jnp.where(first_half, pltpu.roll(x, W - C_DK // 2, 1), pltpu.roll(x, C_DK // 2, 1))
        return x * cos + swapped * sin_signed

    def chunk(c, carry):
        r = pl.ds(pl.multiple_of(c * C, C), C)
        cos = cos_ref[r, :]
        sin = sin_ref[r, :]
        q = rotate(q_ref[r, :].astype(F32), cos, sin)
        k = rotate(k_ref[r, :].astype(F32), cos, sin) * (C_DK ** -0.5)
        q_b = q.astype(BF16)
        k_b = k.astype(BF16)
        q_x = (q * xi_ref[...]).astype(BF16)
        k_z = (k * zeta_ref[...]).astype(BF16)
        for hp in range(C_HEADS // 2):
            ls = slice(hp * T, (hp + 1) * T)
            st = st_ref[hp]
            o_inter = lax.dot_general(q_x[:, ls], st.astype(BF16), NT_DIMS,
                                      preferred_element_type=F32)
            for hh in range(2):
                h = 2 * hp + hh
                mine = (lane < C_DK) if hh == 0 else (lane >= C_DK)
                qm = jnp.where(mine, q_b[:, ls], jnp.zeros((C, T), BF16))
                att = lax.dot_general(qm, k_b[:, ls], NT_DIMS, preferred_element_type=F32)
                att = att * dmat_ref[h]
                vs = slice(h * C_DV, (h + 1) * C_DV)
                o = jnp.dot(att.astype(BF16), v_ref[r, vs], preferred_element_type=F32)
                o = o + o_inter[:, hh * C_DV:(hh + 1) * C_DV]
                o = o - jnp.mean(o, axis=-1, keepdims=True)
                o = o * lax.rsqrt(jnp.mean(o * o, axis=-1, keepdims=True) + NORM_EPS)
                o = o * _silu(g_ref[r, vs].astype(F32))
                o_ref[r, vs] = o.astype(o_ref.dtype)
            vp = v_ref[r, hp * 2 * C_DV:(hp + 1) * 2 * C_DV]
            contrib = lax.dot_general(vp, k_z[:, ls], TN_DIMS, preferred_element_type=F32)
            st_ref[hp] = st * cdec_ref[:, ls] + jnp.where(blockdiag, contrib, 0.0)
        return carry

    lax.fori_loop(0, nchunk, chunk, 0)


def _retention_tables(p):
    log_gamma = np.log(1.0 - np.exp2(-5.0 - np.arange(C_HEADS, dtype=np.float64)))
    i = np.arange(CHUNK, dtype=np.float64)
    diff = i[:, None] - i[None, :]
    dmat = np.where(diff >= 0, np.exp(log_gamma[:, None, None] * np.maximum(diff, 0.0)), 0.0)
    zeta = np.exp(log_gamma[:, None] * (CHUNK - 1 - i))
    xi = np.exp(log_gamma[:, None] * (i + 1))
    cdec = np.exp(log_gamma * CHUNK)
    widen = lambda t: np.repeat(t.T[:, :, None], C_DK, axis=2).reshape(CHUNK, C_HEADS * C_DK)
    half = C_DK // 2
    inv = jnp.asarray(ROPE_BASE, F32) ** (-jnp.arange(half, dtype=F32) / half)
    ang = jnp.arange(p, dtype=jnp.int32).astype(F32)[:, None] * inv[None, :]
    cos = jnp.tile(jnp.cos(ang), (1, 2 * C_HEADS))
    sin = jnp.sin(ang)
    sin_signed = jnp.tile(jnp.concatenate([-sin, sin], axis=1), (1, C_HEADS))
    return (cos, sin_signed, jnp.asarray(dmat, F32), jnp.asarray(widen(zeta), F32),
            jnp.asarray(widen(xi), F32),
            jnp.asarray(np.repeat(cdec, C_DK)[None, :], F32))


def _retention(pcd, tables):
    bsz, p, _ = pcd.shape
    cos, sin, dmat, zeta, xi, cdec = tables
    tc = _pick_tile(p, (640, 128))
    hv = C_HEADS * C_DV
    hk = C_HEADS * C_DK
    full = lambda shape: pl.BlockSpec(shape, lambda b, c: (0,) * len(shape))
    return pl.pallas_call(
        functools.partial(_ret_kernel, nchunk=tc // CHUNK),
        grid=(bsz, p // tc),
        in_specs=[
            pl.BlockSpec((None, tc, hk), lambda b, c: (b, c, 0)),
            pl.BlockSpec((None, tc, hk), lambda b, c: (b, c, 1)),
            pl.BlockSpec((None, tc, hv), lambda b, c: (b, c, 1)),
            pl.BlockSpec((None, tc, hv), lambda b, c: (b, c, 2)),
            pl.BlockSpec((tc, hk), lambda b, c: (c, 0)),
            pl.BlockSpec((tc, hk), lambda b, c: (c, 0)),
            full((C_HEADS, CHUNK, CHUNK)),
            full((CHUNK, hk)),
            full((CHUNK, hk)),
            full((1, hk)),
        ],
        out_specs=pl.BlockSpec((None, tc, hv), lambda b, c: (b, c, 0)),
        out_shape=jax.ShapeDtypeStruct((bsz, p, hv), BF16),
        scratch_shapes=[pltpu.VMEM((C_HEADS // 2, 2 * C_DV, 2 * C_DK), F32)],
        compiler_params=_params(("parallel", "arbitrary")),
        name="retention",
    )(pcd, pcd, pcd, pcd, cos, sin, dmat, zeta, xi, cdec)


def _sb_kernel(q_ref, g_ref, k_ref, v_ref, o_ref, qm_ref, run_ref, acc_ref, hi_ref, lo_ref, ls_ref,
               w_ref):
    T = LANES
    i = pl.program_id(1)
    row = lax.broadcasted_iota(I32, (T, T), 0)
    col = lax.broadcasted_iota(I32, (T, T), 1)
    qpos = i * T + row
    low = col < D_DIM
    u_r = lax.broadcasted_iota(I32, (T, 2 * T), 0)
    u_c = lax.broadcasted_iota(I32, (T, 2 * T), 1)
    suffix = jnp.where((u_c >= T) | (u_r > u_c), 1.0, 0.0).astype(BF16)

    for h in range(D_HEADS):
        pair = q_ref[:, (h // 2) * T:(h // 2 + 1) * T]
        mine = low if h % 2 == 0 else ~low
        qm_ref[h] = jnp.where(mine, pair * (D_DIM ** -0.5), jnp.zeros_like(pair))
    run_ref[...] = jnp.zeros(run_ref.shape, F32)
    acc_ref[...] = jnp.zeros(acc_ref.shape, F32)

    def tile(state):
        t, _ = state
        kt = i - t
        ks = pl.ds(pl.multiple_of(kt * T, T), T)
        kpos = kt * T + col
        ok = (kpos < qpos) & (kpos >= PAD_FRONT)
        for h in range(D_HEADS):
            ps = slice((h // 2) * T, (h // 2 + 1) * T)
            z = lax.dot_general(qm_ref[h], k_ref[ks, ps], NT_DIMS, preferred_element_type=F32)
            sp, logsig = _softplus_parts(z)
            log_1m = jnp.where(ok, -sp, 0.0)
            hi = log_1m.astype(BF16)
            hi_ref[h] = hi
            lo_ref[h] = (log_1m - hi.astype(F32)).astype(BF16)
            ls_ref[h] = jnp.where(ok, logsig, NEG_INF)
        slowest = None
        for h in range(D_HEADS):
            sums = (jnp.dot(hi_ref[h], suffix, preferred_element_type=F32)
                    + jnp.dot(lo_ref[h], suffix, preferred_element_type=F32))
            run = run_ref[h]
            w_ref[h] = jnp.exp(ls_ref[h] + run + sums[:, :T]).astype(BF16)
            run = run + sums[:, T:]
            run_ref[h] = run
            slowest = run if slowest is None else jnp.maximum(slowest, run)
        for h in range(D_HEADS):
            ps = slice((h // 2) * T, (h // 2 + 1) * T)
            acc_ref[h] = acc_ref[h] + jnp.dot(w_ref[h], v_ref[ks, ps],
                                              preferred_element_type=F32)
        return t + 1, (jnp.max(slowest) > LOG_F32_UNDERFLOW).astype(I32)

    lax.while_loop(lambda s: (s[0] <= i) & (s[1] > 0), tile, (jnp.int32(0), jnp.int32(1)))

    for hp in range(D_HEADS // 2):
        ps = slice(hp * T, (hp + 1) * T)
        o = jnp.where(low, acc_ref[2 * hp], acc_ref[2 * hp + 1]) * _silu(g_ref[:, ps].astype(F32))
        o_ref[:, ps] = o.astype(o_ref.dtype)


def _stick_breaking(pcd):
    bsz, p, _ = pcd.shape
    T = LANES
    hw = D_HEADS * D_DIM
    one = pl.Buffered(1)
    return pl.pallas_call(
        _sb_kernel,
        grid=(bsz, p // T),
        in_specs=[
            pl.BlockSpec((None, T, hw), lambda b, i: (b, i, 3)),
            pl.BlockSpec((None, T, hw), lambda b, i: (b, i, 6)),
            pl.BlockSpec((None, p, hw), lambda b, i: (b, 0, 4), pipeline_mode=one),
            pl.BlockSpec((None, p, hw), lambda b, i: (b, 0, 5), pipeline_mode=one),
        ],
        out_specs=pl.BlockSpec((None, T, hw), lambda b, i: (b, i, 0)),
        out_shape=jax.ShapeDtypeStruct((bsz, p, hw), BF16),
        scratch_shapes=[
            pltpu.VMEM((D_HEADS, T, T), BF16),
            pltpu.VMEM((D_HEADS, T, T), F32),
            pltpu.VMEM((D_HEADS, T, T), F32),
            pltpu.VMEM((D_HEADS, T, T), BF16),
            pltpu.VMEM((D_HEADS, T, T), BF16),
            pltpu.VMEM((D_HEADS, T, T), F32),
            pltpu.VMEM((D_HEADS, T, T), BF16),
        ],
        compiler_params=_params(("parallel", "arbitrary")),
        name="stick_breaking",
    )(pcd, pcd, pcd, pcd)


def _rel_bucket_np(rel):
    half = REL_BUCKETS // 2
    max_exact = half // 2
    n = -rel
    ret = np.where(n < 0, half, 0)
    n = np.abs(n)
    edges = [math.ceil(max_exact * (REL_MAX_DIST / max_exact) ** (j / (half - max_exact)) - 1e-9)
             for j in range(1, half - max_exact)]
    large = max_exact + sum((n >= e).astype(np.int64) for e in edges)
    return ret + np.where(n < max_exact, n, large)


def _bias_tiles(rel_bias):
    i = np.arange(LANES)[:, None]
    j = np.arange(LANES)[None, :]
    idx = np.stack([_rel_bucket_np(j - i - LANES * d) for d in range(3)])
    assert (idx[2] == idx[2, 0, 0]).all()
    table = rel_bias.astype(F32)
    tiles = table[idx] - table[idx[2, 0, 0]]
    return jnp.transpose(tiles, (3, 0, 1, 2))


def _layout_ab(w):
    aq, ak, av, ag, iq, ik, iw, bq, bk, bv, bg, ba = jnp.split(w, np.cumsum(SPLIT_AB)[:-1].tolist(), axis=1)
    pad = lambda t: jnp.pad(t, ((0, 0), (0, LANES - t.shape[1])))
    cols = [aq, ak, av, ag, iq, bv, bg, bq, bk, jnp.concatenate([ik, ik], axis=1), pad(iw), pad(ba)]
    return jnp.concatenate(cols, axis=1).astype(BF16)


def kernel(x, meta_tokens, rel_bias, norm_g, final_g, w_in_ab, gla_gate_w2, gla_gate_b, w_out_ab,
           w_in_cd, w_out_cd):
    bsz, seq, d = x.shape
    p = seq + PAD_FRONT + N_META
    depth = norm_g.shape[0]
    topk = min(TOPK_MAX, seq // 4)
    h = jnp.concatenate([jnp.zeros((bsz, PAD_FRONT, d), x.dtype),
                         jnp.broadcast_to(meta_tokens.astype(x.dtype)[None], (bsz, N_META, d)),
                         x], axis=1)
    bias_tiles = _bias_tiles(rel_bias)
    tables = _retention_tables(p)
    for layer in range(depth):
        j = layer // 2
        if layer % 2 == 0:
            pab = _inproj(h, norm_g[layer], _layout_ab(w_in_ab[j]))
            oa = _dsa(pab, bias_tiles, topk)
            w2 = jnp.pad(gla_gate_w2[j], ((0, LANES - GLA_GATE_RANK), (0, 0))).astype(BF16)
            ob = _gla(pab, w2, gla_gate_b[j].reshape(1, -1).astype(F32))
            h = _outproj(oa, ob, w_out_ab[j], h)
        else:
            pcd = _inproj(h, norm_g[layer], w_in_cd[j].astype(BF16))
            oc = _retention(pcd, tables)
            od = _stick_breaking(pcd)
            h = _outproj(oc, od, w_out_cd[j], h)
    return _final_norm(h, final_g, seq)
```

```python
import functools
import math

import numpy as np
import jax
import jax.numpy as jnp
from jax import lax
from jax.experimental import pallas as pl
from jax.experimental.pallas import tpu as pltpu

F32 = jnp.float32
BF16 = jnp.bfloat16
I32 = jnp.int32

D_MODEL = 1024
CHUNK = 64
N_META = 16
PAD_FRONT = 128 - N_META
NORM_EPS = 1e-6
NEG_INF = -1e30
A_HEADS, A_DIM = 8, 64
IDX_HEADS, IDX_DIM = 8, 64
IDX_SCALE = (IDX_DIM ** -0.5) * (IDX_HEADS ** -0.5)
TOPK_MAX = 256
B_HEADS, B_DK, B_DV = 4, 64, 128
GLA_GATE_RANK = 16
GLA_TAU = 16.0
C_HEADS, C_DK, C_DV = 4, 64, 128
ROPE_BASE = 10000.0
D_HEADS, D_DIM = 8, 64
REL_BUCKETS = 32
REL_MAX_DIST = 128
SPLIT_AB = (512, 512, 512, 512, 512, 64, 8, 256, 256, 512, 512, 16)
SPLIT_CD = (256, 256, 512, 512, 512, 512, 512, 512)

LANES = 128
W_AB_PAD = 4480
W_CD = 3584
I32_MIN = -2 ** 31
I32_MAX = 2 ** 31 - 1
DUMMY = 7
IDX_BITS = 14
LOG_F32_UNDERFLOW = -104.0
LOGIT_LIMIT = 40.0
NORM_SLACK = 1.05
TINY = 1e-30
VMEM_LIMIT = 56 * 1024 * 1024

NT_DIMS = (((1,), (1,)), ((), ()))
TN_DIMS = (((0,), (0,)), ((), ()))


def _pick_tile(n, candidates):
    for c in candidates:
        if n % c == 0:
            return c
    raise ValueError(f"no tile for {n}")


def _params(sem):
    return pltpu.CompilerParams(dimension_semantics=sem, vmem_limit_bytes=VMEM_LIMIT)


def _silu(x):
    return x / (1.0 + jnp.exp(-x))


def _softplus_parts(z):
    t = jnp.log1p(jnp.exp(-jnp.abs(z)))
    return jnp.maximum(z, 0.0) + t, jnp.minimum(z, 0.0) - t


def _split3(x):
    a = x.astype(BF16)
    r = x - a.astype(F32)
    b = r.astype(BF16)
    c = (r - b.astype(F32)).astype(BF16)
    return a, b, c


def _inproj_kernel(x_ref, g_ref, w_ref, o_ref, hn_ref):
    @pl.when(pl.program_id(2) == 0)
    def _():
        x = x_ref[...]
        ms = jnp.mean(x * x, axis=-1, keepdims=True)
        hn_ref[...] = (x * lax.rsqrt(ms + NORM_EPS) * g_ref[...]).astype(BF16)

    o_ref[...] = jnp.dot(hn_ref[...], w_ref[...], preferred_element_type=F32).astype(o_ref.dtype)


def _inproj(h, g, w):
    bsz, p, d = h.shape
    n = w.shape[1]
    tp = _pick_tile(p, (1664, 1280, 640, 128))
    tn = _pick_tile(n, (896, 512, 128))
    return pl.pallas_call(
        _inproj_kernel,
        grid=(bsz, p // tp, n // tn),
        in_specs=[
            pl.BlockSpec((None, tp, d), lambda b, i, j: (b, i, 0)),
            pl.BlockSpec((1, d), lambda b, i, j: (0, 0)),
            pl.BlockSpec((d, tn), lambda b, i, j: (0, j)),
        ],
        out_specs=pl.BlockSpec((None, tp, tn), lambda b, i, j: (b, i, j)),
        out_shape=jax.ShapeDtypeStruct((bsz, p, n), BF16),
        scratch_shapes=[pltpu.VMEM((tp, d), BF16)],
        compiler_params=_params(("parallel", "parallel", "arbitrary")),
        name="inproj",
    )(h, g.reshape(1, d), w)


def _outproj_kernel(m1_ref, m2_ref, w1_ref, w2_ref, h_ref, o_ref, *, tp):
    y = jnp.dot(m1_ref[...], w1_ref[...], preferred_element_type=F32)
    y = y + jnp.dot(m2_ref[...], w2_ref[...], preferred_element_type=F32)
    pos = pl.program_id(1) * tp + lax.broadcasted_iota(I32, y.shape, 0)
    o_ref[...] = h_ref[...] + jnp.where(pos >= PAD_FRONT, y, 0.0)


def _outproj(m1, m2, w_out, h):
    bsz, p, d = h.shape
    k1 = m1.shape[-1]
    k2 = m2.shape[-1]
    tp = _pick_tile(p, (832, 640, 128))
    w = w_out.astype(BF16)
    return pl.pallas_call(
        functools.partial(_outproj_kernel, tp=tp),
        grid=(bsz, p // tp),
        in_specs=[
            pl.BlockSpec((None, tp, k1), lambda b, i: (b, i, 0)),
            pl.BlockSpec((None, tp, k2), lambda b, i: (b, i, 0)),
            pl.BlockSpec((k1, d), lambda b, i: (0, 0)),
            pl.BlockSpec((k2, d), lambda b, i: (0, 0)),
            pl.BlockSpec((None, tp, d), lambda b, i: (b, i, 0)),
        ],
        out_specs=pl.BlockSpec((None, tp, d), lambda b, i: (b, i, 0)),
        out_shape=jax.ShapeDtypeStruct((bsz, p, d), F32),
        compiler_params=_params(("parallel", "parallel")),
        name="outproj",
    )(m1, m2, w[:k1], w[k1:], h)


def _final_norm_kernel(x_ref, g_ref, o_ref):
    x = x_ref[...]
    ms = jnp.mean(x * x, axis=-1, keepdims=True)
    o_ref[...] = x * lax.rsqrt(ms + NORM_EPS) * g_ref[...]


def _final_norm(h, g, seq):
    bsz, p, d = h.shape
    skip = (p - seq) // LANES
    return pl.pallas_call(
        _final_norm_kernel,
        grid=(bsz, seq // LANES),
        in_specs=[
            pl.BlockSpec((None, LANES, d), lambda b, i: (b, i + skip, 0)),
            pl.BlockSpec((1, d), lambda b, i: (0, 0)),
        ],
        out_specs=pl.BlockSpec((None, LANES, d), lambda b, i: (b, i, 0)),
        out_shape=jax.ShapeDtypeStruct((bsz, seq, d), F32),
        compiler_params=_params(("parallel", "parallel")),
        name="final_norm",
    )(h, g.reshape(1, d))


def _dsa_kernel(aq_ref, ag_ref, iq_ref, iw_ref, k_ref, v_ref, ik_ref, bias_ref, o_ref,
                key_ref, pos_ref, mask_ref, sa_ref, sb_ref, iqs_ref, wts_ref, qm_ref,
                jdx_ref, kmax_ref, mx_ref, ls_ref, acc_ref, *, topk):
    T = LANES
    i = pl.program_id(1)
    last = i + DUMMY

    def num_groups(group):
        return (i + group) // group

    def first_slot(group):
        return last + 1 - group * num_groups(group)

    row = lax.broadcasted_iota(I32, (T, T), 0)
    col = lax.broadcasted_iota(I32, (T, T), 1)
    qchunk = (i * T + row) >> 6
    low = col < 64

    def admissible(kt):
        kpos = kt * T + col
        return (kpos >= PAD_FRONT) & ((kpos >> 6) <= qchunk)

    def tile_rows(slot):
        return pl.ds(pl.multiple_of(jnp.maximum(slot - DUMMY, 0) * T, T), T)

    def for_groups(group, body):
        first = first_slot(group)

        def step(g, carry):
            body(first + group * g)
            return carry

        lax.fori_loop(0, num_groups(group), step, 0)

    iw = iw_ref[...].astype(F32)
    for h in range(IDX_HEADS):
        pair = iq_ref[:, (h // 2) * T:(h // 2 + 1) * T]
        mine = low if h % 2 == 0 else ~low
        iqs_ref[0, h * T:(h + 1) * T, :] = jnp.where(mine, pair, jnp.zeros_like(pair))
        wts_ref[h] = jnp.broadcast_to(iw[:, h:h + 1], (T, T))
    for h in range(A_HEADS):
        pair = aq_ref[:, (h // 2) * T:(h // 2 + 1) * T]
        mine = low if h % 2 == 0 else ~low
        qm_ref[h // 2, (h % 2) * T:(h % 2 + 1) * T, :] = jnp.where(
            mine, pair * (A_DIM ** -0.5), jnp.zeros_like(pair))

    def sweep(lhs_ref, rhs_ref, consume, consume_last):
        group_rows = lhs_ref.shape[1]

        def issue(s0, stage):
            rows = [tile_rows(s0), tile_rows(s0 + 1)]
            for g in range(lhs_ref.shape[0]):
                ps = slice(g * T, (g + 1) * T)
                kw = jnp.concatenate([rhs_ref[rows[0], ps], rhs_ref[rows[1], ps]], axis=0)
                stage[g * group_rows:(g + 1) * group_rows, :] = lax.dot_general(
                    lhs_ref[g], kw, NT_DIMS, preferred_element_type=F32)

        first = first_slot(4)
        issue(first, sa_ref)

        def body(q, carry):
            s = first + 4 * q
            issue(s + 2, sb_ref)
            consume(s, sa_ref)
            issue(s + 4, sa_ref)
            consume(s + 2, sb_ref)
            return carry

        lax.fori_loop(0, num_groups(4) - 1, body, 0)
        issue(last - 1, sb_ref)
        consume(last - 3, sa_ref)
        consume_last(last - 1, sb_ref)

    def score_pair(s0, stage):
        for u in range(2):
            tot = jnp.zeros((T, T), F32)
            for h in range(IDX_HEADS):
                tot = tot + jnp.maximum(stage[h * T:(h + 1) * T, u * T:(u + 1) * T], 0.0) * wts_ref[h]
            score = jnp.where(admissible(s0 + u - DUMMY), tot * IDX_SCALE, NEG_INF)
            bits = lax.bitcast_convert_type(score, I32)
            key_ref[s0 + u] = bits ^ ((bits >> 31) & 0x7FFFFFFF)

    sweep(iqs_ref, ik_ref, score_pair, score_pair)
    for d in range(DUMMY):
        key_ref[d] = jnp.full((T, T), I32_MIN, I32)

    def row_count(pred):
        one = jnp.ones((T, T), I32)
        zero = jnp.zeros((T, T), I32)
        first = first_slot(8)

        def body(g, acc):
            hits = [jnp.where(pred(first + 8 * g + u), one, zero) for u in range(8)]
            while len(hits) > 1:
                hits = [a + b for a, b in zip(hits[::2], hits[1::2])]
            return acc + hits[0]

        acc = lax.fori_loop(0, num_groups(8), body, zero)
        return jnp.sum(acc.astype(F32), axis=-1, keepdims=True)

    def value_bit(it, t):
        cand = t + (jnp.int32(1) << (31 - it))
        wide = jnp.broadcast_to(cand, (T, T))
        cnt = row_count(lambda s: key_ref[s] >= wide)
        return jnp.where(cnt >= topk, cand, t)

    thr = jnp.broadcast_to(lax.fori_loop(0, 32, value_bit, jnp.full((T, 1), I32_MIN, I32)), (T, T))

    need = topk - row_count(lambda s: key_ref[s] > thr)
    ties = row_count(lambda s: key_ref[s] == thr)
    jdx_ref[...] = jnp.full((T, T), 2 ** IDX_BITS, I32)

    @pl.when(jnp.max(jnp.where(ties > need, 1.0, 0.0)) > 0.0)
    def _():
        def tie_positions(s0):
            for s in [s0 + u for u in range(4)]:
                pos_ref[s] = jnp.where(key_ref[s] == thr, (s - DUMMY) * T + col, I32_MAX)

        for_groups(4, tie_positions)
        for d in range(DUMMY):
            pos_ref[d] = jnp.full((T, T), I32_MAX, I32)

        def index_bit(bit, j):
            cand = j + (jnp.int32(1) << bit)
            wide = jnp.broadcast_to(cand, (T, T))
            cnt = row_count(lambda s: pos_ref[s] < wide)
            return jnp.where(cnt < need, cand, j)

        nbits = 7 + sum((i >= (1 << k)).astype(I32) for k in range(7))
        j = lax.fori_loop(0, nbits, lambda it, j: index_bit(nbits - 1 - it, j),
                          jnp.zeros((T, 1), I32))
        jdx_ref[...] = jnp.broadcast_to(j, (T, T))

    jdx = jdx_ref[...]

    def to_mask(s0):
        for s in [s0 + u for u in range(4)]:
            key = key_ref[s]
            sel = (key > thr) | ((key == thr) & ((s - DUMMY) * T + col <= jdx))
            mask_ref[s] = jnp.where(sel & admissible(s - DUMMY), 0.0, NEG_INF)

    for_groups(4, to_mask)

    head_of_lane = (lax.broadcasted_iota(I32, (A_HEADS * A_DIM, T), 0) // A_DIM
                    == lax.broadcasted_iota(I32, (A_HEADS * A_DIM, T), 1))
    head_sum = jnp.where(head_of_lane, 1.0, 0.0).astype(BF16)

    def head_norms2(t):
        t = t.astype(F32)
        return jnp.dot((t * t).astype(BF16), head_sum, preferred_element_type=F32)

    @pl.when(i == 0)
    def _():
        def widest(t, best):
            return jnp.maximum(best, head_norms2(k_ref[pl.ds(pl.multiple_of(t * T, T), T), :]))
        best = lax.fori_loop(0, k_ref.shape[0] // T, widest, jnp.zeros((T, T), F32))
        kmax_ref[...] = jnp.broadcast_to(jnp.max(best, axis=0, keepdims=True), kmax_ref.shape)

    bound2 = head_norms2(aq_ref[...]) * kmax_ref[0:1, :] * (NORM_SLACK / A_DIM)
    room = LOGIT_LIMIT - jnp.max(jnp.abs(bias_ref[...]))
    small_logits = (room > 0.0) & (jnp.max(bound2) <= room * room)

    ls_ref[...] = jnp.zeros(ls_ref.shape, F32)
    acc_ref[...] = jnp.zeros(acc_ref.shape, F32)

    def attend(s0, stage, near, final):
        madd = [mask_ref[s0], mask_ref[s0 + 1]]
        for hp in range(A_HEADS // 2):
            weights = []
            for h in (2 * hp, 2 * hp + 1):
                parts = [stage[h * T:(h + 1) * T, u * T:(u + 1) * T] + madd[u] for u in range(2)]
                if near:
                    parts = [parts[u] + bias_ref[h, 1 - u] for u in range(2)]
                if not final:
                    mx_ref[h] = jnp.maximum(mx_ref[h], jnp.maximum(parts[0], parts[1]))
                else:
                    es = [jnp.exp(part - mx_ref[h]) for part in parts]
                    ls_ref[h] = ls_ref[h] + (es[0] + es[1])
                    weights.append(jnp.concatenate(es, axis=1).astype(BF16))
            if final:
                ps = slice(hp * T, (hp + 1) * T)
                vw = jnp.concatenate([v_ref[tile_rows(s0), ps], v_ref[tile_rows(s0 + 1), ps]], axis=0)
                acc_ref[hp] = acc_ref[hp] + jnp.dot(jnp.concatenate(weights, axis=0), vw,
                                                    preferred_element_type=F32)

    def attention_sweep(final):
        sweep(qm_ref, k_ref,
              functools.partial(attend, near=False, final=final),
              functools.partial(attend, near=True, final=final))

    @pl.when(small_logits)
    def _():
        mx_ref[...] = jnp.zeros(mx_ref.shape, F32)

    @pl.when(jnp.logical_not(small_logits))
    def _():
        mx_ref[...] = jnp.full(mx_ref.shape, NEG_INF, F32)
        attention_sweep(False)
        for h in range(A_HEADS):
            mx_ref[h] = jnp.broadcast_to(jnp.max(mx_ref[h], axis=-1, keepdims=True), (T, T))

    attention_sweep(True)

    def row_total(h):
        return jnp.maximum(jnp.sum(ls_ref[h], axis=-1, keepdims=True), TINY)

    for hp in range(A_HEADS // 2):
        o0 = acc_ref[hp, :T, :] / row_total(2 * hp)
        o1 = acc_ref[hp, T:, :] / row_total(2 * hp + 1)
        g = ag_ref[:, hp * T:(hp + 1) * T].astype(F32)
        o_ref[:, hp * T:(hp + 1) * T] = (jnp.where(low, o0, o1) * _silu(g)).astype(o_ref.dtype)


def _dsa(pab, bias_tiles, topk):
    bsz, p, _ = pab.shape
    T = LANES
    nq = p // T
    hw = A_HEADS * A_DIM
    one = pl.Buffered(1)
    return pl.pallas_call(
        functools.partial(_dsa_kernel, topk=float(topk)),
        grid=(bsz, nq),
        in_specs=[
            pl.BlockSpec((None, T, hw), lambda b, i: (b, i, 0)),
            pl.BlockSpec((None, T, hw), lambda b, i: (b, i, 3)),
            pl.BlockSpec((None, T, hw), lambda b, i: (b, i, 4)),
            pl.BlockSpec((None, T, T), lambda b, i: (b, i, 33)),
            pl.BlockSpec((None, p, hw), lambda b, i: (b, 0, 1), pipeline_mode=one),
            pl.BlockSpec((None, p, hw), lambda b, i: (b, 0, 2), pipeline_mode=one),
            pl.BlockSpec((None, p, T), lambda b, i: (b, 0, 32), pipeline_mode=one),
            pl.BlockSpec((A_HEADS, 2, T, T), lambda b, i: (0, 0, 0, 0), pipeline_mode=one),
        ],
        out_specs=pl.BlockSpec((None, T, hw), lambda b, i: (b, i, 0)),
        out_shape=jax.ShapeDtypeStruct((bsz, p, hw), BF16),
        scratch_shapes=[
            pltpu.VMEM((nq + DUMMY, T, T), I32),
            pltpu.VMEM((nq + DUMMY, T, T), I32),
            pltpu.VMEM((nq + DUMMY, T, T), F32),
            pltpu.VMEM((A_HEADS * T, 2 * T), F32),
            pltpu.VMEM((A_HEADS * T, 2 * T), F32),
            pltpu.VMEM((1, IDX_HEADS * T, T), BF16),
            pltpu.VMEM((IDX_HEADS, T, T), F32),
            pltpu.VMEM((A_HEADS // 2, 2 * T, T), BF16),
            pltpu.VMEM((T, T), I32),
            pltpu.VMEM((8, T), F32),
            pltpu.VMEM((A_HEADS, T, T), F32),
            pltpu.VMEM((A_HEADS, T, T), F32),
            pltpu.VMEM((A_HEADS // 2, 2 * T, T), F32),
        ],
        compiler_params=_params(("parallel", "arbitrary")),
        name="dsa",
    )(pab, pab, pab, pab, pab, pab, pab, bias_tiles)


def _gla_kernel(q_ref, k_ref, v_ref, g_ref, a_ref, w2_ref, gb_ref, o_ref, st_ref, *, nchunk):
    C = CHUNK
    T = LANES

    @pl.when(pl.program_id(1) == 0)
    def _():
        st_ref[...] = jnp.zeros(st_ref.shape, F32)

    r_i = lax.broadcasted_iota(I32, (C, C), 0)
    c_i = lax.broadcasted_iota(I32, (C, C), 1)
    causal = c_i <= r_i
    tri = jnp.where(causal, 1.0, 0.0).astype(BF16)
    lane = lax.broadcasted_iota(I32, (C, T), 1)
    sd_r = lax.broadcasted_iota(I32, (2 * B_DV, T), 0)
    sd_c = lax.broadcasted_iota(I32, (2 * B_DV, T), 1)
    blockdiag = (sd_r >= B_DV) == (sd_c >= B_DK)

    def chunk(c, carry):
        r = pl.ds(pl.multiple_of(c * C, C), C)
        x = jnp.dot(a_ref[r, :], w2_ref[...], preferred_element_type=F32) + gb_ref[...]
        log_a = _softplus_parts(x)[1] * (1.0 / GLA_TAU)
        a1, a2, a3 = _split3(log_a)
        bcum = (jnp.dot(tri, a1, preferred_element_type=F32)
                + jnp.dot(tri, a2, preferred_element_type=F32)
                + jnp.dot(tri, a3, preferred_element_type=F32))
        b_last = bcum[C - 1:C, :]
        q = q_ref[r, :].astype(F32) * (B_DK ** -0.5)
        k = k_ref[r, :].astype(F32)
        q_t = (q * jnp.exp(bcum)).astype(BF16)
        k_t = (k * jnp.exp(-bcum)).astype(BF16)
        k_d = (k * jnp.exp(b_last - bcum)).astype(BF16)
        decay = jnp.exp(b_last)
        for hp in range(B_HEADS // 2):
            ls = slice(hp * T, (hp + 1) * T)
            st = st_ref[hp]
            o_inter = lax.dot_general(q_t[:, ls], st.astype(BF16), NT_DIMS,
                                      preferred_element_type=F32)
            for hh in range(2):
                h = 2 * hp + hh
                mine = (lane < B_DK) if hh == 0 else (lane >= B_DK)
                qm = jnp.where(mine, q_t[:, ls], jnp.zeros((C, T), BF16))
                att = lax.dot_general(qm, k_t[:, ls], NT_DIMS, preferred_element_type=F32)
                att = jnp.where(causal, att, 0.0)
                vs = slice(h * B_DV, (h + 1) * B_DV)
                o = jnp.dot(att.astype(BF16), v_ref[r, vs], preferred_element_type=F32)
                o = o + o_inter[:, hh * B_DV:(hh + 1) * B_DV]
                o = o * lax.rsqrt(jnp.mean(o * o, axis=-1, keepdims=True) + NORM_EPS)
                o = o * _silu(g_ref[r, vs].astype(F32))
                o_ref[r, vs] = o.astype(o_ref.dtype)
            vp = v_ref[r, hp * 2 * B_DV:(hp + 1) * 2 * B_DV]
            contrib = lax.dot_general(vp, k_d[:, ls], TN_DIMS, preferred_element_type=F32)
            st_ref[hp] = st * decay[:, ls] + jnp.where(blockdiag, contrib, 0.0)
        return carry

    lax.fori_loop(0, nchunk, chunk, 0)


def _gla(pab, w2, gb):
    bsz, p, _ = pab.shape
    tc = _pick_tile(p, (640, 128))
    hv = B_HEADS * B_DV
    hk = B_HEADS * B_DK
    return pl.pallas_call(
        functools.partial(_gla_kernel, nchunk=tc // CHUNK),
        grid=(bsz, p // tc),
        in_specs=[
            pl.BlockSpec((None, tc, hk), lambda b, c: (b, c, 14)),
            pl.BlockSpec((None, tc, hk), lambda b, c: (b, c, 15)),
            pl.BlockSpec((None, tc, hv), lambda b, c: (b, c, 5)),
            pl.BlockSpec((None, tc, hv), lambda b, c: (b, c, 6)),
            pl.BlockSpec((None, tc, LANES), lambda b, c: (b, c, 34)),
            pl.BlockSpec((LANES, hk), lambda b, c: (0, 0)),
            pl.BlockSpec((1, hk), lambda b, c: (0, 0)),
        ],
        out_specs=pl.BlockSpec((None, tc, hv), lambda b, c: (b, c, 0)),
        out_shape=jax.ShapeDtypeStruct((bsz, p, hv), BF16),
        scratch_shapes=[pltpu.VMEM((B_HEADS // 2, 2 * B_DV, 2 * B_DK), F32)],
        compiler_params=_params(("parallel", "arbitrary")),
        name="gla",
    )(pab, pab, pab, pab, pab, w2, gb)


def _ret_kernel(q_ref, k_ref, v_ref, g_ref, cos_ref, sin_ref, dmat_ref, zeta_ref, xi_ref,
                cdec_ref, o_ref, st_ref, *, nchunk):
    C = CHUNK
    T = LANES
    W = C_HEADS * C_DK

    @pl.when(pl.program_id(1) == 0)
    def _():
        st_ref[...] = jnp.zeros(st_ref.shape, F32)

    lane_w = lax.broadcasted_iota(I32, (C, W), 1)
    first_half = (lane_w & (C_DK - 1)) < (C_DK // 2)
    lane = lax.broadcasted_iota(I32, (C, T), 1)
    sd_r = lax.broadcasted_iota(I32, (2 * C_DV, T), 0)
    sd_c = lax.broadcasted_iota(I32, (2 * C_DV, T), 1)
    blockdiag = (sd_r >= C_DV) == (sd_c >= C_DK)

    def rotate(x, cos, sin_signed):
        swapped = jnp.where(first_half, pltpu.roll(x, W - C_DK // 2, 1), pltpu.roll(x, C_DK // 2, 1))
        return x * cos + swapped * sin_signed

    def chunk(c, carry):
        r = pl.ds(pl.multiple_of(c * C, C), C)
        cos = cos_ref[r, :]
        sin = sin_ref[r, :]
        q = rotate(q_ref[r, :].astype(F32), cos, sin)
        k = rotate(k_ref[r, :].astype(F32), cos, sin) * (C_DK ** -0.5)
        q_b = q.astype(BF16)
        k_b = k.astype(BF16)
        q_x = (q * xi_ref[...]).astype(BF16)
        k_z = (k * zeta_ref[...]).astype(BF16)
        for hp in range(C_HEADS // 2):
            ls = slice(hp * T, (hp + 1) * T)
            st = st_ref[hp]
            o_inter = lax.dot_general(q_x[:, ls], st.astype(BF16), NT_DIMS,
                                      preferred_element_type=F32)
            for hh in range(2):
                h = 2 * hp + hh
                mine = (lane < C_DK) if hh == 0 else (lane >= C_DK)
                qm = jnp.where(mine, q_b[:, ls], jnp.zeros((C, T), BF16))
                att = lax.dot_general(qm, k_b[:, ls], NT_DIMS, preferred_element_type=F32)
                att = att * dmat_ref[h]
                vs = slice(h * C_DV, (h + 1) * C_DV)
                o = jnp.dot(att.astype(BF16), v_ref[r, vs], preferred_element_type=F32)
                o = o + o_inter[:, hh * C_DV:(hh + 1) * C_DV]
                o = o - jnp.mean(o, axis=-1, keepdims=True)
                o = o * lax.rsqrt(jnp.mean(o * o, axis=-1, keepdims=True) + NORM_EPS)
                o = o * _silu(g_ref[r, vs].astype(F32))
                o_ref[r, vs] = o.astype(o_ref.dtype)
            vp = v_ref[r, hp * 2 * C_DV:(hp + 1) * 2 * C_DV]
            contrib = lax.dot_general(vp, k_z[:, ls], TN_DIMS, preferred_element_type=F32)
            st_ref[hp] = st * cdec_ref[:, ls] + jnp.where(blockdiag, contrib, 0.0)
        return carry

    lax.fori_loop(0, nchunk, chunk, 0)


def _retention_tables(p):
    log_gamma = np.log(1.0 - np.exp2(-5.0 - np.arange(C_HEADS, dtype=np.float64)))
    i = np.arange(CHUNK, dtype=np.float64)
    diff = i[:, None] - i[None, :]
    dmat = np.where(diff >= 0, np.exp(log_gamma[:, None, None] * np.maximum(diff, 0.0)), 0.0)
    zeta = np.exp(log_gamma[:, None] * (CHUNK - 1 - i))
    xi = np.exp(log_gamma[:, None] * (i + 1))
    cdec = np.exp(log_gamma * CHUNK)
    widen = lambda t: np.repeat(t.T[:, :, None], C_DK, axis=2).reshape(CHUNK, C_HEADS * C_DK)
    half = C_DK // 2
    inv = jnp.asarray(ROPE_BASE, F32) ** (-jnp.arange(half, dtype=F32) / half)
    ang = jnp.arange(p, dtype=jnp.int32).astype(F32)[:, None] * inv[None, :]
    cos = jnp.tile(jnp.cos(ang), (1, 2 * C_HEADS))
    sin = jnp.sin(ang)
    sin_signed = jnp.tile(jnp.concatenate([-sin, sin], axis=1), (1, C_HEADS))
    return (cos, sin_signed, jnp.asarray(dmat, F32), jnp.asarray(widen(zeta), F32),
            jnp.asarray(widen(xi), F32),
            jnp.asarray(np.repeat(cdec, C_DK)[None, :], F32))


def _retention(pcd, tables):
    bsz, p, _ = pcd.shape
    cos, sin, dmat, zeta, xi, cdec = tables
    tc = _pick_tile(p, (640, 128))
    hv = C_HEADS * C_DV
    hk = C_HEADS * C_DK
    full = lambda shape: pl.BlockSpec(shape, lambda b, c: (0,) * len(shape))
    return pl.pallas_call(
        functools.partial(_ret_kernel, nchunk=tc // CHUNK),
        grid=(bsz, p // tc),
        in_specs=[
            pl.BlockSpec((None, tc, hk), lambda b, c: (b, c, 0)),
            pl.BlockSpec((None, tc, hk), lambda b, c: (b, c, 1)),
            pl.BlockSpec((None, tc, hv), lambda b, c: (b, c, 1)),
            pl.BlockSpec((None, tc, hv), lambda b, c: (b, c, 2)),
            pl.BlockSpec((tc, hk), lambda b, c: (c, 0)),
            pl.BlockSpec((tc, hk), lambda b, c: (c, 0)),
            full((C_HEADS, CHUNK, CHUNK)),
            full((CHUNK, hk)),
            full((CHUNK, hk)),
            full((1, hk)),
        ],
        out_specs=pl.BlockSpec((None, tc, hv), lambda b, c: (b, c, 0)),
        out_shape=jax.ShapeDtypeStruct((bsz, p, hv), BF16),
        scratch_shapes=[pltpu.VMEM((C_HEADS // 2, 2 * C_DV, 2 * C_DK), F32)],
        compiler_params=_params(("parallel", "arbitrary")),
        name="retention",
    )(pcd, pcd, pcd, pcd, cos, sin, dmat, zeta, xi, cdec)


def _sb_kernel(q_ref, g_ref, k_ref, v_ref, o_ref, qm_ref, run_ref, acc_ref, hi_ref, lo_ref, ls_ref,
               w_ref):
    T = LANES
    i = pl.program_id(1)
    row = lax.broadcasted_iota(I32, (T, T), 0)
    col = lax.broadcasted_iota(I32, (T, T), 1)
    qpos = i * T + row
    low = col < D_DIM
    u_r = lax.broadcasted_iota(I32, (T, 2 * T), 0)
    u_c = lax.broadcasted_iota(I32, (T, 2 * T), 1)
    suffix = jnp.where((u_c >= T) | (u_r > u_c), 1.0, 0.0).astype(BF16)

    for h in range(D_HEADS):
        pair = q_ref[:, (h // 2) * T:(h // 2 + 1) * T]
        mine = low if h % 2 == 0 else ~low
        qm_ref[h] = jnp.where(mine, pair * (D_DIM ** -0.5), jnp.zeros_like(pair))
    run_ref[...] = jnp.zeros(run_ref.shape, F32)
    acc_ref[...] = jnp.zeros(acc_ref.shape, F32)

    def tile(state):
        t, _ = state
        kt = i - t
        ks = pl.ds(pl.multiple_of(kt * T, T), T)
        kpos = kt * T + col
        ok = (kpos < qpos) & (kpos >= PAD_FRONT)
        for h in range(D_HEADS):
            ps = slice((h // 2) * T, (h // 2 + 1) * T)
            z = lax.dot_general(qm_ref[h], k_ref[ks, ps], NT_DIMS, preferred_element_type=F32)
            sp, logsig = _softplus_parts(z)
            log_1m = jnp.where(ok, -sp, 0.0)
            hi = log_1m.astype(BF16)
            hi_ref[h] = hi
            lo_ref[h] = (log_1m - hi.astype(F32)).astype(BF16)
            ls_ref[h] = jnp.where(ok, logsig, NEG_INF)
        slowest = None
        for h in range(D_HEADS):
            sums = (jnp.dot(hi_ref[h], suffix, preferred_element_type=F32)
                    + jnp.dot(lo_ref[h], suffix, preferred_element_type=F32))
            run = run_ref[h]
            w_ref[h] = jnp.exp(ls_ref[h] + run + sums[:, :T]).astype(BF16)
            run = run + sums[:, T:]
            run_ref[h] = run
            slowest = run if slowest is None else jnp.maximum(slowest, run)
        for h in range(D_HEADS):
            ps = slice((h // 2) * T, (h // 2 + 1) * T)
            acc_ref[h] = acc_ref[h] + jnp.dot(w_ref[h], v_ref[ks, ps],
                                              preferred_element_type=F32)
        return t + 1, (jnp.max(slowest) > LOG_F32_UNDERFLOW).astype(I32)

    lax.while_loop(lambda s: (s[0] <= i) & (s[1] > 0), tile, (jnp.int32(0), jnp.int32(1)))

    for hp in range(D_HEADS // 2):
        ps = slice(hp * T, (hp + 1) * T)
        o = jnp.where(low, acc_ref[2 * hp], acc_ref[2 * hp + 1]) * _silu(g_ref[:, ps].astype(F32))
        o_ref[:, ps] = o.astype(o_ref.dtype)


def _stick_breaking(pcd):
    bsz, p, _ = pcd.shape
    T = LANES
    hw = D_HEADS * D_DIM
    one = pl.Buffered(1)
    return pl.pallas_call(
        _sb_kernel,
        grid=(bsz, p // T),
        in_specs=[
            pl.BlockSpec((None, T, hw), lambda b, i: (b, i, 3)),
            pl.BlockSpec((None, T, hw), lambda b, i: (b, i, 6)),
            pl.BlockSpec((None, p, hw), lambda b, i: (b, 0, 4), pipeline_mode=one),
            pl.BlockSpec((None, p, hw), lambda b, i: (b, 0, 5), pipeline_mode=one),
        ],
        out_specs=pl.BlockSpec((None, T, hw), lambda b, i: (b, i, 0)),
        out_shape=jax.ShapeDtypeStruct((bsz, p, hw), BF16),
        scratch_shapes=[
            pltpu.VMEM((D_HEADS, T, T), BF16),
            pltpu.VMEM((D_HEADS, T, T), F32),
            pltpu.VMEM((D_HEADS, T, T), F32),
            pltpu.VMEM((D_HEADS, T, T), BF16),
            pltpu.VMEM((D_HEADS, T, T), BF16),
            pltpu.VMEM((D_HEADS, T, T), F32),
            pltpu.VMEM((D_HEADS, T, T), BF16),
        ],
        compiler_params=_params(("parallel", "arbitrary")),
        name="stick_breaking",
    )(pcd, pcd, pcd, pcd)


def _rel_bucket_np(rel):
    half = REL_BUCKETS // 2
    max_exact = half // 2
    n = -rel
    ret = np.where(n < 0, half, 0)
    n = np.abs(n)
    edges = [math.ceil(max_exact * (REL_MAX_DIST / max_exact) ** (j / (half - max_exact)) - 1e-9)
             for j in range(1, half - max_exact)]
    large = max_exact + sum((n >= e).astype(np.int64) for e in edges)
    return ret + np.where(n < max_exact, n, large)


def _bias_tiles(rel_bias):
    i = np.arange(LANES)[:, None]
    j = np.arange(LANES)[None, :]
    idx = np.stack([_rel_bucket_np(j - i - LANES * d) for d in range(3)])
    assert (idx[2] == idx[2, 0, 0]).all()
    table = rel_bias.astype(F32) - rel_bias.astype(F32)[idx[2, 0, 0]]
    near = jnp.asarray(idx[:2], I32)[None]
    tiles = jnp.zeros((table.shape[1],) + near.shape[1:], F32)
    for bucket in range(REL_BUCKETS):
        tiles = jnp.where(near == bucket, table[bucket][:, None, None, None], tiles)
    return tiles


def _layout_ab(w):
    aq, ak, av, ag, iq, ik, iw, bq, bk, bv, bg, ba = jnp.split(w, np.cumsum(SPLIT_AB)[:-1].tolist(), axis=1)
    pad = lambda t: jnp.pad(t, ((0, 0), (0, LANES - t.shape[1])))
    cols = [aq, ak, av, ag, iq, bv, bg, bq, bk, jnp.concatenate([ik, ik], axis=1), pad(iw), pad(ba)]
    return jnp.concatenate(cols, axis=1).astype(BF16)


def kernel(x, meta_tokens, rel_bias, norm_g, final_g, w_in_ab, gla_gate_w2, gla_gate_b, w_out_ab,
           w_in_cd, w_out_cd):
    bsz, seq, d = x.shape
    p = seq + PAD_FRONT + N_META
    depth = norm_g.shape[0]
    topk = min(TOPK_MAX, seq // 4)
    h = jnp.concatenate([jnp.zeros((bsz, PAD_FRONT, d), x.dtype),
                         jnp.broadcast_to(meta_tokens.astype(x.dtype)[None], (bsz, N_META, d)),
                         x], axis=1)
    bias_tiles = _bias_tiles(rel_bias)
    tables = _retention_tables(p)
    for layer in range(depth):
        j = layer // 2
        if layer % 2 == 0:
            pab = _inproj(h, norm_g[layer], _layout_ab(w_in_ab[j]))
            oa = _dsa(pab, bias_tiles, topk)
            w2 = jnp.pad(gla_gate_w2[j], ((0, LANES - GLA_GATE_RANK), (0, 0))).astype(BF16)
            ob = _gla(pab, w2, gla_gate_b[j].reshape(1, -1).astype(F32))
            h = _outproj(oa, ob, w_out_ab[j], h)
        else:
            pcd = _inproj(h, norm_g[layer], w_in_cd[j].astype(BF16))
            oc = _retention(pcd, tables)
            od = _stick_breaking(pcd)
            h = _outproj(oc, od, w_out_cd[j], h)
    return _final_norm(h, final_g, seq)
```

```python
import functools
import math

import numpy as np
import jax
import jax.numpy as jnp
from jax import lax
from jax.experimental import pallas as pl
from jax.experimental.pallas import tpu as pltpu

F32 = jnp.float32
BF16 = jnp.bfloat16
I32 = jnp.int32

D_MODEL = 1024
CHUNK = 64
N_META = 16
PAD_FRONT = 128 - N_META
NORM_EPS = 1e-6
NEG_INF = -1e30
A_HEADS, A_DIM = 8, 64
IDX_HEADS, IDX_DIM = 8, 64
IDX_SCALE = (IDX_DIM ** -0.5) * (IDX_HEADS ** -0.5)
TOPK_MAX = 256
B_HEADS, B_DK, B_DV = 4, 64, 128
GLA_GATE_RANK = 16
GLA_TAU = 16.0
C_HEADS, C_DK, C_DV = 4, 64, 128
ROPE_BASE = 10000.0
D_HEADS, D_DIM = 8, 64
REL_BUCKETS = 32
REL_MAX_DIST = 128
SPLIT_AB = (512, 512, 512, 512, 512, 64, 8, 256, 256, 512, 512, 16)
SPLIT_CD = (256, 256, 512, 512, 512, 512, 512, 512)

LANES = 128
W_AB_PAD = 4480
W_CD = 3584
I32_MIN = -2 ** 31
I32_MAX = 2 ** 31 - 1
DUMMY = 7
IDX_BITS = 14
LOG_F32_UNDERFLOW = -104.0
LOGIT_LIMIT = 40.0
NORM_SLACK = 1.05
TINY = 1e-30
BF16_BITS = -65536
CHUNK_UNROLL = 2
VMEM_LIMIT = 56 * 1024 * 1024

NT_DIMS = (((1,), (1,)), ((), ()))
TN_DIMS = (((0,), (0,)), ((), ()))


def _pick_tile(n, candidates):
    for c in candidates:
        if n % c == 0:
            return c
    raise ValueError(f"no tile for {n}")


def _params(sem):
    return pltpu.CompilerParams(dimension_semantics=sem, vmem_limit_bytes=VMEM_LIMIT)


def _silu(x):
    return x / (1.0 + jnp.exp(-x))


def _softplus_parts(z):
    t = jnp.log(1.0 + jnp.exp(-jnp.abs(z)))
    return jnp.maximum(z, 0.0) + t, jnp.minimum(z, 0.0) - t


def _split3(x):
    a = x.astype(BF16)
    r = x - a.astype(F32)
    b = r.astype(BF16)
    c = (r - b.astype(F32)).astype(BF16)
    return a, b, c


def _inproj_kernel(x_ref, g_ref, w_ref, o_ref, hn_ref):
    @pl.when(pl.program_id(2) == 0)
    def _():
        x = x_ref[...]
        ms = jnp.mean(x * x, axis=-1, keepdims=True)
        hn_ref[...] = (x * lax.rsqrt(ms + NORM_EPS) * g_ref[...]).astype(BF16)

    o_ref[...] = jnp.dot(hn_ref[...], w_ref[...], preferred_element_type=F32).astype(o_ref.dtype)


def _inproj(h, g, w):
    bsz, p, d = h.shape
    n = w.shape[1]
    tp = _pick_tile(p, (1664, 1280, 640, 128))
    tn = _pick_tile(n, (896, 512, 128))
    return pl.pallas_call(
        _inproj_kernel,
        grid=(bsz, p // tp, n // tn),
        in_specs=[
            pl.BlockSpec((None, tp, d), lambda b, i, j: (b, i, 0)),
            pl.BlockSpec((1, d), lambda b, i, j: (0, 0)),
            pl.BlockSpec((d, tn), lambda b, i, j: (0, j)),
        ],
        out_specs=pl.BlockSpec((None, tp, tn), lambda b, i, j: (b, i, j)),
        out_shape=jax.ShapeDtypeStruct((bsz, p, n), BF16),
        scratch_shapes=[pltpu.VMEM((tp, d), BF16)],
        compiler_params=_params(("parallel", "parallel", "arbitrary")),
        name="inproj",
    )(h, g.reshape(1, d), w)


def _outproj_kernel(m1_ref, m2_ref, w1_ref, w2_ref, h_ref, o_ref, *, tp):
    y = jnp.dot(m1_ref[...], w1_ref[...], preferred_element_type=F32)
    y = y + jnp.dot(m2_ref[...], w2_ref[...], preferred_element_type=F32)
    pos = pl.program_id(1) * tp + lax.broadcasted_iota(I32, y.shape, 0)
    o_ref[...] = h_ref[...] + jnp.where(pos >= PAD_FRONT, y, 0.0)


def _outproj(m1, m2, w_out, h):
    bsz, p, d = h.shape
    k1 = m1.shape[-1]
    k2 = m2.shape[-1]
    tp = _pick_tile(p, (832, 640, 128))
    w = w_out.astype(BF16)
    return pl.pallas_call(
        functools.partial(_outproj_kernel, tp=tp),
        grid=(bsz, p // tp),
        in_specs=[
            pl.BlockSpec((None, tp, k1), lambda b, i: (b, i, 0)),
            pl.BlockSpec((None, tp, k2), lambda b, i: (b, i, 0)),
            pl.BlockSpec((k1, d), lambda b, i: (0, 0)),
            pl.BlockSpec((k2, d), lambda b, i: (0, 0)),
            pl.BlockSpec((None, tp, d), lambda b, i: (b, i, 0)),
        ],
        out_specs=pl.BlockSpec((None, tp, d), lambda b, i: (b, i, 0)),
        out_shape=jax.ShapeDtypeStruct((bsz, p, d), F32),
        compiler_params=_params(("parallel", "parallel")),
        name="outproj",
    )(m1, m2, w[:k1], w[k1:], h)


def _outproj_norm_kernel(m1_ref, m2_ref, w1_ref, w2_ref, h_ref, g_ref, o_ref):
    y = jnp.dot(m1_ref[...], w1_ref[...], preferred_element_type=F32)
    y = y + jnp.dot(m2_ref[...], w2_ref[...], preferred_element_type=F32)
    x = h_ref[...] + y
    ms = jnp.mean(x * x, axis=-1, keepdims=True)
    o_ref[...] = x * lax.rsqrt(ms + NORM_EPS) * g_ref[...]


def _outproj_norm(m1, m2, w_out, h, g, seq):
    bsz, p, d = h.shape
    k1 = m1.shape[-1]
    k2 = m2.shape[-1]
    tp = LANES
    skip = (p - seq) // tp
    assert skip * tp == p - seq
    w = w_out.astype(BF16)
    rows = lambda b, i: (b, i + skip, 0)
    return pl.pallas_call(
        _outproj_norm_kernel,
        grid=(bsz, seq // tp),
        in_specs=[
            pl.BlockSpec((None, tp, k1), rows),
            pl.BlockSpec((None, tp, k2), rows),
            pl.BlockSpec((k1, d), lambda b, i: (0, 0)),
            pl.BlockSpec((k2, d), lambda b, i: (0, 0)),
            pl.BlockSpec((None, tp, d), rows),
            pl.BlockSpec((1, d), lambda b, i: (0, 0)),
        ],
        out_specs=pl.BlockSpec((None, tp, d), lambda b, i: (b, i, 0)),
        out_shape=jax.ShapeDtypeStruct((bsz, seq, d), F32),
        compiler_params=_params(("parallel", "parallel")),
        name="outproj_norm",
    )(m1, m2, w[:k1], w[k1:], h, g.reshape(1, d))


def _dsa_kernel(aq_ref, ag_ref, iq_ref, iw_ref, k_ref, v_ref, ik_ref, bias_ref, o_ref,
                key_ref, pos_ref, mask_ref, sa_ref, sb_ref, iqs_ref, wts_ref, qm_ref,
                jdx_ref, kmax_ref, mx_ref, ls_ref, acc_ref, *, topk):
    T = LANES
    i = pl.program_id(1)
    last = i + DUMMY

    def num_groups(group):
        return (i + group) // group

    def first_slot(group):
        return last + 1 - group * num_groups(group)

    row = lax.broadcasted_iota(I32, (T, T), 0)
    col = lax.broadcasted_iota(I32, (T, T), 1)
    qchunk = (i * T + row) >> 6
    low = col < 64

    def admissible(kt):
        kpos = kt * T + col
        return (kpos >= PAD_FRONT) & ((kpos >> 6) <= qchunk)

    def tile_rows(slot):
        return pl.ds(pl.multiple_of(jnp.maximum(slot - DUMMY, 0) * T, T), T)

    def for_groups(group, body):
        first = first_slot(group)

        def step(g, carry):
            body(first + group * g)
            return carry

        lax.fori_loop(0, num_groups(group), step, 0)

    iw = iw_ref[...].astype(F32)
    for h in range(IDX_HEADS):
        pair = iq_ref[:, (h // 2) * T:(h // 2 + 1) * T]
        mine = low if h % 2 == 0 else ~low
        iqs_ref[0, h * T:(h + 1) * T, :] = jnp.where(mine, pair, jnp.zeros_like(pair))
        wts_ref[h] = jnp.broadcast_to(iw[:, h:h + 1], (T, T))
    for h in range(A_HEADS):
        pair = aq_ref[:, (h // 2) * T:(h // 2 + 1) * T]
        mine = low if h % 2 == 0 else ~low
        qm_ref[h // 2, (h % 2) * T:(h % 2 + 1) * T, :] = jnp.where(
            mine, pair * (A_DIM ** -0.5), jnp.zeros_like(pair))

    def sweep(lhs_ref, rhs_ref, consume, consume_last):
        group_rows = lhs_ref.shape[1]

        def issue(s0, stage):
            rows = [tile_rows(s0), tile_rows(s0 + 1)]
            for g in range(lhs_ref.shape[0]):
                ps = slice(g * T, (g + 1) * T)
                kw = jnp.concatenate([rhs_ref[rows[0], ps], rhs_ref[rows[1], ps]], axis=0)
                stage[g * group_rows:(g + 1) * group_rows, :] = lax.dot_general(
                    lhs_ref[g], kw, NT_DIMS, preferred_element_type=F32)

        first = first_slot(4)
        issue(first, sa_ref)

        def body(q, carry):
            s = first + 4 * q
            issue(s + 2, sb_ref)
            consume(s, sa_ref)
            issue(s + 4, sa_ref)
            consume(s + 2, sb_ref)
            return carry

        lax.fori_loop(0, num_groups(4) - 1, body, 0)
        issue(last - 1, sb_ref)
        consume(last - 3, sa_ref)
        consume_last(last - 1, sb_ref)

    def score_pair(s0, stage):
        for u in range(2):
            tot = jnp.zeros((T, T), F32)
            for h in range(IDX_HEADS):
                tot = tot + jnp.maximum(stage[h * T:(h + 1) * T, u * T:(u + 1) * T], 0.0) * wts_ref[h]
            score = jnp.where(admissible(s0 + u - DUMMY), tot * IDX_SCALE, NEG_INF)
            bits = lax.bitcast_convert_type(score, I32)
            key_ref[s0 + u] = bits ^ ((bits >> 31) & 0x7FFFFFFF)

    sweep(iqs_ref, ik_ref, score_pair, score_pair)
    for d in range(DUMMY):
        key_ref[d] = jnp.full((T, T), I32_MIN, I32)

    def row_count(pred):
        one = jnp.ones((T, T), I32)
        zero = jnp.zeros((T, T), I32)
        first = first_slot(8)

        def body(g, acc):
            hits = [jnp.where(pred(first + 8 * g + u), one, zero) for u in range(8)]
            while len(hits) > 1:
                hits = [a + b for a, b in zip(hits[::2], hits[1::2])]
            return acc + hits[0]

        acc = lax.fori_loop(0, num_groups(8), body, zero)
        return jnp.sum(acc.astype(F32), axis=-1, keepdims=True)

    def value_bit(it, t):
        cand = t + (jnp.int32(1) << (31 - it))
        wide = jnp.broadcast_to(cand, (T, T))
        cnt = row_count(lambda s: key_ref[s] >= wide)
        return jnp.where(cnt >= topk, cand, t)

    thr = jnp.broadcast_to(lax.fori_loop(0, 32, value_bit, jnp.full((T, 1), I32_MIN, I32)), (T, T))

    need = topk - row_count(lambda s: key_ref[s] > thr)
    ties = row_count(lambda s: key_ref[s] == thr)
    jdx_ref[...] = jnp.full((T, T), 2 ** IDX_BITS, I32)

    @pl.when(jnp.max(jnp.where(ties > need, 1.0, 0.0)) > 0.0)
    def _():
        def tie_positions(s0):
            for s in [s0 + u for u in range(4)]:
                pos_ref[s] = jnp.where(key_ref[s] == thr, (s - DUMMY) * T + col, I32_MAX)

        for_groups(4, tie_positions)
        for d in range(DUMMY):
            pos_ref[d] = jnp.full((T, T), I32_MAX, I32)

        def index_bit(bit, j):
            cand = j + (jnp.int32(1) << bit)
            wide = jnp.broadcast_to(cand, (T, T))
            cnt = row_count(lambda s: pos_ref[s] < wide)
            return jnp.where(cnt < need, cand, j)

        nbits = 7 + sum((i >= (1 << k)).astype(I32) for k in range(7))
        j = lax.fori_loop(0, nbits, lambda it, j: index_bit(nbits - 1 - it, j),
                          jnp.zeros((T, 1), I32))
        jdx_ref[...] = jnp.broadcast_to(j, (T, T))

    jdx = jdx_ref[...]

    def to_mask(s0):
        for s in [s0 + u for u in range(4)]:
            key = key_ref[s]
            sel = (key > thr) | ((key == thr) & ((s - DUMMY) * T + col <= jdx))
            mask_ref[s] = jnp.where(sel & admissible(s - DUMMY), 0.0, NEG_INF)

    for_groups(4, to_mask)

    head_of_lane = (lax.broadcasted_iota(I32, (A_HEADS * A_DIM, T), 0) // A_DIM
                    == lax.broadcasted_iota(I32, (A_HEADS * A_DIM, T), 1))
    head_sum = jnp.where(head_of_lane, 1.0, 0.0).astype(BF16)

    def head_norms2(t):
        t = t.astype(F32)
        return jnp.dot((t * t).astype(BF16), head_sum, preferred_element_type=F32)

    @pl.when(i == 0)
    def _():
        def widest(t, best):
            return jnp.maximum(best, head_norms2(k_ref[pl.ds(pl.multiple_of(t * T, T), T), :]))
        best = lax.fori_loop(0, k_ref.shape[0] // T, widest, jnp.zeros((T, T), F32))
        kmax_ref[...] = jnp.broadcast_to(jnp.max(best, axis=0, keepdims=True), kmax_ref.shape)

    bound2 = head_norms2(aq_ref[...]) * kmax_ref[0:1, :] * (NORM_SLACK / A_DIM)
    room = LOGIT_LIMIT - jnp.max(jnp.abs(bias_ref[...]))
    small_logits = (room > 0.0) & (jnp.max(bound2) <= room * room)

    ls_ref[...] = jnp.zeros(ls_ref.shape, F32)
    acc_ref[...] = jnp.zeros(acc_ref.shape, F32)

    def attend(s0, stage, near, final):
        madd = [mask_ref[s0], mask_ref[s0 + 1]]
        for hp in range(A_HEADS // 2):
            weights = []
            for h in (2 * hp, 2 * hp + 1):
                parts = [stage[h * T:(h + 1) * T, u * T:(u + 1) * T] + madd[u] for u in range(2)]
                if near:
                    parts = [parts[u] + bias_ref[h, 1 - u] for u in range(2)]
                if not final:
                    mx_ref[h] = jnp.maximum(mx_ref[h], jnp.maximum(parts[0], parts[1]))
                else:
                    es = [jnp.exp(part - mx_ref[h]) for part in parts]
                    ls_ref[h] = ls_ref[h] + (es[0] + es[1])
                    weights.append(jnp.concatenate(es, axis=1).astype(BF16))
            if final:
                ps = slice(hp * T, (hp + 1) * T)
                vw = jnp.concatenate([v_ref[tile_rows(s0), ps], v_ref[tile_rows(s0 + 1), ps]], axis=0)
                acc_ref[hp] = acc_ref[hp] + jnp.dot(jnp.concatenate(weights, axis=0), vw,
                                                    preferred_element_type=F32)

    def attention_sweep(final):
        sweep(qm_ref, k_ref,
              functools.partial(attend, near=False, final=final),
              functools.partial(attend, near=True, final=final))

    @pl.when(small_logits)
    def _():
        mx_ref[...] = jnp.zeros(mx_ref.shape, F32)

    @pl.when(jnp.logical_not(small_logits))
    def _():
        mx_ref[...] = jnp.full(mx_ref.shape, NEG_INF, F32)
        attention_sweep(False)
        for h in range(A_HEADS):
            mx_ref[h] = jnp.broadcast_to(jnp.max(mx_ref[h], axis=-1, keepdims=True), (T, T))

    attention_sweep(True)

    def row_total(h):
        return jnp.maximum(jnp.sum(ls_ref[h], axis=-1, keepdims=True), TINY)

    for hp in range(A_HEADS // 2):
        o0 = acc_ref[hp, :T, :] / row_total(2 * hp)
        o1 = acc_ref[hp, T:, :] / row_total(2 * hp + 1)
        g = ag_ref[:, hp * T:(hp + 1) * T].astype(F32)
        o_ref[:, hp * T:(hp + 1) * T] = (jnp.where(low, o0, o1) * _silu(g)).astype(o_ref.dtype)


def _dsa(pab, bias_tiles, topk):
    bsz, p, _ = pab.shape
    T = LANES
    nq = p // T
    hw = A_HEADS * A_DIM
    one = pl.Buffered(1)
    return pl.pallas_call(
        functools.partial(_dsa_kernel, topk=float(topk)),
        grid=(bsz, nq),
        in_specs=[
            pl.BlockSpec((None, T, hw), lambda b, i: (b, i, 0)),
            pl.BlockSpec((None, T, hw), lambda b, i: (b, i, 3)),
            pl.BlockSpec((None, T, hw), lambda b, i: (b, i, 4)),
            pl.BlockSpec((None, T, T), lambda b, i: (b, i, 33)),
            pl.BlockSpec((None, p, hw), lambda b, i: (b, 0, 1), pipeline_mode=one),
            pl.BlockSpec((None, p, hw), lambda b, i: (b, 0, 2), pipeline_mode=one),
            pl.BlockSpec((None, p, T), lambda b, i: (b, 0, 32), pipeline_mode=one),
            pl.BlockSpec((A_HEADS, 2, T, T), lambda b, i: (0, 0, 0, 0), pipeline_mode=one),
        ],
        out_specs=pl.BlockSpec((None, T, hw), lambda b, i: (b, i, 0)),
        out_shape=jax.ShapeDtypeStruct((bsz, p, hw), BF16),
        scratch_shapes=[
            pltpu.VMEM((nq + DUMMY, T, T), I32),
            pltpu.VMEM((nq + DUMMY, T, T), I32),
            pltpu.VMEM((nq + DUMMY, T, T), F32),
            pltpu.VMEM((A_HEADS * T, 2 * T), F32),
            pltpu.VMEM((A_HEADS * T, 2 * T), F32),
            pltpu.VMEM((1, IDX_HEADS * T, T), BF16),
            pltpu.VMEM((IDX_HEADS, T, T), F32),
            pltpu.VMEM((A_HEADS // 2, 2 * T, T), BF16),
            pltpu.VMEM((T, T), I32),
            pltpu.VMEM((8, T), F32),
            pltpu.VMEM((A_HEADS, T, T), F32),
            pltpu.VMEM((A_HEADS, T, T), F32),
            pltpu.VMEM((A_HEADS // 2, 2 * T, T), F32),
        ],
        compiler_params=_params(("parallel", "arbitrary")),
        name="dsa",
    )(pab, pab, pab, pab, pab, pab, pab, bias_tiles)


def _gla_kernel(q_ref, k_ref, v_ref, g_ref, a_ref, w2_ref, gb_ref, o_ref, st_ref, *, nchunk):
    C = CHUNK
    T = LANES

    @pl.when(pl.program_id(1) == 0)
    def _():
        st_ref[...] = jnp.zeros(st_ref.shape, F32)

    r_i = lax.broadcasted_iota(I32, (C, C), 0)
    c_i = lax.broadcasted_iota(I32, (C, C), 1)
    causal = c_i <= r_i
    tri = jnp.where(causal, 1.0, 0.0).astype(BF16)
    lane = lax.broadcasted_iota(I32, (C, T), 1)
    sd_r = lax.broadcasted_iota(I32, (2 * B_DV, T), 0)
    sd_c = lax.broadcasted_iota(I32, (2 * B_DV, T), 1)
    blockdiag = (sd_r >= B_DV) == (sd_c >= B_DK)

    def chunk(c, carry):
        r = pl.ds(pl.multiple_of(c * C, C), C)
        x = jnp.dot(a_ref[r, :], w2_ref[...], preferred_element_type=F32) + gb_ref[...]
        log_a = _softplus_parts(x)[1] * (1.0 / GLA_TAU)
        a1, a2, a3 = _split3(log_a)
        bcum = (jnp.dot(tri, a1, preferred_element_type=F32)
                + jnp.dot(tri, a2, preferred_element_type=F32)
                + jnp.dot(tri, a3, preferred_element_type=F32))
        b_last = bcum[C - 1:C, :]
        q = q_ref[r, :].astype(F32) * (B_DK ** -0.5)
        k = k_ref[r, :].astype(F32)
        q_t = (q * jnp.exp(bcum)).astype(BF16)
        k_t = (k * jnp.exp(-bcum)).astype(BF16)
        k_d = (k * jnp.exp(b_last - bcum)).astype(BF16)
        decay = jnp.exp(b_last)
        for hp in range(B_HEADS // 2):
            ls = slice(hp * T, (hp + 1) * T)
            st = st_ref[hp]
            o_inter = lax.dot_general(q_t[:, ls], st.astype(BF16), NT_DIMS,
                                      preferred_element_type=F32)
            for hh in range(2):
                h = 2 * hp + hh
                mine = (lane < B_DK) if hh == 0 else (lane >= B_DK)
                qm = jnp.where(mine, q_t[:, ls], jnp.zeros((C, T), BF16))
                att = lax.dot_general(qm, k_t[:, ls], NT_DIMS, preferred_element_type=F32)
                att = jnp.where(causal, att, 0.0)
                vs = slice(h * B_DV, (h + 1) * B_DV)
                o = jnp.dot(att.astype(BF16), v_ref[r, vs], preferred_element_type=F32)
                o = o + o_inter[:, hh * B_DV:(hh + 1) * B_DV]
                o = o * lax.rsqrt(jnp.mean(o * o, axis=-1, keepdims=True) + NORM_EPS)
                o = o * _silu(g_ref[r, vs].astype(F32))
                o_ref[r, vs] = o.astype(o_ref.dtype)
            vp = v_ref[r, hp * 2 * B_DV:(hp + 1) * 2 * B_DV]
            contrib = lax.dot_general(vp, k_d[:, ls], TN_DIMS, preferred_element_type=F32)
            st_ref[hp] = st * decay[:, ls] + jnp.where(blockdiag, contrib, 0.0)
        return carry

    lax.fori_loop(0, nchunk, chunk, 0, unroll=CHUNK_UNROLL)


def _gla(pab, w2, gb):
    bsz, p, _ = pab.shape
    tc = _pick_tile(p, (640, 128))
    hv = B_HEADS * B_DV
    hk = B_HEADS * B_DK
    return pl.pallas_call(
        functools.partial(_gla_kernel, nchunk=tc // CHUNK),
        grid=(bsz, p // tc),
        in_specs=[
            pl.BlockSpec((None, tc, hk), lambda b, c: (b, c, 14)),
            pl.BlockSpec((None, tc, hk), lambda b, c: (b, c, 15)),
            pl.BlockSpec((None, tc, hv), lambda b, c: (b, c, 5)),
            pl.BlockSpec((None, tc, hv), lambda b, c: (b, c, 6)),
            pl.BlockSpec((None, tc, LANES), lambda b, c: (b, c, 34)),
            pl.BlockSpec((LANES, hk), lambda b, c: (0, 0)),
            pl.BlockSpec((1, hk), lambda b, c: (0, 0)),
        ],
        out_specs=pl.BlockSpec((None, tc, hv), lambda b, c: (b, c, 0)),
        out_shape=jax.ShapeDtypeStruct((bsz, p, hv), BF16),
        scratch_shapes=[pltpu.VMEM((B_HEADS // 2, 2 * B_DV, 2 * B_DK), F32)],
        compiler_params=_params(("parallel", "arbitrary")),
        name="gla",
    )(pab, pab, pab, pab, pab, w2, gb)


def _ret_kernel(q_ref, k_ref, v_ref, g_ref, cos_ref, sin_ref, dmat_ref, zeta_ref, xi_ref,
                cdec_ref, o_ref, st_ref, *, nchunk):
    C = CHUNK
    T = LANES
    W = C_HEADS * C_DK

    @pl.when(pl.program_id(1) == 0)
    def _():
        st_ref[...] = jnp.zeros(st_ref.shape, F32)

    lane_w = lax.broadcasted_iota(I32, (C, W), 1)
    first_half = (lane_w & (C_DK - 1)) < (C_DK // 2)
    lane = lax.broadcasted_iota(I32, (C, T), 1)
    sd_r = lax.broadcasted_iota(I32, (2 * C_DV, T), 0)
    sd_c = lax.broadcasted_iota(I32, (2 * C_DV, T), 1)
    blockdiag = (sd_r >= C_DV) == (sd_c >= C_DK)

    def rotate(x, cos, sin_signed):
        swapped = jnp.where(first_half, pltpu.roll(x, W - C_DK // 2, 1), pltpu.roll(x, C_DK // 2, 1))
        return x * cos + swapped * sin_signed

    def chunk(c, carry):
        r = pl.ds(pl.multiple_of(c * C, C), C)
        cos = cos_ref[r, :]
        sin = sin_ref[r, :]
        q = rotate(q_ref[r, :].astype(F32), cos, sin)
        k = rotate(k_ref[r, :].astype(F32), cos, sin) * (C_DK ** -0.5)
        q_b = q.astype(BF16)
        k_b = k.astype(BF16)
        q_x = (q * xi_ref[...]).astype(BF16)
        k_z = (k * zeta_ref[...]).astype(BF16)
        for hp in range(C_HEADS // 2):
            ls = slice(hp * T, (hp + 1) * T)
            st = st_ref[hp]
            o_inter = lax.dot_general(q_x[:, ls], st.astype(BF16), NT_DIMS,
                                      preferred_element_type=F32)
            for hh in range(2):
                h = 2 * hp + hh
                mine = (lane < C_DK) if hh == 0 else (lane >= C_DK)
                qm = jnp.where(mine, q_b[:, ls], jnp.zeros((C, T), BF16))
                att = lax.dot_general(qm, k_b[:, ls], NT_DIMS, preferred_element_type=F32)
                att = att * dmat_ref[h]
                vs = slice(h * C_DV, (h + 1) * C_DV)
                o = jnp.dot(att.astype(BF16), v_ref[r, vs], preferred_element_type=F32)
                o = o + o_inter[:, hh * C_DV:(hh + 1) * C_DV]
                o = o - jnp.mean(o, axis=-1, keepdims=True)
                o = o * lax.rsqrt(jnp.mean(o * o, axis=-1, keepdims=True) + NORM_EPS)
                o = o * _silu(g_ref[r, vs].astype(F32))
                o_ref[r, vs] = o.astype(o_ref.dtype)
            vp = v_ref[r, hp * 2 * C_DV:(hp + 1) * 2 * C_DV]
            contrib = lax.dot_general(vp, k_z[:, ls], TN_DIMS, preferred_element_type=F32)
            st_ref[hp] = st * cdec_ref[:, ls] + jnp.where(blockdiag, contrib, 0.0)
        return carry

    lax.fori_loop(0, nchunk, chunk, 0, unroll=CHUNK_UNROLL)


def _retention_tables(p):
    log_gamma = np.log(1.0 - np.exp2(-5.0 - np.arange(C_HEADS, dtype=np.float64)))
    i = np.arange(CHUNK, dtype=np.float64)
    diff = i[:, None] - i[None, :]
    dmat = np.where(diff >= 0, np.exp(log_gamma[:, None, None] * np.maximum(diff, 0.0)), 0.0)
    zeta = np.exp(log_gamma[:, None] * (CHUNK - 1 - i))
    xi = np.exp(log_gamma[:, None] * (i + 1))
    cdec = np.exp(log_gamma * CHUNK)
    widen = lambda t: np.repeat(t.T[:, :, None], C_DK, axis=2).reshape(CHUNK, C_HEADS * C_DK)
    half = C_DK // 2
    inv = jnp.asarray(ROPE_BASE, F32) ** (-jnp.arange(half, dtype=F32) / half)
    ang = jnp.arange(p, dtype=jnp.int32).astype(F32)[:, None] * inv[None, :]
    cos = jnp.tile(jnp.cos(ang), (1, 2 * C_HEADS))
    sin = jnp.sin(ang)
    sin_signed = jnp.tile(jnp.concatenate([-sin, sin], axis=1), (1, C_HEADS))
    return (cos, sin_signed, jnp.asarray(dmat, F32), jnp.asarray(widen(zeta), F32),
            jnp.asarray(widen(xi), F32),
            jnp.asarray(np.repeat(cdec, C_DK)[None, :], F32))


def _retention(pcd, tables):
    bsz, p, _ = pcd.shape
    cos, sin, dmat, zeta, xi, cdec = tables
    tc = _pick_tile(p, (640, 128))
    hv = C_HEADS * C_DV
    hk = C_HEADS * C_DK
    full = lambda shape: pl.BlockSpec(shape, lambda b, c: (0,) * len(shape))
    return pl.pallas_call(
        functools.partial(_ret_kernel, nchunk=tc // CHUNK),
        grid=(bsz, p // tc),
        in_specs=[
            pl.BlockSpec((None, tc, hk), lambda b, c: (b, c, 0)),
            pl.BlockSpec((None, tc, hk), lambda b, c: (b, c, 1)),
            pl.BlockSpec((None, tc, hv), lambda b, c: (b, c, 1)),
            pl.BlockSpec((None, tc, hv), lambda b, c: (b, c, 2)),
            pl.BlockSpec((tc, hk), lambda b, c: (c, 0)),
            pl.BlockSpec((tc, hk), lambda b, c: (c, 0)),
            full((C_HEADS, CHUNK, CHUNK)),
            full((CHUNK, hk)),
            full((CHUNK, hk)),
            full((1, hk)),
        ],
        out_specs=pl.BlockSpec((None, tc, hv), lambda b, c: (b, c, 0)),
        out_shape=jax.ShapeDtypeStruct((bsz, p, hv), BF16),
        scratch_shapes=[pltpu.VMEM((C_HEADS // 2, 2 * C_DV, 2 * C_DK), F32)],
        compiler_params=_params(("parallel", "arbitrary")),
        name="retention",
    )(pcd, pcd, pcd, pcd, cos, sin, dmat, zeta, xi, cdec)


def _sb_kernel(q_ref, g_ref, k_ref, v_ref, o_ref, qm_ref, run_ref, acc_ref, hi_ref, lo_ref, ls_ref,
               w_ref):
    T = LANES
    i = pl.program_id(1)
    row = lax.broadcasted_iota(I32, (T, T), 0)
    col = lax.broadcasted_iota(I32, (T, T), 1)
    qpos = i * T + row
    low = col < D_DIM
    u_r = lax.broadcasted_iota(I32, (T, 2 * T), 0)
    u_c = lax.broadcasted_iota(I32, (T, 2 * T), 1)
    suffix = jnp.where((u_c >= T) | (u_r > u_c), 1.0, 0.0).astype(BF16)

    for h in range(D_HEADS):
        pair = q_ref[:, (h // 2) * T:(h // 2 + 1) * T]
        mine = low if h % 2 == 0 else ~low
        qm_ref[h] = jnp.where(mine, pair * (D_DIM ** -0.5), jnp.zeros_like(pair))
    run_ref[...] = jnp.zeros(run_ref.shape, F32)
    acc_ref[...] = jnp.zeros(acc_ref.shape, F32)

    def tile(state):
        t, _ = state
        kt = i - t
        ks = pl.ds(pl.multiple_of(kt * T, T), T)
        kpos = kt * T + col
        ok = (kpos < qpos) & (kpos >= PAD_FRONT)
        for h in range(D_HEADS):
            ps = slice((h // 2) * T, (h // 2 + 1) * T)
            z = lax.dot_general(qm_ref[h], k_ref[ks, ps], NT_DIMS, preferred_element_type=F32)
            sp, logsig = _softplus_parts(z)
            log_1m = jnp.where(ok, -sp, 0.0)
            hi = lax.bitcast_convert_type(lax.bitcast_convert_type(log_1m, I32) & BF16_BITS, F32)
            hi_ref[h] = hi.astype(BF16)
            lo_ref[h] = (log_1m - hi).astype(BF16)
            ls_ref[h] = jnp.where(ok, logsig, NEG_INF)
        slowest = None
        for h in range(D_HEADS):
            sums = (jnp.dot(hi_ref[h], suffix, preferred_element_type=F32)
                    + jnp.dot(lo_ref[h], suffix, preferred_element_type=F32))
            run = run_ref[h]
            w_ref[h] = jnp.exp(ls_ref[h] + run + sums[:, :T]).astype(BF16)
            run = run + sums[:, T:]
            run_ref[h] = run
            slowest = run if slowest is None else jnp.maximum(slowest, run)
        for h in range(D_HEADS):
            ps = slice((h // 2) * T, (h // 2 + 1) * T)
            acc_ref[h] = acc_ref[h] + jnp.dot(w_ref[h], v_ref[ks, ps],
                                              preferred_element_type=F32)
        return t + 1, (jnp.max(slowest) > LOG_F32_UNDERFLOW).astype(I32)

    lax.while_loop(lambda s: (s[0] <= i) & (s[1] > 0), tile, (jnp.int32(0), jnp.int32(1)))

    for hp in range(D_HEADS // 2):
        ps = slice(hp * T, (hp + 1) * T)
        o = jnp.where(low, acc_ref[2 * hp], acc_ref[2 * hp + 1]) * _silu(g_ref[:, ps].astype(F32))
        o_ref[:, ps] = o.astype(o_ref.dtype)


def _stick_breaking(pcd):
    bsz, p, _ = pcd.shape
    T = LANES
    hw = D_HEADS * D_DIM
    one = pl.Buffered(1)
    return pl.pallas_call(
        _sb_kernel,
        grid=(bsz, p // T),
        in_specs=[
            pl.BlockSpec((None, T, hw), lambda b, i: (b, i, 3)),
            pl.BlockSpec((None, T, hw), lambda b, i: (b, i, 6)),
            pl.BlockSpec((None, p, hw), lambda b, i: (b, 0, 4), pipeline_mode=one),
            pl.BlockSpec((None, p, hw), lambda b, i: (b, 0, 5), pipeline_mode=one),
        ],
        out_specs=pl.BlockSpec((None, T, hw), lambda b, i: (b, i, 0)),
        out_shape=jax.ShapeDtypeStruct((bsz, p, hw), BF16),
        scratch_shapes=[
            pltpu.VMEM((D_HEADS, T, T), BF16),
            pltpu.VMEM((D_HEADS, T, T), F32),
            pltpu.VMEM((D_HEADS, T, T), F32),
            pltpu.VMEM((D_HEADS, T, T), BF16),
            pltpu.VMEM((D_HEADS, T, T), BF16),
            pltpu.VMEM((D_HEADS, T, T), F32),
            pltpu.VMEM((D_HEADS, T, T), BF16),
        ],
        compiler_params=_params(("parallel", "arbitrary")),
        name="stick_breaking",
    )(pcd, pcd, pcd, pcd)


def _rel_bucket_np(rel):
    half = REL_BUCKETS // 2
    max_exact = half // 2
    n = -rel
    ret = np.where(n < 0, half, 0)
    n = np.abs(n)
    edges = [math.ceil(max_exact * (REL_MAX_DIST / max_exact) ** (j / (half - max_exact)) - 1e-9)
             for j in range(1, half - max_exact)]
    large = max_exact + sum((n >= e).astype(np.int64) for e in edges)
    return ret + np.where(n < max_exact, n, large)


def _bias_tiles(rel_bias):
    i = np.arange(LANES)[:, None]
    j = np.arange(LANES)[None, :]
    idx = np.stack([_rel_bucket_np(j - i - LANES * d) for d in range(3)])
    assert (idx[2] == idx[2, 0, 0]).all()
    table = rel_bias.astype(F32) - rel_bias.astype(F32)[idx[2, 0, 0]]
    near = jnp.asarray(idx[:2], I32)[None]
    tiles = jnp.zeros((table.shape[1],) + near.shape[1:], F32)
    for bucket in range(REL_BUCKETS):
        tiles = jnp.where(near == bucket, table[bucket][:, None, None, None], tiles)
    return tiles


def _layout_ab(w):
    aq, ak, av, ag, iq, ik, iw, bq, bk, bv, bg, ba = jnp.split(w, np.cumsum(SPLIT_AB)[:-1].tolist(), axis=1)
    pad = lambda t: jnp.pad(t, ((0, 0), (0, LANES - t.shape[1])))
    cols = [aq, ak, av, ag, iq, bv, bg, bq, bk, jnp.concatenate([ik, ik], axis=1), pad(iw), pad(ba)]
    return jnp.concatenate(cols, axis=1).astype(BF16)


def kernel(x, meta_tokens, rel_bias, norm_g, final_g, w_in_ab, gla_gate_w2, gla_gate_b, w_out_ab,
           w_in_cd, w_out_cd):
    bsz, seq, d = x.shape
    p = seq + PAD_FRONT + N_META
    depth = norm_g.shape[0]
    topk = min(TOPK_MAX, seq // 4)
    h = jnp.concatenate([jnp.zeros((bsz, PAD_FRONT, d), x.dtype),
                         jnp.broadcast_to(meta_tokens.astype(x.dtype)[None], (bsz, N_META, d)),
                         x], axis=1)
    bias_tiles = _bias_tiles(rel_bias)
    tables = _retention_tables(p)
    assert depth >= 1
    for layer in range(depth):
        j = layer // 2
        if layer % 2 == 0:
            pab = _inproj(h, norm_g[layer], _layout_ab(w_in_ab[j]))
            first = _dsa(pab, bias_tiles, topk)
            w2 = jnp.pad(gla_gate_w2[j], ((0, LANES - GLA_GATE_RANK), (0, 0))).astype(BF16)
            second = _gla(pab, w2, gla_gate_b[j].reshape(1, -1).astype(F32))
            w_out = w_out_ab[j]
        else:
            pcd = _inproj(h, norm_g[layer], w_in_cd[j].astype(BF16))
            first = _retention(pcd, tables)
            second = _stick_breaking(pcd)
            w_out = w_out_cd[j]
        if layer + 1 < depth:
            h = _outproj(first, second, w_out, h)
    return _outproj_norm(first, second, w_out, h, final_g, seq)
```

```python
import functools
import math

import numpy as np
import jax
import jax.numpy as jnp
from jax import lax
from jax.experimental import pallas as pl
from jax.experimental.pallas import tpu as pltpu

F32 = jnp.float32
BF16 = jnp.bfloat16
I32 = jnp.int32

D_MODEL = 1024
CHUNK = 64
N_META = 16
PAD_FRONT = 128 - N_META
NORM_EPS = 1e-6
NEG_INF = -1e30
A_HEADS, A_DIM = 8, 64
IDX_HEADS, IDX_DIM = 8, 64
IDX_SCALE = (IDX_DIM ** -0.5) * (IDX_HEADS ** -0.5)
TOPK_MAX = 256
B_HEADS, B_DK, B_DV = 4, 64, 128
GLA_GATE_RANK = 16
GLA_TAU = 16.0
C_HEADS, C_DK, C_DV = 4, 64, 128
ROPE_BASE = 10000.0
D_HEADS, D_DIM = 8, 64
REL_BUCKETS = 32
REL_MAX_DIST = 128
SPLIT_AB = (512, 512, 512, 512, 512, 64, 8, 256, 256, 512, 512, 16)
SPLIT_CD = (256, 256, 512, 512, 512, 512, 512, 512)

LANES = 128
W_AB_USED = 4480
W_AB_PAD = 4608
I32_MIN = -2 ** 31
I32_MAX = 2 ** 31 - 1
DUMMY = 7
IDX_BITS = 14
LOG_F32_UNDERFLOW = -104.0
LOGIT_LIMIT = 40.0
NORM_SLACK = 1.05
TINY = 1e-30
BF16_BITS = -65536
CHUNK_UNROLL = 2
VMEM_LIMIT = 56 * 1024 * 1024

NT_DIMS = (((1,), (1,)), ((), ()))
TN_DIMS = (((0,), (0,)), ((), ()))


def _pick_tile(n, candidates):
    for c in candidates:
        if n % c == 0:
            return c
    raise ValueError(f"no tile for {n}")


def _params(sem):
    return pltpu.CompilerParams(dimension_semantics=sem, vmem_limit_bytes=VMEM_LIMIT)


def _silu(x):
    return x / (1.0 + jnp.exp(-x))


def _softplus_parts(z):
    t = jnp.log(1.0 + jnp.exp(-jnp.abs(z)))
    return jnp.maximum(z, 0.0) + t, jnp.minimum(z, 0.0) - t


def _split3(x):
    a = x.astype(BF16)
    r = x - a.astype(F32)
    b = r.astype(BF16)
    c = (r - b.astype(F32)).astype(BF16)
    return a, b, c


def _inproj_kernel(x_ref, g_ref, w_ref, o_ref, hn_ref):
    @pl.when(pl.program_id(2) == 0)
    def _():
        x = x_ref[...]
        ms = jnp.mean(x * x, axis=-1, keepdims=True)
        hn_ref[...] = (x * lax.rsqrt(ms + NORM_EPS) * g_ref[...]).astype(BF16)

    o_ref[...] = jnp.dot(hn_ref[...], w_ref[...], preferred_element_type=F32).astype(o_ref.dtype)


def _inproj(h, g, w):
    bsz, p, d = h.shape
    n = w.shape[1]
    tp = _pick_tile(p, (1664, 1280, 640, 128))
    tn = _pick_tile(n, (1792, 1536, 896, 512, 128))
    return pl.pallas_call(
        _inproj_kernel,
        grid=(bsz, p // tp, n // tn),
        in_specs=[
            pl.BlockSpec((None, tp, d), lambda b, i, j: (b, i, 0)),
            pl.BlockSpec((1, d), lambda b, i, j: (0, 0)),
            pl.BlockSpec((d, tn), lambda b, i, j: (0, j)),
        ],
        out_specs=pl.BlockSpec((None, tp, tn), lambda b, i, j: (b, i, j)),
        out_shape=jax.ShapeDtypeStruct((bsz, p, n), BF16),
        scratch_shapes=[pltpu.VMEM((tp, d), BF16)],
        compiler_params=_params(("parallel", "parallel", "arbitrary")),
        name="inproj",
    )(h, g.reshape(1, d), w)


def _outproj_kernel(m1_ref, m2_ref, w1_ref, w2_ref, h_ref, o_ref, *, tp):
    y = jnp.dot(m1_ref[...], w1_ref[...], preferred_element_type=F32)
    y = y + jnp.dot(m2_ref[...], w2_ref[...], preferred_element_type=F32)
    pos = pl.program_id(1) * tp + lax.broadcasted_iota(I32, y.shape, 0)
    o_ref[...] = h_ref[...] + jnp.where(pos >= PAD_FRONT, y, 0.0)


def _outproj(m1, m2, w_out, h):
    bsz, p, d = h.shape
    k1 = m1.shape[-1]
    k2 = m2.shape[-1]
    tp = _pick_tile(p, (832, 640, 128))
    w = w_out.astype(BF16)
    return pl.pallas_call(
        functools.partial(_outproj_kernel, tp=tp),
        grid=(bsz, p // tp),
        in_specs=[
            pl.BlockSpec((None, tp, k1), lambda b, i: (b, i, 0)),
            pl.BlockSpec((None, tp, k2), lambda b, i: (b, i, 0)),
            pl.BlockSpec((k1, d), lambda b, i: (0, 0)),
            pl.BlockSpec((k2, d), lambda b, i: (0, 0)),
            pl.BlockSpec((None, tp, d), lambda b, i: (b, i, 0)),
        ],
        out_specs=pl.BlockSpec((None, tp, d), lambda b, i: (b, i, 0)),
        out_shape=jax.ShapeDtypeStruct((bsz, p, d), F32),
        compiler_params=_params(("parallel", "parallel")),
        name="outproj",
    )(m1, m2, w[:k1], w[k1:], h)


def _outproj_norm_kernel(m1_ref, m2_ref, w1_ref, w2_ref, h_ref, g_ref, o_ref):
    y = jnp.dot(m1_ref[...], w1_ref[...], preferred_element_type=F32)
    y = y + jnp.dot(m2_ref[...], w2_ref[...], preferred_element_type=F32)
    x = h_ref[...] + y
    ms = jnp.mean(x * x, axis=-1, keepdims=True)
    o_ref[...] = x * lax.rsqrt(ms + NORM_EPS) * g_ref[...]


def _outproj_norm(m1, m2, w_out, h, g, seq):
    bsz, p, d = h.shape
    k1 = m1.shape[-1]
    k2 = m2.shape[-1]
    tp = LANES
    skip = (p - seq) // tp
    assert skip * tp == p - seq
    w = w_out.astype(BF16)
    rows = lambda b, i: (b, i + skip, 0)
    return pl.pallas_call(
        _outproj_norm_kernel,
        grid=(bsz, seq // tp),
        in_specs=[
            pl.BlockSpec((None, tp, k1), rows),
            pl.BlockSpec((None, tp, k2), rows),
            pl.BlockSpec((k1, d), lambda b, i: (0, 0)),
            pl.BlockSpec((k2, d), lambda b, i: (0, 0)),
            pl.BlockSpec((None, tp, d), rows),
            pl.BlockSpec((1, d), lambda b, i: (0, 0)),
        ],
        out_specs=pl.BlockSpec((None, tp, d), lambda b, i: (b, i, 0)),
        out_shape=jax.ShapeDtypeStruct((bsz, seq, d), F32),
        compiler_params=_params(("parallel", "parallel")),
        name="outproj_norm",
    )(m1, m2, w[:k1], w[k1:], h, g.reshape(1, d))


def _dsa_kernel(aq_ref, ag_ref, iq_ref, iw_ref, k_ref, v_ref, ik_ref, bias_ref, o_ref,
                key_ref, pos_ref, mask_ref, sa_ref, sb_ref, iqs_ref, wts_ref, qm_ref,
                jdx_ref, kmax_ref, mx_ref, ls_ref, acc_ref, *, topk):
    T = LANES
    i = pl.program_id(1)
    last = i + DUMMY

    def num_groups(group):
        return (i + group) // group

    def first_slot(group):
        return last + 1 - group * num_groups(group)

    row = lax.broadcasted_iota(I32, (T, T), 0)
    col = lax.broadcasted_iota(I32, (T, T), 1)
    qchunk = (i * T + row) >> 6
    low = col < 64

    def admissible(kt):
        kpos = kt * T + col
        return (kpos >= PAD_FRONT) & ((kpos >> 6) <= qchunk)

    def tile_rows(slot):
        return pl.ds(pl.multiple_of(jnp.maximum(slot - DUMMY, 0) * T, T), T)

    def for_groups(group, body):
        first = first_slot(group)

        def step(g, carry):
            body(first + group * g)
            return carry

        lax.fori_loop(0, num_groups(group), step, 0)

    iw = iw_ref[...].astype(F32)
    for h in range(IDX_HEADS):
        pair = iq_ref[:, (h // 2) * T:(h // 2 + 1) * T]
        mine = low if h % 2 == 0 else ~low
        iqs_ref[0, h * T:(h + 1) * T, :] = jnp.where(mine, pair, jnp.zeros_like(pair))
        wts_ref[h] = jnp.broadcast_to(iw[:, h:h + 1], (T, T))
    for h in range(A_HEADS):
        pair = aq_ref[:, (h // 2) * T:(h // 2 + 1) * T]
        mine = low if h % 2 == 0 else ~low
        qm_ref[h // 2, (h % 2) * T:(h % 2 + 1) * T, :] = jnp.where(
            mine, pair * (A_DIM ** -0.5), jnp.zeros_like(pair))

    def sweep(lhs_ref, rhs_ref, consume, consume_last):
        group_rows = lhs_ref.shape[1]

        def issue(s0, stage):
            rows = [tile_rows(s0), tile_rows(s0 + 1)]
            for g in range(lhs_ref.shape[0]):
                ps = slice(g * T, (g + 1) * T)
                kw = jnp.concatenate([rhs_ref[rows[0], ps], rhs_ref[rows[1], ps]], axis=0)
                stage[g * group_rows:(g + 1) * group_rows, :] = lax.dot_general(
                    lhs_ref[g], kw, NT_DIMS, preferred_element_type=F32)

        first = first_slot(4)
        issue(first, sa_ref)

        def body(q, carry):
            s = first + 4 * q
            issue(s + 2, sb_ref)
            consume(s, sa_ref)
            issue(s + 4, sa_ref)
            consume(s + 2, sb_ref)
            return carry

        lax.fori_loop(0, num_groups(4) - 1, body, 0)
        issue(last - 1, sb_ref)
        consume(last - 3, sa_ref)
        consume_last(last - 1, sb_ref)

    def score_pair(s0, stage):
        for u in range(2):
            tot = jnp.zeros((T, T), F32)
            for h in range(IDX_HEADS):
                tot = tot + jnp.maximum(stage[h * T:(h + 1) * T, u * T:(u + 1) * T], 0.0) * wts_ref[h]
            score = jnp.where(admissible(s0 + u - DUMMY), tot * IDX_SCALE, NEG_INF)
            bits = lax.bitcast_convert_type(score, I32)
            key_ref[s0 + u] = bits ^ ((bits >> 31) & 0x7FFFFFFF)

    sweep(iqs_ref, ik_ref, score_pair, score_pair)
    for d in range(DUMMY):
        key_ref[d] = jnp.full((T, T), I32_MIN, I32)

    def row_count(pred):
        one = jnp.ones((T, T), I32)
        zero = jnp.zeros((T, T), I32)
        first = first_slot(8)

        def body(g, acc):
            hits = [jnp.where(pred(first + 8 * g + u), one, zero) for u in range(8)]
            while len(hits) > 1:
                hits = [a + b for a, b in zip(hits[::2], hits[1::2])]
            return acc + hits[0]

        acc = lax.fori_loop(0, num_groups(8), body, zero)
        return jnp.sum(acc.astype(F32), axis=-1, keepdims=True)

    def value_bit(it, t):
        cand = t + (jnp.int32(1) << (31 - it))
        wide = jnp.broadcast_to(cand, (T, T))
        cnt = row_count(lambda s: key_ref[s] >= wide)
        return jnp.where(cnt >= topk, cand, t)

    thr = jnp.broadcast_to(lax.fori_loop(0, 32, value_bit, jnp.full((T, 1), I32_MIN, I32)), (T, T))

    need = topk - row_count(lambda s: key_ref[s] > thr)
    ties = row_count(lambda s: key_ref[s] == thr)
    jdx_ref[...] = jnp.full((T, T), 2 ** IDX_BITS, I32)

    @pl.when(jnp.max(jnp.where(ties > need, 1.0, 0.0)) > 0.0)
    def _():
        def tie_positions(s0):
            for s in [s0 + u for u in range(4)]:
                pos_ref[s] = jnp.where(key_ref[s] == thr, (s - DUMMY) * T + col, I32_MAX)

        for_groups(4, tie_positions)
        for d in range(DUMMY):
            pos_ref[d] = jnp.full((T, T), I32_MAX, I32)

        def index_bit(bit, j):
            cand = j + (jnp.int32(1) << bit)
            wide = jnp.broadcast_to(cand, (T, T))
            cnt = row_count(lambda s: pos_ref[s] < wide)
            return jnp.where(cnt < need, cand, j)

        nbits = 7 + sum((i >= (1 << k)).astype(I32) for k in range(7))
        j = lax.fori_loop(0, nbits, lambda it, j: index_bit(nbits - 1 - it, j),
                          jnp.zeros((T, 1), I32))
        jdx_ref[...] = jnp.broadcast_to(j, (T, T))

    jdx = jdx_ref[...]

    def to_mask(s0):
        for s in [s0 + u for u in range(4)]:
            key = key_ref[s]
            sel = (key > thr) | ((key == thr) & ((s - DUMMY) * T + col <= jdx))
            mask_ref[s] = jnp.where(sel & admissible(s - DUMMY), 0.0, NEG_INF)

    for_groups(4, to_mask)

    head_of_lane = (lax.broadcasted_iota(I32, (A_HEADS * A_DIM, T), 0) // A_DIM
                    == lax.broadcasted_iota(I32, (A_HEADS * A_DIM, T), 1))
    head_sum = jnp.where(head_of_lane, 1.0, 0.0).astype(BF16)

    def head_norms2(t):
        t = t.astype(F32)
        return jnp.dot((t * t).astype(BF16), head_sum, preferred_element_type=F32)

    @pl.when(i == 0)
    def _():
        def widest(t, best):
            return jnp.maximum(best, head_norms2(k_ref[pl.ds(pl.multiple_of(t * T, T), T), :]))
        best = lax.fori_loop(0, k_ref.shape[0] // T, widest, jnp.zeros((T, T), F32))
        kmax_ref[...] = jnp.broadcast_to(jnp.max(best, axis=0, keepdims=True), kmax_ref.shape)

    bound2 = head_norms2(aq_ref[...]) * kmax_ref[0:1, :] * (NORM_SLACK / A_DIM)
    room = LOGIT_LIMIT - jnp.max(jnp.abs(bias_ref[...]))
    small_logits = (room > 0.0) & (jnp.max(bound2) <= room * room)

    ls_ref[...] = jnp.zeros(ls_ref.shape, F32)
    acc_ref[...] = jnp.zeros(acc_ref.shape, F32)

    def attend(s0, stage, near, final):
        madd = [mask_ref[s0], mask_ref[s0 + 1]]
        for hp in range(A_HEADS // 2):
            weights = []
            for h in (2 * hp, 2 * hp + 1):
                parts = [stage[h * T:(h + 1) * T, u * T:(u + 1) * T] + madd[u] for u in range(2)]
                if near:
                    parts = [parts[u] + bias_ref[h, 1 - u] for u in range(2)]
                if not final:
                    mx_ref[h] = jnp.maximum(mx_ref[h], jnp.maximum(parts[0], parts[1]))
                else:
                    es = [jnp.exp(part - mx_ref[h]) for part in parts]
                    ls_ref[h] = ls_ref[h] + (es[0] + es[1])
                    weights.append(jnp.concatenate(es, axis=1).astype(BF16))
            if final:
                ps = slice(hp * T, (hp + 1) * T)
                vw = jnp.concatenate([v_ref[tile_rows(s0), ps], v_ref[tile_rows(s0 + 1), ps]], axis=0)
                acc_ref[hp] = acc_ref[hp] + jnp.dot(jnp.concatenate(weights, axis=0), vw,
                                                    preferred_element_type=F32)

    def attention_sweep(final):
        sweep(qm_ref, k_ref,
              functools.partial(attend, near=False, final=final),
              functools.partial(attend, near=True, final=final))

    @pl.when(small_logits)
    def _():
        mx_ref[...] = jnp.zeros(mx_ref.shape, F32)

    @pl.when(jnp.logical_not(small_logits))
    def _():
        mx_ref[...] = jnp.full(mx_ref.shape, NEG_INF, F32)
        attention_sweep(False)
        for h in range(A_HEADS):
            mx_ref[h] = jnp.broadcast_to(jnp.max(mx_ref[h], axis=-1, keepdims=True), (T, T))

    attention_sweep(True)

    def row_total(h):
        return jnp.maximum(jnp.sum(ls_ref[h], axis=-1, keepdims=True), TINY)

    for hp in range(A_HEADS // 2):
        o0 = acc_ref[hp, :T, :] / row_total(2 * hp)
        o1 = acc_ref[hp, T:, :] / row_total(2 * hp + 1)
        g = ag_ref[:, hp * T:(hp + 1) * T].astype(F32)
        o_ref[:, hp * T:(hp + 1) * T] = (jnp.where(low, o0, o1) * _silu(g)).astype(o_ref.dtype)


def _dsa(pab, bias_tiles, topk):
    bsz, p, _ = pab.shape
    T = LANES
    nq = p // T
    hw = A_HEADS * A_DIM
    one = pl.Buffered(1)
    return pl.pallas_call(
        functools.partial(_dsa_kernel, topk=float(topk)),
        grid=(bsz, nq),
        in_specs=[
            pl.BlockSpec((None, T, hw), lambda b, i: (b, i, 0)),
            pl.BlockSpec((None, T, hw), lambda b, i: (b, i, 3)),
            pl.BlockSpec((None, T, hw), lambda b, i: (b, i, 4)),
            pl.BlockSpec((None, T, T), lambda b, i: (b, i, 33)),
            pl.BlockSpec((None, p, hw), lambda b, i: (b, 0, 1), pipeline_mode=one),
            pl.BlockSpec((None, p, hw), lambda b, i: (b, 0, 2), pipeline_mode=one),
            pl.BlockSpec((None, p, T), lambda b, i: (b, 0, 32), pipeline_mode=one),
            pl.BlockSpec((A_HEADS, 2, T, T), lambda b, i: (0, 0, 0, 0), pipeline_mode=one),
        ],
        out_specs=pl.BlockSpec((None, T, hw), lambda b, i: (b, i, 0)),
        out_shape=jax.ShapeDtypeStruct((bsz, p, hw), BF16),
        scratch_shapes=[
            pltpu.VMEM((nq + DUMMY, T, T), I32),
            pltpu.VMEM((nq + DUMMY, T, T), I32),
            pltpu.VMEM((nq + DUMMY, T, T), F32),
            pltpu.VMEM((A_HEADS * T, 2 * T), F32),
            pltpu.VMEM((A_HEADS * T, 2 * T), F32),
            pltpu.VMEM((1, IDX_HEADS * T, T), BF16),
            pltpu.VMEM((IDX_HEADS, T, T), F32),
            pltpu.VMEM((A_HEADS // 2, 2 * T, T), BF16),
            pltpu.VMEM((T, T), I32),
            pltpu.VMEM((8, T), F32),
            pltpu.VMEM((A_HEADS, T, T), F32),
            pltpu.VMEM((A_HEADS, T, T), F32),
            pltpu.VMEM((A_HEADS // 2, 2 * T, T), F32),
        ],
        compiler_params=_params(("parallel", "arbitrary")),
        name="dsa",
    )(pab, pab, pab, pab, pab, pab, pab, bias_tiles)


def _gla_kernel(q_ref, k_ref, v_ref, g_ref, a_ref, w2_ref, gb_ref, o_ref, st_ref, *, nchunk):
    C = CHUNK
    T = LANES

    @pl.when(pl.program_id(1) == 0)
    def _():
        st_ref[...] = jnp.zeros(st_ref.shape, F32)

    r_i = lax.broadcasted_iota(I32, (C, C), 0)
    c_i = lax.broadcasted_iota(I32, (C, C), 1)
    causal = c_i <= r_i
    tri = jnp.where(causal, 1.0, 0.0).astype(BF16)
    lane = lax.broadcasted_iota(I32, (C, T), 1)
    sd_r = lax.broadcasted_iota(I32, (2 * B_DV, T), 0)
    sd_c = lax.broadcasted_iota(I32, (2 * B_DV, T), 1)
    blockdiag = (sd_r >= B_DV) == (sd_c >= B_DK)

    def chunk(c, carry):
        r = pl.ds(pl.multiple_of(c * C, C), C)
        x = jnp.dot(a_ref[r, :], w2_ref[...], preferred_element_type=F32) + gb_ref[...]
        log_a = _softplus_parts(x)[1] * (1.0 / GLA_TAU)
        a1, a2, a3 = _split3(log_a)
        bcum = (jnp.dot(tri, a1, preferred_element_type=F32)
                + jnp.dot(tri, a2, preferred_element_type=F32)
                + jnp.dot(tri, a3, preferred_element_type=F32))
        b_last = bcum[C - 1:C, :]
        q = q_ref[r, :].astype(F32) * (B_DK ** -0.5)
        k = k_ref[r, :].astype(F32)
        q_t = (q * jnp.exp(bcum)).astype(BF16)
        k_t = (k * jnp.exp(-bcum)).astype(BF16)
        k_d = (k * jnp.exp(b_last - bcum)).astype(BF16)
        decay = jnp.exp(b_last)
        for hp in range(B_HEADS // 2):
            ls = slice(hp * T, (hp + 1) * T)
            st = st_ref[hp]
            o_inter = lax.dot_general(q_t[:, ls], st.astype(BF16), NT_DIMS,
                                      preferred_element_type=F32)
            for hh in range(2):
                h = 2 * hp + hh
                mine = (lane < B_DK) if hh == 0 else (lane >= B_DK)
                qm = jnp.where(mine, q_t[:, ls], jnp.zeros((C, T), BF16))
                att = lax.dot_general(qm, k_t[:, ls], NT_DIMS, preferred_element_type=F32)
                att = jnp.where(causal, att, 0.0)
                vs = slice(h * B_DV, (h + 1) * B_DV)
                o = jnp.dot(att.astype(BF16), v_ref[r, vs], preferred_element_type=F32)
                o = o + o_inter[:, hh * B_DV:(hh + 1) * B_DV]
                o = o * lax.rsqrt(jnp.mean(o * o, axis=-1, keepdims=True) + NORM_EPS)
                o = o * _silu(g_ref[r, vs].astype(F32))
                o_ref[r, vs] = o.astype(o_ref.dtype)
            vp = v_ref[r, hp * 2 * B_DV:(hp + 1) * 2 * B_DV]
            contrib = lax.dot_general(vp, k_d[:, ls], TN_DIMS, preferred_element_type=F32)
            st_ref[hp] = st * decay[:, ls] + jnp.where(blockdiag, contrib, 0.0)
        return carry

    lax.fori_loop(0, nchunk, chunk, 0, unroll=CHUNK_UNROLL)


def _gla(pab, w2, gb):
    bsz, p, _ = pab.shape
    tc = _pick_tile(p, (640, 128))
    hv = B_HEADS * B_DV
    hk = B_HEADS * B_DK
    return pl.pallas_call(
        functools.partial(_gla_kernel, nchunk=tc // CHUNK),
        grid=(bsz, p // tc),
        in_specs=[
            pl.BlockSpec((None, tc, hk), lambda b, c: (b, c, 14)),
            pl.BlockSpec((None, tc, hk), lambda b, c: (b, c, 15)),
            pl.BlockSpec((None, tc, hv), lambda b, c: (b, c, 5)),
            pl.BlockSpec((None, tc, hv), lambda b, c: (b, c, 6)),
            pl.BlockSpec((None, tc, LANES), lambda b, c: (b, c, 34)),
            pl.BlockSpec((LANES, hk), lambda b, c: (0, 0)),
            pl.BlockSpec((1, hk), lambda b, c: (0, 0)),
        ],
        out_specs=pl.BlockSpec((None, tc, hv), lambda b, c: (b, c, 0)),
        out_shape=jax.ShapeDtypeStruct((bsz, p, hv), BF16),
        scratch_shapes=[pltpu.VMEM((B_HEADS // 2, 2 * B_DV, 2 * B_DK), F32)],
        compiler_params=_params(("parallel", "arbitrary")),
        name="gla",
    )(pab, pab, pab, pab, pab, w2, gb)


def _ret_kernel(q_ref, k_ref, v_ref, g_ref, cos_ref, sin_ref, dmat_ref, zeta_ref, xi_ref,
                cdec_ref, o_ref, st_ref, *, nchunk):
    C = CHUNK
    T = LANES
    W = C_HEADS * C_DK

    @pl.when(pl.program_id(1) == 0)
    def _():
        st_ref[...] = jnp.zeros(st_ref.shape, F32)

    lane_w = lax.broadcasted_iota(I32, (C, W), 1)
    first_half = (lane_w & (C_DK - 1)) < (C_DK // 2)
    lane = lax.broadcasted_iota(I32, (C, T), 1)
    sd_r = lax.broadcasted_iota(I32, (2 * C_DV, T), 0)
    sd_c = lax.broadcasted_iota(I32, (2 * C_DV, T), 1)
    blockdiag = (sd_r >= C_DV) == (sd_c >= C_DK)

    def rotate(x, cos, sin_signed):
        swapped = jnp.where(first_half, pltpu.roll(x, W - C_DK // 2, 1), pltpu.roll(x, C_DK // 2, 1))
        return x * cos + swapped * sin_signed

    def chunk(c, carry):
        r = pl.ds(pl.multiple_of(c * C, C), C)
        cos = cos_ref[r, :]
        sin = sin_ref[r, :]
        q = rotate(q_ref[r, :].astype(F32), cos, sin)
        k = rotate(k_ref[r, :].astype(F32), cos, sin) * (C_DK ** -0.5)
        q_b = q.astype(BF16)
        k_b = k.astype(BF16)
        q_x = (q * xi_ref[...]).astype(BF16)
        k_z = (k * zeta_ref[...]).astype(BF16)
        for hp in range(C_HEADS // 2):
            ls = slice(hp * T, (hp + 1) * T)
            st = st_ref[hp]
            o_inter = lax.dot_general(q_x[:, ls], st.astype(BF16), NT_DIMS,
                                      preferred_element_type=F32)
            for hh in range(2):
                h = 2 * hp + hh
                mine = (lane < C_DK) if hh == 0 else (lane >= C_DK)
                qm = jnp.where(mine, q_b[:, ls], jnp.zeros((C, T), BF16))
                att = lax.dot_general(qm, k_b[:, ls], NT_DIMS, preferred_element_type=F32)
                att = att * dmat_ref[h]
                vs = slice(h * C_DV, (h + 1) * C_DV)
                o = jnp.dot(att.astype(BF16), v_ref[r, vs], preferred_element_type=F32)
                o = o + o_inter[:, hh * C_DV:(hh + 1) * C_DV]
                o = o - jnp.mean(o, axis=-1, keepdims=True)
                o = o * lax.rsqrt(jnp.mean(o * o, axis=-1, keepdims=True) + NORM_EPS)
                o = o * _silu(g_ref[r, vs].astype(F32))
                o_ref[r, vs] = o.astype(o_ref.dtype)
            vp = v_ref[r, hp * 2 * C_DV:(hp + 1) * 2 * C_DV]
            contrib = lax.dot_general(vp, k_z[:, ls], TN_DIMS, preferred_element_type=F32)
            st_ref[hp] = st * cdec_ref[:, ls] + jnp.where(blockdiag, contrib, 0.0)
        return carry

    lax.fori_loop(0, nchunk, chunk, 0, unroll=CHUNK_UNROLL)


def _retention_tables(p):
    log_gamma = np.log(1.0 - np.exp2(-5.0 - np.arange(C_HEADS, dtype=np.float64)))
    i = np.arange(CHUNK, dtype=np.float64)
    diff = i[:, None] - i[None, :]
    dmat = np.where(diff >= 0, np.exp(log_gamma[:, None, None] * np.maximum(diff, 0.0)), 0.0)
    zeta = np.exp(log_gamma[:, None] * (CHUNK - 1 - i))
    xi = np.exp(log_gamma[:, None] * (i + 1))
    cdec = np.exp(log_gamma * CHUNK)
    widen = lambda t: np.repeat(t.T[:, :, None], C_DK, axis=2).reshape(CHUNK, C_HEADS * C_DK)
    half = C_DK // 2
    inv = jnp.asarray(ROPE_BASE, F32) ** (-jnp.arange(half, dtype=F32) / half)
    ang = jnp.arange(p, dtype=jnp.int32).astype(F32)[:, None] * inv[None, :]
    cos = jnp.tile(jnp.cos(ang), (1, 2 * C_HEADS))
    sin = jnp.sin(ang)
    sin_signed = jnp.tile(jnp.concatenate([-sin, sin], axis=1), (1, C_HEADS))
    return (cos, sin_signed, jnp.asarray(dmat, F32), jnp.asarray(widen(zeta), F32),
            jnp.asarray(widen(xi), F32),
            jnp.asarray(np.repeat(cdec, C_DK)[None, :], F32))


def _retention(pcd, tables):
    bsz, p, _ = pcd.shape
    cos, sin, dmat, zeta, xi, cdec = tables
    tc = _pick_tile(p, (640, 128))
    hv = C_HEADS * C_DV
    hk = C_HEADS * C_DK
    full = lambda shape: pl.BlockSpec(shape, lambda b, c: (0,) * len(shape))
    return pl.pallas_call(
        functools.partial(_ret_kernel, nchunk=tc // CHUNK),
        grid=(bsz, p // tc),
        in_specs=[
            pl.BlockSpec((None, tc, hk), lambda b, c: (b, c, 0)),
            pl.BlockSpec((None, tc, hk), lambda b, c: (b, c, 1)),
            pl.BlockSpec((None, tc, hv), lambda b, c: (b, c, 1)),
            pl.BlockSpec((None, tc, hv), lambda b, c: (b, c, 2)),
            pl.BlockSpec((tc, hk), lambda b, c: (c, 0)),
            pl.BlockSpec((tc, hk), lambda b, c: (c, 0)),
            full((C_HEADS, CHUNK, CHUNK)),
            full((CHUNK, hk)),
            full((CHUNK, hk)),
            full((1, hk)),
        ],
        out_specs=pl.BlockSpec((None, tc, hv), lambda b, c: (b, c, 0)),
        out_shape=jax.ShapeDtypeStruct((bsz, p, hv), BF16),
        scratch_shapes=[pltpu.VMEM((C_HEADS // 2, 2 * C_DV, 2 * C_DK), F32)],
        compiler_params=_params(("parallel", "arbitrary")),
        name="retention",
    )(pcd, pcd, pcd, pcd, cos, sin, dmat, zeta, xi, cdec)


def _sb_kernel(q_ref, g_ref, k_ref, v_ref, o_ref, qm_ref, run_ref, acc_ref, hi_ref, lo_ref, ls_ref,
               w_ref):
    T = LANES
    i = pl.program_id(1)
    row = lax.broadcasted_iota(I32, (T, T), 0)
    col = lax.broadcasted_iota(I32, (T, T), 1)
    qpos = i * T + row
    low = col < D_DIM
    u_r = lax.broadcasted_iota(I32, (T, 2 * T), 0)
    u_c = lax.broadcasted_iota(I32, (T, 2 * T), 1)
    suffix = jnp.where((u_c >= T) | (u_r > u_c), 1.0, 0.0).astype(BF16)

    for h in range(D_HEADS):
        pair = q_ref[:, (h // 2) * T:(h // 2 + 1) * T]
        mine = low if h % 2 == 0 else ~low
        qm_ref[h] = jnp.where(mine, pair * (D_DIM ** -0.5), jnp.zeros_like(pair))
    run_ref[...] = jnp.zeros(run_ref.shape, F32)
    acc_ref[...] = jnp.zeros(acc_ref.shape, F32)

    def tile(state):
        t, _ = state
        kt = i - t
        ks = pl.ds(pl.multiple_of(kt * T, T), T)
        kpos = kt * T + col
        ok = (kpos < qpos) & (kpos >= PAD_FRONT)
        for h in range(D_HEADS):
            ps = slice((h // 2) * T, (h // 2 + 1) * T)
            z = lax.dot_general(qm_ref[h], k_ref[ks, ps], NT_DIMS, preferred_element_type=F32)
            sp, logsig = _softplus_parts(z)
            log_1m = jnp.where(ok, -sp, 0.0)
            hi = lax.bitcast_convert_type(lax.bitcast_convert_type(log_1m, I32) & BF16_BITS, F32)
            hi_ref[h] = hi.astype(BF16)
            lo_ref[h] = (log_1m - hi).astype(BF16)
            ls_ref[h] = jnp.where(ok, logsig, NEG_INF)
        slowest = None
        for h in range(D_HEADS):
            sums = (jnp.dot(hi_ref[h], suffix, preferred_element_type=F32)
                    + jnp.dot(lo_ref[h], suffix, preferred_element_type=F32))
            run = run_ref[h]
            w_ref[h] = jnp.exp(ls_ref[h] + run + sums[:, :T]).astype(BF16)
            run = run + sums[:, T:]
            run_ref[h] = run
            slowest = run if slowest is None else jnp.maximum(slowest, run)
        for h in range(D_HEADS):
            ps = slice((h // 2) * T, (h // 2 + 1) * T)
            acc_ref[h] = acc_ref[h] + jnp.dot(w_ref[h], v_ref[ks, ps],
                                              preferred_element_type=F32)
        return t + 1, (jnp.max(slowest) > LOG_F32_UNDERFLOW).astype(I32)

    lax.while_loop(lambda s: (s[0] <= i) & (s[1] > 0), tile, (jnp.int32(0), jnp.int32(1)))

    for hp in range(D_HEADS // 2):
        ps = slice(hp * T, (hp + 1) * T)
        o = jnp.where(low, acc_ref[2 * hp], acc_ref[2 * hp + 1]) * _silu(g_ref[:, ps].astype(F32))
        o_ref[:, ps] = o.astype(o_ref.dtype)


def _stick_breaking(pcd):
    bsz, p, _ = pcd.shape
    T = LANES
    hw = D_HEADS * D_DIM
    one = pl.Buffered(1)
    return pl.pallas_call(
        _sb_kernel,
        grid=(bsz, p // T),
        in_specs=[
            pl.BlockSpec((None, T, hw), lambda b, i: (b, i, 3)),
            pl.BlockSpec((None, T, hw), lambda b, i: (b, i, 6)),
            pl.BlockSpec((None, p, hw), lambda b, i: (b, 0, 4), pipeline_mode=one),
            pl.BlockSpec((None, p, hw), lambda b, i: (b, 0, 5), pipeline_mode=one),
        ],
        out_specs=pl.BlockSpec((None, T, hw), lambda b, i: (b, i, 0)),
        out_shape=jax.ShapeDtypeStruct((bsz, p, hw), BF16),
        scratch_shapes=[
            pltpu.VMEM((D_HEADS, T, T), BF16),
            pltpu.VMEM((D_HEADS, T, T), F32),
            pltpu.VMEM((D_HEADS, T, T), F32),
            pltpu.VMEM((D_HEADS, T, T), BF16),
            pltpu.VMEM((D_HEADS, T, T), BF16),
            pltpu.VMEM((D_HEADS, T, T), F32),
            pltpu.VMEM((D_HEADS, T, T), BF16),
        ],
        compiler_params=_params(("parallel", "arbitrary")),
        name="stick_breaking",
    )(pcd, pcd, pcd, pcd)


def _rel_bucket_np(rel):
    half = REL_BUCKETS // 2
    max_exact = half // 2
    n = -rel
    ret = np.where(n < 0, half, 0)
    n = np.abs(n)
    edges = [math.ceil(max_exact * (REL_MAX_DIST / max_exact) ** (j / (half - max_exact)) - 1e-9)
             for j in range(1, half - max_exact)]
    large = max_exact + sum((n >= e).astype(np.int64) for e in edges)
    return ret + np.where(n < max_exact, n, large)


def _bias_tiles(rel_bias):
    i = np.arange(LANES)[:, None]
    j = np.arange(LANES)[None, :]
    idx = np.stack([_rel_bucket_np(j - i - LANES * d) for d in range(3)])
    assert (idx[2] == idx[2, 0, 0]).all()
    table = rel_bias.astype(F32) - rel_bias.astype(F32)[idx[2, 0, 0]]
    near = jnp.asarray(idx[:2], I32)[None]
    tiles = jnp.zeros((table.shape[1],) + near.shape[1:], F32)
    for bucket in range(REL_BUCKETS):
        tiles = jnp.where(near == bucket, table[bucket][:, None, None, None], tiles)
    return tiles


def _layout_ab(w):
    aq, ak, av, ag, iq, ik, iw, bq, bk, bv, bg, ba = jnp.split(w, np.cumsum(SPLIT_AB)[:-1].tolist(), axis=1)
    pad = lambda t: jnp.pad(t, ((0, 0), (0, LANES - t.shape[1])))
    cols = [aq, ak, av, ag, iq, bv, bg, bq, bk, jnp.concatenate([ik, ik], axis=1), pad(iw), pad(ba),
            jnp.zeros((w.shape[0], W_AB_PAD - W_AB_USED), w.dtype)]
    return jnp.concatenate(cols, axis=1).astype(BF16)


def kernel(x, meta_tokens, rel_bias, norm_g, final_g, w_in_ab, gla_gate_w2, gla_gate_b, w_out_ab,
           w_in_cd, w_out_cd):
    bsz, seq, d = x.shape
    p = seq + PAD_FRONT + N_META
    depth = norm_g.shape[0]
    topk = min(TOPK_MAX, seq // 4)
    h = jnp.concatenate([jnp.zeros((bsz, PAD_FRONT, d), x.dtype),
                         jnp.broadcast_to(meta_tokens.astype(x.dtype)[None], (bsz, N_META, d)),
                         x], axis=1)
    bias_tiles = _bias_tiles(rel_bias)
    tables = _retention_tables(p)
    assert depth >= 1
    for layer in range(depth):
        j = layer // 2
        if layer % 2 == 0:
            pab = _inproj(h, norm_g[layer], _layout_ab(w_in_ab[j]))
            first = _dsa(pab, bias_tiles, topk)
            w2 = jnp.pad(gla_gate_w2[j], ((0, LANES - GLA_GATE_RANK), (0, 0))).astype(BF16)
            second = _gla(pab, w2, gla_gate_b[j].reshape(1, -1).astype(F32))
            w_out = w_out_ab[j]
        else:
            pcd = _inproj(h, norm_g[layer], w_in_cd[j].astype(BF16))
            first = _retention(pcd, tables)
            second = _stick_breaking(pcd)
            w_out = w_out_cd[j]
        if layer + 1 < depth:
            h = _outproj(first, second, w_out, h)
    return _outproj_norm(first, second, w_out, h, final_g, seq)
```

```python
import functools
import math

import numpy as np
import jax
import jax.numpy as jnp
from jax import lax
from jax.experimental import pallas as pl
from jax.experimental.pallas import tpu as pltpu

F32 = jnp.float32
BF16 = jnp.bfloat16
I32 = jnp.int32

D_MODEL = 1024
CHUNK = 64
N_META = 16
PAD_FRONT = 128 - N_META
NORM_EPS = 1e-6
NEG_INF = -1e30
A_HEADS, A_DIM = 8, 64
IDX_HEADS, IDX_DIM = 8, 64
IDX_SCALE = (IDX_DIM ** -0.5) * (IDX_HEADS ** -0.5)
TOPK_MAX = 256
B_HEADS, B_DK, B_DV = 4, 64, 128
GLA_GATE_RANK = 16
GLA_TAU = 16.0
C_HEADS, C_DK, C_DV = 4, 64, 128
ROPE_BASE = 10000.0
D_HEADS, D_DIM = 8, 64
REL_BUCKETS = 32
REL_MAX_DIST = 128
SPLIT_AB = (512, 512, 512, 512, 512, 64, 8, 256, 256, 512, 512, 16)
SPLIT_CD = (256, 256, 512, 512, 512, 512, 512, 512)

LANES = 128
W_AB_USED = 4480
W_AB_PAD = 4608
I32_MIN = -2 ** 31
I32_MAX = 2 ** 31 - 1
DUMMY = 7
IDX_BITS = 14
LOG_F32_UNDERFLOW = -104.0
LOGIT_LIMIT = 40.0
NORM_SLACK = 1.05
TINY = 1e-30
BF16_BITS = -65536
CHUNK_UNROLL = 2
VMEM_LIMIT = 56 * 1024 * 1024

NT_DIMS = (((1,), (1,)), ((), ()))
TN_DIMS = (((0,), (0,)), ((), ()))


def _pick_tile(n, candidates):
    for c in candidates:
        if n % c == 0:
            return c
    raise ValueError(f"no tile for {n}")


def _params(sem):
    return pltpu.CompilerParams(dimension_semantics=sem, vmem_limit_bytes=VMEM_LIMIT)


def _silu(x):
    return x / (1.0 + jnp.exp(-x))


def _softplus_parts(z):
    t = jnp.log(1.0 + jnp.exp(-jnp.abs(z)))
    return jnp.maximum(z, 0.0) + t, jnp.minimum(z, 0.0) - t


def _split3(x):
    a = x.astype(BF16)
    r = x - a.astype(F32)
    b = r.astype(BF16)
    c = (r - b.astype(F32)).astype(BF16)
    return a, b, c


def _inproj_kernel(x_ref, g_ref, w_ref, o_ref, hn_ref):
    @pl.when(pl.program_id(2) == 0)
    def _():
        x = x_ref[...]
        ms = jnp.mean(x * x, axis=-1, keepdims=True)
        hn_ref[...] = (x * lax.rsqrt(ms + NORM_EPS) * g_ref[...]).astype(BF16)

    o_ref[...] = jnp.dot(hn_ref[...], w_ref[...], preferred_element_type=F32).astype(o_ref.dtype)


def _inproj(h, g, w):
    bsz, p, d = h.shape
    n = w.shape[1]
    tp = _pick_tile(p, (1664, 1280, 640, 128))
    tn = _pick_tile(n, (1792, 1536, 896, 512, 128))
    return pl.pallas_call(
        _inproj_kernel,
        grid=(bsz, p // tp, n // tn),
        in_specs=[
            pl.BlockSpec((None, tp, d), lambda b, i, j: (b, i, 0)),
            pl.BlockSpec((1, d), lambda b, i, j: (0, 0)),
            pl.BlockSpec((d, tn), lambda b, i, j: (0, j)),
        ],
        out_specs=pl.BlockSpec((None, tp, tn), lambda b, i, j: (b, i, j)),
        out_shape=jax.ShapeDtypeStruct((bsz, p, n), BF16),
        scratch_shapes=[pltpu.VMEM((tp, d), BF16)],
        compiler_params=_params(("parallel", "parallel", "arbitrary")),
        name="inproj",
    )(h, g.reshape(1, d), w)


def _outproj_kernel(m1_ref, m2_ref, w1_ref, w2_ref, h_ref, o_ref, *, tp):
    y = jnp.dot(m1_ref[...], w1_ref[...], preferred_element_type=F32)
    y = y + jnp.dot(m2_ref[...], w2_ref[...], preferred_element_type=F32)
    pos = pl.program_id(1) * tp + lax.broadcasted_iota(I32, y.shape, 0)
    o_ref[...] = h_ref[...] + jnp.where(pos >= PAD_FRONT, y, 0.0)


def _outproj(m1, m2, w_out, h):
    bsz, p, d = h.shape
    k1 = m1.shape[-1]
    k2 = m2.shape[-1]
    tp = _pick_tile(p, (832, 640, 128))
    w = w_out.astype(BF16)
    return pl.pallas_call(
        functools.partial(_outproj_kernel, tp=tp),
        grid=(bsz, p // tp),
        in_specs=[
            pl.BlockSpec((None, tp, k1), lambda b, i: (b, i, 0)),
            pl.BlockSpec((None, tp, k2), lambda b, i: (b, i, 0)),
            pl.BlockSpec((k1, d), lambda b, i: (0, 0)),
            pl.BlockSpec((k2, d), lambda b, i: (0, 0)),
            pl.BlockSpec((None, tp, d), lambda b, i: (b, i, 0)),
        ],
        out_specs=pl.BlockSpec((None, tp, d), lambda b, i: (b, i, 0)),
        out_shape=jax.ShapeDtypeStruct((bsz, p, d), F32),
        compiler_params=_params(("parallel", "parallel")),
        name="outproj",
    )(m1, m2, w[:k1], w[k1:], h)


def _outproj_norm_kernel(m1_ref, m2_ref, w1_ref, w2_ref, h_ref, g_ref, o_ref):
    y = jnp.dot(m1_ref[...], w1_ref[...], preferred_element_type=F32)
    y = y + jnp.dot(m2_ref[...], w2_ref[...], preferred_element_type=F32)
    x = h_ref[...] + y
    ms = jnp.mean(x * x, axis=-1, keepdims=True)
    o_ref[...] = x * lax.rsqrt(ms + NORM_EPS) * g_ref[...]


def _outproj_norm(m1, m2, w_out, h, g, seq):
    bsz, p, d = h.shape
    k1 = m1.shape[-1]
    k2 = m2.shape[-1]
    tp = LANES
    skip = (p - seq) // tp
    assert skip * tp == p - seq
    w = w_out.astype(BF16)
    rows = lambda b, i: (b, i + skip, 0)
    return pl.pallas_call(
        _outproj_norm_kernel,
        grid=(bsz, seq // tp),
        in_specs=[
            pl.BlockSpec((None, tp, k1), rows),
            pl.BlockSpec((None, tp, k2), rows),
            pl.BlockSpec((k1, d), lambda b, i: (0, 0)),
            pl.BlockSpec((k2, d), lambda b, i: (0, 0)),
            pl.BlockSpec((None, tp, d), rows),
            pl.BlockSpec((1, d), lambda b, i: (0, 0)),
        ],
        out_specs=pl.BlockSpec((None, tp, d), lambda b, i: (b, i, 0)),
        out_shape=jax.ShapeDtypeStruct((bsz, seq, d), F32),
        compiler_params=_params(("parallel", "parallel")),
        name="outproj_norm",
    )(m1, m2, w[:k1], w[k1:], h, g.reshape(1, d))


def _dsa_kernel(aq_ref, ag_ref, iq_ref, iw_ref, k_ref, v_ref, ik_ref, bias_ref, o_ref,
                key_ref, pos_ref, sa_ref, sb_ref, iqs_ref, wts_ref, qm_ref,
                jdx_ref, kmax_ref, mx_ref, ls_ref, acc_ref, *, topk):
    T = LANES
    i = pl.program_id(1)
    last = i + DUMMY

    def num_groups(group):
        return (i + group) // group

    def first_slot(group):
        return last + 1 - group * num_groups(group)

    row = lax.broadcasted_iota(I32, (T, T), 0)
    col = lax.broadcasted_iota(I32, (T, T), 1)
    qchunk = (i * T + row) >> 6
    low = col < 64

    def admissible(kt):
        kpos = kt * T + col
        return (kpos >= PAD_FRONT) & ((kpos >> 6) <= qchunk)

    def tile_rows(slot):
        return pl.ds(pl.multiple_of(jnp.maximum(slot - DUMMY, 0) * T, T), T)

    def for_groups(group, body):
        first = first_slot(group)

        def step(g, carry):
            body(first + group * g)
            return carry

        lax.fori_loop(0, num_groups(group), step, 0)

    iw = iw_ref[...].astype(F32)
    for h in range(IDX_HEADS):
        pair = iq_ref[:, (h // 2) * T:(h // 2 + 1) * T]
        mine = low if h % 2 == 0 else ~low
        iqs_ref[0, h * T:(h + 1) * T, :] = jnp.where(mine, pair, jnp.zeros_like(pair))
        wts_ref[h] = jnp.broadcast_to(iw[:, h:h + 1], (T, T))
    for h in range(A_HEADS):
        pair = aq_ref[:, (h // 2) * T:(h // 2 + 1) * T]
        mine = low if h % 2 == 0 else ~low
        qm_ref[h // 2, (h % 2) * T:(h % 2 + 1) * T, :] = jnp.where(
            mine, pair * (A_DIM ** -0.5), jnp.zeros_like(pair))

    def sweep(lhs_ref, rhs_ref, consume, consume_last):
        group_rows = lhs_ref.shape[1]

        def issue(s0, stage):
            rows = [tile_rows(s0), tile_rows(s0 + 1)]
            for g in range(lhs_ref.shape[0]):
                ps = slice(g * T, (g + 1) * T)
                kw = jnp.concatenate([rhs_ref[rows[0], ps], rhs_ref[rows[1], ps]], axis=0)
                stage[g * group_rows:(g + 1) * group_rows, :] = lax.dot_general(
                    lhs_ref[g], kw, NT_DIMS, preferred_element_type=F32)

        first = first_slot(4)
        issue(first, sa_ref)

        def body(q, carry):
            s = first + 4 * q
            issue(s + 2, sb_ref)
            consume(s, sa_ref)
            issue(s + 4, sa_ref)
            consume(s + 2, sb_ref)
            return carry

        lax.fori_loop(0, num_groups(4) - 1, body, 0)
        issue(last - 1, sb_ref)
        consume(last - 3, sa_ref)
        consume_last(last - 1, sb_ref)

    def score_pair(s0, stage):
        for u in range(2):
            tot = jnp.zeros((T, T), F32)
            for h in range(IDX_HEADS):
                tot = tot + jnp.maximum(stage[h * T:(h + 1) * T, u * T:(u + 1) * T], 0.0) * wts_ref[h]
            score = jnp.where(admissible(s0 + u - DUMMY), tot * IDX_SCALE, NEG_INF)
            bits = lax.bitcast_convert_type(score, I32)
            key_ref[s0 + u] = bits ^ ((bits >> 31) & 0x7FFFFFFF)

    sweep(iqs_ref, ik_ref, score_pair, score_pair)
    for d in range(DUMMY):
        key_ref[d] = jnp.full((T, T), I32_MIN, I32)

    def row_count(pred):
        one = jnp.ones((T, T), I32)
        zero = jnp.zeros((T, T), I32)
        first = first_slot(8)

        def body(g, acc):
            hits = [jnp.where(pred(first + 8 * g + u), one, zero) for u in range(8)]
            while len(hits) > 1:
                hits = [a + b for a, b in zip(hits[::2], hits[1::2])]
            return acc + hits[0]

        acc = lax.fori_loop(0, num_groups(8), body, zero)
        return jnp.sum(acc.astype(F32), axis=-1, keepdims=True)

    def value_bit(it, t):
        cand = t + (jnp.int32(1) << (31 - it))
        wide = jnp.broadcast_to(cand, (T, T))
        cnt = row_count(lambda s: key_ref[s] >= wide)
        return jnp.where(cnt >= topk, cand, t)

    thr = jnp.broadcast_to(lax.fori_loop(0, 32, value_bit, jnp.full((T, 1), I32_MIN, I32)), (T, T))

    need = topk - row_count(lambda s: key_ref[s] > thr)
    ties = row_count(lambda s: key_ref[s] == thr)
    jdx_ref[...] = jnp.full((T, T), 2 ** IDX_BITS, I32)

    @pl.when(jnp.max(jnp.where(ties > need, 1.0, 0.0)) > 0.0)
    def _():
        def tie_positions(s0):
            for s in [s0 + u for u in range(4)]:
                pos_ref[s] = jnp.where(key_ref[s] == thr, (s - DUMMY) * T + col, I32_MAX)

        for_groups(4, tie_positions)
        for d in range(DUMMY):
            pos_ref[d] = jnp.full((T, T), I32_MAX, I32)

        def index_bit(bit, j):
            cand = j + (jnp.int32(1) << bit)
            wide = jnp.broadcast_to(cand, (T, T))
            cnt = row_count(lambda s: pos_ref[s] < wide)
            return jnp.where(cnt < need, cand, j)

        nbits = 7 + sum((i >= (1 << k)).astype(I32) for k in range(7))
        j = lax.fori_loop(0, nbits, lambda it, j: index_bit(nbits - 1 - it, j),
                          jnp.zeros((T, 1), I32))
        jdx_ref[...] = jnp.broadcast_to(j, (T, T))

    jdx = jdx_ref[...]

    def selection_mask(s):
        key = key_ref[s]
        sel = (key > thr) | ((key == thr) & ((s - DUMMY) * T + col <= jdx))
        return jnp.where(sel & admissible(s - DUMMY), 0.0, NEG_INF)

    head_of_lane = (lax.broadcasted_iota(I32, (A_HEADS * A_DIM, T), 0) // A_DIM
                    == lax.broadcasted_iota(I32, (A_HEADS * A_DIM, T), 1))
    head_sum = jnp.where(head_of_lane, 1.0, 0.0).astype(BF16)

    def head_norms2(t):
        t = t.astype(F32)
        return jnp.dot((t * t).astype(BF16), head_sum, preferred_element_type=F32)

    @pl.when(i == 0)
    def _():
        def widest(t, best):
            return jnp.maximum(best, head_norms2(k_ref[pl.ds(pl.multiple_of(t * T, T), T), :]))
        best = lax.fori_loop(0, k_ref.shape[0] // T, widest, jnp.zeros((T, T), F32))
        kmax_ref[...] = jnp.broadcast_to(jnp.max(best, axis=0, keepdims=True), kmax_ref.shape)

    bound2 = head_norms2(aq_ref[...]) * kmax_ref[0:1, :] * (NORM_SLACK / A_DIM)
    room = LOGIT_LIMIT - jnp.max(jnp.abs(bias_ref[...]))
    small_logits = (room > 0.0) & (jnp.max(bound2) <= room * room)

    ls_ref[...] = jnp.zeros(ls_ref.shape, F32)
    acc_ref[...] = jnp.zeros(acc_ref.shape, F32)

    def attend(s0, stage, near, final):
        madd = [selection_mask(s0), selection_mask(s0 + 1)]
        for hp in range(A_HEADS // 2):
            weights = []
            for h in (2 * hp, 2 * hp + 1):
                parts = [stage[h * T:(h + 1) * T, u * T:(u + 1) * T] + madd[u] for u in range(2)]
                if near:
                    parts = [parts[u] + bias_ref[h, 1 - u] for u in range(2)]
                if not final:
                    mx_ref[h] = jnp.maximum(mx_ref[h], jnp.maximum(parts[0], parts[1]))
                else:
                    es = [jnp.exp(part - mx_ref[h]) for part in parts]
                    ls_ref[h] = ls_ref[h] + (es[0] + es[1])
                    weights.append(jnp.concatenate(es, axis=1).astype(BF16))
            if final:
                ps = slice(hp * T, (hp + 1) * T)
                vw = jnp.concatenate([v_ref[tile_rows(s0), ps], v_ref[tile_rows(s0 + 1), ps]], axis=0)
                acc_ref[hp] = acc_ref[hp] + jnp.dot(jnp.concatenate(weights, axis=0), vw,
                                                    preferred_element_type=F32)

    def attention_sweep(final):
        sweep(qm_ref, k_ref,
              functools.partial(attend, near=False, final=final),
              functools.partial(attend, near=True, final=final))

    @pl.when(small_logits)
    def _():
        mx_ref[...] = jnp.zeros(mx_ref.shape, F32)

    @pl.when(jnp.logical_not(small_logits))
    def _():
        mx_ref[...] = jnp.full(mx_ref.shape, NEG_INF, F32)
        attention_sweep(False)
        for h in range(A_HEADS):
            mx_ref[h] = jnp.broadcast_to(jnp.max(mx_ref[h], axis=-1, keepdims=True), (T, T))

    attention_sweep(True)

    def row_total(h):
        return jnp.maximum(jnp.sum(ls_ref[h], axis=-1, keepdims=True), TINY)

    for hp in range(A_HEADS // 2):
        o0 = acc_ref[hp, :T, :] / row_total(2 * hp)
        o1 = acc_ref[hp, T:, :] / row_total(2 * hp + 1)
        g = ag_ref[:, hp * T:(hp + 1) * T].astype(F32)
        o_ref[:, hp * T:(hp + 1) * T] = (jnp.where(low, o0, o1) * _silu(g)).astype(o_ref.dtype)


def _dsa(pab, bias_tiles, topk):
    bsz, p, _ = pab.shape
    T = LANES
    nq = p // T
    hw = A_HEADS * A_DIM
    one = pl.Buffered(1)
    return pl.pallas_call(
        functools.partial(_dsa_kernel, topk=float(topk)),
        grid=(bsz, nq),
        in_specs=[
            pl.BlockSpec((None, T, hw), lambda b, i: (b, i, 0)),
            pl.BlockSpec((None, T, hw), lambda b, i: (b, i, 3)),
            pl.BlockSpec((None, T, hw), lambda b, i: (b, i, 4)),
            pl.BlockSpec((None, T, T), lambda b, i: (b, i, 33)),
            pl.BlockSpec((None, p, hw), lambda b, i: (b, 0, 1), pipeline_mode=one),
            pl.BlockSpec((None, p, hw), lambda b, i: (b, 0, 2), pipeline_mode=one),
            pl.BlockSpec((None, p, T), lambda b, i: (b, 0, 32), pipeline_mode=one),
            pl.BlockSpec((A_HEADS, 2, T, T), lambda b, i: (0, 0, 0, 0), pipeline_mode=one),
        ],
        out_specs=pl.BlockSpec((None, T, hw), lambda b, i: (b, i, 0)),
        out_shape=jax.ShapeDtypeStruct((bsz, p, hw), BF16),
        scratch_shapes=[
            pltpu.VMEM((nq + DUMMY, T, T), I32),
            pltpu.VMEM((nq + DUMMY, T, T), I32),
            pltpu.VMEM((A_HEADS * T, 2 * T), F32),
            pltpu.VMEM((A_HEADS * T, 2 * T), F32),
            pltpu.VMEM((1, IDX_HEADS * T, T), BF16),
            pltpu.VMEM((IDX_HEADS, T, T), F32),
            pltpu.VMEM((A_HEADS // 2, 2 * T, T), BF16),
            pltpu.VMEM((T, T), I32),
            pltpu.VMEM((8, T), F32),
            pltpu.VMEM((A_HEADS, T, T), F32),
            pltpu.VMEM((A_HEADS, T, T), F32),
            pltpu.VMEM((A_HEADS // 2, 2 * T, T), F32),
        ],
        compiler_params=_params(("parallel", "arbitrary")),
        name="dsa",
    )(pab, pab, pab, pab, pab, pab, pab, bias_tiles)


def _gla_kernel(q_ref, k_ref, v_ref, g_ref, a_ref, w2_ref, gb_ref, o_ref, st_ref, *, nchunk):
    C = CHUNK
    T = LANES

    @pl.when(pl.program_id(1) == 0)
    def _():
        st_ref[...] = jnp.zeros(st_ref.shape, F32)

    r_i = lax.broadcasted_iota(I32, (C, C), 0)
    c_i = lax.broadcasted_iota(I32, (C, C), 1)
    causal = c_i <= r_i
    tri = jnp.where(causal, 1.0, 0.0).astype(BF16)
    lane = lax.broadcasted_iota(I32, (C, T), 1)
    sd_r = lax.broadcasted_iota(I32, (2 * B_DV, T), 0)
    sd_c = lax.broadcasted_iota(I32, (2 * B_DV, T), 1)
    blockdiag = (sd_r >= B_DV) == (sd_c >= B_DK)

    def chunk(c, carry):
        r = pl.ds(pl.multiple_of(c * C, C), C)
        x = jnp.dot(a_ref[r, :], w2_ref[...], preferred_element_type=F32) + gb_ref[...]
        log_a = _softplus_parts(x)[1] * (1.0 / GLA_TAU)
        a1, a2, a3 = _split3(log_a)
        bcum = (jnp.dot(tri, a1, preferred_element_type=F32)
                + jnp.dot(tri, a2, preferred_element_type=F32)
                + jnp.dot(tri, a3, preferred_element_type=F32))
        b_last = bcum[C - 1:C, :]
        q = q_ref[r, :].astype(F32) * (B_DK ** -0.5)
        k = k_ref[r, :].astype(F32)
        q_t = (q * jnp.exp(bcum)).astype(BF16)
        k_t = (k * jnp.exp(-bcum)).astype(BF16)
        k_d = (k * jnp.exp(b_last - bcum)).astype(BF16)
        decay = jnp.exp(b_last)
        for hp in range(B_HEADS // 2):
            ls = slice(hp * T, (hp + 1) * T)
            st = st_ref[hp]
            o_inter = lax.dot_general(q_t[:, ls], st.astype(BF16), NT_DIMS,
                                      preferred_element_type=F32)
            for hh in range(2):
                h = 2 * hp + hh
                mine = (lane < B_DK) if hh == 0 else (lane >= B_DK)
                qm = jnp.where(mine, q_t[:, ls], jnp.zeros((C, T), BF16))
                att = lax.dot_general(qm, k_t[:, ls], NT_DIMS, preferred_element_type=F32)
                att = jnp.where(causal, att, 0.0)
                vs = slice(h * B_DV, (h + 1) * B_DV)
                o = jnp.dot(att.astype(BF16), v_ref[r, vs], preferred_element_type=F32)
                o = o + o_inter[:, hh * B_DV:(hh + 1) * B_DV]
                o = o * lax.rsqrt(jnp.mean(o * o, axis=-1, keepdims=True) + NORM_EPS)
                o = o * _silu(g_ref[r, vs].astype(F32))
                o_ref[r, vs] = o.astype(o_ref.dtype)
            vp = v_ref[r, hp * 2 * B_DV:(hp + 1) * 2 * B_DV]
            contrib = lax.dot_general(vp, k_d[:, ls], TN_DIMS, preferred_element_type=F32)
            st_ref[hp] = st * decay[:, ls] + jnp.where(blockdiag, contrib, 0.0)
        return carry

    lax.fori_loop(0, nchunk, chunk, 0, unroll=CHUNK_UNROLL)


def _gla(pab, w2, gb):
    bsz, p, _ = pab.shape
    tc = _pick_tile(p, (640, 128))
    hv = B_HEADS * B_DV
    hk = B_HEADS * B_DK
    return pl.pallas_call(
        functools.partial(_gla_kernel, nchunk=tc // CHUNK),
        grid=(bsz, p // tc),
        in_specs=[
            pl.BlockSpec((None, tc, hk), lambda b, c: (b, c, 14)),
            pl.BlockSpec((None, tc, hk), lambda b, c: (b, c, 15)),
            pl.BlockSpec((None, tc, hv), lambda b, c: (b, c, 5)),
            pl.BlockSpec((None, tc, hv), lambda b, c: (b, c, 6)),
            pl.BlockSpec((None, tc, LANES), lambda b, c: (b, c, 34)),
            pl.BlockSpec((LANES, hk), lambda b, c: (0, 0)),
            pl.BlockSpec((1, hk), lambda b, c: (0, 0)),
        ],
        out_specs=pl.BlockSpec((None, tc, hv), lambda b, c: (b, c, 0)),
        out_shape=jax.ShapeDtypeStruct((bsz, p, hv), BF16),
        scratch_shapes=[pltpu.VMEM((B_HEADS // 2, 2 * B_DV, 2 * B_DK), F32)],
        compiler_params=_params(("parallel", "arbitrary")),
        name="gla",
    )(pab, pab, pab, pab, pab, w2, gb)


def _ret_kernel(q_ref, k_ref, v_ref, g_ref, cos_ref, sin_ref, dmat_ref, zeta_ref, xi_ref,
                cdec_ref, o_ref, st_ref, *, nchunk):
    C = CHUNK
    T = LANES
    W = C_HEADS * C_DK

    @pl.when(pl.program_id(1) == 0)
    def _():
        st_ref[...] = jnp.zeros(st_ref.shape, F32)

    lane_w = lax.broadcasted_iota(I32, (C, W), 1)
    first_half = (lane_w & (C_DK - 1)) < (C_DK // 2)
    lane = lax.broadcasted_iota(I32, (C, T), 1)
    sd_r = lax.broadcasted_iota(I32, (2 * C_DV, T), 0)
    sd_c = lax.broadcasted_iota(I32, (2 * C_DV, T), 1)
    blockdiag = (sd_r >= C_DV) == (sd_c >= C_DK)

    def rotate(x, cos, sin_signed):
        swapped = jnp.where(first_half, pltpu.roll(x, W - C_DK // 2, 1), pltpu.roll(x, C_DK // 2, 1))
        return x * cos + swapped * sin_signed

    def chunk(c, carry):
        r = pl.ds(pl.multiple_of(c * C, C), C)
        cos = cos_ref[r, :]
        sin = sin_ref[r, :]
        q = rotate(q_ref[r, :].astype(F32), cos, sin)
        k = rotate(k_ref[r, :].astype(F32), cos, sin) * (C_DK ** -0.5)
        q_b = q.astype(BF16)
        k_b = k.astype(BF16)
        q_x = (q * xi_ref[...]).astype(BF16)
        k_z = (k * zeta_ref[...]).astype(BF16)
        for hp in range(C_HEADS // 2):
            ls = slice(hp * T, (hp + 1) * T)
            st = st_ref[hp]
            o_inter = lax.dot_general(q_x[:, ls], st.astype(BF16), NT_DIMS,
                                      preferred_element_type=F32)
            for hh in range(2):
                h = 2 * hp + hh
                mine = (lane < C_DK) if hh == 0 else (lane >= C_DK)
                qm = jnp.where(mine, q_b[:, ls], jnp.zeros((C, T), BF16))
                att = lax.dot_general(qm, k_b[:, ls], NT_DIMS, preferred_element_type=F32)
                att = att * dmat_ref[h]
                vs = slice(h * C_DV, (h + 1) * C_DV)
                o = jnp.dot(att.astype(BF16), v_ref[r, vs], preferred_element_type=F32)
                o = o + o_inter[:, hh * C_DV:(hh + 1) * C_DV]
                o = o - jnp.mean(o, axis=-1, keepdims=True)
                o = o * lax.rsqrt(jnp.mean(o * o, axis=-1, keepdims=True) + NORM_EPS)
                o = o * _silu(g_ref[r, vs].astype(F32))
                o_ref[r, vs] = o.astype(o_ref.dtype)
            vp = v_ref[r, hp * 2 * C_DV:(hp + 1) * 2 * C_DV]
            contrib = lax.dot_general(vp, k_z[:, ls], TN_DIMS, preferred_element_type=F32)
            st_ref[hp] = st * cdec_ref[:, ls] + jnp.where(blockdiag, contrib, 0.0)
        return carry

    lax.fori_loop(0, nchunk, chunk, 0, unroll=CHUNK_UNROLL)


def _retention_tables(p):
    log_gamma = np.log(1.0 - np.exp2(-5.0 - np.arange(C_HEADS, dtype=np.float64)))
    i = np.arange(CHUNK, dtype=np.float64)
    diff = i[:, None] - i[None, :]
    dmat = np.where(diff >= 0, np.exp(log_gamma[:, None, None] * np.maximum(diff, 0.0)), 0.0)
    zeta = np.exp(log_gamma[:, None] * (CHUNK - 1 - i))
    xi = np.exp(log_gamma[:, None] * (i + 1))
    cdec = np.exp(log_gamma * CHUNK)
    widen = lambda t: np.repeat(t.T[:, :, None], C_DK, axis=2).reshape(CHUNK, C_HEADS * C_DK)
    half = C_DK // 2
    inv = jnp.asarray(ROPE_BASE, F32) ** (-jnp.arange(half, dtype=F32) / half)
    ang = jnp.arange(p, dtype=jnp.int32).astype(F32)[:, None] * inv[None, :]
    cos = jnp.tile(jnp.cos(ang), (1, 2 * C_HEADS))
    sin = jnp.sin(ang)
    sin_signed = jnp.tile(jnp.concatenate([-sin, sin], axis=1), (1, C_HEADS))
    return (cos, sin_signed, jnp.asarray(dmat, F32), jnp.asarray(widen(zeta), F32),
            jnp.asarray(widen(xi), F32),
            jnp.asarray(np.repeat(cdec, C_DK)[None, :], F32))


def _retention(pcd, tables):
    bsz, p, _ = pcd.shape
    cos, sin, dmat, zeta, xi, cdec = tables
    tc = _pick_tile(p, (640, 128))
    hv = C_HEADS * C_DV
    hk = C_HEADS * C_DK
    full = lambda shape: pl.BlockSpec(shape, lambda b, c: (0,) * len(shape))
    return pl.pallas_call(
        functools.partial(_ret_kernel, nchunk=tc // CHUNK),
        grid=(bsz, p // tc),
        in_specs=[
            pl.BlockSpec((None, tc, hk), lambda b, c: (b, c, 0)),
            pl.BlockSpec((None, tc, hk), lambda b, c: (b, c, 1)),
            pl.BlockSpec((None, tc, hv), lambda b, c: (b, c, 1)),
            pl.BlockSpec((None, tc, hv), lambda b, c: (b, c, 2)),
            pl.BlockSpec((tc, hk), lambda b, c: (c, 0)),
            pl.BlockSpec((tc, hk), lambda b, c: (c, 0)),
            full((C_HEADS, CHUNK, CHUNK)),
            full((CHUNK, hk)),
            full((CHUNK, hk)),
            full((1, hk)),
        ],
        out_specs=pl.BlockSpec((None, tc, hv), lambda b, c: (b, c, 0)),
        out_shape=jax.ShapeDtypeStruct((bsz, p, hv), BF16),
        scratch_shapes=[pltpu.VMEM((C_HEADS // 2, 2 * C_DV, 2 * C_DK), F32)],
        compiler_params=_params(("parallel", "arbitrary")),
        name="retention",
    )(pcd, pcd, pcd, pcd, cos, sin, dmat, zeta, xi, cdec)


def _sb_kernel(q_ref, g_ref, k_ref, v_ref, o_ref, qm_ref, run_ref, acc_ref, hi_ref, lo_ref, ls_ref,
               w_ref):
    T = LANES
    i = pl.program_id(1)
    row = lax.broadcasted_iota(I32, (T, T), 0)
    col = lax.broadcasted_iota(I32, (T, T), 1)
    qpos = i * T + row
    low = col < D_DIM
    u_r = lax.broadcasted_iota(I32, (T, 2 * T), 0)
    u_c = lax.broadcasted_iota(I32, (T, 2 * T), 1)
    suffix = jnp.where((u_c >= T) | (u_r > u_c), 1.0, 0.0).astype(BF16)

    for h in range(D_HEADS):
        pair = q_ref[:, (h // 2) * T:(h // 2 + 1) * T]
        mine = low if h % 2 == 0 else ~low
        qm_ref[h] = jnp.where(mine, pair * (D_DIM ** -0.5), jnp.zeros_like(pair))
    run_ref[...] = jnp.zeros(run_ref.shape, F32)
    acc_ref[...] = jnp.zeros(acc_ref.shape, F32)

    def tile(state):
        t, _ = state
        kt = i - t
        ks = pl.ds(pl.multiple_of(kt * T, T), T)
        kpos = kt * T + col
        ok = (kpos < qpos) & (kpos >= PAD_FRONT)
        for h in range(D_HEADS):
            ps = slice((h // 2) * T, (h // 2 + 1) * T)
            z = lax.dot_general(qm_ref[h], k_ref[ks, ps], NT_DIMS, preferred_element_type=F32)
            sp, logsig = _softplus_parts(z)
            log_1m = jnp.where(ok, -sp, 0.0)
            hi = lax.bitcast_convert_type(lax.bitcast_convert_type(log_1m, I32) & BF16_BITS, F32)
            hi_ref[h] = hi.astype(BF16)
            lo_ref[h] = (log_1m - hi).astype(BF16)
            ls_ref[h] = jnp.where(ok, logsig, NEG_INF)
        slowest = None
        for h in range(D_HEADS):
            sums = (jnp.dot(hi_ref[h], suffix, preferred_element_type=F32)
                    + jnp.dot(lo_ref[h], suffix, preferred_element_type=F32))
            run = run_ref[h]
            w_ref[h] = jnp.exp(ls_ref[h] + run + sums[:, :T]).astype(BF16)
            run = run + sums[:, T:]
            run_ref[h] = run
            slowest = run if slowest is None else jnp.maximum(slowest, run)
        for h in range(D_HEADS):
            ps = slice((h // 2) * T, (h // 2 + 1) * T)
            acc_ref[h] = acc_ref[h] + jnp.dot(w_ref[h], v_ref[ks, ps],
                                              preferred_element_type=F32)
        return t + 1, (jnp.max(slowest) > LOG_F32_UNDERFLOW).astype(I32)

    lax.while_loop(lambda s: (s[0] <= i) & (s[1] > 0), tile, (jnp.int32(0), jnp.int32(1)))

    for hp in range(D_HEADS // 2):
        ps = slice(hp * T, (hp + 1) * T)
        o = jnp.where(low, acc_ref[2 * hp], acc_ref[2 * hp + 1]) * _silu(g_ref[:, ps].astype(F32))
        o_ref[:, ps] = o.astype(o_ref.dtype)


def _stick_breaking(pcd):
    bsz, p, _ = pcd.shape
    T = LANES
    hw = D_HEADS * D_DIM
    one = pl.Buffered(1)
    return pl.pallas_call(
        _sb_kernel,
        grid=(bsz, p // T),
        in_specs=[
            pl.BlockSpec((None, T, hw), lambda b, i: (b, i, 3)),
            pl.BlockSpec((None, T, hw), lambda b, i: (b, i, 6)),
            pl.BlockSpec((None, p, hw), lambda b, i: (b, 0, 4), pipeline_mode=one),
            pl.BlockSpec((None, p, hw), lambda b, i: (b, 0, 5), pipeline_mode=one),
        ],
        out_specs=pl.BlockSpec((None, T, hw), lambda b, i: (b, i, 0)),
        out_shape=jax.ShapeDtypeStruct((bsz, p, hw), BF16),
        scratch_shapes=[
            pltpu.VMEM((D_HEADS, T, T), BF16),
            pltpu.VMEM((D_HEADS, T, T), F32),
            pltpu.VMEM((D_HEADS, T, T), F32),
            pltpu.VMEM((D_HEADS, T, T), BF16),
            pltpu.VMEM((D_HEADS, T, T), BF16),
            pltpu.VMEM((D_HEADS, T, T), F32),
            pltpu.VMEM((D_HEADS, T, T), BF16),
        ],
        compiler_params=_params(("parallel", "arbitrary")),
        name="stick_breaking",
    )(pcd, pcd, pcd, pcd)


def _rel_bucket_np(rel):
    half = REL_BUCKETS // 2
    max_exact = half // 2
    n = -rel
    ret = np.where(n < 0, half, 0)
    n = np.abs(n)
    edges = [math.ceil(max_exact * (REL_MAX_DIST / max_exact) ** (j / (half - max_exact)) - 1e-9)
             for j in range(1, half - max_exact)]
    large = max_exact + sum((n >= e).astype(np.int64) for e in edges)
    return ret + np.where(n < max_exact, n, large)


def _bias_tiles(rel_bias):
    i = np.arange(LANES)[:, None]
    j = np.arange(LANES)[None, :]
    idx = np.stack([_rel_bucket_np(j - i - LANES * d) for d in range(3)])
    assert (idx[2] == idx[2, 0, 0]).all()
    table = rel_bias.astype(F32) - rel_bias.astype(F32)[idx[2, 0, 0]]
    near = jnp.asarray(idx[:2], I32)[None]
    tiles = jnp.zeros((table.shape[1],) + near.shape[1:], F32)
    for bucket in range(REL_BUCKETS):
        tiles = jnp.where(near == bucket, table[bucket][:, None, None, None], tiles)
    return tiles


def _layout_ab(w):
    aq, ak, av, ag, iq, ik, iw, bq, bk, bv, bg, ba = jnp.split(w, np.cumsum(SPLIT_AB)[:-1].tolist(), axis=1)
    pad = lambda t: jnp.pad(t, ((0, 0), (0, LANES - t.shape[1])))
    cols = [aq, ak, av, ag, iq, bv, bg, bq, bk, jnp.concatenate([ik, ik], axis=1), pad(iw), pad(ba),
            jnp.zeros((w.shape[0], W_AB_PAD - W_AB_USED), w.dtype)]
    return jnp.concatenate(cols, axis=1).astype(BF16)


def kernel(x, meta_tokens, rel_bias, norm_g, final_g, w_in_ab, gla_gate_w2, gla_gate_b, w_out_ab,
           w_in_cd, w_out_cd):
    bsz, seq, d = x.shape
    p = seq + PAD_FRONT + N_META
    depth = norm_g.shape[0]
    topk = min(TOPK_MAX, seq // 4)
    h = jnp.concatenate([jnp.zeros((bsz, PAD_FRONT, d), x.dtype),
                         jnp.broadcast_to(meta_tokens.astype(x.dtype)[None], (bsz, N_META, d)),
                         x], axis=1)
    bias_tiles = _bias_tiles(rel_bias)
    tables = _retention_tables(p)
    assert depth >= 1
    for layer in range(depth):
        j = layer // 2
        if layer % 2 == 0:
            pab = _inproj(h, norm_g[layer], _layout_ab(w_in_ab[j]))
            first = _dsa(pab, bias_tiles, topk)
            w2 = jnp.pad(gla_gate_w2[j], ((0, LANES - GLA_GATE_RANK), (0, 0))).astype(BF16)
            second = _gla(pab, w2, gla_gate_b[j].reshape(1, -1).astype(F32))
            w_out = w_out_ab[j]
        else:
            pcd = _inproj(h, norm_g[layer], w_in_cd[j].astype(BF16))
            first = _retention(pcd, tables)
            second = _stick_breaking(pcd)
            w_out = w_out_cd[j]
        if layer + 1 < depth:
            h = _outproj(first, second, w_out, h)
    return _outproj_norm(first, second, w_out, h, final_g, seq)
```

```python
import functools
import math

import numpy as np
import jax
import jax.numpy as jnp
from jax import lax
from jax.experimental import pallas as pl
from jax.experimental.pallas import tpu as pltpu

F32 = jnp.float32
BF16 = jnp.bfloat16
I32 = jnp.int32

D_MODEL = 1024
CHUNK = 64
N_META = 16
PAD_FRONT = 128 - N_META
NORM_EPS = 1e-6
NEG_INF = -1e30
A_HEADS, A_DIM = 8, 64
IDX_HEADS, IDX_DIM = 8, 64
IDX_SCALE = (IDX_DIM ** -0.5) * (IDX_HEADS ** -0.5)
TOPK_MAX = 256
B_HEADS, B_DK, B_DV = 4, 64, 128
GLA_GATE_RANK = 16
GLA_TAU = 16.0
C_HEADS, C_DK, C_DV = 4, 64, 128
ROPE_BASE = 10000.0
D_HEADS, D_DIM = 8, 64
REL_BUCKETS = 32
REL_MAX_DIST = 128
SPLIT_AB = (512, 512, 512, 512, 512, 64, 8, 256, 256, 512, 512, 16)
SPLIT_CD = (256, 256, 512, 512, 512, 512, 512, 512)

LANES = 128
W_AB_USED = 4480
W_AB_PAD = 4608
I32_MIN = -2 ** 31
DUMMY = 7
IDX_BITS = 14
LOG_F32_UNDERFLOW = -104.0
LOGIT_LIMIT = 40.0
NORM_SLACK = 1.05
TINY = 1e-30
BF16_BITS = -65536
CHUNK_UNROLL = 2
VMEM_LIMIT = 56 * 1024 * 1024

NT_DIMS = (((1,), (1,)), ((), ()))
TN_DIMS = (((0,), (0,)), ((), ()))


def _pick_tile(n, candidates):
    for c in candidates:
        if n % c == 0:
            return c
    raise ValueError(f"no tile for {n}")


def _params(sem):
    return pltpu.CompilerParams(dimension_semantics=sem, vmem_limit_bytes=VMEM_LIMIT)


def _silu(x):
    return x / (1.0 + jnp.exp(-x))


def _softplus_parts(z):
    t = jnp.log(1.0 + jnp.exp(-jnp.abs(z)))
    return jnp.maximum(z, 0.0) + t, jnp.minimum(z, 0.0) - t


def _split3(x):
    a = x.astype(BF16)
    r = x - a.astype(F32)
    b = r.astype(BF16)
    c = (r - b.astype(F32)).astype(BF16)
    return a, b, c


def _inproj_kernel(x_ref, g_ref, w_ref, o_ref, hn_ref):
    @pl.when(pl.program_id(2) == 0)
    def _():
        x = x_ref[...]
        ms = jnp.mean(x * x, axis=-1, keepdims=True)
        hn_ref[...] = (x * lax.rsqrt(ms + NORM_EPS) * g_ref[...]).astype(BF16)

    o_ref[...] = jnp.dot(hn_ref[...], w_ref[...], preferred_element_type=F32).astype(o_ref.dtype)


def _inproj(h, g, w):
    bsz, p, d = h.shape
    n = w.shape[1]
    tp = _pick_tile(p, (1664, 1280, 640, 128))
    tn = _pick_tile(n, (1792, 1536, 896, 512, 128))
    return pl.pallas_call(
        _inproj_kernel,
        grid=(bsz, p // tp, n // tn),
        in_specs=[
            pl.BlockSpec((None, tp, d), lambda b, i, j: (b, i, 0)),
            pl.BlockSpec((1, d), lambda b, i, j: (0, 0)),
            pl.BlockSpec((d, tn), lambda b, i, j: (0, j)),
        ],
        out_specs=pl.BlockSpec((None, tp, tn), lambda b, i, j: (b, i, j)),
        out_shape=jax.ShapeDtypeStruct((bsz, p, n), BF16),
        scratch_shapes=[pltpu.VMEM((tp, d), BF16)],
        compiler_params=_params(("parallel", "parallel", "arbitrary")),
        name="inproj",
    )(h, g.reshape(1, d), w)


def _outproj_kernel(m1_ref, m2_ref, w1_ref, w2_ref, h_ref, o_ref, *, tp):
    y = jnp.dot(m1_ref[...], w1_ref[...], preferred_element_type=F32)
    y = y + jnp.dot(m2_ref[...], w2_ref[...], preferred_element_type=F32)
    pos = pl.program_id(1) * tp + lax.broadcasted_iota(I32, y.shape, 0)
    o_ref[...] = h_ref[...] + jnp.where(pos >= PAD_FRONT, y, 0.0)


def _outproj(m1, m2, w_out, h):
    bsz, p, d = h.shape
    k1 = m1.shape[-1]
    k2 = m2.shape[-1]
    tp = _pick_tile(p, (832, 640, 128))
    w = w_out.astype(BF16)
    return pl.pallas_call(
        functools.partial(_outproj_kernel, tp=tp),
        grid=(bsz, p // tp),
        in_specs=[
            pl.BlockSpec((None, tp, k1), lambda b, i: (b, i, 0)),
            pl.BlockSpec((None, tp, k2), lambda b, i: (b, i, 0)),
            pl.BlockSpec((k1, d), lambda b, i: (0, 0)),
            pl.BlockSpec((k2, d), lambda b, i: (0, 0)),
            pl.BlockSpec((None, tp, d), lambda b, i: (b, i, 0)),
        ],
        out_specs=pl.BlockSpec((None, tp, d), lambda b, i: (b, i, 0)),
        out_shape=jax.ShapeDtypeStruct((bsz, p, d), F32),
        compiler_params=_params(("parallel", "parallel")),
        name="outproj",
    )(m1, m2, w[:k1], w[k1:], h)


def _outproj_norm_kernel(m1_ref, m2_ref, w1_ref, w2_ref, h_ref, g_ref, o_ref):
    y = jnp.dot(m1_ref[...], w1_ref[...], preferred_element_type=F32)
    y = y + jnp.dot(m2_ref[...], w2_ref[...], preferred_element_type=F32)
    x = h_ref[...] + y
    ms = jnp.mean(x * x, axis=-1, keepdims=True)
    o_ref[...] = x * lax.rsqrt(ms + NORM_EPS) * g_ref[...]


def _outproj_norm(m1, m2, w_out, h, g, seq):
    bsz, p, d = h.shape
    k1 = m1.shape[-1]
    k2 = m2.shape[-1]
    tp = LANES
    skip = (p - seq) // tp
    assert skip * tp == p - seq
    w = w_out.astype(BF16)
    rows = lambda b, i: (b, i + skip, 0)
    return pl.pallas_call(
        _outproj_norm_kernel,
        grid=(bsz, seq // tp),
        in_specs=[
            pl.BlockSpec((None, tp, k1), rows),
            pl.BlockSpec((None, tp, k2), rows),
            pl.BlockSpec((k1, d), lambda b, i: (0, 0)),
            pl.BlockSpec((k2, d), lambda b, i: (0, 0)),
            pl.BlockSpec((None, tp, d), rows),
            pl.BlockSpec((1, d), lambda b, i: (0, 0)),
        ],
        out_specs=pl.BlockSpec((None, tp, d), lambda b, i: (b, i, 0)),
        out_shape=jax.ShapeDtypeStruct((bsz, seq, d), F32),
        compiler_params=_params(("parallel", "parallel")),
        name="outproj_norm",
    )(m1, m2, w[:k1], w[k1:], h, g.reshape(1, d))


def _dsa_kernel(aq_ref, ag_ref, iq_ref, iw_ref, k_ref, v_ref, ik_ref, bias_ref, o_ref,
                key_ref, sa_ref, sb_ref, iqs_ref, wts_ref, qm_ref,
                jdx_ref, kmax_ref, mx_ref, ls_ref, acc_ref, *, topk):
    T = LANES
    i = pl.program_id(1)
    last = i + DUMMY

    def num_groups(group):
        return (i + group) // group

    def first_slot(group):
        return last + 1 - group * num_groups(group)

    row = lax.broadcasted_iota(I32, (T, T), 0)
    col = lax.broadcasted_iota(I32, (T, T), 1)
    qchunk = (i * T + row) >> 6
    low = col < 64

    def admissible(kt):
        kpos = kt * T + col
        return (kpos >= PAD_FRONT) & ((kpos >> 6) <= qchunk)

    def tile_rows(slot):
        return pl.ds(pl.multiple_of(jnp.maximum(slot - DUMMY, 0) * T, T), T)

    def for_groups(group, body):
        first = first_slot(group)

        def step(g, carry):
            body(first + group * g)
            return carry

        lax.fori_loop(0, num_groups(group), step, 0)

    iw = iw_ref[...].astype(F32)
    for h in range(IDX_HEADS):
        pair = iq_ref[:, (h // 2) * T:(h // 2 + 1) * T]
        mine = low if h % 2 == 0 else ~low
        iqs_ref[0, h * T:(h + 1) * T, :] = jnp.where(mine, pair, jnp.zeros_like(pair))
        wts_ref[h] = jnp.broadcast_to(iw[:, h:h + 1], (T, T))
    for h in range(A_HEADS):
        pair = aq_ref[:, (h // 2) * T:(h // 2 + 1) * T]
        mine = low if h % 2 == 0 else ~low
        qm_ref[h // 2, (h % 2) * T:(h % 2 + 1) * T, :] = jnp.where(
            mine, pair * (A_DIM ** -0.5), jnp.zeros_like(pair))

    def sweep(lhs_ref, rhs_ref, consume, consume_last):
        group_rows = lhs_ref.shape[1]

        def issue(s0, stage):
            rows = [tile_rows(s0), tile_rows(s0 + 1)]
            for g in range(lhs_ref.shape[0]):
                ps = slice(g * T, (g + 1) * T)
                kw = jnp.concatenate([rhs_ref[rows[0], ps], rhs_ref[rows[1], ps]], axis=0)
                stage[g * group_rows:(g + 1) * group_rows, :] = lax.dot_general(
                    lhs_ref[g], kw, NT_DIMS, preferred_element_type=F32)

        first = first_slot(4)
        issue(first, sa_ref)

        def body(q, carry):
            s = first + 4 * q
            issue(s + 2, sb_ref)
            consume(s, sa_ref)
            issue(s + 4, sa_ref)
            consume(s + 2, sb_ref)
            return carry

        lax.fori_loop(0, num_groups(4) - 1, body, 0)
        issue(last - 1, sb_ref)
        consume(last - 3, sa_ref)
        consume_last(last - 1, sb_ref)

    def score_pair(s0, stage):
        for u in range(2):
            tot = jnp.zeros((T, T), F32)
            for h in range(IDX_HEADS):
                tot = tot + jnp.maximum(stage[h * T:(h + 1) * T, u * T:(u + 1) * T], 0.0) * wts_ref[h]
            score = jnp.where(admissible(s0 + u - DUMMY), tot * IDX_SCALE, NEG_INF)
            bits = lax.bitcast_convert_type(score, I32)
            key_ref[s0 + u] = bits ^ ((bits >> 31) & 0x7FFFFFFF)

    sweep(iqs_ref, ik_ref, score_pair, score_pair)
    for d in range(DUMMY):
        key_ref[d] = jnp.full((T, T), I32_MIN, I32)

    def row_count(pred):
        one = jnp.ones((T, T), I32)
        zero = jnp.zeros((T, T), I32)
        first = first_slot(8)

        def body(g, acc):
            hits = [jnp.where(pred(first + 8 * g + u), one, zero) for u in range(8)]
            while len(hits) > 1:
                hits = [a + b for a, b in zip(hits[::2], hits[1::2])]
            return acc + hits[0]

        acc = lax.fori_loop(0, num_groups(8), body, zero)
        return jnp.sum(acc.astype(F32), axis=-1, keepdims=True)

    def value_bit(it, t):
        cand = t + (jnp.int32(1) << (31 - it))
        wide = jnp.broadcast_to(cand, (T, T))
        cnt = row_count(lambda s: key_ref[s] >= wide)
        return jnp.where(cnt >= topk, cand, t)

    thr = jnp.broadcast_to(lax.fori_loop(0, 32, value_bit, jnp.full((T, 1), I32_MIN, I32)), (T, T))

    need = topk - row_count(lambda s: key_ref[s] > thr)
    ties = row_count(lambda s: key_ref[s] == thr)
    jdx_ref[...] = jnp.full((T, T), 2 ** IDX_BITS, I32)

    @pl.when(jnp.max(jnp.where(ties > need, 1.0, 0.0)) > 0.0)
    def _():
        p_r = lax.broadcasted_iota(I32, (T, 2 * T), 0)
        p_c = lax.broadcasted_iota(I32, (T, 2 * T), 1)
        prefix_ones = jnp.where((p_c >= T) | (p_r <= p_c), 1.0, 0.0).astype(BF16)
        first = first_slot(8)

        def body(g, state):
            seen, best = state
            slots = [first + 8 * g + u for u in range(8)]
            tied = [key_ref[s] == thr for s in slots]
            sums = [jnp.dot(jnp.where(t, 1.0, 0.0).astype(BF16), prefix_ones,
                            preferred_element_type=F32) for t in tied]
            for s, t, sm in zip(slots, tied, sums):
                keep = t & (seen + sm[:, :T] <= need)
                best = jnp.maximum(best, jnp.where(keep, (s - DUMMY) * T + col, -1))
                seen = seen + sm[:, T:]
            return seen, best

        _, best = lax.fori_loop(0, num_groups(8), body,
                                (jnp.zeros((T, T), F32), jnp.full((T, T), -1, I32)))
        jdx_ref[...] = jnp.broadcast_to(jnp.max(best, axis=-1, keepdims=True), (T, T))

    jdx = jdx_ref[...]

    def selection_mask(s):
        key = key_ref[s]
        sel = (key > thr) | ((key == thr) & ((s - DUMMY) * T + col <= jdx))
        return jnp.where(sel & admissible(s - DUMMY), 0.0, NEG_INF)

    head_of_lane = (lax.broadcasted_iota(I32, (A_HEADS * A_DIM, T), 0) // A_DIM
                    == lax.broadcasted_iota(I32, (A_HEADS * A_DIM, T), 1))
    head_sum = jnp.where(head_of_lane, 1.0, 0.0).astype(BF16)

    def head_norms2(t):
        t = t.astype(F32)
        return jnp.dot((t * t).astype(BF16), head_sum, preferred_element_type=F32)

    @pl.when(i == 0)
    def _():
        def widest(t, best):
            return jnp.maximum(best, head_norms2(k_ref[pl.ds(pl.multiple_of(t * T, T), T), :]))
        best = lax.fori_loop(0, k_ref.shape[0] // T, widest, jnp.zeros((T, T), F32))
        kmax_ref[...] = jnp.broadcast_to(jnp.max(best, axis=0, keepdims=True), kmax_ref.shape)

    bound2 = head_norms2(aq_ref[...]) * kmax_ref[0:1, :] * (NORM_SLACK / A_DIM)
    room = LOGIT_LIMIT - jnp.max(jnp.abs(bias_ref[...]))
    small_logits = (room > 0.0) & (jnp.max(bound2) <= room * room)

    ls_ref[...] = jnp.zeros(ls_ref.shape, F32)
    acc_ref[...] = jnp.zeros(acc_ref.shape, F32)

    def attend(s0, stage, near, final):
        madd = [selection_mask(s0), selection_mask(s0 + 1)]
        for hp in range(A_HEADS // 2):
            weights = []
            for h in (2 * hp, 2 * hp + 1):
                parts = [stage[h * T:(h + 1) * T, u * T:(u + 1) * T] + madd[u] for u in range(2)]
                if near:
                    parts = [parts[u] + bias_ref[h, 1 - u] for u in range(2)]
                if not final:
                    mx_ref[h] = jnp.maximum(mx_ref[h], jnp.maximum(parts[0], parts[1]))
                else:
                    es = [jnp.exp(part - mx_ref[h]) for part in parts]
                    ls_ref[h] = ls_ref[h] + (es[0] + es[1])
                    weights.append(jnp.concatenate(es, axis=1).astype(BF16))
            if final:
                ps = slice(hp * T, (hp + 1) * T)
                vw = jnp.concatenate([v_ref[tile_rows(s0), ps], v_ref[tile_rows(s0 + 1), ps]], axis=0)
                acc_ref[hp] = acc_ref[hp] + jnp.dot(jnp.concatenate(weights, axis=0), vw,
                                                    preferred_element_type=F32)

    def attention_sweep(final):
        sweep(qm_ref, k_ref,
              functools.partial(attend, near=False, final=final),
              functools.partial(attend, near=True, final=final))

    @pl.when(small_logits)
    def _():
        mx_ref[...] = jnp.zeros(mx_ref.shape, F32)

    @pl.when(jnp.logical_not(small_logits))
    def _():
        mx_ref[...] = jnp.full(mx_ref.shape, NEG_INF, F32)
        attention_sweep(False)
        for h in range(A_HEADS):
            mx_ref[h] = jnp.broadcast_to(jnp.max(mx_ref[h], axis=-1, keepdims=True), (T, T))

    attention_sweep(True)

    def row_total(h):
        return jnp.maximum(jnp.sum(ls_ref[h], axis=-1, keepdims=True), TINY)

    for hp in range(A_HEADS // 2):
        o0 = acc_ref[hp, :T, :] / row_total(2 * hp)
        o1 = acc_ref[hp, T:, :] / row_total(2 * hp + 1)
        g = ag_ref[:, hp * T:(hp + 1) * T].astype(F32)
        o_ref[:, hp * T:(hp + 1) * T] = (jnp.where(low, o0, o1) * _silu(g)).astype(o_ref.dtype)


def _dsa(pab, bias_tiles, topk):
    bsz, p, _ = pab.shape
    T = LANES
    nq = p // T
    hw = A_HEADS * A_DIM
    one = pl.Buffered(1)
    return pl.pallas_call(
        functools.partial(_dsa_kernel, topk=float(topk)),
        grid=(bsz, nq),
        in_specs=[
            pl.BlockSpec((None, T, hw), lambda b, i: (b, i, 0)),
            pl.BlockSpec((None, T, hw), lambda b, i: (b, i, 3)),
            pl.BlockSpec((None, T, hw), lambda b, i: (b, i, 4)),
            pl.BlockSpec((None, T, T), lambda b, i: (b, i, 33)),
            pl.BlockSpec((None, p, hw), lambda b, i: (b, 0, 1), pipeline_mode=one),
            pl.BlockSpec((None, p, hw), lambda b, i: (b, 0, 2), pipeline_mode=one),
            pl.BlockSpec((None, p, T), lambda b, i: (b, 0, 32), pipeline_mode=one),
            pl.BlockSpec((A_HEADS, 2, T, T), lambda b, i: (0, 0, 0, 0), pipeline_mode=one),
        ],
        out_specs=pl.BlockSpec((None, T, hw), lambda b, i: (b, i, 0)),
        out_shape=jax.ShapeDtypeStruct((bsz, p, hw), BF16),
        scratch_shapes=[
            pltpu.VMEM((nq + DUMMY, T, T), I32),
            pltpu.VMEM((A_HEADS * T, 2 * T), F32),
            pltpu.VMEM((A_HEADS * T, 2 * T), F32),
            pltpu.VMEM((1, IDX_HEADS * T, T), BF16),
            pltpu.VMEM((IDX_HEADS, T, T), F32),
            pltpu.VMEM((A_HEADS // 2, 2 * T, T), BF16),
            pltpu.VMEM((T, T), I32),
            pltpu.VMEM((8, T), F32),
            pltpu.VMEM((A_HEADS, T, T), F32),
            pltpu.VMEM((A_HEADS, T, T), F32),
            pltpu.VMEM((A_HEADS // 2, 2 * T, T), F32),
        ],
        compiler_params=_params(("parallel", "arbitrary")),
        name="dsa",
    )(pab, pab, pab, pab, pab, pab, pab, bias_tiles)


def _gla_kernel(q_ref, k_ref, v_ref, g_ref, a_ref, w2_ref, gb_ref, o_ref, st_ref, *, nchunk):
    C = CHUNK
    T = LANES

    @pl.when(pl.program_id(1) == 0)
    def _():
        st_ref[...] = jnp.zeros(st_ref.shape, F32)

    r_i = lax.broadcasted_iota(I32, (C, C), 0)
    c_i = lax.broadcasted_iota(I32, (C, C), 1)
    causal = c_i <= r_i
    tri = jnp.where(causal, 1.0, 0.0).astype(BF16)
    lane = lax.broadcasted_iota(I32, (C, T), 1)
    sd_r = lax.broadcasted_iota(I32, (2 * B_DV, T), 0)
    sd_c = lax.broadcasted_iota(I32, (2 * B_DV, T), 1)
    blockdiag = (sd_r >= B_DV) == (sd_c >= B_DK)

    def chunk(c, carry):
        r = pl.ds(pl.multiple_of(c * C, C), C)
        x = jnp.dot(a_ref[r, :], w2_ref[...], preferred_element_type=F32) + gb_ref[...]
        log_a = _softplus_parts(x)[1] * (1.0 / GLA_TAU)
        a1, a2, a3 = _split3(log_a)
        bcum = (jnp.dot(tri, a1, preferred_element_type=F32)
                + jnp.dot(tri, a2, preferred_element_type=F32)
                + jnp.dot(tri, a3, preferred_element_type=F32))
        b_last = bcum[C - 1:C, :]
        q = q_ref[r, :].astype(F32) * (B_DK ** -0.5)
        k = k_ref[r, :].astype(F32)
        q_t = (q * jnp.exp(bcum)).astype(BF16)
        k_t = (k * jnp.exp(-bcum)).astype(BF16)
        k_d = (k * jnp.exp(b_last - bcum)).astype(BF16)
        decay = jnp.exp(b_last)
        for hp in range(B_HEADS // 2):
            ls = slice(hp * T, (hp + 1) * T)
            st = st_ref[hp]
            o_inter = lax.dot_general(q_t[:, ls], st.astype(BF16), NT_DIMS,
                                      preferred_element_type=F32)
            for hh in range(2):
                h = 2 * hp + hh
                mine = (lane < B_DK) if hh == 0 else (lane >= B_DK)
                qm = jnp.where(mine, q_t[:, ls], jnp.zeros((C, T), BF16))
                att = lax.dot_general(qm, k_t[:, ls], NT_DIMS, preferred_element_type=F32)
                att = jnp.where(causal, att, 0.0)
                vs = slice(h * B_DV, (h + 1) * B_DV)
                o = jnp.dot(att.astype(BF16), v_ref[r, vs], preferred_element_type=F32)
                o = o + o_inter[:, hh * B_DV:(hh + 1) * B_DV]
                o = o * lax.rsqrt(jnp.mean(o * o, axis=-1, keepdims=True) + NORM_EPS)
                o = o * _silu(g_ref[r, vs].astype(F32))
                o_ref[r, vs] = o.astype(o_ref.dtype)
            vp = v_ref[r, hp * 2 * B_DV:(hp + 1) * 2 * B_DV]
            contrib = lax.dot_general(vp, k_d[:, ls], TN_DIMS, preferred_element_type=F32)
            st_ref[hp] = st * decay[:, ls] + jnp.where(blockdiag, contrib, 0.0)
        return carry

    lax.fori_loop(0, nchunk, chunk, 0, unroll=CHUNK_UNROLL)


def _gla(pab, w2, gb):
    bsz, p, _ = pab.shape
    tc = _pick_tile(p, (640, 128))
    hv = B_HEADS * B_DV
    hk = B_HEADS * B_DK
    return pl.pallas_call(
        functools.partial(_gla_kernel, nchunk=tc // CHUNK),
        grid=(bsz, p // tc),
        in_specs=[
            pl.BlockSpec((None, tc, hk), lambda b, c: (b, c, 14)),
            pl.BlockSpec((None, tc, hk), lambda b, c: (b, c, 15)),
            pl.BlockSpec((None, tc, hv), lambda b, c: (b, c, 5)),
            pl.BlockSpec((None, tc, hv), lambda b, c: (b, c, 6)),
            pl.BlockSpec((None, tc, LANES), lambda b, c: (b, c, 34)),
            pl.BlockSpec((LANES, hk), lambda b, c: (0, 0)),
            pl.BlockSpec((1, hk), lambda b, c: (0, 0)),
        ],
        out_specs=pl.BlockSpec((None, tc, hv), lambda b, c: (b, c, 0)),
        out_shape=jax.ShapeDtypeStruct((bsz, p, hv), BF16),
        scratch_shapes=[pltpu.VMEM((B_HEADS // 2, 2 * B_DV, 2 * B_DK), F32)],
        compiler_params=_params(("parallel", "arbitrary")),
        name="gla",
    )(pab, pab, pab, pab, pab, w2, gb)


def _ret_kernel(q_ref, k_ref, v_ref, g_ref, cos_ref, sin_ref, dmat_ref, zeta_ref, xi_ref,
                cdec_ref, o_ref, st_ref, *, nchunk):
    C = CHUNK
    T = LANES
    W = C_HEADS * C_DK

    @pl.when(pl.program_id(1) == 0)
    def _():
        st_ref[...] = jnp.zeros(st_ref.shape, F32)

    lane_w = lax.broadcasted_iota(I32, (C, W), 1)
    first_half = (lane_w & (C_DK - 1)) < (C_DK // 2)
    lane = lax.broadcasted_iota(I32, (C, T), 1)
    sd_r = lax.broadcasted_iota(I32, (2 * C_DV, T), 0)
    sd_c = lax.broadcasted_iota(I32, (2 * C_DV, T), 1)
    blockdiag = (sd_r >= C_DV) == (sd_c >= C_DK)

    def rotate(x, cos, sin_signed):
        swapped = jnp.where(first_half, pltpu.roll(x, W - C_DK // 2, 1), pltpu.roll(x, C_DK // 2, 1))
        return x * cos + swapped * sin_signed

    def chunk(c, carry):
        r = pl.ds(pl.multiple_of(c * C, C), C)
        cos = cos_ref[r, :]
        sin = sin_ref[r, :]
        q = rotate(q_ref[r, :].astype(F32), cos, sin)
        k = rotate(k_ref[r, :].astype(F32), cos, sin) * (C_DK ** -0.5)
        q_b = q.astype(BF16)
        k_b = k.astype(BF16)
        q_x = (q * xi_ref[...]).astype(BF16)
        k_z = (k * zeta_ref[...]).astype(BF16)
        for hp in range(C_HEADS // 2):
            ls = slice(hp * T, (hp + 1) * T)
            st = st_ref[hp]
            o_inter = lax.dot_general(q_x[:, ls], st.astype(BF16), NT_DIMS,
                                      preferred_element_type=F32)
            for hh in range(2):
                h = 2 * hp + hh
                mine = (lane < C_DK) if hh == 0 else (lane >= C_DK)
                qm = jnp.where(mine, q_b[:, ls], jnp.zeros((C, T), BF16))
                att = lax.dot_general(qm, k_b[:, ls], NT_DIMS, preferred_element_type=F32)
                att = att * dmat_ref[h]
                vs = slice(h * C_DV, (h + 1) * C_DV)
                o = jnp.dot(att.astype(BF16), v_ref[r, vs], preferred_element_type=F32)
                o = o + o_inter[:, hh * C_DV:(hh + 1) * C_DV]
                o = o - jnp.mean(o, axis=-1, keepdims=True)
                o = o * lax.rsqrt(jnp.mean(o * o, axis=-1, keepdims=True) + NORM_EPS)
                o = o * _silu(g_ref[r, vs].astype(F32))
                o_ref[r, vs] = o.astype(o_ref.dtype)
            vp = v_ref[r, hp * 2 * C_DV:(hp + 1) * 2 * C_DV]
            contrib = lax.dot_general(vp, k_z[:, ls], TN_DIMS, preferred_element_type=F32)
            st_ref[hp] = st * cdec_ref[:, ls] + jnp.where(blockdiag, contrib, 0.0)
        return carry

    lax.fori_loop(0, nchunk, chunk, 0, unroll=CHUNK_UNROLL)


def _retention_tables(p):
    log_gamma = np.log(1.0 - np.exp2(-5.0 - np.arange(C_HEADS, dtype=np.float64)))
    i = np.arange(CHUNK, dtype=np.float64)
    diff = i[:, None] - i[None, :]
    dmat = np.where(diff >= 0, np.exp(log_gamma[:, None, None] * np.maximum(diff, 0.0)), 0.0)
    zeta = np.exp(log_gamma[:, None] * (CHUNK - 1 - i))
    xi = np.exp(log_gamma[:, None] * (i + 1))
    cdec = np.exp(log_gamma * CHUNK)
    widen = lambda t: np.repeat(t.T[:, :, None], C_DK, axis=2).reshape(CHUNK, C_HEADS * C_DK)
    half = C_DK // 2
    inv = jnp.asarray(ROPE_BASE, F32) ** (-jnp.arange(half, dtype=F32) / half)
    ang = jnp.arange(p, dtype=jnp.int32).astype(F32)[:, None] * inv[None, :]
    cos = jnp.tile(jnp.cos(ang), (1, 2 * C_HEADS))
    sin = jnp.sin(ang)
    sin_signed = jnp.tile(jnp.concatenate([-sin, sin], axis=1), (1, C_HEADS))
    return (cos, sin_signed, jnp.asarray(dmat, F32), jnp.asarray(widen(zeta), F32),
            jnp.asarray(widen(xi), F32),
            jnp.asarray(np.repeat(cdec, C_DK)[None, :], F32))


def _retention(pcd, tables):
    bsz, p, _ = pcd.shape
    cos, sin, dmat, zeta, xi, cdec = tables
    tc = _pick_tile(p, (640, 128))
    hv = C_HEADS * C_DV
    hk = C_HEADS * C_DK
    full = lambda shape: pl.BlockSpec(shape, lambda b, c: (0,) * len(shape))
    return pl.pallas_call(
        functools.partial(_ret_kernel, nchunk=tc // CHUNK),
        grid=(bsz, p // tc),
        in_specs=[
            pl.BlockSpec((None, tc, hk), lambda b, c: (b, c, 0)),
            pl.BlockSpec((None, tc, hk), lambda b, c: (b, c, 1)),
            pl.BlockSpec((None, tc, hv), lambda b, c: (b, c, 1)),
            pl.BlockSpec((None, tc, hv), lambda b, c: (b, c, 2)),
            pl.BlockSpec((tc, hk), lambda b, c: (c, 0)),
            pl.BlockSpec((tc, hk), lambda b, c: (c, 0)),
            full((C_HEADS, CHUNK, CHUNK)),
            full((CHUNK, hk)),
            full((CHUNK, hk)),
            full((1, hk)),
        ],
        out_specs=pl.BlockSpec((None, tc, hv), lambda b, c: (b, c, 0)),
        out_shape=jax.ShapeDtypeStruct((bsz, p, hv), BF16),
        scratch_shapes=[pltpu.VMEM((C_HEADS // 2, 2 * C_DV, 2 * C_DK), F32)],
        compiler_params=_params(("parallel", "arbitrary")),
        name="retention",
    )(pcd, pcd, pcd, pcd, cos, sin, dmat, zeta, xi, cdec)


def _sb_kernel(q_ref, g_ref, k_ref, v_ref, o_ref, qm_ref, run_ref, acc_ref, hi_ref, lo_ref, ls_ref,
               w_ref):
    T = LANES
    i = pl.program_id(1)
    row = lax.broadcasted_iota(I32, (T, T), 0)
    col = lax.broadcasted_iota(I32, (T, T), 1)
    qpos = i * T + row
    low = col < D_DIM
    u_r = lax.broadcasted_iota(I32, (T, 2 * T), 0)
    u_c = lax.broadcasted_iota(I32, (T, 2 * T), 1)
    suffix = jnp.where((u_c >= T) | (u_r > u_c), 1.0, 0.0).astype(BF16)

    for h in range(D_HEADS):
        pair = q_ref[:, (h // 2) * T:(h // 2 + 1) * T]
        mine = low if h % 2 == 0 else ~low
        qm_ref[h] = jnp.where(mine, pair * (D_DIM ** -0.5), jnp.zeros_like(pair))
    run_ref[...] = jnp.zeros(run_ref.shape, F32)
    acc_ref[...] = jnp.zeros(acc_ref.shape, F32)

    def tile(state):
        t, _ = state
        kt = i - t
        ks = pl.ds(pl.multiple_of(kt * T, T), T)
        kpos = kt * T + col
        ok = (kpos < qpos) & (kpos >= PAD_FRONT)
        for h in range(D_HEADS):
            ps = slice((h // 2) * T, (h // 2 + 1) * T)
            z = lax.dot_general(qm_ref[h], k_ref[ks, ps], NT_DIMS, preferred_element_type=F32)
            sp, logsig = _softplus_parts(z)
            log_1m = jnp.where(ok, -sp, 0.0)
            hi = lax.bitcast_convert_type(lax.bitcast_convert_type(log_1m, I32) & BF16_BITS, F32)
            hi_ref[h] = hi.astype(BF16)
            lo_ref[h] = (log_1m - hi).astype(BF16)
            ls_ref[h] = jnp.where(ok, logsig, NEG_INF)
        slowest = None
        for h in range(D_HEADS):
            sums = (jnp.dot(hi_ref[h], suffix, preferred_element_type=F32)
                    + jnp.dot(lo_ref[h], suffix, preferred_element_type=F32))
            run = run_ref[h]
            w_ref[h] = jnp.exp(ls_ref[h] + run + sums[:, :T]).astype(BF16)
            run = run + sums[:, T:]
            run_ref[h] = run
            slowest = run if slowest is None else jnp.maximum(slowest, run)
        for h in range(D_HEADS):
            ps = slice((h // 2) * T, (h // 2 + 1) * T)
            acc_ref[h] = acc_ref[h] + jnp.dot(w_ref[h], v_ref[ks, ps],
                                              preferred_element_type=F32)
        return t + 1, (jnp.max(slowest) > LOG_F32_UNDERFLOW).astype(I32)

    lax.while_loop(lambda s: (s[0] <= i) & (s[1] > 0), tile, (jnp.int32(0), jnp.int32(1)))

    for hp in range(D_HEADS // 2):
        ps = slice(hp * T, (hp + 1) * T)
        o = jnp.where(low, acc_ref[2 * hp], acc_ref[2 * hp + 1]) * _silu(g_ref[:, ps].astype(F32))
        o_ref[:, ps] = o.astype(o_ref.dtype)


def _stick_breaking(pcd):
    bsz, p, _ = pcd.shape
    T = LANES
    hw = D_HEADS * D_DIM
    one = pl.Buffered(1)
    return pl.pallas_call(
        _sb_kernel,
        grid=(bsz, p // T),
        in_specs=[
            pl.BlockSpec((None, T, hw), lambda b, i: (b, i, 3)),
            pl.BlockSpec((None, T, hw), lambda b, i: (b, i, 6)),
            pl.BlockSpec((None, p, hw), lambda b, i: (b, 0, 4), pipeline_mode=one),
            pl.BlockSpec((None, p, hw), lambda b, i: (b, 0, 5), pipeline_mode=one),
        ],
        out_specs=pl.BlockSpec((None, T, hw), lambda b, i: (b, i, 0)),
        out_shape=jax.ShapeDtypeStruct((bsz, p, hw), BF16),
        scratch_shapes=[
            pltpu.VMEM((D_HEADS, T, T), BF16),
            pltpu.VMEM((D_HEADS, T, T), F32),
            pltpu.VMEM((D_HEADS, T, T), F32),
            pltpu.VMEM((D_HEADS, T, T), BF16),
            pltpu.VMEM((D_HEADS, T, T), BF16),
            pltpu.VMEM((D_HEADS, T, T), F32),
            pltpu.VMEM((D_HEADS, T, T), BF16),
        ],
        compiler_params=_params(("parallel", "arbitrary")),
        name="stick_breaking",
    )(pcd, pcd, pcd, pcd)


def _rel_bucket_np(rel):
    half = REL_BUCKETS // 2
    max_exact = half // 2
    n = -rel
    ret = np.where(n < 0, half, 0)
    n = np.abs(n)
    edges = [math.ceil(max_exact * (REL_MAX_DIST / max_exact) ** (j / (half - max_exact)) - 1e-9)
             for j in range(1, half - max_exact)]
    large = max_exact + sum((n >= e).astype(np.int64) for e in edges)
    return ret + np.where(n < max_exact, n, large)


def _bias_tiles(rel_bias):
    i = np.arange(LANES)[:, None]
    j = np.arange(LANES)[None, :]
    idx = np.stack([_rel_bucket_np(j - i - LANES * d) for d in range(3)])
    assert (idx[2] == idx[2, 0, 0]).all()
    table = rel_bias.astype(F32) - rel_bias.astype(F32)[idx[2, 0, 0]]
    near = jnp.asarray(idx[:2], I32)[None]
    tiles = jnp.zeros((table.shape[1],) + near.shape[1:], F32)
    for bucket in range(REL_BUCKETS):
        tiles = jnp.where(near == bucket, table[bucket][:, None, None, None], tiles)
    return tiles


def _layout_ab(w):
    aq, ak, av, ag, iq, ik, iw, bq, bk, bv, bg, ba = jnp.split(w, np.cumsum(SPLIT_AB)[:-1].tolist(), axis=1)
    pad = lambda t: jnp.pad(t, ((0, 0), (0, LANES - t.shape[1])))
    cols = [aq, ak, av, ag, iq, bv, bg, bq, bk, jnp.concatenate([ik, ik], axis=1), pad(iw), pad(ba),
            jnp.zeros((w.shape[0], W_AB_PAD - W_AB_USED), w.dtype)]
    return jnp.concatenate(cols, axis=1).astype(BF16)


def kernel(x, meta_tokens, rel_bias, norm_g, final_g, w_in_ab, gla_gate_w2, gla_gate_b, w_out_ab,
           w_in_cd, w_out_cd):
    bsz, seq, d = x.shape
    p = seq + PAD_FRONT + N_META
    depth = norm_g.shape[0]
    topk = min(TOPK_MAX, seq // 4)
    h = jnp.concatenate([jnp.zeros((bsz, PAD_FRONT, d), x.dtype),
                         jnp.broadcast_to(meta_tokens.astype(x.dtype)[None], (bsz, N_META, d)),
                         x], axis=1)
    bias_tiles = _bias_tiles(rel_bias)
    tables = _retention_tables(p)
    assert depth >= 1
    for layer in range(depth):
        j = layer // 2
        if layer % 2 == 0:
            pab = _inproj(h, norm_g[layer], _layout_ab(w_in_ab[j]))
            first = _dsa(pab, bias_tiles, topk)
            w2 = jnp.pad(gla_gate_w2[j], ((0, LANES - GLA_GATE_RANK), (0, 0))).astype(BF16)
            second = _gla(pab, w2, gla_gate_b[j].reshape(1, -1).astype(F32))
            w_out = w_out_ab[j]
        else:
            pcd = _inproj(h, norm_g[layer], w_in_cd[j].astype(BF16))
            first = _retention(pcd, tables)
            second = _stick_breaking(pcd)
            w_out = w_out_cd[j]
        if layer + 1 < depth:
            h = _outproj(first, second, w_out, h)
    return _outproj_norm(first, second, w_out, h, final_g, seq)
```

```python
import functools
import math

import numpy as np
import jax
import jax.numpy as jnp
from jax import lax
from jax.experimental import pallas as pl
from jax.experimental.pallas import tpu as pltpu

F32 = jnp.float32
BF16 = jnp.bfloat16
I32 = jnp.int32

CHUNK = 64
N_META = 16
PAD_FRONT = 128 - N_META
NORM_EPS = 1e-6
NEG_INF = -1e30
A_HEADS, A_DIM = 8, 64
IDX_HEADS, IDX_DIM = 8, 64
IDX_SCALE = (IDX_DIM ** -0.5) * (IDX_HEADS ** -0.5)
TOPK_MAX = 256
B_HEADS, B_DK, B_DV = 4, 64, 128
GLA_GATE_RANK = 16
GLA_TAU = 16.0
C_HEADS, C_DK, C_DV = 4, 64, 128
ROPE_BASE = 10000.0
D_HEADS, D_DIM = 8, 64
REL_BUCKETS = 32
REL_MAX_DIST = 128
SPLIT_AB = (512, 512, 512, 512, 512, 64, 8, 256, 256, 512, 512, 16)

LANES = 128
W_AB_USED = 4480
W_AB_PAD = 4608
I32_MIN = -2 ** 31
DUMMY = 7
IDX_BITS = 14
LOG_F32_UNDERFLOW = -104.0
LOGIT_LIMIT = 40.0
NORM_SLACK = 1.05
TINY = 1e-30
BF16_BITS = -65536
CHUNK_GROUP = 5
VMEM_LIMIT = 56 * 1024 * 1024

NT_DIMS = (((1,), (1,)), ((), ()))
TN_DIMS = (((0,), (0,)), ((), ()))


def _pick_tile(n, candidates):
    for c in candidates:
        if n % c == 0:
            return c
    raise ValueError(f"no tile for {n}")


def _params(sem):
    return pltpu.CompilerParams(dimension_semantics=sem, vmem_limit_bytes=VMEM_LIMIT)


def _silu(x):
    return x / (1.0 + jnp.exp(-x))


def _softplus_parts(z):
    t = jnp.log(1.0 + jnp.exp(-jnp.abs(z)))
    return jnp.maximum(z, 0.0) + t, jnp.minimum(z, 0.0) - t


def _split3(x):
    a = x.astype(BF16)
    r = x - a.astype(F32)
    b = r.astype(BF16)
    c = (r - b.astype(F32)).astype(BF16)
    return a, b, c


def _inproj_kernel(x_ref, g_ref, w_ref, o_ref, hn_ref):
    @pl.when(pl.program_id(2) == 0)
    def _():
        x = x_ref[...]
        ms = jnp.mean(x * x, axis=-1, keepdims=True)
        hn_ref[...] = (x * lax.rsqrt(ms + NORM_EPS) * g_ref[...]).astype(BF16)

    o_ref[...] = jnp.dot(hn_ref[...], w_ref[...], preferred_element_type=F32).astype(o_ref.dtype)


def _inproj(h, g, w):
    bsz, p, d = h.shape
    n = w.shape[1]
    tp = _pick_tile(p, (1664, 1280, 640, 128))
    tn = _pick_tile(n, (1792, 1536, 896, 512, 128))
    return pl.pallas_call(
        _inproj_kernel,
        grid=(bsz, p // tp, n // tn),
        in_specs=[
            pl.BlockSpec((None, tp, d), lambda b, i, j: (b, i, 0)),
            pl.BlockSpec((1, d), lambda b, i, j: (0, 0)),
            pl.BlockSpec((d, tn), lambda b, i, j: (0, j)),
        ],
        out_specs=pl.BlockSpec((None, tp, tn), lambda b, i, j: (b, i, j)),
        out_shape=jax.ShapeDtypeStruct((bsz, p, n), BF16),
        scratch_shapes=[pltpu.VMEM((tp, d), BF16)],
        compiler_params=_params(("parallel", "parallel", "arbitrary")),
        name="inproj",
    )(h, g.reshape(1, d), w)


def _outproj_kernel(m1_ref, m2_ref, w1_ref, w2_ref, h_ref, o_ref, *, tp):
    y = jnp.dot(m1_ref[...], w1_ref[...], preferred_element_type=F32)
    y = y + jnp.dot(m2_ref[...], w2_ref[...], preferred_element_type=F32)
    pos = pl.program_id(1) * tp + lax.broadcasted_iota(I32, y.shape, 0)
    o_ref[...] = h_ref[...] + jnp.where(pos >= PAD_FRONT, y, 0.0)


def _outproj(m1, m2, w_out, h):
    bsz, p, d = h.shape
    k1 = m1.shape[-1]
    k2 = m2.shape[-1]
    tp = _pick_tile(p, (832, 640, 128))
    w = w_out.astype(BF16)
    return pl.pallas_call(
        functools.partial(_outproj_kernel, tp=tp),
        grid=(bsz, p // tp),
        in_specs=[
            pl.BlockSpec((None, tp, k1), lambda b, i: (b, i, 0)),
            pl.BlockSpec((None, tp, k2), lambda b, i: (b, i, 0)),
            pl.BlockSpec((k1, d), lambda b, i: (0, 0)),
            pl.BlockSpec((k2, d), lambda b, i: (0, 0)),
            pl.BlockSpec((None, tp, d), lambda b, i: (b, i, 0)),
        ],
        out_specs=pl.BlockSpec((None, tp, d), lambda b, i: (b, i, 0)),
        out_shape=jax.ShapeDtypeStruct((bsz, p, d), F32),
        compiler_params=_params(("parallel", "parallel")),
        name="outproj",
    )(m1, m2, w[:k1], w[k1:], h)


def _outproj_norm_kernel(m1_ref, m2_ref, w1_ref, w2_ref, h_ref, g_ref, o_ref):
    y = jnp.dot(m1_ref[...], w1_ref[...], preferred_element_type=F32)
    y = y + jnp.dot(m2_ref[...], w2_ref[...], preferred_element_type=F32)
    x = h_ref[...] + y
    ms = jnp.mean(x * x, axis=-1, keepdims=True)
    o_ref[...] = x * lax.rsqrt(ms + NORM_EPS) * g_ref[...]


def _outproj_norm(m1, m2, w_out, h, g, seq):
    bsz, p, d = h.shape
    k1 = m1.shape[-1]
    k2 = m2.shape[-1]
    tp = LANES
    skip = (p - seq) // tp
    assert skip * tp == p - seq
    w = w_out.astype(BF16)
    rows = lambda b, i: (b, i + skip, 0)
    return pl.pallas_call(
        _outproj_norm_kernel,
        grid=(bsz, seq // tp),
        in_specs=[
            pl.BlockSpec((None, tp, k1), rows),
            pl.BlockSpec((None, tp, k2), rows),
            pl.BlockSpec((k1, d), lambda b, i: (0, 0)),
            pl.BlockSpec((k2, d), lambda b, i: (0, 0)),
            pl.BlockSpec((None, tp, d), rows),
            pl.BlockSpec((1, d), lambda b, i: (0, 0)),
        ],
        out_specs=pl.BlockSpec((None, tp, d), lambda b, i: (b, i, 0)),
        out_shape=jax.ShapeDtypeStruct((bsz, seq, d), F32),
        compiler_params=_params(("parallel", "parallel")),
        name="outproj_norm",
    )(m1, m2, w[:k1], w[k1:], h, g.reshape(1, d))


def _dsa_kernel(aq_ref, ag_ref, iq_ref, iw_ref, k_ref, v_ref, ik_ref, bias_ref, o_ref,
                key_ref, sa_ref, sb_ref, iqs_ref, wts_ref, qm_ref,
                jdx_ref, kmax_ref, mx_ref, ls_ref, acc_ref, *, topk):
    T = LANES
    i = pl.program_id(1)
    last = i + DUMMY

    def num_groups(group):
        return (i + group) // group

    def first_slot(group):
        return last + 1 - group * num_groups(group)

    row = lax.broadcasted_iota(I32, (T, T), 0)
    col = lax.broadcasted_iota(I32, (T, T), 1)
    qchunk = (i * T + row) >> 6
    low = col < 64

    def admissible(kt):
        kpos = kt * T + col
        return (kpos >= PAD_FRONT) & ((kpos >> 6) <= qchunk)

    def tile_rows(slot):
        return pl.ds(pl.multiple_of(jnp.maximum(slot - DUMMY, 0) * T, T), T)

    def for_groups(group, body):
        first = first_slot(group)

        def step(g, carry):
            body(first + group * g)
            return carry

        lax.fori_loop(0, num_groups(group), step, 0)

    iw = iw_ref[...].astype(F32)
    for h in range(IDX_HEADS):
        pair = iq_ref[:, (h // 2) * T:(h // 2 + 1) * T]
        mine = low if h % 2 == 0 else ~low
        iqs_ref[0, h * T:(h + 1) * T, :] = jnp.where(mine, pair, jnp.zeros_like(pair))
        wts_ref[h] = jnp.broadcast_to(iw[:, h:h + 1], (T, T))
    for h in range(A_HEADS):
        pair = aq_ref[:, (h // 2) * T:(h // 2 + 1) * T]
        mine = low if h % 2 == 0 else ~low
        qm_ref[h // 2, (h % 2) * T:(h % 2 + 1) * T, :] = jnp.where(
            mine, pair * (A_DIM ** -0.5), jnp.zeros_like(pair))

    def sweep(lhs_ref, rhs_ref, consume, consume_last):
        group_rows = lhs_ref.shape[1]

        def issue(s0, stage):
            rows = [tile_rows(s0), tile_rows(s0 + 1)]
            for g in range(lhs_ref.shape[0]):
                ps = slice(g * T, (g + 1) * T)
                kw = jnp.concatenate([rhs_ref[rows[0], ps], rhs_ref[rows[1], ps]], axis=0)
                stage[g * group_rows:(g + 1) * group_rows, :] = lax.dot_general(
                    lhs_ref[g], kw, NT_DIMS, preferred_element_type=F32)

        first = first_slot(4)
        issue(first, sa_ref)

        def body(q, carry):
            s = first + 4 * q
            issue(s + 2, sb_ref)
            consume(s, sa_ref)
            issue(s + 4, sa_ref)
            consume(s + 2, sb_ref)
            return carry

        lax.fori_loop(0, num_groups(4) - 1, body, 0)
        issue(last - 1, sb_ref)
        consume(last - 3, sa_ref)
        consume_last(last - 1, sb_ref)

    def score_pair(s0, stage):
        for u in range(2):
            tot = jnp.zeros((T, T), F32)
            for h in range(IDX_HEADS):
                tot = tot + jnp.maximum(stage[h * T:(h + 1) * T, u * T:(u + 1) * T], 0.0) * wts_ref[h]
            score = jnp.where(admissible(s0 + u - DUMMY), tot * IDX_SCALE, NEG_INF)
            bits = lax.bitcast_convert_type(score, I32)
            key_ref[s0 + u] = bits ^ ((bits >> 31) & 0x7FFFFFFF)

    sweep(iqs_ref, ik_ref, score_pair, score_pair)
    for d in range(DUMMY):
        key_ref[d] = jnp.full((T, T), I32_MIN, I32)

    def row_count(pred):
        one = jnp.ones((T, T), I32)
        zero = jnp.zeros((T, T), I32)
        first = first_slot(8)

        def body(g, acc):
            hits = [jnp.where(pred(first + 8 * g + u), one, zero) for u in range(8)]
            while len(hits) > 1:
                hits = [a + b for a, b in zip(hits[::2], hits[1::2])]
            return acc + hits[0]

        acc = lax.fori_loop(0, num_groups(8), body, zero)
        return jnp.sum(acc.astype(F32), axis=-1, keepdims=True)

    def value_bit(it, state):
        t, reach = state
        cand = t + (jnp.int32(1) << (31 - it))
        wide = jnp.broadcast_to(cand, (T, T))
        cnt = row_count(lambda s: key_ref[s] >= wide)
        fits = cnt >= topk
        return jnp.where(fits, cand, t), jnp.where(fits, cnt, reach)

    everything = (8 * T * num_groups(8)).astype(F32)
    thr, reach = lax.fori_loop(0, 32, value_bit, (jnp.full((T, 1), I32_MIN, I32),
                                                  jnp.full((T, 1), everything, F32)))
    thr = jnp.broadcast_to(thr, (T, T))
    jdx_ref[...] = jnp.full((T, T), 2 ** IDX_BITS, I32)

    @pl.when(jnp.max(reach) > topk)
    def _():
        need = topk - (reach - row_count(lambda s: key_ref[s] == thr))
        p_r = lax.broadcasted_iota(I32, (T, 2 * T), 0)
        p_c = lax.broadcasted_iota(I32, (T, 2 * T), 1)
        prefix_ones = jnp.where((p_c >= T) | (p_r <= p_c), 1.0, 0.0).astype(BF16)
        first = first_slot(8)

        def body(g, state):
            seen, best = state
            slots = [first + 8 * g + u for u in range(8)]
            tied = [key_ref[s] == thr for s in slots]
            sums = [jnp.dot(jnp.where(t, 1.0, 0.0).astype(BF16), prefix_ones,
                            preferred_element_type=F32) for t in tied]
            for s, t, sm in zip(slots, tied, sums):
                keep = t & (seen + sm[:, :T] <= need)
                best = jnp.maximum(best, jnp.where(keep, (s - DUMMY) * T + col, -1))
                seen = seen + sm[:, T:]
            return seen, best

        _, best = lax.fori_loop(0, num_groups(8), body,
                                (jnp.zeros((T, T), F32), jnp.full((T, T), -1, I32)))
        jdx_ref[...] = jnp.broadcast_to(jnp.max(best, axis=-1, keepdims=True), (T, T))

    jdx = jdx_ref[...]

    def selection_mask(s):
        key = key_ref[s]
        sel = (key > thr) | ((key == thr) & ((s - DUMMY) * T + col <= jdx))
        return jnp.where(sel & admissible(s - DUMMY), 0.0, NEG_INF)

    head_of_lane = (lax.broadcasted_iota(I32, (A_HEADS * A_DIM, T), 0) // A_DIM
                    == lax.broadcasted_iota(I32, (A_HEADS * A_DIM, T), 1))
    head_sum = jnp.where(head_of_lane, 1.0, 0.0).astype(BF16)

    def head_norms2(t):
        t = t.astype(F32)
        return jnp.dot((t * t).astype(BF16), head_sum, preferred_element_type=F32)

    @pl.when(i == 0)
    def _():
        def widest(t, best):
            return jnp.maximum(best, head_norms2(k_ref[pl.ds(pl.multiple_of(t * T, T), T), :]))
        best = lax.fori_loop(0, k_ref.shape[0] // T, widest, jnp.zeros((T, T), F32))
        kmax_ref[...] = jnp.broadcast_to(jnp.max(best, axis=0, keepdims=True), kmax_ref.shape)

    bound2 = head_norms2(aq_ref[...]) * kmax_ref[0:1, :] * (NORM_SLACK / A_DIM)
    room = LOGIT_LIMIT - jnp.max(jnp.abs(bias_ref[...]))
    small_logits = (room > 0.0) & (jnp.max(bound2) <= room * room)

    ls_ref[...] = jnp.zeros(ls_ref.shape, F32)
    acc_ref[...] = jnp.zeros(acc_ref.shape, F32)

    def attend(s0, stage, near, final):
        madd = [selection_mask(s0), selection_mask(s0 + 1)]
        for hp in range(A_HEADS // 2):
            weights = []
            for h in (2 * hp, 2 * hp + 1):
                parts = [stage[h * T:(h + 1) * T, u * T:(u + 1) * T] + madd[u] for u in range(2)]
                if near:
                    parts = [parts[u] + bias_ref[h, 1 - u] for u in range(2)]
                if not final:
                    mx_ref[h] = jnp.maximum(mx_ref[h], jnp.maximum(parts[0], parts[1]))
                else:
                    es = [jnp.exp(part - mx_ref[h]) for part in parts]
                    ls_ref[h] = ls_ref[h] + (es[0] + es[1])
                    weights.append(jnp.concatenate(es, axis=1).astype(BF16))
            if final:
                ps = slice(hp * T, (hp + 1) * T)
                vw = jnp.concatenate([v_ref[tile_rows(s0), ps], v_ref[tile_rows(s0 + 1), ps]], axis=0)
                acc_ref[hp] = acc_ref[hp] + jnp.dot(jnp.concatenate(weights, axis=0), vw,
                                                    preferred_element_type=F32)

    def attention_sweep(final):
        sweep(qm_ref, k_ref,
              functools.partial(attend, near=False, final=final),
              functools.partial(attend, near=True, final=final))

    @pl.when(small_logits)
    def _():
        mx_ref[...] = jnp.zeros(mx_ref.shape, F32)

    @pl.when(jnp.logical_not(small_logits))
    def _():
        mx_ref[...] = jnp.full(mx_ref.shape, NEG_INF, F32)
        attention_sweep(False)
        for h in range(A_HEADS):
            mx_ref[h] = jnp.broadcast_to(jnp.max(mx_ref[h], axis=-1, keepdims=True), (T, T))

    attention_sweep(True)

    def row_total(h):
        return jnp.maximum(jnp.sum(ls_ref[h], axis=-1, keepdims=True), TINY)

    for hp in range(A_HEADS // 2):
        o0 = acc_ref[hp, :T, :] / row_total(2 * hp)
        o1 = acc_ref[hp, T:, :] / row_total(2 * hp + 1)
        g = ag_ref[:, hp * T:(hp + 1) * T].astype(F32)
        o_ref[:, hp * T:(hp + 1) * T] = (jnp.where(low, o0, o1) * _silu(g)).astype(o_ref.dtype)


def _dsa(pab, bias_tiles, topk):
    bsz, p, _ = pab.shape
    T = LANES
    nq = p // T
    hw = A_HEADS * A_DIM
    one = pl.Buffered(1)
    return pl.pallas_call(
        functools.partial(_dsa_kernel, topk=float(topk)),
        grid=(bsz, nq),
        in_specs=[
            pl.BlockSpec((None, T, hw), lambda b, i: (b, i, 0)),
            pl.BlockSpec((None, T, hw), lambda b, i: (b, i, 3)),
            pl.BlockSpec((None, T, hw), lambda b, i: (b, i, 4)),
            pl.BlockSpec((None, T, T), lambda b, i: (b, i, 33)),
            pl.BlockSpec((None, p, hw), lambda b, i: (b, 0, 1), pipeline_mode=one),
            pl.BlockSpec((None, p, hw), lambda b, i: (b, 0, 2), pipeline_mode=one),
            pl.BlockSpec((None, p, T), lambda b, i: (b, 0, 32), pipeline_mode=one),
            pl.BlockSpec((A_HEADS, 2, T, T), lambda b, i: (0, 0, 0, 0), pipeline_mode=one),
        ],
        out_specs=pl.BlockSpec((None, T, hw), lambda b, i: (b, i, 0)),
        out_shape=jax.ShapeDtypeStruct((bsz, p, hw), BF16),
        scratch_shapes=[
            pltpu.VMEM((nq + DUMMY, T, T), I32),
            pltpu.VMEM((A_HEADS * T, 2 * T), F32),
            pltpu.VMEM((A_HEADS * T, 2 * T), F32),
            pltpu.VMEM((1, IDX_HEADS * T, T), BF16),
            pltpu.VMEM((IDX_HEADS, T, T), F32),
            pltpu.VMEM((A_HEADS // 2, 2 * T, T), BF16),
            pltpu.VMEM((T, T), I32),
            pltpu.VMEM((8, T), F32),
            pltpu.VMEM((A_HEADS, T, T), F32),
            pltpu.VMEM((A_HEADS, T, T), F32),
            pltpu.VMEM((A_HEADS // 2, 2 * T, T), F32),
        ],
        compiler_params=_params(("parallel", "arbitrary")),
        name="dsa",
    )(pab, pab, pab, pab, pab, pab, pab, bias_tiles)


def _gla_kernel(q_ref, k_ref, v_ref, g_ref, a_ref, w2_ref, gb_ref, o_ref, st_ref, *, nchunk):
    C = CHUNK
    T = LANES

    @pl.when(pl.program_id(1) == 0)
    def _():
        st_ref[...] = jnp.zeros(st_ref.shape, F32)

    r_i = lax.broadcasted_iota(I32, (C, C), 0)
    c_i = lax.broadcasted_iota(I32, (C, C), 1)
    causal = c_i <= r_i
    tri = jnp.where(causal, 1.0, 0.0).astype(BF16)
    lane = lax.broadcasted_iota(I32, (C, T), 1)
    sd_r = lax.broadcasted_iota(I32, (2 * B_DV, T), 0)
    sd_c = lax.broadcasted_iota(I32, (2 * B_DV, T), 1)
    blockdiag = (sd_r >= B_DV) == (sd_c >= B_DK)

    pairs = range(B_HEADS // 2)
    lanes = [slice(hp * T, (hp + 1) * T) for hp in pairs]
    G = math.gcd(nchunk, CHUNK_GROUP)

    def group(gi, carry):
        rows = [pl.ds(pl.multiple_of((gi * G + c) * C, C), C) for c in range(G)]
        log_a = [_softplus_parts(jnp.dot(a_ref[r, :], w2_ref[...], preferred_element_type=F32)
                                 + gb_ref[...])[1] * (1.0 / GLA_TAU) for r in rows]
        terms = [_split3(x) for x in log_a]
        bcum = [sum(jnp.dot(tri, t, preferred_element_type=F32) for t in ts) for ts in terms]
        q_t, k_t, k_d, decay = [], [], [], []
        for r, b in zip(rows, bcum):
            b_last = b[C - 1:C, :]
            q = q_ref[r, :].astype(F32) * (B_DK ** -0.5)
            k = k_ref[r, :].astype(F32)
            q_t.append((q * jnp.exp(b)).astype(BF16))
            k_t.append((k * jnp.exp(-b)).astype(BF16))
            k_d.append((k * jnp.exp(b_last - b)).astype(BF16))
            decay.append(jnp.exp(b_last))
        att = {}
        for c in range(G):
            for h in range(B_HEADS):
                ls = lanes[h // 2]
                mine = (lane < B_DK) if h % 2 == 0 else (lane >= B_DK)
                qm = jnp.where(mine, q_t[c][:, ls], jnp.zeros((C, T), BF16))
                s = lax.dot_general(qm, k_t[c][:, ls], NT_DIMS, preferred_element_type=F32)
                att[c, h] = jnp.where(causal, s, 0.0).astype(BF16)
        contrib = {}
        for c in range(G):
            for hp in pairs:
                vp = v_ref[rows[c], hp * 2 * B_DV:(hp + 1) * 2 * B_DV]
                raw = lax.dot_general(vp, k_d[c][:, lanes[hp]], TN_DIMS, preferred_element_type=F32)
                contrib[c, hp] = jnp.where(blockdiag, raw, 0.0)
        before = {}
        for hp in pairs:
            st = st_ref[hp]
            for c in range(G):
                before[c, hp] = st
                st = st * decay[c][:, lanes[hp]] + contrib[c, hp]
            st_ref[hp] = st
        for c in range(G):
            for hp in pairs:
                o_inter = lax.dot_general(q_t[c][:, lanes[hp]], before[c, hp].astype(BF16), NT_DIMS,
                                          preferred_element_type=F32)
                for hh in range(2):
                    h = 2 * hp + hh
                    vs = slice(h * B_DV, (h + 1) * B_DV)
                    o = jnp.dot(att[c, h], v_ref[rows[c], vs], preferred_element_type=F32)
                    o = o + o_inter[:, hh * B_DV:(hh + 1) * B_DV]
                    o = o * lax.rsqrt(jnp.mean(o * o, axis=-1, keepdims=True) + NORM_EPS)
                    o = o * _silu(g_ref[rows[c], vs].astype(F32))
                    o_ref[rows[c], vs] = o.astype(o_ref.dtype)
        return carry

    lax.fori_loop(0, nchunk // G, group, 0)


def _gla(pab, w2, gb):
    bsz, p, _ = pab.shape
    tc = _pick_tile(p, (640, 128))
    hv = B_HEADS * B_DV
    hk = B_HEADS * B_DK
    return pl.pallas_call(
        functools.partial(_gla_kernel, nchunk=tc // CHUNK),
        grid=(bsz, p // tc),
        in_specs=[
            pl.BlockSpec((None, tc, hk), lambda b, c: (b, c, 14)),
            pl.BlockSpec((None, tc, hk), lambda b, c: (b, c, 15)),
            pl.BlockSpec((None, tc, hv), lambda b, c: (b, c, 5)),
            pl.BlockSpec((None, tc, hv), lambda b, c: (b, c, 6)),
            pl.BlockSpec((None, tc, LANES), lambda b, c: (b, c, 34)),
            pl.BlockSpec((LANES, hk), lambda b, c: (0, 0)),
            pl.BlockSpec((1, hk), lambda b, c: (0, 0)),
        ],
        out_specs=pl.BlockSpec((None, tc, hv), lambda b, c: (b, c, 0)),
        out_shape=jax.ShapeDtypeStruct((bsz, p, hv), BF16),
        scratch_shapes=[pltpu.VMEM((B_HEADS // 2, 2 * B_DV, 2 * B_DK), F32)],
        compiler_params=_params(("parallel", "arbitrary")),
        name="gla",
    )(pab, pab, pab, pab, pab, w2, gb)


def _ret_kernel(q_ref, k_ref, v_ref, g_ref, cos_ref, sin_ref, dmat_ref, zeta_ref, xi_ref,
                cdec_ref, o_ref, st_ref, *, nchunk):
    C = CHUNK
    T = LANES
    W = C_HEADS * C_DK

    @pl.when(pl.program_id(1) == 0)
    def _():
        st_ref[...] = jnp.zeros(st_ref.shape, F32)

    lane_w = lax.broadcasted_iota(I32, (C, W), 1)
    first_half = (lane_w & (C_DK - 1)) < (C_DK // 2)
    lane = lax.broadcasted_iota(I32, (C, T), 1)
    sd_r = lax.broadcasted_iota(I32, (2 * C_DV, T), 0)
    sd_c = lax.broadcasted_iota(I32, (2 * C_DV, T), 1)
    blockdiag = (sd_r >= C_DV) == (sd_c >= C_DK)

    def rotate(x, cos, sin_signed):
        swapped = jnp.where(first_half, pltpu.roll(x, W - C_DK // 2, 1), pltpu.roll(x, C_DK // 2, 1))
        return x * cos + swapped * sin_signed

    pairs = range(C_HEADS // 2)
    lanes = [slice(hp * T, (hp + 1) * T) for hp in pairs]
    G = math.gcd(nchunk, CHUNK_GROUP)

    def group(gi, carry):
        rows = [pl.ds(pl.multiple_of((gi * G + c) * C, C), C) for c in range(G)]
        q_b, k_b, q_x, k_z = [], [], [], []
        for r in rows:
            cos = cos_ref[r, :]
            sin = sin_ref[r, :]
            q = rotate(q_ref[r, :].astype(F32), cos, sin)
            k = rotate(k_ref[r, :].astype(F32), cos, sin) * (C_DK ** -0.5)
            q_b.append(q.astype(BF16))
            k_b.append(k.astype(BF16))
            q_x.append((q * xi_ref[...]).astype(BF16))
            k_z.append((k * zeta_ref[...]).astype(BF16))
        att = {}
        for c in range(G):
            for h in range(C_HEADS):
                ls = lanes[h // 2]
                mine = (lane < C_DK) if h % 2 == 0 else (lane >= C_DK)
                qm = jnp.where(mine, q_b[c][:, ls], jnp.zeros((C, T), BF16))
                s = lax.dot_general(qm, k_b[c][:, ls], NT_DIMS, preferred_element_type=F32)
                att[c, h] = (s * dmat_ref[h]).astype(BF16)
        contrib = {}
        for c in range(G):
            for hp in pairs:
                vp = v_ref[rows[c], hp * 2 * C_DV:(hp + 1) * 2 * C_DV]
                raw = lax.dot_general(vp, k_z[c][:, lanes[hp]], TN_DIMS, preferred_element_type=F32)
                contrib[c, hp] = jnp.where(blockdiag, raw, 0.0)
        before = {}
        for hp in pairs:
            st = st_ref[hp]
            for c in range(G):
                before[c, hp] = st
                st = st * cdec_ref[:, lanes[hp]] + contrib[c, hp]
            st_ref[hp] = st
        for c in range(G):
            for hp in pairs:
                o_inter = lax.dot_general(q_x[c][:, lanes[hp]], before[c, hp].astype(BF16), NT_DIMS,
                                          preferred_element_type=F32)
                for hh in range(2):
                    h = 2 * hp + hh
                    vs = slice(h * C_DV, (h + 1) * C_DV)
                    o = jnp.dot(att[c, h], v_ref[rows[c], vs], preferred_element_type=F32)
                    o = o + o_inter[:, hh * C_DV:(hh + 1) * C_DV]
                    o = o - jnp.mean(o, axis=-1, keepdims=True)
                    o = o * lax.rsqrt(jnp.mean(o * o, axis=-1, keepdims=True) + NORM_EPS)
                    o = o * _silu(g_ref[rows[c], vs].astype(F32))
                    o_ref[rows[c], vs] = o.astype(o_ref.dtype)
        return carry

    lax.fori_loop(0, nchunk // G, group, 0)


def _retention_tables(p):
    log_gamma = np.log(1.0 - np.exp2(-5.0 - np.arange(C_HEADS, dtype=np.float64)))
    i = np.arange(CHUNK, dtype=np.float64)
    diff = i[:, None] - i[None, :]
    dmat = np.where(diff >= 0, np.exp(log_gamma[:, None, None] * np.maximum(diff, 0.0)), 0.0)
    zeta = np.exp(log_gamma[:, None] * (CHUNK - 1 - i))
    xi = np.exp(log_gamma[:, None] * (i + 1))
    cdec = np.exp(log_gamma * CHUNK)
    widen = lambda t: np.repeat(t.T[:, :, None], C_DK, axis=2).reshape(CHUNK, C_HEADS * C_DK)
    half = C_DK // 2
    inv = jnp.asarray(ROPE_BASE, F32) ** (-jnp.arange(half, dtype=F32) / half)
    ang = jnp.arange(p, dtype=jnp.int32).astype(F32)[:, None] * inv[None, :]
    cos = jnp.tile(jnp.cos(ang), (1, 2 * C_HEADS))
    sin = jnp.sin(ang)
    sin_signed = jnp.tile(jnp.concatenate([-sin, sin], axis=1), (1, C_HEADS))
    return (cos, sin_signed, jnp.asarray(dmat, F32), jnp.asarray(widen(zeta), F32),
            jnp.asarray(widen(xi), F32),
            jnp.asarray(np.repeat(cdec, C_DK)[None, :], F32))


def _retention(pcd, tables):
    bsz, p, _ = pcd.shape
    cos, sin, dmat, zeta, xi, cdec = tables
    tc = _pick_tile(p, (640, 128))
    hv = C_HEADS * C_DV
    hk = C_HEADS * C_DK
    full = lambda shape: pl.BlockSpec(shape, lambda b, c: (0,) * len(shape))
    return pl.pallas_call(
        functools.partial(_ret_kernel, nchunk=tc // CHUNK),
        grid=(bsz, p // tc),
        in_specs=[
            pl.BlockSpec((None, tc, hk), lambda b, c: (b, c, 0)),
            pl.BlockSpec((None, tc, hk), lambda b, c: (b, c, 1)),
            pl.BlockSpec((None, tc, hv), lambda b, c: (b, c, 1)),
            pl.BlockSpec((None, tc, hv), lambda b, c: (b, c, 2)),
            pl.BlockSpec((tc, hk), lambda b, c: (c, 0)),
            pl.BlockSpec((tc, hk), lambda b, c: (c, 0)),
            full((C_HEADS, CHUNK, CHUNK)),
            full((CHUNK, hk)),
            full((CHUNK, hk)),
            full((1, hk)),
        ],
        out_specs=pl.BlockSpec((None, tc, hv), lambda b, c: (b, c, 0)),
        out_shape=jax.ShapeDtypeStruct((bsz, p, hv), BF16),
        scratch_shapes=[pltpu.VMEM((C_HEADS // 2, 2 * C_DV, 2 * C_DK), F32)],
        compiler_params=_params(("parallel", "arbitrary")),
        name="retention",
    )(pcd, pcd, pcd, pcd, cos, sin, dmat, zeta, xi, cdec)


def _sb_kernel(q_ref, g_ref, k_ref, v_ref, o_ref, qm_ref, run_ref, acc_ref, hi_ref, lo_ref, ls_ref,
               w_ref):
    T = LANES
    i = pl.program_id(1)
    row = lax.broadcasted_iota(I32, (T, T), 0)
    col = lax.broadcasted_iota(I32, (T, T), 1)
    qpos = i * T + row
    low = col < D_DIM
    u_r = lax.broadcasted_iota(I32, (T, 2 * T), 0)
    u_c = lax.broadcasted_iota(I32, (T, 2 * T), 1)
    suffix = jnp.where((u_c >= T) | (u_r > u_c), 1.0, 0.0).astype(BF16)

    for h in range(D_HEADS):
        pair = q_ref[:, (h // 2) * T:(h // 2 + 1) * T]
        mine = low if h % 2 == 0 else ~low
        qm_ref[h] = jnp.where(mine, pair * (D_DIM ** -0.5), jnp.zeros_like(pair))
    run_ref[...] = jnp.zeros(run_ref.shape, F32)
    acc_ref[...] = jnp.zeros(acc_ref.shape, F32)

    def tile(state):
        t, _ = state
        kt = i - t
        ks = pl.ds(pl.multiple_of(kt * T, T), T)
        kpos = kt * T + col
        ok = (kpos < qpos) & (kpos >= PAD_FRONT)
        for h in range(D_HEADS):
            ps = slice((h // 2) * T, (h // 2 + 1) * T)
            z = lax.dot_general(qm_ref[h], k_ref[ks, ps], NT_DIMS, preferred_element_type=F32)
            sp, logsig = _softplus_parts(z)
            log_1m = jnp.where(ok, -sp, 0.0)
            hi = lax.bitcast_convert_type(lax.bitcast_convert_type(log_1m, I32) & BF16_BITS, F32)
            hi_ref[h] = hi.astype(BF16)
            lo_ref[h] = (log_1m - hi).astype(BF16)
            ls_ref[h] = jnp.where(ok, logsig, NEG_INF)
        slowest = None
        for h in range(D_HEADS):
            sums = (jnp.dot(hi_ref[h], suffix, preferred_element_type=F32)
                    + jnp.dot(lo_ref[h], suffix, preferred_element_type=F32))
            run = run_ref[h]
            w_ref[h] = jnp.exp(ls_ref[h] + run + sums[:, :T]).astype(BF16)
            run = run + sums[:, T:]
            run_ref[h] = run
            slowest = run if slowest is None else jnp.maximum(slowest, run)
        for h in range(D_HEADS):
            ps = slice((h // 2) * T, (h // 2 + 1) * T)
            acc_ref[h] = acc_ref[h] + jnp.dot(w_ref[h], v_ref[ks, ps],
                                              preferred_element_type=F32)
        return t + 1, (jnp.max(slowest) > LOG_F32_UNDERFLOW).astype(I32)

    lax.while_loop(lambda s: (s[0] <= i) & (s[1] > 0), tile, (jnp.int32(0), jnp.int32(1)))

    for hp in range(D_HEADS // 2):
        ps = slice(hp * T, (hp + 1) * T)
        o = jnp.where(low, acc_ref[2 * hp], acc_ref[2 * hp + 1]) * _silu(g_ref[:, ps].astype(F32))
        o_ref[:, ps] = o.astype(o_ref.dtype)


def _stick_breaking(pcd):
    bsz, p, _ = pcd.shape
    T = LANES
    hw = D_HEADS * D_DIM
    one = pl.Buffered(1)
    return pl.pallas_call(
        _sb_kernel,
        grid=(bsz, p // T),
        in_specs=[
            pl.BlockSpec((None, T, hw), lambda b, i: (b, i, 3)),
            pl.BlockSpec((None, T, hw), lambda b, i: (b, i, 6)),
            pl.BlockSpec((None, p, hw), lambda b, i: (b, 0, 4), pipeline_mode=one),
            pl.BlockSpec((None, p, hw), lambda b, i: (b, 0, 5), pipeline_mode=one),
        ],
        out_specs=pl.BlockSpec((None, T, hw), lambda b, i: (b, i, 0)),
        out_shape=jax.ShapeDtypeStruct((bsz, p, hw), BF16),
        scratch_shapes=[
            pltpu.VMEM((D_HEADS, T, T), BF16),
            pltpu.VMEM((D_HEADS, T, T), F32),
            pltpu.VMEM((D_HEADS, T, T), F32),
            pltpu.VMEM((D_HEADS, T, T), BF16),
            pltpu.VMEM((D_HEADS, T, T), BF16),
            pltpu.VMEM((D_HEADS, T, T), F32),
            pltpu.VMEM((D_HEADS, T, T), BF16),
        ],
        compiler_params=_params(("parallel", "arbitrary")),
        name="stick_breaking",
    )(pcd, pcd, pcd, pcd)


def _rel_bucket_np(rel):
    half = REL_BUCKETS // 2
    max_exact = half // 2
    n = -rel
    ret = np.where(n < 0, half, 0)
    n = np.abs(n)
    edges = [math.ceil(max_exact * (REL_MAX_DIST / max_exact) ** (j / (half - max_exact)) - 1e-9)
             for j in range(1, half - max_exact)]
    large = max_exact + sum((n >= e).astype(np.int64) for e in edges)
    return ret + np.where(n < max_exact, n, large)


def _bias_tiles(rel_bias):
    i = np.arange(LANES)[:, None]
    j = np.arange(LANES)[None, :]
    idx = np.stack([_rel_bucket_np(j - i - LANES * d) for d in range(3)])
    assert (idx[2] == idx[2, 0, 0]).all()
    table = rel_bias.astype(F32) - rel_bias.astype(F32)[idx[2, 0, 0]]
    near = jnp.asarray(idx[:2], I32)[None]
    tiles = jnp.zeros((table.shape[1],) + near.shape[1:], F32)
    for bucket in range(REL_BUCKETS):
        tiles = jnp.where(near == bucket, table[bucket][:, None, None, None], tiles)
    return tiles


def _layout_ab(w):
    aq, ak, av, ag, iq, ik, iw, bq, bk, bv, bg, ba = jnp.split(w, np.cumsum(SPLIT_AB)[:-1].tolist(), axis=1)
    pad = lambda t: jnp.pad(t, ((0, 0), (0, LANES - t.shape[1])))
    cols = [aq, ak, av, ag, iq, bv, bg, bq, bk, jnp.concatenate([ik, ik], axis=1), pad(iw), pad(ba),
            jnp.zeros((w.shape[0], W_AB_PAD - W_AB_USED), w.dtype)]
    return jnp.concatenate(cols, axis=1).astype(BF16)


def kernel(x, meta_tokens, rel_bias, norm_g, final_g, w_in_ab, gla_gate_w2, gla_gate_b, w_out_ab,
           w_in_cd, w_out_cd):
    bsz, seq, d = x.shape
    p = seq + PAD_FRONT + N_META
    depth = norm_g.shape[0]
    topk = min(TOPK_MAX, seq // 4)
    h = jnp.concatenate([jnp.zeros((bsz, PAD_FRONT, d), x.dtype),
                         jnp.broadcast_to(meta_tokens.astype(x.dtype)[None], (bsz, N_META, d)),
                         x], axis=1)
    bias_tiles = _bias_tiles(rel_bias)
    tables = _retention_tables(p)
    assert depth >= 1
    for layer in range(depth):
        j = layer // 2
        if layer % 2 == 0:
            pab = _inproj(h, norm_g[layer], _layout_ab(w_in_ab[j]))
            first = _dsa(pab, bias_tiles, topk)
            w2 = jnp.pad(gla_gate_w2[j], ((0, LANES - GLA_GATE_RANK), (0, 0))).astype(BF16)
            second = _gla(pab, w2, gla_gate_b[j].reshape(1, -1).astype(F32))
            w_out = w_out_ab[j]
        else:
            pcd = _inproj(h, norm_g[layer], w_in_cd[j].astype(BF16))
            first = _retention(pcd, tables)
            second = _stick_breaking(pcd)
            w_out = w_out_cd[j]
        if layer + 1 < depth:
            h = _outproj(first, second, w_out, h)
    return _outproj_norm(first, second, w_out, h, final_g, seq)
```

```python
import functools
import math

import numpy as np
import jax
import jax.numpy as jnp
from jax import lax
from jax.experimental import pallas as pl
from jax.experimental.pallas import tpu as pltpu

F32 = jnp.float32
BF16 = jnp.bfloat16
I32 = jnp.int32

CHUNK = 64
N_META = 16
PAD_FRONT = 128 - N_META
NORM_EPS = 1e-6
NEG_INF = -1e30
A_HEADS, A_DIM = 8, 64
IDX_HEADS, IDX_DIM = 8, 64
IDX_SCALE = (IDX_DIM ** -0.5) * (IDX_HEADS ** -0.5)
TOPK_MAX = 256
B_HEADS, B_DK, B_DV = 4, 64, 128
GLA_GATE_RANK = 16
GLA_TAU = 16.0
C_HEADS, C_DK, C_DV = 4, 64, 128
ROPE_BASE = 10000.0
D_HEADS, D_DIM = 8, 64
REL_BUCKETS = 32
REL_MAX_DIST = 128
SPLIT_AB = (512, 512, 512, 512, 512, 64, 8, 256, 256, 512, 512, 16)

LANES = 128
W_AB_USED = 4480
W_AB_PAD = 4608
I32_MIN = -2 ** 31
DUMMY = 7
IDX_BITS = 14
SETTLE_FROM_BIT = 22
LOG_F32_UNDERFLOW = -104.0
LOGIT_LIMIT = 40.0
NORM_SLACK = 1.05
TINY = 1e-30
BF16_BITS = -65536
CHUNK_GROUP = 5
VMEM_LIMIT = 56 * 1024 * 1024

NT_DIMS = (((1,), (1,)), ((), ()))
TN_DIMS = (((0,), (0,)), ((), ()))


def _pick_tile(n, candidates):
    for c in candidates:
        if n % c == 0:
            return c
    raise ValueError(f"no tile for {n}")


def _params(sem):
    return pltpu.CompilerParams(dimension_semantics=sem, vmem_limit_bytes=VMEM_LIMIT)


def _silu(x):
    return x / (1.0 + jnp.exp(-x))


def _softplus_parts(z):
    t = jnp.log(1.0 + jnp.exp(-jnp.abs(z)))
    return jnp.maximum(z, 0.0) + t, jnp.minimum(z, 0.0) - t


def _split3(x):
    a = x.astype(BF16)
    r = x - a.astype(F32)
    b = r.astype(BF16)
    c = (r - b.astype(F32)).astype(BF16)
    return a, b, c


def _inproj_kernel(x_ref, g_ref, w_ref, o_ref, hn_ref):
    @pl.when(pl.program_id(2) == 0)
    def _():
        x = x_ref[...]
        ms = jnp.mean(x * x, axis=-1, keepdims=True)
        hn_ref[...] = (x * lax.rsqrt(ms + NORM_EPS) * g_ref[...]).astype(BF16)

    o_ref[...] = jnp.dot(hn_ref[...], w_ref[...], preferred_element_type=F32).astype(o_ref.dtype)


def _inproj(h, g, w):
    bsz, p, d = h.shape
    n = w.shape[1]
    tp = _pick_tile(p, (1664, 1280, 640, 128))
    tn = _pick_tile(n, (1792, 1536, 896, 512, 128))
    return pl.pallas_call(
        _inproj_kernel,
        grid=(bsz, p // tp, n // tn),
        in_specs=[
            pl.BlockSpec((None, tp, d), lambda b, i, j: (b, i, 0)),
            pl.BlockSpec((1, d), lambda b, i, j: (0, 0)),
            pl.BlockSpec((d, tn), lambda b, i, j: (0, j)),
        ],
        out_specs=pl.BlockSpec((None, tp, tn), lambda b, i, j: (b, i, j)),
        out_shape=jax.ShapeDtypeStruct((bsz, p, n), BF16),
        scratch_shapes=[pltpu.VMEM((tp, d), BF16)],
        compiler_params=_params(("parallel", "parallel", "arbitrary")),
        name="inproj",
    )(h, g.reshape(1, d), w)


def _outproj_kernel(m1_ref, m2_ref, w1_ref, w2_ref, h_ref, o_ref, *, tp):
    y = jnp.dot(m1_ref[...], w1_ref[...], preferred_element_type=F32)
    y = y + jnp.dot(m2_ref[...], w2_ref[...], preferred_element_type=F32)
    pos = pl.program_id(1) * tp + lax.broadcasted_iota(I32, y.shape, 0)
    o_ref[...] = h_ref[...] + jnp.where(pos >= PAD_FRONT, y, 0.0)


def _outproj(m1, m2, w_out, h):
    bsz, p, d = h.shape
    k1 = m1.shape[-1]
    k2 = m2.shape[-1]
    tp = _pick_tile(p, (832, 640, 128))
    w = w_out.astype(BF16)
    return pl.pallas_call(
        functools.partial(_outproj_kernel, tp=tp),
        grid=(bsz, p // tp),
        in_specs=[
            pl.BlockSpec((None, tp, k1), lambda b, i: (b, i, 0)),
            pl.BlockSpec((None, tp, k2), lambda b, i: (b, i, 0)),
            pl.BlockSpec((k1, d), lambda b, i: (0, 0)),
            pl.BlockSpec((k2, d), lambda b, i: (0, 0)),
            pl.BlockSpec((None, tp, d), lambda b, i: (b, i, 0)),
        ],
        out_specs=pl.BlockSpec((None, tp, d), lambda b, i: (b, i, 0)),
        out_shape=jax.ShapeDtypeStruct((bsz, p, d), F32),
        compiler_params=_params(("parallel", "parallel")),
        name="outproj",
    )(m1, m2, w[:k1], w[k1:], h)


def _outproj_norm_kernel(m1_ref, m2_ref, w1_ref, w2_ref, h_ref, g_ref, o_ref):
    y = jnp.dot(m1_ref[...], w1_ref[...], preferred_element_type=F32)
    y = y + jnp.dot(m2_ref[...], w2_ref[...], preferred_element_type=F32)
    x = h_ref[...] + y
    ms = jnp.mean(x * x, axis=-1, keepdims=True)
    o_ref[...] = x * lax.rsqrt(ms + NORM_EPS) * g_ref[...]


def _outproj_norm(m1, m2, w_out, h, g, seq):
    bsz, p, d = h.shape
    k1 = m1.shape[-1]
    k2 = m2.shape[-1]
    tp = LANES
    skip = (p - seq) // tp
    assert skip * tp == p - seq
    w = w_out.astype(BF16)
    rows = lambda b, i: (b, i + skip, 0)
    return pl.pallas_call(
        _outproj_norm_kernel,
        grid=(bsz, seq // tp),
        in_specs=[
            pl.BlockSpec((None, tp, k1), rows),
            pl.BlockSpec((None, tp, k2), rows),
            pl.BlockSpec((k1, d), lambda b, i: (0, 0)),
            pl.BlockSpec((k2, d), lambda b, i: (0, 0)),
            pl.BlockSpec((None, tp, d), rows),
            pl.BlockSpec((1, d), lambda b, i: (0, 0)),
        ],
        out_specs=pl.BlockSpec((None, tp, d), lambda b, i: (b, i, 0)),
        out_shape=jax.ShapeDtypeStruct((bsz, seq, d), F32),
        compiler_params=_params(("parallel", "parallel")),
        name="outproj_norm",
    )(m1, m2, w[:k1], w[k1:], h, g.reshape(1, d))


def _dsa_kernel(aq_ref, ag_ref, iq_ref, iw_ref, k_ref, v_ref, ik_ref, bias_ref, o_ref,
                key_ref, sa_ref, sb_ref, iqs_ref, wts_ref, qm_ref,
                jdx_ref, kmax_ref, mx_ref, ls_ref, acc_ref, *, topk):
    T = LANES
    i = pl.program_id(1)
    last = i + DUMMY

    def num_groups(group):
        return (i + group) // group

    def first_slot(group):
        return last + 1 - group * num_groups(group)

    row = lax.broadcasted_iota(I32, (T, T), 0)
    col = lax.broadcasted_iota(I32, (T, T), 1)
    qchunk = (i * T + row) >> 6
    low = col < 64

    def admissible(kt):
        kpos = kt * T + col
        return (kpos >= PAD_FRONT) & ((kpos >> 6) <= qchunk)

    def tile_rows(slot):
        return pl.ds(pl.multiple_of(jnp.maximum(slot - DUMMY, 0) * T, T), T)

    def for_groups(group, body):
        first = first_slot(group)

        def step(g, carry):
            body(first + group * g)
            return carry

        lax.fori_loop(0, num_groups(group), step, 0)

    iw = iw_ref[...].astype(F32)
    for h in range(IDX_HEADS):
        pair = iq_ref[:, (h // 2) * T:(h // 2 + 1) * T]
        mine = low if h % 2 == 0 else ~low
        iqs_ref[0, h * T:(h + 1) * T, :] = jnp.where(mine, pair, jnp.zeros_like(pair))
        wts_ref[h] = jnp.broadcast_to(iw[:, h:h + 1], (T, T))
    for h in range(A_HEADS):
        pair = aq_ref[:, (h // 2) * T:(h // 2 + 1) * T]
        mine = low if h % 2 == 0 else ~low
        qm_ref[h // 2, (h % 2) * T:(h % 2 + 1) * T, :] = jnp.where(
            mine, pair * (A_DIM ** -0.5), jnp.zeros_like(pair))

    def sweep(lhs_ref, rhs_ref, consume, consume_last):
        group_rows = lhs_ref.shape[1]

        def issue(s0, stage):
            rows = [tile_rows(s0), tile_rows(s0 + 1)]
            for g in range(lhs_ref.shape[0]):
                ps = slice(g * T, (g + 1) * T)
                kw = jnp.concatenate([rhs_ref[rows[0], ps], rhs_ref[rows[1], ps]], axis=0)
                stage[g * group_rows:(g + 1) * group_rows, :] = lax.dot_general(
                    lhs_ref[g], kw, NT_DIMS, preferred_element_type=F32)

        first = first_slot(4)
        issue(first, sa_ref)

        def body(q, carry):
            s = first + 4 * q
            issue(s + 2, sb_ref)
            consume(s, sa_ref)
            issue(s + 4, sa_ref)
            consume(s + 2, sb_ref)
            return carry

        lax.fori_loop(0, num_groups(4) - 1, body, 0)
        issue(last - 1, sb_ref)
        consume(last - 3, sa_ref)
        consume_last(last - 1, sb_ref)

    def score_pair(s0, stage):
        for u in range(2):
            tot = jnp.zeros((T, T), F32)
            for h in range(IDX_HEADS):
                tot = tot + jnp.maximum(stage[h * T:(h + 1) * T, u * T:(u + 1) * T], 0.0) * wts_ref[h]
            score = jnp.where(admissible(s0 + u - DUMMY), tot * IDX_SCALE, NEG_INF)
            bits = lax.bitcast_convert_type(score, I32)
            key_ref[s0 + u] = bits ^ ((bits >> 31) & 0x7FFFFFFF)

    sweep(iqs_ref, ik_ref, score_pair, score_pair)
    for d in range(DUMMY):
        key_ref[d] = jnp.full((T, T), I32_MIN, I32)

    def row_count(pred):
        one = jnp.ones((T, T), I32)
        zero = jnp.zeros((T, T), I32)
        first = first_slot(8)

        def body(g, acc):
            hits = [jnp.where(pred(first + 8 * g + u), one, zero) for u in range(8)]
            while len(hits) > 1:
                hits = [a + b for a, b in zip(hits[::2], hits[1::2])]
            return acc + hits[0]

        acc = lax.fori_loop(0, num_groups(8), body, zero)
        return jnp.sum(acc.astype(F32), axis=-1, keepdims=True)

    def value_bit(it, state):
        t, reach = state
        cand = t + (jnp.int32(1) << (31 - it))
        wide = jnp.broadcast_to(cand, (T, T))
        cnt = row_count(lambda s: key_ref[s] >= wide)
        fits = cnt >= topk
        return jnp.where(fits, cand, t), jnp.where(fits, cnt, reach)

    everything = (8 * T * num_groups(8)).astype(F32)
    state = lax.fori_loop(0, SETTLE_FROM_BIT, value_bit, (jnp.full((T, 1), I32_MIN, I32),
                                                          jnp.full((T, 1), everything, F32)))

    def unsettled(carry):
        it, (_, reach) = carry
        return (it < 32) & (jnp.max(reach) > topk)

    def two_bits(carry):
        it, state = carry
        return it + 2, value_bit(it + 1, value_bit(it, state))

    _, (thr, reach) = lax.while_loop(unsettled, two_bits, (jnp.int32(SETTLE_FROM_BIT), state))
    thr = jnp.broadcast_to(thr, (T, T))
    jdx_ref[...] = jnp.full((T, T), 2 ** IDX_BITS, I32)

    @pl.when(jnp.max(reach) > topk)
    def _():
        need = topk - (reach - row_count(lambda s: key_ref[s] == thr))
        p_r = lax.broadcasted_iota(I32, (T, 2 * T), 0)
        p_c = lax.broadcasted_iota(I32, (T, 2 * T), 1)
        prefix_ones = jnp.where((p_c >= T) | (p_r <= p_c), 1.0, 0.0).astype(BF16)
        first = first_slot(8)

        def body(g, state):
            seen, best = state
            slots = [first + 8 * g + u for u in range(8)]
            tied = [key_ref[s] == thr for s in slots]
            sums = [jnp.dot(jnp.where(t, 1.0, 0.0).astype(BF16), prefix_ones,
                            preferred_element_type=F32) for t in tied]
            for s, t, sm in zip(slots, tied, sums):
                keep = t & (seen + sm[:, :T] <= need)
                best = jnp.maximum(best, jnp.where(keep, (s - DUMMY) * T + col, -1))
                seen = seen + sm[:, T:]
            return seen, best

        _, best = lax.fori_loop(0, num_groups(8), body,
                                (jnp.zeros((T, T), F32), jnp.full((T, T), -1, I32)))
        jdx_ref[...] = jnp.broadcast_to(jnp.max(best, axis=-1, keepdims=True), (T, T))

    jdx = jdx_ref[...]

    def selection_mask(s):
        key = key_ref[s]
        sel = (key > thr) | ((key == thr) & ((s - DUMMY) * T + col <= jdx))
        return jnp.where(sel & admissible(s - DUMMY), 0.0, NEG_INF)

    head_of_lane = (lax.broadcasted_iota(I32, (A_HEADS * A_DIM, T), 0) // A_DIM
                    == lax.broadcasted_iota(I32, (A_HEADS * A_DIM, T), 1))
    head_sum = jnp.where(head_of_lane, 1.0, 0.0).astype(BF16)

    def head_norms2(t):
        t = t.astype(F32)
        return jnp.dot((t * t).astype(BF16), head_sum, preferred_element_type=F32)

    @pl.when(i == 0)
    def _():
        def widest(t, best):
            return jnp.maximum(best, head_norms2(k_ref[pl.ds(pl.multiple_of(t * T, T), T), :]))
        best = lax.fori_loop(0, k_ref.shape[0] // T, widest, jnp.zeros((T, T), F32))
        kmax_ref[...] = jnp.broadcast_to(jnp.max(best, axis=0, keepdims=True), kmax_ref.shape)

    bound2 = head_norms2(aq_ref[...]) * kmax_ref[0:1, :] * (NORM_SLACK / A_DIM)
    room = LOGIT_LIMIT - jnp.max(jnp.abs(bias_ref[...]))
    small_logits = (room > 0.0) & (jnp.max(bound2) <= room * room)

    ls_ref[...] = jnp.zeros(ls_ref.shape, F32)
    acc_ref[...] = jnp.zeros(acc_ref.shape, F32)

    def attend(s0, stage, near, final):
        madd = [selection_mask(s0), selection_mask(s0 + 1)]
        for hp in range(A_HEADS // 2):
            weights = []
            for h in (2 * hp, 2 * hp + 1):
                parts = [stage[h * T:(h + 1) * T, u * T:(u + 1) * T] + madd[u] for u in range(2)]
                if near:
                    parts = [parts[u] + bias_ref[h, 1 - u] for u in range(2)]
                if not final:
                    mx_ref[h] = jnp.maximum(mx_ref[h], jnp.maximum(parts[0], parts[1]))
                else:
                    es = [jnp.exp(part - mx_ref[h]) for part in parts]
                    ls_ref[h] = ls_ref[h] + (es[0] + es[1])
                    weights.append(jnp.concatenate(es, axis=1).astype(BF16))
            if final:
                ps = slice(hp * T, (hp + 1) * T)
                vw = jnp.concatenate([v_ref[tile_rows(s0), ps], v_ref[tile_rows(s0 + 1), ps]], axis=0)
                acc_ref[hp] = acc_ref[hp] + jnp.dot(jnp.concatenate(weights, axis=0), vw,
                                                    preferred_element_type=F32)

    def attention_sweep(final):
        sweep(qm_ref, k_ref,
              functools.partial(attend, near=False, final=final),
              functools.partial(attend, near=True, final=final))

    @pl.when(small_logits)
    def _():
        mx_ref[...] = jnp.zeros(mx_ref.shape, F32)

    @pl.when(jnp.logical_not(small_logits))
    def _():
        mx_ref[...] = jnp.full(mx_ref.shape, NEG_INF, F32)
        attention_sweep(False)
        for h in range(A_HEADS):
            mx_ref[h] = jnp.broadcast_to(jnp.max(mx_ref[h], axis=-1, keepdims=True), (T, T))

    attention_sweep(True)

    def row_total(h):
        return jnp.maximum(jnp.sum(ls_ref[h], axis=-1, keepdims=True), TINY)

    for hp in range(A_HEADS // 2):
        o0 = acc_ref[hp, :T, :] / row_total(2 * hp)
        o1 = acc_ref[hp, T:, :] / row_total(2 * hp + 1)
        g = ag_ref[:, hp * T:(hp + 1) * T].astype(F32)
        o_ref[:, hp * T:(hp + 1) * T] = (jnp.where(low, o0, o1) * _silu(g)).astype(o_ref.dtype)


def _dsa(pab, bias_tiles, topk):
    bsz, p, _ = pab.shape
    T = LANES
    nq = p // T
    hw = A_HEADS * A_DIM
    one = pl.Buffered(1)
    return pl.pallas_call(
        functools.partial(_dsa_kernel, topk=float(topk)),
        grid=(bsz, nq),
        in_specs=[
            pl.BlockSpec((None, T, hw), lambda b, i: (b, i, 0)),
            pl.BlockSpec((None, T, hw), lambda b, i: (b, i, 3)),
            pl.BlockSpec((None, T, hw), lambda b, i: (b, i, 4)),
            pl.BlockSpec((None, T, T), lambda b, i: (b, i, 33)),
            pl.BlockSpec((None, p, hw), lambda b, i: (b, 0, 1), pipeline_mode=one),
            pl.BlockSpec((None, p, hw), lambda b, i: (b, 0, 2), pipeline_mode=one),
            pl.BlockSpec((None, p, T), lambda b, i: (b, 0, 32), pipeline_mode=one),
            pl.BlockSpec((A_HEADS, 2, T, T), lambda b, i: (0, 0, 0, 0), pipeline_mode=one),
        ],
        out_specs=pl.BlockSpec((None, T, hw), lambda b, i: (b, i, 0)),
        out_shape=jax.ShapeDtypeStruct((bsz, p, hw), BF16),
        scratch_shapes=[
            pltpu.VMEM((nq + DUMMY, T, T), I32),
            pltpu.VMEM((A_HEADS * T, 2 * T), F32),
            pltpu.VMEM((A_HEADS * T, 2 * T), F32),
            pltpu.VMEM((1, IDX_HEADS * T, T), BF16),
            pltpu.VMEM((IDX_HEADS, T, T), F32),
            pltpu.VMEM((A_HEADS // 2, 2 * T, T), BF16),
            pltpu.VMEM((T, T), I32),
            pltpu.VMEM((8, T), F32),
            pltpu.VMEM((A_HEADS, T, T), F32),
            pltpu.VMEM((A_HEADS, T, T), F32),
            pltpu.VMEM((A_HEADS // 2, 2 * T, T), F32),
        ],
        compiler_params=_params(("parallel", "arbitrary")),
        name="dsa",
    )(pab, pab, pab, pab, pab, pab, pab, bias_tiles)


def _gla_kernel(q_ref, k_ref, v_ref, g_ref, a_ref, w2_ref, gb_ref, o_ref, st_ref, *, nchunk):
    C = CHUNK
    T = LANES

    @pl.when(pl.program_id(1) == 0)
    def _():
        st_ref[...] = jnp.zeros(st_ref.shape, F32)

    r_i = lax.broadcasted_iota(I32, (C, C), 0)
    c_i = lax.broadcasted_iota(I32, (C, C), 1)
    causal = c_i <= r_i
    tri = jnp.where(causal, 1.0, 0.0).astype(BF16)
    lane = lax.broadcasted_iota(I32, (C, T), 1)
    sd_r = lax.broadcasted_iota(I32, (2 * B_DV, T), 0)
    sd_c = lax.broadcasted_iota(I32, (2 * B_DV, T), 1)
    blockdiag = (sd_r >= B_DV) == (sd_c >= B_DK)

    pairs = range(B_HEADS // 2)
    lanes = [slice(hp * T, (hp + 1) * T) for hp in pairs]
    G = math.gcd(nchunk, CHUNK_GROUP)

    def group(gi, carry):
        rows = [pl.ds(pl.multiple_of((gi * G + c) * C, C), C) for c in range(G)]
        log_a = [_softplus_parts(jnp.dot(a_ref[r, :], w2_ref[...], preferred_element_type=F32)
                                 + gb_ref[...])[1] * (1.0 / GLA_TAU) for r in rows]
        terms = [_split3(x) for x in log_a]
        bcum = [sum(jnp.dot(tri, t, preferred_element_type=F32) for t in ts) for ts in terms]
        q_t, k_t, k_d, decay = [], [], [], []
        for r, b in zip(rows, bcum):
            b_last = b[C - 1:C, :]
            q = q_ref[r, :].astype(F32) * (B_DK ** -0.5)
            k = k_ref[r, :].astype(F32)
            q_t.append((q * jnp.exp(b)).astype(BF16))
            k_t.append((k * jnp.exp(-b)).astype(BF16))
            k_d.append((k * jnp.exp(b_last - b)).astype(BF16))
            decay.append(jnp.exp(b_last))
        att = {}
        for c in range(G):
            for h in range(B_HEADS):
                ls = lanes[h // 2]
                mine = (lane < B_DK) if h % 2 == 0 else (lane >= B_DK)
                qm = jnp.where(mine, q_t[c][:, ls], jnp.zeros((C, T), BF16))
                s = lax.dot_general(qm, k_t[c][:, ls], NT_DIMS, preferred_element_type=F32)
                att[c, h] = jnp.where(causal, s, 0.0).astype(BF16)
        contrib = {}
        for c in range(G):
            for hp in pairs:
                vp = v_ref[rows[c], hp * 2 * B_DV:(hp + 1) * 2 * B_DV]
                raw = lax.dot_general(vp, k_d[c][:, lanes[hp]], TN_DIMS, preferred_element_type=F32)
                contrib[c, hp] = jnp.where(blockdiag, raw, 0.0)
        before = {}
        for hp in pairs:
            st = st_ref[hp]
            for c in range(G):
                before[c, hp] = st
                st = st * decay[c][:, lanes[hp]] + contrib[c, hp]
            st_ref[hp] = st
        for c in range(G):
            for hp in pairs:
                o_inter = lax.dot_general(q_t[c][:, lanes[hp]], before[c, hp].astype(BF16), NT_DIMS,
                                          preferred_element_type=F32)
                for hh in range(2):
                    h = 2 * hp + hh
                    vs = slice(h * B_DV, (h + 1) * B_DV)
                    o = jnp.dot(att[c, h], v_ref[rows[c], vs], preferred_element_type=F32)
                    o = o + o_inter[:, hh * B_DV:(hh + 1) * B_DV]
                    o = o * lax.rsqrt(jnp.mean(o * o, axis=-1, keepdims=True) + NORM_EPS)
                    o = o * _silu(g_ref[rows[c], vs].astype(F32))
                    o_ref[rows[c], vs] = o.astype(o_ref.dtype)
        return carry

    lax.fori_loop(0, nchunk // G, group, 0)


def _gla(pab, w2, gb):
    bsz, p, _ = pab.shape
    tc = _pick_tile(p, (640, 128))
    hv = B_HEADS * B_DV
    hk = B_HEADS * B_DK
    return pl.pallas_call(
        functools.partial(_gla_kernel, nchunk=tc // CHUNK),
        grid=(bsz, p // tc),
        in_specs=[
            pl.BlockSpec((None, tc, hk), lambda b, c: (b, c, 14)),
            pl.BlockSpec((None, tc, hk), lambda b, c: (b, c, 15)),
            pl.BlockSpec((None, tc, hv), lambda b, c: (b, c, 5)),
            pl.BlockSpec((None, tc, hv), lambda b, c: (b, c, 6)),
            pl.BlockSpec((None, tc, LANES), lambda b, c: (b, c, 34)),
            pl.BlockSpec((LANES, hk), lambda b, c: (0, 0)),
            pl.BlockSpec((1, hk), lambda b, c: (0, 0)),
        ],
        out_specs=pl.BlockSpec((None, tc, hv), lambda b, c: (b, c, 0)),
        out_shape=jax.ShapeDtypeStruct((bsz, p, hv), BF16),
        scratch_shapes=[pltpu.VMEM((B_HEADS // 2, 2 * B_DV, 2 * B_DK), F32)],
        compiler_params=_params(("parallel", "arbitrary")),
        name="gla",
    )(pab, pab, pab, pab, pab, w2, gb)


def _ret_kernel(q_ref, k_ref, v_ref, g_ref, cos_ref, sin_ref, dmat_ref, zeta_ref, xi_ref,
                cdec_ref, o_ref, st_ref, *, nchunk):
    C = CHUNK
    T = LANES
    W = C_HEADS * C_DK

    @pl.when(pl.program_id(1) == 0)
    def _():
        st_ref[...] = jnp.zeros(st_ref.shape, F32)

    lane_w = lax.broadcasted_iota(I32, (C, W), 1)
    first_half = (lane_w & (C_DK - 1)) < (C_DK // 2)
    lane = lax.broadcasted_iota(I32, (C, T), 1)
    sd_r = lax.broadcasted_iota(I32, (2 * C_DV, T), 0)
    sd_c = lax.broadcasted_iota(I32, (2 * C_DV, T), 1)
    blockdiag = (sd_r >= C_DV) == (sd_c >= C_DK)

    def rotate(x, cos, sin_signed):
        swapped = jnp.where(first_half, pltpu.roll(x, W - C_DK // 2, 1), pltpu.roll(x, C_DK // 2, 1))
        return x * cos + swapped * sin_signed

    pairs = range(C_HEADS // 2)
    lanes = [slice(hp * T, (hp + 1) * T) for hp in pairs]
    G = math.gcd(nchunk, CHUNK_GROUP)

    def group(gi, carry):
        rows = [pl.ds(pl.multiple_of((gi * G + c) * C, C), C) for c in range(G)]
        q_b, k_b, q_x, k_z = [], [], [], []
        for r in rows:
            cos = cos_ref[r, :]
            sin = sin_ref[r, :]
            q = rotate(q_ref[r, :].astype(F32), cos, sin)
            k = rotate(k_ref[r, :].astype(F32), cos, sin) * (C_DK ** -0.5)
            q_b.append(q.astype(BF16))
            k_b.append(k.astype(BF16))
            q_x.append((q * xi_ref[...]).astype(BF16))
            k_z.append((k * zeta_ref[...]).astype(BF16))
        att = {}
        for c in range(G):
            for h in range(C_HEADS):
                ls = lanes[h // 2]
                mine = (lane < C_DK) if h % 2 == 0 else (lane >= C_DK)
                qm = jnp.where(mine, q_b[c][:, ls], jnp.zeros((C, T), BF16))
                s = lax.dot_general(qm, k_b[c][:, ls], NT_DIMS, preferred_element_type=F32)
                att[c, h] = (s * dmat_ref[h]).astype(BF16)
        contrib = {}
        for c in range(G):
            for hp in pairs:
                vp = v_ref[rows[c], hp * 2 * C_DV:(hp + 1) * 2 * C_DV]
                raw = lax.dot_general(vp, k_z[c][:, lanes[hp]], TN_DIMS, preferred_element_type=F32)
                contrib[c, hp] = jnp.where(blockdiag, raw, 0.0)
        before = {}
        for hp in pairs:
            st = st_ref[hp]
            for c in range(G):
                before[c, hp] = st
                st = st * cdec_ref[:, lanes[hp]] + contrib[c, hp]
            st_ref[hp] = st
        for c in range(G):
            for hp in pairs:
                o_inter = lax.dot_general(q_x[c][:, lanes[hp]], before[c, hp].astype(BF16), NT_DIMS,
                                          preferred_element_type=F32)
                for hh in range(2):
                    h = 2 * hp + hh
                    vs = slice(h * C_DV, (h + 1) * C_DV)
                    o = jnp.dot(att[c, h], v_ref[rows[c], vs], preferred_element_type=F32)
                    o = o + o_inter[:, hh * C_DV:(hh + 1) * C_DV]
                    o = o - jnp.mean(o, axis=-1, keepdims=True)
                    o = o * lax.rsqrt(jnp.mean(o * o, axis=-1, keepdims=True) + NORM_EPS)
                    o = o * _silu(g_ref[rows[c], vs].astype(F32))
                    o_ref[rows[c], vs] = o.astype(o_ref.dtype)
        return carry

    lax.fori_loop(0, nchunk // G, group, 0)


def _retention_tables(p):
    log_gamma = np.log(1.0 - np.exp2(-5.0 - np.arange(C_HEADS, dtype=np.float64)))
    i = np.arange(CHUNK, dtype=np.float64)
    diff = i[:, None] - i[None, :]
    dmat = np.where(diff >= 0, np.exp(log_gamma[:, None, None] * np.maximum(diff, 0.0)), 0.0)
    zeta = np.exp(log_gamma[:, None] * (CHUNK - 1 - i))
    xi = np.exp(log_gamma[:, None] * (i + 1))
    cdec = np.exp(log_gamma * CHUNK)
    widen = lambda t: np.repeat(t.T[:, :, None], C_DK, axis=2).reshape(CHUNK, C_HEADS * C_DK)
    half = C_DK // 2
    inv = jnp.asarray(ROPE_BASE, F32) ** (-jnp.arange(half, dtype=F32) / half)
    ang = jnp.arange(p, dtype=jnp.int32).astype(F32)[:, None] * inv[None, :]
    cos = jnp.tile(jnp.cos(ang), (1, 2 * C_HEADS))
    sin = jnp.sin(ang)
    sin_signed = jnp.tile(jnp.concatenate([-sin, sin], axis=1), (1, C_HEADS))
    return (cos, sin_signed, jnp.asarray(dmat, F32), jnp.asarray(widen(zeta), F32),
            jnp.asarray(widen(xi), F32),
            jnp.asarray(np.repeat(cdec, C_DK)[None, :], F32))


def _retention(pcd, tables):
    bsz, p, _ = pcd.shape
    cos, sin, dmat, zeta, xi, cdec = tables
    tc = _pick_tile(p, (640, 128))
    hv = C_HEADS * C_DV
    hk = C_HEADS * C_DK
    full = lambda shape: pl.BlockSpec(shape, lambda b, c: (0,) * len(shape))
    return pl.pallas_call(
        functools.partial(_ret_kernel, nchunk=tc // CHUNK),
        grid=(bsz, p // tc),
        in_specs=[
            pl.BlockSpec((None, tc, hk), lambda b, c: (b, c, 0)),
            pl.BlockSpec((None, tc, hk), lambda b, c: (b, c, 1)),
            pl.BlockSpec((None, tc, hv), lambda b, c: (b, c, 1)),
            pl.BlockSpec((None, tc, hv), lambda b, c: (b, c, 2)),
            pl.BlockSpec((tc, hk), lambda b, c: (c, 0)),
            pl.BlockSpec((tc, hk), lambda b, c: (c, 0)),
            full((C_HEADS, CHUNK, CHUNK)),
            full((CHUNK, hk)),
            full((CHUNK, hk)),
            full((1, hk)),
        ],
        out_specs=pl.BlockSpec((None, tc, hv), lambda b, c: (b, c, 0)),
        out_shape=jax.ShapeDtypeStruct((bsz, p, hv), BF16),
        scratch_shapes=[pltpu.VMEM((C_HEADS // 2, 2 * C_DV, 2 * C_DK), F32)],
        compiler_params=_params(("parallel", "arbitrary")),
        name="retention",
    )(pcd, pcd, pcd, pcd, cos, sin, dmat, zeta, xi, cdec)


def _sb_kernel(q_ref, g_ref, k_ref, v_ref, o_ref, qm_ref, run_ref, acc_ref, hi_ref, lo_ref, ls_ref,
               w_ref):
    T = LANES
    i = pl.program_id(1)
    row = lax.broadcasted_iota(I32, (T, T), 0)
    col = lax.broadcasted_iota(I32, (T, T), 1)
    qpos = i * T + row
    low = col < D_DIM
    u_r = lax.broadcasted_iota(I32, (T, 2 * T), 0)
    u_c = lax.broadcasted_iota(I32, (T, 2 * T), 1)
    suffix = jnp.where((u_c >= T) | (u_r > u_c), 1.0, 0.0).astype(BF16)

    for h in range(D_HEADS):
        pair = q_ref[:, (h // 2) * T:(h // 2 + 1) * T]
        mine = low if h % 2 == 0 else ~low
        qm_ref[h] = jnp.where(mine, pair * (D_DIM ** -0.5), jnp.zeros_like(pair))
    run_ref[...] = jnp.zeros(run_ref.shape, F32)
    acc_ref[...] = jnp.zeros(acc_ref.shape, F32)

    def tile(state):
        t, _ = state
        kt = i - t
        ks = pl.ds(pl.multiple_of(kt * T, T), T)
        kpos = kt * T + col
        ok = (kpos < qpos) & (kpos >= PAD_FRONT)
        for h in range(D_HEADS):
            ps = slice((h // 2) * T, (h // 2 + 1) * T)
            z = lax.dot_general(qm_ref[h], k_ref[ks, ps], NT_DIMS, preferred_element_type=F32)
            sp, logsig = _softplus_parts(z)
            log_1m = jnp.where(ok, -sp, 0.0)
            hi = lax.bitcast_convert_type(lax.bitcast_convert_type(log_1m, I32) & BF16_BITS, F32)
            hi_ref[h] = hi.astype(BF16)
            lo_ref[h] = (log_1m - hi).astype(BF16)
            ls_ref[h] = jnp.where(ok, logsig, NEG_INF)
        slowest = None
        for h in range(D_HEADS):
            sums = (jnp.dot(hi_ref[h], suffix, preferred_element_type=F32)
                    + jnp.dot(lo_ref[h], suffix, preferred_element_type=F32))
            run = run_ref[h]
            w_ref[h] = jnp.exp(ls_ref[h] + run + sums[:, :T]).astype(BF16)
            run = run + sums[:, T:]
            run_ref[h] = run
            slowest = run if slowest is None else jnp.maximum(slowest, run)
        for h in range(D_HEADS):
            ps = slice((h // 2) * T, (h // 2 + 1) * T)
            acc_ref[h] = acc_ref[h] + jnp.dot(w_ref[h], v_ref[ks, ps],
                                              preferred_element_type=F32)
        return t + 1, (jnp.max(slowest) > LOG_F32_UNDERFLOW).astype(I32)

    lax.while_loop(lambda s: (s[0] <= i) & (s[1] > 0), tile, (jnp.int32(0), jnp.int32(1)))

    for hp in range(D_HEADS // 2):
        ps = slice(hp * T, (hp + 1) * T)
        o = jnp.where(low, acc_ref[2 * hp], acc_ref[2 * hp + 1]) * _silu(g_ref[:, ps].astype(F32))
        o_ref[:, ps] = o.astype(o_ref.dtype)


def _stick_breaking(pcd):
    bsz, p, _ = pcd.shape
    T = LANES
    hw = D_HEADS * D_DIM
    one = pl.Buffered(1)
    return pl.pallas_call(
        _sb_kernel,
        grid=(bsz, p // T),
        in_specs=[
            pl.BlockSpec((None, T, hw), lambda b, i: (b, i, 3)),
            pl.BlockSpec((None, T, hw), lambda b, i: (b, i, 6)),
            pl.BlockSpec((None, p, hw), lambda b, i: (b, 0, 4), pipeline_mode=one),
            pl.BlockSpec((None, p, hw), lambda b, i: (b, 0, 5), pipeline_mode=one),
        ],
        out_specs=pl.BlockSpec((None, T, hw), lambda b, i: (b, i, 0)),
        out_shape=jax.ShapeDtypeStruct((bsz, p, hw), BF16),
        scratch_shapes=[
            pltpu.VMEM((D_HEADS, T, T), BF16),
            pltpu.VMEM((D_HEADS, T, T), F32),
            pltpu.VMEM((D_HEADS, T, T), F32),
            pltpu.VMEM((D_HEADS, T, T), BF16),
            pltpu.VMEM((D_HEADS, T, T), BF16),
            pltpu.VMEM((D_HEADS, T, T), F32),
            pltpu.VMEM((D_HEADS, T, T), BF16),
        ],
        compiler_params=_params(("parallel", "arbitrary")),
        name="stick_breaking",
    )(pcd, pcd, pcd, pcd)


def _rel_bucket_np(rel):
    half = REL_BUCKETS // 2
    max_exact = half // 2
    n = -rel
    ret = np.where(n < 0, half, 0)
    n = np.abs(n)
    edges = [math.ceil(max_exact * (REL_MAX_DIST / max_exact) ** (j / (half - max_exact)) - 1e-9)
             for j in range(1, half - max_exact)]
    large = max_exact + sum((n >= e).astype(np.int64) for e in edges)
    return ret + np.where(n < max_exact, n, large)


def _bias_tiles(rel_bias):
    i = np.arange(LANES)[:, None]
    j = np.arange(LANES)[None, :]
    idx = np.stack([_rel_bucket_np(j - i - LANES * d) for d in range(3)])
    assert (idx[2] == idx[2, 0, 0]).all()
    table = rel_bias.astype(F32) - rel_bias.astype(F32)[idx[2, 0, 0]]
    near = jnp.asarray(idx[:2], I32)[None]
    tiles = jnp.zeros((table.shape[1],) + near.shape[1:], F32)
    for bucket in range(REL_BUCKETS):
        tiles = jnp.where(near == bucket, table[bucket][:, None, None, None], tiles)
    return tiles


def _layout_ab(w):
    aq, ak, av, ag, iq, ik, iw, bq, bk, bv, bg, ba = jnp.split(w, np.cumsum(SPLIT_AB)[:-1].tolist(), axis=1)
    pad = lambda t: jnp.pad(t, ((0, 0), (0, LANES - t.shape[1])))
    cols = [aq, ak, av, ag, iq, bv, bg, bq, bk, jnp.concatenate([ik, ik], axis=1), pad(iw), pad(ba),
            jnp.zeros((w.shape[0], W_AB_PAD - W_AB_USED), w.dtype)]
    return jnp.concatenate(cols, axis=1).astype(BF16)


def kernel(x, meta_tokens, rel_bias, norm_g, final_g, w_in_ab, gla_gate_w2, gla_gate_b, w_out_ab,
           w_in_cd, w_out_cd):
    bsz, seq, d = x.shape
    p = seq + PAD_FRONT + N_META
    depth = norm_g.shape[0]
    topk = min(TOPK_MAX, seq // 4)
    h = jnp.concatenate([jnp.zeros((bsz, PAD_FRONT, d), x.dtype),
                         jnp.broadcast_to(meta_tokens.astype(x.dtype)[None], (bsz, N_META, d)),
                         x], axis=1)
    bias_tiles = _bias_tiles(rel_bias)
    tables = _retention_tables(p)
    assert depth >= 1
    for layer in range(depth):
        j = layer // 2
        if layer % 2 == 0:
            pab = _inproj(h, norm_g[layer], _layout_ab(w_in_ab[j]))
            first = _dsa(pab, bias_tiles, topk)
            w2 = jnp.pad(gla_gate_w2[j], ((0, LANES - GLA_GATE_RANK), (0, 0))).astype(BF16)
            second = _gla(pab, w2, gla_gate_b[j].reshape(1, -1).astype(F32))
            w_out = w_out_ab[j]
        else:
            pcd = _inproj(h, norm_g[layer], w_in_cd[j].astype(BF16))
            first = _retention(pcd, tables)
            second = _stick_breaking(pcd)
            w_out = w_out_cd[j]
        if layer + 1 < depth:
            h = _outproj(first, second, w_out, h)
    return _outproj_norm(first, second, w_out, h, final_g, seq)
```

```python
import functools
import math

import numpy as np
import jax
import jax.numpy as jnp
from jax import lax
from jax.experimental import pallas as pl
from jax.experimental.pallas import tpu as pltpu

F32 = jnp.float32
BF16 = jnp.bfloat16
I32 = jnp.int32

CHUNK = 64
N_META = 16
PAD_FRONT = 128 - N_META
NORM_EPS = 1e-6
NEG_INF = -1e30
A_HEADS, A_DIM = 8, 64
IDX_HEADS, IDX_DIM = 8, 64
IDX_SCALE = (IDX_DIM ** -0.5) * (IDX_HEADS ** -0.5)
TOPK_MAX = 256
B_HEADS, B_DK, B_DV = 4, 64, 128
GLA_GATE_RANK = 16
GLA_TAU = 16.0
C_HEADS, C_DK, C_DV = 4, 64, 128
ROPE_BASE = 10000.0
D_HEADS, D_DIM = 8, 64
REL_BUCKETS = 32
REL_MAX_DIST = 128
SPLIT_AB = (512, 512, 512, 512, 512, 64, 8, 256, 256, 512, 512, 16)

LANES = 128
W_AB_USED = 4480
W_AB_PAD = 4608
I32_MIN = -2 ** 31
DUMMY = 7
IDX_BITS = 14
SETTLE_FROM_BIT = 22
LOG_F32_UNDERFLOW = -104.0
LOGIT_LIMIT = 40.0
NORM_SLACK = 1.05
TINY = 1e-30
BF16_BITS = -65536
CHUNK_GROUP = 5
VMEM_LIMIT = 56 * 1024 * 1024

NT_DIMS = (((1,), (1,)), ((), ()))
TN_DIMS = (((0,), (0,)), ((), ()))


def _pick_tile(n, candidates):
    for c in candidates:
        if n % c == 0:
            return c
    raise ValueError(f"no tile for {n}")


def _params(sem):
    return pltpu.CompilerParams(dimension_semantics=sem, vmem_limit_bytes=VMEM_LIMIT)


def _silu(x):
    return x / (1.0 + jnp.exp(-x))


def _softplus_parts(z):
    t = jnp.log(1.0 + jnp.exp(-jnp.abs(z)))
    return jnp.maximum(z, 0.0) + t, jnp.minimum(z, 0.0) - t


def _split3(x):
    a = x.astype(BF16)
    r = x - a.astype(F32)
    b = r.astype(BF16)
    c = (r - b.astype(F32)).astype(BF16)
    return a, b, c


def _inproj_kernel(x_ref, g_ref, w_ref, o_ref, hn_ref):
    @pl.when(pl.program_id(2) == 0)
    def _():
        x = x_ref[...]
        ms = jnp.mean(x * x, axis=-1, keepdims=True)
        hn_ref[...] = (x * lax.rsqrt(ms + NORM_EPS) * g_ref[...]).astype(BF16)

    o_ref[...] = jnp.dot(hn_ref[...], w_ref[...], preferred_element_type=F32).astype(o_ref.dtype)


def _inproj(h, g, w):
    bsz, p, d = h.shape
    n = w.shape[1]
    tp = _pick_tile(p, (1664, 1280, 640, 128))
    tn = _pick_tile(n, (1792, 1536, 896, 512, 128))
    return pl.pallas_call(
        _inproj_kernel,
        grid=(bsz, p // tp, n // tn),
        in_specs=[
            pl.BlockSpec((None, tp, d), lambda b, i, j: (b, i, 0)),
            pl.BlockSpec((1, d), lambda b, i, j: (0, 0)),
            pl.BlockSpec((d, tn), lambda b, i, j: (0, j)),
        ],
        out_specs=pl.BlockSpec((None, tp, tn), lambda b, i, j: (b, i, j)),
        out_shape=jax.ShapeDtypeStruct((bsz, p, n), BF16),
        scratch_shapes=[pltpu.VMEM((tp, d), BF16)],
        compiler_params=_params(("parallel", "parallel", "arbitrary")),
        name="inproj",
    )(h, g.reshape(1, d), w)


def _outproj_kernel(m1_ref, m2_ref, w1_ref, w2_ref, h_ref, o_ref, *, tp):
    y = jnp.dot(m1_ref[...], w1_ref[...], preferred_element_type=F32)
    y = y + jnp.dot(m2_ref[...], w2_ref[...], preferred_element_type=F32)
    pos = pl.program_id(1) * tp + lax.broadcasted_iota(I32, y.shape, 0)
    o_ref[...] = h_ref[...] + jnp.where(pos >= PAD_FRONT, y, 0.0)


def _outproj(m1, m2, w_out, h):
    bsz, p, d = h.shape
    k1 = m1.shape[-1]
    k2 = m2.shape[-1]
    tp = _pick_tile(p, (832, 640, 128))
    w = w_out.astype(BF16)
    return pl.pallas_call(
        functools.partial(_outproj_kernel, tp=tp),
        grid=(bsz, p // tp),
        in_specs=[
            pl.BlockSpec((None, tp, k1), lambda b, i: (b, i, 0)),
            pl.BlockSpec((None, tp, k2), lambda b, i: (b, i, 0)),
            pl.BlockSpec((k1, d), lambda b, i: (0, 0)),
            pl.BlockSpec((k2, d), lambda b, i: (0, 0)),
            pl.BlockSpec((None, tp, d), lambda b, i: (b, i, 0)),
        ],
        out_specs=pl.BlockSpec((None, tp, d), lambda b, i: (b, i, 0)),
        out_shape=jax.ShapeDtypeStruct((bsz, p, d), F32),
        compiler_params=_params(("parallel", "parallel")),
        name="outproj",
    )(m1, m2, w[:k1], w[k1:], h)


def _outproj_norm_kernel(m1_ref, m2_ref, w1_ref, w2_ref, h_ref, g_ref, o_ref):
    y = jnp.dot(m1_ref[...], w1_ref[...], preferred_element_type=F32)
    y = y + jnp.dot(m2_ref[...], w2_ref[...], preferred_element_type=F32)
    x = h_ref[...] + y
    ms = jnp.mean(x * x, axis=-1, keepdims=True)
    o_ref[...] = x * lax.rsqrt(ms + NORM_EPS) * g_ref[...]


def _outproj_norm(m1, m2, w_out, h, g, seq):
    bsz, p, d = h.shape
    k1 = m1.shape[-1]
    k2 = m2.shape[-1]
    tp = LANES
    skip = (p - seq) // tp
    assert skip * tp == p - seq
    w = w_out.astype(BF16)
    rows = lambda b, i: (b, i + skip, 0)
    return pl.pallas_call(
        _outproj_norm_kernel,
        grid=(bsz, seq // tp),
        in_specs=[
            pl.BlockSpec((None, tp, k1), rows),
            pl.BlockSpec((None, tp, k2), rows),
            pl.BlockSpec((k1, d), lambda b, i: (0, 0)),
            pl.BlockSpec((k2, d), lambda b, i: (0, 0)),
            pl.BlockSpec((None, tp, d), rows),
            pl.BlockSpec((1, d), lambda b, i: (0, 0)),
        ],
        out_specs=pl.BlockSpec((None, tp, d), lambda b, i: (b, i, 0)),
        out_shape=jax.ShapeDtypeStruct((bsz, seq, d), F32),
        compiler_params=_params(("parallel", "parallel")),
        name="outproj_norm",
    )(m1, m2, w[:k1], w[k1:], h, g.reshape(1, d))


def _dsa_kernel(aq_ref, ag_ref, iq_ref, iw_ref, k_ref, v_ref, ik_ref, bias_ref, o_ref,
                key_ref, sa_ref, sb_ref, iqs_ref, wts_ref, qm_ref,
                jdx_ref, kmax_ref, mx_ref, ls_ref, acc_ref, *, topk):
    T = LANES
    i = pl.program_id(1)
    last = i + DUMMY

    def num_groups(group):
        return (i + group) // group

    def first_slot(group):
        return last + 1 - group * num_groups(group)

    row = lax.broadcasted_iota(I32, (T, T), 0)
    col = lax.broadcasted_iota(I32, (T, T), 1)
    qchunk = (i * T + row) >> 6
    low = col < 64

    def admissible(kt):
        kpos = kt * T + col
        return (kpos >= PAD_FRONT) & ((kpos >> 6) <= qchunk)

    def tile_rows(slot):
        return pl.ds(pl.multiple_of(jnp.maximum(slot - DUMMY, 0) * T, T), T)

    def for_groups(group, body):
        first = first_slot(group)

        def step(g, carry):
            body(first + group * g)
            return carry

        lax.fori_loop(0, num_groups(group), step, 0)

    iw = iw_ref[...].astype(F32)
    for h in range(IDX_HEADS):
        pair = iq_ref[:, (h // 2) * T:(h // 2 + 1) * T]
        mine = low if h % 2 == 0 else ~low
        iqs_ref[0, h * T:(h + 1) * T, :] = jnp.where(mine, pair, jnp.zeros_like(pair))
        wts_ref[h] = jnp.broadcast_to(iw[:, h:h + 1], (T, T))
    for h in range(A_HEADS):
        pair = aq_ref[:, (h // 2) * T:(h // 2 + 1) * T]
        mine = low if h % 2 == 0 else ~low
        qm_ref[h // 2, (h % 2) * T:(h % 2 + 1) * T, :] = jnp.where(
            mine, pair * (A_DIM ** -0.5), jnp.zeros_like(pair))

    def sweep(lhs_ref, rhs_ref, consume, consume_last):
        group_rows = lhs_ref.shape[1]

        def issue(s0, stage):
            rows = [tile_rows(s0), tile_rows(s0 + 1)]
            for g in range(lhs_ref.shape[0]):
                ps = slice(g * T, (g + 1) * T)
                kw = jnp.concatenate([rhs_ref[rows[0], ps], rhs_ref[rows[1], ps]], axis=0)
                stage[g * group_rows:(g + 1) * group_rows, :] = lax.dot_general(
                    lhs_ref[g], kw, NT_DIMS, preferred_element_type=F32)

        first = first_slot(4)
        issue(first, sa_ref)

        def body(q, carry):
            s = first + 4 * q
            issue(s + 2, sb_ref)
            consume(s, sa_ref)
            issue(s + 4, sa_ref)
            consume(s + 2, sb_ref)
            return carry

        lax.fori_loop(0, num_groups(4) - 1, body, 0)
        issue(last - 1, sb_ref)
        consume(last - 3, sa_ref)
        consume_last(last - 1, sb_ref)

    def score_pair(s0, stage):
        for u in range(2):
            tot = jnp.zeros((T, T), F32)
            for h in range(IDX_HEADS):
                tot = tot + jnp.maximum(stage[h * T:(h + 1) * T, u * T:(u + 1) * T], 0.0) * wts_ref[h]
            score = jnp.where(admissible(s0 + u - DUMMY), tot * IDX_SCALE, NEG_INF)
            bits = lax.bitcast_convert_type(score, I32)
            key_ref[s0 + u] = bits ^ ((bits >> 31) & 0x7FFFFFFF)

    sweep(iqs_ref, ik_ref, score_pair, score_pair)
    for d in range(DUMMY):
        key_ref[d] = jnp.full((T, T), I32_MIN, I32)

    def row_count(pred):
        one = jnp.ones((T, T), I32)
        zero = jnp.zeros((T, T), I32)
        first = first_slot(8)

        def body(g, acc):
            hits = [jnp.where(pred(first + 8 * g + u), one, zero) for u in range(8)]
            while len(hits) > 1:
                hits = [a + b for a, b in zip(hits[::2], hits[1::2])]
            return acc + hits[0]

        acc = lax.fori_loop(0, num_groups(8), body, zero)
        return jnp.sum(acc.astype(F32), axis=-1, keepdims=True)

    def value_bit(it, state):
        t, reach = state
        cand = t + (jnp.int32(1) << (31 - it))
        wide = jnp.broadcast_to(cand, (T, T))
        cnt = row_count(lambda s: key_ref[s] >= wide)
        fits = cnt >= topk
        return jnp.where(fits, cand, t), jnp.where(fits, cnt, reach)

    everything = (8 * T * num_groups(8)).astype(F32)
    state = lax.fori_loop(0, SETTLE_FROM_BIT, value_bit, (jnp.full((T, 1), I32_MIN, I32),
                                                          jnp.full((T, 1), everything, F32)))

    wide = jnp.broadcast_to(state[0], (T, T))
    final = row_count(lambda s: key_ref[s] > wide) < topk

    def unsettled(carry):
        it, (_, reach) = carry
        return (it < 32) & (jnp.max(jnp.where(final, topk, reach)) > topk)

    def two_bits(carry):
        it, state = carry
        return it + 2, value_bit(it + 1, value_bit(it, state))

    _, (thr, reach) = lax.while_loop(unsettled, two_bits, (jnp.int32(SETTLE_FROM_BIT), state))
    thr = jnp.broadcast_to(thr, (T, T))
    jdx_ref[...] = jnp.full((T, T), 2 ** IDX_BITS, I32)

    @pl.when(jnp.max(reach) > topk)
    def _():
        need = topk - (reach - row_count(lambda s: key_ref[s] == thr))
        p_r = lax.broadcasted_iota(I32, (T, 2 * T), 0)
        p_c = lax.broadcasted_iota(I32, (T, 2 * T), 1)
        prefix_ones = jnp.where((p_c >= T) | (p_r <= p_c), 1.0, 0.0).astype(BF16)
        first = first_slot(8)

        def body(g, state):
            seen, best = state
            slots = [first + 8 * g + u for u in range(8)]
            tied = [key_ref[s] == thr for s in slots]
            sums = [jnp.dot(jnp.where(t, 1.0, 0.0).astype(BF16), prefix_ones,
                            preferred_element_type=F32) for t in tied]
            for s, t, sm in zip(slots, tied, sums):
                keep = t & (seen + sm[:, :T] <= need)
                best = jnp.maximum(best, jnp.where(keep, (s - DUMMY) * T + col, -1))
                seen = seen + sm[:, T:]
            return seen, best

        _, best = lax.fori_loop(0, num_groups(8), body,
                                (jnp.zeros((T, T), F32), jnp.full((T, T), -1, I32)))
        jdx_ref[...] = jnp.broadcast_to(jnp.max(best, axis=-1, keepdims=True), (T, T))

    jdx = jdx_ref[...]

    def selection_mask(s):
        key = key_ref[s]
        sel = (key > thr) | ((key == thr) & ((s - DUMMY) * T + col <= jdx))
        return jnp.where(sel & admissible(s - DUMMY), 0.0, NEG_INF)

    head_of_lane = (lax.broadcasted_iota(I32, (A_HEADS * A_DIM, T), 0) // A_DIM
                    == lax.broadcasted_iota(I32, (A_HEADS * A_DIM, T), 1))
    head_sum = jnp.where(head_of_lane, 1.0, 0.0).astype(BF16)

    def head_norms2(t):
        t = t.astype(F32)
        return jnp.dot((t * t).astype(BF16), head_sum, preferred_element_type=F32)

    @pl.when(i == 0)
    def _():
        def widest(t, best):
            return jnp.maximum(best, head_norms2(k_ref[pl.ds(pl.multiple_of(t * T, T), T), :]))
        best = lax.fori_loop(0, k_ref.shape[0] // T, widest, jnp.zeros((T, T), F32))
        kmax_ref[...] = jnp.broadcast_to(jnp.max(best, axis=0, keepdims=True), kmax_ref.shape)

    bound2 = head_norms2(aq_ref[...]) * kmax_ref[0:1, :] * (NORM_SLACK / A_DIM)
    room = LOGIT_LIMIT - jnp.max(jnp.abs(bias_ref[...]))
    small_logits = (room > 0.0) & (jnp.max(bound2) <= room * room)

    ls_ref[...] = jnp.zeros(ls_ref.shape, F32)
    acc_ref[...] = jnp.zeros(acc_ref.shape, F32)

    def attend(s0, stage, near, final):
        madd = [selection_mask(s0), selection_mask(s0 + 1)]
        for hp in range(A_HEADS // 2):
            weights = []
            for h in (2 * hp, 2 * hp + 1):
                parts = [stage[h * T:(h + 1) * T, u * T:(u + 1) * T] + madd[u] for u in range(2)]
                if near:
                    parts = [parts[u] + bias_ref[h, 1 - u] for u in range(2)]
                if not final:
                    mx_ref[h] = jnp.maximum(mx_ref[h], jnp.maximum(parts[0], parts[1]))
                else:
                    es = [jnp.exp(part - mx_ref[h]) for part in parts]
                    ls_ref[h] = ls_ref[h] + (es[0] + es[1])
                    weights.append(jnp.concatenate(es, axis=1).astype(BF16))
            if final:
                ps = slice(hp * T, (hp + 1) * T)
                vw = jnp.concatenate([v_ref[tile_rows(s0), ps], v_ref[tile_rows(s0 + 1), ps]], axis=0)
                acc_ref[hp] = acc_ref[hp] + jnp.dot(jnp.concatenate(weights, axis=0), vw,
                                                    preferred_element_type=F32)

    def attention_sweep(final):
        sweep(qm_ref, k_ref,
              functools.partial(attend, near=False, final=final),
              functools.partial(attend, near=True, final=final))

    @pl.when(small_logits)
    def _():
        mx_ref[...] = jnp.zeros(mx_ref.shape, F32)

    @pl.when(jnp.logical_not(small_logits))
    def _():
        mx_ref[...] = jnp.full(mx_ref.shape, NEG_INF, F32)
        attention_sweep(False)
        for h in range(A_HEADS):
            mx_ref[h] = jnp.broadcast_to(jnp.max(mx_ref[h], axis=-1, keepdims=True), (T, T))

    attention_sweep(True)

    def row_total(h):
        return jnp.maximum(jnp.sum(ls_ref[h], axis=-1, keepdims=True), TINY)

    for hp in range(A_HEADS // 2):
        o0 = acc_ref[hp, :T, :] / row_total(2 * hp)
        o1 = acc_ref[hp, T:, :] / row_total(2 * hp + 1)
        g = ag_ref[:, hp * T:(hp + 1) * T].astype(F32)
        o_ref[:, hp * T:(hp + 1) * T] = (jnp.where(low, o0, o1) * _silu(g)).astype(o_ref.dtype)


def _dsa(pab, bias_tiles, topk):
    bsz, p, _ = pab.shape
    T = LANES
    nq = p // T
    hw = A_HEADS * A_DIM
    one = pl.Buffered(1)
    return pl.pallas_call(
        functools.partial(_dsa_kernel, topk=float(topk)),
        grid=(bsz, nq),
        in_specs=[
            pl.BlockSpec((None, T, hw), lambda b, i: (b, i, 0)),
            pl.BlockSpec((None, T, hw), lambda b, i: (b, i, 3)),
            pl.BlockSpec((None, T, hw), lambda b, i: (b, i, 4)),
            pl.BlockSpec((None, T, T), lambda b, i: (b, i, 33)),
            pl.BlockSpec((None, p, hw), lambda b, i: (b, 0, 1), pipeline_mode=one),
            pl.BlockSpec((None, p, hw), lambda b, i: (b, 0, 2), pipeline_mode=one),
            pl.BlockSpec((None, p, T), lambda b, i: (b, 0, 32), pipeline_mode=one),
            pl.BlockSpec((A_HEADS, 2, T, T), lambda b, i: (0, 0, 0, 0), pipeline_mode=one),
        ],
        out_specs=pl.BlockSpec((None, T, hw), lambda b, i: (b, i, 0)),
        out_shape=jax.ShapeDtypeStruct((bsz, p, hw), BF16),
        scratch_shapes=[
            pltpu.VMEM((nq + DUMMY, T, T), I32),
            pltpu.VMEM((A_HEADS * T, 2 * T), F32),
            pltpu.VMEM((A_HEADS * T, 2 * T), F32),
            pltpu.VMEM((1, IDX_HEADS * T, T), BF16),
            pltpu.VMEM((IDX_HEADS, T, T), F32),
            pltpu.VMEM((A_HEADS // 2, 2 * T, T), BF16),
            pltpu.VMEM((T, T), I32),
            pltpu.VMEM((8, T), F32),
            pltpu.VMEM((A_HEADS, T, T), F32),
            pltpu.VMEM((A_HEADS, T, T), F32),
            pltpu.VMEM((A_HEADS // 2, 2 * T, T), F32),
        ],
        compiler_params=_params(("parallel", "arbitrary")),
        name="dsa",
    )(pab, pab, pab, pab, pab, pab, pab, bias_tiles)


def _gla_kernel(q_ref, k_ref, v_ref, g_ref, a_ref, w2_ref, gb_ref, o_ref, st_ref, *, nchunk):
    C = CHUNK
    T = LANES

    @pl.when(pl.program_id(1) == 0)
    def _():
        st_ref[...] = jnp.zeros(st_ref.shape, F32)

    r_i = lax.broadcasted_iota(I32, (C, C), 0)
    c_i = lax.broadcasted_iota(I32, (C, C), 1)
    causal = c_i <= r_i
    tri = jnp.where(causal, 1.0, 0.0).astype(BF16)
    lane = lax.broadcasted_iota(I32, (C, T), 1)
    sd_r = lax.broadcasted_iota(I32, (2 * B_DV, T), 0)
    sd_c = lax.broadcasted_iota(I32, (2 * B_DV, T), 1)
    blockdiag = (sd_r >= B_DV) == (sd_c >= B_DK)

    pairs = range(B_HEADS // 2)
    lanes = [slice(hp * T, (hp + 1) * T) for hp in pairs]
    G = math.gcd(nchunk, CHUNK_GROUP)

    def group(gi, carry):
        rows = [pl.ds(pl.multiple_of((gi * G + c) * C, C), C) for c in range(G)]
        log_a = [_softplus_parts(jnp.dot(a_ref[r, :], w2_ref[...], preferred_element_type=F32)
                                 + gb_ref[...])[1] * (1.0 / GLA_TAU) for r in rows]
        terms = [_split3(x) for x in log_a]
        bcum = [sum(jnp.dot(tri, t, preferred_element_type=F32) for t in ts) for ts in terms]
        q_t, k_t, k_d, decay = [], [], [], []
        for r, b in zip(rows, bcum):
            b_last = b[C - 1:C, :]
            q = q_ref[r, :].astype(F32) * (B_DK ** -0.5)
            k = k_ref[r, :].astype(F32)
            q_t.append((q * jnp.exp(b)).astype(BF16))
            k_t.append((k * jnp.exp(-b)).astype(BF16))
            k_d.append((k * jnp.exp(b_last - b)).astype(BF16))
            decay.append(jnp.exp(b_last))
        att = {}
        for c in range(G):
            for h in range(B_HEADS):
                ls = lanes[h // 2]
                mine = (lane < B_DK) if h % 2 == 0 else (lane >= B_DK)
                qm = jnp.where(mine, q_t[c][:, ls], jnp.zeros((C, T), BF16))
                s = lax.dot_general(qm, k_t[c][:, ls], NT_DIMS, preferred_element_type=F32)
                att[c, h] = jnp.where(causal, s, 0.0).astype(BF16)
        contrib = {}
        for c in range(G):
            for hp in pairs:
                vp = v_ref[rows[c], hp * 2 * B_DV:(hp + 1) * 2 * B_DV]
                raw = lax.dot_general(vp, k_d[c][:, lanes[hp]], TN_DIMS, preferred_element_type=F32)
                contrib[c, hp] = jnp.where(blockdiag, raw, 0.0)
        before = {}
        for hp in pairs:
            st = st_ref[hp]
            for c in range(G):
                before[c, hp] = st
                st = st * decay[c][:, lanes[hp]] + contrib[c, hp]
            st_ref[hp] = st
        for c in range(G):
            for hp in pairs:
                o_inter = lax.dot_general(q_t[c][:, lanes[hp]], before[c, hp].astype(BF16), NT_DIMS,
                                          preferred_element_type=F32)
                for hh in range(2):
                    h = 2 * hp + hh
                    vs = slice(h * B_DV, (h + 1) * B_DV)
                    o = jnp.dot(att[c, h], v_ref[rows[c], vs], preferred_element_type=F32)
                    o = o + o_inter[:, hh * B_DV:(hh + 1) * B_DV]
                    o = o * lax.rsqrt(jnp.mean(o * o, axis=-1, keepdims=True) + NORM_EPS)
                    o = o * _silu(g_ref[rows[c], vs].astype(F32))
                    o_ref[rows[c], vs] = o.astype(o_ref.dtype)
        return carry

    lax.fori_loop(0, nchunk // G, group, 0)


def _gla(pab, w2, gb):
    bsz, p, _ = pab.shape
    tc = _pick_tile(p, (640, 128))
    hv = B_HEADS * B_DV
    hk = B_HEADS * B_DK
    return pl.pallas_call(
        functools.partial(_gla_kernel, nchunk=tc // CHUNK),
        grid=(bsz, p // tc),
        in_specs=[
            pl.BlockSpec((None, tc, hk), lambda b, c: (b, c, 14)),
            pl.BlockSpec((None, tc, hk), lambda b, c: (b, c, 15)),
            pl.BlockSpec((None, tc, hv), lambda b, c: (b, c, 5)),
            pl.BlockSpec((None, tc, hv), lambda b, c: (b, c, 6)),
            pl.BlockSpec((None, tc, LANES), lambda b, c: (b, c, 34)),
            pl.BlockSpec((LANES, hk), lambda b, c: (0, 0)),
            pl.BlockSpec((1, hk), lambda b, c: (0, 0)),
        ],
        out_specs=pl.BlockSpec((None, tc, hv), lambda b, c: (b, c, 0)),
        out_shape=jax.ShapeDtypeStruct((bsz, p, hv), BF16),
        scratch_shapes=[pltpu.VMEM((B_HEADS // 2, 2 * B_DV, 2 * B_DK), F32)],
        compiler_params=_params(("parallel", "arbitrary")),
        name="gla",
    )(pab, pab, pab, pab, pab, w2, gb)


def _ret_kernel(q_ref, k_ref, v_ref, g_ref, cos_ref, sin_ref, dmat_ref, zeta_ref, xi_ref,
                cdec_ref, o_ref, st_ref, *, nchunk):
    C = CHUNK
    T = LANES
    W = C_HEADS * C_DK

    @pl.when(pl.program_id(1) == 0)
    def _():
        st_ref[...] = jnp.zeros(st_ref.shape, F32)

    lane_w = lax.broadcasted_iota(I32, (C, W), 1)
    first_half = (lane_w & (C_DK - 1)) < (C_DK // 2)
    lane = lax.broadcasted_iota(I32, (C, T), 1)
    sd_r = lax.broadcasted_iota(I32, (2 * C_DV, T), 0)
    sd_c = lax.broadcasted_iota(I32, (2 * C_DV, T), 1)
    blockdiag = (sd_r >= C_DV) == (sd_c >= C_DK)

    def rotate(x, cos, sin_signed):
        swapped = jnp.where(first_half, pltpu.roll(x, W - C_DK // 2, 1), pltpu.roll(x, C_DK // 2, 1))
        return x * cos + swapped * sin_signed

    pairs = range(C_HEADS // 2)
    lanes = [slice(hp * T, (hp + 1) * T) for hp in pairs]
    G = math.gcd(nchunk, CHUNK_GROUP)

    def group(gi, carry):
        rows = [pl.ds(pl.multiple_of((gi * G + c) * C, C), C) for c in range(G)]
        q_b, k_b, q_x, k_z = [], [], [], []
        for r in rows:
            cos = cos_ref[r, :]
            sin = sin_ref[r, :]
            q = rotate(q_ref[r, :].astype(F32), cos, sin)
            k = rotate(k_ref[r, :].astype(F32), cos, sin) * (C_DK ** -0.5)
            q_b.append(q.astype(BF16))
            k_b.append(k.astype(BF16))
            q_x.append((q * xi_ref[...]).astype(BF16))
            k_z.append((k * zeta_ref[...]).astype(BF16))
        att = {}
        for c in range(G):
            for h in range(C_HEADS):
                ls = lanes[h // 2]
                mine = (lane < C_DK) if h % 2 == 0 else (lane >= C_DK)
                qm = jnp.where(mine, q_b[c][:, ls], jnp.zeros((C, T), BF16))
                s = lax.dot_general(qm, k_b[c][:, ls], NT_DIMS, preferred_element_type=F32)
                att[c, h] = (s * dmat_ref[h]).astype(BF16)
        contrib = {}
        for c in range(G):
            for hp in pairs:
                vp = v_ref[rows[c], hp * 2 * C_DV:(hp + 1) * 2 * C_DV]
                raw = lax.dot_general(vp, k_z[c][:, lanes[hp]], TN_DIMS, preferred_element_type=F32)
                contrib[c, hp] = jnp.where(blockdiag, raw, 0.0)
        before = {}
        for hp in pairs:
            st = st_ref[hp]
            for c in range(G):
                before[c, hp] = st
                st = st * cdec_ref[:, lanes[hp]] + contrib[c, hp]
            st_ref[hp] = st
        for c in range(G):
            for hp in pairs:
                o_inter = lax.dot_general(q_x[c][:, lanes[hp]], before[c, hp].astype(BF16), NT_DIMS,
                                          preferred_element_type=F32)
                for hh in range(2):
                    h = 2 * hp + hh
                    vs = slice(h * C_DV, (h + 1) * C_DV)
                    o = jnp.dot(att[c, h], v_ref[rows[c], vs], preferred_element_type=F32)
                    o = o + o_inter[:, hh * C_DV:(hh + 1) * C_DV]
                    o = o - jnp.mean(o, axis=-1, keepdims=True)
                    o = o * lax.rsqrt(jnp.mean(o * o, axis=-1, keepdims=True) + NORM_EPS)
                    o = o * _silu(g_ref[rows[c], vs].astype(F32))
                    o_ref[rows[c], vs] = o.astype(o_ref.dtype)
        return carry

    lax.fori_loop(0, nchunk // G, group, 0)


def _retention_tables(p):
    log_gamma = np.log(1.0 - np.exp2(-5.0 - np.arange(C_HEADS, dtype=np.float64)))
    i = np.arange(CHUNK, dtype=np.float64)
    diff = i[:, None] - i[None, :]
    dmat = np.where(diff >= 0, np.exp(log_gamma[:, None, None] * np.maximum(diff, 0.0)), 0.0)
    zeta = np.exp(log_gamma[:, None] * (CHUNK - 1 - i))
    xi = np.exp(log_gamma[:, None] * (i + 1))
    cdec = np.exp(log_gamma * CHUNK)
    widen = lambda t: np.repeat(t.T[:, :, None], C_DK, axis=2).reshape(CHUNK, C_HEADS * C_DK)
    half = C_DK // 2
    inv = jnp.asarray(ROPE_BASE, F32) ** (-jnp.arange(half, dtype=F32) / half)
    ang = jnp.arange(p, dtype=jnp.int32).astype(F32)[:, None] * inv[None, :]
    cos = jnp.tile(jnp.cos(ang), (1, 2 * C_HEADS))
    sin = jnp.sin(ang)
    sin_signed = jnp.tile(jnp.concatenate([-sin, sin], axis=1), (1, C_HEADS))
    return (cos, sin_signed, jnp.asarray(dmat, F32), jnp.asarray(widen(zeta), F32),
            jnp.asarray(widen(xi), F32),
            jnp.asarray(np.repeat(cdec, C_DK)[None, :], F32))


def _retention(pcd, tables):
    bsz, p, _ = pcd.shape
    cos, sin, dmat, zeta, xi, cdec = tables
    tc = _pick_tile(p, (640, 128))
    hv = C_HEADS * C_DV
    hk = C_HEADS * C_DK
    full = lambda shape: pl.BlockSpec(shape, lambda b, c: (0,) * len(shape))
    return pl.pallas_call(
        functools.partial(_ret_kernel, nchunk=tc // CHUNK),
        grid=(bsz, p // tc),
        in_specs=[
            pl.BlockSpec((None, tc, hk), lambda b, c: (b, c, 0)),
            pl.BlockSpec((None, tc, hk), lambda b, c: (b, c, 1)),
            pl.BlockSpec((None, tc, hv), lambda b, c: (b, c, 1)),
            pl.BlockSpec((None, tc, hv), lambda b, c: (b, c, 2)),
            pl.BlockSpec((tc, hk), lambda b, c: (c, 0)),
            pl.BlockSpec((tc, hk), lambda b, c: (c, 0)),
            full((C_HEADS, CHUNK, CHUNK)),
            full((CHUNK, hk)),
            full((CHUNK, hk)),
            full((1, hk)),
        ],
        out_specs=pl.BlockSpec((None, tc, hv), lambda b, c: (b, c, 0)),
        out_shape=jax.ShapeDtypeStruct((bsz, p, hv), BF16),
        scratch_shapes=[pltpu.VMEM((C_HEADS // 2, 2 * C_DV, 2 * C_DK), F32)],
        compiler_params=_params(("parallel", "arbitrary")),
        name="retention",
    )(pcd, pcd, pcd, pcd, cos, sin, dmat, zeta, xi, cdec)


def _sb_kernel(q_ref, g_ref, k_ref, v_ref, o_ref, qm_ref, run_ref, acc_ref, hi_ref, lo_ref, ls_ref,
               w_ref):
    T = LANES
    i = pl.program_id(1)
    row = lax.broadcasted_iota(I32, (T, T), 0)
    col = lax.broadcasted_iota(I32, (T, T), 1)
    qpos = i * T + row
    low = col < D_DIM
    u_r = lax.broadcasted_iota(I32, (T, 2 * T), 0)
    u_c = lax.broadcasted_iota(I32, (T, 2 * T), 1)
    suffix = jnp.where((u_c >= T) | (u_r > u_c), 1.0, 0.0).astype(BF16)

    for h in range(D_HEADS):
        pair = q_ref[:, (h // 2) * T:(h // 2 + 1) * T]
        mine = low if h % 2 == 0 else ~low
        qm_ref[h] = jnp.where(mine, pair * (D_DIM ** -0.5), jnp.zeros_like(pair))
    run_ref[...] = jnp.zeros(run_ref.shape, F32)
    acc_ref[...] = jnp.zeros(acc_ref.shape, F32)

    def tile(state):
        t, _ = state
        kt = i - t
        ks = pl.ds(pl.multiple_of(kt * T, T), T)
        kpos = kt * T + col
        ok = (kpos < qpos) & (kpos >= PAD_FRONT)
        for h in range(D_HEADS):
            ps = slice((h // 2) * T, (h // 2 + 1) * T)
            z = lax.dot_general(qm_ref[h], k_ref[ks, ps], NT_DIMS, preferred_element_type=F32)
            sp, logsig = _softplus_parts(z)
            log_1m = jnp.where(ok, -sp, 0.0)
            hi = lax.bitcast_convert_type(lax.bitcast_convert_type(log_1m, I32) & BF16_BITS, F32)
            hi_ref[h] = hi.astype(BF16)
            lo_ref[h] = (log_1m - hi).astype(BF16)
            ls_ref[h] = jnp.where(ok, logsig, NEG_INF)
        slowest = None
        for h in range(D_HEADS):
            sums = (jnp.dot(hi_ref[h], suffix, preferred_element_type=F32)
                    + jnp.dot(lo_ref[h], suffix, preferred_element_type=F32))
            run = run_ref[h]
            w_ref[h] = jnp.exp(ls_ref[h] + run + sums[:, :T]).astype(BF16)
            run = run + sums[:, T:]
            run_ref[h] = run
            slowest = run if slowest is None else jnp.maximum(slowest, run)
        for h in range(D_HEADS):
            ps = slice((h // 2) * T, (h // 2 + 1) * T)
            acc_ref[h] = acc_ref[h] + jnp.dot(w_ref[h], v_ref[ks, ps],
                                              preferred_element_type=F32)
        return t + 1, (jnp.max(slowest) > LOG_F32_UNDERFLOW).astype(I32)

    lax.while_loop(lambda s: (s[0] <= i) & (s[1] > 0), tile, (jnp.int32(0), jnp.int32(1)))

    for hp in range(D_HEADS // 2):
        ps = slice(hp * T, (hp + 1) * T)
        o = jnp.where(low, acc_ref[2 * hp], acc_ref[2 * hp + 1]) * _silu(g_ref[:, ps].astype(F32))
        o_ref[:, ps] = o.astype(o_ref.dtype)


def _stick_breaking(pcd):
    bsz, p, _ = pcd.shape
    T = LANES
    hw = D_HEADS * D_DIM
    one = pl.Buffered(1)
    return pl.pallas_call(
        _sb_kernel,
        grid=(bsz, p // T),
        in_specs=[
            pl.BlockSpec((None, T, hw), lambda b, i: (b, i, 3)),
            pl.BlockSpec((None, T, hw), lambda b, i: (b, i, 6)),
            pl.BlockSpec((None, p, hw), lambda b, i: (b, 0, 4), pipeline_mode=one),
            pl.BlockSpec((None, p, hw), lambda b, i: (b, 0, 5), pipeline_mode=one),
        ],
        out_specs=pl.BlockSpec((None, T, hw), lambda b, i: (b, i, 0)),
        out_shape=jax.ShapeDtypeStruct((bsz, p, hw), BF16),
        scratch_shapes=[
            pltpu.VMEM((D_HEADS, T, T), BF16),
            pltpu.VMEM((D_HEADS, T, T), F32),
            pltpu.VMEM((D_HEADS, T, T), F32),
            pltpu.VMEM((D_HEADS, T, T), BF16),
            pltpu.VMEM((D_HEADS, T, T), BF16),
            pltpu.VMEM((D_HEADS, T, T), F32),
            pltpu.VMEM((D_HEADS, T, T), BF16),
        ],
        compiler_params=_params(("parallel", "arbitrary")),
        name="stick_breaking",
    )(pcd, pcd, pcd, pcd)


def _rel_bucket_np(rel):
    half = REL_BUCKETS // 2
    max_exact = half // 2
    n = -rel
    ret = np.where(n < 0, half, 0)
    n = np.abs(n)
    edges = [math.ceil(max_exact * (REL_MAX_DIST / max_exact) ** (j / (half - max_exact)) - 1e-9)
             for j in range(1, half - max_exact)]
    large = max_exact + sum((n >= e).astype(np.int64) for e in edges)
    return ret + np.where(n < max_exact, n, large)


def _bias_tiles(rel_bias):
    i = np.arange(LANES)[:, None]
    j = np.arange(LANES)[None, :]
    idx = np.stack([_rel_bucket_np(j - i - LANES * d) for d in range(3)])
    assert (idx[2] == idx[2, 0, 0]).all()
    table = rel_bias.astype(F32) - rel_bias.astype(F32)[idx[2, 0, 0]]
    near = jnp.asarray(idx[:2], I32)[None]
    tiles = jnp.zeros((table.shape[1],) + near.shape[1:], F32)
    for bucket in range(REL_BUCKETS):
        tiles = jnp.where(near == bucket, table[bucket][:, None, None, None], tiles)
    return tiles


def _layout_ab(w):
    aq, ak, av, ag, iq, ik, iw, bq, bk, bv, bg, ba = jnp.split(w, np.cumsum(SPLIT_AB)[:-1].tolist(), axis=1)
    pad = lambda t: jnp.pad(t, ((0, 0), (0, LANES - t.shape[1])))
    cols = [aq, ak, av, ag, iq, bv, bg, bq, bk, jnp.concatenate([ik, ik], axis=1), pad(iw), pad(ba),
            jnp.zeros((w.shape[0], W_AB_PAD - W_AB_USED), w.dtype)]
    return jnp.concatenate(cols, axis=1).astype(BF16)


def kernel(x, meta_tokens, rel_bias, norm_g, final_g, w_in_ab, gla_gate_w2, gla_gate_b, w_out_ab,
           w_in_cd, w_out_cd):
    bsz, seq, d = x.shape
    p = seq + PAD_FRONT + N_META
    depth = norm_g.shape[0]
    topk = min(TOPK_MAX, seq // 4)
    h = jnp.concatenate([jnp.zeros((bsz, PAD_FRONT, d), x.dtype),
                         jnp.broadcast_to(meta_tokens.astype(x.dtype)[None], (bsz, N_META, d)),
                         x], axis=1)
    bias_tiles = _bias_tiles(rel_bias)
    tables = _retention_tables(p)
    assert depth >= 1
    for layer in range(depth):
        j = layer // 2
        if layer % 2 == 0:
            pab = _inproj(h, norm_g[layer], _layout_ab(w_in_ab[j]))
            first = _dsa(pab, bias_tiles, topk)
            w2 = jnp.pad(gla_gate_w2[j], ((0, LANES - GLA_GATE_RANK), (0, 0))).astype(BF16)
            second = _gla(pab, w2, gla_gate_b[j].reshape(1, -1).astype(F32))
            w_out = w_out_ab[j]
        else:
            pcd = _inproj(h, norm_g[layer], w_in_cd[j].astype(BF16))
            first = _retention(pcd, tables)
            second = _stick_breaking(pcd)
            w_out = w_out_cd[j]
        if layer + 1 < depth:
            h = _outproj(first, second, w_out, h)
    return _outproj_norm(first, second, w_out, h, final_g, seq)
```

```python
import functools
import math

import numpy as np
import jax
import jax.numpy as jnp
from jax import lax
from jax.experimental import pallas as pl
from jax.experimental.pallas import tpu as pltpu

F32 = jnp.float32
BF16 = jnp.bfloat16
I32 = jnp.int32

CHUNK = 64
N_META = 16
PAD_FRONT = 128 - N_META
NORM_EPS = 1e-6
NEG_INF = -1e30
A_HEADS, A_DIM = 8, 64
IDX_HEADS, IDX_DIM = 8, 64
IDX_SCALE = (IDX_DIM ** -0.5) * (IDX_HEADS ** -0.5)
TOPK_MAX = 256
B_HEADS, B_DK, B_DV = 4, 64, 128
GLA_GATE_RANK = 16
GLA_TAU = 16.0
C_HEADS, C_DK, C_DV = 4, 64, 128
ROPE_BASE = 10000.0
D_HEADS, D_DIM = 8, 64
REL_BUCKETS = 32
REL_MAX_DIST = 128
SPLIT_AB = (512, 512, 512, 512, 512, 64, 8, 256, 256, 512, 512, 16)

LANES = 128
W_AB_USED = 4480
W_AB_PAD = 4608
I32_MIN = -2 ** 31
DUMMY = 7
IDX_BITS = 14
SETTLE_FROM_BIT = 18
LOG_F32_UNDERFLOW = -104.0
LOGIT_LIMIT = 40.0
NORM_SLACK = 1.05
TINY = 1e-30
BF16_BITS = -65536
CHUNK_GROUP = 5
VMEM_LIMIT = 56 * 1024 * 1024

NT_DIMS = (((1,), (1,)), ((), ()))
TN_DIMS = (((0,), (0,)), ((), ()))


def _pick_tile(n, candidates):
    for c in candidates:
        if n % c == 0:
            return c
    raise ValueError(f"no tile for {n}")


def _params(sem):
    return pltpu.CompilerParams(dimension_semantics=sem, vmem_limit_bytes=VMEM_LIMIT)


def _silu(x):
    return x / (1.0 + jnp.exp(-x))


def _softplus_parts(z):
    t = jnp.log(1.0 + jnp.exp(-jnp.abs(z)))
    return jnp.maximum(z, 0.0) + t, jnp.minimum(z, 0.0) - t


def _split3(x):
    a = x.astype(BF16)
    r = x - a.astype(F32)
    b = r.astype(BF16)
    c = (r - b.astype(F32)).astype(BF16)
    return a, b, c


def _inproj_kernel(x_ref, g_ref, w_ref, o_ref, hn_ref):
    @pl.when(pl.program_id(2) == 0)
    def _():
        x = x_ref[...]
        ms = jnp.mean(x * x, axis=-1, keepdims=True)
        hn_ref[...] = (x * lax.rsqrt(ms + NORM_EPS) * g_ref[...]).astype(BF16)

    o_ref[...] = jnp.dot(hn_ref[...], w_ref[...], preferred_element_type=F32).astype(o_ref.dtype)


def _inproj(h, g, w):
    bsz, p, d = h.shape
    n = w.shape[1]
    tp = _pick_tile(p, (1664, 1280, 640, 128))
    tn = _pick_tile(n, (1792, 1536, 896, 512, 128))
    return pl.pallas_call(
        _inproj_kernel,
        grid=(bsz, p // tp, n // tn),
        in_specs=[
            pl.BlockSpec((None, tp, d), lambda b, i, j: (b, i, 0)),
            pl.BlockSpec((1, d), lambda b, i, j: (0, 0)),
            pl.BlockSpec((d, tn), lambda b, i, j: (0, j)),
        ],
        out_specs=pl.BlockSpec((None, tp, tn), lambda b, i, j: (b, i, j)),
        out_shape=jax.ShapeDtypeStruct((bsz, p, n), BF16),
        scratch_shapes=[pltpu.VMEM((tp, d), BF16)],
        compiler_params=_params(("parallel", "parallel", "arbitrary")),
        name="inproj",
    )(h, g.reshape(1, d), w)


def _outproj_kernel(m1_ref, m2_ref, w1_ref, w2_ref, h_ref, o_ref, *, tp):
    y = jnp.dot(m1_ref[...], w1_ref[...], preferred_element_type=F32)
    y = y + jnp.dot(m2_ref[...], w2_ref[...], preferred_element_type=F32)
    pos = pl.program_id(1) * tp + lax.broadcasted_iota(I32, y.shape, 0)
    o_ref[...] = h_ref[...] + jnp.where(pos >= PAD_FRONT, y, 0.0)


def _outproj(m1, m2, w_out, h):
    bsz, p, d = h.shape
    k1 = m1.shape[-1]
    k2 = m2.shape[-1]
    tp = _pick_tile(p, (832, 640, 128))
    w = w_out.astype(BF16)
    return pl.pallas_call(
        functools.partial(_outproj_kernel, tp=tp),
        grid=(bsz, p // tp),
        in_specs=[
            pl.BlockSpec((None, tp, k1), lambda b, i: (b, i, 0)),
            pl.BlockSpec((None, tp, k2), lambda b, i: (b, i, 0)),
            pl.BlockSpec((k1, d), lambda b, i: (0, 0)),
            pl.BlockSpec((k2, d), lambda b, i: (0, 0)),
            pl.BlockSpec((None, tp, d), lambda b, i: (b, i, 0)),
        ],
        out_specs=pl.BlockSpec((None, tp, d), lambda b, i: (b, i, 0)),
        out_shape=jax.ShapeDtypeStruct((bsz, p, d), F32),
        compiler_params=_params(("parallel", "parallel")),
        name="outproj",
    )(m1, m2, w[:k1], w[k1:], h)


def _outproj_norm_kernel(m1_ref, m2_ref, w1_ref, w2_ref, h_ref, g_ref, o_ref):
    y = jnp.dot(m1_ref[...], w1_ref[...], preferred_element_type=F32)
    y = y + jnp.dot(m2_ref[...], w2_ref[...], preferred_element_type=F32)
    x = h_ref[...] + y
    ms = jnp.mean(x * x, axis=-1, keepdims=True)
    o_ref[...] = x * lax.rsqrt(ms + NORM_EPS) * g_ref[...]


def _outproj_norm(m1, m2, w_out, h, g, seq):
    bsz, p, d = h.shape
    k1 = m1.shape[-1]
    k2 = m2.shape[-1]
    tp = LANES
    skip = (p - seq) // tp
    assert skip * tp == p - seq
    w = w_out.astype(BF16)
    rows = lambda b, i: (b, i + skip, 0)
    return pl.pallas_call(
        _outproj_norm_kernel,
        grid=(bsz, seq // tp),
        in_specs=[
            pl.BlockSpec((None, tp, k1), rows),
            pl.BlockSpec((None, tp, k2), rows),
            pl.BlockSpec((k1, d), lambda b, i: (0, 0)),
            pl.BlockSpec((k2, d), lambda b, i: (0, 0)),
            pl.BlockSpec((None, tp, d), rows),
            pl.BlockSpec((1, d), lambda b, i: (0, 0)),
        ],
        out_specs=pl.BlockSpec((None, tp, d), lambda b, i: (b, i, 0)),
        out_shape=jax.ShapeDtypeStruct((bsz, seq, d), F32),
        compiler_params=_params(("parallel", "parallel")),
        name="outproj_norm",
    )(m1, m2, w[:k1], w[k1:], h, g.reshape(1, d))


def _dsa_kernel(aq_ref, ag_ref, iq_ref, iw_ref, k_ref, v_ref, ik_ref, bias_ref, o_ref,
                key_ref, sa_ref, sb_ref, iqs_ref, wts_ref, qm_ref,
                jdx_ref, kmax_ref, mx_ref, ls_ref, acc_ref, *, topk):
    T = LANES
    i = pl.program_id(1)
    last = i + DUMMY

    def num_groups(group):
        return (i + group) // group

    def first_slot(group):
        return last + 1 - group * num_groups(group)

    row = lax.broadcasted_iota(I32, (T, T), 0)
    col = lax.broadcasted_iota(I32, (T, T), 1)
    qchunk = (i * T + row) >> 6
    low = col < 64

    def admissible(kt):
        kpos = kt * T + col
        return (kpos >= PAD_FRONT) & ((kpos >> 6) <= qchunk)

    def tile_rows(slot):
        return pl.ds(pl.multiple_of(jnp.maximum(slot - DUMMY, 0) * T, T), T)

    def for_groups(group, body):
        first = first_slot(group)

        def step(g, carry):
            body(first + group * g)
            return carry

        lax.fori_loop(0, num_groups(group), step, 0)

    iw = iw_ref[...].astype(F32)
    for h in range(IDX_HEADS):
        pair = iq_ref[:, (h // 2) * T:(h // 2 + 1) * T]
        mine = low if h % 2 == 0 else ~low
        iqs_ref[0, h * T:(h + 1) * T, :] = jnp.where(mine, pair, jnp.zeros_like(pair))
        wts_ref[h] = jnp.broadcast_to(iw[:, h:h + 1], (T, T))
    for h in range(A_HEADS):
        pair = aq_ref[:, (h // 2) * T:(h // 2 + 1) * T]
        mine = low if h % 2 == 0 else ~low
        qm_ref[h // 2, (h % 2) * T:(h % 2 + 1) * T, :] = jnp.where(
            mine, pair * (A_DIM ** -0.5), jnp.zeros_like(pair))

    def sweep(lhs_ref, rhs_ref, consume, consume_last):
        group_rows = lhs_ref.shape[1]

        def issue(s0, stage):
            rows = [tile_rows(s0), tile_rows(s0 + 1)]
            for g in range(lhs_ref.shape[0]):
                ps = slice(g * T, (g + 1) * T)
                kw = jnp.concatenate([rhs_ref[rows[0], ps], rhs_ref[rows[1], ps]], axis=0)
                stage[g * group_rows:(g + 1) * group_rows, :] = lax.dot_general(
                    lhs_ref[g], kw, NT_DIMS, preferred_element_type=F32)

        first = first_slot(4)
        issue(first, sa_ref)

        def body(q, carry):
            s = first + 4 * q
            issue(s + 2, sb_ref)
            consume(s, sa_ref)
            issue(s + 4, sa_ref)
            consume(s + 2, sb_ref)
            return carry

        lax.fori_loop(0, num_groups(4) - 1, body, 0)
        issue(last - 1, sb_ref)
        consume(last - 3, sa_ref)
        consume_last(last - 1, sb_ref)

    def score_pair(s0, stage):
        for u in range(2):
            tot = jnp.zeros((T, T), F32)
            for h in range(IDX_HEADS):
                tot = tot + jnp.maximum(stage[h * T:(h + 1) * T, u * T:(u + 1) * T], 0.0) * wts_ref[h]
            score = jnp.where(admissible(s0 + u - DUMMY), tot * IDX_SCALE, NEG_INF)
            bits = lax.bitcast_convert_type(score, I32)
            key_ref[s0 + u] = bits ^ ((bits >> 31) & 0x7FFFFFFF)

    sweep(iqs_ref, ik_ref, score_pair, score_pair)
    for d in range(DUMMY):
        key_ref[d] = jnp.full((T, T), I32_MIN, I32)

    def row_count(pred):
        one = jnp.ones((T, T), I32)
        zero = jnp.zeros((T, T), I32)
        first = first_slot(8)

        def body(g, acc):
            hits = [jnp.where(pred(first + 8 * g + u), one, zero) for u in range(8)]
            while len(hits) > 1:
                hits = [a + b for a, b in zip(hits[::2], hits[1::2])]
            return acc + hits[0]

        acc = lax.fori_loop(0, num_groups(8), body, zero)
        return jnp.sum(acc.astype(F32), axis=-1, keepdims=True)

    def value_bit(it, state):
        t, reach = state
        cand = t + (jnp.int32(1) << (31 - it))
        wide = jnp.broadcast_to(cand, (T, T))
        cnt = row_count(lambda s: key_ref[s] >= wide)
        fits = cnt >= topk
        return jnp.where(fits, cand, t), jnp.where(fits, cnt, reach)

    everything = (8 * T * num_groups(8)).astype(F32)
    state = lax.fori_loop(0, SETTLE_FROM_BIT, value_bit, (jnp.full((T, 1), I32_MIN, I32),
                                                          jnp.full((T, 1), everything, F32)))

    wide = jnp.broadcast_to(state[0], (T, T))
    final = row_count(lambda s: key_ref[s] > wide) < topk

    def unsettled(carry):
        it, (_, reach) = carry
        return (it < 32) & (jnp.max(jnp.where(final, topk, reach)) > topk)

    def two_bits(carry):
        it, state = carry
        return it + 2, value_bit(it + 1, value_bit(it, state))

    _, (thr, reach) = lax.while_loop(unsettled, two_bits, (jnp.int32(SETTLE_FROM_BIT), state))
    thr = jnp.broadcast_to(thr, (T, T))
    jdx_ref[...] = jnp.full((T, T), 2 ** IDX_BITS, I32)

    @pl.when(jnp.max(reach) > topk)
    def _():
        need = topk - (reach - row_count(lambda s: key_ref[s] == thr))
        p_r = lax.broadcasted_iota(I32, (T, 2 * T), 0)
        p_c = lax.broadcasted_iota(I32, (T, 2 * T), 1)
        prefix_ones = jnp.where((p_c >= T) | (p_r <= p_c), 1.0, 0.0).astype(BF16)
        first = first_slot(8)

        def body(g, state):
            seen, best = state
            slots = [first + 8 * g + u for u in range(8)]
            tied = [key_ref[s] == thr for s in slots]
            sums = [jnp.dot(jnp.where(t, 1.0, 0.0).astype(BF16), prefix_ones,
                            preferred_element_type=F32) for t in tied]
            for s, t, sm in zip(slots, tied, sums):
                keep = t & (seen + sm[:, :T] <= need)
                best = jnp.maximum(best, jnp.where(keep, (s - DUMMY) * T + col, -1))
                seen = seen + sm[:, T:]
            return seen, best

        _, best = lax.fori_loop(0, num_groups(8), body,
                                (jnp.zeros((T, T), F32), jnp.full((T, T), -1, I32)))
        jdx_ref[...] = jnp.broadcast_to(jnp.max(best, axis=-1, keepdims=True), (T, T))

    jdx = jdx_ref[...]

    def selection_mask(s):
        key = key_ref[s]
        sel = (key > thr) | ((key == thr) & ((s - DUMMY) * T + col <= jdx))
        return jnp.where(sel & admissible(s - DUMMY), 0.0, NEG_INF)

    head_of_lane = (lax.broadcasted_iota(I32, (A_HEADS * A_DIM, T), 0) // A_DIM
                    == lax.broadcasted_iota(I32, (A_HEADS * A_DIM, T), 1))
    head_sum = jnp.where(head_of_lane, 1.0, 0.0).astype(BF16)

    def head_norms2(t):
        t = t.astype(F32)
        return jnp.dot((t * t).astype(BF16), head_sum, preferred_element_type=F32)

    @pl.when(i == 0)
    def _():
        def widest(t, best):
            return jnp.maximum(best, head_norms2(k_ref[pl.ds(pl.multiple_of(t * T, T), T), :]))
        best = lax.fori_loop(0, k_ref.shape[0] // T, widest, jnp.zeros((T, T), F32))
        kmax_ref[...] = jnp.broadcast_to(jnp.max(best, axis=0, keepdims=True), kmax_ref.shape)

    bound2 = head_norms2(aq_ref[...]) * kmax_ref[0:1, :] * (NORM_SLACK / A_DIM)
    room = LOGIT_LIMIT - jnp.max(jnp.abs(bias_ref[...]))
    small_logits = (room > 0.0) & (jnp.max(bound2) <= room * room)

    ls_ref[...] = jnp.zeros(ls_ref.shape, F32)
    acc_ref[...] = jnp.zeros(acc_ref.shape, F32)

    def attend(s0, stage, near, final):
        madd = [selection_mask(s0), selection_mask(s0 + 1)]
        for hp in range(A_HEADS // 2):
            weights = []
            for h in (2 * hp, 2 * hp + 1):
                parts = [stage[h * T:(h + 1) * T, u * T:(u + 1) * T] + madd[u] for u in range(2)]
                if near:
                    parts = [parts[u] + bias_ref[h, 1 - u] for u in range(2)]
                if not final:
                    mx_ref[h] = jnp.maximum(mx_ref[h], jnp.maximum(parts[0], parts[1]))
                else:
                    es = [jnp.exp(part - mx_ref[h]) for part in parts]
                    ls_ref[h] = ls_ref[h] + (es[0] + es[1])
                    weights.append(jnp.concatenate(es, axis=1).astype(BF16))
            if final:
                ps = slice(hp * T, (hp + 1) * T)
                vw = jnp.concatenate([v_ref[tile_rows(s0), ps], v_ref[tile_rows(s0 + 1), ps]], axis=0)
                acc_ref[hp] = acc_ref[hp] + jnp.dot(jnp.concatenate(weights, axis=0), vw,
                                                    preferred_element_type=F32)

    def attention_sweep(final):
        sweep(qm_ref, k_ref,
              functools.partial(attend, near=False, final=final),
              functools.partial(attend, near=True, final=final))

    @pl.when(small_logits)
    def _():
        mx_ref[...] = jnp.zeros(mx_ref.shape, F32)

    @pl.when(jnp.logical_not(small_logits))
    def _():
        mx_ref[...] = jnp.full(mx_ref.shape, NEG_INF, F32)
        attention_sweep(False)
        for h in range(A_HEADS):
            mx_ref[h] = jnp.broadcast_to(jnp.max(mx_ref[h], axis=-1, keepdims=True), (T, T))

    attention_sweep(True)

    def row_total(h):
        return jnp.maximum(jnp.sum(ls_ref[h], axis=-1, keepdims=True), TINY)

    for hp in range(A_HEADS // 2):
        o0 = acc_ref[hp, :T, :] / row_total(2 * hp)
        o1 = acc_ref[hp, T:, :] / row_total(2 * hp + 1)
        g = ag_ref[:, hp * T:(hp + 1) * T].astype(F32)
        o_ref[:, hp * T:(hp + 1) * T] = (jnp.where(low, o0, o1) * _silu(g)).astype(o_ref.dtype)


def _dsa(pab, bias_tiles, topk):
    bsz, p, _ = pab.shape
    T = LANES
    nq = p // T
    hw = A_HEADS * A_DIM
    one = pl.Buffered(1)
    return pl.pallas_call(
        functools.partial(_dsa_kernel, topk=float(topk)),
        grid=(bsz, nq),
        in_specs=[
            pl.BlockSpec((None, T, hw), lambda b, i: (b, i, 0)),
            pl.BlockSpec((None, T, hw), lambda b, i: (b, i, 3)),
            pl.BlockSpec((None, T, hw), lambda b, i: (b, i, 4)),
            pl.BlockSpec((None, T, T), lambda b, i: (b, i, 33)),
            pl.BlockSpec((None, p, hw), lambda b, i: (b, 0, 1), pipeline_mode=one),
            pl.BlockSpec((None, p, hw), lambda b, i: (b, 0, 2), pipeline_mode=one),
            pl.BlockSpec((None, p, T), lambda b, i: (b, 0, 32), pipeline_mode=one),
            pl.BlockSpec((A_HEADS, 2, T, T), lambda b, i: (0, 0, 0, 0), pipeline_mode=one),
        ],
        out_specs=pl.BlockSpec((None, T, hw), lambda b, i: (b, i, 0)),
        out_shape=jax.ShapeDtypeStruct((bsz, p, hw), BF16),
        scratch_shapes=[
            pltpu.VMEM((nq + DUMMY, T, T), I32),
            pltpu.VMEM((A_HEADS * T, 2 * T), F32),
            pltpu.VMEM((A_HEADS * T, 2 * T), F32),
            pltpu.VMEM((1, IDX_HEADS * T, T), BF16),
            pltpu.VMEM((IDX_HEADS, T, T), F32),
            pltpu.VMEM((A_HEADS // 2, 2 * T, T), BF16),
            pltpu.VMEM((T, T), I32),
            pltpu.VMEM((8, T), F32),
            pltpu.VMEM((A_HEADS, T, T), F32),
            pltpu.VMEM((A_HEADS, T, T), F32),
            pltpu.VMEM((A_HEADS // 2, 2 * T, T), F32),
        ],
        compiler_params=_params(("parallel", "arbitrary")),
        name="dsa",
    )(pab, pab, pab, pab, pab, pab, pab, bias_tiles)


def _gla_kernel(q_ref, k_ref, v_ref, g_ref, a_ref, w2_ref, gb_ref, o_ref, st_ref, *, nchunk):
    C = CHUNK
    T = LANES

    @pl.when(pl.program_id(1) == 0)
    def _():
        st_ref[...] = jnp.zeros(st_ref.shape, F32)

    r_i = lax.broadcasted_iota(I32, (C, C), 0)
    c_i = lax.broadcasted_iota(I32, (C, C), 1)
    causal = c_i <= r_i
    tri = jnp.where(causal, 1.0, 0.0).astype(BF16)
    lane = lax.broadcasted_iota(I32, (C, T), 1)
    sd_r = lax.broadcasted_iota(I32, (2 * B_DV, T), 0)
    sd_c = lax.broadcasted_iota(I32, (2 * B_DV, T), 1)
    blockdiag = (sd_r >= B_DV) == (sd_c >= B_DK)

    pairs = range(B_HEADS // 2)
    lanes = [slice(hp * T, (hp + 1) * T) for hp in pairs]
    G = math.gcd(nchunk, CHUNK_GROUP)

    def group(gi, carry):
        rows = [pl.ds(pl.multiple_of((gi * G + c) * C, C), C) for c in range(G)]
        log_a = [_softplus_parts(jnp.dot(a_ref[r, :], w2_ref[...], preferred_element_type=F32)
                                 + gb_ref[...])[1] * (1.0 / GLA_TAU) for r in rows]
        terms = [_split3(x) for x in log_a]
        bcum = [sum(jnp.dot(tri, t, preferred_element_type=F32) for t in ts) for ts in terms]
        q_t, k_t, k_d, decay = [], [], [], []
        for r, b in zip(rows, bcum):
            b_last = b[C - 1:C, :]
            q = q_ref[r, :].astype(F32) * (B_DK ** -0.5)
            k = k_ref[r, :].astype(F32)
            q_t.append((q * jnp.exp(b)).astype(BF16))
            k_t.append((k * jnp.exp(-b)).astype(BF16))
            k_d.append((k * jnp.exp(b_last - b)).astype(BF16))
            decay.append(jnp.exp(b_last))
        att = {}
        for c in range(G):
            for h in range(B_HEADS):
                ls = lanes[h // 2]
                mine = (lane < B_DK) if h % 2 == 0 else (lane >= B_DK)
                qm = jnp.where(mine, q_t[c][:, ls], jnp.zeros((C, T), BF16))
                s = lax.dot_general(qm, k_t[c][:, ls], NT_DIMS, preferred_element_type=F32)
                att[c, h] = jnp.where(causal, s, 0.0).astype(BF16)
        contrib = {}
        for c in range(G):
            for hp in pairs:
                vp = v_ref[rows[c], hp * 2 * B_DV:(hp + 1) * 2 * B_DV]
                raw = lax.dot_general(vp, k_d[c][:, lanes[hp]], TN_DIMS, preferred_element_type=F32)
                contrib[c, hp] = jnp.where(blockdiag, raw, 0.0)
        before = {}
        for hp in pairs:
            st = st_ref[hp]
            for c in range(G):
                before[c, hp] = st
                st = st * decay[c][:, lanes[hp]] + contrib[c, hp]
            st_ref[hp] = st
        for c in range(G):
            for hp in pairs:
                o_inter = lax.dot_general(q_t[c][:, lanes[hp]], before[c, hp].astype(BF16), NT_DIMS,
                                          preferred_element_type=F32)
                for hh in range(2):
                    h = 2 * hp + hh
                    vs = slice(h * B_DV, (h + 1) * B_DV)
                    o = jnp.dot(att[c, h], v_ref[rows[c], vs], preferred_element_type=F32)
                    o = o + o_inter[:, hh * B_DV:(hh + 1) * B_DV]
                    o = o * lax.rsqrt(jnp.mean(o * o, axis=-1, keepdims=True) + NORM_EPS)
                    o = o * _silu(g_ref[rows[c], vs].astype(F32))
                    o_ref[rows[c], vs] = o.astype(o_ref.dtype)
        return carry

    lax.fori_loop(0, nchunk // G, group, 0)


def _gla(pab, w2, gb):
    bsz, p, _ = pab.shape
    tc = _pick_tile(p, (640, 128))
    hv = B_HEADS * B_DV
    hk = B_HEADS * B_DK
    return pl.pallas_call(
        functools.partial(_gla_kernel, nchunk=tc // CHUNK),
        grid=(bsz, p // tc),
        in_specs=[
            pl.BlockSpec((None, tc, hk), lambda b, c: (b, c, 14)),
            pl.BlockSpec((None, tc, hk), lambda b, c: (b, c, 15)),
            pl.BlockSpec((None, tc, hv), lambda b, c: (b, c, 5)),
            pl.BlockSpec((None, tc, hv), lambda b, c: (b, c, 6)),
            pl.BlockSpec((None, tc, LANES), lambda b, c: (b, c, 34)),
            pl.BlockSpec((LANES, hk), lambda b, c: (0, 0)),
            pl.BlockSpec((1, hk), lambda b, c: (0, 0)),
        ],
        out_specs=pl.BlockSpec((None, tc, hv), lambda b, c: (b, c, 0)),
        out_shape=jax.ShapeDtypeStruct((bsz, p, hv), BF16),
        scratch_shapes=[pltpu.VMEM((B_HEADS // 2, 2 * B_DV, 2 * B_DK), F32)],
        compiler_params=_params(("parallel", "arbitrary")),
        name="gla",
    )(pab, pab, pab, pab, pab, w2, gb)


def _ret_kernel(q_ref, k_ref, v_ref, g_ref, cos_ref, sin_ref, dmat_ref, zeta_ref, xi_ref,
                cdec_ref, o_ref, st_ref, *, nchunk):
    C = CHUNK
    T = LANES
    W = C_HEADS * C_DK

    @pl.when(pl.program_id(1) == 0)
    def _():
        st_ref[...] = jnp.zeros(st_ref.shape, F32)

    lane_w = lax.broadcasted_iota(I32, (C, W), 1)
    first_half = (lane_w & (C_DK - 1)) < (C_DK // 2)
    lane = lax.broadcasted_iota(I32, (C, T), 1)
    sd_r = lax.broadcasted_iota(I32, (2 * C_DV, T), 0)
    sd_c = lax.broadcasted_iota(I32, (2 * C_DV, T), 1)
    blockdiag = (sd_r >= C_DV) == (sd_c >= C_DK)

    def rotate(x, cos, sin_signed):
        swapped = jnp.where(first_half, pltpu.roll(x, W - C_DK // 2, 1), pltpu.roll(x, C_DK // 2, 1))
        return x * cos + swapped * sin_signed

    pairs = range(C_HEADS // 2)
    lanes = [slice(hp * T, (hp + 1) * T) for hp in pairs]
    G = math.gcd(nchunk, CHUNK_GROUP)

    def group(gi, carry):
        rows = [pl.ds(pl.multiple_of((gi * G + c) * C, C), C) for c in range(G)]
        q_b, k_b, q_x, k_z = [], [], [], []
        for r in rows:
            cos = cos_ref[r, :]
            sin = sin_ref[r, :]
            q = rotate(q_ref[r, :].astype(F32), cos, sin)
            k = rotate(k_ref[r, :].astype(F32), cos, sin) * (C_DK ** -0.5)
            q_b.append(q.astype(BF16))
            k_b.append(k.astype(BF16))
            q_x.append((q * xi_ref[...]).astype(BF16))
            k_z.append((k * zeta_ref[...]).astype(BF16))
        att = {}
        for c in range(G):
            for h in range(C_HEADS):
                ls = lanes[h // 2]
                mine = (lane < C_DK) if h % 2 == 0 else (lane >= C_DK)
                qm = jnp.where(mine, q_b[c][:, ls], jnp.zeros((C, T), BF16))
                s = lax.dot_general(qm, k_b[c][:, ls], NT_DIMS, preferred_element_type=F32)
                att[c, h] = (s * dmat_ref[h]).astype(BF16)
        contrib = {}
        for c in range(G):
            for hp in pairs:
                vp = v_ref[rows[c], hp * 2 * C_DV:(hp + 1) * 2 * C_DV]
                raw = lax.dot_general(vp, k_z[c][:, lanes[hp]], TN_DIMS, preferred_element_type=F32)
                contrib[c, hp] = jnp.where(blockdiag, raw, 0.0)
        before = {}
        for hp in pairs:
            st = st_ref[hp]
            for c in range(G):
                before[c, hp] = st
                st = st * cdec_ref[:, lanes[hp]] + contrib[c, hp]
            st_ref[hp] = st
        for c in range(G):
            for hp in pairs:
                o_inter = lax.dot_general(q_x[c][:, lanes[hp]], before[c, hp].astype(BF16), NT_DIMS,
                                          preferred_element_type=F32)
                for hh in range(2):
                    h = 2 * hp + hh
                    vs = slice(h * C_DV, (h + 1) * C_DV)
                    o = jnp.dot(att[c, h], v_ref[rows[c], vs], preferred_element_type=F32)
                    o = o + o_inter[:, hh * C_DV:(hh + 1) * C_DV]
                    o = o - jnp.mean(o, axis=-1, keepdims=True)
                    o = o * lax.rsqrt(jnp.mean(o * o, axis=-1, keepdims=True) + NORM_EPS)
                    o = o * _silu(g_ref[rows[c], vs].astype(F32))
                    o_ref[rows[c], vs] = o.astype(o_ref.dtype)
        return carry

    lax.fori_loop(0, nchunk // G, group, 0)


def _retention_tables(p):
    log_gamma = np.log(1.0 - np.exp2(-5.0 - np.arange(C_HEADS, dtype=np.float64)))
    i = np.arange(CHUNK, dtype=np.float64)
    diff = i[:, None] - i[None, :]
    dmat = np.where(diff >= 0, np.exp(log_gamma[:, None, None] * np.maximum(diff, 0.0)), 0.0)
    zeta = np.exp(log_gamma[:, None] * (CHUNK - 1 - i))
    xi = np.exp(log_gamma[:, None] * (i + 1))
    cdec = np.exp(log_gamma * CHUNK)
    widen = lambda t: np.repeat(t.T[:, :, None], C_DK, axis=2).reshape(CHUNK, C_HEADS * C_DK)
    half = C_DK // 2
    inv = jnp.asarray(ROPE_BASE, F32) ** (-jnp.arange(half, dtype=F32) / half)
    ang = jnp.arange(p, dtype=jnp.int32).astype(F32)[:, None] * inv[None, :]
    cos = jnp.tile(jnp.cos(ang), (1, 2 * C_HEADS))
    sin = jnp.sin(ang)
    sin_signed = jnp.tile(jnp.concatenate([-sin, sin], axis=1), (1, C_HEADS))
    return (cos, sin_signed, jnp.asarray(dmat, F32), jnp.asarray(widen(zeta), F32),
            jnp.asarray(widen(xi), F32),
            jnp.asarray(np.repeat(cdec, C_DK)[None, :], F32))


def _retention(pcd, tables):
    bsz, p, _ = pcd.shape
    cos, sin, dmat, zeta, xi, cdec = tables
    tc = _pick_tile(p, (640, 128))
    hv = C_HEADS * C_DV
    hk = C_HEADS * C_DK
    full = lambda shape: pl.BlockSpec(shape, lambda b, c: (0,) * len(shape))
    return pl.pallas_call(
        functools.partial(_ret_kernel, nchunk=tc // CHUNK),
        grid=(bsz, p // tc),
        in_specs=[
            pl.BlockSpec((None, tc, hk), lambda b, c: (b, c, 0)),
            pl.BlockSpec((None, tc, hk), lambda b, c: (b, c, 1)),
            pl.BlockSpec((None, tc, hv), lambda b, c: (b, c, 1)),
            pl.BlockSpec((None, tc, hv), lambda b, c: (b, c, 2)),
            pl.BlockSpec((tc, hk), lambda b, c: (c, 0)),
            pl.BlockSpec((tc, hk), lambda b, c: (c, 0)),
            full((C_HEADS, CHUNK, CHUNK)),
            full((CHUNK, hk)),
            full((CHUNK, hk)),
            full((1, hk)),
        ],
        out_specs=pl.BlockSpec((None, tc, hv), lambda b, c: (b, c, 0)),
        out_shape=jax.ShapeDtypeStruct((bsz, p, hv), BF16),
        scratch_shapes=[pltpu.VMEM((C_HEADS // 2, 2 * C_DV, 2 * C_DK), F32)],
        compiler_params=_params(("parallel", "arbitrary")),
        name="retention",
    )(pcd, pcd, pcd, pcd, cos, sin, dmat, zeta, xi, cdec)


def _sb_kernel(q_ref, g_ref, k_ref, v_ref, o_ref, qm_ref, run_ref, acc_ref, hi_ref, lo_ref, ls_ref,
               w_ref):
    T = LANES
    i = pl.program_id(1)
    row = lax.broadcasted_iota(I32, (T, T), 0)
    col = lax.broadcasted_iota(I32, (T, T), 1)
    qpos = i * T + row
    low = col < D_DIM
    u_r = lax.broadcasted_iota(I32, (T, 2 * T), 0)
    u_c = lax.broadcasted_iota(I32, (T, 2 * T), 1)
    suffix = jnp.where((u_c >= T) | (u_r > u_c), 1.0, 0.0).astype(BF16)

    for h in range(D_HEADS):
        pair = q_ref[:, (h // 2) * T:(h // 2 + 1) * T]
        mine = low if h % 2 == 0 else ~low
        qm_ref[h] = jnp.where(mine, pair * (D_DIM ** -0.5), jnp.zeros_like(pair))
    run_ref[...] = jnp.zeros(run_ref.shape, F32)
    acc_ref[...] = jnp.zeros(acc_ref.shape, F32)

    def tile(state):
        t, _ = state
        kt = i - t
        ks = pl.ds(pl.multiple_of(kt * T, T), T)
        kpos = kt * T + col
        ok = (kpos < qpos) & (kpos >= PAD_FRONT)
        for h in range(D_HEADS):
            ps = slice((h // 2) * T, (h // 2 + 1) * T)
            z = lax.dot_general(qm_ref[h], k_ref[ks, ps], NT_DIMS, preferred_element_type=F32)
            sp, logsig = _softplus_parts(z)
            log_1m = jnp.where(ok, -sp, 0.0)
            hi = lax.bitcast_convert_type(lax.bitcast_convert_type(log_1m, I32) & BF16_BITS, F32)
            hi_ref[h] = hi.astype(BF16)
            lo_ref[h] = (log_1m - hi).astype(BF16)
            ls_ref[h] = jnp.where(ok, logsig, NEG_INF)
        slowest = None
        for h in range(D_HEADS):
            sums = (jnp.dot(hi_ref[h], suffix, preferred_element_type=F32)
                    + jnp.dot(lo_ref[h], suffix, preferred_element_type=F32))
            run = run_ref[h]
            w_ref[h] = jnp.exp(ls_ref[h] + run + sums[:, :T]).astype(BF16)
            run = run + sums[:, T:]
            run_ref[h] = run
            slowest = run if slowest is None else jnp.maximum(slowest, run)
        for h in range(D_HEADS):
            ps = slice((h // 2) * T, (h // 2 + 1) * T)
            acc_ref[h] = acc_ref[h] + jnp.dot(w_ref[h], v_ref[ks, ps],
                                              preferred_element_type=F32)
        return t + 1, (jnp.max(slowest) > LOG_F32_UNDERFLOW).astype(I32)

    lax.while_loop(lambda s: (s[0] <= i) & (s[1] > 0), tile, (jnp.int32(0), jnp.int32(1)))

    for hp in range(D_HEADS // 2):
        ps = slice(hp * T, (hp + 1) * T)
        o = jnp.where(low, acc_ref[2 * hp], acc_ref[2 * hp + 1]) * _silu(g_ref[:, ps].astype(F32))
        o_ref[:, ps] = o.astype(o_ref.dtype)


def _stick_breaking(pcd):
    bsz, p, _ = pcd.shape
    T = LANES
    hw = D_HEADS * D_DIM
    one = pl.Buffered(1)
    return pl.pallas_call(
        _sb_kernel,
        grid=(bsz, p // T),
        in_specs=[
            pl.BlockSpec((None, T, hw), lambda b, i: (b, i, 3)),
            pl.BlockSpec((None, T, hw), lambda b, i: (b, i, 6)),
            pl.BlockSpec((None, p, hw), lambda b, i: (b, 0, 4), pipeline_mode=one),
            pl.BlockSpec((None, p, hw), lambda b, i: (b, 0, 5), pipeline_mode=one),
        ],
        out_specs=pl.BlockSpec((None, T, hw), lambda b, i: (b, i, 0)),
        out_shape=jax.ShapeDtypeStruct((bsz, p, hw), BF16),
        scratch_shapes=[
            pltpu.VMEM((D_HEADS, T, T), BF16),
            pltpu.VMEM((D_HEADS, T, T), F32),
            pltpu.VMEM((D_HEADS, T, T), F32),
            pltpu.VMEM((D_HEADS, T, T), BF16),
            pltpu.VMEM((D_HEADS, T, T), BF16),
            pltpu.VMEM((D_HEADS, T, T), F32),
            pltpu.VMEM((D_HEADS, T, T), BF16),
        ],
        compiler_params=_params(("parallel", "arbitrary")),
        name="stick_breaking",
    )(pcd, pcd, pcd, pcd)


def _rel_bucket_np(rel):
    half = REL_BUCKETS // 2
    max_exact = half // 2
    n = -rel
    ret = np.where(n < 0, half, 0)
    n = np.abs(n)
    edges = [math.ceil(max_exact * (REL_MAX_DIST / max_exact) ** (j / (half - max_exact)) - 1e-9)
             for j in range(1, half - max_exact)]
    large = max_exact + sum((n >= e).astype(np.int64) for e in edges)
    return ret + np.where(n < max_exact, n, large)


def _bias_tiles(rel_bias):
    i = np.arange(LANES)[:, None]
    j = np.arange(LANES)[None, :]
    idx = np.stack([_rel_bucket_np(j - i - LANES * d) for d in range(3)])
    assert (idx[2] == idx[2, 0, 0]).all()
    table = rel_bias.astype(F32) - rel_bias.astype(F32)[idx[2, 0, 0]]
    near = jnp.asarray(idx[:2], I32)[None]
    tiles = jnp.zeros((table.shape[1],) + near.shape[1:], F32)
    for bucket in range(REL_BUCKETS):
        tiles = jnp.where(near == bucket, table[bucket][:, None, None, None], tiles)
    return tiles


def _layout_ab(w):
    aq, ak, av, ag, iq, ik, iw, bq, bk, bv, bg, ba = jnp.split(w, np.cumsum(SPLIT_AB)[:-1].tolist(), axis=1)
    pad = lambda t: jnp.pad(t, ((0, 0), (0, LANES - t.shape[1])))
    cols = [aq, ak, av, ag, iq, bv, bg, bq, bk, jnp.concatenate([ik, ik], axis=1), pad(iw), pad(ba),
            jnp.zeros((w.shape[0], W_AB_PAD - W_AB_USED), w.dtype)]
    return jnp.concatenate(cols, axis=1).astype(BF16)


def kernel(x, meta_tokens, rel_bias, norm_g, final_g, w_in_ab, gla_gate_w2, gla_gate_b, w_out_ab,
           w_in_cd, w_out_cd):
    bsz, seq, d = x.shape
    p = seq + PAD_FRONT + N_META
    depth = norm_g.shape[0]
    topk = min(TOPK_MAX, seq // 4)
    h = jnp.concatenate([jnp.zeros((bsz, PAD_FRONT, d), x.dtype),
                         jnp.broadcast_to(meta_tokens.astype(x.dtype)[None], (bsz, N_META, d)),
                         x], axis=1)
    bias_tiles = _bias_tiles(rel_bias)
    tables = _retention_tables(p)
    assert depth >= 1
    for layer in range(depth):
        j = layer // 2
        if layer % 2 == 0:
            pab = _inproj(h, norm_g[layer], _layout_ab(w_in_ab[j]))
            first = _dsa(pab, bias_tiles, topk)
            w2 = jnp.pad(gla_gate_w2[j], ((0, LANES - GLA_GATE_RANK), (0, 0))).astype(BF16)
            second = _gla(pab, w2, gla_gate_b[j].reshape(1, -1).astype(F32))
            w_out = w_out_ab[j]
        else:
            pcd = _inproj(h, norm_g[layer], w_in_cd[j].astype(BF16))
            first = _retention(pcd, tables)
            second = _stick_breaking(pcd)
            w_out = w_out_cd[j]
        if layer + 1 < depth:
            h = _outproj(first, second, w_out, h)
    return _outproj_norm(first, second, w_out, h, final_g, seq)
```

```python
import functools
import math

import numpy as np
import jax
import jax.numpy as jnp
from jax import lax
from jax.experimental import pallas as pl
from jax.experimental.pallas import tpu as pltpu

F32 = jnp.float32
BF16 = jnp.bfloat16
I32 = jnp.int32

CHUNK = 64
N_META = 16
PAD_FRONT = 128 - N_META
NORM_EPS = 1e-6
NEG_INF = -1e30
A_HEADS, A_DIM = 8, 64
IDX_HEADS, IDX_DIM = 8, 64
IDX_SCALE = (IDX_DIM ** -0.5) * (IDX_HEADS ** -0.5)
TOPK_MAX = 256
B_HEADS, B_DK, B_DV = 4, 64, 128
GLA_GATE_RANK = 16
GLA_TAU = 16.0
C_HEADS, C_DK, C_DV = 4, 64, 128
ROPE_BASE = 10000.0
D_HEADS, D_DIM = 8, 64
REL_BUCKETS = 32
REL_MAX_DIST = 128
SPLIT_AB = (512, 512, 512, 512, 512, 64, 8, 256, 256, 512, 512, 16)

LANES = 128
W_AB_USED = 4480
W_AB_PAD = 4608
I32_MIN = -2 ** 31
DUMMY = 7
IDX_BITS = 14
SETTLE_FROM_BIT = 22
LOG_F32_UNDERFLOW = -104.0
LOGIT_LIMIT = 40.0
NORM_SLACK = 1.05
TINY = 1e-30
BF16_BITS = -65536
CHUNK_GROUP = 5
SB_FIRST = 3
VMEM_LIMIT = 56 * 1024 * 1024

NT_DIMS = (((1,), (1,)), ((), ()))
TN_DIMS = (((0,), (0,)), ((), ()))


def _pick_tile(n, candidates):
    for c in candidates:
        if n % c == 0:
            return c
    raise ValueError(f"no tile for {n}")


def _params(sem):
    return pltpu.CompilerParams(dimension_semantics=sem, vmem_limit_bytes=VMEM_LIMIT)


def _silu(x):
    return x / (1.0 + jnp.exp(-x))


def _softplus_parts(z):
    t = jnp.log(1.0 + jnp.exp(-jnp.abs(z)))
    return jnp.maximum(z, 0.0) + t, jnp.minimum(z, 0.0) - t


def _split3(x):
    a = x.astype(BF16)
    r = x - a.astype(F32)
    b = r.astype(BF16)
    c = (r - b.astype(F32)).astype(BF16)
    return a, b, c


def _inproj_kernel(x_ref, g_ref, w_ref, o_ref, hn_ref):
    @pl.when(pl.program_id(2) == 0)
    def _():
        x = x_ref[...]
        ms = jnp.mean(x * x, axis=-1, keepdims=True)
        hn_ref[...] = (x * lax.rsqrt(ms + NORM_EPS) * g_ref[...]).astype(BF16)

    o_ref[...] = jnp.dot(hn_ref[...], w_ref[...], preferred_element_type=F32).astype(o_ref.dtype)


def _inproj(h, g, w):
    bsz, p, d = h.shape
    n = w.shape[1]
    tp = _pick_tile(p, (1664, 1280, 640, 128))
    tn = _pick_tile(n, (1792, 1536, 896, 512, 128))
    return pl.pallas_call(
        _inproj_kernel,
        grid=(bsz, p // tp, n // tn),
        in_specs=[
            pl.BlockSpec((None, tp, d), lambda b, i, j: (b, i, 0)),
            pl.BlockSpec((1, d), lambda b, i, j: (0, 0)),
            pl.BlockSpec((d, tn), lambda b, i, j: (0, j)),
        ],
        out_specs=pl.BlockSpec((None, tp, tn), lambda b, i, j: (b, i, j)),
        out_shape=jax.ShapeDtypeStruct((bsz, p, n), BF16),
        scratch_shapes=[pltpu.VMEM((tp, d), BF16)],
        compiler_params=_params(("parallel", "parallel", "arbitrary")),
        name="inproj",
    )(h, g.reshape(1, d), w)


def _outproj_kernel(m1_ref, m2_ref, w1_ref, w2_ref, h_ref, o_ref, *, tp):
    y = jnp.dot(m1_ref[...], w1_ref[...], preferred_element_type=F32)
    y = y + jnp.dot(m2_ref[...], w2_ref[...], preferred_element_type=F32)
    pos = pl.program_id(1) * tp + lax.broadcasted_iota(I32, y.shape, 0)
    o_ref[...] = h_ref[...] + jnp.where(pos >= PAD_FRONT, y, 0.0)


def _outproj(m1, m2, w_out, h):
    bsz, p, d = h.shape
    k1 = m1.shape[-1]
    k2 = m2.shape[-1]
    tp = _pick_tile(p, (832, 640, 128))
    w = w_out.astype(BF16)
    return pl.pallas_call(
        functools.partial(_outproj_kernel, tp=tp),
        grid=(bsz, p // tp),
        in_specs=[
            pl.BlockSpec((None, tp, k1), lambda b, i: (b, i, 0)),
            pl.BlockSpec((None, tp, k2), lambda b, i: (b, i, 0)),
            pl.BlockSpec((k1, d), lambda b, i: (0, 0)),
            pl.BlockSpec((k2, d), lambda b, i: (0, 0)),
            pl.BlockSpec((None, tp, d), lambda b, i: (b, i, 0)),
        ],
        out_specs=pl.BlockSpec((None, tp, d), lambda b, i: (b, i, 0)),
        out_shape=jax.ShapeDtypeStruct((bsz, p, d), F32),
        compiler_params=_params(("parallel", "parallel")),
        name="outproj",
    )(m1, m2, w[:k1], w[k1:], h)


def _outproj_norm_kernel(m1_ref, m2_ref, w1_ref, w2_ref, h_ref, g_ref, o_ref):
    y = jnp.dot(m1_ref[...], w1_ref[...], preferred_element_type=F32)
    y = y + jnp.dot(m2_ref[...], w2_ref[...], preferred_element_type=F32)
    x = h_ref[...] + y
    ms = jnp.mean(x * x, axis=-1, keepdims=True)
    o_ref[...] = x * lax.rsqrt(ms + NORM_EPS) * g_ref[...]


def _outproj_norm(m1, m2, w_out, h, g, seq):
    bsz, p, d = h.shape
    k1 = m1.shape[-1]
    k2 = m2.shape[-1]
    tp = LANES
    skip = (p - seq) // tp
    assert skip * tp == p - seq
    w = w_out.astype(BF16)
    rows = lambda b, i: (b, i + skip, 0)
    return pl.pallas_call(
        _outproj_norm_kernel,
        grid=(bsz, seq // tp),
        in_specs=[
            pl.BlockSpec((None, tp, k1), rows),
            pl.BlockSpec((None, tp, k2), rows),
            pl.BlockSpec((k1, d), lambda b, i: (0, 0)),
            pl.BlockSpec((k2, d), lambda b, i: (0, 0)),
            pl.BlockSpec((None, tp, d), rows),
            pl.BlockSpec((1, d), lambda b, i: (0, 0)),
        ],
        out_specs=pl.BlockSpec((None, tp, d), lambda b, i: (b, i, 0)),
        out_shape=jax.ShapeDtypeStruct((bsz, seq, d), F32),
        compiler_params=_params(("parallel", "parallel")),
        name="outproj_norm",
    )(m1, m2, w[:k1], w[k1:], h, g.reshape(1, d))


def _dsa_kernel(aq_ref, ag_ref, iq_ref, iw_ref, k_ref, v_ref, ik_ref, bias_ref, o_ref,
                key_ref, sa_ref, sb_ref, iqs_ref, wts_ref, qm_ref,
                jdx_ref, kmax_ref, mx_ref, ls_ref, acc_ref, *, topk):
    T = LANES
    i = pl.program_id(1)
    last = i + DUMMY

    def num_groups(group):
        return (i + group) // group

    def first_slot(group):
        return last + 1 - group * num_groups(group)

    row = lax.broadcasted_iota(I32, (T, T), 0)
    col = lax.broadcasted_iota(I32, (T, T), 1)
    qchunk = (i * T + row) >> 6
    low = col < 64

    def admissible(kt):
        kpos = kt * T + col
        return (kpos >= PAD_FRONT) & ((kpos >> 6) <= qchunk)

    def tile_rows(slot):
        return pl.ds(pl.multiple_of(jnp.maximum(slot - DUMMY, 0) * T, T), T)

    def for_groups(group, body):
        first = first_slot(group)

        def step(g, carry):
            body(first + group * g)
            return carry

        lax.fori_loop(0, num_groups(group), step, 0)

    iw = iw_ref[...].astype(F32)
    for h in range(IDX_HEADS):
        pair = iq_ref[:, (h // 2) * T:(h // 2 + 1) * T]
        mine = low if h % 2 == 0 else ~low
        iqs_ref[0, h * T:(h + 1) * T, :] = jnp.where(mine, pair, jnp.zeros_like(pair))
        wts_ref[h] = jnp.broadcast_to(iw[:, h:h + 1], (T, T))
    for h in range(A_HEADS):
        pair = aq_ref[:, (h // 2) * T:(h // 2 + 1) * T]
        mine = low if h % 2 == 0 else ~low
        qm_ref[h // 2, (h % 2) * T:(h % 2 + 1) * T, :] = jnp.where(
            mine, pair * (A_DIM ** -0.5), jnp.zeros_like(pair))

    def sweep(lhs_ref, rhs_ref, consume, consume_last):
        group_rows = lhs_ref.shape[1]

        def issue(s0, stage):
            rows = [tile_rows(s0), tile_rows(s0 + 1)]
            for g in range(lhs_ref.shape[0]):
                ps = slice(g * T, (g + 1) * T)
                kw = jnp.concatenate([rhs_ref[rows[0], ps], rhs_ref[rows[1], ps]], axis=0)
                stage[g * group_rows:(g + 1) * group_rows, :] = lax.dot_general(
                    lhs_ref[g], kw, NT_DIMS, preferred_element_type=F32)

        first = first_slot(4)
        issue(first, sa_ref)

        def body(q, carry):
            s = first + 4 * q
            issue(s + 2, sb_ref)
            consume(s, sa_ref)
            issue(s + 4, sa_ref)
            consume(s + 2, sb_ref)
            return carry

        lax.fori_loop(0, num_groups(4) - 1, body, 0)
        issue(last - 1, sb_ref)
        consume(last - 3, sa_ref)
        consume_last(last - 1, sb_ref)

    def score_pair(s0, stage):
        for u in range(2):
            tot = jnp.zeros((T, T), F32)
            for h in range(IDX_HEADS):
                tot = tot + jnp.maximum(stage[h * T:(h + 1) * T, u * T:(u + 1) * T], 0.0) * wts_ref[h]
            score = jnp.where(admissible(s0 + u - DUMMY), tot * IDX_SCALE, NEG_INF)
            bits = lax.bitcast_convert_type(score, I32)
            key_ref[s0 + u] = bits ^ ((bits >> 31) & 0x7FFFFFFF)

    sweep(iqs_ref, ik_ref, score_pair, score_pair)
    for d in range(DUMMY):
        key_ref[d] = jnp.full((T, T), I32_MIN, I32)

    def row_count(pred):
        one = jnp.ones((T, T), I32)
        zero = jnp.zeros((T, T), I32)
        first = first_slot(8)

        def body(g, acc):
            hits = [jnp.where(pred(first + 8 * g + u), one, zero) for u in range(8)]
            while len(hits) > 1:
                hits = [a + b for a, b in zip(hits[::2], hits[1::2])]
            return acc + hits[0]

        acc = lax.fori_loop(0, num_groups(8), body, zero)
        return jnp.sum(acc.astype(F32), axis=-1, keepdims=True)

    def value_bit(it, state):
        t, reach = state
        cand = t + (jnp.int32(1) << (31 - it))
        wide = jnp.broadcast_to(cand, (T, T))
        cnt = row_count(lambda s: key_ref[s] >= wide)
        fits = cnt >= topk
        return jnp.where(fits, cand, t), jnp.where(fits, cnt, reach)

    everything = (8 * T * num_groups(8)).astype(F32)
    state = lax.fori_loop(0, SETTLE_FROM_BIT, value_bit, (jnp.full((T, 1), I32_MIN, I32),
                                                          jnp.full((T, 1), everything, F32)))

    wide = jnp.broadcast_to(state[0], (T, T))
    final = row_count(lambda s: key_ref[s] > wide) < topk

    def unsettled(carry):
        it, (_, reach) = carry
        return (it < 32) & (jnp.max(jnp.where(final, topk, reach)) > topk)

    def two_bits(carry):
        it, state = carry
        return it + 2, value_bit(it + 1, value_bit(it, state))

    _, (thr, reach) = lax.while_loop(unsettled, two_bits, (jnp.int32(SETTLE_FROM_BIT), state))
    thr = jnp.broadcast_to(thr, (T, T))
    jdx_ref[...] = jnp.full((T, T), 2 ** IDX_BITS, I32)

    @pl.when(jnp.max(reach) > topk)
    def _():
        need = topk - (reach - row_count(lambda s: key_ref[s] == thr))
        p_r = lax.broadcasted_iota(I32, (T, 2 * T), 0)
        p_c = lax.broadcasted_iota(I32, (T, 2 * T), 1)
        prefix_ones = jnp.where((p_c >= T) | (p_r <= p_c), 1.0, 0.0).astype(BF16)
        first = first_slot(8)

        def body(g, state):
            seen, best = state
            slots = [first + 8 * g + u for u in range(8)]
            tied = [key_ref[s] == thr for s in slots]
            sums = [jnp.dot(jnp.where(t, 1.0, 0.0).astype(BF16), prefix_ones,
                            preferred_element_type=F32) for t in tied]
            for s, t, sm in zip(slots, tied, sums):
                keep = t & (seen + sm[:, :T] <= need)
                best = jnp.maximum(best, jnp.where(keep, (s - DUMMY) * T + col, -1))
                seen = seen + sm[:, T:]
            return seen, best

        _, best = lax.fori_loop(0, num_groups(8), body,
                                (jnp.zeros((T, T), F32), jnp.full((T, T), -1, I32)))
        jdx_ref[...] = jnp.broadcast_to(jnp.max(best, axis=-1, keepdims=True), (T, T))

    jdx = jdx_ref[...]

    def selection_mask(s):
        key = key_ref[s]
        sel = (key > thr) | ((key == thr) & ((s - DUMMY) * T + col <= jdx))
        return jnp.where(sel & admissible(s - DUMMY), 0.0, NEG_INF)

    head_of_lane = (lax.broadcasted_iota(I32, (A_HEADS * A_DIM, T), 0) // A_DIM
                    == lax.broadcasted_iota(I32, (A_HEADS * A_DIM, T), 1))
    head_sum = jnp.where(head_of_lane, 1.0, 0.0).astype(BF16)

    def head_norms2(t):
        t = t.astype(F32)
        return jnp.dot((t * t).astype(BF16), head_sum, preferred_element_type=F32)

    @pl.when(i == 0)
    def _():
        def widest(t, best):
            return jnp.maximum(best, head_norms2(k_ref[pl.ds(pl.multiple_of(t * T, T), T), :]))
        best = lax.fori_loop(0, k_ref.shape[0] // T, widest, jnp.zeros((T, T), F32))
        kmax_ref[...] = jnp.broadcast_to(jnp.max(best, axis=0, keepdims=True), kmax_ref.shape)

    bound2 = head_norms2(aq_ref[...]) * kmax_ref[0:1, :] * (NORM_SLACK / A_DIM)
    room = LOGIT_LIMIT - jnp.max(jnp.abs(bias_ref[...]))
    small_logits = (room > 0.0) & (jnp.max(bound2) <= room * room)

    ls_ref[...] = jnp.zeros(ls_ref.shape, F32)
    acc_ref[...] = jnp.zeros(acc_ref.shape, F32)

    def attend(s0, stage, near, final):
        madd = [selection_mask(s0), selection_mask(s0 + 1)]
        for hp in range(A_HEADS // 2):
            weights = []
            for h in (2 * hp, 2 * hp + 1):
                parts = [stage[h * T:(h + 1) * T, u * T:(u + 1) * T] + madd[u] for u in range(2)]
                if near:
                    parts = [parts[u] + bias_ref[h, 1 - u] for u in range(2)]
                if not final:
                    mx_ref[h] = jnp.maximum(mx_ref[h], jnp.maximum(parts[0], parts[1]))
                else:
                    es = [jnp.exp(part - mx_ref[h]) for part in parts]
                    ls_ref[h] = ls_ref[h] + (es[0] + es[1])
                    weights.append(jnp.concatenate(es, axis=1).astype(BF16))
            if final:
                ps = slice(hp * T, (hp + 1) * T)
                vw = jnp.concatenate([v_ref[tile_rows(s0), ps], v_ref[tile_rows(s0 + 1), ps]], axis=0)
                acc_ref[hp] = acc_ref[hp] + jnp.dot(jnp.concatenate(weights, axis=0), vw,
                                                    preferred_element_type=F32)

    def attention_sweep(final):
        sweep(qm_ref, k_ref,
              functools.partial(attend, near=False, final=final),
              functools.partial(attend, near=True, final=final))

    @pl.when(small_logits)
    def _():
        mx_ref[...] = jnp.zeros(mx_ref.shape, F32)

    @pl.when(jnp.logical_not(small_logits))
    def _():
        mx_ref[...] = jnp.full(mx_ref.shape, NEG_INF, F32)
        attention_sweep(False)
        for h in range(A_HEADS):
            mx_ref[h] = jnp.broadcast_to(jnp.max(mx_ref[h], axis=-1, keepdims=True), (T, T))

    attention_sweep(True)

    def row_total(h):
        return jnp.maximum(jnp.sum(ls_ref[h], axis=-1, keepdims=True), TINY)

    for hp in range(A_HEADS // 2):
        o0 = acc_ref[hp, :T, :] / row_total(2 * hp)
        o1 = acc_ref[hp, T:, :] / row_total(2 * hp + 1)
        g = ag_ref[:, hp * T:(hp + 1) * T].astype(F32)
        o_ref[:, hp * T:(hp + 1) * T] = (jnp.where(low, o0, o1) * _silu(g)).astype(o_ref.dtype)


def _dsa(pab, bias_tiles, topk):
    bsz, p, _ = pab.shape
    T = LANES
    nq = p // T
    hw = A_HEADS * A_DIM
    one = pl.Buffered(1)
    return pl.pallas_call(
        functools.partial(_dsa_kernel, topk=float(topk)),
        grid=(bsz, nq),
        in_specs=[
            pl.BlockSpec((None, T, hw), lambda b, i: (b, i, 0)),
            pl.BlockSpec((None, T, hw), lambda b, i: (b, i, 3)),
            pl.BlockSpec((None, T, hw), lambda b, i: (b, i, 4)),
            pl.BlockSpec((None, T, T), lambda b, i: (b, i, 33)),
            pl.BlockSpec((None, p, hw), lambda b, i: (b, 0, 1), pipeline_mode=one),
            pl.BlockSpec((None, p, hw), lambda b, i: (b, 0, 2), pipeline_mode=one),
            pl.BlockSpec((None, p, T), lambda b, i: (b, 0, 32), pipeline_mode=one),
            pl.BlockSpec((A_HEADS, 2, T, T), lambda b, i: (0, 0, 0, 0), pipeline_mode=one),
        ],
        out_specs=pl.BlockSpec((None, T, hw), lambda b, i: (b, i, 0)),
        out_shape=jax.ShapeDtypeStruct((bsz, p, hw), BF16),
        scratch_shapes=[
            pltpu.VMEM((nq + DUMMY, T, T), I32),
            pltpu.VMEM((A_HEADS * T, 2 * T), F32),
            pltpu.VMEM((A_HEADS * T, 2 * T), F32),
            pltpu.VMEM((1, IDX_HEADS * T, T), BF16),
            pltpu.VMEM((IDX_HEADS, T, T), F32),
            pltpu.VMEM((A_HEADS // 2, 2 * T, T), BF16),
            pltpu.VMEM((T, T), I32),
            pltpu.VMEM((8, T), F32),
            pltpu.VMEM((A_HEADS, T, T), F32),
            pltpu.VMEM((A_HEADS, T, T), F32),
            pltpu.VMEM((A_HEADS // 2, 2 * T, T), F32),
        ],
        compiler_params=_params(("parallel", "arbitrary")),
        name="dsa",
    )(pab, pab, pab, pab, pab, pab, pab, bias_tiles)


def _gla_kernel(q_ref, k_ref, v_ref, g_ref, a_ref, w2_ref, gb_ref, o_ref, st_ref, *, nchunk):
    C = CHUNK
    T = LANES

    @pl.when(pl.program_id(1) == 0)
    def _():
        st_ref[...] = jnp.zeros(st_ref.shape, F32)

    r_i = lax.broadcasted_iota(I32, (C, C), 0)
    c_i = lax.broadcasted_iota(I32, (C, C), 1)
    causal = c_i <= r_i
    tri = jnp.where(causal, 1.0, 0.0).astype(BF16)
    lane = lax.broadcasted_iota(I32, (C, T), 1)
    sd_r = lax.broadcasted_iota(I32, (2 * B_DV, T), 0)
    sd_c = lax.broadcasted_iota(I32, (2 * B_DV, T), 1)
    blockdiag = (sd_r >= B_DV) == (sd_c >= B_DK)

    pairs = range(B_HEADS // 2)
    lanes = [slice(hp * T, (hp + 1) * T) for hp in pairs]
    G = math.gcd(nchunk, CHUNK_GROUP)

    def group(gi, carry):
        rows = [pl.ds(pl.multiple_of((gi * G + c) * C, C), C) for c in range(G)]
        log_a = [_softplus_parts(jnp.dot(a_ref[r, :], w2_ref[...], preferred_element_type=F32)
                                 + gb_ref[...])[1] * (1.0 / GLA_TAU) for r in rows]
        terms = [_split3(x) for x in log_a]
        bcum = [sum(jnp.dot(tri, t, preferred_element_type=F32) for t in ts) for ts in terms]
        q_t, k_t, k_d, decay = [], [], [], []
        for r, b in zip(rows, bcum):
            b_last = b[C - 1:C, :]
            q = q_ref[r, :].astype(F32) * (B_DK ** -0.5)
            k = k_ref[r, :].astype(F32)
            q_t.append((q * jnp.exp(b)).astype(BF16))
            k_t.append((k * jnp.exp(-b)).astype(BF16))
            k_d.append((k * jnp.exp(b_last - b)).astype(BF16))
            decay.append(jnp.exp(b_last))
        att = {}
        for c in range(G):
            for h in range(B_HEADS):
                ls = lanes[h // 2]
                mine = (lane < B_DK) if h % 2 == 0 else (lane >= B_DK)
                qm = jnp.where(mine, q_t[c][:, ls], jnp.zeros((C, T), BF16))
                s = lax.dot_general(qm, k_t[c][:, ls], NT_DIMS, preferred_element_type=F32)
                att[c, h] = jnp.where(causal, s, 0.0).astype(BF16)
        contrib = {}
        for c in range(G):
            for hp in pairs:
                vp = v_ref[rows[c], hp * 2 * B_DV:(hp + 1) * 2 * B_DV]
                raw = lax.dot_general(vp, k_d[c][:, lanes[hp]], TN_DIMS, preferred_element_type=F32)
                contrib[c, hp] = jnp.where(blockdiag, raw, 0.0)
        before = {}
        for hp in pairs:
            st = st_ref[hp]
            for c in range(G):
                before[c, hp] = st
                st = st * decay[c][:, lanes[hp]] + contrib[c, hp]
            st_ref[hp] = st
        for c in range(G):
            for hp in pairs:
                o_inter = lax.dot_general(q_t[c][:, lanes[hp]], before[c, hp].astype(BF16), NT_DIMS,
                                          preferred_element_type=F32)
                for hh in range(2):
                    h = 2 * hp + hh
                    vs = slice(h * B_DV, (h + 1) * B_DV)
                    o = jnp.dot(att[c, h], v_ref[rows[c], vs], preferred_element_type=F32)
                    o = o + o_inter[:, hh * B_DV:(hh + 1) * B_DV]
                    o = o * lax.rsqrt(jnp.mean(o * o, axis=-1, keepdims=True) + NORM_EPS)
                    o = o * _silu(g_ref[rows[c], vs].astype(F32))
                    o_ref[rows[c], vs] = o.astype(o_ref.dtype)
        return carry

    lax.fori_loop(0, nchunk // G, group, 0)


def _gla(pab, w2, gb):
    bsz, p, _ = pab.shape
    tc = _pick_tile(p, (640, 128))
    hv = B_HEADS * B_DV
    hk = B_HEADS * B_DK
    return pl.pallas_call(
        functools.partial(_gla_kernel, nchunk=tc // CHUNK),
        grid=(bsz, p // tc),
        in_specs=[
            pl.BlockSpec((None, tc, hk), lambda b, c: (b, c, 14)),
            pl.BlockSpec((None, tc, hk), lambda b, c: (b, c, 15)),
            pl.BlockSpec((None, tc, hv), lambda b, c: (b, c, 5)),
            pl.BlockSpec((None, tc, hv), lambda b, c: (b, c, 6)),
            pl.BlockSpec((None, tc, LANES), lambda b, c: (b, c, 34)),
            pl.BlockSpec((LANES, hk), lambda b, c: (0, 0)),
            pl.BlockSpec((1, hk), lambda b, c: (0, 0)),
        ],
        out_specs=pl.BlockSpec((None, tc, hv), lambda b, c: (b, c, 0)),
        out_shape=jax.ShapeDtypeStruct((bsz, p, hv), BF16),
        scratch_shapes=[pltpu.VMEM((B_HEADS // 2, 2 * B_DV, 2 * B_DK), F32)],
        compiler_params=_params(("parallel", "arbitrary")),
        name="gla",
    )(pab, pab, pab, pab, pab, w2, gb)


def _ret_kernel(q_ref, k_ref, v_ref, g_ref, cos_ref, sin_ref, dmat_ref, zeta_ref, xi_ref,
                cdec_ref, o_ref, st_ref, *, nchunk):
    C = CHUNK
    T = LANES
    W = C_HEADS * C_DK

    @pl.when(pl.program_id(1) == 0)
    def _():
        st_ref[...] = jnp.zeros(st_ref.shape, F32)

    lane_w = lax.broadcasted_iota(I32, (C, W), 1)
    first_half = (lane_w & (C_DK - 1)) < (C_DK // 2)
    lane = lax.broadcasted_iota(I32, (C, T), 1)
    sd_r = lax.broadcasted_iota(I32, (2 * C_DV, T), 0)
    sd_c = lax.broadcasted_iota(I32, (2 * C_DV, T), 1)
    blockdiag = (sd_r >= C_DV) == (sd_c >= C_DK)

    def rotate(x, cos, sin_signed):
        swapped = jnp.where(first_half, pltpu.roll(x, W - C_DK // 2, 1), pltpu.roll(x, C_DK // 2, 1))
        return x * cos + swapped * sin_signed

    pairs = range(C_HEADS // 2)
    lanes = [slice(hp * T, (hp + 1) * T) for hp in pairs]
    G = math.gcd(nchunk, CHUNK_GROUP)

    def group(gi, carry):
        rows = [pl.ds(pl.multiple_of((gi * G + c) * C, C), C) for c in range(G)]
        q_b, k_b, q_x, k_z = [], [], [], []
        for r in rows:
            cos = cos_ref[r, :]
            sin = sin_ref[r, :]
            q = rotate(q_ref[r, :].astype(F32), cos, sin)
            k = rotate(k_ref[r, :].astype(F32), cos, sin) * (C_DK ** -0.5)
            q_b.append(q.astype(BF16))
            k_b.append(k.astype(BF16))
            q_x.append((q * xi_ref[...]).astype(BF16))
            k_z.append((k * zeta_ref[...]).astype(BF16))
        att = {}
        for c in range(G):
            for h in range(C_HEADS):
                ls = lanes[h // 2]
                mine = (lane < C_DK) if h % 2 == 0 else (lane >= C_DK)
                qm = jnp.where(mine, q_b[c][:, ls], jnp.zeros((C, T), BF16))
                s = lax.dot_general(qm, k_b[c][:, ls], NT_DIMS, preferred_element_type=F32)
                att[c, h] = (s * dmat_ref[h]).astype(BF16)
        contrib = {}
        for c in range(G):
            for hp in pairs:
                vp = v_ref[rows[c], hp * 2 * C_DV:(hp + 1) * 2 * C_DV]
                raw = lax.dot_general(vp, k_z[c][:, lanes[hp]], TN_DIMS, preferred_element_type=F32)
                contrib[c, hp] = jnp.where(blockdiag, raw, 0.0)
        before = {}
        for hp in pairs:
            st = st_ref[hp]
            for c in range(G):
                before[c, hp] = st
                st = st * cdec_ref[:, lanes[hp]] + contrib[c, hp]
            st_ref[hp] = st
        for c in range(G):
            for hp in pairs:
                o_inter = lax.dot_general(q_x[c][:, lanes[hp]], before[c, hp].astype(BF16), NT_DIMS,
                                          preferred_element_type=F32)
                for hh in range(2):
                    h = 2 * hp + hh
                    vs = slice(h * C_DV, (h + 1) * C_DV)
                    o = jnp.dot(att[c, h], v_ref[rows[c], vs], preferred_element_type=F32)
                    o = o + o_inter[:, hh * C_DV:(hh + 1) * C_DV]
                    o = o - jnp.mean(o, axis=-1, keepdims=True)
                    o = o * lax.rsqrt(jnp.mean(o * o, axis=-1, keepdims=True) + NORM_EPS)
                    o = o * _silu(g_ref[rows[c], vs].astype(F32))
                    o_ref[rows[c], vs] = o.astype(o_ref.dtype)
        return carry

    lax.fori_loop(0, nchunk // G, group, 0)


def _retention_tables(p):
    log_gamma = np.log(1.0 - np.exp2(-5.0 - np.arange(C_HEADS, dtype=np.float64)))
    i = np.arange(CHUNK, dtype=np.float64)
    diff = i[:, None] - i[None, :]
    dmat = np.where(diff >= 0, np.exp(log_gamma[:, None, None] * np.maximum(diff, 0.0)), 0.0)
    zeta = np.exp(log_gamma[:, None] * (CHUNK - 1 - i))
    xi = np.exp(log_gamma[:, None] * (i + 1))
    cdec = np.exp(log_gamma * CHUNK)
    widen = lambda t: np.repeat(t.T[:, :, None], C_DK, axis=2).reshape(CHUNK, C_HEADS * C_DK)
    half = C_DK // 2
    inv = jnp.asarray(ROPE_BASE, F32) ** (-jnp.arange(half, dtype=F32) / half)
    ang = jnp.arange(p, dtype=jnp.int32).astype(F32)[:, None] * inv[None, :]
    cos = jnp.tile(jnp.cos(ang), (1, 2 * C_HEADS))
    sin = jnp.sin(ang)
    sin_signed = jnp.tile(jnp.concatenate([-sin, sin], axis=1), (1, C_HEADS))
    return (cos, sin_signed, jnp.asarray(dmat, F32), jnp.asarray(widen(zeta), F32),
            jnp.asarray(widen(xi), F32),
            jnp.asarray(np.repeat(cdec, C_DK)[None, :], F32))


def _retention(pcd, tables):
    bsz, p, _ = pcd.shape
    cos, sin, dmat, zeta, xi, cdec = tables
    tc = _pick_tile(p, (640, 128))
    hv = C_HEADS * C_DV
    hk = C_HEADS * C_DK
    full = lambda shape: pl.BlockSpec(shape, lambda b, c: (0,) * len(shape))
    return pl.pallas_call(
        functools.partial(_ret_kernel, nchunk=tc // CHUNK),
        grid=(bsz, p // tc),
        in_specs=[
            pl.BlockSpec((None, tc, hk), lambda b, c: (b, c, 0)),
            pl.BlockSpec((None, tc, hk), lambda b, c: (b, c, 1)),
            pl.BlockSpec((None, tc, hv), lambda b, c: (b, c, 1)),
            pl.BlockSpec((None, tc, hv), lambda b, c: (b, c, 2)),
            pl.BlockSpec((tc, hk), lambda b, c: (c, 0)),
            pl.BlockSpec((tc, hk), lambda b, c: (c, 0)),
            full((C_HEADS, CHUNK, CHUNK)),
            full((CHUNK, hk)),
            full((CHUNK, hk)),
            full((1, hk)),
        ],
        out_specs=pl.BlockSpec((None, tc, hv), lambda b, c: (b, c, 0)),
        out_shape=jax.ShapeDtypeStruct((bsz, p, hv), BF16),
        scratch_shapes=[pltpu.VMEM((C_HEADS // 2, 2 * C_DV, 2 * C_DK), F32)],
        compiler_params=_params(("parallel", "arbitrary")),
        name="retention",
    )(pcd, pcd, pcd, pcd, cos, sin, dmat, zeta, xi, cdec)


def _sb_kernel(q_ref, g_ref, k_ref, v_ref, o_ref, qm_ref, run_ref, acc_ref, hi_ref, lo_ref, ls_ref,
               w_ref):
    T = LANES
    i = pl.program_id(1)
    row = lax.broadcasted_iota(I32, (T, T), 0)
    col = lax.broadcasted_iota(I32, (T, T), 1)
    qpos = i * T + row
    low = col < D_DIM
    u_r = lax.broadcasted_iota(I32, (T, 2 * T), 0)
    u_c = lax.broadcasted_iota(I32, (T, 2 * T), 1)
    suffix = jnp.where((u_c >= T) | (u_r > u_c), 1.0, 0.0).astype(BF16)

    for h in range(D_HEADS):
        pair = q_ref[:, (h // 2) * T:(h // 2 + 1) * T]
        mine = low if h % 2 == 0 else ~low
        qm_ref[h] = jnp.where(mine, pair * (D_DIM ** -0.5), jnp.zeros_like(pair))
    run_ref[...] = jnp.zeros(run_ref.shape, F32)
    acc_ref[...] = jnp.zeros(acc_ref.shape, F32)

    def walk(t0, count):
        kts = [i - t0 - j for j in range(count)]
        rows = [pl.ds(pl.multiple_of(jnp.maximum(kt, 0) * T, T), T) for kt in kts]
        for j, kt in enumerate(kts):
            kpos = kt * T + col
            ok = (kpos < qpos) & (kpos >= PAD_FRONT)
            for h in range(D_HEADS):
                ps = slice((h // 2) * T, (h // 2 + 1) * T)
                z = lax.dot_general(qm_ref[h], k_ref[rows[j], ps], NT_DIMS, preferred_element_type=F32)
                sp, logsig = _softplus_parts(z)
                log_1m = jnp.where(ok, -sp, 0.0)
                hi = lax.bitcast_convert_type(lax.bitcast_convert_type(log_1m, I32) & BF16_BITS, F32)
                hi_ref[j, h] = hi.astype(BF16)
                lo_ref[j, h] = (log_1m - hi).astype(BF16)
                ls_ref[j, h] = jnp.where(ok, logsig, NEG_INF)
        sums = {}
        for j in range(count):
            for h in range(D_HEADS):
                sums[j, h] = (jnp.dot(hi_ref[j, h], suffix, preferred_element_type=F32)
                              + jnp.dot(lo_ref[j, h], suffix, preferred_element_type=F32))
        slowest = None
        for h in range(D_HEADS):
            run = run_ref[h]
            for j in range(count):
                w_ref[j, h] = jnp.exp(ls_ref[j, h] + run + sums[j, h][:, :T]).astype(BF16)
                run = run + sums[j, h][:, T:]
            run_ref[h] = run
            slowest = run if slowest is None else jnp.maximum(slowest, run)
        for h in range(D_HEADS):
            ps = slice((h // 2) * T, (h // 2 + 1) * T)
            acc = acc_ref[h]
            for j in range(count):
                acc = acc + jnp.dot(w_ref[j, h], v_ref[rows[j], ps], preferred_element_type=F32)
            acc_ref[h] = acc
        return (jnp.max(slowest) > LOG_F32_UNDERFLOW).astype(I32)

    def one_more(state):
        t, _ = state
        return t + 1, walk(t, 1)

    lax.while_loop(lambda s: (s[0] <= i) & (s[1] > 0), one_more, (jnp.int32(SB_FIRST), walk(0, SB_FIRST)))

    for hp in range(D_HEADS // 2):
        ps = slice(hp * T, (hp + 1) * T)
        o = jnp.where(low, acc_ref[2 * hp], acc_ref[2 * hp + 1]) * _silu(g_ref[:, ps].astype(F32))
        o_ref[:, ps] = o.astype(o_ref.dtype)


def _stick_breaking(pcd):
    bsz, p, _ = pcd.shape
    T = LANES
    hw = D_HEADS * D_DIM
    one = pl.Buffered(1)
    return pl.pallas_call(
        _sb_kernel,
        grid=(bsz, p // T),
        in_specs=[
            pl.BlockSpec((None, T, hw), lambda b, i: (b, i, 3)),
            pl.BlockSpec((None, T, hw), lambda b, i: (b, i, 6)),
            pl.BlockSpec((None, p, hw), lambda b, i: (b, 0, 4), pipeline_mode=one),
            pl.BlockSpec((None, p, hw), lambda b, i: (b, 0, 5), pipeline_mode=one),
        ],
        out_specs=pl.BlockSpec((None, T, hw), lambda b, i: (b, i, 0)),
        out_shape=jax.ShapeDtypeStruct((bsz, p, hw), BF16),
        scratch_shapes=[
            pltpu.VMEM((D_HEADS, T, T), BF16),
            pltpu.VMEM((D_HEADS, T, T), F32),
            pltpu.VMEM((D_HEADS, T, T), F32),
            pltpu.VMEM((SB_FIRST, D_HEADS, T, T), BF16),
            pltpu.VMEM((SB_FIRST, D_HEADS, T, T), BF16),
            pltpu.VMEM((SB_FIRST, D_HEADS, T, T), F32),
            pltpu.VMEM((SB_FIRST, D_HEADS, T, T), BF16),
        ],
        compiler_params=_params(("parallel", "arbitrary")),
        name="stick_breaking",
    )(pcd, pcd, pcd, pcd)


def _rel_bucket_np(rel):
    half = REL_BUCKETS // 2
    max_exact = half // 2
    n = -rel
    ret = np.where(n < 0, half, 0)
    n = np.abs(n)
    edges = [math.ceil(max_exact * (REL_MAX_DIST / max_exact) ** (j / (half - max_exact)) - 1e-9)
             for j in range(1, half - max_exact)]
    large = max_exact + sum((n >= e).astype(np.int64) for e in edges)
    return ret + np.where(n < max_exact, n, large)


def _bias_tiles(rel_bias):
    i = np.arange(LANES)[:, None]
    j = np.arange(LANES)[None, :]
    idx = np.stack([_rel_bucket_np(j - i - LANES * d) for d in range(3)])
    assert (idx[2] == idx[2, 0, 0]).all()
    table = rel_bias.astype(F32) - rel_bias.astype(F32)[idx[2, 0, 0]]
    near = jnp.asarray(idx[:2], I32)[None]
    tiles = jnp.zeros((table.shape[1],) + near.shape[1:], F32)
    for bucket in range(REL_BUCKETS):
        tiles = jnp.where(near == bucket, table[bucket][:, None, None, None], tiles)
    return tiles


def _layout_ab(w):
    aq, ak, av, ag, iq, ik, iw, bq, bk, bv, bg, ba = jnp.split(w, np.cumsum(SPLIT_AB)[:-1].tolist(), axis=1)
    pad = lambda t: jnp.pad(t, ((0, 0), (0, LANES - t.shape[1])))
    cols = [aq, ak, av, ag, iq, bv, bg, bq, bk, jnp.concatenate([ik, ik], axis=1), pad(iw), pad(ba),
            jnp.zeros((w.shape[0], W_AB_PAD - W_AB_USED), w.dtype)]
    return jnp.concatenate(cols, axis=1).astype(BF16)


def kernel(x, meta_tokens, rel_bias, norm_g, final_g, w_in_ab, gla_gate_w2, gla_gate_b, w_out_ab,
           w_in_cd, w_out_cd):
    bsz, seq, d = x.shape
    p = seq + PAD_FRONT + N_META
    depth = norm_g.shape[0]
    topk = min(TOPK_MAX, seq // 4)
    h = jnp.concatenate([jnp.zeros((bsz, PAD_FRONT, d), x.dtype),
                         jnp.broadcast_to(meta_tokens.astype(x.dtype)[None], (bsz, N_META, d)),
                         x], axis=1)
    bias_tiles = _bias_tiles(rel_bias)
    tables = _retention_tables(p)
    assert depth >= 1
    for layer in range(depth):
        j = layer // 2
        if layer % 2 == 0:
            pab = _inproj(h, norm_g[layer], _layout_ab(w_in_ab[j]))
            first = _dsa(pab, bias_tiles, topk)
            w2 = jnp.pad(gla_gate_w2[j], ((0, LANES - GLA_GATE_RANK), (0, 0))).astype(BF16)
            second = _gla(pab, w2, gla_gate_b[j].reshape(1, -1).astype(F32))
            w_out = w_out_ab[j]
        else:
            pcd = _inproj(h, norm_g[layer], w_in_cd[j].astype(BF16))
            first = _retention(pcd, tables)
            second = _stick_breaking(pcd)
            w_out = w_out_cd[j]
        if layer + 1 < depth:
            h = _outproj(first, second, w_out, h)
    return _outproj_norm(first, second, w_out, h, final_g, seq)
```

```python
import functools
import math

import numpy as np
import jax
import jax.numpy as jnp
from jax import lax
from jax.experimental import pallas as pl
from jax.experimental.pallas import tpu as pltpu

F32 = jnp.float32
BF16 = jnp.bfloat16
I32 = jnp.int32

CHUNK = 64
CHUNK_SHIFT = CHUNK.bit_length() - 1
N_META = 16
PAD_FRONT = 128 - N_META
NORM_EPS = 1e-6
NEG_INF = -1e30
A_HEADS, A_DIM = 8, 64
IDX_HEADS, IDX_DIM = 8, 64
IDX_SCALE = (IDX_DIM ** -0.5) * (IDX_HEADS ** -0.5)
TOPK_MAX = 256
B_HEADS, B_DK, B_DV = 4, 64, 128
GLA_GATE_RANK = 16
GLA_TAU = 16.0
C_HEADS, C_DK, C_DV = 4, 64, 128
ROPE_BASE = 10000.0
D_HEADS, D_DIM = 8, 64
REL_BUCKETS = 32
REL_MAX_DIST = 128
SPLIT_AB = (512, 512, 512, 512, 512, 64, 8, 256, 256, 512, 512, 16)

LANES = 128
W_AB_USED = 4480
W_AB_PAD = 4608
I32_MIN = -2 ** 31
DUMMY = 7
IDX_BITS = 14
SETTLE_FROM_BIT = 22
LOG_F32_UNDERFLOW = -104.0
LOGIT_LIMIT = 40.0
NORM_SLACK = 1.05
TINY = 1e-30
BF16_BITS = -65536
CHUNK_GROUP = 5
SB_FIRST = 3
VMEM_LIMIT = 56 * 1024 * 1024

NT_DIMS = (((1,), (1,)), ((), ()))
TN_DIMS = (((0,), (0,)), ((), ()))


def _pick_tile(n, candidates):
    for c in candidates:
        if n % c == 0:
            return c
    raise ValueError(f"no tile for {n}")


def _params(sem):
    return pltpu.CompilerParams(dimension_semantics=sem, vmem_limit_bytes=VMEM_LIMIT)


def _silu(x):
    return x / (1.0 + jnp.exp(-x))


def _softplus_parts(z):
    t = jnp.log(1.0 + jnp.exp(-jnp.abs(z)))
    return jnp.maximum(z, 0.0) + t, jnp.minimum(z, 0.0) - t


def _split3(x):
    a = x.astype(BF16)
    r = x - a.astype(F32)
    b = r.astype(BF16)
    c = (r - b.astype(F32)).astype(BF16)
    return a, b, c


def _inproj_kernel(x_ref, g_ref, w_ref, o_ref, hn_ref):
    @pl.when(pl.program_id(2) == 0)
    def _():
        x = x_ref[...]
        ms = jnp.mean(x * x, axis=-1, keepdims=True)
        hn_ref[...] = (x * lax.rsqrt(ms + NORM_EPS) * g_ref[...]).astype(BF16)

    o_ref[...] = jnp.dot(hn_ref[...], w_ref[...], preferred_element_type=F32).astype(o_ref.dtype)


def _inproj(h, g, w):
    bsz, p, d = h.shape
    n = w.shape[1]
    tp = _pick_tile(p, (1664, 1280, 640, 128))
    tn = _pick_tile(n, (1792, 1536, 896, 512, 128))
    return pl.pallas_call(
        _inproj_kernel,
        grid=(bsz, p // tp, n // tn),
        in_specs=[
            pl.BlockSpec((None, tp, d), lambda b, i, j: (b, i, 0)),
            pl.BlockSpec((1, d), lambda b, i, j: (0, 0)),
            pl.BlockSpec((d, tn), lambda b, i, j: (0, j)),
        ],
        out_specs=pl.BlockSpec((None, tp, tn), lambda b, i, j: (b, i, j)),
        out_shape=jax.ShapeDtypeStruct((bsz, p, n), BF16),
        scratch_shapes=[pltpu.VMEM((tp, d), BF16)],
        compiler_params=_params(("parallel", "parallel", "arbitrary")),
        name="inproj",
    )(h, g.reshape(1, d), w)


def _outproj_kernel(m1_ref, m2_ref, w1_ref, w2_ref, h_ref, o_ref, *, tp):
    y = jnp.dot(m1_ref[...], w1_ref[...], preferred_element_type=F32)
    y = y + jnp.dot(m2_ref[...], w2_ref[...], preferred_element_type=F32)
    pos = pl.program_id(1) * tp + lax.broadcasted_iota(I32, y.shape, 0)
    o_ref[...] = h_ref[...] + jnp.where(pos >= PAD_FRONT, y, 0.0)


def _outproj(m1, m2, w_out, h):
    bsz, p, d = h.shape
    k1 = m1.shape[-1]
    k2 = m2.shape[-1]
    tp = _pick_tile(p, (832, 640, 128))
    w = w_out.astype(BF16)
    return pl.pallas_call(
        functools.partial(_outproj_kernel, tp=tp),
        grid=(bsz, p // tp),
        in_specs=[
            pl.BlockSpec((None, tp, k1), lambda b, i: (b, i, 0)),
            pl.BlockSpec((None, tp, k2), lambda b, i: (b, i, 0)),
            pl.BlockSpec((k1, d), lambda b, i: (0, 0)),
            pl.BlockSpec((k2, d), lambda b, i: (0, 0)),
            pl.BlockSpec((None, tp, d), lambda b, i: (b, i, 0)),
        ],
        out_specs=pl.BlockSpec((None, tp, d), lambda b, i: (b, i, 0)),
        out_shape=jax.ShapeDtypeStruct((bsz, p, d), F32),
        compiler_params=_params(("parallel", "parallel")),
        name="outproj",
    )(m1, m2, w[:k1], w[k1:], h)


def _outproj_norm_kernel(m1_ref, m2_ref, w1_ref, w2_ref, h_ref, g_ref, o_ref):
    y = jnp.dot(m1_ref[...], w1_ref[...], preferred_element_type=F32)
    y = y + jnp.dot(m2_ref[...], w2_ref[...], preferred_element_type=F32)
    x = h_ref[...] + y
    ms = jnp.mean(x * x, axis=-1, keepdims=True)
    o_ref[...] = x * lax.rsqrt(ms + NORM_EPS) * g_ref[...]


def _outproj_norm(m1, m2, w_out, h, g, seq):
    bsz, p, d = h.shape
    k1 = m1.shape[-1]
    k2 = m2.shape[-1]
    tp = LANES
    skip = (p - seq) // tp
    assert skip * tp == p - seq
    w = w_out.astype(BF16)
    rows = lambda b, i: (b, i + skip, 0)
    return pl.pallas_call(
        _outproj_norm_kernel,
        grid=(bsz, seq // tp),
        in_specs=[
            pl.BlockSpec((None, tp, k1), rows),
            pl.BlockSpec((None, tp, k2), rows),
            pl.BlockSpec((k1, d), lambda b, i: (0, 0)),
            pl.BlockSpec((k2, d), lambda b, i: (0, 0)),
            pl.BlockSpec((None, tp, d), rows),
            pl.BlockSpec((1, d), lambda b, i: (0, 0)),
        ],
        out_specs=pl.BlockSpec((None, tp, d), lambda b, i: (b, i, 0)),
        out_shape=jax.ShapeDtypeStruct((bsz, seq, d), F32),
        compiler_params=_params(("parallel", "parallel")),
        name="outproj_norm",
    )(m1, m2, w[:k1], w[k1:], h, g.reshape(1, d))


def _dsa_kernel(aq_ref, ag_ref, iq_ref, iw_ref, k_ref, v_ref, ik_ref, bias_ref, o_ref,
                key_ref, sa_ref, sb_ref, iqs_ref, wts_ref, qm_ref,
                jdx_ref, kmax_ref, mx_ref, ls_ref, acc_ref, *, topk):
    T = LANES
    i = pl.program_id(1)
    last = i + DUMMY

    def num_groups(group):
        return (i + group) // group

    def first_slot(group):
        return last + 1 - group * num_groups(group)

    row = lax.broadcasted_iota(I32, (T, T), 0)
    col = lax.broadcasted_iota(I32, (T, T), 1)
    qchunk = (i * T + row) >> CHUNK_SHIFT
    low = col < A_DIM

    def admissible(kt):
        kpos = kt * T + col
        return (kpos >= PAD_FRONT) & ((kpos >> CHUNK_SHIFT) <= qchunk)

    def tile_rows(slot):
        return pl.ds(pl.multiple_of(jnp.maximum(slot - DUMMY, 0) * T, T), T)

    def for_groups(group, body):
        first = first_slot(group)

        def step(g, carry):
            body(first + group * g)
            return carry

        lax.fori_loop(0, num_groups(group), step, 0)

    iw = iw_ref[...].astype(F32)
    for h in range(IDX_HEADS):
        pair = iq_ref[:, (h // 2) * T:(h // 2 + 1) * T]
        mine = low if h % 2 == 0 else ~low
        iqs_ref[0, h * T:(h + 1) * T, :] = jnp.where(mine, pair, jnp.zeros_like(pair))
        wts_ref[h] = jnp.broadcast_to(iw[:, h:h + 1], (T, T))
    for h in range(A_HEADS):
        pair = aq_ref[:, (h // 2) * T:(h // 2 + 1) * T]
        mine = low if h % 2 == 0 else ~low
        qm_ref[h // 2, (h % 2) * T:(h % 2 + 1) * T, :] = jnp.where(
            mine, pair * (A_DIM ** -0.5), jnp.zeros_like(pair))

    def sweep(lhs_ref, rhs_ref, consume, consume_last):
        group_rows = lhs_ref.shape[1]

        def issue(s0, stage):
            rows = [tile_rows(s0), tile_rows(s0 + 1)]
            for g in range(lhs_ref.shape[0]):
                ps = slice(g * T, (g + 1) * T)
                kw = jnp.concatenate([rhs_ref[rows[0], ps], rhs_ref[rows[1], ps]], axis=0)
                stage[g * group_rows:(g + 1) * group_rows, :] = lax.dot_general(
                    lhs_ref[g], kw, NT_DIMS, preferred_element_type=F32)

        first = first_slot(4)
        issue(first, sa_ref)

        def body(q, carry):
            s = first + 4 * q
            issue(s + 2, sb_ref)
            consume(s, sa_ref)
            issue(s + 4, sa_ref)
            consume(s + 2, sb_ref)
            return carry

        lax.fori_loop(0, num_groups(4) - 1, body, 0)
        issue(last - 1, sb_ref)
        consume(last - 3, sa_ref)
        consume_last(last - 1, sb_ref)

    def score_pair(s0, stage):
        for u in range(2):
            tot = jnp.zeros((T, T), F32)
            for h in range(IDX_HEADS):
                tot = tot + jnp.maximum(stage[h * T:(h + 1) * T, u * T:(u + 1) * T], 0.0) * wts_ref[h]
            score = jnp.where(admissible(s0 + u - DUMMY), tot * IDX_SCALE, NEG_INF)
            bits = lax.bitcast_convert_type(score, I32)
            key_ref[s0 + u] = bits ^ ((bits >> 31) & 0x7FFFFFFF)

    sweep(iqs_ref, ik_ref, score_pair, score_pair)
    for d in range(DUMMY):
        key_ref[d] = jnp.full((T, T), I32_MIN, I32)

    def row_count(pred):
        one = jnp.ones((T, T), I32)
        zero = jnp.zeros((T, T), I32)
        first = first_slot(8)

        def body(g, acc):
            hits = [jnp.where(pred(first + 8 * g + u), one, zero) for u in range(8)]
            while len(hits) > 1:
                hits = [a + b for a, b in zip(hits[::2], hits[1::2])]
            return acc + hits[0]

        acc = lax.fori_loop(0, num_groups(8), body, zero)
        return jnp.sum(acc.astype(F32), axis=-1, keepdims=True)

    def value_bit(it, state):
        t, reach = state
        cand = t + (jnp.int32(1) << (31 - it))
        wide = jnp.broadcast_to(cand, (T, T))
        cnt = row_count(lambda s: key_ref[s] >= wide)
        fits = cnt >= topk
        return jnp.where(fits, cand, t), jnp.where(fits, cnt, reach)

    everything = (8 * T * num_groups(8)).astype(F32)
    state = lax.fori_loop(0, SETTLE_FROM_BIT, value_bit, (jnp.full((T, 1), I32_MIN, I32),
                                                          jnp.full((T, 1), everything, F32)))

    wide = jnp.broadcast_to(state[0], (T, T))
    final = row_count(lambda s: key_ref[s] > wide) < topk

    def unsettled(carry):
        it, (_, reach) = carry
        return (it < 32) & (jnp.max(jnp.where(final, topk, reach)) > topk)

    def two_bits(carry):
        it, state = carry
        return it + 2, value_bit(it + 1, value_bit(it, state))

    _, (thr, reach) = lax.while_loop(unsettled, two_bits, (jnp.int32(SETTLE_FROM_BIT), state))
    thr = jnp.broadcast_to(thr, (T, T))
    jdx_ref[...] = jnp.full((T, T), 2 ** IDX_BITS, I32)

    @pl.when(jnp.max(reach) > topk)
    def _():
        need = topk - (reach - row_count(lambda s: key_ref[s] == thr))
        p_r = lax.broadcasted_iota(I32, (T, 2 * T), 0)
        p_c = lax.broadcasted_iota(I32, (T, 2 * T), 1)
        prefix_ones = jnp.where((p_c >= T) | (p_r <= p_c), 1.0, 0.0).astype(BF16)
        first = first_slot(8)

        def body(g, state):
            seen, best = state
            slots = [first + 8 * g + u for u in range(8)]
            tied = [key_ref[s] == thr for s in slots]
            sums = [jnp.dot(jnp.where(t, 1.0, 0.0).astype(BF16), prefix_ones,
                            preferred_element_type=F32) for t in tied]
            for s, t, sm in zip(slots, tied, sums):
                keep = t & (seen + sm[:, :T] <= need)
                best = jnp.maximum(best, jnp.where(keep, (s - DUMMY) * T + col, -1))
                seen = seen + sm[:, T:]
            return seen, best

        _, best = lax.fori_loop(0, num_groups(8), body,
                                (jnp.zeros((T, T), F32), jnp.full((T, T), -1, I32)))
        jdx_ref[...] = jnp.broadcast_to(jnp.max(best, axis=-1, keepdims=True), (T, T))

    jdx = jdx_ref[...]

    def selection_mask(s):
        key = key_ref[s]
        sel = (key > thr) | ((key == thr) & ((s - DUMMY) * T + col <= jdx))
        return jnp.where(sel & admissible(s - DUMMY), 0.0, NEG_INF)

    head_of_lane = (lax.broadcasted_iota(I32, (A_HEADS * A_DIM, T), 0) // A_DIM
                    == lax.broadcasted_iota(I32, (A_HEADS * A_DIM, T), 1))
    head_sum = jnp.where(head_of_lane, 1.0, 0.0).astype(BF16)

    def head_norms2(t):
        t = t.astype(F32)
        return jnp.dot((t * t).astype(BF16), head_sum, preferred_element_type=F32)

    @pl.when(i == 0)
    def _():
        def widest(t, best):
            return jnp.maximum(best, head_norms2(k_ref[pl.ds(pl.multiple_of(t * T, T), T), :]))
        best = lax.fori_loop(0, k_ref.shape[0] // T, widest, jnp.zeros((T, T), F32))
        kmax_ref[...] = jnp.broadcast_to(jnp.max(best, axis=0, keepdims=True), kmax_ref.shape)

    bound2 = head_norms2(aq_ref[...]) * kmax_ref[0:1, :] * (NORM_SLACK / A_DIM)
    room = LOGIT_LIMIT - jnp.max(jnp.abs(bias_ref[...]))
    small_logits = (room > 0.0) & (jnp.max(bound2) <= room * room)

    ls_ref[...] = jnp.zeros(ls_ref.shape, F32)
    acc_ref[...] = jnp.zeros(acc_ref.shape, F32)

    def attend(s0, stage, near, final):
        madd = [selection_mask(s0), selection_mask(s0 + 1)]
        for hp in range(A_HEADS // 2):
            weights = []
            for h in (2 * hp, 2 * hp + 1):
                parts = [stage[h * T:(h + 1) * T, u * T:(u + 1) * T] + madd[u] for u in range(2)]
                if near:
                    parts = [parts[u] + bias_ref[h, 1 - u] for u in range(2)]
                if not final:
                    mx_ref[h] = jnp.maximum(mx_ref[h], jnp.maximum(parts[0], parts[1]))
                else:
                    es = [jnp.exp(part - mx_ref[h]) for part in parts]
                    ls_ref[h] = ls_ref[h] + (es[0] + es[1])
                    weights.append(jnp.concatenate(es, axis=1).astype(BF16))
            if final:
                ps = slice(hp * T, (hp + 1) * T)
                vw = jnp.concatenate([v_ref[tile_rows(s0), ps], v_ref[tile_rows(s0 + 1), ps]], axis=0)
                acc_ref[hp] = acc_ref[hp] + jnp.dot(jnp.concatenate(weights, axis=0), vw,
                                                    preferred_element_type=F32)

    def attention_sweep(final):
        sweep(qm_ref, k_ref,
              functools.partial(attend, near=False, final=final),
              functools.partial(attend, near=True, final=final))

    @pl.when(small_logits)
    def _():
        mx_ref[...] = jnp.zeros(mx_ref.shape, F32)

    @pl.when(jnp.logical_not(small_logits))
    def _():
        mx_ref[...] = jnp.full(mx_ref.shape, NEG_INF, F32)
        attention_sweep(False)
        for h in range(A_HEADS):
            mx_ref[h] = jnp.broadcast_to(jnp.max(mx_ref[h], axis=-1, keepdims=True), (T, T))

    attention_sweep(True)

    def row_total(h):
        return jnp.maximum(jnp.sum(ls_ref[h], axis=-1, keepdims=True), TINY)

    for hp in range(A_HEADS // 2):
        o0 = acc_ref[hp, :T, :] / row_total(2 * hp)
        o1 = acc_ref[hp, T:, :] / row_total(2 * hp + 1)
        g = ag_ref[:, hp * T:(hp + 1) * T].astype(F32)
        o_ref[:, hp * T:(hp + 1) * T] = (jnp.where(low, o0, o1) * _silu(g)).astype(o_ref.dtype)


def _dsa(pab, bias_tiles, topk):
    bsz, p, _ = pab.shape
    T = LANES
    nq = p // T
    hw = A_HEADS * A_DIM
    one = pl.Buffered(1)
    return pl.pallas_call(
        functools.partial(_dsa_kernel, topk=float(topk)),
        grid=(bsz, nq),
        in_specs=[
            pl.BlockSpec((None, T, hw), lambda b, i: (b, i, 0)),
            pl.BlockSpec((None, T, hw), lambda b, i: (b, i, 3)),
            pl.BlockSpec((None, T, hw), lambda b, i: (b, i, 4)),
            pl.BlockSpec((None, T, T), lambda b, i: (b, i, 33)),
            pl.BlockSpec((None, p, hw), lambda b, i: (b, 0, 1), pipeline_mode=one),
            pl.BlockSpec((None, p, hw), lambda b, i: (b, 0, 2), pipeline_mode=one),
            pl.BlockSpec((None, p, T), lambda b, i: (b, 0, 32), pipeline_mode=one),
            pl.BlockSpec((A_HEADS, 2, T, T), lambda b, i: (0, 0, 0, 0), pipeline_mode=one),
        ],
        out_specs=pl.BlockSpec((None, T, hw), lambda b, i: (b, i, 0)),
        out_shape=jax.ShapeDtypeStruct((bsz, p, hw), BF16),
        scratch_shapes=[
            pltpu.VMEM((nq + DUMMY, T, T), I32),
            pltpu.VMEM((A_HEADS * T, 2 * T), F32),
            pltpu.VMEM((A_HEADS * T, 2 * T), F32),
            pltpu.VMEM((1, IDX_HEADS * T, T), BF16),
            pltpu.VMEM((IDX_HEADS, T, T), F32),
            pltpu.VMEM((A_HEADS // 2, 2 * T, T), BF16),
            pltpu.VMEM((T, T), I32),
            pltpu.VMEM((8, T), F32),
            pltpu.VMEM((A_HEADS, T, T), F32),
            pltpu.VMEM((A_HEADS, T, T), F32),
            pltpu.VMEM((A_HEADS // 2, 2 * T, T), F32),
        ],
        compiler_params=_params(("parallel", "arbitrary")),
        name="dsa",
    )(pab, pab, pab, pab, pab, pab, pab, bias_tiles)


def _gla_kernel(q_ref, k_ref, v_ref, g_ref, a_ref, w2_ref, gb_ref, o_ref, st_ref, *, nchunk):
    C = CHUNK
    T = LANES

    @pl.when(pl.program_id(1) == 0)
    def _():
        st_ref[...] = jnp.zeros(st_ref.shape, F32)

    r_i = lax.broadcasted_iota(I32, (C, C), 0)
    c_i = lax.broadcasted_iota(I32, (C, C), 1)
    causal = c_i <= r_i
    tri = jnp.where(causal, 1.0, 0.0).astype(BF16)
    lane = lax.broadcasted_iota(I32, (C, T), 1)
    sd_r = lax.broadcasted_iota(I32, (2 * B_DV, T), 0)
    sd_c = lax.broadcasted_iota(I32, (2 * B_DV, T), 1)
    blockdiag = (sd_r >= B_DV) == (sd_c >= B_DK)

    pairs = range(B_HEADS // 2)
    lanes = [slice(hp * T, (hp + 1) * T) for hp in pairs]
    G = math.gcd(nchunk, CHUNK_GROUP)

    def group(gi, carry):
        rows = [pl.ds(pl.multiple_of((gi * G + c) * C, C), C) for c in range(G)]
        log_a = [_softplus_parts(jnp.dot(a_ref[r, :], w2_ref[...], preferred_element_type=F32)
                                 + gb_ref[...])[1] * (1.0 / GLA_TAU) for r in rows]
        terms = [_split3(x) for x in log_a]
        bcum = [sum(jnp.dot(tri, t, preferred_element_type=F32) for t in ts) for ts in terms]
        q_t, k_t, k_d, decay = [], [], [], []
        for r, b in zip(rows, bcum):
            b_last = b[C - 1:C, :]
            q = q_ref[r, :].astype(F32) * (B_DK ** -0.5)
            k = k_ref[r, :].astype(F32)
            q_t.append((q * jnp.exp(b)).astype(BF16))
            k_t.append((k * jnp.exp(-b)).astype(BF16))
            k_d.append((k * jnp.exp(b_last - b)).astype(BF16))
            decay.append(jnp.exp(b_last))
        att = {}
        for c in range(G):
            for h in range(B_HEADS):
                ls = lanes[h // 2]
                mine = (lane < B_DK) if h % 2 == 0 else (lane >= B_DK)
                qm = jnp.where(mine, q_t[c][:, ls], jnp.zeros((C, T), BF16))
                s = lax.dot_general(qm, k_t[c][:, ls], NT_DIMS, preferred_element_type=F32)
                att[c, h] = jnp.where(causal, s, 0.0).astype(BF16)
        contrib = {}
        for c in range(G):
            for hp in pairs:
                vp = v_ref[rows[c], hp * 2 * B_DV:(hp + 1) * 2 * B_DV]
                raw = lax.dot_general(vp, k_d[c][:, lanes[hp]], TN_DIMS, preferred_element_type=F32)
                contrib[c, hp] = jnp.where(blockdiag, raw, 0.0)
        before = {}
        for hp in pairs:
            st = st_ref[hp]
            for c in range(G):
                before[c, hp] = st
                st = st * decay[c][:, lanes[hp]] + contrib[c, hp]
            st_ref[hp] = st
        for c in range(G):
            for hp in pairs:
                o_inter = lax.dot_general(q_t[c][:, lanes[hp]], before[c, hp].astype(BF16), NT_DIMS,
                                          preferred_element_type=F32)
                for hh in range(2):
                    h = 2 * hp + hh
                    vs = slice(h * B_DV, (h + 1) * B_DV)
                    o = jnp.dot(att[c, h], v_ref[rows[c], vs], preferred_element_type=F32)
                    o = o + o_inter[:, hh * B_DV:(hh + 1) * B_DV]
                    o = o * lax.rsqrt(jnp.mean(o * o, axis=-1, keepdims=True) + NORM_EPS)
                    o = o * _silu(g_ref[rows[c], vs].astype(F32))
                    o_ref[rows[c], vs] = o.astype(o_ref.dtype)
        return carry

    lax.fori_loop(0, nchunk // G, group, 0)


def _gla(pab, w2, gb):
    bsz, p, _ = pab.shape
    tc = _pick_tile(p, (640, 128))
    hv = B_HEADS * B_DV
    hk = B_HEADS * B_DK
    return pl.pallas_call(
        functools.partial(_gla_kernel, nchunk=tc // CHUNK),
        grid=(bsz, p // tc),
        in_specs=[
            pl.BlockSpec((None, tc, hk), lambda b, c: (b, c, 14)),
            pl.BlockSpec((None, tc, hk), lambda b, c: (b, c, 15)),
            pl.BlockSpec((None, tc, hv), lambda b, c: (b, c, 5)),
            pl.BlockSpec((None, tc, hv), lambda b, c: (b, c, 6)),
            pl.BlockSpec((None, tc, LANES), lambda b, c: (b, c, 34)),
            pl.BlockSpec((LANES, hk), lambda b, c: (0, 0)),
            pl.BlockSpec((1, hk), lambda b, c: (0, 0)),
        ],
        out_specs=pl.BlockSpec((None, tc, hv), lambda b, c: (b, c, 0)),
        out_shape=jax.ShapeDtypeStruct((bsz, p, hv), BF16),
        scratch_shapes=[pltpu.VMEM((B_HEADS // 2, 2 * B_DV, 2 * B_DK), F32)],
        compiler_params=_params(("parallel", "arbitrary")),
        name="gla",
    )(pab, pab, pab, pab, pab, w2, gb)


def _ret_kernel(q_ref, k_ref, v_ref, g_ref, cos_ref, sin_ref, dmat_ref, zeta_ref, xi_ref,
                cdec_ref, o_ref, st_ref, *, nchunk):
    C = CHUNK
    T = LANES
    W = C_HEADS * C_DK

    @pl.when(pl.program_id(1) == 0)
    def _():
        st_ref[...] = jnp.zeros(st_ref.shape, F32)

    lane_w = lax.broadcasted_iota(I32, (C, W), 1)
    first_half = (lane_w & (C_DK - 1)) < (C_DK // 2)
    lane = lax.broadcasted_iota(I32, (C, T), 1)
    sd_r = lax.broadcasted_iota(I32, (2 * C_DV, T), 0)
    sd_c = lax.broadcasted_iota(I32, (2 * C_DV, T), 1)
    blockdiag = (sd_r >= C_DV) == (sd_c >= C_DK)

    def rotate(x, cos, sin_signed):
        swapped = jnp.where(first_half, pltpu.roll(x, W - C_DK // 2, 1), pltpu.roll(x, C_DK // 2, 1))
        return x * cos + swapped * sin_signed

    pairs = range(C_HEADS // 2)
    lanes = [slice(hp * T, (hp + 1) * T) for hp in pairs]
    G = math.gcd(nchunk, CHUNK_GROUP)

    def group(gi, carry):
        rows = [pl.ds(pl.multiple_of((gi * G + c) * C, C), C) for c in range(G)]
        q_b, k_b, q_x, k_z = [], [], [], []
        for r in rows:
            cos = cos_ref[r, :]
            sin = sin_ref[r, :]
            q = rotate(q_ref[r, :].astype(F32), cos, sin)
            k = rotate(k_ref[r, :].astype(F32), cos, sin) * (C_DK ** -0.5)
            q_b.append(q.astype(BF16))
            k_b.append(k.astype(BF16))
            q_x.append((q * xi_ref[...]).astype(BF16))
            k_z.append((k * zeta_ref[...]).astype(BF16))
        att = {}
        for c in range(G):
            for h in range(C_HEADS):
                ls = lanes[h // 2]
                mine = (lane < C_DK) if h % 2 == 0 else (lane >= C_DK)
                qm = jnp.where(mine, q_b[c][:, ls], jnp.zeros((C, T), BF16))
                s = lax.dot_general(qm, k_b[c][:, ls], NT_DIMS, preferred_element_type=F32)
                att[c, h] = (s * dmat_ref[h]).astype(BF16)
        contrib = {}
        for c in range(G):
            for hp in pairs:
                vp = v_ref[rows[c], hp * 2 * C_DV:(hp + 1) * 2 * C_DV]
                raw = lax.dot_general(vp, k_z[c][:, lanes[hp]], TN_DIMS, preferred_element_type=F32)
                contrib[c, hp] = jnp.where(blockdiag, raw, 0.0)
        before = {}
        for hp in pairs:
            st = st_ref[hp]
            for c in range(G):
                before[c, hp] = st
                st = st * cdec_ref[:, lanes[hp]] + contrib[c, hp]
            st_ref[hp] = st
        for c in range(G):
            for hp in pairs:
                o_inter = lax.dot_general(q_x[c][:, lanes[hp]], before[c, hp].astype(BF16), NT_DIMS,
                                          preferred_element_type=F32)
                for hh in range(2):
                    h = 2 * hp + hh
                    vs = slice(h * C_DV, (h + 1) * C_DV)
                    o = jnp.dot(att[c, h], v_ref[rows[c], vs], preferred_element_type=F32)
                    o = o + o_inter[:, hh * C_DV:(hh + 1) * C_DV]
                    o = o - jnp.mean(o, axis=-1, keepdims=True)
                    o = o * lax.rsqrt(jnp.mean(o * o, axis=-1, keepdims=True) + NORM_EPS)
                    o = o * _silu(g_ref[rows[c], vs].astype(F32))
                    o_ref[rows[c], vs] = o.astype(o_ref.dtype)
        return carry

    lax.fori_loop(0, nchunk // G, group, 0)


def _retention_tables(p):
    log_gamma = np.log(1.0 - np.exp2(-5.0 - np.arange(C_HEADS, dtype=np.float64)))
    i = np.arange(CHUNK, dtype=np.float64)
    diff = i[:, None] - i[None, :]
    dmat = np.where(diff >= 0, np.exp(log_gamma[:, None, None] * np.maximum(diff, 0.0)), 0.0)
    zeta = np.exp(log_gamma[:, None] * (CHUNK - 1 - i))
    xi = np.exp(log_gamma[:, None] * (i + 1))
    cdec = np.exp(log_gamma * CHUNK)
    widen = lambda t: np.repeat(t.T[:, :, None], C_DK, axis=2).reshape(CHUNK, C_HEADS * C_DK)
    half = C_DK // 2
    inv = jnp.asarray(ROPE_BASE, F32) ** (-jnp.arange(half, dtype=F32) / half)
    ang = jnp.arange(p, dtype=jnp.int32).astype(F32)[:, None] * inv[None, :]
    cos = jnp.tile(jnp.cos(ang), (1, 2 * C_HEADS))
    sin = jnp.sin(ang)
    sin_signed = jnp.tile(jnp.concatenate([-sin, sin], axis=1), (1, C_HEADS))
    return (cos, sin_signed, jnp.asarray(dmat, F32), jnp.asarray(widen(zeta), F32),
            jnp.asarray(widen(xi), F32),
            jnp.asarray(np.repeat(cdec, C_DK)[None, :], F32))


def _retention(pcd, tables):
    bsz, p, _ = pcd.shape
    cos, sin, dmat, zeta, xi, cdec = tables
    tc = _pick_tile(p, (640, 128))
    hv = C_HEADS * C_DV
    hk = C_HEADS * C_DK
    full = lambda shape: pl.BlockSpec(shape, lambda b, c: (0,) * len(shape))
    return pl.pallas_call(
        functools.partial(_ret_kernel, nchunk=tc // CHUNK),
        grid=(bsz, p // tc),
        in_specs=[
            pl.BlockSpec((None, tc, hk), lambda b, c: (b, c, 0)),
            pl.BlockSpec((None, tc, hk), lambda b, c: (b, c, 1)),
            pl.BlockSpec((None, tc, hv), lambda b, c: (b, c, 1)),
            pl.BlockSpec((None, tc, hv), lambda b, c: (b, c, 2)),
            pl.BlockSpec((tc, hk), lambda b, c: (c, 0)),
            pl.BlockSpec((tc, hk), lambda b, c: (c, 0)),
            full((C_HEADS, CHUNK, CHUNK)),
            full((CHUNK, hk)),
            full((CHUNK, hk)),
            full((1, hk)),
        ],
        out_specs=pl.BlockSpec((None, tc, hv), lambda b, c: (b, c, 0)),
        out_shape=jax.ShapeDtypeStruct((bsz, p, hv), BF16),
        scratch_shapes=[pltpu.VMEM((C_HEADS // 2, 2 * C_DV, 2 * C_DK), F32)],
        compiler_params=_params(("parallel", "arbitrary")),
        name="retention",
    )(pcd, pcd, pcd, pcd, cos, sin, dmat, zeta, xi, cdec)


def _sb_kernel(q_ref, g_ref, k_ref, v_ref, o_ref, qm_ref, run_ref, acc_ref, hi_ref, lo_ref, ls_ref,
               w_ref):
    T = LANES
    i = pl.program_id(1)
    row = lax.broadcasted_iota(I32, (T, T), 0)
    col = lax.broadcasted_iota(I32, (T, T), 1)
    qpos = i * T + row
    low = col < D_DIM
    u_r = lax.broadcasted_iota(I32, (T, 2 * T), 0)
    u_c = lax.broadcasted_iota(I32, (T, 2 * T), 1)
    suffix = jnp.where((u_c >= T) | (u_r > u_c), 1.0, 0.0).astype(BF16)

    for h in range(D_HEADS):
        pair = q_ref[:, (h // 2) * T:(h // 2 + 1) * T]
        mine = low if h % 2 == 0 else ~low
        qm_ref[h] = jnp.where(mine, pair * (D_DIM ** -0.5), jnp.zeros_like(pair))
    run_ref[...] = jnp.zeros(run_ref.shape, F32)
    acc_ref[...] = jnp.zeros(acc_ref.shape, F32)

    def walk(t0, count):
        kts = [i - t0 - j for j in range(count)]
        rows = [pl.ds(pl.multiple_of(jnp.maximum(kt, 0) * T, T), T) for kt in kts]
        for j, kt in enumerate(kts):
            kpos = kt * T + col
            ok = (kpos < qpos) & (kpos >= PAD_FRONT)
            for h in range(D_HEADS):
                ps = slice((h // 2) * T, (h // 2 + 1) * T)
                z = lax.dot_general(qm_ref[h], k_ref[rows[j], ps], NT_DIMS, preferred_element_type=F32)
                sp, logsig = _softplus_parts(z)
                log_1m = jnp.where(ok, -sp, 0.0)
                hi = lax.bitcast_convert_type(lax.bitcast_convert_type(log_1m, I32) & BF16_BITS, F32)
                hi_ref[j, h] = hi.astype(BF16)
                lo_ref[j, h] = (log_1m - hi).astype(BF16)
                ls_ref[j, h] = jnp.where(ok, logsig, NEG_INF)
        sums = {}
        for j in range(count):
            for h in range(D_HEADS):
                sums[j, h] = (jnp.dot(hi_ref[j, h], suffix, preferred_element_type=F32)
                              + jnp.dot(lo_ref[j, h], suffix, preferred_element_type=F32))
        slowest = None
        for h in range(D_HEADS):
            run = run_ref[h]
            for j in range(count):
                w_ref[j, h] = jnp.exp(ls_ref[j, h] + run + sums[j, h][:, :T]).astype(BF16)
                run = run + sums[j, h][:, T:]
            run_ref[h] = run
            slowest = run if slowest is None else jnp.maximum(slowest, run)
        for h in range(D_HEADS):
            ps = slice((h // 2) * T, (h // 2 + 1) * T)
            acc = acc_ref[h]
            for j in range(count):
                acc = acc + jnp.dot(w_ref[j, h], v_ref[rows[j], ps], preferred_element_type=F32)
            acc_ref[h] = acc
        return (jnp.max(slowest) > LOG_F32_UNDERFLOW).astype(I32)

    def one_more(state):
        t, _ = state
        return t + 1, walk(t, 1)

    lax.while_loop(lambda s: (s[0] <= i) & (s[1] > 0), one_more, (jnp.int32(SB_FIRST), walk(0, SB_FIRST)))

    for hp in range(D_HEADS // 2):
        ps = slice(hp * T, (hp + 1) * T)
        o = jnp.where(low, acc_ref[2 * hp], acc_ref[2 * hp + 1]) * _silu(g_ref[:, ps].astype(F32))
        o_ref[:, ps] = o.astype(o_ref.dtype)


def _stick_breaking(pcd):
    bsz, p, _ = pcd.shape
    T = LANES
    hw = D_HEADS * D_DIM
    one = pl.Buffered(1)
    return pl.pallas_call(
        _sb_kernel,
        grid=(bsz, p // T),
        in_specs=[
            pl.BlockSpec((None, T, hw), lambda b, i: (b, i, 3)),
            pl.BlockSpec((None, T, hw), lambda b, i: (b, i, 6)),
            pl.BlockSpec((None, p, hw), lambda b, i: (b, 0, 4), pipeline_mode=one),
            pl.BlockSpec((None, p, hw), lambda b, i: (b, 0, 5), pipeline_mode=one),
        ],
        out_specs=pl.BlockSpec((None, T, hw), lambda b, i: (b, i, 0)),
        out_shape=jax.ShapeDtypeStruct((bsz, p, hw), BF16),
        scratch_shapes=[
            pltpu.VMEM((D_HEADS, T, T), BF16),
            pltpu.VMEM((D_HEADS, T, T), F32),
            pltpu.VMEM((D_HEADS, T, T), F32),
            pltpu.VMEM((SB_FIRST, D_HEADS, T, T), BF16),
            pltpu.VMEM((SB_FIRST, D_HEADS, T, T), BF16),
            pltpu.VMEM((SB_FIRST, D_HEADS, T, T), F32),
            pltpu.VMEM((SB_FIRST, D_HEADS, T, T), BF16),
        ],
        compiler_params=_params(("parallel", "arbitrary")),
        name="stick_breaking",
    )(pcd, pcd, pcd, pcd)


def _rel_bucket_np(rel):
    half = REL_BUCKETS // 2
    max_exact = half // 2
    n = -rel
    ret = np.where(n < 0, half, 0)
    n = np.abs(n)
    edges = [math.ceil(max_exact * (REL_MAX_DIST / max_exact) ** (j / (half - max_exact)) - 1e-9)
             for j in range(1, half - max_exact)]
    large = max_exact + sum((n >= e).astype(np.int64) for e in edges)
    return ret + np.where(n < max_exact, n, large)


def _bias_tiles(rel_bias):
    i = np.arange(LANES)[:, None]
    j = np.arange(LANES)[None, :]
    idx = np.stack([_rel_bucket_np(j - i - LANES * d) for d in range(3)])
    assert (idx[2] == idx[2, 0, 0]).all()
    table = rel_bias.astype(F32) - rel_bias.astype(F32)[idx[2, 0, 0]]
    near = jnp.asarray(idx[:2], I32)[None]
    tiles = jnp.zeros((table.shape[1],) + near.shape[1:], F32)
    for bucket in range(REL_BUCKETS):
        tiles = jnp.where(near == bucket, table[bucket][:, None, None, None], tiles)
    return tiles


def _layout_ab(w):
    aq, ak, av, ag, iq, ik, iw, bq, bk, bv, bg, ba = jnp.split(w, np.cumsum(SPLIT_AB)[:-1].tolist(), axis=1)
    pad = lambda t: jnp.pad(t, ((0, 0), (0, LANES - t.shape[1])))
    cols = [aq, ak, av, ag, iq, bv, bg, bq, bk, jnp.concatenate([ik, ik], axis=1), pad(iw), pad(ba),
            jnp.zeros((w.shape[0], W_AB_PAD - W_AB_USED), w.dtype)]
    return jnp.concatenate(cols, axis=1).astype(BF16)


def kernel(x, meta_tokens, rel_bias, norm_g, final_g, w_in_ab, gla_gate_w2, gla_gate_b, w_out_ab,
           w_in_cd, w_out_cd):
    bsz, seq, d = x.shape
    p = seq + PAD_FRONT + N_META
    depth = norm_g.shape[0]
    topk = min(TOPK_MAX, seq // 4)
    h = jnp.concatenate([jnp.zeros((bsz, PAD_FRONT, d), x.dtype),
                         jnp.broadcast_to(meta_tokens.astype(x.dtype)[None], (bsz, N_META, d)),
                         x], axis=1)
    bias_tiles = _bias_tiles(rel_bias)
    tables = _retention_tables(p)
    assert depth >= 1
    for layer in range(depth):
        j = layer // 2
        if layer % 2 == 0:
            pab = _inproj(h, norm_g[layer], _layout_ab(w_in_ab[j]))
            first = _dsa(pab, bias_tiles, topk)
            w2 = jnp.pad(gla_gate_w2[j], ((0, LANES - GLA_GATE_RANK), (0, 0))).astype(BF16)
            second = _gla(pab, w2, gla_gate_b[j].reshape(1, -1).astype(F32))
            w_out = w_out_ab[j]
        else:
            pcd = _inproj(h, norm_g[layer], w_in_cd[j].astype(BF16))
            first = _retention(pcd, tables)
            second = _stick_breaking(pcd)
            w_out = w_out_cd[j]
        if layer + 1 < depth:
            h = _outproj(first, second, w_out, h)
    return _outproj_norm(first, second, w_out, h, final_g, seq)
```

```python
import functools
import math

import numpy as np
import jax
import jax.numpy as jnp
from jax import lax
from jax.experimental import pallas as pl
from jax.experimental.pallas import tpu as pltpu

F32 = jnp.float32
BF16 = jnp.bfloat16
I32 = jnp.int32

CHUNK = 64
CHUNK_SHIFT = CHUNK.bit_length() - 1
N_META = 16
PAD_FRONT = 128 - N_META
NORM_EPS = 1e-6
NEG_INF = -1e30
A_HEADS, A_DIM = 8, 64
IDX_HEADS, IDX_DIM = 8, 64
IDX_SCALE = (IDX_DIM ** -0.5) * (IDX_HEADS ** -0.5)
TOPK_MAX = 256
B_HEADS, B_DK, B_DV = 4, 64, 128
GLA_GATE_RANK = 16
GLA_TAU = 16.0
C_HEADS, C_DK, C_DV = 4, 64, 128
ROPE_BASE = 10000.0
D_HEADS, D_DIM = 8, 64
REL_BUCKETS = 32
REL_MAX_DIST = 128
SPLIT_AB = (512, 512, 512, 512, 512, 64, 8, 256, 256, 512, 512, 16)

LANES = 128
W_AB_USED = 4480
W_AB_PAD = 4608
I32_MIN = -2 ** 31
DUMMY = 7
IDX_BITS = 14
SETTLE_FROM_BIT = 22
LOG_F32_UNDERFLOW = -104.0
LOGIT_LIMIT = 40.0
NORM_SLACK = 1.05
TINY = 1e-30
BF16_BITS = -65536
CHUNK_GROUP = 10
SB_FIRST = 3
VMEM_LIMIT = 56 * 1024 * 1024

NT_DIMS = (((1,), (1,)), ((), ()))
TN_DIMS = (((0,), (0,)), ((), ()))


def _pick_tile(n, candidates):
    for c in candidates:
        if n % c == 0:
            return c
    raise ValueError(f"no tile for {n}")


def _params(sem):
    return pltpu.CompilerParams(dimension_semantics=sem, vmem_limit_bytes=VMEM_LIMIT)


def _silu(x):
    return x / (1.0 + jnp.exp(-x))


def _softplus_parts(z):
    t = jnp.log(1.0 + jnp.exp(-jnp.abs(z)))
    return jnp.maximum(z, 0.0) + t, jnp.minimum(z, 0.0) - t


def _split3(x):
    a = x.astype(BF16)
    r = x - a.astype(F32)
    b = r.astype(BF16)
    c = (r - b.astype(F32)).astype(BF16)
    return a, b, c


def _inproj_kernel(x_ref, g_ref, w_ref, o_ref, hn_ref):
    @pl.when(pl.program_id(2) == 0)
    def _():
        x = x_ref[...]
        ms = jnp.mean(x * x, axis=-1, keepdims=True)
        hn_ref[...] = (x * lax.rsqrt(ms + NORM_EPS) * g_ref[...]).astype(BF16)

    o_ref[...] = jnp.dot(hn_ref[...], w_ref[...], preferred_element_type=F32).astype(o_ref.dtype)


def _inproj(h, g, w):
    bsz, p, d = h.shape
    n = w.shape[1]
    tp = _pick_tile(p, (1664, 1280, 640, 128))
    tn = _pick_tile(n, (1792, 1536, 896, 512, 128))
    return pl.pallas_call(
        _inproj_kernel,
        grid=(bsz, p // tp, n // tn),
        in_specs=[
            pl.BlockSpec((None, tp, d), lambda b, i, j: (b, i, 0)),
            pl.BlockSpec((1, d), lambda b, i, j: (0, 0)),
            pl.BlockSpec((d, tn), lambda b, i, j: (0, j)),
        ],
        out_specs=pl.BlockSpec((None, tp, tn), lambda b, i, j: (b, i, j)),
        out_shape=jax.ShapeDtypeStruct((bsz, p, n), BF16),
        scratch_shapes=[pltpu.VMEM((tp, d), BF16)],
        compiler_params=_params(("parallel", "parallel", "arbitrary")),
        name="inproj",
    )(h, g.reshape(1, d), w)


def _proj_kernel(x_ref, w_ref, o_ref):
    o_ref[...] = jnp.dot(x_ref[...], w_ref[...], preferred_element_type=F32).astype(o_ref.dtype)


def _proj(hn, w):
    bsz, p, d = hn.shape
    n = w.shape[1]
    tp = _pick_tile(p, (1664, 1280, 640, 128))
    tn = _pick_tile(n, (1792, 1536, 896, 512, 128))
    return pl.pallas_call(
        _proj_kernel,
        grid=(bsz, p // tp, n // tn),
        in_specs=[
            pl.BlockSpec((None, tp, d), lambda b, i, j: (b, i, 0)),
            pl.BlockSpec((d, tn), lambda b, i, j: (0, j)),
        ],
        out_specs=pl.BlockSpec((None, tp, tn), lambda b, i, j: (b, i, j)),
        out_shape=jax.ShapeDtypeStruct((bsz, p, n), BF16),
        compiler_params=_params(("parallel", "parallel", "parallel")),
        name="proj",
    )(hn, w)


def _outproj_kernel(m1_ref, m2_ref, w1_ref, w2_ref, h_ref, g_ref, o_ref, hn_ref, *, tp):
    y = jnp.dot(m1_ref[...], w1_ref[...], preferred_element_type=F32)
    y = y + jnp.dot(m2_ref[...], w2_ref[...], preferred_element_type=F32)
    pos = pl.program_id(1) * tp + lax.broadcasted_iota(I32, y.shape, 0)
    x = h_ref[...] + jnp.where(pos >= PAD_FRONT, y, 0.0)
    o_ref[...] = x
    ms = jnp.mean(x * x, axis=-1, keepdims=True)
    hn_ref[...] = (x * lax.rsqrt(ms + NORM_EPS) * g_ref[...]).astype(BF16)


def _outproj(m1, m2, w_out, h, g_next):
    bsz, p, d = h.shape
    k1 = m1.shape[-1]
    k2 = m2.shape[-1]
    tp = _pick_tile(p, (832, 640, 128))
    w = w_out.astype(BF16)
    return pl.pallas_call(
        functools.partial(_outproj_kernel, tp=tp),
        grid=(bsz, p // tp),
        in_specs=[
            pl.BlockSpec((None, tp, k1), lambda b, i: (b, i, 0)),
            pl.BlockSpec((None, tp, k2), lambda b, i: (b, i, 0)),
            pl.BlockSpec((k1, d), lambda b, i: (0, 0)),
            pl.BlockSpec((k2, d), lambda b, i: (0, 0)),
            pl.BlockSpec((None, tp, d), lambda b, i: (b, i, 0)),
            pl.BlockSpec((1, d), lambda b, i: (0, 0)),
        ],
        out_specs=[pl.BlockSpec((None, tp, d), lambda b, i: (b, i, 0)),
                   pl.BlockSpec((None, tp, d), lambda b, i: (b, i, 0))],
        out_shape=[jax.ShapeDtypeStruct((bsz, p, d), F32), jax.ShapeDtypeStruct((bsz, p, d), BF16)],
        compiler_params=_params(("parallel", "parallel")),
        name="outproj",
    )(m1, m2, w[:k1], w[k1:], h, g_next.reshape(1, d))


def _outproj_norm_kernel(m1_ref, m2_ref, w1_ref, w2_ref, h_ref, g_ref, o_ref):
    y = jnp.dot(m1_ref[...], w1_ref[...], preferred_element_type=F32)
    y = y + jnp.dot(m2_ref[...], w2_ref[...], preferred_element_type=F32)
    x = h_ref[...] + y
    ms = jnp.mean(x * x, axis=-1, keepdims=True)
    o_ref[...] = x * lax.rsqrt(ms + NORM_EPS) * g_ref[...]


def _outproj_norm(m1, m2, w_out, h, g, seq):
    bsz, p, d = h.shape
    k1 = m1.shape[-1]
    k2 = m2.shape[-1]
    tp = LANES
    skip = (p - seq) // tp
    assert skip * tp == p - seq
    w = w_out.astype(BF16)
    rows = lambda b, i: (b, i + skip, 0)
    return pl.pallas_call(
        _outproj_norm_kernel,
        grid=(bsz, seq // tp),
        in_specs=[
            pl.BlockSpec((None, tp, k1), rows),
            pl.BlockSpec((None, tp, k2), rows),
            pl.BlockSpec((k1, d), lambda b, i: (0, 0)),
            pl.BlockSpec((k2, d), lambda b, i: (0, 0)),
            pl.BlockSpec((None, tp, d), rows),
            pl.BlockSpec((1, d), lambda b, i: (0, 0)),
        ],
        out_specs=pl.BlockSpec((None, tp, d), lambda b, i: (b, i, 0)),
        out_shape=jax.ShapeDtypeStruct((bsz, seq, d), F32),
        compiler_params=_params(("parallel", "parallel")),
        name="outproj_norm",
    )(m1, m2, w[:k1], w[k1:], h, g.reshape(1, d))


def _dsa_kernel(aq_ref, ag_ref, iq_ref, iw_ref, k_ref, v_ref, ik_ref, bias_ref, o_ref,
                key_ref, sa_ref, sb_ref, iqs_ref, wts_ref, qm_ref,
                jdx_ref, kmax_ref, mx_ref, ls_ref, acc_ref, *, topk):
    T = LANES
    i = pl.program_id(1)
    last = i + DUMMY

    def num_groups(group):
        return (i + group) // group

    def first_slot(group):
        return last + 1 - group * num_groups(group)

    row = lax.broadcasted_iota(I32, (T, T), 0)
    col = lax.broadcasted_iota(I32, (T, T), 1)
    qchunk = (i * T + row) >> CHUNK_SHIFT
    low = col < A_DIM

    def admissible(kt):
        kpos = kt * T + col
        return (kpos >= PAD_FRONT) & ((kpos >> CHUNK_SHIFT) <= qchunk)

    def tile_rows(slot):
        return pl.ds(pl.multiple_of(jnp.maximum(slot - DUMMY, 0) * T, T), T)

    def for_groups(group, body):
        first = first_slot(group)

        def step(g, carry):
            body(first + group * g)
            return carry

        lax.fori_loop(0, num_groups(group), step, 0)

    iw = iw_ref[...].astype(F32)
    for h in range(IDX_HEADS):
        pair = iq_ref[:, (h // 2) * T:(h // 2 + 1) * T]
        mine = low if h % 2 == 0 else ~low
        iqs_ref[0, h * T:(h + 1) * T, :] = jnp.where(mine, pair, jnp.zeros_like(pair))
        wts_ref[h] = jnp.broadcast_to(iw[:, h:h + 1], (T, T))
    for h in range(A_HEADS):
        pair = aq_ref[:, (h // 2) * T:(h // 2 + 1) * T]
        mine = low if h % 2 == 0 else ~low
        qm_ref[h // 2, (h % 2) * T:(h % 2 + 1) * T, :] = jnp.where(
            mine, pair * (A_DIM ** -0.5), jnp.zeros_like(pair))

    def sweep(lhs_ref, rhs_ref, consume, consume_last):
        group_rows = lhs_ref.shape[1]

        def issue(s0, stage):
            rows = [tile_rows(s0), tile_rows(s0 + 1)]
            for g in range(lhs_ref.shape[0]):
                ps = slice(g * T, (g + 1) * T)
                kw = jnp.concatenate([rhs_ref[rows[0], ps], rhs_ref[rows[1], ps]], axis=0)
                stage[g * group_rows:(g + 1) * group_rows, :] = lax.dot_general(
                    lhs_ref[g], kw, NT_DIMS, preferred_element_type=F32)

        first = first_slot(4)
        issue(first, sa_ref)

        def body(q, carry):
            s = first + 4 * q
            issue(s + 2, sb_ref)
            consume(s, sa_ref)
            issue(s + 4, sa_ref)
            consume(s + 2, sb_ref)
            return carry

        lax.fori_loop(0, num_groups(4) - 1, body, 0)
        issue(last - 1, sb_ref)
        consume(last - 3, sa_ref)
        consume_last(last - 1, sb_ref)

    def score_pair(s0, stage):
        for u in range(2):
            tot = jnp.zeros((T, T), F32)
            for h in range(IDX_HEADS):
                tot = tot + jnp.maximum(stage[h * T:(h + 1) * T, u * T:(u + 1) * T], 0.0) * wts_ref[h]
            score = jnp.where(admissible(s0 + u - DUMMY), tot * IDX_SCALE, NEG_INF)
            bits = lax.bitcast_convert_type(score, I32)
            key_ref[s0 + u] = bits ^ ((bits >> 31) & 0x7FFFFFFF)

    sweep(iqs_ref, ik_ref, score_pair, score_pair)
    for d in range(DUMMY):
        key_ref[d] = jnp.full((T, T), I32_MIN, I32)

    def row_count(pred):
        one = jnp.ones((T, T), I32)
        zero = jnp.zeros((T, T), I32)
        first = first_slot(8)

        def body(g, acc):
            hits = [jnp.where(pred(first + 8 * g + u), one, zero) for u in range(8)]
            while len(hits) > 1:
                hits = [a + b for a, b in zip(hits[::2], hits[1::2])]
            return acc + hits[0]

        acc = lax.fori_loop(0, num_groups(8), body, zero)
        return jnp.sum(acc.astype(F32), axis=-1, keepdims=True)

    def value_bit(it, state):
        t, reach = state
        cand = t + (jnp.int32(1) << (31 - it))
        wide = jnp.broadcast_to(cand, (T, T))
        cnt = row_count(lambda s: key_ref[s] >= wide)
        fits = cnt >= topk
        return jnp.where(fits, cand, t), jnp.where(fits, cnt, reach)

    everything = (8 * T * num_groups(8)).astype(F32)
    state = lax.fori_loop(0, SETTLE_FROM_BIT, value_bit, (jnp.full((T, 1), I32_MIN, I32),
                                                          jnp.full((T, 1), everything, F32)))

    wide = jnp.broadcast_to(state[0], (T, T))
    final = row_count(lambda s: key_ref[s] > wide) < topk

    def unsettled(carry):
        it, (_, reach) = carry
        return (it < 32) & (jnp.max(jnp.where(final, topk, reach)) > topk)

    def two_bits(carry):
        it, state = carry
        return it + 2, value_bit(it + 1, value_bit(it, state))

    _, (thr, reach) = lax.while_loop(unsettled, two_bits, (jnp.int32(SETTLE_FROM_BIT), state))
    thr = jnp.broadcast_to(thr, (T, T))
    jdx_ref[...] = jnp.full((T, T), 2 ** IDX_BITS, I32)

    @pl.when(jnp.max(reach) > topk)
    def _():
        need = topk - (reach - row_count(lambda s: key_ref[s] == thr))
        p_r = lax.broadcasted_iota(I32, (T, 2 * T), 0)
        p_c = lax.broadcasted_iota(I32, (T, 2 * T), 1)
        prefix_ones = jnp.where((p_c >= T) | (p_r <= p_c), 1.0, 0.0).astype(BF16)
        first = first_slot(8)

        def body(g, state):
            seen, best = state
            slots = [first + 8 * g + u for u in range(8)]
            tied = [key_ref[s] == thr for s in slots]
            sums = [jnp.dot(jnp.where(t, 1.0, 0.0).astype(BF16), prefix_ones,
                            preferred_element_type=F32) for t in tied]
            for s, t, sm in zip(slots, tied, sums):
                keep = t & (seen + sm[:, :T] <= need)
                best = jnp.maximum(best, jnp.where(keep, (s - DUMMY) * T + col, -1))
                seen = seen + sm[:, T:]
            return seen, best

        _, best = lax.fori_loop(0, num_groups(8), body,
                                (jnp.zeros((T, T), F32), jnp.full((T, T), -1, I32)))
        jdx_ref[...] = jnp.broadcast_to(jnp.max(best, axis=-1, keepdims=True), (T, T))

    jdx = jdx_ref[...]

    def selection_mask(s):
        key = key_ref[s]
        sel = (key > thr) | ((key == thr) & ((s - DUMMY) * T + col <= jdx))
        return jnp.where(sel & admissible(s - DUMMY), 0.0, NEG_INF)

    head_of_lane = (lax.broadcasted_iota(I32, (A_HEADS * A_DIM, T), 0) // A_DIM
                    == lax.broadcasted_iota(I32, (A_HEADS * A_DIM, T), 1))
    head_sum = jnp.where(head_of_lane, 1.0, 0.0).astype(BF16)

    def head_norms2(t):
        t = t.astype(F32)
        return jnp.dot((t * t).astype(BF16), head_sum, preferred_element_type=F32)

    @pl.when(i == 0)
    def _():
        def widest(t, best):
            return jnp.maximum(best, head_norms2(k_ref[pl.ds(pl.multiple_of(t * T, T), T), :]))
        best = lax.fori_loop(0, k_ref.shape[0] // T, widest, jnp.zeros((T, T), F32))
        kmax_ref[...] = jnp.broadcast_to(jnp.max(best, axis=0, keepdims=True), kmax_ref.shape)

    bound2 = head_norms2(aq_ref[...]) * kmax_ref[0:1, :] * (NORM_SLACK / A_DIM)
    room = LOGIT_LIMIT - jnp.max(jnp.abs(bias_ref[...]))
    small_logits = (room > 0.0) & (jnp.max(bound2) <= room * room)

    ls_ref[...] = jnp.zeros(ls_ref.shape, F32)
    acc_ref[...] = jnp.zeros(acc_ref.shape, F32)

    def attend(s0, stage, near, final):
        madd = [selection_mask(s0), selection_mask(s0 + 1)]
        for hp in range(A_HEADS // 2):
            weights = []
            for h in (2 * hp, 2 * hp + 1):
                parts = [stage[h * T:(h + 1) * T, u * T:(u + 1) * T] + madd[u] for u in range(2)]
                if near:
                    parts = [parts[u] + bias_ref[h, 1 - u] for u in range(2)]
                if not final:
                    mx_ref[h] = jnp.maximum(mx_ref[h], jnp.maximum(parts[0], parts[1]))
                else:
                    es = [jnp.exp(part - mx_ref[h]) for part in parts]
                    ls_ref[h] = ls_ref[h] + (es[0] + es[1])
                    weights.append(jnp.concatenate(es, axis=1).astype(BF16))
            if final:
                ps = slice(hp * T, (hp + 1) * T)
                vw = jnp.concatenate([v_ref[tile_rows(s0), ps], v_ref[tile_rows(s0 + 1), ps]], axis=0)
                acc_ref[hp] = acc_ref[hp] + jnp.dot(jnp.concatenate(weights, axis=0), vw,
                                                    preferred_element_type=F32)

    def attention_sweep(final):
        sweep(qm_ref, k_ref,
              functools.partial(attend, near=False, final=final),
              functools.partial(attend, near=True, final=final))

    @pl.when(small_logits)
    def _():
        mx_ref[...] = jnp.zeros(mx_ref.shape, F32)

    @pl.when(jnp.logical_not(small_logits))
    def _():
        mx_ref[...] = jnp.full(mx_ref.shape, NEG_INF, F32)
        attention_sweep(False)
        for h in range(A_HEADS):
            mx_ref[h] = jnp.broadcast_to(jnp.max(mx_ref[h], axis=-1, keepdims=True), (T, T))

    attention_sweep(True)

    def row_total(h):
        return jnp.maximum(jnp.sum(ls_ref[h], axis=-1, keepdims=True), TINY)

    for hp in range(A_HEADS // 2):
        o0 = acc_ref[hp, :T, :] / row_total(2 * hp)
        o1 = acc_ref[hp, T:, :] / row_total(2 * hp + 1)
        g = ag_ref[:, hp * T:(hp + 1) * T].astype(F32)
        o_ref[:, hp * T:(hp + 1) * T] = (jnp.where(low, o0, o1) * _silu(g)).astype(o_ref.dtype)


def _dsa(pab, bias_tiles, topk):
    bsz, p, _ = pab.shape
    T = LANES
    nq = p // T
    hw = A_HEADS * A_DIM
    one = pl.Buffered(1)
    return pl.pallas_call(
        functools.partial(_dsa_kernel, topk=float(topk)),
        grid=(bsz, nq),
        in_specs=[
            pl.BlockSpec((None, T, hw), lambda b, i: (b, i, 0)),
            pl.BlockSpec((None, T, hw), lambda b, i: (b, i, 3)),
            pl.BlockSpec((None, T, hw), lambda b, i: (b, i, 4)),
            pl.BlockSpec((None, T, T), lambda b, i: (b, i, 33)),
            pl.BlockSpec((None, p, hw), lambda b, i: (b, 0, 1), pipeline_mode=one),
            pl.BlockSpec((None, p, hw), lambda b, i: (b, 0, 2), pipeline_mode=one),
            pl.BlockSpec((None, p, T), lambda b, i: (b, 0, 32), pipeline_mode=one),
            pl.BlockSpec((A_HEADS, 2, T, T), lambda b, i: (0, 0, 0, 0), pipeline_mode=one),
        ],
        out_specs=pl.BlockSpec((None, T, hw), lambda b, i: (b, i, 0)),
        out_shape=jax.ShapeDtypeStruct((bsz, p, hw), BF16),
        scratch_shapes=[
            pltpu.VMEM((nq + DUMMY, T, T), I32),
            pltpu.VMEM((A_HEADS * T, 2 * T), F32),
            pltpu.VMEM((A_HEADS * T, 2 * T), F32),
            pltpu.VMEM((1, IDX_HEADS * T, T), BF16),
            pltpu.VMEM((IDX_HEADS, T, T), F32),
            pltpu.VMEM((A_HEADS // 2, 2 * T, T), BF16),
            pltpu.VMEM((T, T), I32),
            pltpu.VMEM((8, T), F32),
            pltpu.VMEM((A_HEADS, T, T), F32),
            pltpu.VMEM((A_HEADS, T, T), F32),
            pltpu.VMEM((A_HEADS // 2, 2 * T, T), F32),
        ],
        compiler_params=_params(("parallel", "arbitrary")),
        name="dsa",
    )(pab, pab, pab, pab, pab, pab, pab, bias_tiles)


def _gla_kernel(q_ref, k_ref, v_ref, g_ref, a_ref, w2_ref, gb_ref, o_ref, st_ref, *, nchunk):
    C = CHUNK
    T = LANES

    @pl.when(pl.program_id(1) == 0)
    def _():
        st_ref[...] = jnp.zeros(st_ref.shape, F32)

    r_i = lax.broadcasted_iota(I32, (C, C), 0)
    c_i = lax.broadcasted_iota(I32, (C, C), 1)
    causal = c_i <= r_i
    tri = jnp.where(causal, 1.0, 0.0).astype(BF16)
    lane = lax.broadcasted_iota(I32, (C, T), 1)
    sd_r = lax.broadcasted_iota(I32, (2 * B_DV, T), 0)
    sd_c = lax.broadcasted_iota(I32, (2 * B_DV, T), 1)
    blockdiag = (sd_r >= B_DV) == (sd_c >= B_DK)

    pairs = range(B_HEADS // 2)
    lanes = [slice(hp * T, (hp + 1) * T) for hp in pairs]
    G = math.gcd(nchunk, CHUNK_GROUP)

    def group(gi, carry):
        rows = [pl.ds(pl.multiple_of((gi * G + c) * C, C), C) for c in range(G)]
        log_a = [_softplus_parts(jnp.dot(a_ref[r, :], w2_ref[...], preferred_element_type=F32)
                                 + gb_ref[...])[1] * (1.0 / GLA_TAU) for r in rows]
        terms = [_split3(x) for x in log_a]
        bcum = [sum(jnp.dot(tri, t, preferred_element_type=F32) for t in ts) for ts in terms]
        q_t, k_t, k_d, decay = [], [], [], []
        for r, b in zip(rows, bcum):
            b_last = b[C - 1:C, :]
            q = q_ref[r, :].astype(F32) * (B_DK ** -0.5)
            k = k_ref[r, :].astype(F32)
            q_t.append((q * jnp.exp(b)).astype(BF16))
            k_t.append((k * jnp.exp(-b)).astype(BF16))
            k_d.append((k * jnp.exp(b_last - b)).astype(BF16))
            decay.append(jnp.exp(b_last))
        att = {}
        for c in range(G):
            for h in range(B_HEADS):
                ls = lanes[h // 2]
                mine = (lane < B_DK) if h % 2 == 0 else (lane >= B_DK)
                qm = jnp.where(mine, q_t[c][:, ls], jnp.zeros((C, T), BF16))
                s = lax.dot_general(qm, k_t[c][:, ls], NT_DIMS, preferred_element_type=F32)
                att[c, h] = jnp.where(causal, s, 0.0).astype(BF16)
        contrib = {}
        for c in range(G):
            for hp in pairs:
                vp = v_ref[rows[c], hp * 2 * B_DV:(hp + 1) * 2 * B_DV]
                raw = lax.dot_general(vp, k_d[c][:, lanes[hp]], TN_DIMS, preferred_element_type=F32)
                contrib[c, hp] = jnp.where(blockdiag, raw, 0.0)
        before = {}
        for hp in pairs:
            st = st_ref[hp]
            for c in range(G):
                before[c, hp] = st
                st = st * decay[c][:, lanes[hp]] + contrib[c, hp]
            st_ref[hp] = st
        for c in range(G):
            for hp in pairs:
                o_inter = lax.dot_general(q_t[c][:, lanes[hp]], before[c, hp].astype(BF16), NT_DIMS,
                                          preferred_element_type=F32)
                for hh in range(2):
                    h = 2 * hp + hh
                    vs = slice(h * B_DV, (h + 1) * B_DV)
                    o = jnp.dot(att[c, h], v_ref[rows[c], vs], preferred_element_type=F32)
                    o = o + o_inter[:, hh * B_DV:(hh + 1) * B_DV]
                    o = o * lax.rsqrt(jnp.mean(o * o, axis=-1, keepdims=True) + NORM_EPS)
                    o = o * _silu(g_ref[rows[c], vs].astype(F32))
                    o_ref[rows[c], vs] = o.astype(o_ref.dtype)
        return carry

    lax.fori_loop(0, nchunk // G, group, 0)


def _gla(pab, w2, gb):
    bsz, p, _ = pab.shape
    tc = _pick_tile(p, (640, 128))
    hv = B_HEADS * B_DV
    hk = B_HEADS * B_DK
    return pl.pallas_call(
        functools.partial(_gla_kernel, nchunk=tc // CHUNK),
        grid=(bsz, p // tc),
        in_specs=[
            pl.BlockSpec((None, tc, hk), lambda b, c: (b, c, 14)),
            pl.BlockSpec((None, tc, hk), lambda b, c: (b, c, 15)),
            pl.BlockSpec((None, tc, hv), lambda b, c: (b, c, 5)),
            pl.BlockSpec((None, tc, hv), lambda b, c: (b, c, 6)),
            pl.BlockSpec((None, tc, LANES), lambda b, c: (b, c, 34)),
            pl.BlockSpec((LANES, hk), lambda b, c: (0, 0)),
            pl.BlockSpec((1, hk), lambda b, c: (0, 0)),
        ],
        out_specs=pl.BlockSpec((None, tc, hv), lambda b, c: (b, c, 0)),
        out_shape=jax.ShapeDtypeStruct((bsz, p, hv), BF16),
        scratch_shapes=[pltpu.VMEM((B_HEADS // 2, 2 * B_DV, 2 * B_DK), F32)],
        compiler_params=_params(("parallel", "arbitrary")),
        name="gla",
    )(pab, pab, pab, pab, pab, w2, gb)


def _ret_kernel(q_ref, k_ref, v_ref, g_ref, cos_ref, sin_ref, dmat_ref, zeta_ref, xi_ref,
                cdec_ref, o_ref, st_ref, *, nchunk):
    C = CHUNK
    T = LANES
    W = C_HEADS * C_DK

    @pl.when(pl.program_id(1) == 0)
    def _():
        st_ref[...] = jnp.zeros(st_ref.shape, F32)

    lane_w = lax.broadcasted_iota(I32, (C, W), 1)
    first_half = (lane_w & (C_DK - 1)) < (C_DK // 2)
    lane = lax.broadcasted_iota(I32, (C, T), 1)
    sd_r = lax.broadcasted_iota(I32, (2 * C_DV, T), 0)
    sd_c = lax.broadcasted_iota(I32, (2 * C_DV, T), 1)
    blockdiag = (sd_r >= C_DV) == (sd_c >= C_DK)

    def rotate(x, cos, sin_signed):
        swapped = jnp.where(first_half, pltpu.roll(x, W - C_DK // 2, 1), pltpu.roll(x, C_DK // 2, 1))
        return x * cos + swapped * sin_signed

    pairs = range(C_HEADS // 2)
    lanes = [slice(hp * T, (hp + 1) * T) for hp in pairs]
    G = math.gcd(nchunk, CHUNK_GROUP)

    def group(gi, carry):
        rows = [pl.ds(pl.multiple_of((gi * G + c) * C, C), C) for c in range(G)]
        q_b, k_b, q_x, k_z = [], [], [], []
        for r in rows:
            cos = cos_ref[r, :]
            sin = sin_ref[r, :]
            q = rotate(q_ref[r, :].astype(F32), cos, sin)
            k = rotate(k_ref[r, :].astype(F32), cos, sin) * (C_DK ** -0.5)
            q_b.append(q.astype(BF16))
            k_b.append(k.astype(BF16))
            q_x.append((q * xi_ref[...]).astype(BF16))
            k_z.append((k * zeta_ref[...]).astype(BF16))
        att = {}
        for c in range(G):
            for h in range(C_HEADS):
                ls = lanes[h // 2]
                mine = (lane < C_DK) if h % 2 == 0 else (lane >= C_DK)
                qm = jnp.where(mine, q_b[c][:, ls], jnp.zeros((C, T), BF16))
                s = lax.dot_general(qm, k_b[c][:, ls], NT_DIMS, preferred_element_type=F32)
                att[c, h] = (s * dmat_ref[h]).astype(BF16)
        contrib = {}
        for c in range(G):
            for hp in pairs:
                vp = v_ref[rows[c], hp * 2 * C_DV:(hp + 1) * 2 * C_DV]
                raw = lax.dot_general(vp, k_z[c][:, lanes[hp]], TN_DIMS, preferred_element_type=F32)
                contrib[c, hp] = jnp.where(blockdiag, raw, 0.0)
        before = {}
        for hp in pairs:
            st = st_ref[hp]
            for c in range(G):
                before[c, hp] = st
                st = st * cdec_ref[:, lanes[hp]] + contrib[c, hp]
            st_ref[hp] = st
        for c in range(G):
            for hp in pairs:
                o_inter = lax.dot_general(q_x[c][:, lanes[hp]], before[c, hp].astype(BF16), NT_DIMS,
                                          preferred_element_type=F32)
                for hh in range(2):
                    h = 2 * hp + hh
                    vs = slice(h * C_DV, (h + 1) * C_DV)
                    o = jnp.dot(att[c, h], v_ref[rows[c], vs], preferred_element_type=F32)
                    o = o + o_inter[:, hh * C_DV:(hh + 1) * C_DV]
                    o = o - jnp.mean(o, axis=-1, keepdims=True)
                    o = o * lax.rsqrt(jnp.mean(o * o, axis=-1, keepdims=True) + NORM_EPS)
                    o = o * _silu(g_ref[rows[c], vs].astype(F32))
                    o_ref[rows[c], vs] = o.astype(o_ref.dtype)
        return carry

    lax.fori_loop(0, nchunk // G, group, 0)


def _retention_tables(p):
    log_gamma = np.log(1.0 - np.exp2(-5.0 - np.arange(C_HEADS, dtype=np.float64)))
    i = np.arange(CHUNK, dtype=np.float64)
    diff = i[:, None] - i[None, :]
    dmat = np.where(diff >= 0, np.exp(log_gamma[:, None, None] * np.maximum(diff, 0.0)), 0.0)
    zeta = np.exp(log_gamma[:, None] * (CHUNK - 1 - i))
    xi = np.exp(log_gamma[:, None] * (i + 1))
    cdec = np.exp(log_gamma * CHUNK)
    widen = lambda t: np.repeat(t.T[:, :, None], C_DK, axis=2).reshape(CHUNK, C_HEADS * C_DK)
    half = C_DK // 2
    inv = jnp.asarray(ROPE_BASE, F32) ** (-jnp.arange(half, dtype=F32) / half)
    ang = jnp.arange(p, dtype=jnp.int32).astype(F32)[:, None] * inv[None, :]
    cos = jnp.tile(jnp.cos(ang), (1, 2 * C_HEADS))
    sin = jnp.sin(ang)
    sin_signed = jnp.tile(jnp.concatenate([-sin, sin], axis=1), (1, C_HEADS))
    return (cos, sin_signed, jnp.asarray(dmat, F32), jnp.asarray(widen(zeta), F32),
            jnp.asarray(widen(xi), F32),
            jnp.asarray(np.repeat(cdec, C_DK)[None, :], F32))


def _retention(pcd, tables):
    bsz, p, _ = pcd.shape
    cos, sin, dmat, zeta, xi, cdec = tables
    tc = _pick_tile(p, (640, 128))
    hv = C_HEADS * C_DV
    hk = C_HEADS * C_DK
    full = lambda shape: pl.BlockSpec(shape, lambda b, c: (0,) * len(shape))
    return pl.pallas_call(
        functools.partial(_ret_kernel, nchunk=tc // CHUNK),
        grid=(bsz, p // tc),
        in_specs=[
            pl.BlockSpec((None, tc, hk), lambda b, c: (b, c, 0)),
            pl.BlockSpec((None, tc, hk), lambda b, c: (b, c, 1)),
            pl.BlockSpec((None, tc, hv), lambda b, c: (b, c, 1)),
            pl.BlockSpec((None, tc, hv), lambda b, c: (b, c, 2)),
            pl.BlockSpec((tc, hk), lambda b, c: (c, 0)),
            pl.BlockSpec((tc, hk), lambda b, c: (c, 0)),
            full((C_HEADS, CHUNK, CHUNK)),
            full((CHUNK, hk)),
            full((CHUNK, hk)),
            full((1, hk)),
        ],
        out_specs=pl.BlockSpec((None, tc, hv), lambda b, c: (b, c, 0)),
        out_shape=jax.ShapeDtypeStruct((bsz, p, hv), BF16),
        scratch_shapes=[pltpu.VMEM((C_HEADS // 2, 2 * C_DV, 2 * C_DK), F32)],
        compiler_params=_params(("parallel", "arbitrary")),
        name="retention",
    )(pcd, pcd, pcd, pcd, cos, sin, dmat, zeta, xi, cdec)


def _sb_kernel(q_ref, g_ref, k_ref, v_ref, o_ref, qm_ref, run_ref, acc_ref, hi_ref, lo_ref, ls_ref,
               w_ref):
    T = LANES
    i = pl.program_id(1)
    row = lax.broadcasted_iota(I32, (T, T), 0)
    col = lax.broadcasted_iota(I32, (T, T), 1)
    qpos = i * T + row
    low = col < D_DIM
    u_r = lax.broadcasted_iota(I32, (T, 2 * T), 0)
    u_c = lax.broadcasted_iota(I32, (T, 2 * T), 1)
    suffix = jnp.where((u_c >= T) | (u_r > u_c), 1.0, 0.0).astype(BF16)

    for h in range(D_HEADS):
        pair = q_ref[:, (h // 2) * T:(h // 2 + 1) * T]
        mine = low if h % 2 == 0 else ~low
        qm_ref[h] = jnp.where(mine, pair * (D_DIM ** -0.5), jnp.zeros_like(pair))
    run_ref[...] = jnp.zeros(run_ref.shape, F32)
    acc_ref[...] = jnp.zeros(acc_ref.shape, F32)

    def walk(t0, count):
        kts = [i - t0 - j for j in range(count)]
        rows = [pl.ds(pl.multiple_of(jnp.maximum(kt, 0) * T, T), T) for kt in kts]
        for j, kt in enumerate(kts):
            kpos = kt * T + col
            ok = (kpos < qpos) & (kpos >= PAD_FRONT)
            for h in range(D_HEADS):
                ps = slice((h // 2) * T, (h // 2 + 1) * T)
                z = lax.dot_general(qm_ref[h], k_ref[rows[j], ps], NT_DIMS, preferred_element_type=F32)
                sp, logsig = _softplus_parts(z)
                log_1m = jnp.where(ok, -sp, 0.0)
                hi = lax.bitcast_convert_type(lax.bitcast_convert_type(log_1m, I32) & BF16_BITS, F32)
                hi_ref[j, h] = hi.astype(BF16)
                lo_ref[j, h] = (log_1m - hi).astype(BF16)
                ls_ref[j, h] = jnp.where(ok, logsig, NEG_INF)
        sums = {}
        for j in range(count):
            for h in range(D_HEADS):
                sums[j, h] = (jnp.dot(hi_ref[j, h], suffix, preferred_element_type=F32)
                              + jnp.dot(lo_ref[j, h], suffix, preferred_element_type=F32))
        slowest = None
        for h in range(D_HEADS):
            run = run_ref[h]
            for j in range(count):
                w_ref[j, h] = jnp.exp(ls_ref[j, h] + run + sums[j, h][:, :T]).astype(BF16)
                run = run + sums[j, h][:, T:]
            run_ref[h] = run
            slowest = run if slowest is None else jnp.maximum(slowest, run)
        for h in range(D_HEADS):
            ps = slice((h // 2) * T, (h // 2 + 1) * T)
            acc = acc_ref[h]
            for j in range(count):
                acc = acc + jnp.dot(w_ref[j, h], v_ref[rows[j], ps], preferred_element_type=F32)
            acc_ref[h] = acc
        return (jnp.max(slowest) > LOG_F32_UNDERFLOW).astype(I32)

    def one_more(state):
        t, _ = state
        return t + 1, walk(t, 1)

    lax.while_loop(lambda s: (s[0] <= i) & (s[1] > 0), one_more, (jnp.int32(SB_FIRST), walk(0, SB_FIRST)))

    for hp in range(D_HEADS // 2):
        ps = slice(hp * T, (hp + 1) * T)
        o = jnp.where(low, acc_ref[2 * hp], acc_ref[2 * hp + 1]) * _silu(g_ref[:, ps].astype(F32))
        o_ref[:, ps] = o.astype(o_ref.dtype)


def _stick_breaking(pcd):
    bsz, p, _ = pcd.shape
    T = LANES
    hw = D_HEADS * D_DIM
    one = pl.Buffered(1)
    return pl.pallas_call(
        _sb_kernel,
        grid=(bsz, p // T),
        in_specs=[
            pl.BlockSpec((None, T, hw), lambda b, i: (b, i, 3)),
            pl.BlockSpec((None, T, hw), lambda b, i: (b, i, 6)),
            pl.BlockSpec((None, p, hw), lambda b, i: (b, 0, 4), pipeline_mode=one),
            pl.BlockSpec((None, p, hw), lambda b, i: (b, 0, 5), pipeline_mode=one),
        ],
        out_specs=pl.BlockSpec((None, T, hw), lambda b, i: (b, i, 0)),
        out_shape=jax.ShapeDtypeStruct((bsz, p, hw), BF16),
        scratch_shapes=[
            pltpu.VMEM((D_HEADS, T, T), BF16),
            pltpu.VMEM((D_HEADS, T, T), F32),
            pltpu.VMEM((D_HEADS, T, T), F32),
            pltpu.VMEM((SB_FIRST, D_HEADS, T, T), BF16),
            pltpu.VMEM((SB_FIRST, D_HEADS, T, T), BF16),
            pltpu.VMEM((SB_FIRST, D_HEADS, T, T), F32),
            pltpu.VMEM((SB_FIRST, D_HEADS, T, T), BF16),
        ],
        compiler_params=_params(("parallel", "arbitrary")),
        name="stick_breaking",
    )(pcd, pcd, pcd, pcd)


def _rel_bucket_np(rel):
    half = REL_BUCKETS // 2
    max_exact = half // 2
    n = -rel
    ret = np.where(n < 0, half, 0)
    n = np.abs(n)
    edges = [math.ceil(max_exact * (REL_MAX_DIST / max_exact) ** (j / (half - max_exact)) - 1e-9)
             for j in range(1, half - max_exact)]
    large = max_exact + sum((n >= e).astype(np.int64) for e in edges)
    return ret + np.where(n < max_exact, n, large)


def _bias_tiles(rel_bias):
    i = np.arange(LANES)[:, None]
    j = np.arange(LANES)[None, :]
    idx = np.stack([_rel_bucket_np(j - i - LANES * d) for d in range(3)])
    assert (idx[2] == idx[2, 0, 0]).all()
    table = rel_bias.astype(F32) - rel_bias.astype(F32)[idx[2, 0, 0]]
    near = jnp.asarray(idx[:2], I32)[None]
    tiles = jnp.zeros((table.shape[1],) + near.shape[1:], F32)
    for bucket in range(REL_BUCKETS):
        tiles = jnp.where(near == bucket, table[bucket][:, None, None, None], tiles)
    return tiles


def _layout_ab(w):
    aq, ak, av, ag, iq, ik, iw, bq, bk, bv, bg, ba = jnp.split(w, np.cumsum(SPLIT_AB)[:-1].tolist(), axis=1)
    pad = lambda t: jnp.pad(t, ((0, 0), (0, LANES - t.shape[1])))
    cols = [aq, ak, av, ag, iq, bv, bg, bq, bk, jnp.concatenate([ik, ik], axis=1), pad(iw), pad(ba),
            jnp.zeros((w.shape[0], W_AB_PAD - W_AB_USED), w.dtype)]
    return jnp.concatenate(cols, axis=1).astype(BF16)


def kernel(x, meta_tokens, rel_bias, norm_g, final_g, w_in_ab, gla_gate_w2, gla_gate_b, w_out_ab,
           w_in_cd, w_out_cd):
    bsz, seq, d = x.shape
    p = seq + PAD_FRONT + N_META
    depth = norm_g.shape[0]
    topk = min(TOPK_MAX, seq // 4)
    h = jnp.concatenate([jnp.zeros((bsz, PAD_FRONT, d), x.dtype),
                         jnp.broadcast_to(meta_tokens.astype(x.dtype)[None], (bsz, N_META, d)),
                         x], axis=1)
    bias_tiles = _bias_tiles(rel_bias)
    tables = _retention_tables(p)
    assert depth >= 1
    hn = None
    for layer in range(depth):
        j = layer // 2
        w_in = _layout_ab(w_in_ab[j]) if layer % 2 == 0 else w_in_cd[j].astype(BF16)
        proj = _inproj(h, norm_g[layer], w_in) if hn is None else _proj(hn, w_in)
        if layer % 2 == 0:
            first = _dsa(proj, bias_tiles, topk)
            w2 = jnp.pad(gla_gate_w2[j], ((0, LANES - GLA_GATE_RANK), (0, 0))).astype(BF16)
            second = _gla(proj, w2, gla_gate_b[j].reshape(1, -1).astype(F32))
            w_out = w_out_ab[j]
        else:
            first = _retention(proj, tables)
            second = _stick_breaking(proj)
            w_out = w_out_cd[j]
        if layer + 1 < depth:
            h, hn = _outproj(first, second, w_out, h, norm_g[layer + 1])
    return _outproj_norm(first, second, w_out, h, final_g, seq)
```

```python
import functools
import math

import numpy as np
import jax
import jax.numpy as jnp
from jax import lax
from jax.experimental import pallas as pl
from jax.experimental.pallas import tpu as pltpu

F32 = jnp.float32
BF16 = jnp.bfloat16
I32 = jnp.int32

CHUNK = 64
CHUNK_SHIFT = CHUNK.bit_length() - 1
N_META = 16
PAD_FRONT = 128 - N_META
NORM_EPS = 1e-6
NEG_INF = -1e30
A_HEADS, A_DIM = 8, 64
IDX_HEADS, IDX_DIM = 8, 64
IDX_SCALE = (IDX_DIM ** -0.5) * (IDX_HEADS ** -0.5)
TOPK_MAX = 256
B_HEADS, B_DK, B_DV = 4, 64, 128
GLA_GATE_RANK = 16
GLA_TAU = 16.0
C_HEADS, C_DK, C_DV = 4, 64, 128
ROPE_BASE = 10000.0
D_HEADS, D_DIM = 8, 64
REL_BUCKETS = 32
REL_MAX_DIST = 128
SPLIT_AB = (512, 512, 512, 512, 512, 64, 8, 256, 256, 512, 512, 16)

LANES = 128
W_AB_USED = 4480
W_AB_PAD = 4608
I32_MIN = -2 ** 31
DUMMY = 7
IDX_BITS = 14
SETTLE_FROM_BIT = 22
LOG_F32_UNDERFLOW = -104.0
LOGIT_LIMIT = 40.0
NORM_SLACK = 1.05
TINY = 1e-30
BF16_BITS = -65536
CHUNK_GROUP = 10
SB_FIRST = 3
VMEM_LIMIT = 56 * 1024 * 1024

NT_DIMS = (((1,), (1,)), ((), ()))
TN_DIMS = (((0,), (0,)), ((), ()))


def _pick_tile(n, candidates):
    for c in candidates:
        if n % c == 0:
            return c
    raise ValueError(f"no tile for {n}")


def _params(sem):
    return pltpu.CompilerParams(dimension_semantics=sem, vmem_limit_bytes=VMEM_LIMIT)


def _silu(x):
    return x / (1.0 + jnp.exp(-x))


def _softplus_parts(z):
    t = jnp.log(1.0 + jnp.exp(-jnp.abs(z)))
    return jnp.maximum(z, 0.0) + t, jnp.minimum(z, 0.0) - t


def _split3(x):
    a = x.astype(BF16)
    r = x - a.astype(F32)
    b = r.astype(BF16)
    c = (r - b.astype(F32)).astype(BF16)
    return a, b, c


def _inproj_kernel(x_ref, g_ref, w_ref, o_ref, hn_ref):
    @pl.when(pl.program_id(2) == 0)
    def _():
        x = x_ref[...]
        ms = jnp.mean(x * x, axis=-1, keepdims=True)
        hn_ref[...] = (x * lax.rsqrt(ms + NORM_EPS) * g_ref[...]).astype(BF16)

    o_ref[...] = jnp.dot(hn_ref[...], w_ref[...], preferred_element_type=F32).astype(o_ref.dtype)


def _inproj(h, g, w):
    bsz, p, d = h.shape
    n = w.shape[1]
    tp = _pick_tile(p, (1664, 1280, 640, 128))
    tn = _pick_tile(n, (1792, 1536, 896, 512, 128))
    return pl.pallas_call(
        _inproj_kernel,
        grid=(bsz, p // tp, n // tn),
        in_specs=[
            pl.BlockSpec((None, tp, d), lambda b, i, j: (b, i, 0)),
            pl.BlockSpec((1, d), lambda b, i, j: (0, 0)),
            pl.BlockSpec((d, tn), lambda b, i, j: (0, j)),
        ],
        out_specs=pl.BlockSpec((None, tp, tn), lambda b, i, j: (b, i, j)),
        out_shape=jax.ShapeDtypeStruct((bsz, p, n), BF16),
        scratch_shapes=[pltpu.VMEM((tp, d), BF16)],
        compiler_params=_params(("parallel", "parallel", "arbitrary")),
        name="inproj",
    )(h, g.reshape(1, d), w)


def _proj_kernel(x_ref, w_ref, o_ref):
    o_ref[...] = jnp.dot(x_ref[...], w_ref[...], preferred_element_type=F32).astype(o_ref.dtype)


def _proj(hn, w):
    bsz, p, d = hn.shape
    n = w.shape[1]
    tp = _pick_tile(p, (1664, 1280, 640, 128))
    tn = _pick_tile(n, (1792, 1536, 896, 512, 128))
    return pl.pallas_call(
        _proj_kernel,
        grid=(bsz, p // tp, n // tn),
        in_specs=[
            pl.BlockSpec((None, tp, d), lambda b, i, j: (b, i, 0)),
            pl.BlockSpec((d, tn), lambda b, i, j: (0, j)),
        ],
        out_specs=pl.BlockSpec((None, tp, tn), lambda b, i, j: (b, i, j)),
        out_shape=jax.ShapeDtypeStruct((bsz, p, n), BF16),
        compiler_params=_params(("parallel", "parallel", "parallel")),
        name="proj",
    )(hn, w)


def _outproj_kernel(m1_ref, m2_ref, w1_ref, w2_ref, h_ref, g_ref, o_ref, hn_ref, *, tp):
    y = jnp.dot(m1_ref[...], w1_ref[...], preferred_element_type=F32)
    y = y + jnp.dot(m2_ref[...], w2_ref[...], preferred_element_type=F32)
    pos = pl.program_id(1) * tp + lax.broadcasted_iota(I32, y.shape, 0)
    x = h_ref[...] + jnp.where(pos >= PAD_FRONT, y, 0.0)
    o_ref[...] = x
    ms = jnp.mean(x * x, axis=-1, keepdims=True)
    hn_ref[...] = (x * lax.rsqrt(ms + NORM_EPS) * g_ref[...]).astype(BF16)


def _outproj(m1, m2, w_out, h, g_next):
    bsz, p, d = h.shape
    k1 = m1.shape[-1]
    k2 = m2.shape[-1]
    tp = _pick_tile(p, (832, 640, 128))
    w = w_out.astype(BF16)
    return pl.pallas_call(
        functools.partial(_outproj_kernel, tp=tp),
        grid=(bsz, p // tp),
        in_specs=[
            pl.BlockSpec((None, tp, k1), lambda b, i: (b, i, 0)),
            pl.BlockSpec((None, tp, k2), lambda b, i: (b, i, 0)),
            pl.BlockSpec((k1, d), lambda b, i: (0, 0)),
            pl.BlockSpec((k2, d), lambda b, i: (0, 0)),
            pl.BlockSpec((None, tp, d), lambda b, i: (b, i, 0)),
            pl.BlockSpec((1, d), lambda b, i: (0, 0)),
        ],
        out_specs=[pl.BlockSpec((None, tp, d), lambda b, i: (b, i, 0)),
                   pl.BlockSpec((None, tp, d), lambda b, i: (b, i, 0))],
        out_shape=[jax.ShapeDtypeStruct((bsz, p, d), F32), jax.ShapeDtypeStruct((bsz, p, d), BF16)],
        compiler_params=_params(("parallel", "parallel")),
        name="outproj",
    )(m1, m2, w[:k1], w[k1:], h, g_next.reshape(1, d))


def _outproj_norm_kernel(m1_ref, m2_ref, w1_ref, w2_ref, h_ref, g_ref, o_ref):
    y = jnp.dot(m1_ref[...], w1_ref[...], preferred_element_type=F32)
    y = y + jnp.dot(m2_ref[...], w2_ref[...], preferred_element_type=F32)
    x = h_ref[...] + y
    ms = jnp.mean(x * x, axis=-1, keepdims=True)
    o_ref[...] = x * lax.rsqrt(ms + NORM_EPS) * g_ref[...]


def _outproj_norm(m1, m2, w_out, h, g, seq):
    bsz, p, d = h.shape
    k1 = m1.shape[-1]
    k2 = m2.shape[-1]
    tp = LANES
    skip = (p - seq) // tp
    assert skip * tp == p - seq
    w = w_out.astype(BF16)
    rows = lambda b, i: (b, i + skip, 0)
    return pl.pallas_call(
        _outproj_norm_kernel,
        grid=(bsz, seq // tp),
        in_specs=[
            pl.BlockSpec((None, tp, k1), rows),
            pl.BlockSpec((None, tp, k2), rows),
            pl.BlockSpec((k1, d), lambda b, i: (0, 0)),
            pl.BlockSpec((k2, d), lambda b, i: (0, 0)),
            pl.BlockSpec((None, tp, d), rows),
            pl.BlockSpec((1, d), lambda b, i: (0, 0)),
        ],
        out_specs=pl.BlockSpec((None, tp, d), lambda b, i: (b, i, 0)),
        out_shape=jax.ShapeDtypeStruct((bsz, seq, d), F32),
        compiler_params=_params(("parallel", "parallel")),
        name="outproj_norm",
    )(m1, m2, w[:k1], w[k1:], h, g.reshape(1, d))


def _dsa_kernel(aq_ref, ag_ref, iq_ref, iw_ref, k_ref, v_ref, ik_ref, bias_ref, o_ref,
                key_ref, sa_ref, sb_ref, iqs_ref, wts_ref, qm_ref,
                jdx_ref, kmax_ref, mx_ref, ls_ref, acc_ref, *, topk):
    T = LANES
    i = pl.program_id(1)
    last = i + DUMMY

    def num_groups(group):
        return (i + group) // group

    def first_slot(group):
        return last + 1 - group * num_groups(group)

    row = lax.broadcasted_iota(I32, (T, T), 0)
    col = lax.broadcasted_iota(I32, (T, T), 1)
    qchunk = (i * T + row) >> CHUNK_SHIFT
    low = col < A_DIM

    def admissible(kt):
        kpos = kt * T + col
        return (kpos >= PAD_FRONT) & ((kpos >> CHUNK_SHIFT) <= qchunk)

    def tile_rows(slot):
        return pl.ds(pl.multiple_of(jnp.maximum(slot - DUMMY, 0) * T, T), T)

    def for_groups(group, body):
        first = first_slot(group)

        def step(g, carry):
            body(first + group * g)
            return carry

        lax.fori_loop(0, num_groups(group), step, 0)

    iw = iw_ref[...].astype(F32)
    for h in range(IDX_HEADS):
        pair = iq_ref[:, (h // 2) * T:(h // 2 + 1) * T]
        mine = low if h % 2 == 0 else ~low
        iqs_ref[0, h * T:(h + 1) * T, :] = jnp.where(mine, pair, jnp.zeros_like(pair))
        wts_ref[h] = jnp.broadcast_to(iw[:, h:h + 1], (T, T))
    for h in range(A_HEADS):
        pair = aq_ref[:, (h // 2) * T:(h // 2 + 1) * T]
        mine = low if h % 2 == 0 else ~low
        qm_ref[h // 2, (h % 2) * T:(h % 2 + 1) * T, :] = jnp.where(
            mine, pair * (A_DIM ** -0.5), jnp.zeros_like(pair))

    def sweep(lhs_ref, rhs_ref, consume, consume_last):
        group_rows = lhs_ref.shape[1]

        def issue(s0, stage):
            rows = [tile_rows(s0), tile_rows(s0 + 1)]
            for g in range(lhs_ref.shape[0]):
                ps = slice(g * T, (g + 1) * T)
                kw = jnp.concatenate([rhs_ref[rows[0], ps], rhs_ref[rows[1], ps]], axis=0)
                stage[g * group_rows:(g + 1) * group_rows, :] = lax.dot_general(
                    lhs_ref[g], kw, NT_DIMS, preferred_element_type=F32)

        first = first_slot(4)
        issue(first, sa_ref)

        def quad(s):
            issue(s + 2, sb_ref)
            consume(s, sa_ref)
            issue(s + 4, sa_ref)
            consume(s + 2, sb_ref)

        inner = num_groups(4) - 1

        @pl.when(inner % 2 == 1)
        def _():
            quad(first)

        def body(q, carry):
            s = first + 4 * (inner % 2) + 8 * q
            quad(s)
            quad(s + 4)
            return carry

        lax.fori_loop(0, inner // 2, body, 0)
        issue(last - 1, sb_ref)
        consume(last - 3, sa_ref)
        consume_last(last - 1, sb_ref)

    def score_pair(s0, stage):
        for u in range(2):
            tot = jnp.zeros((T, T), F32)
            for h in range(IDX_HEADS):
                tot = tot + jnp.maximum(stage[h * T:(h + 1) * T, u * T:(u + 1) * T], 0.0) * wts_ref[h]
            score = jnp.where(admissible(s0 + u - DUMMY), tot * IDX_SCALE, NEG_INF)
            bits = lax.bitcast_convert_type(score, I32)
            key_ref[s0 + u] = bits ^ ((bits >> 31) & 0x7FFFFFFF)

    sweep(iqs_ref, ik_ref, score_pair, score_pair)
    for d in range(DUMMY):
        key_ref[d] = jnp.full((T, T), I32_MIN, I32)

    def row_count(pred):
        one = jnp.ones((T, T), I32)
        zero = jnp.zeros((T, T), I32)
        first = first_slot(8)

        def body(g, acc):
            hits = [jnp.where(pred(first + 8 * g + u), one, zero) for u in range(8)]
            while len(hits) > 1:
                hits = [a + b for a, b in zip(hits[::2], hits[1::2])]
            return acc + hits[0]

        acc = lax.fori_loop(0, num_groups(8), body, zero)
        return jnp.sum(acc.astype(F32), axis=-1, keepdims=True)

    def value_bit(it, state):
        t, reach = state
        cand = t + (jnp.int32(1) << (31 - it))
        wide = jnp.broadcast_to(cand, (T, T))
        cnt = row_count(lambda s: key_ref[s] >= wide)
        fits = cnt >= topk
        return jnp.where(fits, cand, t), jnp.where(fits, cnt, reach)

    everything = (8 * T * num_groups(8)).astype(F32)
    state = lax.fori_loop(0, SETTLE_FROM_BIT, value_bit, (jnp.full((T, 1), I32_MIN, I32),
                                                          jnp.full((T, 1), everything, F32)))

    wide = jnp.broadcast_to(state[0], (T, T))
    final = row_count(lambda s: key_ref[s] > wide) < topk

    def unsettled(carry):
        it, (_, reach) = carry
        return (it < 32) & (jnp.max(jnp.where(final, topk, reach)) > topk)

    def two_bits(carry):
        it, state = carry
        return it + 2, value_bit(it + 1, value_bit(it, state))

    _, (thr, reach) = lax.while_loop(unsettled, two_bits, (jnp.int32(SETTLE_FROM_BIT), state))
    thr = jnp.broadcast_to(thr, (T, T))
    jdx_ref[...] = jnp.full((T, T), 2 ** IDX_BITS, I32)

    @pl.when(jnp.max(reach) > topk)
    def _():
        need = topk - (reach - row_count(lambda s: key_ref[s] == thr))
        p_r = lax.broadcasted_iota(I32, (T, 2 * T), 0)
        p_c = lax.broadcasted_iota(I32, (T, 2 * T), 1)
        prefix_ones = jnp.where((p_c >= T) | (p_r <= p_c), 1.0, 0.0).astype(BF16)
        first = first_slot(8)

        def body(g, state):
            seen, best = state
            slots = [first + 8 * g + u for u in range(8)]
            tied = [key_ref[s] == thr for s in slots]
            sums = [jnp.dot(jnp.where(t, 1.0, 0.0).astype(BF16), prefix_ones,
                            preferred_element_type=F32) for t in tied]
            for s, t, sm in zip(slots, tied, sums):
                keep = t & (seen + sm[:, :T] <= need)
                best = jnp.maximum(best, jnp.where(keep, (s - DUMMY) * T + col, -1))
                seen = seen + sm[:, T:]
            return seen, best

        _, best = lax.fori_loop(0, num_groups(8), body,
                                (jnp.zeros((T, T), F32), jnp.full((T, T), -1, I32)))
        jdx_ref[...] = jnp.broadcast_to(jnp.max(best, axis=-1, keepdims=True), (T, T))

    jdx = jdx_ref[...]

    def selection_mask(s):
        key = key_ref[s]
        sel = (key > thr) | ((key == thr) & ((s - DUMMY) * T + col <= jdx))
        return jnp.where(sel & admissible(s - DUMMY), 0.0, NEG_INF)

    head_of_lane = (lax.broadcasted_iota(I32, (A_HEADS * A_DIM, T), 0) // A_DIM
                    == lax.broadcasted_iota(I32, (A_HEADS * A_DIM, T), 1))
    head_sum = jnp.where(head_of_lane, 1.0, 0.0).astype(BF16)

    def head_norms2(t):
        t = t.astype(F32)
        return jnp.dot((t * t).astype(BF16), head_sum, preferred_element_type=F32)

    @pl.when(i == 0)
    def _():
        def widest(t, best):
            return jnp.maximum(best, head_norms2(k_ref[pl.ds(pl.multiple_of(t * T, T), T), :]))
        best = lax.fori_loop(0, k_ref.shape[0] // T, widest, jnp.zeros((T, T), F32))
        kmax_ref[...] = jnp.broadcast_to(jnp.max(best, axis=0, keepdims=True), kmax_ref.shape)

    bound2 = head_norms2(aq_ref[...]) * kmax_ref[0:1, :] * (NORM_SLACK / A_DIM)
    room = LOGIT_LIMIT - jnp.max(jnp.abs(bias_ref[...]))
    small_logits = (room > 0.0) & (jnp.max(bound2) <= room * room)

    ls_ref[...] = jnp.zeros(ls_ref.shape, F32)
    acc_ref[...] = jnp.zeros(acc_ref.shape, F32)

    def attend(s0, stage, near, final):
        madd = [selection_mask(s0), selection_mask(s0 + 1)]
        for hp in range(A_HEADS // 2):
            weights = []
            for h in (2 * hp, 2 * hp + 1):
                parts = [stage[h * T:(h + 1) * T, u * T:(u + 1) * T] + madd[u] for u in range(2)]
                if near:
                    parts = [parts[u] + bias_ref[h, 1 - u] for u in range(2)]
                if not final:
                    mx_ref[h] = jnp.maximum(mx_ref[h], jnp.maximum(parts[0], parts[1]))
                else:
                    es = [jnp.exp(part - mx_ref[h]) for part in parts]
                    ls_ref[h] = ls_ref[h] + (es[0] + es[1])
                    weights.append(jnp.concatenate(es, axis=1).astype(BF16))
            if final:
                ps = slice(hp * T, (hp + 1) * T)
                vw = jnp.concatenate([v_ref[tile_rows(s0), ps], v_ref[tile_rows(s0 + 1), ps]], axis=0)
                acc_ref[hp] = acc_ref[hp] + jnp.dot(jnp.concatenate(weights, axis=0), vw,
                                                    preferred_element_type=F32)

    def attention_sweep(final):
        sweep(qm_ref, k_ref,
              functools.partial(attend, near=False, final=final),
              functools.partial(attend, near=True, final=final))

    @pl.when(small_logits)
    def _():
        mx_ref[...] = jnp.zeros(mx_ref.shape, F32)

    @pl.when(jnp.logical_not(small_logits))
    def _():
        mx_ref[...] = jnp.full(mx_ref.shape, NEG_INF, F32)
        attention_sweep(False)
        for h in range(A_HEADS):
            mx_ref[h] = jnp.broadcast_to(jnp.max(mx_ref[h], axis=-1, keepdims=True), (T, T))

    attention_sweep(True)

    def row_total(h):
        return jnp.maximum(jnp.sum(ls_ref[h], axis=-1, keepdims=True), TINY)

    for hp in range(A_HEADS // 2):
        o0 = acc_ref[hp, :T, :] / row_total(2 * hp)
        o1 = acc_ref[hp, T:, :] / row_total(2 * hp + 1)
        g = ag_ref[:, hp * T:(hp + 1) * T].astype(F32)
        o_ref[:, hp * T:(hp + 1) * T] = (jnp.where(low, o0, o1) * _silu(g)).astype(o_ref.dtype)


def _dsa(pab, bias_tiles, topk):
    bsz, p, _ = pab.shape
    T = LANES
    nq = p // T
    hw = A_HEADS * A_DIM
    one = pl.Buffered(1)
    return pl.pallas_call(
        functools.partial(_dsa_kernel, topk=float(topk)),
        grid=(bsz, nq),
        in_specs=[
            pl.BlockSpec((None, T, hw), lambda b, i: (b, i, 0)),
            pl.BlockSpec((None, T, hw), lambda b, i: (b, i, 3)),
            pl.BlockSpec((None, T, hw), lambda b, i: (b, i, 4)),
            pl.BlockSpec((None, T, T), lambda b, i: (b, i, 33)),
            pl.BlockSpec((None, p, hw), lambda b, i: (b, 0, 1), pipeline_mode=one),
            pl.BlockSpec((None, p, hw), lambda b, i: (b, 0, 2), pipeline_mode=one),
            pl.BlockSpec((None, p, T), lambda b, i: (b, 0, 32), pipeline_mode=one),
            pl.BlockSpec((A_HEADS, 2, T, T), lambda b, i: (0, 0, 0, 0), pipeline_mode=one),
        ],
        out_specs=pl.BlockSpec((None, T, hw), lambda b, i: (b, i, 0)),
        out_shape=jax.ShapeDtypeStruct((bsz, p, hw), BF16),
        scratch_shapes=[
            pltpu.VMEM((nq + DUMMY, T, T), I32),
            pltpu.VMEM((A_HEADS * T, 2 * T), F32),
            pltpu.VMEM((A_HEADS * T, 2 * T), F32),
            pltpu.VMEM((1, IDX_HEADS * T, T), BF16),
            pltpu.VMEM((IDX_HEADS, T, T), F32),
            pltpu.VMEM((A_HEADS // 2, 2 * T, T), BF16),
            pltpu.VMEM((T, T), I32),
            pltpu.VMEM((8, T), F32),
            pltpu.VMEM((A_HEADS, T, T), F32),
            pltpu.VMEM((A_HEADS, T, T), F32),
            pltpu.VMEM((A_HEADS // 2, 2 * T, T), F32),
        ],
        compiler_params=_params(("parallel", "arbitrary")),
        name="dsa",
    )(pab, pab, pab, pab, pab, pab, pab, bias_tiles)


def _gla_kernel(q_ref, k_ref, v_ref, g_ref, a_ref, w2_ref, gb_ref, o_ref, st_ref, *, nchunk):
    C = CHUNK
    T = LANES

    @pl.when(pl.program_id(1) == 0)
    def _():
        st_ref[...] = jnp.zeros(st_ref.shape, F32)

    r_i = lax.broadcasted_iota(I32, (C, C), 0)
    c_i = lax.broadcasted_iota(I32, (C, C), 1)
    causal = c_i <= r_i
    tri = jnp.where(causal, 1.0, 0.0).astype(BF16)
    lane = lax.broadcasted_iota(I32, (C, T), 1)
    sd_r = lax.broadcasted_iota(I32, (2 * B_DV, T), 0)
    sd_c = lax.broadcasted_iota(I32, (2 * B_DV, T), 1)
    blockdiag = (sd_r >= B_DV) == (sd_c >= B_DK)

    pairs = range(B_HEADS // 2)
    lanes = [slice(hp * T, (hp + 1) * T) for hp in pairs]
    G = math.gcd(nchunk, CHUNK_GROUP)

    def group(gi, carry):
        rows = [pl.ds(pl.multiple_of((gi * G + c) * C, C), C) for c in range(G)]
        log_a = [_softplus_parts(jnp.dot(a_ref[r, :], w2_ref[...], preferred_element_type=F32)
                                 + gb_ref[...])[1] * (1.0 / GLA_TAU) for r in rows]
        terms = [_split3(x) for x in log_a]
        bcum = [sum(jnp.dot(tri, t, preferred_element_type=F32) for t in ts) for ts in terms]
        q_t, k_t, k_d, decay = [], [], [], []
        for r, b in zip(rows, bcum):
            b_last = b[C - 1:C, :]
            q = q_ref[r, :].astype(F32) * (B_DK ** -0.5)
            k = k_ref[r, :].astype(F32)
            q_t.append((q * jnp.exp(b)).astype(BF16))
            k_t.append((k * jnp.exp(-b)).astype(BF16))
            k_d.append((k * jnp.exp(b_last - b)).astype(BF16))
            decay.append(jnp.exp(b_last))
        att = {}
        for c in range(G):
            for h in range(B_HEADS):
                ls = lanes[h // 2]
                mine = (lane < B_DK) if h % 2 == 0 else (lane >= B_DK)
                qm = jnp.where(mine, q_t[c][:, ls], jnp.zeros((C, T), BF16))
                s = lax.dot_general(qm, k_t[c][:, ls], NT_DIMS, preferred_element_type=F32)
                att[c, h] = jnp.where(causal, s, 0.0).astype(BF16)
        contrib = {}
        for c in range(G):
            for hp in pairs:
                vp = v_ref[rows[c], hp * 2 * B_DV:(hp + 1) * 2 * B_DV]
                raw = lax.dot_general(vp, k_d[c][:, lanes[hp]], TN_DIMS, preferred_element_type=F32)
                contrib[c, hp] = jnp.where(blockdiag, raw, 0.0)
        before = {}
        for hp in pairs:
            st = st_ref[hp]
            for c in range(G):
                before[c, hp] = st
                st = st * decay[c][:, lanes[hp]] + contrib[c, hp]
            st_ref[hp] = st
        for c in range(G):
            for hp in pairs:
                o_inter = lax.dot_general(q_t[c][:, lanes[hp]], before[c, hp].astype(BF16), NT_DIMS,
                                          preferred_element_type=F32)
                for hh in range(2):
                    h = 2 * hp + hh
                    vs = slice(h * B_DV, (h + 1) * B_DV)
                    o = jnp.dot(att[c, h], v_ref[rows[c], vs], preferred_element_type=F32)
                    o = o + o_inter[:, hh * B_DV:(hh + 1) * B_DV]
                    o = o * lax.rsqrt(jnp.mean(o * o, axis=-1, keepdims=True) + NORM_EPS)
                    o = o * _silu(g_ref[rows[c], vs].astype(F32))
                    o_ref[rows[c], vs] = o.astype(o_ref.dtype)
        return carry

    lax.fori_loop(0, nchunk // G, group, 0)


def _gla(pab, w2, gb):
    bsz, p, _ = pab.shape
    tc = _pick_tile(p, (640, 128))
    hv = B_HEADS * B_DV
    hk = B_HEADS * B_DK
    return pl.pallas_call(
        functools.partial(_gla_kernel, nchunk=tc // CHUNK),
        grid=(bsz, p // tc),
        in_specs=[
            pl.BlockSpec((None, tc, hk), lambda b, c: (b, c, 14)),
            pl.BlockSpec((None, tc, hk), lambda b, c: (b, c, 15)),
            pl.BlockSpec((None, tc, hv), lambda b, c: (b, c, 5)),
            pl.BlockSpec((None, tc, hv), lambda b, c: (b, c, 6)),
            pl.BlockSpec((None, tc, LANES), lambda b, c: (b, c, 34)),
            pl.BlockSpec((LANES, hk), lambda b, c: (0, 0)),
            pl.BlockSpec((1, hk), lambda b, c: (0, 0)),
        ],
        out_specs=pl.BlockSpec((None, tc, hv), lambda b, c: (b, c, 0)),
        out_shape=jax.ShapeDtypeStruct((bsz, p, hv), BF16),
        scratch_shapes=[pltpu.VMEM((B_HEADS // 2, 2 * B_DV, 2 * B_DK), F32)],
        compiler_params=_params(("parallel", "arbitrary")),
        name="gla",
    )(pab, pab, pab, pab, pab, w2, gb)


def _ret_kernel(q_ref, k_ref, v_ref, g_ref, cos_ref, sin_ref, dmat_ref, zeta_ref, xi_ref,
                cdec_ref, o_ref, st_ref, *, nchunk):
    C = CHUNK
    T = LANES
    W = C_HEADS * C_DK

    @pl.when(pl.program_id(1) == 0)
    def _():
        st_ref[...] = jnp.zeros(st_ref.shape, F32)

    lane_w = lax.broadcasted_iota(I32, (C, W), 1)
    first_half = (lane_w & (C_DK - 1)) < (C_DK // 2)
    lane = lax.broadcasted_iota(I32, (C, T), 1)
    sd_r = lax.broadcasted_iota(I32, (2 * C_DV, T), 0)
    sd_c = lax.broadcasted_iota(I32, (2 * C_DV, T), 1)
    blockdiag = (sd_r >= C_DV) == (sd_c >= C_DK)

    def rotate(x, cos, sin_signed):
        swapped = jnp.where(first_half, pltpu.roll(x, W - C_DK // 2, 1), pltpu.roll(x, C_DK // 2, 1))
        return x * cos + swapped * sin_signed

    pairs = range(C_HEADS // 2)
    lanes = [slice(hp * T, (hp + 1) * T) for hp in pairs]
    G = math.gcd(nchunk, CHUNK_GROUP)

    def group(gi, carry):
        rows = [pl.ds(pl.multiple_of((gi * G + c) * C, C), C) for c in range(G)]
        q_b, k_b, q_x, k_z = [], [], [], []
        for r in rows:
            cos = cos_ref[r, :]
            sin = sin_ref[r, :]
            q = rotate(q_ref[r, :].astype(F32), cos, sin)
            k = rotate(k_ref[r, :].astype(F32), cos, sin) * (C_DK ** -0.5)
            q_b.append(q.astype(BF16))
            k_b.append(k.astype(BF16))
            q_x.append((q * xi_ref[...]).astype(BF16))
            k_z.append((k * zeta_ref[...]).astype(BF16))
        att = {}
        for c in range(G):
            for h in range(C_HEADS):
                ls = lanes[h // 2]
                mine = (lane < C_DK) if h % 2 == 0 else (lane >= C_DK)
                qm = jnp.where(mine, q_b[c][:, ls], jnp.zeros((C, T), BF16))
                s = lax.dot_general(qm, k_b[c][:, ls], NT_DIMS, preferred_element_type=F32)
                att[c, h] = (s * dmat_ref[h]).astype(BF16)
        contrib = {}
        for c in range(G):
            for hp in pairs:
                vp = v_ref[rows[c], hp * 2 * C_DV:(hp + 1) * 2 * C_DV]
                raw = lax.dot_general(vp, k_z[c][:, lanes[hp]], TN_DIMS, preferred_element_type=F32)
                contrib[c, hp] = jnp.where(blockdiag, raw, 0.0)
        before = {}
        for hp in pairs:
            st = st_ref[hp]
            for c in range(G):
                before[c, hp] = st
                st = st * cdec_ref[:, lanes[hp]] + contrib[c, hp]
            st_ref[hp] = st
        for c in range(G):
            for hp in pairs:
                o_inter = lax.dot_general(q_x[c][:, lanes[hp]], before[c, hp].astype(BF16), NT_DIMS,
                                          preferred_element_type=F32)
                for hh in range(2):
                    h = 2 * hp + hh
                    vs = slice(h * C_DV, (h + 1) * C_DV)
                    o = jnp.dot(att[c, h], v_ref[rows[c], vs], preferred_element_type=F32)
                    o = o + o_inter[:, hh * C_DV:(hh + 1) * C_DV]
                    o = o - jnp.mean(o, axis=-1, keepdims=True)
                    o = o * lax.rsqrt(jnp.mean(o * o, axis=-1, keepdims=True) + NORM_EPS)
                    o = o * _silu(g_ref[rows[c], vs].astype(F32))
                    o_ref[rows[c], vs] = o.astype(o_ref.dtype)
        return carry

    lax.fori_loop(0, nchunk // G, group, 0)


def _retention_tables(p):
    log_gamma = np.log(1.0 - np.exp2(-5.0 - np.arange(C_HEADS, dtype=np.float64)))
    i = np.arange(CHUNK, dtype=np.float64)
    diff = i[:, None] - i[None, :]
    dmat = np.where(diff >= 0, np.exp(log_gamma[:, None, None] * np.maximum(diff, 0.0)), 0.0)
    zeta = np.exp(log_gamma[:, None] * (CHUNK - 1 - i))
    xi = np.exp(log_gamma[:, None] * (i + 1))
    cdec = np.exp(log_gamma * CHUNK)
    widen = lambda t: np.repeat(t.T[:, :, None], C_DK, axis=2).reshape(CHUNK, C_HEADS * C_DK)
    half = C_DK // 2
    inv = jnp.asarray(ROPE_BASE, F32) ** (-jnp.arange(half, dtype=F32) / half)
    ang = jnp.arange(p, dtype=jnp.int32).astype(F32)[:, None] * inv[None, :]
    cos = jnp.tile(jnp.cos(ang), (1, 2 * C_HEADS))
    sin = jnp.sin(ang)
    sin_signed = jnp.tile(jnp.concatenate([-sin, sin], axis=1), (1, C_HEADS))
    return (cos, sin_signed, jnp.asarray(dmat, F32), jnp.asarray(widen(zeta), F32),
            jnp.asarray(widen(xi), F32),
            jnp.asarray(np.repeat(cdec, C_DK)[None, :], F32))


def _retention(pcd, tables):
    bsz, p, _ = pcd.shape
    cos, sin, dmat, zeta, xi, cdec = tables
    tc = _pick_tile(p, (640, 128))
    hv = C_HEADS * C_DV
    hk = C_HEADS * C_DK
    full = lambda shape: pl.BlockSpec(shape, lambda b, c: (0,) * len(shape))
    return pl.pallas_call(
        functools.partial(_ret_kernel, nchunk=tc // CHUNK),
        grid=(bsz, p // tc),
        in_specs=[
            pl.BlockSpec((None, tc, hk), lambda b, c: (b, c, 0)),
            pl.BlockSpec((None, tc, hk), lambda b, c: (b, c, 1)),
            pl.BlockSpec((None, tc, hv), lambda b, c: (b, c, 1)),
            pl.BlockSpec((None, tc, hv), lambda b, c: (b, c, 2)),
            pl.BlockSpec((tc, hk), lambda b, c: (c, 0)),
            pl.BlockSpec((tc, hk), lambda b, c: (c, 0)),
            full((C_HEADS, CHUNK, CHUNK)),
            full((CHUNK, hk)),
            full((CHUNK, hk)),
            full((1, hk)),
        ],
        out_specs=pl.BlockSpec((None, tc, hv), lambda b, c: (b, c, 0)),
        out_shape=jax.ShapeDtypeStruct((bsz, p, hv), BF16),
        scratch_shapes=[pltpu.VMEM((C_HEADS // 2, 2 * C_DV, 2 * C_DK), F32)],
        compiler_params=_params(("parallel", "arbitrary")),
        name="retention",
    )(pcd, pcd, pcd, pcd, cos, sin, dmat, zeta, xi, cdec)


def _sb_kernel(q_ref, g_ref, k_ref, v_ref, o_ref, qm_ref, run_ref, acc_ref, hi_ref, lo_ref, ls_ref,
               w_ref):
    T = LANES
    i = pl.program_id(1)
    row = lax.broadcasted_iota(I32, (T, T), 0)
    col = lax.broadcasted_iota(I32, (T, T), 1)
    qpos = i * T + row
    low = col < D_DIM
    u_r = lax.broadcasted_iota(I32, (T, 2 * T), 0)
    u_c = lax.broadcasted_iota(I32, (T, 2 * T), 1)
    suffix = jnp.where((u_c >= T) | (u_r > u_c), 1.0, 0.0).astype(BF16)

    for h in range(D_HEADS):
        pair = q_ref[:, (h // 2) * T:(h // 2 + 1) * T]
        mine = low if h % 2 == 0 else ~low
        qm_ref[h] = jnp.where(mine, pair * (D_DIM ** -0.5), jnp.zeros_like(pair))
    run_ref[...] = jnp.zeros(run_ref.shape, F32)
    acc_ref[...] = jnp.zeros(acc_ref.shape, F32)

    def walk(t0, count):
        kts = [i - t0 - j for j in range(count)]
        rows = [pl.ds(pl.multiple_of(jnp.maximum(kt, 0) * T, T), T) for kt in kts]
        for j, kt in enumerate(kts):
            kpos = kt * T + col
            ok = (kpos < qpos) & (kpos >= PAD_FRONT)
            for h in range(D_HEADS):
                ps = slice((h // 2) * T, (h // 2 + 1) * T)
                z = lax.dot_general(qm_ref[h], k_ref[rows[j], ps], NT_DIMS, preferred_element_type=F32)
                sp, logsig = _softplus_parts(z)
                log_1m = jnp.where(ok, -sp, 0.0)
                hi = lax.bitcast_convert_type(lax.bitcast_convert_type(log_1m, I32) & BF16_BITS, F32)
                hi_ref[j, h] = hi.astype(BF16)
                lo_ref[j, h] = (log_1m - hi).astype(BF16)
                ls_ref[j, h] = jnp.where(ok, logsig, NEG_INF)
        sums = {}
        for j in range(count):
            for h in range(D_HEADS):
                sums[j, h] = (jnp.dot(hi_ref[j, h], suffix, preferred_element_type=F32)
                              + jnp.dot(lo_ref[j, h], suffix, preferred_element_type=F32))
        slowest = None
        for h in range(D_HEADS):
            run = run_ref[h]
            for j in range(count):
                w_ref[j, h] = jnp.exp(ls_ref[j, h] + run + sums[j, h][:, :T]).astype(BF16)
                run = run + sums[j, h][:, T:]
            run_ref[h] = run
            slowest = run if slowest is None else jnp.maximum(slowest, run)
        for h in range(D_HEADS):
            ps = slice((h // 2) * T, (h // 2 + 1) * T)
            acc = acc_ref[h]
            for j in range(count):
                acc = acc + jnp.dot(w_ref[j, h], v_ref[rows[j], ps], preferred_element_type=F32)
            acc_ref[h] = acc
        return (jnp.max(slowest) > LOG_F32_UNDERFLOW).astype(I32)

    def one_more(state):
        t, _ = state
        return t + 1, walk(t, 1)

    lax.while_loop(lambda s: (s[0] <= i) & (s[1] > 0), one_more, (jnp.int32(SB_FIRST), walk(0, SB_FIRST)))

    for hp in range(D_HEADS // 2):
        ps = slice(hp * T, (hp + 1) * T)
        o = jnp.where(low, acc_ref[2 * hp], acc_ref[2 * hp + 1]) * _silu(g_ref[:, ps].astype(F32))
        o_ref[:, ps] = o.astype(o_ref.dtype)


def _stick_breaking(pcd):
    bsz, p, _ = pcd.shape
    T = LANES
    hw = D_HEADS * D_DIM
    one = pl.Buffered(1)
    return pl.pallas_call(
        _sb_kernel,
        grid=(bsz, p // T),
        in_specs=[
            pl.BlockSpec((None, T, hw), lambda b, i: (b, i, 3)),
            pl.BlockSpec((None, T, hw), lambda b, i: (b, i, 6)),
            pl.BlockSpec((None, p, hw), lambda b, i: (b, 0, 4), pipeline_mode=one),
            pl.BlockSpec((None, p, hw), lambda b, i: (b, 0, 5), pipeline_mode=one),
        ],
        out_specs=pl.BlockSpec((None, T, hw), lambda b, i: (b, i, 0)),
        out_shape=jax.ShapeDtypeStruct((bsz, p, hw), BF16),
        scratch_shapes=[
            pltpu.VMEM((D_HEADS, T, T), BF16),
            pltpu.VMEM((D_HEADS, T, T), F32),
            pltpu.VMEM((D_HEADS, T, T), F32),
            pltpu.VMEM((SB_FIRST, D_HEADS, T, T), BF16),
            pltpu.VMEM((SB_FIRST, D_HEADS, T, T), BF16),
            pltpu.VMEM((SB_FIRST, D_HEADS, T, T), F32),
            pltpu.VMEM((SB_FIRST, D_HEADS, T, T), BF16),
        ],
        compiler_params=_params(("parallel", "arbitrary")),
        name="stick_breaking",
    )(pcd, pcd, pcd, pcd)


def _rel_bucket_np(rel):
    half = REL_BUCKETS // 2
    max_exact = half // 2
    n = -rel
    ret = np.where(n < 0, half, 0)
    n = np.abs(n)
    edges = [math.ceil(max_exact * (REL_MAX_DIST / max_exact) ** (j / (half - max_exact)) - 1e-9)
             for j in range(1, half - max_exact)]
    large = max_exact + sum((n >= e).astype(np.int64) for e in edges)
    return ret + np.where(n < max_exact, n, large)


def _bias_tiles(rel_bias):
    i = np.arange(LANES)[:, None]
    j = np.arange(LANES)[None, :]
    idx = np.stack([_rel_bucket_np(j - i - LANES * d) for d in range(3)])
    assert (idx[2] == idx[2, 0, 0]).all()
    table = rel_bias.astype(F32) - rel_bias.astype(F32)[idx[2, 0, 0]]
    near = jnp.asarray(idx[:2], I32)[None]
    tiles = jnp.zeros((table.shape[1],) + near.shape[1:], F32)
    for bucket in range(REL_BUCKETS):
        tiles = jnp.where(near == bucket, table[bucket][:, None, None, None], tiles)
    return tiles


def _layout_ab(w):
    aq, ak, av, ag, iq, ik, iw, bq, bk, bv, bg, ba = jnp.split(w, np.cumsum(SPLIT_AB)[:-1].tolist(), axis=1)
    pad = lambda t: jnp.pad(t, ((0, 0), (0, LANES - t.shape[1])))
    cols = [aq, ak, av, ag, iq, bv, bg, bq, bk, jnp.concatenate([ik, ik], axis=1), pad(iw), pad(ba),
            jnp.zeros((w.shape[0], W_AB_PAD - W_AB_USED), w.dtype)]
    return jnp.concatenate(cols, axis=1).astype(BF16)


def kernel(x, meta_tokens, rel_bias, norm_g, final_g, w_in_ab, gla_gate_w2, gla_gate_b, w_out_ab,
           w_in_cd, w_out_cd):
    bsz, seq, d = x.shape
    p = seq + PAD_FRONT + N_META
    depth = norm_g.shape[0]
    topk = min(TOPK_MAX, seq // 4)
    h = jnp.concatenate([jnp.zeros((bsz, PAD_FRONT, d), x.dtype),
                         jnp.broadcast_to(meta_tokens.astype(x.dtype)[None], (bsz, N_META, d)),
                         x], axis=1)
    bias_tiles = _bias_tiles(rel_bias)
    tables = _retention_tables(p)
    assert depth >= 1
    hn = None
    for layer in range(depth):
        j = layer // 2
        w_in = _layout_ab(w_in_ab[j]) if layer % 2 == 0 else w_in_cd[j].astype(BF16)
        proj = _inproj(h, norm_g[layer], w_in) if hn is None else _proj(hn, w_in)
        if layer % 2 == 0:
            first = _dsa(proj, bias_tiles, topk)
            w2 = jnp.pad(gla_gate_w2[j], ((0, LANES - GLA_GATE_RANK), (0, 0))).astype(BF16)
            second = _gla(proj, w2, gla_gate_b[j].reshape(1, -1).astype(F32))
            w_out = w_out_ab[j]
        else:
            first = _retention(proj, tables)
            second = _stick_breaking(proj)
            w_out = w_out_cd[j]
        if layer + 1 < depth:
            h, hn = _outproj(first, second, w_out, h, norm_g[layer + 1])
    return _outproj_norm(first, second, w_out, h, final_g, seq)
```

```python
import functools
import math

import numpy as np
import jax
import jax.numpy as jnp
from jax import lax
from jax.experimental import pallas as pl
from jax.experimental.pallas import tpu as pltpu

F32 = jnp.float32
BF16 = jnp.bfloat16
I32 = jnp.int32

CHUNK = 64
CHUNK_SHIFT = CHUNK.bit_length() - 1
N_META = 16
PAD_FRONT = 128 - N_META
NORM_EPS = 1e-6
NEG_INF = -1e30
A_HEADS, A_DIM = 8, 64
IDX_HEADS, IDX_DIM = 8, 64
IDX_SCALE = (IDX_DIM ** -0.5) * (IDX_HEADS ** -0.5)
TOPK_MAX = 256
B_HEADS, B_DK, B_DV = 4, 64, 128
GLA_GATE_RANK = 16
GLA_TAU = 16.0
C_HEADS, C_DK, C_DV = 4, 64, 128
ROPE_BASE = 10000.0
D_HEADS, D_DIM = 8, 64
REL_BUCKETS = 32
REL_MAX_DIST = 128
SPLIT_AB = (512, 512, 512, 512, 512, 64, 8, 256, 256, 512, 512, 16)

LANES = 128
W_AB_USED = 4480
W_AB_PAD = 4608
I32_MIN = -2 ** 31
DUMMY = 7
IDX_BITS = 14
SETTLE_FROM_BIT = 22
LOG_F32_UNDERFLOW = -104.0
LOGIT_LIMIT = 40.0
NORM_SLACK = 1.05
TINY = 1e-30
BF16_BITS = -65536
CHUNK_GROUP = 10
SB_FIRST = 3
VMEM_LIMIT = 56 * 1024 * 1024

NT_DIMS = (((1,), (1,)), ((), ()))
TN_DIMS = (((0,), (0,)), ((), ()))


def _pick_tile(n, candidates):
    for c in candidates:
        if n % c == 0:
            return c
    raise ValueError(f"no tile for {n}")


def _params(sem):
    return pltpu.CompilerParams(dimension_semantics=sem, vmem_limit_bytes=VMEM_LIMIT)


def _silu(x):
    return x / (1.0 + jnp.exp(-x))


def _softplus_parts(z):
    t = jnp.log(1.0 + jnp.exp(-jnp.abs(z)))
    return jnp.maximum(z, 0.0) + t, jnp.minimum(z, 0.0) - t


def _split3(x):
    a = x.astype(BF16)
    r = x - a.astype(F32)
    b = r.astype(BF16)
    c = (r - b.astype(F32)).astype(BF16)
    return a, b, c


def _inproj_kernel(x_ref, g_ref, w_ref, o_ref, hn_ref):
    @pl.when(pl.program_id(2) == 0)
    def _():
        x = x_ref[...]
        ms = jnp.mean(x * x, axis=-1, keepdims=True)
        hn_ref[...] = (x * lax.rsqrt(ms + NORM_EPS) * g_ref[...]).astype(BF16)

    o_ref[...] = jnp.dot(hn_ref[...], w_ref[...], preferred_element_type=F32).astype(o_ref.dtype)


def _inproj(h, g, w):
    bsz, p, d = h.shape
    n = w.shape[1]
    tp = _pick_tile(p, (1664, 1280, 640, 128))
    tn = _pick_tile(n, (1792, 1536, 896, 512, 128))
    return pl.pallas_call(
        _inproj_kernel,
        grid=(bsz, p // tp, n // tn),
        in_specs=[
            pl.BlockSpec((None, tp, d), lambda b, i, j: (b, i, 0)),
            pl.BlockSpec((1, d), lambda b, i, j: (0, 0)),
            pl.BlockSpec((d, tn), lambda b, i, j: (0, j)),
        ],
        out_specs=pl.BlockSpec((None, tp, tn), lambda b, i, j: (b, i, j)),
        out_shape=jax.ShapeDtypeStruct((bsz, p, n), BF16),
        scratch_shapes=[pltpu.VMEM((tp, d), BF16)],
        compiler_params=_params(("parallel", "parallel", "arbitrary")),
        name="inproj",
    )(h, g.reshape(1, d), w)


def _proj_kernel(x_ref, w_ref, o_ref):
    o_ref[...] = jnp.dot(x_ref[...], w_ref[...], preferred_element_type=F32).astype(o_ref.dtype)


def _proj(hn, w):
    bsz, p, d = hn.shape
    n = w.shape[1]
    tp = _pick_tile(p, (1664, 1280, 640, 128))
    tn = _pick_tile(n, (1792, 1536, 896, 512, 128))
    return pl.pallas_call(
        _proj_kernel,
        grid=(bsz, p // tp, n // tn),
        in_specs=[
            pl.BlockSpec((None, tp, d), lambda b, i, j: (b, i, 0)),
            pl.BlockSpec((d, tn), lambda b, i, j: (0, j)),
        ],
        out_specs=pl.BlockSpec((None, tp, tn), lambda b, i, j: (b, i, j)),
        out_shape=jax.ShapeDtypeStruct((bsz, p, n), BF16),
        compiler_params=_params(("parallel", "parallel", "parallel")),
        name="proj",
    )(hn, w)


def _outproj_kernel(m1_ref, m2_ref, w1_ref, w2_ref, h_ref, g_ref, o_ref, hn_ref, *, tp):
    y = jnp.dot(m1_ref[...], w1_ref[...], preferred_element_type=F32)
    y = y + jnp.dot(m2_ref[...], w2_ref[...], preferred_element_type=F32)
    pos = pl.program_id(1) * tp + lax.broadcasted_iota(I32, y.shape, 0)
    x = h_ref[...] + jnp.where(pos >= PAD_FRONT, y, 0.0)
    o_ref[...] = x
    ms = jnp.mean(x * x, axis=-1, keepdims=True)
    hn_ref[...] = (x * lax.rsqrt(ms + NORM_EPS) * g_ref[...]).astype(BF16)


def _outproj(m1, m2, w_out, h, g_next):
    bsz, p, d = h.shape
    k1 = m1.shape[-1]
    k2 = m2.shape[-1]
    tp = _pick_tile(p, (832, 640, 128))
    w = w_out.astype(BF16)
    return pl.pallas_call(
        functools.partial(_outproj_kernel, tp=tp),
        grid=(bsz, p // tp),
        in_specs=[
            pl.BlockSpec((None, tp, k1), lambda b, i: (b, i, 0)),
            pl.BlockSpec((None, tp, k2), lambda b, i: (b, i, 0)),
            pl.BlockSpec((k1, d), lambda b, i: (0, 0)),
            pl.BlockSpec((k2, d), lambda b, i: (0, 0)),
            pl.BlockSpec((None, tp, d), lambda b, i: (b, i, 0)),
            pl.BlockSpec((1, d), lambda b, i: (0, 0)),
        ],
        out_specs=[pl.BlockSpec((None, tp, d), lambda b, i: (b, i, 0)),
                   pl.BlockSpec((None, tp, d), lambda b, i: (b, i, 0))],
        out_shape=[jax.ShapeDtypeStruct((bsz, p, d), F32), jax.ShapeDtypeStruct((bsz, p, d), BF16)],
        compiler_params=_params(("parallel", "parallel")),
        name="outproj",
    )(m1, m2, w[:k1], w[k1:], h, g_next.reshape(1, d))


def _outproj_norm_kernel(m1_ref, m2_ref, w1_ref, w2_ref, h_ref, g_ref, o_ref):
    y = jnp.dot(m1_ref[...], w1_ref[...], preferred_element_type=F32)
    y = y + jnp.dot(m2_ref[...], w2_ref[...], preferred_element_type=F32)
    x = h_ref[...] + y
    ms = jnp.mean(x * x, axis=-1, keepdims=True)
    o_ref[...] = x * lax.rsqrt(ms + NORM_EPS) * g_ref[...]


def _outproj_norm(m1, m2, w_out, h, g, seq):
    bsz, p, d = h.shape
    k1 = m1.shape[-1]
    k2 = m2.shape[-1]
    tp = LANES
    skip = (p - seq) // tp
    assert skip * tp == p - seq
    w = w_out.astype(BF16)
    rows = lambda b, i: (b, i + skip, 0)
    return pl.pallas_call(
        _outproj_norm_kernel,
        grid=(bsz, seq // tp),
        in_specs=[
            pl.BlockSpec((None, tp, k1), rows),
            pl.BlockSpec((None, tp, k2), rows),
            pl.BlockSpec((k1, d), lambda b, i: (0, 0)),
            pl.BlockSpec((k2, d), lambda b, i: (0, 0)),
            pl.BlockSpec((None, tp, d), rows),
            pl.BlockSpec((1, d), lambda b, i: (0, 0)),
        ],
        out_specs=pl.BlockSpec((None, tp, d), lambda b, i: (b, i, 0)),
        out_shape=jax.ShapeDtypeStruct((bsz, seq, d), F32),
        compiler_params=_params(("parallel", "parallel")),
        name="outproj_norm",
    )(m1, m2, w[:k1], w[k1:], h, g.reshape(1, d))


def _dsa_kernel(aq_ref, ag_ref, iq_ref, iw_ref, k_ref, v_ref, ik_ref, bias_ref, o_ref,
                key_ref, sa_ref, sb_ref, iqs_ref, wts_ref, qm_ref,
                jdx_ref, kmax_ref, mx_ref, ls_ref, acc_ref, *, topk):
    T = LANES
    i = pl.program_id(1)
    last = i + DUMMY

    def num_groups(group):
        return (i + group) // group

    def first_slot(group):
        return last + 1 - group * num_groups(group)

    row = lax.broadcasted_iota(I32, (T, T), 0)
    col = lax.broadcasted_iota(I32, (T, T), 1)
    qchunk = (i * T + row) >> CHUNK_SHIFT
    low = col < A_DIM

    def admissible(kt):
        kpos = kt * T + col
        return (kpos >= PAD_FRONT) & ((kpos >> CHUNK_SHIFT) <= qchunk)

    def tile_rows(slot):
        return pl.ds(pl.multiple_of(jnp.maximum(slot - DUMMY, 0) * T, T), T)

    iw = iw_ref[...].astype(F32)
    for h in range(IDX_HEADS):
        pair = iq_ref[:, (h // 2) * T:(h // 2 + 1) * T]
        mine = low if h % 2 == 0 else ~low
        iqs_ref[0, h * T:(h + 1) * T, :] = jnp.where(mine, pair, jnp.zeros_like(pair))
        wts_ref[h] = jnp.broadcast_to(iw[:, h:h + 1], (T, T))
    for h in range(A_HEADS):
        pair = aq_ref[:, (h // 2) * T:(h // 2 + 1) * T]
        mine = low if h % 2 == 0 else ~low
        qm_ref[h // 2, (h % 2) * T:(h % 2 + 1) * T, :] = jnp.where(
            mine, pair * (A_DIM ** -0.5), jnp.zeros_like(pair))

    def sweep(lhs_ref, rhs_ref, consume, consume_last):
        group_rows = lhs_ref.shape[1]

        def issue(s0, stage):
            rows = [tile_rows(s0), tile_rows(s0 + 1)]
            for g in range(lhs_ref.shape[0]):
                ps = slice(g * T, (g + 1) * T)
                kw = jnp.concatenate([rhs_ref[rows[0], ps], rhs_ref[rows[1], ps]], axis=0)
                stage[g * group_rows:(g + 1) * group_rows, :] = lax.dot_general(
                    lhs_ref[g], kw, NT_DIMS, preferred_element_type=F32)

        first = first_slot(4)
        issue(first, sa_ref)

        def quad(s):
            issue(s + 2, sb_ref)
            consume(s, sa_ref)
            issue(s + 4, sa_ref)
            consume(s + 2, sb_ref)

        inner = num_groups(4) - 1

        @pl.when(inner % 2 == 1)
        def _():
            quad(first)

        def body(q, carry):
            s = first + 4 * (inner % 2) + 8 * q
            quad(s)
            quad(s + 4)
            return carry

        lax.fori_loop(0, inner // 2, body, 0)
        issue(last - 1, sb_ref)
        consume(last - 3, sa_ref)
        consume_last(last - 1, sb_ref)

    def score_pair(s0, stage):
        for u in range(2):
            tot = jnp.zeros((T, T), F32)
            for h in range(IDX_HEADS):
                tot = tot + jnp.maximum(stage[h * T:(h + 1) * T, u * T:(u + 1) * T], 0.0) * wts_ref[h]
            score = jnp.where(admissible(s0 + u - DUMMY), tot * IDX_SCALE, NEG_INF)
            bits = lax.bitcast_convert_type(score, I32)
            key_ref[s0 + u] = bits ^ ((bits >> 31) & 0x7FFFFFFF)

    sweep(iqs_ref, ik_ref, score_pair, score_pair)
    for d in range(DUMMY):
        key_ref[d] = jnp.full((T, T), I32_MIN, I32)

    def row_count(pred):
        one = jnp.ones((T, T), I32)
        zero = jnp.zeros((T, T), I32)
        first = first_slot(8)

        def body(g, acc):
            hits = [jnp.where(pred(first + 8 * g + u), one, zero) for u in range(8)]
            while len(hits) > 1:
                hits = [a + b for a, b in zip(hits[::2], hits[1::2])]
            return acc + hits[0]

        acc = lax.fori_loop(0, num_groups(8), body, zero)
        return jnp.sum(acc.astype(F32), axis=-1, keepdims=True)

    def value_bit(it, state):
        t, reach = state
        cand = t + (jnp.int32(1) << (31 - it))
        wide = jnp.broadcast_to(cand, (T, T))
        cnt = row_count(lambda s: key_ref[s] >= wide)
        fits = cnt >= topk
        return jnp.where(fits, cand, t), jnp.where(fits, cnt, reach)

    everything = (8 * T * num_groups(8)).astype(F32)
    state = lax.fori_loop(0, SETTLE_FROM_BIT, value_bit, (jnp.full((T, 1), I32_MIN, I32),
                                                          jnp.full((T, 1), everything, F32)))

    wide = jnp.broadcast_to(state[0], (T, T))
    final = row_count(lambda s: key_ref[s] > wide) < topk

    def unsettled(carry):
        it, (_, reach) = carry
        return (it < 32) & (jnp.max(jnp.where(final, topk, reach)) > topk)

    def two_bits(carry):
        it, state = carry
        return it + 2, value_bit(it + 1, value_bit(it, state))

    _, (thr, reach) = lax.while_loop(unsettled, two_bits, (jnp.int32(SETTLE_FROM_BIT), state))
    thr = jnp.broadcast_to(thr, (T, T))
    jdx_ref[...] = jnp.full((T, T), 2 ** IDX_BITS, I32)

    @pl.when(jnp.max(reach) > topk)
    def _():
        need = topk - (reach - row_count(lambda s: key_ref[s] == thr))
        p_r = lax.broadcasted_iota(I32, (T, 2 * T), 0)
        p_c = lax.broadcasted_iota(I32, (T, 2 * T), 1)
        prefix_ones = jnp.where((p_c >= T) | (p_r <= p_c), 1.0, 0.0).astype(BF16)
        first = first_slot(8)

        def body(g, state):
            seen, best = state
            slots = [first + 8 * g + u for u in range(8)]
            tied = [key_ref[s] == thr for s in slots]
            sums = [jnp.dot(jnp.where(t, 1.0, 0.0).astype(BF16), prefix_ones,
                            preferred_element_type=F32) for t in tied]
            for s, t, sm in zip(slots, tied, sums):
                keep = t & (seen + sm[:, :T] <= need)
                best = jnp.maximum(best, jnp.where(keep, (s - DUMMY) * T + col, -1))
                seen = seen + sm[:, T:]
            return seen, best

        _, best = lax.fori_loop(0, num_groups(8), body,
                                (jnp.zeros((T, T), F32), jnp.full((T, T), -1, I32)))
        jdx_ref[...] = jnp.broadcast_to(jnp.max(best, axis=-1, keepdims=True), (T, T))

    jdx = jdx_ref[...]

    def selection_mask(s):
        key = key_ref[s]
        sel = (key > thr) | ((key == thr) & ((s - DUMMY) * T + col <= jdx))
        return jnp.where(sel & admissible(s - DUMMY), 0.0, NEG_INF)

    head_of_lane = (lax.broadcasted_iota(I32, (A_HEADS * A_DIM, T), 0) // A_DIM
                    == lax.broadcasted_iota(I32, (A_HEADS * A_DIM, T), 1))
    head_sum = jnp.where(head_of_lane, 1.0, 0.0).astype(BF16)

    def head_norms2(t):
        t = t.astype(F32)
        return jnp.dot((t * t).astype(BF16), head_sum, preferred_element_type=F32)

    @pl.when(i == 0)
    def _():
        def widest(t, best):
            return jnp.maximum(best, head_norms2(k_ref[pl.ds(pl.multiple_of(t * T, T), T), :]))
        best = lax.fori_loop(0, k_ref.shape[0] // T, widest, jnp.zeros((T, T), F32))
        kmax_ref[...] = jnp.broadcast_to(jnp.max(best, axis=0, keepdims=True), kmax_ref.shape)

    bound2 = head_norms2(aq_ref[...]) * kmax_ref[0:1, :] * (NORM_SLACK / A_DIM)
    room = LOGIT_LIMIT - jnp.max(jnp.abs(bias_ref[...]))
    small_logits = (room > 0.0) & (jnp.max(bound2) <= room * room)

    ls_ref[...] = jnp.zeros(ls_ref.shape, F32)
    acc_ref[...] = jnp.zeros(acc_ref.shape, F32)

    def attend(s0, stage, near, final):
        madd = [selection_mask(s0), selection_mask(s0 + 1)]
        for hp in range(A_HEADS // 2):
            weights = []
            for h in (2 * hp, 2 * hp + 1):
                parts = [stage[h * T:(h + 1) * T, u * T:(u + 1) * T] + madd[u] for u in range(2)]
                if near:
                    parts = [parts[u] + bias_ref[h, 1 - u] for u in range(2)]
                if not final:
                    mx_ref[h] = jnp.maximum(mx_ref[h], jnp.maximum(parts[0], parts[1]))
                else:
                    es = [jnp.exp(part - mx_ref[h]) for part in parts]
                    ls_ref[h] = ls_ref[h] + (es[0] + es[1])
                    weights.append(jnp.concatenate(es, axis=1).astype(BF16))
            if final:
                ps = slice(hp * T, (hp + 1) * T)
                vw = jnp.concatenate([v_ref[tile_rows(s0), ps], v_ref[tile_rows(s0 + 1), ps]], axis=0)
                acc_ref[hp] = acc_ref[hp] + jnp.dot(jnp.concatenate(weights, axis=0), vw,
                                                    preferred_element_type=F32)

    def attention_sweep(final):
        sweep(qm_ref, k_ref,
              functools.partial(attend, near=False, final=final),
              functools.partial(attend, near=True, final=final))

    @pl.when(small_logits)
    def _():
        mx_ref[...] = jnp.zeros(mx_ref.shape, F32)

    @pl.when(jnp.logical_not(small_logits))
    def _():
        mx_ref[...] = jnp.full(mx_ref.shape, NEG_INF, F32)
        attention_sweep(False)
        for h in range(A_HEADS):
            mx_ref[h] = jnp.broadcast_to(jnp.max(mx_ref[h], axis=-1, keepdims=True), (T, T))

    attention_sweep(True)

    def row_total(h):
        return jnp.maximum(jnp.sum(ls_ref[h], axis=-1, keepdims=True), TINY)

    for hp in range(A_HEADS // 2):
        o0 = acc_ref[hp, :T, :] / row_total(2 * hp)
        o1 = acc_ref[hp, T:, :] / row_total(2 * hp + 1)
        g = ag_ref[:, hp * T:(hp + 1) * T].astype(F32)
        o_ref[:, hp * T:(hp + 1) * T] = (jnp.where(low, o0, o1) * _silu(g)).astype(o_ref.dtype)


def _dsa(pab, bias_tiles, topk):
    bsz, p, _ = pab.shape
    T = LANES
    nq = p // T
    hw = A_HEADS * A_DIM
    one = pl.Buffered(1)
    return pl.pallas_call(
        functools.partial(_dsa_kernel, topk=float(topk)),
        grid=(bsz, nq),
        in_specs=[
            pl.BlockSpec((None, T, hw), lambda b, i: (b, i, 0)),
            pl.BlockSpec((None, T, hw), lambda b, i: (b, i, 3)),
            pl.BlockSpec((None, T, hw), lambda b, i: (b, i, 4)),
            pl.BlockSpec((None, T, T), lambda b, i: (b, i, 33)),
            pl.BlockSpec((None, p, hw), lambda b, i: (b, 0, 1), pipeline_mode=one),
            pl.BlockSpec((None, p, hw), lambda b, i: (b, 0, 2), pipeline_mode=one),
            pl.BlockSpec((None, p, T), lambda b, i: (b, 0, 32), pipeline_mode=one),
            pl.BlockSpec((A_HEADS, 2, T, T), lambda b, i: (0, 0, 0, 0), pipeline_mode=one),
        ],
        out_specs=pl.BlockSpec((None, T, hw), lambda b, i: (b, i, 0)),
        out_shape=jax.ShapeDtypeStruct((bsz, p, hw), BF16),
        scratch_shapes=[
            pltpu.VMEM((nq + DUMMY, T, T), I32),
            pltpu.VMEM((A_HEADS * T, 2 * T), F32),
            pltpu.VMEM((A_HEADS * T, 2 * T), F32),
            pltpu.VMEM((1, IDX_HEADS * T, T), BF16),
            pltpu.VMEM((IDX_HEADS, T, T), F32),
            pltpu.VMEM((A_HEADS // 2, 2 * T, T), BF16),
            pltpu.VMEM((T, T), I32),
            pltpu.VMEM((8, T), F32),
            pltpu.VMEM((A_HEADS, T, T), F32),
            pltpu.VMEM((A_HEADS, T, T), F32),
            pltpu.VMEM((A_HEADS // 2, 2 * T, T), F32),
        ],
        compiler_params=_params(("parallel", "arbitrary")),
        name="dsa",
    )(pab, pab, pab, pab, pab, pab, pab, bias_tiles)


def _gla_kernel(q_ref, k_ref, v_ref, g_ref, a_ref, w2_ref, gb_ref, o_ref, st_ref, *, nchunk):
    C = CHUNK
    T = LANES

    @pl.when(pl.program_id(1) == 0)
    def _():
        st_ref[...] = jnp.zeros(st_ref.shape, F32)

    r_i = lax.broadcasted_iota(I32, (C, C), 0)
    c_i = lax.broadcasted_iota(I32, (C, C), 1)
    causal = c_i <= r_i
    tri = jnp.where(causal, 1.0, 0.0).astype(BF16)
    lane = lax.broadcasted_iota(I32, (C, T), 1)
    sd_r = lax.broadcasted_iota(I32, (2 * B_DV, T), 0)
    sd_c = lax.broadcasted_iota(I32, (2 * B_DV, T), 1)
    blockdiag = (sd_r >= B_DV) == (sd_c >= B_DK)

    pairs = range(B_HEADS // 2)
    lanes = [slice(hp * T, (hp + 1) * T) for hp in pairs]
    G = math.gcd(nchunk, CHUNK_GROUP)

    def group(gi, carry):
        rows = [pl.ds(pl.multiple_of((gi * G + c) * C, C), C) for c in range(G)]
        log_a = [_softplus_parts(jnp.dot(a_ref[r, :], w2_ref[...], preferred_element_type=F32)
                                 + gb_ref[...])[1] * (1.0 / GLA_TAU) for r in rows]
        terms = [_split3(x) for x in log_a]
        bcum = [sum(jnp.dot(tri, t, preferred_element_type=F32) for t in ts) for ts in terms]
        q_t, k_t, k_d, decay = [], [], [], []
        for r, b in zip(rows, bcum):
            b_last = b[C - 1:C, :]
            q = q_ref[r, :].astype(F32) * (B_DK ** -0.5)
            k = k_ref[r, :].astype(F32)
            q_t.append((q * jnp.exp(b)).astype(BF16))
            k_t.append((k * jnp.exp(-b)).astype(BF16))
            k_d.append((k * jnp.exp(b_last - b)).astype(BF16))
            decay.append(jnp.exp(b_last))
        att = {}
        for c in range(G):
            for h in range(B_HEADS):
                ls = lanes[h // 2]
                mine = (lane < B_DK) if h % 2 == 0 else (lane >= B_DK)
                qm = jnp.where(mine, q_t[c][:, ls], jnp.zeros((C, T), BF16))
                s = lax.dot_general(qm, k_t[c][:, ls], NT_DIMS, preferred_element_type=F32)
                att[c, h] = jnp.where(causal, s, 0.0).astype(BF16)
        contrib = {}
        for c in range(G):
            for hp in pairs:
                vp = v_ref[rows[c], hp * 2 * B_DV:(hp + 1) * 2 * B_DV]
                raw = lax.dot_general(vp, k_d[c][:, lanes[hp]], TN_DIMS, preferred_element_type=F32)
                contrib[c, hp] = jnp.where(blockdiag, raw, 0.0)
        before = {}
        for hp in pairs:
            st = st_ref[hp]
            for c in range(G):
                before[c, hp] = st
                st = st * decay[c][:, lanes[hp]] + contrib[c, hp]
            st_ref[hp] = st
        for c in range(G):
            for hp in pairs:
                o_inter = lax.dot_general(q_t[c][:, lanes[hp]], before[c, hp].astype(BF16), NT_DIMS,
                                          preferred_element_type=F32)
                for hh in range(2):
                    h = 2 * hp + hh
                    vs = slice(h * B_DV, (h + 1) * B_DV)
                    o = jnp.dot(att[c, h], v_ref[rows[c], vs], preferred_element_type=F32)
                    o = o + o_inter[:, hh * B_DV:(hh + 1) * B_DV]
                    o = o * lax.rsqrt(jnp.mean(o * o, axis=-1, keepdims=True) + NORM_EPS)
                    o = o * _silu(g_ref[rows[c], vs].astype(F32))
                    o_ref[rows[c], vs] = o.astype(o_ref.dtype)
        return carry

    lax.fori_loop(0, nchunk // G, group, 0)


def _gla(pab, w2, gb):
    bsz, p, _ = pab.shape
    tc = _pick_tile(p, (640, 128))
    hv = B_HEADS * B_DV
    hk = B_HEADS * B_DK
    return pl.pallas_call(
        functools.partial(_gla_kernel, nchunk=tc // CHUNK),
        grid=(bsz, p // tc),
        in_specs=[
            pl.BlockSpec((None, tc, hk), lambda b, c: (b, c, 14)),
            pl.BlockSpec((None, tc, hk), lambda b, c: (b, c, 15)),
            pl.BlockSpec((None, tc, hv), lambda b, c: (b, c, 5)),
            pl.BlockSpec((None, tc, hv), lambda b, c: (b, c, 6)),
            pl.BlockSpec((None, tc, LANES), lambda b, c: (b, c, 34)),
            pl.BlockSpec((LANES, hk), lambda b, c: (0, 0)),
            pl.BlockSpec((1, hk), lambda b, c: (0, 0)),
        ],
        out_specs=pl.BlockSpec((None, tc, hv), lambda b, c: (b, c, 0)),
        out_shape=jax.ShapeDtypeStruct((bsz, p, hv), BF16),
        scratch_shapes=[pltpu.VMEM((B_HEADS // 2, 2 * B_DV, 2 * B_DK), F32)],
        compiler_params=_params(("parallel", "arbitrary")),
        name="gla",
    )(pab, pab, pab, pab, pab, w2, gb)


def _ret_kernel(q_ref, k_ref, v_ref, g_ref, cos_ref, sin_ref, dmat_ref, zeta_ref, xi_ref,
                cdec_ref, o_ref, st_ref, *, nchunk):
    C = CHUNK
    T = LANES
    W = C_HEADS * C_DK

    @pl.when(pl.program_id(1) == 0)
    def _():
        st_ref[...] = jnp.zeros(st_ref.shape, F32)

    lane_w = lax.broadcasted_iota(I32, (C, W), 1)
    first_half = (lane_w & (C_DK - 1)) < (C_DK // 2)
    lane = lax.broadcasted_iota(I32, (C, T), 1)
    sd_r = lax.broadcasted_iota(I32, (2 * C_DV, T), 0)
    sd_c = lax.broadcasted_iota(I32, (2 * C_DV, T), 1)
    blockdiag = (sd_r >= C_DV) == (sd_c >= C_DK)

    def rotate(x, cos, sin_signed):
        swapped = jnp.where(first_half, pltpu.roll(x, W - C_DK // 2, 1), pltpu.roll(x, C_DK // 2, 1))
        return x * cos + swapped * sin_signed

    pairs = range(C_HEADS // 2)
    lanes = [slice(hp * T, (hp + 1) * T) for hp in pairs]
    G = math.gcd(nchunk, CHUNK_GROUP)

    def group(gi, carry):
        rows = [pl.ds(pl.multiple_of((gi * G + c) * C, C), C) for c in range(G)]
        q_b, k_b, q_x, k_z = [], [], [], []
        for r in rows:
            cos = cos_ref[r, :]
            sin = sin_ref[r, :]
            q = rotate(q_ref[r, :].astype(F32), cos, sin)
            k = rotate(k_ref[r, :].astype(F32), cos, sin) * (C_DK ** -0.5)
            q_b.append(q.astype(BF16))
            k_b.append(k.astype(BF16))
            q_x.append((q * xi_ref[...]).astype(BF16))
            k_z.append((k * zeta_ref[...]).astype(BF16))
        att = {}
        for c in range(G):
            for h in range(C_HEADS):
                ls = lanes[h // 2]
                mine = (lane < C_DK) if h % 2 == 0 else (lane >= C_DK)
                qm = jnp.where(mine, q_b[c][:, ls], jnp.zeros((C, T), BF16))
                s = lax.dot_general(qm, k_b[c][:, ls], NT_DIMS, preferred_element_type=F32)
                att[c, h] = (s * dmat_ref[h]).astype(BF16)
        contrib = {}
        for c in range(G):
            for hp in pairs:
                vp = v_ref[rows[c], hp * 2 * C_DV:(hp + 1) * 2 * C_DV]
                raw = lax.dot_general(vp, k_z[c][:, lanes[hp]], TN_DIMS, preferred_element_type=F32)
                contrib[c, hp] = jnp.where(blockdiag, raw, 0.0)
        before = {}
        for hp in pairs:
            st = st_ref[hp]
            for c in range(G):
                before[c, hp] = st
                st = st * cdec_ref[:, lanes[hp]] + contrib[c, hp]
            st_ref[hp] = st
        for c in range(G):
            for hp in pairs:
                o_inter = lax.dot_general(q_x[c][:, lanes[hp]], before[c, hp].astype(BF16), NT_DIMS,
                                          preferred_element_type=F32)
                for hh in range(2):
                    h = 2 * hp + hh
                    vs = slice(h * C_DV, (h + 1) * C_DV)
                    o = jnp.dot(att[c, h], v_ref[rows[c], vs], preferred_element_type=F32)
                    o = o + o_inter[:, hh * C_DV:(hh + 1) * C_DV]
                    o = o - jnp.mean(o, axis=-1, keepdims=True)
                    o = o * lax.rsqrt(jnp.mean(o * o, axis=-1, keepdims=True) + NORM_EPS)
                    o = o * _silu(g_ref[rows[c], vs].astype(F32))
                    o_ref[rows[c], vs] = o.astype(o_ref.dtype)
        return carry

    lax.fori_loop(0, nchunk // G, group, 0)


def _retention_tables(p):
    log_gamma = np.log(1.0 - np.exp2(-5.0 - np.arange(C_HEADS, dtype=np.float64)))
    i = np.arange(CHUNK, dtype=np.float64)
    diff = i[:, None] - i[None, :]
    dmat = np.where(diff >= 0, np.exp(log_gamma[:, None, None] * np.maximum(diff, 0.0)), 0.0)
    zeta = np.exp(log_gamma[:, None] * (CHUNK - 1 - i))
    xi = np.exp(log_gamma[:, None] * (i + 1))
    cdec = np.exp(log_gamma * CHUNK)
    widen = lambda t: np.repeat(t.T[:, :, None], C_DK, axis=2).reshape(CHUNK, C_HEADS * C_DK)
    half = C_DK // 2
    inv = jnp.asarray(ROPE_BASE, F32) ** (-jnp.arange(half, dtype=F32) / half)
    ang = jnp.arange(p, dtype=jnp.int32).astype(F32)[:, None] * inv[None, :]
    cos = jnp.tile(jnp.cos(ang), (1, 2 * C_HEADS))
    sin = jnp.sin(ang)
    sin_signed = jnp.tile(jnp.concatenate([-sin, sin], axis=1), (1, C_HEADS))
    return (cos, sin_signed, jnp.asarray(dmat, F32), jnp.asarray(widen(zeta), F32),
            jnp.asarray(widen(xi), F32),
            jnp.asarray(np.repeat(cdec, C_DK)[None, :], F32))


def _retention(pcd, tables):
    bsz, p, _ = pcd.shape
    cos, sin, dmat, zeta, xi, cdec = tables
    tc = _pick_tile(p, (640, 128))
    hv = C_HEADS * C_DV
    hk = C_HEADS * C_DK
    full = lambda shape: pl.BlockSpec(shape, lambda b, c: (0,) * len(shape))
    return pl.pallas_call(
        functools.partial(_ret_kernel, nchunk=tc // CHUNK),
        grid=(bsz, p // tc),
        in_specs=[
            pl.BlockSpec((None, tc, hk), lambda b, c: (b, c, 0)),
            pl.BlockSpec((None, tc, hk), lambda b, c: (b, c, 1)),
            pl.BlockSpec((None, tc, hv), lambda b, c: (b, c, 1)),
            pl.BlockSpec((None, tc, hv), lambda b, c: (b, c, 2)),
            pl.BlockSpec((tc, hk), lambda b, c: (c, 0)),
            pl.BlockSpec((tc, hk), lambda b, c: (c, 0)),
            full((C_HEADS, CHUNK, CHUNK)),
            full((CHUNK, hk)),
            full((CHUNK, hk)),
            full((1, hk)),
        ],
        out_specs=pl.BlockSpec((None, tc, hv), lambda b, c: (b, c, 0)),
        out_shape=jax.ShapeDtypeStruct((bsz, p, hv), BF16),
        scratch_shapes=[pltpu.VMEM((C_HEADS // 2, 2 * C_DV, 2 * C_DK), F32)],
        compiler_params=_params(("parallel", "arbitrary")),
        name="retention",
    )(pcd, pcd, pcd, pcd, cos, sin, dmat, zeta, xi, cdec)


def _sb_kernel(q_ref, g_ref, k_ref, v_ref, o_ref, qm_ref, run_ref, acc_ref, hi_ref, lo_ref, ls_ref,
               w_ref):
    T = LANES
    i = pl.program_id(1)
    row = lax.broadcasted_iota(I32, (T, T), 0)
    col = lax.broadcasted_iota(I32, (T, T), 1)
    qpos = i * T + row
    low = col < D_DIM
    u_r = lax.broadcasted_iota(I32, (T, 2 * T), 0)
    u_c = lax.broadcasted_iota(I32, (T, 2 * T), 1)
    suffix = jnp.where((u_c >= T) | (u_r > u_c), 1.0, 0.0).astype(BF16)

    for h in range(D_HEADS):
        pair = q_ref[:, (h // 2) * T:(h // 2 + 1) * T]
        mine = low if h % 2 == 0 else ~low
        qm_ref[h] = jnp.where(mine, pair * (D_DIM ** -0.5), jnp.zeros_like(pair))
    run_ref[...] = jnp.zeros(run_ref.shape, F32)
    acc_ref[...] = jnp.zeros(acc_ref.shape, F32)

    def walk(t0, count):
        kts = [i - t0 - j for j in range(count)]
        rows = [pl.ds(pl.multiple_of(jnp.maximum(kt, 0) * T, T), T) for kt in kts]
        for j, kt in enumerate(kts):
            kpos = kt * T + col
            ok = (kpos < qpos) & (kpos >= PAD_FRONT)
            for h in range(D_HEADS):
                ps = slice((h // 2) * T, (h // 2 + 1) * T)
                z = lax.dot_general(qm_ref[h], k_ref[rows[j], ps], NT_DIMS, preferred_element_type=F32)
                sp, logsig = _softplus_parts(z)
                log_1m = jnp.where(ok, -sp, 0.0)
                hi = lax.bitcast_convert_type(lax.bitcast_convert_type(log_1m, I32) & BF16_BITS, F32)
                hi_ref[j, h] = hi.astype(BF16)
                lo_ref[j, h] = (log_1m - hi).astype(BF16)
                ls_ref[j, h] = jnp.where(ok, logsig, NEG_INF)
        sums = {}
        for j in range(count):
            for h in range(D_HEADS):
                sums[j, h] = (jnp.dot(hi_ref[j, h], suffix, preferred_element_type=F32)
                              + jnp.dot(lo_ref[j, h], suffix, preferred_element_type=F32))
        slowest = None
        for h in range(D_HEADS):
            run = run_ref[h]
            for j in range(count):
                w_ref[j, h] = jnp.exp(ls_ref[j, h] + run + sums[j, h][:, :T]).astype(BF16)
                run = run + sums[j, h][:, T:]
            run_ref[h] = run
            slowest = run if slowest is None else jnp.maximum(slowest, run)
        for h in range(D_HEADS):
            ps = slice((h // 2) * T, (h // 2 + 1) * T)
            acc = acc_ref[h]
            for j in range(count):
                acc = acc + jnp.dot(w_ref[j, h], v_ref[rows[j], ps], preferred_element_type=F32)
            acc_ref[h] = acc
        return (jnp.max(slowest) > LOG_F32_UNDERFLOW).astype(I32)

    def one_more(state):
        t, _ = state
        return t + 1, walk(t, 1)

    lax.while_loop(lambda s: (s[0] <= i) & (s[1] > 0), one_more, (jnp.int32(SB_FIRST), walk(0, SB_FIRST)))

    for hp in range(D_HEADS // 2):
        ps = slice(hp * T, (hp + 1) * T)
        o = jnp.where(low, acc_ref[2 * hp], acc_ref[2 * hp + 1]) * _silu(g_ref[:, ps].astype(F32))
        o_ref[:, ps] = o.astype(o_ref.dtype)


def _stick_breaking(pcd):
    bsz, p, _ = pcd.shape
    T = LANES
    hw = D_HEADS * D_DIM
    one = pl.Buffered(1)
    return pl.pallas_call(
        _sb_kernel,
        grid=(bsz, p // T),
        in_specs=[
            pl.BlockSpec((None, T, hw), lambda b, i: (b, i, 3)),
            pl.BlockSpec((None, T, hw), lambda b, i: (b, i, 6)),
            pl.BlockSpec((None, p, hw), lambda b, i: (b, 0, 4), pipeline_mode=one),
            pl.BlockSpec((None, p, hw), lambda b, i: (b, 0, 5), pipeline_mode=one),
        ],
        out_specs=pl.BlockSpec((None, T, hw), lambda b, i: (b, i, 0)),
        out_shape=jax.ShapeDtypeStruct((bsz, p, hw), BF16),
        scratch_shapes=[
            pltpu.VMEM((D_HEADS, T, T), BF16),
            pltpu.VMEM((D_HEADS, T, T), F32),
            pltpu.VMEM((D_HEADS, T, T), F32),
            pltpu.VMEM((SB_FIRST, D_HEADS, T, T), BF16),
            pltpu.VMEM((SB_FIRST, D_HEADS, T, T), BF16),
            pltpu.VMEM((SB_FIRST, D_HEADS, T, T), F32),
            pltpu.VMEM((SB_FIRST, D_HEADS, T, T), BF16),
        ],
        compiler_params=_params(("parallel", "arbitrary")),
        name="stick_breaking",
    )(pcd, pcd, pcd, pcd)


def _rel_bucket_np(rel):
    half = REL_BUCKETS // 2
    max_exact = half // 2
    n = -rel
    ret = np.where(n < 0, half, 0)
    n = np.abs(n)
    edges = [math.ceil(max_exact * (REL_MAX_DIST / max_exact) ** (j / (half - max_exact)) - 1e-9)
             for j in range(1, half - max_exact)]
    large = max_exact + sum((n >= e).astype(np.int64) for e in edges)
    return ret + np.where(n < max_exact, n, large)


def _bias_tiles(rel_bias):
    i = np.arange(LANES)[:, None]
    j = np.arange(LANES)[None, :]
    idx = np.stack([_rel_bucket_np(j - i - LANES * d) for d in range(3)])
    assert (idx[2] == idx[2, 0, 0]).all()
    table = rel_bias.astype(F32) - rel_bias.astype(F32)[idx[2, 0, 0]]
    near = jnp.asarray(idx[:2], I32)[None]
    tiles = jnp.zeros((table.shape[1],) + near.shape[1:], F32)
    for bucket in range(REL_BUCKETS):
        tiles = jnp.where(near == bucket, table[bucket][:, None, None, None], tiles)
    return tiles


def _layout_ab(w):
    aq, ak, av, ag, iq, ik, iw, bq, bk, bv, bg, ba = jnp.split(w, np.cumsum(SPLIT_AB)[:-1].tolist(), axis=1)
    pad = lambda t: jnp.pad(t, ((0, 0), (0, LANES - t.shape[1])))
    cols = [aq, ak, av, ag, iq, bv, bg, bq, bk, jnp.concatenate([ik, ik], axis=1), pad(iw), pad(ba),
            jnp.zeros((w.shape[0], W_AB_PAD - W_AB_USED), w.dtype)]
    return jnp.concatenate(cols, axis=1).astype(BF16)


def kernel(x, meta_tokens, rel_bias, norm_g, final_g, w_in_ab, gla_gate_w2, gla_gate_b, w_out_ab,
           w_in_cd, w_out_cd):
    bsz, seq, d = x.shape
    p = seq + PAD_FRONT + N_META
    depth = norm_g.shape[0]
    topk = min(TOPK_MAX, seq // 4)
    h = jnp.concatenate([jnp.zeros((bsz, PAD_FRONT, d), x.dtype),
                         jnp.broadcast_to(meta_tokens.astype(x.dtype)[None], (bsz, N_META, d)),
                         x], axis=1)
    bias_tiles = _bias_tiles(rel_bias)
    tables = _retention_tables(p)
    assert depth >= 1
    hn = None
    for layer in range(depth):
        j = layer // 2
        w_in = _layout_ab(w_in_ab[j]) if layer % 2 == 0 else w_in_cd[j].astype(BF16)
        proj = _inproj(h, norm_g[layer], w_in) if hn is None else _proj(hn, w_in)
        if layer % 2 == 0:
            first = _dsa(proj, bias_tiles, topk)
            w2 = jnp.pad(gla_gate_w2[j], ((0, LANES - GLA_GATE_RANK), (0, 0))).astype(BF16)
            second = _gla(proj, w2, gla_gate_b[j].reshape(1, -1).astype(F32))
            w_out = w_out_ab[j]
        else:
            first = _retention(proj, tables)
            second = _stick_breaking(proj)
            w_out = w_out_cd[j]
        if layer + 1 < depth:
            h, hn = _outproj(first, second, w_out, h, norm_g[layer + 1])
    return _outproj_norm(first, second, w_out, h, final_g, seq)
```

```python
import functools
import math

import numpy as np
import jax
import jax.numpy as jnp
from jax import lax
from jax.experimental import pallas as pl
from jax.experimental.pallas import tpu as pltpu

F32 = jnp.float32
BF16 = jnp.bfloat16
I32 = jnp.int32

CHUNK = 64
CHUNK_SHIFT = CHUNK.bit_length() - 1
N_META = 16
PAD_FRONT = 128 - N_META
NORM_EPS = 1e-6
NEG_INF = -1e30
A_HEADS, A_DIM = 8, 64
IDX_HEADS, IDX_DIM = 8, 64
IDX_SCALE = (IDX_DIM ** -0.5) * (IDX_HEADS ** -0.5)
TOPK_MAX = 256
B_HEADS, B_DK, B_DV = 4, 64, 128
GLA_GATE_RANK = 16
GLA_TAU = 16.0
C_HEADS, C_DK, C_DV = 4, 64, 128
ROPE_BASE = 10000.0
D_HEADS, D_DIM = 8, 64
REL_BUCKETS = 32
REL_MAX_DIST = 128
SPLIT_AB = (512, 512, 512, 512, 512, 64, 8, 256, 256, 512, 512, 16)

LANES = 128
W_AB_USED = 4480
W_AB_PAD = 4608
I32_MIN = -2 ** 31
DUMMY = 7
IDX_BITS = 14
SETTLE_FROM_BIT = 22
LOG_F32_UNDERFLOW = -104.0
LOGIT_LIMIT = 40.0
NORM_SLACK = 1.05
TINY = 1e-30
BF16_BITS = -65536
CHUNK_GROUP = 10
SB_FIRST = 3
VMEM_LIMIT = 56 * 1024 * 1024

NT_DIMS = (((1,), (1,)), ((), ()))
TN_DIMS = (((0,), (0,)), ((), ()))


def _pick_tile(n, candidates):
    for c in candidates:
        if n % c == 0:
            return c
    raise ValueError(f"no tile for {n}")


def _params(sem):
    return pltpu.CompilerParams(dimension_semantics=sem, vmem_limit_bytes=VMEM_LIMIT)


def _silu(x):
    return x / (1.0 + jnp.exp(-x))


def _softplus_parts(z):
    t = jnp.log(1.0 + jnp.exp(-jnp.abs(z)))
    return jnp.maximum(z, 0.0) + t, jnp.minimum(z, 0.0) - t


def _split3(x):
    a = x.astype(BF16)
    r = x - a.astype(F32)
    b = r.astype(BF16)
    c = (r - b.astype(F32)).astype(BF16)
    return a, b, c


def _inproj_kernel(x_ref, g_ref, w_ref, o_ref, hn_ref):
    @pl.when(pl.program_id(2) == 0)
    def _():
        x = x_ref[...]
        ms = jnp.mean(x * x, axis=-1, keepdims=True)
        hn_ref[...] = (x * lax.rsqrt(ms + NORM_EPS) * g_ref[...]).astype(BF16)

    o_ref[...] = jnp.dot(hn_ref[...], w_ref[...], preferred_element_type=F32).astype(o_ref.dtype)


def _inproj(h, g, w):
    bsz, p, d = h.shape
    n = w.shape[1]
    tp = _pick_tile(p, (1664, 1280, 640, 128))
    tn = _pick_tile(n, (1792, 1536, 896, 512, 128))
    return pl.pallas_call(
        _inproj_kernel,
        grid=(bsz, p // tp, n // tn),
        in_specs=[
            pl.BlockSpec((None, tp, d), lambda b, i, j: (b, i, 0)),
            pl.BlockSpec((1, d), lambda b, i, j: (0, 0)),
            pl.BlockSpec((d, tn), lambda b, i, j: (0, j)),
        ],
        out_specs=pl.BlockSpec((None, tp, tn), lambda b, i, j: (b, i, j)),
        out_shape=jax.ShapeDtypeStruct((bsz, p, n), BF16),
        scratch_shapes=[pltpu.VMEM((tp, d), BF16)],
        compiler_params=_params(("parallel", "parallel", "arbitrary")),
        name="inproj",
    )(h, g.reshape(1, d), w)


def _proj_kernel(x_ref, w_ref, o_ref):
    o_ref[...] = jnp.dot(x_ref[...], w_ref[...], preferred_element_type=F32).astype(o_ref.dtype)


def _proj(hn, w):
    bsz, p, d = hn.shape
    n = w.shape[1]
    tp = _pick_tile(p, (1664, 1280, 640, 128))
    tn = _pick_tile(n, (1792, 1536, 896, 512, 128))
    return pl.pallas_call(
        _proj_kernel,
        grid=(bsz, p // tp, n // tn),
        in_specs=[
            pl.BlockSpec((None, tp, d), lambda b, i, j: (b, i, 0)),
            pl.BlockSpec((d, tn), lambda b, i, j: (0, j)),
        ],
        out_specs=pl.BlockSpec((None, tp, tn), lambda b, i, j: (b, i, j)),
        out_shape=jax.ShapeDtypeStruct((bsz, p, n), BF16),
        compiler_params=_params(("parallel", "parallel", "parallel")),
        name="proj",
    )(hn, w)


def _outproj_kernel(m1_ref, m2_ref, w1_ref, w2_ref, h_ref, g_ref, o_ref, hn_ref, *, tp):
    y = jnp.dot(m1_ref[...], w1_ref[...], preferred_element_type=F32)
    y = y + jnp.dot(m2_ref[...], w2_ref[...], preferred_element_type=F32)
    pos = pl.program_id(1) * tp + lax.broadcasted_iota(I32, y.shape, 0)
    x = h_ref[...] + jnp.where(pos >= PAD_FRONT, y, 0.0)
    o_ref[...] = x
    ms = jnp.mean(x * x, axis=-1, keepdims=True)
    hn_ref[...] = (x * lax.rsqrt(ms + NORM_EPS) * g_ref[...]).astype(BF16)


def _outproj(m1, m2, w_out, h, g_next):
    bsz, p, d = h.shape
    k1 = m1.shape[-1]
    k2 = m2.shape[-1]
    tp = _pick_tile(p, (832, 640, 128))
    w = w_out.astype(BF16)
    return pl.pallas_call(
        functools.partial(_outproj_kernel, tp=tp),
        grid=(bsz, p // tp),
        in_specs=[
            pl.BlockSpec((None, tp, k1), lambda b, i: (b, i, 0)),
            pl.BlockSpec((None, tp, k2), lambda b, i: (b, i, 0)),
            pl.BlockSpec((k1, d), lambda b, i: (0, 0)),
            pl.BlockSpec((k2, d), lambda b, i: (0, 0)),
            pl.BlockSpec((None, tp, d), lambda b, i: (b, i, 0)),
            pl.BlockSpec((1, d), lambda b, i: (0, 0)),
        ],
        out_specs=[pl.BlockSpec((None, tp, d), lambda b, i: (b, i, 0)),
                   pl.BlockSpec((None, tp, d), lambda b, i: (b, i, 0))],
        out_shape=[jax.ShapeDtypeStruct((bsz, p, d), F32), jax.ShapeDtypeStruct((bsz, p, d), BF16)],
        compiler_params=_params(("parallel", "parallel")),
        name="outproj",
    )(m1, m2, w[:k1], w[k1:], h, g_next.reshape(1, d))


def _outproj_norm_kernel(m1_ref, m2_ref, w1_ref, w2_ref, h_ref, g_ref, o_ref):
    y = jnp.dot(m1_ref[...], w1_ref[...], preferred_element_type=F32)
    y = y + jnp.dot(m2_ref[...], w2_ref[...], preferred_element_type=F32)
    x = h_ref[...] + y
    ms = jnp.mean(x * x, axis=-1, keepdims=True)
    o_ref[...] = x * lax.rsqrt(ms + NORM_EPS) * g_ref[...]


def _outproj_norm(m1, m2, w_out, h, g, seq):
    bsz, p, d = h.shape
    k1 = m1.shape[-1]
    k2 = m2.shape[-1]
    tp = LANES
    skip = (p - seq) // tp
    assert skip * tp == p - seq
    w = w_out.astype(BF16)
    rows = lambda b, i: (b, i + skip, 0)
    return pl.pallas_call(
        _outproj_norm_kernel,
        grid=(bsz, seq // tp),
        in_specs=[
            pl.BlockSpec((None, tp, k1), rows),
            pl.BlockSpec((None, tp, k2), rows),
            pl.BlockSpec((k1, d), lambda b, i: (0, 0)),
            pl.BlockSpec((k2, d), lambda b, i: (0, 0)),
            pl.BlockSpec((None, tp, d), rows),
            pl.BlockSpec((1, d), lambda b, i: (0, 0)),
        ],
        out_specs=pl.BlockSpec((None, tp, d), lambda b, i: (b, i, 0)),
        out_shape=jax.ShapeDtypeStruct((bsz, seq, d), F32),
        compiler_params=_params(("parallel", "parallel")),
        name="outproj_norm",
    )(m1, m2, w[:k1], w[k1:], h, g.reshape(1, d))


def _dsa_kernel(aq_ref, ag_ref, iq_ref, iw_ref, k_ref, v_ref, ik_ref, bias_ref, o_ref,
                key_ref, sa_ref, sb_ref, iqs_ref, wts_ref, qm_ref,
                jdx_ref, kmax_ref, mx_ref, ls_ref, acc_ref, *, topk):
    T = LANES
    R = 2 * T
    i = pl.program_id(1)
    top = 2 * i + 1
    last = top + DUMMY

    def num_groups(group):
        return (top + group) // group

    def first_slot(group):
        return last + 1 - group * num_groups(group)

    row = lax.broadcasted_iota(I32, (R, T), 0)
    col = lax.broadcasted_iota(I32, (R, T), 1)
    qchunk = (i * R + row) >> CHUNK_SHIFT
    low = col < A_DIM

    def admissible(kt):
        kpos = kt * T + col
        return (kpos >= PAD_FRONT) & ((kpos >> CHUNK_SHIFT) <= qchunk)

    def tile_rows(slot):
        tile = jnp.clip(slot - DUMMY, 0, k_ref.shape[0] // T - 1)
        return pl.ds(pl.multiple_of(tile * T, T), T)

    iw = iw_ref[...].astype(F32)
    for h in range(IDX_HEADS):
        pair = iq_ref[:, (h // 2) * T:(h // 2 + 1) * T]
        mine = low if h % 2 == 0 else ~low
        iqs_ref[0, h * R:(h + 1) * R, :] = jnp.where(mine, pair, jnp.zeros_like(pair))
        wts_ref[h] = jnp.broadcast_to(iw[:, h:h + 1], (R, T))
    for h in range(A_HEADS):
        pair = aq_ref[:, (h // 2) * T:(h // 2 + 1) * T]
        mine = low if h % 2 == 0 else ~low
        qm_ref[h // 2, (h % 2) * R:(h % 2 + 1) * R, :] = jnp.where(
            mine, pair * (A_DIM ** -0.5), jnp.zeros_like(pair))

    def sweep(lhs_ref, rhs_ref, consume, consume_prev, consume_last):
        group_rows = lhs_ref.shape[1]

        def issue(s0, stage):
            rows = [tile_rows(s0), tile_rows(s0 + 1)]
            for g in range(lhs_ref.shape[0]):
                ps = slice(g * T, (g + 1) * T)
                kw = jnp.concatenate([rhs_ref[rows[0], ps], rhs_ref[rows[1], ps]], axis=0)
                stage[g * group_rows:(g + 1) * group_rows, :] = lax.dot_general(
                    lhs_ref[g], kw, NT_DIMS, preferred_element_type=F32)

        first = first_slot(4)
        issue(first, sa_ref)

        def quad(s):
            issue(s + 2, sb_ref)
            consume(s, sa_ref)
            issue(s + 4, sa_ref)
            consume(s + 2, sb_ref)

        inner = num_groups(4) - 1

        @pl.when(inner % 2 == 1)
        def _():
            quad(first)

        def body(q, carry):
            s = first + 4 * (inner % 2) + 8 * q
            quad(s)
            quad(s + 4)
            return carry

        lax.fori_loop(0, inner // 2, body, 0)
        issue(last - 1, sb_ref)
        consume_prev(last - 3, sa_ref)
        consume_last(last - 1, sb_ref)

    def score_pair(s0, stage):
        for u in range(2):
            tot = jnp.zeros((R, T), F32)
            for h in range(IDX_HEADS):
                tot = tot + jnp.maximum(stage[h * R:(h + 1) * R, u * T:(u + 1) * T], 0.0) * wts_ref[h]
            score = jnp.where(admissible(s0 + u - DUMMY), tot * IDX_SCALE, NEG_INF)
            bits = lax.bitcast_convert_type(score, I32)
            key_ref[s0 + u] = bits ^ ((bits >> 31) & 0x7FFFFFFF)

    sweep(iqs_ref, ik_ref, score_pair, score_pair, score_pair)
    for d in range(DUMMY):
        key_ref[d] = jnp.full((R, T), I32_MIN, I32)

    def row_count(pred):
        one = jnp.ones((R, T), I32)
        zero = jnp.zeros((R, T), I32)
        first = first_slot(8)

        def body(g, acc):
            hits = [jnp.where(pred(first + 8 * g + u), one, zero) for u in range(8)]
            while len(hits) > 1:
                hits = [a + b for a, b in zip(hits[::2], hits[1::2])]
            return acc + hits[0]

        acc = lax.fori_loop(0, num_groups(8), body, zero)
        return jnp.sum(acc.astype(F32), axis=-1, keepdims=True)

    def value_bit(it, state):
        t, reach = state
        cand = t + (jnp.int32(1) << (31 - it))
        wide = jnp.broadcast_to(cand, (R, T))
        cnt = row_count(lambda s: key_ref[s] >= wide)
        fits = cnt >= topk
        return jnp.where(fits, cand, t), jnp.where(fits, cnt, reach)

    everything = (8 * T * num_groups(8)).astype(F32)
    state = lax.fori_loop(0, SETTLE_FROM_BIT, value_bit, (jnp.full((R, 1), I32_MIN, I32),
                                                          jnp.full((R, 1), everything, F32)))

    wide = jnp.broadcast_to(state[0], (R, T))
    final = row_count(lambda s: key_ref[s] > wide) < topk

    def unsettled(carry):
        it, (_, reach) = carry
        return (it < 32) & (jnp.max(jnp.where(final, topk, reach)) > topk)

    def two_bits(carry):
        it, state = carry
        return it + 2, value_bit(it + 1, value_bit(it, state))

    _, (thr, reach) = lax.while_loop(unsettled, two_bits, (jnp.int32(SETTLE_FROM_BIT), state))
    thr = jnp.broadcast_to(thr, (R, T))
    jdx_ref[...] = jnp.full((R, T), 2 ** IDX_BITS, I32)

    @pl.when(jnp.max(reach) > topk)
    def _():
        need = topk - (reach - row_count(lambda s: key_ref[s] == thr))
        p_r = lax.broadcasted_iota(I32, (T, 2 * T), 0)
        p_c = lax.broadcasted_iota(I32, (T, 2 * T), 1)
        prefix_ones = jnp.where((p_c >= T) | (p_r <= p_c), 1.0, 0.0).astype(BF16)
        first = first_slot(8)

        def body(g, state):
            seen, best = state
            slots = [first + 8 * g + u for u in range(8)]
            tied = [key_ref[s] == thr for s in slots]
            sums = [jnp.dot(jnp.where(t, 1.0, 0.0).astype(BF16), prefix_ones,
                            preferred_element_type=F32) for t in tied]
            for s, t, sm in zip(slots, tied, sums):
                keep = t & (seen + sm[:, :T] <= need)
                best = jnp.maximum(best, jnp.where(keep, (s - DUMMY) * T + col, -1))
                seen = seen + sm[:, T:]
            return seen, best

        _, best = lax.fori_loop(0, num_groups(8), body,
                                (jnp.zeros((R, T), F32), jnp.full((R, T), -1, I32)))
        jdx_ref[...] = jnp.broadcast_to(jnp.max(best, axis=-1, keepdims=True), (R, T))

    jdx = jdx_ref[...]

    def selection_mask(s):
        key = key_ref[s]
        sel = (key > thr) | ((key == thr) & ((s - DUMMY) * T + col <= jdx))
        return jnp.where(sel & admissible(s - DUMMY), 0.0, NEG_INF)

    head_of_lane = (lax.broadcasted_iota(I32, (A_HEADS * A_DIM, T), 0) // A_DIM
                    == lax.broadcasted_iota(I32, (A_HEADS * A_DIM, T), 1))
    head_sum = jnp.where(head_of_lane, 1.0, 0.0).astype(BF16)

    def head_norms2(t):
        t = t.astype(F32)
        return jnp.dot((t * t).astype(BF16), head_sum, preferred_element_type=F32)

    @pl.when(i == 0)
    def _():
        def widest(t, best):
            return jnp.maximum(best, head_norms2(k_ref[pl.ds(pl.multiple_of(t * T, T), T), :]))
        best = lax.fori_loop(0, k_ref.shape[0] // T, widest, jnp.zeros((T, T), F32))
        kmax_ref[...] = jnp.broadcast_to(jnp.max(best, axis=0, keepdims=True), kmax_ref.shape)

    bound2 = head_norms2(aq_ref[...]) * kmax_ref[0:1, :] * (NORM_SLACK / A_DIM)
    room = LOGIT_LIMIT - jnp.max(jnp.abs(bias_ref[...]))
    small_logits = (room > 0.0) & (jnp.max(bound2) <= room * room)

    ls_ref[...] = jnp.zeros(ls_ref.shape, F32)
    acc_ref[...] = jnp.zeros(acc_ref.shape, F32)

    def attend(s0, stage, near, final):
        madd = [selection_mask(s0), selection_mask(s0 + 1)]
        for hp in range(A_HEADS // 2):
            weights = []
            for h in (2 * hp, 2 * hp + 1):
                parts = [stage[h * R:(h + 1) * R, u * T:(u + 1) * T] + madd[u] for u in range(2)]
                if near is not None:
                    parts = [parts[u] + bias_ref[h, near, :, u * T:(u + 1) * T] for u in range(2)]
                if not final:
                    mx_ref[h] = jnp.maximum(mx_ref[h], jnp.maximum(parts[0], parts[1]))
                else:
                    es = [jnp.exp(part - mx_ref[h]) for part in parts]
                    ls_ref[h] = ls_ref[h] + (es[0] + es[1])
                    weights.append(jnp.concatenate(es, axis=1).astype(BF16))
            if final:
                ps = slice(hp * T, (hp + 1) * T)
                vw = jnp.concatenate([v_ref[tile_rows(s0), ps], v_ref[tile_rows(s0 + 1), ps]], axis=0)
                acc_ref[hp] = acc_ref[hp] + jnp.dot(jnp.concatenate(weights, axis=0), vw,
                                                    preferred_element_type=F32)

    def attention_sweep(final):
        sweep(qm_ref, k_ref,
              functools.partial(attend, near=None, final=final),
              functools.partial(attend, near=1, final=final),
              functools.partial(attend, near=0, final=final))

    @pl.when(small_logits)
    def _():
        mx_ref[...] = jnp.zeros(mx_ref.shape, F32)

    @pl.when(jnp.logical_not(small_logits))
    def _():
        mx_ref[...] = jnp.full(mx_ref.shape, NEG_INF, F32)
        attention_sweep(False)
        for h in range(A_HEADS):
            mx_ref[h] = jnp.broadcast_to(jnp.max(mx_ref[h], axis=-1, keepdims=True), (R, T))

    attention_sweep(True)

    def row_total(h):
        return jnp.maximum(jnp.sum(ls_ref[h], axis=-1, keepdims=True), TINY)

    for hp in range(A_HEADS // 2):
        o0 = acc_ref[hp, :R, :] / row_total(2 * hp)
        o1 = acc_ref[hp, R:, :] / row_total(2 * hp + 1)
        g = ag_ref[:, hp * T:(hp + 1) * T].astype(F32)
        o_ref[:, hp * T:(hp + 1) * T] = (jnp.where(low, o0, o1) * _silu(g)).astype(o_ref.dtype)


def _dsa(pab, bias_tiles, topk):
    bsz, p, _ = pab.shape
    T = LANES
    R = 2 * T
    steps = pl.cdiv(p, R)
    slots = 2 * steps + DUMMY
    hw = A_HEADS * A_DIM
    one = pl.Buffered(1)
    return pl.pallas_call(
        functools.partial(_dsa_kernel, topk=float(topk)),
        grid=(bsz, steps),
        in_specs=[
            pl.BlockSpec((None, R, hw), lambda b, i: (b, i, 0)),
            pl.BlockSpec((None, R, hw), lambda b, i: (b, i, 3)),
            pl.BlockSpec((None, R, hw), lambda b, i: (b, i, 4)),
            pl.BlockSpec((None, R, T), lambda b, i: (b, i, 33)),
            pl.BlockSpec((None, p, hw), lambda b, i: (b, 0, 1), pipeline_mode=one),
            pl.BlockSpec((None, p, hw), lambda b, i: (b, 0, 2), pipeline_mode=one),
            pl.BlockSpec((None, p, T), lambda b, i: (b, 0, 32), pipeline_mode=one),
            pl.BlockSpec((A_HEADS, 2, R, R), lambda b, i: (0, 0, 0, 0), pipeline_mode=one),
        ],
        out_specs=pl.BlockSpec((None, R, hw), lambda b, i: (b, i, 0)),
        out_shape=jax.ShapeDtypeStruct((bsz, p, hw), BF16),
        scratch_shapes=[
            pltpu.VMEM((slots, R, T), I32),
            pltpu.VMEM((A_HEADS * R, 2 * T), F32),
            pltpu.VMEM((A_HEADS * R, 2 * T), F32),
            pltpu.VMEM((1, IDX_HEADS * R, T), BF16),
            pltpu.VMEM((IDX_HEADS, R, T), F32),
            pltpu.VMEM((A_HEADS // 2, 2 * R, T), BF16),
            pltpu.VMEM((R, T), I32),
            pltpu.VMEM((8, T), F32),
            pltpu.VMEM((A_HEADS, R, T), F32),
            pltpu.VMEM((A_HEADS, R, T), F32),
            pltpu.VMEM((A_HEADS // 2, 2 * R, T), F32),
        ],
        compiler_params=_params(("parallel", "arbitrary")),
        name="dsa",
    )(pab, pab, pab, pab, pab, pab, pab, bias_tiles)


def _gla_kernel(q_ref, k_ref, v_ref, g_ref, a_ref, w2_ref, gb_ref, o_ref, st_ref, *, nchunk):
    C = CHUNK
    T = LANES

    @pl.when(pl.program_id(1) == 0)
    def _():
        st_ref[...] = jnp.zeros(st_ref.shape, F32)

    r_i = lax.broadcasted_iota(I32, (C, C), 0)
    c_i = lax.broadcasted_iota(I32, (C, C), 1)
    causal = c_i <= r_i
    tri = jnp.where(causal, 1.0, 0.0).astype(BF16)
    lane = lax.broadcasted_iota(I32, (C, T), 1)
    sd_r = lax.broadcasted_iota(I32, (2 * B_DV, T), 0)
    sd_c = lax.broadcasted_iota(I32, (2 * B_DV, T), 1)
    blockdiag = (sd_r >= B_DV) == (sd_c >= B_DK)

    pairs = range(B_HEADS // 2)
    lanes = [slice(hp * T, (hp + 1) * T) for hp in pairs]
    G = math.gcd(nchunk, CHUNK_GROUP)

    def group(gi, carry):
        rows = [pl.ds(pl.multiple_of((gi * G + c) * C, C), C) for c in range(G)]
        log_a = [_softplus_parts(jnp.dot(a_ref[r, :], w2_ref[...], preferred_element_type=F32)
                                 + gb_ref[...])[1] * (1.0 / GLA_TAU) for r in rows]
        terms = [_split3(x) for x in log_a]
        bcum = [sum(jnp.dot(tri, t, preferred_element_type=F32) for t in ts) for ts in terms]
        q_t, k_t, k_d, decay = [], [], [], []
        for r, b in zip(rows, bcum):
            b_last = b[C - 1:C, :]
            q = q_ref[r, :].astype(F32) * (B_DK ** -0.5)
            k = k_ref[r, :].astype(F32)
            q_t.append((q * jnp.exp(b)).astype(BF16))
            k_t.append((k * jnp.exp(-b)).astype(BF16))
            k_d.append((k * jnp.exp(b_last - b)).astype(BF16))
            decay.append(jnp.exp(b_last))
        att = {}
        for c in range(G):
            for h in range(B_HEADS):
                ls = lanes[h // 2]
                mine = (lane < B_DK) if h % 2 == 0 else (lane >= B_DK)
                qm = jnp.where(mine, q_t[c][:, ls], jnp.zeros((C, T), BF16))
                s = lax.dot_general(qm, k_t[c][:, ls], NT_DIMS, preferred_element_type=F32)
                att[c, h] = jnp.where(causal, s, 0.0).astype(BF16)
        contrib = {}
        for c in range(G):
            for hp in pairs:
                vp = v_ref[rows[c], hp * 2 * B_DV:(hp + 1) * 2 * B_DV]
                raw = lax.dot_general(vp, k_d[c][:, lanes[hp]], TN_DIMS, preferred_element_type=F32)
                contrib[c, hp] = jnp.where(blockdiag, raw, 0.0)
        before = {}
        for hp in pairs:
            st = st_ref[hp]
            for c in range(G):
                before[c, hp] = st
                st = st * decay[c][:, lanes[hp]] + contrib[c, hp]
            st_ref[hp] = st
        for c in range(G):
            for hp in pairs:
                o_inter = lax.dot_general(q_t[c][:, lanes[hp]], before[c, hp].astype(BF16), NT_DIMS,
                                          preferred_element_type=F32)
                for hh in range(2):
                    h = 2 * hp + hh
                    vs = slice(h * B_DV, (h + 1) * B_DV)
                    o = jnp.dot(att[c, h], v_ref[rows[c], vs], preferred_element_type=F32)
                    o = o + o_inter[:, hh * B_DV:(hh + 1) * B_DV]
                    o = o * lax.rsqrt(jnp.mean(o * o, axis=-1, keepdims=True) + NORM_EPS)
                    o = o * _silu(g_ref[rows[c], vs].astype(F32))
                    o_ref[rows[c], vs] = o.astype(o_ref.dtype)
        return carry

    lax.fori_loop(0, nchunk // G, group, 0)


def _gla(pab, w2, gb):
    bsz, p, _ = pab.shape
    tc = _pick_tile(p, (640, 128))
    hv = B_HEADS * B_DV
    hk = B_HEADS * B_DK
    return pl.pallas_call(
        functools.partial(_gla_kernel, nchunk=tc // CHUNK),
        grid=(bsz, p // tc),
        in_specs=[
            pl.BlockSpec((None, tc, hk), lambda b, c: (b, c, 14)),
            pl.BlockSpec((None, tc, hk), lambda b, c: (b, c, 15)),
            pl.BlockSpec((None, tc, hv), lambda b, c: (b, c, 5)),
            pl.BlockSpec((None, tc, hv), lambda b, c: (b, c, 6)),
            pl.BlockSpec((None, tc, LANES), lambda b, c: (b, c, 34)),
            pl.BlockSpec((LANES, hk), lambda b, c: (0, 0)),
            pl.BlockSpec((1, hk), lambda b, c: (0, 0)),
        ],
        out_specs=pl.BlockSpec((None, tc, hv), lambda b, c: (b, c, 0)),
        out_shape=jax.ShapeDtypeStruct((bsz, p, hv), BF16),
        scratch_shapes=[pltpu.VMEM((B_HEADS // 2, 2 * B_DV, 2 * B_DK), F32)],
        compiler_params=_params(("parallel", "arbitrary")),
        name="gla",
    )(pab, pab, pab, pab, pab, w2, gb)


def _ret_kernel(q_ref, k_ref, v_ref, g_ref, cos_ref, sin_ref, dmat_ref, zeta_ref, xi_ref,
                cdec_ref, o_ref, st_ref, *, nchunk):
    C = CHUNK
    T = LANES
    W = C_HEADS * C_DK

    @pl.when(pl.program_id(1) == 0)
    def _():
        st_ref[...] = jnp.zeros(st_ref.shape, F32)

    lane_w = lax.broadcasted_iota(I32, (C, W), 1)
    first_half = (lane_w & (C_DK - 1)) < (C_DK // 2)
    lane = lax.broadcasted_iota(I32, (C, T), 1)
    sd_r = lax.broadcasted_iota(I32, (2 * C_DV, T), 0)
    sd_c = lax.broadcasted_iota(I32, (2 * C_DV, T), 1)
    blockdiag = (sd_r >= C_DV) == (sd_c >= C_DK)

    def rotate(x, cos, sin_signed):
        swapped = jnp.where(first_half, pltpu.roll(x, W - C_DK // 2, 1), pltpu.roll(x, C_DK // 2, 1))
        return x * cos + swapped * sin_signed

    pairs = range(C_HEADS // 2)
    lanes = [slice(hp * T, (hp + 1) * T) for hp in pairs]
    G = math.gcd(nchunk, CHUNK_GROUP)

    def group(gi, carry):
        rows = [pl.ds(pl.multiple_of((gi * G + c) * C, C), C) for c in range(G)]
        q_b, k_b, q_x, k_z = [], [], [], []
        for r in rows:
            cos = cos_ref[r, :]
            sin = sin_ref[r, :]
            q = rotate(q_ref[r, :].astype(F32), cos, sin)
            k = rotate(k_ref[r, :].astype(F32), cos, sin) * (C_DK ** -0.5)
            q_b.append(q.astype(BF16))
            k_b.append(k.astype(BF16))
            q_x.append((q * xi_ref[...]).astype(BF16))
            k_z.append((k * zeta_ref[...]).astype(BF16))
        att = {}
        for c in range(G):
            for h in range(C_HEADS):
                ls = lanes[h // 2]
                mine = (lane < C_DK) if h % 2 == 0 else (lane >= C_DK)
                qm = jnp.where(mine, q_b[c][:, ls], jnp.zeros((C, T), BF16))
                s = lax.dot_general(qm, k_b[c][:, ls], NT_DIMS, preferred_element_type=F32)
                att[c, h] = (s * dmat_ref[h]).astype(BF16)
        contrib = {}
        for c in range(G):
            for hp in pairs:
                vp = v_ref[rows[c], hp * 2 * C_DV:(hp + 1) * 2 * C_DV]
                raw = lax.dot_general(vp, k_z[c][:, lanes[hp]], TN_DIMS, preferred_element_type=F32)
                contrib[c, hp] = jnp.where(blockdiag, raw, 0.0)
        before = {}
        for hp in pairs:
            st = st_ref[hp]
            for c in range(G):
                before[c, hp] = st
                st = st * cdec_ref[:, lanes[hp]] + contrib[c, hp]
            st_ref[hp] = st
        for c in range(G):
            for hp in pairs:
                o_inter = lax.dot_general(q_x[c][:, lanes[hp]], before[c, hp].astype(BF16), NT_DIMS,
                                          preferred_element_type=F32)
                for hh in range(2):
                    h = 2 * hp + hh
                    vs = slice(h * C_DV, (h + 1) * C_DV)
                    o = jnp.dot(att[c, h], v_ref[rows[c], vs], preferred_element_type=F32)
                    o = o + o_inter[:, hh * C_DV:(hh + 1) * C_DV]
                    o = o - jnp.mean(o, axis=-1, keepdims=True)
                    o = o * lax.rsqrt(jnp.mean(o * o, axis=-1, keepdims=True) + NORM_EPS)
                    o = o * _silu(g_ref[rows[c], vs].astype(F32))
                    o_ref[rows[c], vs] = o.astype(o_ref.dtype)
        return carry

    lax.fori_loop(0, nchunk // G, group, 0)


def _retention_tables(p):
    log_gamma = np.log(1.0 - np.exp2(-5.0 - np.arange(C_HEADS, dtype=np.float64)))
    i = np.arange(CHUNK, dtype=np.float64)
    diff = i[:, None] - i[None, :]
    dmat = np.where(diff >= 0, np.exp(log_gamma[:, None, None] * np.maximum(diff, 0.0)), 0.0)
    zeta = np.exp(log_gamma[:, None] * (CHUNK - 1 - i))
    xi = np.exp(log_gamma[:, None] * (i + 1))
    cdec = np.exp(log_gamma * CHUNK)
    widen = lambda t: np.repeat(t.T[:, :, None], C_DK, axis=2).reshape(CHUNK, C_HEADS * C_DK)
    half = C_DK // 2
    inv = jnp.asarray(ROPE_BASE, F32) ** (-jnp.arange(half, dtype=F32) / half)
    ang = jnp.arange(p, dtype=jnp.int32).astype(F32)[:, None] * inv[None, :]
    cos = jnp.tile(jnp.cos(ang), (1, 2 * C_HEADS))
    sin = jnp.sin(ang)
    sin_signed = jnp.tile(jnp.concatenate([-sin, sin], axis=1), (1, C_HEADS))
    return (cos, sin_signed, jnp.asarray(dmat, F32), jnp.asarray(widen(zeta), F32),
            jnp.asarray(widen(xi), F32),
            jnp.asarray(np.repeat(cdec, C_DK)[None, :], F32))


def _retention(pcd, tables):
    bsz, p, _ = pcd.shape
    cos, sin, dmat, zeta, xi, cdec = tables
    tc = _pick_tile(p, (640, 128))
    hv = C_HEADS * C_DV
    hk = C_HEADS * C_DK
    full = lambda shape: pl.BlockSpec(shape, lambda b, c: (0,) * len(shape))
    return pl.pallas_call(
        functools.partial(_ret_kernel, nchunk=tc // CHUNK),
        grid=(bsz, p // tc),
        in_specs=[
            pl.BlockSpec((None, tc, hk), lambda b, c: (b, c, 0)),
            pl.BlockSpec((None, tc, hk), lambda b, c: (b, c, 1)),
            pl.BlockSpec((None, tc, hv), lambda b, c: (b, c, 1)),
            pl.BlockSpec((None, tc, hv), lambda b, c: (b, c, 2)),
            pl.BlockSpec((tc, hk), lambda b, c: (c, 0)),
            pl.BlockSpec((tc, hk), lambda b, c: (c, 0)),
            full((C_HEADS, CHUNK, CHUNK)),
            full((CHUNK, hk)),
            full((CHUNK, hk)),
            full((1, hk)),
        ],
        out_specs=pl.BlockSpec((None, tc, hv), lambda b, c: (b, c, 0)),
        out_shape=jax.ShapeDtypeStruct((bsz, p, hv), BF16),
        scratch_shapes=[pltpu.VMEM((C_HEADS // 2, 2 * C_DV, 2 * C_DK), F32)],
        compiler_params=_params(("parallel", "arbitrary")),
        name="retention",
    )(pcd, pcd, pcd, pcd, cos, sin, dmat, zeta, xi, cdec)


def _sb_kernel(q_ref, g_ref, k_ref, v_ref, o_ref, qm_ref, run_ref, acc_ref, hi_ref, lo_ref, ls_ref,
               w_ref):
    T = LANES
    i = pl.program_id(1)
    row = lax.broadcasted_iota(I32, (T, T), 0)
    col = lax.broadcasted_iota(I32, (T, T), 1)
    qpos = i * T + row
    low = col < D_DIM
    u_r = lax.broadcasted_iota(I32, (T, 2 * T), 0)
    u_c = lax.broadcasted_iota(I32, (T, 2 * T), 1)
    suffix = jnp.where((u_c >= T) | (u_r > u_c), 1.0, 0.0).astype(BF16)

    for h in range(D_HEADS):
        pair = q_ref[:, (h // 2) * T:(h // 2 + 1) * T]
        mine = low if h % 2 == 0 else ~low
        qm_ref[h] = jnp.where(mine, pair * (D_DIM ** -0.5), jnp.zeros_like(pair))
    run_ref[...] = jnp.zeros(run_ref.shape, F32)
    acc_ref[...] = jnp.zeros(acc_ref.shape, F32)

    def walk(t0, count):
        kts = [i - t0 - j for j in range(count)]
        rows = [pl.ds(pl.multiple_of(jnp.maximum(kt, 0) * T, T), T) for kt in kts]
        for j, kt in enumerate(kts):
            kpos = kt * T + col
            ok = (kpos < qpos) & (kpos >= PAD_FRONT)
            for h in range(D_HEADS):
                ps = slice((h // 2) * T, (h // 2 + 1) * T)
                z = lax.dot_general(qm_ref[h], k_ref[rows[j], ps], NT_DIMS, preferred_element_type=F32)
                sp, logsig = _softplus_parts(z)
                log_1m = jnp.where(ok, -sp, 0.0)
                hi = lax.bitcast_convert_type(lax.bitcast_convert_type(log_1m, I32) & BF16_BITS, F32)
                hi_ref[j, h] = hi.astype(BF16)
                lo_ref[j, h] = (log_1m - hi).astype(BF16)
                ls_ref[j, h] = jnp.where(ok, logsig, NEG_INF)
        sums = {}
        for j in range(count):
            for h in range(D_HEADS):
                sums[j, h] = (jnp.dot(hi_ref[j, h], suffix, preferred_element_type=F32)
                              + jnp.dot(lo_ref[j, h], suffix, preferred_element_type=F32))
        slowest = None
        for h in range(D_HEADS):
            run = run_ref[h]
            for j in range(count):
                w_ref[j, h] = jnp.exp(ls_ref[j, h] + run + sums[j, h][:, :T]).astype(BF16)
                run = run + sums[j, h][:, T:]
            run_ref[h] = run
            slowest = run if slowest is None else jnp.maximum(slowest, run)
        for h in range(D_HEADS):
            ps = slice((h // 2) * T, (h // 2 + 1) * T)
            acc = acc_ref[h]
            for j in range(count):
                acc = acc + jnp.dot(w_ref[j, h], v_ref[rows[j], ps], preferred_element_type=F32)
            acc_ref[h] = acc
        return (jnp.max(slowest) > LOG_F32_UNDERFLOW).astype(I32)

    def one_more(state):
        t, _ = state
        return t + 1, walk(t, 1)

    lax.while_loop(lambda s: (s[0] <= i) & (s[1] > 0), one_more, (jnp.int32(SB_FIRST), walk(0, SB_FIRST)))

    for hp in range(D_HEADS // 2):
        ps = slice(hp * T, (hp + 1) * T)
        o = jnp.where(low, acc_ref[2 * hp], acc_ref[2 * hp + 1]) * _silu(g_ref[:, ps].astype(F32))
        o_ref[:, ps] = o.astype(o_ref.dtype)


def _stick_breaking(pcd):
    bsz, p, _ = pcd.shape
    T = LANES
    hw = D_HEADS * D_DIM
    one = pl.Buffered(1)
    return pl.pallas_call(
        _sb_kernel,
        grid=(bsz, p // T),
        in_specs=[
            pl.BlockSpec((None, T, hw), lambda b, i: (b, i, 3)),
            pl.BlockSpec((None, T, hw), lambda b, i: (b, i, 6)),
            pl.BlockSpec((None, p, hw), lambda b, i: (b, 0, 4), pipeline_mode=one),
            pl.BlockSpec((None, p, hw), lambda b, i: (b, 0, 5), pipeline_mode=one),
        ],
        out_specs=pl.BlockSpec((None, T, hw), lambda b, i: (b, i, 0)),
        out_shape=jax.ShapeDtypeStruct((bsz, p, hw), BF16),
        scratch_shapes=[
            pltpu.VMEM((D_HEADS, T, T), BF16),
            pltpu.VMEM((D_HEADS, T, T), F32),
            pltpu.VMEM((D_HEADS, T, T), F32),
            pltpu.VMEM((SB_FIRST, D_HEADS, T, T), BF16),
            pltpu.VMEM((SB_FIRST, D_HEADS, T, T), BF16),
            pltpu.VMEM((SB_FIRST, D_HEADS, T, T), F32),
            pltpu.VMEM((SB_FIRST, D_HEADS, T, T), BF16),
        ],
        compiler_params=_params(("parallel", "arbitrary")),
        name="stick_breaking",
    )(pcd, pcd, pcd, pcd)


def _rel_bucket_np(rel):
    half = REL_BUCKETS // 2
    max_exact = half // 2
    n = -rel
    ret = np.where(n < 0, half, 0)
    n = np.abs(n)
    edges = [math.ceil(max_exact * (REL_MAX_DIST / max_exact) ** (j / (half - max_exact)) - 1e-9)
             for j in range(1, half - max_exact)]
    large = max_exact + sum((n >= e).astype(np.int64) for e in edges)
    return ret + np.where(n < max_exact, n, large)


def _bias_tiles(rel_bias):
    r = np.arange(2 * LANES)[:, None]
    c = np.arange(2 * LANES)[None, :]
    idx = np.stack([_rel_bucket_np(c - r - 2 * LANES * d) for d in range(3)])
    assert (idx[2] == idx[2, 0, 0]).all()
    table = rel_bias.astype(F32) - rel_bias.astype(F32)[idx[2, 0, 0]]
    near = jnp.asarray(idx[:2], I32)[None]
    tiles = jnp.zeros((table.shape[1],) + near.shape[1:], F32)
    for bucket in range(REL_BUCKETS):
        tiles = jnp.where(near == bucket, table[bucket][:, None, None, None], tiles)
    return tiles


def _layout_ab(w):
    aq, ak, av, ag, iq, ik, iw, bq, bk, bv, bg, ba = jnp.split(w, np.cumsum(SPLIT_AB)[:-1].tolist(), axis=1)
    pad = lambda t: jnp.pad(t, ((0, 0), (0, LANES - t.shape[1])))
    cols = [aq, ak, av, ag, iq, bv, bg, bq, bk, jnp.concatenate([ik, ik], axis=1), pad(iw), pad(ba),
            jnp.zeros((w.shape[0], W_AB_PAD - W_AB_USED), w.dtype)]
    return jnp.concatenate(cols, axis=1).astype(BF16)


def kernel(x, meta_tokens, rel_bias, norm_g, final_g, w_in_ab, gla_gate_w2, gla_gate_b, w_out_ab,
           w_in_cd, w_out_cd):
    bsz, seq, d = x.shape
    p = seq + PAD_FRONT + N_META
    depth = norm_g.shape[0]
    topk = min(TOPK_MAX, seq // 4)
    h = jnp.concatenate([jnp.zeros((bsz, PAD_FRONT, d), x.dtype),
                         jnp.broadcast_to(meta_tokens.astype(x.dtype)[None], (bsz, N_META, d)),
                         x], axis=1)
    bias_tiles = _bias_tiles(rel_bias)
    tables = _retention_tables(p)
    assert depth >= 1
    hn = None
    for layer in range(depth):
        j = layer // 2
        w_in = _layout_ab(w_in_ab[j]) if layer % 2 == 0 else w_in_cd[j].astype(BF16)
        proj = _inproj(h, norm_g[layer], w_in) if hn is None else _proj(hn, w_in)
        if layer % 2 == 0:
            first = _dsa(proj, bias_tiles, topk)
            w2 = jnp.pad(gla_gate_w2[j], ((0, LANES - GLA_GATE_RANK), (0, 0))).astype(BF16)
            second = _gla(proj, w2, gla_gate_b[j].reshape(1, -1).astype(F32))
            w_out = w_out_ab[j]
        else:
            first = _retention(proj, tables)
            second = _stick_breaking(proj)
            w_out = w_out_cd[j]
        if layer + 1 < depth:
            h, hn = _outproj(first, second, w_out, h, norm_g[layer + 1])
    return _outproj_norm(first, second, w_out, h, final_g, seq)
```

```python
import functools
import math

import numpy as np
import jax
import jax.numpy as jnp
from jax import lax
from jax.experimental import pallas as pl
from jax.experimental.pallas import tpu as pltpu

F32 = jnp.float32
BF16 = jnp.bfloat16
I32 = jnp.int32

CHUNK = 64
CHUNK_SHIFT = CHUNK.bit_length() - 1
N_META = 16
PAD_FRONT = 128 - N_META
NORM_EPS = 1e-6
NEG_INF = -1e30
A_HEADS, A_DIM = 8, 64
IDX_HEADS, IDX_DIM = 8, 64
IDX_SCALE = (IDX_DIM ** -0.5) * (IDX_HEADS ** -0.5)
TOPK_MAX = 256
B_HEADS, B_DK, B_DV = 4, 64, 128
GLA_GATE_RANK = 16
GLA_TAU = 16.0
C_HEADS, C_DK, C_DV = 4, 64, 128
ROPE_BASE = 10000.0
D_HEADS, D_DIM = 8, 64
REL_BUCKETS = 32
REL_MAX_DIST = 128
SPLIT_AB = (512, 512, 512, 512, 512, 64, 8, 256, 256, 512, 512, 16)

LANES = 128
W_AB_USED = 4480
W_AB_PAD = 4608
I32_MIN = -2 ** 31
DUMMY = 7
IDX_BITS = 14
SETTLE_FROM_BIT = 22
LOG_F32_UNDERFLOW = -104.0
LOGIT_LIMIT = 40.0
NORM_SLACK = 1.05
TINY = 1e-30
BF16_BITS = -65536
CHUNK_GROUP = 10
SB_FIRST = 3
VMEM_LIMIT = 56 * 1024 * 1024

NT_DIMS = (((1,), (1,)), ((), ()))
TN_DIMS = (((0,), (0,)), ((), ()))


def _pick_tile(n, candidates):
    for c in candidates:
        if n % c == 0:
            return c
    raise ValueError(f"no tile for {n}")


def _params(sem):
    return pltpu.CompilerParams(dimension_semantics=sem, vmem_limit_bytes=VMEM_LIMIT)


def _silu(x):
    return x / (1.0 + jnp.exp(-x))


def _softplus_parts(z):
    t = jnp.log(1.0 + jnp.exp(-jnp.abs(z)))
    return jnp.maximum(z, 0.0) + t, jnp.minimum(z, 0.0) - t


def _split3(x):
    a = x.astype(BF16)
    r = x - a.astype(F32)
    b = r.astype(BF16)
    c = (r - b.astype(F32)).astype(BF16)
    return a, b, c


def _inproj_kernel(x_ref, g_ref, w_ref, o_ref, hn_ref):
    @pl.when(pl.program_id(2) == 0)
    def _():
        x = x_ref[...]
        ms = jnp.mean(x * x, axis=-1, keepdims=True)
        hn_ref[...] = (x * lax.rsqrt(ms + NORM_EPS) * g_ref[...]).astype(BF16)

    o_ref[...] = jnp.dot(hn_ref[...], w_ref[...], preferred_element_type=F32).astype(o_ref.dtype)


def _inproj(h, g, w):
    bsz, p, d = h.shape
    n = w.shape[1]
    tp = _pick_tile(p, (1664, 1280, 640, 128))
    tn = _pick_tile(n, (1792, 1536, 896, 512, 128))
    return pl.pallas_call(
        _inproj_kernel,
        grid=(bsz, p // tp, n // tn),
        in_specs=[
            pl.BlockSpec((None, tp, d), lambda b, i, j: (b, i, 0)),
            pl.BlockSpec((1, d), lambda b, i, j: (0, 0)),
            pl.BlockSpec((d, tn), lambda b, i, j: (0, j)),
        ],
        out_specs=pl.BlockSpec((None, tp, tn), lambda b, i, j: (b, i, j)),
        out_shape=jax.ShapeDtypeStruct((bsz, p, n), BF16),
        scratch_shapes=[pltpu.VMEM((tp, d), BF16)],
        compiler_params=_params(("parallel", "parallel", "arbitrary")),
        name="inproj",
    )(h, g.reshape(1, d), w)


def _proj_kernel(x_ref, w_ref, o_ref):
    o_ref[...] = jnp.dot(x_ref[...], w_ref[...], preferred_element_type=F32).astype(o_ref.dtype)


def _proj(hn, w):
    bsz, p, d = hn.shape
    n = w.shape[1]
    tp = _pick_tile(p, (1664, 1280, 640, 128))
    tn = _pick_tile(n, (1792, 1536, 896, 512, 128))
    return pl.pallas_call(
        _proj_kernel,
        grid=(bsz, p // tp, n // tn),
        in_specs=[
            pl.BlockSpec((None, tp, d), lambda b, i, j: (b, i, 0)),
            pl.BlockSpec((d, tn), lambda b, i, j: (0, j)),
        ],
        out_specs=pl.BlockSpec((None, tp, tn), lambda b, i, j: (b, i, j)),
        out_shape=jax.ShapeDtypeStruct((bsz, p, n), BF16),
        compiler_params=_params(("parallel", "parallel", "parallel")),
        name="proj",
    )(hn, w)


def _outproj_kernel(m1_ref, m2_ref, w1_ref, w2_ref, h_ref, g_ref, o_ref, hn_ref, *, tp):
    y = jnp.dot(m1_ref[...], w1_ref[...], preferred_element_type=F32)
    y = y + jnp.dot(m2_ref[...], w2_ref[...], preferred_element_type=F32)
    pos = pl.program_id(1) * tp + lax.broadcasted_iota(I32, y.shape, 0)
    x = h_ref[...] + jnp.where(pos >= PAD_FRONT, y, 0.0)
    o_ref[...] = x
    ms = jnp.mean(x * x, axis=-1, keepdims=True)
    hn_ref[...] = (x * lax.rsqrt(ms + NORM_EPS) * g_ref[...]).astype(BF16)


def _outproj(m1, m2, w_out, h, g_next):
    bsz, p, d = h.shape
    k1 = m1.shape[-1]
    k2 = m2.shape[-1]
    tp = _pick_tile(p, (832, 640, 128))
    w = w_out.astype(BF16)
    return pl.pallas_call(
        functools.partial(_outproj_kernel, tp=tp),
        grid=(bsz, p // tp),
        in_specs=[
            pl.BlockSpec((None, tp, k1), lambda b, i: (b, i, 0)),
            pl.BlockSpec((None, tp, k2), lambda b, i: (b, i, 0)),
            pl.BlockSpec((k1, d), lambda b, i: (0, 0)),
            pl.BlockSpec((k2, d), lambda b, i: (0, 0)),
            pl.BlockSpec((None, tp, d), lambda b, i: (b, i, 0)),
            pl.BlockSpec((1, d), lambda b, i: (0, 0)),
        ],
        out_specs=[pl.BlockSpec((None, tp, d), lambda b, i: (b, i, 0)),
                   pl.BlockSpec((None, tp, d), lambda b, i: (b, i, 0))],
        out_shape=[jax.ShapeDtypeStruct((bsz, p, d), F32), jax.ShapeDtypeStruct((bsz, p, d), BF16)],
        compiler_params=_params(("parallel", "parallel")),
        name="outproj",
    )(m1, m2, w[:k1], w[k1:], h, g_next.reshape(1, d))


def _outproj_norm_kernel(m1_ref, m2_ref, w1_ref, w2_ref, h_ref, g_ref, o_ref):
    y = jnp.dot(m1_ref[...], w1_ref[...], preferred_element_type=F32)
    y = y + jnp.dot(m2_ref[...], w2_ref[...], preferred_element_type=F32)
    x = h_ref[...] + y
    ms = jnp.mean(x * x, axis=-1, keepdims=True)
    o_ref[...] = x * lax.rsqrt(ms + NORM_EPS) * g_ref[...]


def _outproj_norm(m1, m2, w_out, h, g, seq):
    bsz, p, d = h.shape
    k1 = m1.shape[-1]
    k2 = m2.shape[-1]
    tp = LANES
    skip = (p - seq) // tp
    assert skip * tp == p - seq
    w = w_out.astype(BF16)
    rows = lambda b, i: (b, i + skip, 0)
    return pl.pallas_call(
        _outproj_norm_kernel,
        grid=(bsz, seq // tp),
        in_specs=[
            pl.BlockSpec((None, tp, k1), rows),
            pl.BlockSpec((None, tp, k2), rows),
            pl.BlockSpec((k1, d), lambda b, i: (0, 0)),
            pl.BlockSpec((k2, d), lambda b, i: (0, 0)),
            pl.BlockSpec((None, tp, d), rows),
            pl.BlockSpec((1, d), lambda b, i: (0, 0)),
        ],
        out_specs=pl.BlockSpec((None, tp, d), lambda b, i: (b, i, 0)),
        out_shape=jax.ShapeDtypeStruct((bsz, seq, d), F32),
        compiler_params=_params(("parallel", "parallel")),
        name="outproj_norm",
    )(m1, m2, w[:k1], w[k1:], h, g.reshape(1, d))


def _dsa_kernel(aq_ref, ag_ref, iq_ref, iw_ref, k_ref, v_ref, ik_ref, bias_ref, o_ref,
                key_ref, sa_ref, sb_ref, iqs_ref, wts_ref, qm_ref,
                jdx_ref, kmax_ref, mx_ref, ls_ref, acc_ref, *, topk):
    T = LANES
    i = pl.program_id(1)
    last = i + DUMMY

    def num_groups(group):
        return (i + group) // group

    def first_slot(group):
        return last + 1 - group * num_groups(group)

    row = lax.broadcasted_iota(I32, (T, T), 0)
    col = lax.broadcasted_iota(I32, (T, T), 1)
    qchunk = (i * T + row) >> CHUNK_SHIFT
    low = col < A_DIM

    def admissible(kt):
        kpos = kt * T + col
        return (kpos >= PAD_FRONT) & ((kpos >> CHUNK_SHIFT) <= qchunk)

    def tile_rows(slot):
        return pl.ds(pl.multiple_of(jnp.maximum(slot - DUMMY, 0) * T, T), T)

    iw = iw_ref[...].astype(F32)
    for h in range(IDX_HEADS):
        pair = iq_ref[:, (h // 2) * T:(h // 2 + 1) * T]
        mine = low if h % 2 == 0 else ~low
        iqs_ref[0, h * T:(h + 1) * T, :] = jnp.where(mine, pair, jnp.zeros_like(pair))
        wts_ref[h] = jnp.broadcast_to(iw[:, h:h + 1], (T, T))
    for h in range(A_HEADS):
        pair = aq_ref[:, (h // 2) * T:(h // 2 + 1) * T]
        mine = low if h % 2 == 0 else ~low
        qm_ref[h // 2, (h % 2) * T:(h % 2 + 1) * T, :] = jnp.where(
            mine, pair * (A_DIM ** -0.5), jnp.zeros_like(pair))

    def sweep(lhs_ref, rhs_ref, consume, consume_last):
        group_rows = lhs_ref.shape[1]

        def issue(s0, stage):
            rows = [tile_rows(s0), tile_rows(s0 + 1)]
            for g in range(lhs_ref.shape[0]):
                ps = slice(g * T, (g + 1) * T)
                kw = jnp.concatenate([rhs_ref[rows[0], ps], rhs_ref[rows[1], ps]], axis=0)
                stage[g * group_rows:(g + 1) * group_rows, :] = lax.dot_general(
                    lhs_ref[g], kw, NT_DIMS, preferred_element_type=F32)

        first = first_slot(4)
        issue(first, sa_ref)

        def quad(s):
            issue(s + 2, sb_ref)
            consume(s, sa_ref)
            issue(s + 4, sa_ref)
            consume(s + 2, sb_ref)

        inner = num_groups(4) - 1

        @pl.when(inner % 2 == 1)
        def _():
            quad(first)

        def body(q, carry):
            s = first + 4 * (inner % 2) + 8 * q
            quad(s)
            quad(s + 4)
            return carry

        lax.fori_loop(0, inner // 2, body, 0)
        issue(last - 1, sb_ref)
        consume(last - 3, sa_ref)
        consume_last(last - 1, sb_ref)

    def score_pair(s0, stage):
        for u in range(2):
            tot = jnp.zeros((T, T), F32)
            for h in range(IDX_HEADS):
                tot = tot + jnp.maximum(stage[h * T:(h + 1) * T, u * T:(u + 1) * T], 0.0) * wts_ref[h]
            score = jnp.where(admissible(s0 + u - DUMMY), tot * IDX_SCALE, NEG_INF)
            bits = lax.bitcast_convert_type(score, I32)
            key_ref[s0 + u] = bits ^ ((bits >> 31) & 0x7FFFFFFF)

    sweep(iqs_ref, ik_ref, score_pair, score_pair)
    for d in range(DUMMY):
        key_ref[d] = jnp.full((T, T), I32_MIN, I32)

    def row_count(pred):
        one = jnp.ones((T, T), I32)
        zero = jnp.zeros((T, T), I32)
        first = first_slot(8)

        def body(g, acc):
            hits = [jnp.where(pred(first + 8 * g + u), one, zero) for u in range(8)]
            while len(hits) > 1:
                hits = [a + b for a, b in zip(hits[::2], hits[1::2])]
            return acc + hits[0]

        acc = lax.fori_loop(0, num_groups(8), body, zero)
        return jnp.sum(acc.astype(F32), axis=-1, keepdims=True)

    def value_bit(it, state):
        t, reach = state
        cand = t + (jnp.int32(1) << (31 - it))
        wide = jnp.broadcast_to(cand, (T, T))
        cnt = row_count(lambda s: key_ref[s] >= wide)
        fits = cnt >= topk
        return jnp.where(fits, cand, t), jnp.where(fits, cnt, reach)

    everything = (8 * T * num_groups(8)).astype(F32)
    state = lax.fori_loop(0, SETTLE_FROM_BIT, value_bit, (jnp.full((T, 1), I32_MIN, I32),
                                                          jnp.full((T, 1), everything, F32)))

    wide = jnp.broadcast_to(state[0], (T, T))
    final = row_count(lambda s: key_ref[s] > wide) < topk

    def unsettled(carry):
        it, (_, reach) = carry
        return (it < 32) & (jnp.max(jnp.where(final, topk, reach)) > topk)

    def two_bits(carry):
        it, state = carry
        return it + 2, value_bit(it + 1, value_bit(it, state))

    _, (thr, reach) = lax.while_loop(unsettled, two_bits, (jnp.int32(SETTLE_FROM_BIT), state))
    thr = jnp.broadcast_to(thr, (T, T))
    jdx_ref[...] = jnp.full((T, T), 2 ** IDX_BITS, I32)

    @pl.when(jnp.max(reach) > topk)
    def _():
        need = topk - (reach - row_count(lambda s: key_ref[s] == thr))
        p_r = lax.broadcasted_iota(I32, (T, 2 * T), 0)
        p_c = lax.broadcasted_iota(I32, (T, 2 * T), 1)
        prefix_ones = jnp.where((p_c >= T) | (p_r <= p_c), 1.0, 0.0).astype(BF16)
        first = first_slot(8)

        def body(g, state):
            seen, best = state
            slots = [first + 8 * g + u for u in range(8)]
            tied = [key_ref[s] == thr for s in slots]
            sums = [jnp.dot(jnp.where(t, 1.0, 0.0).astype(BF16), prefix_ones,
                            preferred_element_type=F32) for t in tied]
            for s, t, sm in zip(slots, tied, sums):
                keep = t & (seen + sm[:, :T] <= need)
                best = jnp.maximum(best, jnp.where(keep, (s - DUMMY) * T + col, -1))
                seen = seen + sm[:, T:]
            return seen, best

        _, best = lax.fori_loop(0, num_groups(8), body,
                                (jnp.zeros((T, T), F32), jnp.full((T, T), -1, I32)))
        jdx_ref[...] = jnp.broadcast_to(jnp.max(best, axis=-1, keepdims=True), (T, T))

    jdx = jdx_ref[...]

    def selection_mask(s):
        key = key_ref[s]
        sel = (key > thr) | ((key == thr) & ((s - DUMMY) * T + col <= jdx))
        return jnp.where(sel & admissible(s - DUMMY), 0.0, NEG_INF)

    head_of_lane = (lax.broadcasted_iota(I32, (A_HEADS * A_DIM, T), 0) // A_DIM
                    == lax.broadcasted_iota(I32, (A_HEADS * A_DIM, T), 1))
    head_sum = jnp.where(head_of_lane, 1.0, 0.0).astype(BF16)

    def head_norms2(t):
        t = t.astype(F32)
        return jnp.dot((t * t).astype(BF16), head_sum, preferred_element_type=F32)

    @pl.when(i == 0)
    def _():
        def widest(t, best):
            return jnp.maximum(best, head_norms2(k_ref[pl.ds(pl.multiple_of(t * T, T), T), :]))
        best = lax.fori_loop(0, k_ref.shape[0] // T, widest, jnp.zeros((T, T), F32))
        kmax_ref[...] = jnp.broadcast_to(jnp.max(best, axis=0, keepdims=True), kmax_ref.shape)

    bound2 = head_norms2(aq_ref[...]) * kmax_ref[0:1, :] * (NORM_SLACK / A_DIM)
    room = LOGIT_LIMIT - jnp.max(jnp.abs(bias_ref[...]))
    small_logits = (room > 0.0) & (jnp.max(bound2) <= room * room)

    ls_ref[...] = jnp.zeros(ls_ref.shape, F32)
    acc_ref[...] = jnp.zeros(acc_ref.shape, F32)

    def attend(s0, stage, near, final):
        madd = [selection_mask(s0), selection_mask(s0 + 1)]
        for hp in range(A_HEADS // 2):
            weights = []
            for h in (2 * hp, 2 * hp + 1):
                parts = [stage[h * T:(h + 1) * T, u * T:(u + 1) * T] + madd[u] for u in range(2)]
                if near:
                    parts = [parts[u] + bias_ref[h, 1 - u] for u in range(2)]
                if not final:
                    mx_ref[h] = jnp.maximum(mx_ref[h], jnp.maximum(parts[0], parts[1]))
                else:
                    es = [jnp.exp(part - mx_ref[h]) for part in parts]
                    ls_ref[h] = ls_ref[h] + (es[0] + es[1])
                    weights.append(jnp.concatenate(es, axis=1).astype(BF16))
            if final:
                ps = slice(hp * T, (hp + 1) * T)
                vw = jnp.concatenate([v_ref[tile_rows(s0), ps], v_ref[tile_rows(s0 + 1), ps]], axis=0)
                acc_ref[hp] = acc_ref[hp] + jnp.dot(jnp.concatenate(weights, axis=0), vw,
                                                    preferred_element_type=F32)

    def attention_sweep(final):
        sweep(qm_ref, k_ref,
              functools.partial(attend, near=False, final=final),
              functools.partial(attend, near=True, final=final))

    @pl.when(small_logits)
    def _():
        mx_ref[...] = jnp.zeros(mx_ref.shape, F32)

    @pl.when(jnp.logical_not(small_logits))
    def _():
        mx_ref[...] = jnp.full(mx_ref.shape, NEG_INF, F32)
        attention_sweep(False)
        for h in range(A_HEADS):
            mx_ref[h] = jnp.broadcast_to(jnp.max(mx_ref[h], axis=-1, keepdims=True), (T, T))

    attention_sweep(True)

    def row_total(h):
        return jnp.maximum(jnp.sum(ls_ref[h], axis=-1, keepdims=True), TINY)

    for hp in range(A_HEADS // 2):
        o0 = acc_ref[hp, :T, :] / row_total(2 * hp)
        o1 = acc_ref[hp, T:, :] / row_total(2 * hp + 1)
        g = ag_ref[:, hp * T:(hp + 1) * T].astype(F32)
        o_ref[:, hp * T:(hp + 1) * T] = (jnp.where(low, o0, o1) * _silu(g)).astype(o_ref.dtype)


def _dsa(pab, bias_tiles, topk):
    bsz, p, _ = pab.shape
    T = LANES
    nq = p // T
    hw = A_HEADS * A_DIM
    one = pl.Buffered(1)
    return pl.pallas_call(
        functools.partial(_dsa_kernel, topk=float(topk)),
        grid=(bsz, nq),
        in_specs=[
            pl.BlockSpec((None, T, hw), lambda b, i: (b, i, 0)),
            pl.BlockSpec((None, T, hw), lambda b, i: (b, i, 3)),
            pl.BlockSpec((None, T, hw), lambda b, i: (b, i, 4)),
            pl.BlockSpec((None, T, T), lambda b, i: (b, i, 33)),
            pl.BlockSpec((None, p, hw), lambda b, i: (b, 0, 1)),
            pl.BlockSpec((None, p, hw), lambda b, i: (b, 0, 2)),
            pl.BlockSpec((None, p, T), lambda b, i: (b, 0, 32)),
            pl.BlockSpec((A_HEADS, 2, T, T), lambda b, i: (0, 0, 0, 0), pipeline_mode=one),
        ],
        out_specs=pl.BlockSpec((None, T, hw), lambda b, i: (b, i, 0)),
        out_shape=jax.ShapeDtypeStruct((bsz, p, hw), BF16),
        scratch_shapes=[
            pltpu.VMEM((nq + DUMMY, T, T), I32),
            pltpu.VMEM((A_HEADS * T, 2 * T), F32),
            pltpu.VMEM((A_HEADS * T, 2 * T), F32),
            pltpu.VMEM((1, IDX_HEADS * T, T), BF16),
            pltpu.VMEM((IDX_HEADS, T, T), F32),
            pltpu.VMEM((A_HEADS // 2, 2 * T, T), BF16),
            pltpu.VMEM((T, T), I32),
            pltpu.VMEM((8, T), F32),
            pltpu.VMEM((A_HEADS, T, T), F32),
            pltpu.VMEM((A_HEADS, T, T), F32),
            pltpu.VMEM((A_HEADS // 2, 2 * T, T), F32),
        ],
        compiler_params=_params(("parallel", "arbitrary")),
        name="dsa",
    )(pab, pab, pab, pab, pab, pab, pab, bias_tiles)


def _gla_kernel(q_ref, k_ref, v_ref, g_ref, a_ref, w2_ref, gb_ref, o_ref, st_ref, *, nchunk):
    C = CHUNK
    T = LANES

    @pl.when(pl.program_id(1) == 0)
    def _():
        st_ref[...] = jnp.zeros(st_ref.shape, F32)

    r_i = lax.broadcasted_iota(I32, (C, C), 0)
    c_i = lax.broadcasted_iota(I32, (C, C), 1)
    causal = c_i <= r_i
    tri = jnp.where(causal, 1.0, 0.0).astype(BF16)
    lane = lax.broadcasted_iota(I32, (C, T), 1)
    sd_r = lax.broadcasted_iota(I32, (2 * B_DV, T), 0)
    sd_c = lax.broadcasted_iota(I32, (2 * B_DV, T), 1)
    blockdiag = (sd_r >= B_DV) == (sd_c >= B_DK)

    pairs = range(B_HEADS // 2)
    lanes = [slice(hp * T, (hp + 1) * T) for hp in pairs]
    G = math.gcd(nchunk, CHUNK_GROUP)

    def group(gi, carry):
        rows = [pl.ds(pl.multiple_of((gi * G + c) * C, C), C) for c in range(G)]
        log_a = [_softplus_parts(jnp.dot(a_ref[r, :], w2_ref[...], preferred_element_type=F32)
                                 + gb_ref[...])[1] * (1.0 / GLA_TAU) for r in rows]
        terms = [_split3(x) for x in log_a]
        bcum = [sum(jnp.dot(tri, t, preferred_element_type=F32) for t in ts) for ts in terms]
        q_t, k_t, k_d, decay = [], [], [], []
        for r, b in zip(rows, bcum):
            b_last = b[C - 1:C, :]
            q = q_ref[r, :].astype(F32) * (B_DK ** -0.5)
            k = k_ref[r, :].astype(F32)
            q_t.append((q * jnp.exp(b)).astype(BF16))
            k_t.append((k * jnp.exp(-b)).astype(BF16))
            k_d.append((k * jnp.exp(b_last - b)).astype(BF16))
            decay.append(jnp.exp(b_last))
        att = {}
        for c in range(G):
            for h in range(B_HEADS):
                ls = lanes[h // 2]
                mine = (lane < B_DK) if h % 2 == 0 else (lane >= B_DK)
                qm = jnp.where(mine, q_t[c][:, ls], jnp.zeros((C, T), BF16))
                s = lax.dot_general(qm, k_t[c][:, ls], NT_DIMS, preferred_element_type=F32)
                att[c, h] = jnp.where(causal, s, 0.0).astype(BF16)
        contrib = {}
        for c in range(G):
            for hp in pairs:
                vp = v_ref[rows[c], hp * 2 * B_DV:(hp + 1) * 2 * B_DV]
                raw = lax.dot_general(vp, k_d[c][:, lanes[hp]], TN_DIMS, preferred_element_type=F32)
                contrib[c, hp] = jnp.where(blockdiag, raw, 0.0)
        before = {}
        for hp in pairs:
            st = st_ref[hp]
            for c in range(G):
                before[c, hp] = st
                st = st * decay[c][:, lanes[hp]] + contrib[c, hp]
            st_ref[hp] = st
        for c in range(G):
            for hp in pairs:
                o_inter = lax.dot_general(q_t[c][:, lanes[hp]], before[c, hp].astype(BF16), NT_DIMS,
                                          preferred_element_type=F32)
                for hh in range(2):
                    h = 2 * hp + hh
                    vs = slice(h * B_DV, (h + 1) * B_DV)
                    o = jnp.dot(att[c, h], v_ref[rows[c], vs], preferred_element_type=F32)
                    o = o + o_inter[:, hh * B_DV:(hh + 1) * B_DV]
                    o = o * lax.rsqrt(jnp.mean(o * o, axis=-1, keepdims=True) + NORM_EPS)
                    o = o * _silu(g_ref[rows[c], vs].astype(F32))
                    o_ref[rows[c], vs] = o.astype(o_ref.dtype)
        return carry

    lax.fori_loop(0, nchunk // G, group, 0)


def _gla(pab, w2, gb):
    bsz, p, _ = pab.shape
    tc = _pick_tile(p, (640, 128))
    hv = B_HEADS * B_DV
    hk = B_HEADS * B_DK
    return pl.pallas_call(
        functools.partial(_gla_kernel, nchunk=tc // CHUNK),
        grid=(bsz, p // tc),
        in_specs=[
            pl.BlockSpec((None, tc, hk), lambda b, c: (b, c, 14)),
            pl.BlockSpec((None, tc, hk), lambda b, c: (b, c, 15)),
            pl.BlockSpec((None, tc, hv), lambda b, c: (b, c, 5)),
            pl.BlockSpec((None, tc, hv), lambda b, c: (b, c, 6)),
            pl.BlockSpec((None, tc, LANES), lambda b, c: (b, c, 34)),
            pl.BlockSpec((LANES, hk), lambda b, c: (0, 0)),
            pl.BlockSpec((1, hk), lambda b, c: (0, 0)),
        ],
        out_specs=pl.BlockSpec((None, tc, hv), lambda b, c: (b, c, 0)),
        out_shape=jax.ShapeDtypeStruct((bsz, p, hv), BF16),
        scratch_shapes=[pltpu.VMEM((B_HEADS // 2, 2 * B_DV, 2 * B_DK), F32)],
        compiler_params=_params(("parallel", "arbitrary")),
        name="gla",
    )(pab, pab, pab, pab, pab, w2, gb)


def _ret_kernel(q_ref, k_ref, v_ref, g_ref, cos_ref, sin_ref, dmat_ref, zeta_ref, xi_ref,
                cdec_ref, o_ref, st_ref, *, nchunk):
    C = CHUNK
    T = LANES
    W = C_HEADS * C_DK

    @pl.when(pl.program_id(1) == 0)
    def _():
        st_ref[...] = jnp.zeros(st_ref.shape, F32)

    lane_w = lax.broadcasted_iota(I32, (C, W), 1)
    first_half = (lane_w & (C_DK - 1)) < (C_DK // 2)
    lane = lax.broadcasted_iota(I32, (C, T), 1)
    sd_r = lax.broadcasted_iota(I32, (2 * C_DV, T), 0)
    sd_c = lax.broadcasted_iota(I32, (2 * C_DV, T), 1)
    blockdiag = (sd_r >= C_DV) == (sd_c >= C_DK)

    def rotate(x, cos, sin_signed):
        swapped = jnp.where(first_half, pltpu.roll(x, W - C_DK // 2, 1), pltpu.roll(x, C_DK // 2, 1))
        return x * cos + swapped * sin_signed

    pairs = range(C_HEADS // 2)
    lanes = [slice(hp * T, (hp + 1) * T) for hp in pairs]
    G = math.gcd(nchunk, CHUNK_GROUP)

    def group(gi, carry):
        rows = [pl.ds(pl.multiple_of((gi * G + c) * C, C), C) for c in range(G)]
        q_b, k_b, q_x, k_z = [], [], [], []
        for r in rows:
            cos = cos_ref[r, :]
            sin = sin_ref[r, :]
            q = rotate(q_ref[r, :].astype(F32), cos, sin)
            k = rotate(k_ref[r, :].astype(F32), cos, sin) * (C_DK ** -0.5)
            q_b.append(q.astype(BF16))
            k_b.append(k.astype(BF16))
            q_x.append((q * xi_ref[...]).astype(BF16))
            k_z.append((k * zeta_ref[...]).astype(BF16))
        att = {}
        for c in range(G):
            for h in range(C_HEADS):
                ls = lanes[h // 2]
                mine = (lane < C_DK) if h % 2 == 0 else (lane >= C_DK)
                qm = jnp.where(mine, q_b[c][:, ls], jnp.zeros((C, T), BF16))
                s = lax.dot_general(qm, k_b[c][:, ls], NT_DIMS, preferred_element_type=F32)
                att[c, h] = (s * dmat_ref[h]).astype(BF16)
        contrib = {}
        for c in range(G):
            for hp in pairs:
                vp = v_ref[rows[c], hp * 2 * C_DV:(hp + 1) * 2 * C_DV]
                raw = lax.dot_general(vp, k_z[c][:, lanes[hp]], TN_DIMS, preferred_element_type=F32)
                contrib[c, hp] = jnp.where(blockdiag, raw, 0.0)
        before = {}
        for hp in pairs:
            st = st_ref[hp]
            for c in range(G):
                before[c, hp] = st
                st = st * cdec_ref[:, lanes[hp]] + contrib[c, hp]
            st_ref[hp] = st
        for c in range(G):
            for hp in pairs:
                o_inter = lax.dot_general(q_x[c][:, lanes[hp]], before[c, hp].astype(BF16), NT_DIMS,
                                          preferred_element_type=F32)
                for hh in range(2):
                    h = 2 * hp + hh
                    vs = slice(h * C_DV, (h + 1) * C_DV)
                    o = jnp.dot(att[c, h], v_ref[rows[c], vs], preferred_element_type=F32)
                    o = o + o_inter[:, hh * C_DV:(hh + 1) * C_DV]
                    o = o - jnp.mean(o, axis=-1, keepdims=True)
                    o = o * lax.rsqrt(jnp.mean(o * o, axis=-1, keepdims=True) + NORM_EPS)
                    o = o * _silu(g_ref[rows[c], vs].astype(F32))
                    o_ref[rows[c], vs] = o.astype(o_ref.dtype)
        return carry

    lax.fori_loop(0, nchunk // G, group, 0)


def _retention_tables(p):
    log_gamma = np.log(1.0 - np.exp2(-5.0 - np.arange(C_HEADS, dtype=np.float64)))
    i = np.arange(CHUNK, dtype=np.float64)
    diff = i[:, None] - i[None, :]
    dmat = np.where(diff >= 0, np.exp(log_gamma[:, None, None] * np.maximum(diff, 0.0)), 0.0)
    zeta = np.exp(log_gamma[:, None] * (CHUNK - 1 - i))
    xi = np.exp(log_gamma[:, None] * (i + 1))
    cdec = np.exp(log_gamma * CHUNK)
    widen = lambda t: np.repeat(t.T[:, :, None], C_DK, axis=2).reshape(CHUNK, C_HEADS * C_DK)
    half = C_DK // 2
    inv = jnp.asarray(ROPE_BASE, F32) ** (-jnp.arange(half, dtype=F32) / half)
    ang = jnp.arange(p, dtype=jnp.int32).astype(F32)[:, None] * inv[None, :]
    cos = jnp.tile(jnp.cos(ang), (1, 2 * C_HEADS))
    sin = jnp.sin(ang)
    sin_signed = jnp.tile(jnp.concatenate([-sin, sin], axis=1), (1, C_HEADS))
    return (cos, sin_signed, jnp.asarray(dmat, F32), jnp.asarray(widen(zeta), F32),
            jnp.asarray(widen(xi), F32),
            jnp.asarray(np.repeat(cdec, C_DK)[None, :], F32))


def _retention(pcd, tables):
    bsz, p, _ = pcd.shape
    cos, sin, dmat, zeta, xi, cdec = tables
    tc = _pick_tile(p, (640, 128))
    hv = C_HEADS * C_DV
    hk = C_HEADS * C_DK
    full = lambda shape: pl.BlockSpec(shape, lambda b, c: (0,) * len(shape))
    return pl.pallas_call(
        functools.partial(_ret_kernel, nchunk=tc // CHUNK),
        grid=(bsz, p // tc),
        in_specs=[
            pl.BlockSpec((None, tc, hk), lambda b, c: (b, c, 0)),
            pl.BlockSpec((None, tc, hk), lambda b, c: (b, c, 1)),
            pl.BlockSpec((None, tc, hv), lambda b, c: (b, c, 1)),
            pl.BlockSpec((None, tc, hv), lambda b, c: (b, c, 2)),
            pl.BlockSpec((tc, hk), lambda b, c: (c, 0)),
            pl.BlockSpec((tc, hk), lambda b, c: (c, 0)),
            full((C_HEADS, CHUNK, CHUNK)),
            full((CHUNK, hk)),
            full((CHUNK, hk)),
            full((1, hk)),
        ],
        out_specs=pl.BlockSpec((None, tc, hv), lambda b, c: (b, c, 0)),
        out_shape=jax.ShapeDtypeStruct((bsz, p, hv), BF16),
        scratch_shapes=[pltpu.VMEM((C_HEADS // 2, 2 * C_DV, 2 * C_DK), F32)],
        compiler_params=_params(("parallel", "arbitrary")),
        name="retention",
    )(pcd, pcd, pcd, pcd, cos, sin, dmat, zeta, xi, cdec)


def _sb_kernel(q_ref, g_ref, k_ref, v_ref, o_ref, qm_ref, run_ref, acc_ref, hi_ref, lo_ref, ls_ref,
               w_ref):
    T = LANES
    i = pl.program_id(1)
    row = lax.broadcasted_iota(I32, (T, T), 0)
    col = lax.broadcasted_iota(I32, (T, T), 1)
    qpos = i * T + row
    low = col < D_DIM
    u_r = lax.broadcasted_iota(I32, (T, 2 * T), 0)
    u_c = lax.broadcasted_iota(I32, (T, 2 * T), 1)
    suffix = jnp.where((u_c >= T) | (u_r > u_c), 1.0, 0.0).astype(BF16)

    for h in range(D_HEADS):
        pair = q_ref[:, (h // 2) * T:(h // 2 + 1) * T]
        mine = low if h % 2 == 0 else ~low
        qm_ref[h] = jnp.where(mine, pair * (D_DIM ** -0.5), jnp.zeros_like(pair))
    run_ref[...] = jnp.zeros(run_ref.shape, F32)
    acc_ref[...] = jnp.zeros(acc_ref.shape, F32)

    def walk(t0, count):
        kts = [i - t0 - j for j in range(count)]
        rows = [pl.ds(pl.multiple_of(jnp.maximum(kt, 0) * T, T), T) for kt in kts]
        for j, kt in enumerate(kts):
            kpos = kt * T + col
            ok = (kpos < qpos) & (kpos >= PAD_FRONT)
            for h in range(D_HEADS):
                ps = slice((h // 2) * T, (h // 2 + 1) * T)
                z = lax.dot_general(qm_ref[h], k_ref[rows[j], ps], NT_DIMS, preferred_element_type=F32)
                sp, logsig = _softplus_parts(z)
                log_1m = jnp.where(ok, -sp, 0.0)
                hi = lax.bitcast_convert_type(lax.bitcast_convert_type(log_1m, I32) & BF16_BITS, F32)
                hi_ref[j, h] = hi.astype(BF16)
                lo_ref[j, h] = (log_1m - hi).astype(BF16)
                ls_ref[j, h] = jnp.where(ok, logsig, NEG_INF)
        sums = {}
        for j in range(count):
            for h in range(D_HEADS):
                sums[j, h] = (jnp.dot(hi_ref[j, h], suffix, preferred_element_type=F32)
                              + jnp.dot(lo_ref[j, h], suffix, preferred_element_type=F32))
        slowest = None
        for h in range(D_HEADS):
            run = run_ref[h]
            for j in range(count):
                w_ref[j, h] = jnp.exp(ls_ref[j, h] + run + sums[j, h][:, :T]).astype(BF16)
                run = run + sums[j, h][:, T:]
            run_ref[h] = run
            slowest = run if slowest is None else jnp.maximum(slowest, run)
        for h in range(D_HEADS):
            ps = slice((h // 2) * T, (h // 2 + 1) * T)
            acc = acc_ref[h]
            for j in range(count):
                acc = acc + jnp.dot(w_ref[j, h], v_ref[rows[j], ps], preferred_element_type=F32)
            acc_ref[h] = acc
        return (jnp.max(slowest) > LOG_F32_UNDERFLOW).astype(I32)

    def one_more(state):
        t, _ = state
        return t + 1, walk(t, 1)

    lax.while_loop(lambda s: (s[0] <= i) & (s[1] > 0), one_more, (jnp.int32(SB_FIRST), walk(0, SB_FIRST)))

    for hp in range(D_HEADS // 2):
        ps = slice(hp * T, (hp + 1) * T)
        o = jnp.where(low, acc_ref[2 * hp], acc_ref[2 * hp + 1]) * _silu(g_ref[:, ps].astype(F32))
        o_ref[:, ps] = o.astype(o_ref.dtype)


def _stick_breaking(pcd):
    bsz, p, _ = pcd.shape
    T = LANES
    hw = D_HEADS * D_DIM
    return pl.pallas_call(
        _sb_kernel,
        grid=(bsz, p // T),
        in_specs=[
            pl.BlockSpec((None, T, hw), lambda b, i: (b, i, 3)),
            pl.BlockSpec((None, T, hw), lambda b, i: (b, i, 6)),
            pl.BlockSpec((None, p, hw), lambda b, i: (b, 0, 4)),
            pl.BlockSpec((None, p, hw), lambda b, i: (b, 0, 5)),
        ],
        out_specs=pl.BlockSpec((None, T, hw), lambda b, i: (b, i, 0)),
        out_shape=jax.ShapeDtypeStruct((bsz, p, hw), BF16),
        scratch_shapes=[
            pltpu.VMEM((D_HEADS, T, T), BF16),
            pltpu.VMEM((D_HEADS, T, T), F32),
            pltpu.VMEM((D_HEADS, T, T), F32),
            pltpu.VMEM((SB_FIRST, D_HEADS, T, T), BF16),
            pltpu.VMEM((SB_FIRST, D_HEADS, T, T), BF16),
            pltpu.VMEM((SB_FIRST, D_HEADS, T, T), F32),
            pltpu.VMEM((SB_FIRST, D_HEADS, T, T), BF16),
        ],
        compiler_params=_params(("parallel", "arbitrary")),
        name="stick_breaking",
    )(pcd, pcd, pcd, pcd)


def _rel_bucket_np(rel):
    half = REL_BUCKETS // 2
    max_exact = half // 2
    n = -rel
    ret = np.where(n < 0, half, 0)
    n = np.abs(n)
    edges = [math.ceil(max_exact * (REL_MAX_DIST / max_exact) ** (j / (half - max_exact)) - 1e-9)
             for j in range(1, half - max_exact)]
    large = max_exact + sum((n >= e).astype(np.int64) for e in edges)
    return ret + np.where(n < max_exact, n, large)


def _bias_tiles(rel_bias):
    i = np.arange(LANES)[:, None]
    j = np.arange(LANES)[None, :]
    idx = np.stack([_rel_bucket_np(j - i - LANES * d) for d in range(3)])
    assert (idx[2] == idx[2, 0, 0]).all()
    table = rel_bias.astype(F32) - rel_bias.astype(F32)[idx[2, 0, 0]]
    near = jnp.asarray(idx[:2], I32)[None]
    tiles = jnp.zeros((table.shape[1],) + near.shape[1:], F32)
    for bucket in range(REL_BUCKETS):
        tiles = jnp.where(near == bucket, table[bucket][:, None, None, None], tiles)
    return tiles


def _layout_ab(w):
    aq, ak, av, ag, iq, ik, iw, bq, bk, bv, bg, ba = jnp.split(w, np.cumsum(SPLIT_AB)[:-1].tolist(), axis=1)
    pad = lambda t: jnp.pad(t, ((0, 0), (0, LANES - t.shape[1])))
    cols = [aq, ak, av, ag, iq, bv, bg, bq, bk, jnp.concatenate([ik, ik], axis=1), pad(iw), pad(ba),
            jnp.zeros((w.shape[0], W_AB_PAD - W_AB_USED), w.dtype)]
    return jnp.concatenate(cols, axis=1).astype(BF16)


def kernel(x, meta_tokens, rel_bias, norm_g, final_g, w_in_ab, gla_gate_w2, gla_gate_b, w_out_ab,
           w_in_cd, w_out_cd):
    bsz, seq, d = x.shape
    p = seq + PAD_FRONT + N_META
    depth = norm_g.shape[0]
    topk = min(TOPK_MAX, seq // 4)
    h = jnp.concatenate([jnp.zeros((bsz, PAD_FRONT, d), x.dtype),
                         jnp.broadcast_to(meta_tokens.astype(x.dtype)[None], (bsz, N_META, d)),
                         x], axis=1)
    bias_tiles = _bias_tiles(rel_bias)
    tables = _retention_tables(p)
    assert depth >= 1
    hn = None
    for layer in range(depth):
        j = layer // 2
        w_in = _layout_ab(w_in_ab[j]) if layer % 2 == 0 else w_in_cd[j].astype(BF16)
        proj = _inproj(h, norm_g[layer], w_in) if hn is None else _proj(hn, w_in)
        if layer % 2 == 0:
            first = _dsa(proj, bias_tiles, topk)
            w2 = jnp.pad(gla_gate_w2[j], ((0, LANES - GLA_GATE_RANK), (0, 0))).astype(BF16)
            second = _gla(proj, w2, gla_gate_b[j].reshape(1, -1).astype(F32))
            w_out = w_out_ab[j]
        else:
            first = _retention(proj, tables)
            second = _stick_breaking(proj)
            w_out = w_out_cd[j]
        if layer + 1 < depth:
            h, hn = _outproj(first, second, w_out, h, norm_g[layer + 1])
    return _outproj_norm(first, second, w_out, h, final_g, seq)
```

```python
import functools
import math

import numpy as np
import jax
import jax.numpy as jnp
from jax import lax
from jax.experimental import pallas as pl
from jax.experimental.pallas import tpu as pltpu

F32 = jnp.float32
BF16 = jnp.bfloat16
I32 = jnp.int32

CHUNK = 64
CHUNK_SHIFT = CHUNK.bit_length() - 1
N_META = 16
PAD_FRONT = 128 - N_META
NORM_EPS = 1e-6
NEG_INF = -1e30
A_HEADS, A_DIM = 8, 64
IDX_HEADS, IDX_DIM = 8, 64
IDX_SCALE = (IDX_DIM ** -0.5) * (IDX_HEADS ** -0.5)
TOPK_MAX = 256
B_HEADS, B_DK, B_DV = 4, 64, 128
GLA_GATE_RANK = 16
GLA_TAU = 16.0
C_HEADS, C_DK, C_DV = 4, 64, 128
ROPE_BASE = 10000.0
D_HEADS, D_DIM = 8, 64
REL_BUCKETS = 32
REL_MAX_DIST = 128
SPLIT_AB = (512, 512, 512, 512, 512, 64, 8, 256, 256, 512, 512, 16)

LANES = 128
W_AB_USED = 4480
W_AB_PAD = 4608
I32_MIN = -2 ** 31
DUMMY = 7
IDX_BITS = 14
SETTLE_FROM_BIT = 22
LOG_F32_UNDERFLOW = -104.0
LOGIT_LIMIT = 40.0
NORM_SLACK = 1.05
TINY = 1e-30
BF16_BITS = -65536
CHUNK_GROUP = 10
SB_FIRST = 3
VMEM_LIMIT = 56 * 1024 * 1024

NT_DIMS = (((1,), (1,)), ((), ()))
TN_DIMS = (((0,), (0,)), ((), ()))


def _pick_tile(n, candidates):
    for c in candidates:
        if n % c == 0:
            return c
    raise ValueError(f"no tile for {n}")


def _params(sem):
    return pltpu.CompilerParams(dimension_semantics=sem, vmem_limit_bytes=VMEM_LIMIT)


def _silu(x):
    return x / (1.0 + jnp.exp(-x))


def _softplus_parts(z):
    t = jnp.log(1.0 + jnp.exp(-jnp.abs(z)))
    return jnp.maximum(z, 0.0) + t, jnp.minimum(z, 0.0) - t


def _split3(x):
    a = x.astype(BF16)
    r = x - a.astype(F32)
    b = r.astype(BF16)
    c = (r - b.astype(F32)).astype(BF16)
    return a, b, c


def _inproj_kernel(x_ref, g_ref, w_ref, o_ref, hn_ref):
    @pl.when(pl.program_id(2) == 0)
    def _():
        x = x_ref[...]
        ms = jnp.mean(x * x, axis=-1, keepdims=True)
        hn_ref[...] = (x * lax.rsqrt(ms + NORM_EPS) * g_ref[...]).astype(BF16)

    o_ref[...] = jnp.dot(hn_ref[...], w_ref[...], preferred_element_type=F32).astype(o_ref.dtype)


def _inproj(h, g, w):
    bsz, p, d = h.shape
    n = w.shape[1]
    tp = _pick_tile(p, (1664, 1280, 640, 128))
    tn = _pick_tile(n, (1792, 1536, 896, 512, 128))
    return pl.pallas_call(
        _inproj_kernel,
        grid=(bsz, p // tp, n // tn),
        in_specs=[
            pl.BlockSpec((None, tp, d), lambda b, i, j: (b, i, 0)),
            pl.BlockSpec((1, d), lambda b, i, j: (0, 0)),
            pl.BlockSpec((d, tn), lambda b, i, j: (0, j)),
        ],
        out_specs=pl.BlockSpec((None, tp, tn), lambda b, i, j: (b, i, j)),
        out_shape=jax.ShapeDtypeStruct((bsz, p, n), BF16),
        scratch_shapes=[pltpu.VMEM((tp, d), BF16)],
        compiler_params=_params(("parallel", "parallel", "arbitrary")),
        name="inproj",
    )(h, g.reshape(1, d), w)


def _proj_kernel(x_ref, w_ref, o_ref):
    o_ref[...] = jnp.dot(x_ref[...], w_ref[...], preferred_element_type=F32).astype(o_ref.dtype)


def _proj(hn, w):
    bsz, p, d = hn.shape
    n = w.shape[1]
    tp = _pick_tile(p, (832, 640, 128))
    tn = _pick_tile(n, (1792, 1536, 896, 512, 128))
    return pl.pallas_call(
        _proj_kernel,
        grid=(n // tn, bsz, p // tp),
        in_specs=[
            pl.BlockSpec((None, tp, d), lambda j, b, i: (b, i, 0)),
            pl.BlockSpec((d, tn), lambda j, b, i: (0, j)),
        ],
        out_specs=pl.BlockSpec((None, tp, tn), lambda j, b, i: (b, i, j)),
        out_shape=jax.ShapeDtypeStruct((bsz, p, n), BF16),
        compiler_params=_params(("parallel", "parallel", "parallel")),
        name="proj",
    )(hn, w)


def _outproj_kernel(m1_ref, m2_ref, w1_ref, w2_ref, h_ref, g_ref, o_ref, hn_ref, *, tp):
    y = jnp.dot(m1_ref[...], w1_ref[...], preferred_element_type=F32)
    y = y + jnp.dot(m2_ref[...], w2_ref[...], preferred_element_type=F32)
    pos = pl.program_id(1) * tp + lax.broadcasted_iota(I32, y.shape, 0)
    x = h_ref[...] + jnp.where(pos >= PAD_FRONT, y, 0.0)
    o_ref[...] = x
    ms = jnp.mean(x * x, axis=-1, keepdims=True)
    hn_ref[...] = (x * lax.rsqrt(ms + NORM_EPS) * g_ref[...]).astype(BF16)


def _outproj(m1, m2, w_out, h, g_next):
    bsz, p, d = h.shape
    k1 = m1.shape[-1]
    k2 = m2.shape[-1]
    tp = _pick_tile(p, (832, 640, 128))
    w = w_out.astype(BF16)
    return pl.pallas_call(
        functools.partial(_outproj_kernel, tp=tp),
        grid=(bsz, p // tp),
        in_specs=[
            pl.BlockSpec((None, tp, k1), lambda b, i: (b, i, 0)),
            pl.BlockSpec((None, tp, k2), lambda b, i: (b, i, 0)),
            pl.BlockSpec((k1, d), lambda b, i: (0, 0)),
            pl.BlockSpec((k2, d), lambda b, i: (0, 0)),
            pl.BlockSpec((None, tp, d), lambda b, i: (b, i, 0)),
            pl.BlockSpec((1, d), lambda b, i: (0, 0)),
        ],
        out_specs=[pl.BlockSpec((None, tp, d), lambda b, i: (b, i, 0)),
                   pl.BlockSpec((None, tp, d), lambda b, i: (b, i, 0))],
        out_shape=[jax.ShapeDtypeStruct((bsz, p, d), F32), jax.ShapeDtypeStruct((bsz, p, d), BF16)],
        compiler_params=_params(("parallel", "parallel")),
        name="outproj",
    )(m1, m2, w[:k1], w[k1:], h, g_next.reshape(1, d))


def _outproj_norm_kernel(m1_ref, m2_ref, w1_ref, w2_ref, h_ref, g_ref, o_ref):
    y = jnp.dot(m1_ref[...], w1_ref[...], preferred_element_type=F32)
    y = y + jnp.dot(m2_ref[...], w2_ref[...], preferred_element_type=F32)
    x = h_ref[...] + y
    ms = jnp.mean(x * x, axis=-1, keepdims=True)
    o_ref[...] = x * lax.rsqrt(ms + NORM_EPS) * g_ref[...]


def _outproj_norm(m1, m2, w_out, h, g, seq):
    bsz, p, d = h.shape
    k1 = m1.shape[-1]
    k2 = m2.shape[-1]
    tp = LANES
    skip = (p - seq) // tp
    assert skip * tp == p - seq
    w = w_out.astype(BF16)
    rows = lambda b, i: (b, i + skip, 0)
    return pl.pallas_call(
        _outproj_norm_kernel,
        grid=(bsz, seq // tp),
        in_specs=[
            pl.BlockSpec((None, tp, k1), rows),
            pl.BlockSpec((None, tp, k2), rows),
            pl.BlockSpec((k1, d), lambda b, i: (0, 0)),
            pl.BlockSpec((k2, d), lambda b, i: (0, 0)),
            pl.BlockSpec((None, tp, d), rows),
            pl.BlockSpec((1, d), lambda b, i: (0, 0)),
        ],
        out_specs=pl.BlockSpec((None, tp, d), lambda b, i: (b, i, 0)),
        out_shape=jax.ShapeDtypeStruct((bsz, seq, d), F32),
        compiler_params=_params(("parallel", "parallel")),
        name="outproj_norm",
    )(m1, m2, w[:k1], w[k1:], h, g.reshape(1, d))


def _dsa_kernel(aq_ref, ag_ref, iq_ref, iw_ref, k_ref, v_ref, ik_ref, bias_ref, o_ref,
                key_ref, sa_ref, sb_ref, iqs_ref, wts_ref, qm_ref,
                jdx_ref, kmax_ref, mx_ref, ls_ref, acc_ref, *, topk):
    T = LANES
    i = pl.program_id(1)
    last = i + DUMMY

    def num_groups(group):
        return (i + group) // group

    def first_slot(group):
        return last + 1 - group * num_groups(group)

    row = lax.broadcasted_iota(I32, (T, T), 0)
    col = lax.broadcasted_iota(I32, (T, T), 1)
    qchunk = (i * T + row) >> CHUNK_SHIFT
    low = col < A_DIM

    def admissible(kt):
        kpos = kt * T + col
        return (kpos >= PAD_FRONT) & ((kpos >> CHUNK_SHIFT) <= qchunk)

    def tile_rows(slot):
        return pl.ds(pl.multiple_of(jnp.maximum(slot - DUMMY, 0) * T, T), T)

    iw = iw_ref[...].astype(F32)
    for h in range(IDX_HEADS):
        pair = iq_ref[:, (h // 2) * T:(h // 2 + 1) * T]
        mine = low if h % 2 == 0 else ~low
        iqs_ref[0, h * T:(h + 1) * T, :] = jnp.where(mine, pair, jnp.zeros_like(pair))
        wts_ref[h] = jnp.broadcast_to(iw[:, h:h + 1], (T, T))
    for h in range(A_HEADS):
        pair = aq_ref[:, (h // 2) * T:(h // 2 + 1) * T]
        mine = low if h % 2 == 0 else ~low
        qm_ref[h // 2, (h % 2) * T:(h % 2 + 1) * T, :] = jnp.where(
            mine, pair * (A_DIM ** -0.5), jnp.zeros_like(pair))

    def sweep(lhs_ref, rhs_ref, consume, consume_last):
        group_rows = lhs_ref.shape[1]

        def issue(s0, stage):
            rows = [tile_rows(s0), tile_rows(s0 + 1)]
            for g in range(lhs_ref.shape[0]):
                ps = slice(g * T, (g + 1) * T)
                kw = jnp.concatenate([rhs_ref[rows[0], ps], rhs_ref[rows[1], ps]], axis=0)
                stage[g * group_rows:(g + 1) * group_rows, :] = lax.dot_general(
                    lhs_ref[g], kw, NT_DIMS, preferred_element_type=F32)

        first = first_slot(4)
        issue(first, sa_ref)

        def quad(s):
            issue(s + 2, sb_ref)
            consume(s, sa_ref)
            issue(s + 4, sa_ref)
            consume(s + 2, sb_ref)

        inner = num_groups(4) - 1

        @pl.when(inner % 2 == 1)
        def _():
            quad(first)

        def body(q, carry):
            s = first + 4 * (inner % 2) + 8 * q
            quad(s)
            quad(s + 4)
            return carry

        lax.fori_loop(0, inner // 2, body, 0)
        issue(last - 1, sb_ref)
        consume(last - 3, sa_ref)
        consume_last(last - 1, sb_ref)

    def score_pair(s0, stage):
        for u in range(2):
            tot = jnp.zeros((T, T), F32)
            for h in range(IDX_HEADS):
                tot = tot + jnp.maximum(stage[h * T:(h + 1) * T, u * T:(u + 1) * T], 0.0) * wts_ref[h]
            score = jnp.where(admissible(s0 + u - DUMMY), tot * IDX_SCALE, NEG_INF)
            bits = lax.bitcast_convert_type(score, I32)
            key_ref[s0 + u] = bits ^ ((bits >> 31) & 0x7FFFFFFF)

    sweep(iqs_ref, ik_ref, score_pair, score_pair)
    for d in range(DUMMY):
        key_ref[d] = jnp.full((T, T), I32_MIN, I32)

    def row_count(pred):
        one = jnp.ones((T, T), I32)
        zero = jnp.zeros((T, T), I32)
        first = first_slot(8)

        def body(g, acc):
            hits = [jnp.where(pred(first + 8 * g + u), one, zero) for u in range(8)]
            while len(hits) > 1:
                hits = [a + b for a, b in zip(hits[::2], hits[1::2])]
            return acc + hits[0]

        acc = lax.fori_loop(0, num_groups(8), body, zero)
        return jnp.sum(acc.astype(F32), axis=-1, keepdims=True)

    def value_bit(it, state):
        t, reach = state
        cand = t + (jnp.int32(1) << (31 - it))
        wide = jnp.broadcast_to(cand, (T, T))
        cnt = row_count(lambda s: key_ref[s] >= wide)
        fits = cnt >= topk
        return jnp.where(fits, cand, t), jnp.where(fits, cnt, reach)

    everything = (8 * T * num_groups(8)).astype(F32)
    state = lax.fori_loop(0, SETTLE_FROM_BIT, value_bit, (jnp.full((T, 1), I32_MIN, I32),
                                                          jnp.full((T, 1), everything, F32)))

    wide = jnp.broadcast_to(state[0], (T, T))
    final = row_count(lambda s: key_ref[s] > wide) < topk

    def unsettled(carry):
        it, (_, reach) = carry
        return (it < 32) & (jnp.max(jnp.where(final, topk, reach)) > topk)

    def two_bits(carry):
        it, state = carry
        return it + 2, value_bit(it + 1, value_bit(it, state))

    _, (thr, reach) = lax.while_loop(unsettled, two_bits, (jnp.int32(SETTLE_FROM_BIT), state))
    thr = jnp.broadcast_to(thr, (T, T))
    jdx_ref[...] = jnp.full((T, T), 2 ** IDX_BITS, I32)

    @pl.when(jnp.max(reach) > topk)
    def _():
        need = topk - (reach - row_count(lambda s: key_ref[s] == thr))
        p_r = lax.broadcasted_iota(I32, (T, 2 * T), 0)
        p_c = lax.broadcasted_iota(I32, (T, 2 * T), 1)
        prefix_ones = jnp.where((p_c >= T) | (p_r <= p_c), 1.0, 0.0).astype(BF16)
        first = first_slot(8)

        def body(g, state):
            seen, best = state
            slots = [first + 8 * g + u for u in range(8)]
            tied = [key_ref[s] == thr for s in slots]
            sums = [jnp.dot(jnp.where(t, 1.0, 0.0).astype(BF16), prefix_ones,
                            preferred_element_type=F32) for t in tied]
            for s, t, sm in zip(slots, tied, sums):
                keep = t & (seen + sm[:, :T] <= need)
                best = jnp.maximum(best, jnp.where(keep, (s - DUMMY) * T + col, -1))
                seen = seen + sm[:, T:]
            return seen, best

        _, best = lax.fori_loop(0, num_groups(8), body,
                                (jnp.zeros((T, T), F32), jnp.full((T, T), -1, I32)))
        jdx_ref[...] = jnp.broadcast_to(jnp.max(best, axis=-1, keepdims=True), (T, T))

    jdx = jdx_ref[...]

    def selection_mask(s):
        key = key_ref[s]
        sel = (key > thr) | ((key == thr) & ((s - DUMMY) * T + col <= jdx))
        return jnp.where(sel & admissible(s - DUMMY), 0.0, NEG_INF)

    head_of_lane = (lax.broadcasted_iota(I32, (A_HEADS * A_DIM, T), 0) // A_DIM
                    == lax.broadcasted_iota(I32, (A_HEADS * A_DIM, T), 1))
    head_sum = jnp.where(head_of_lane, 1.0, 0.0).astype(BF16)

    def head_norms2(t):
        t = t.astype(F32)
        return jnp.dot((t * t).astype(BF16), head_sum, preferred_element_type=F32)

    @pl.when(i == 0)
    def _():
        def widest(t, best):
            return jnp.maximum(best, head_norms2(k_ref[pl.ds(pl.multiple_of(t * T, T), T), :]))
        best = lax.fori_loop(0, k_ref.shape[0] // T, widest, jnp.zeros((T, T), F32))
        kmax_ref[...] = jnp.broadcast_to(jnp.max(best, axis=0, keepdims=True), kmax_ref.shape)

    bound2 = head_norms2(aq_ref[...]) * kmax_ref[0:1, :] * (NORM_SLACK / A_DIM)
    room = LOGIT_LIMIT - jnp.max(jnp.abs(bias_ref[...]))
    small_logits = (room > 0.0) & (jnp.max(bound2) <= room * room)

    ls_ref[...] = jnp.zeros(ls_ref.shape, F32)
    acc_ref[...] = jnp.zeros(acc_ref.shape, F32)

    def attend(s0, stage, near, final):
        madd = [selection_mask(s0), selection_mask(s0 + 1)]
        for hp in range(A_HEADS // 2):
            weights = []
            for h in (2 * hp, 2 * hp + 1):
                parts = [stage[h * T:(h + 1) * T, u * T:(u + 1) * T] + madd[u] for u in range(2)]
                if near:
                    parts = [parts[u] + bias_ref[h, 1 - u] for u in range(2)]
                if not final:
                    mx_ref[h] = jnp.maximum(mx_ref[h], jnp.maximum(parts[0], parts[1]))
                else:
                    es = [jnp.exp(part - mx_ref[h]) for part in parts]
                    ls_ref[h] = ls_ref[h] + (es[0] + es[1])
                    weights.append(jnp.concatenate(es, axis=1).astype(BF16))
            if final:
                ps = slice(hp * T, (hp + 1) * T)
                vw = jnp.concatenate([v_ref[tile_rows(s0), ps], v_ref[tile_rows(s0 + 1), ps]], axis=0)
                acc_ref[hp] = acc_ref[hp] + jnp.dot(jnp.concatenate(weights, axis=0), vw,
                                                    preferred_element_type=F32)

    def attention_sweep(final):
        sweep(qm_ref, k_ref,
              functools.partial(attend, near=False, final=final),
              functools.partial(attend, near=True, final=final))

    @pl.when(small_logits)
    def _():
        mx_ref[...] = jnp.zeros(mx_ref.shape, F32)

    @pl.when(jnp.logical_not(small_logits))
    def _():
        mx_ref[...] = jnp.full(mx_ref.shape, NEG_INF, F32)
        attention_sweep(False)
        for h in range(A_HEADS):
            mx_ref[h] = jnp.broadcast_to(jnp.max(mx_ref[h], axis=-1, keepdims=True), (T, T))

    attention_sweep(True)

    def row_total(h):
        return jnp.maximum(jnp.sum(ls_ref[h], axis=-1, keepdims=True), TINY)

    for hp in range(A_HEADS // 2):
        o0 = acc_ref[hp, :T, :] / row_total(2 * hp)
        o1 = acc_ref[hp, T:, :] / row_total(2 * hp + 1)
        g = ag_ref[:, hp * T:(hp + 1) * T].astype(F32)
        o_ref[:, hp * T:(hp + 1) * T] = (jnp.where(low, o0, o1) * _silu(g)).astype(o_ref.dtype)


def _dsa(pab, bias_tiles, topk):
    bsz, p, _ = pab.shape
    T = LANES
    nq = p // T
    hw = A_HEADS * A_DIM
    one = pl.Buffered(1)
    return pl.pallas_call(
        functools.partial(_dsa_kernel, topk=float(topk)),
        grid=(bsz, nq),
        in_specs=[
            pl.BlockSpec((None, T, hw), lambda b, i: (b, i, 0)),
            pl.BlockSpec((None, T, hw), lambda b, i: (b, i, 3)),
            pl.BlockSpec((None, T, hw), lambda b, i: (b, i, 4)),
            pl.BlockSpec((None, T, T), lambda b, i: (b, i, 33)),
            pl.BlockSpec((None, p, hw), lambda b, i: (b, 0, 1)),
            pl.BlockSpec((None, p, hw), lambda b, i: (b, 0, 2)),
            pl.BlockSpec((None, p, T), lambda b, i: (b, 0, 32)),
            pl.BlockSpec((A_HEADS, 2, T, T), lambda b, i: (0, 0, 0, 0), pipeline_mode=one),
        ],
        out_specs=pl.BlockSpec((None, T, hw), lambda b, i: (b, i, 0)),
        out_shape=jax.ShapeDtypeStruct((bsz, p, hw), BF16),
        scratch_shapes=[
            pltpu.VMEM((nq + DUMMY, T, T), I32),
            pltpu.VMEM((A_HEADS * T, 2 * T), F32),
            pltpu.VMEM((A_HEADS * T, 2 * T), F32),
            pltpu.VMEM((1, IDX_HEADS * T, T), BF16),
            pltpu.VMEM((IDX_HEADS, T, T), F32),
            pltpu.VMEM((A_HEADS // 2, 2 * T, T), BF16),
            pltpu.VMEM((T, T), I32),
            pltpu.VMEM((8, T), F32),
            pltpu.VMEM((A_HEADS, T, T), F32),
            pltpu.VMEM((A_HEADS, T, T), F32),
            pltpu.VMEM((A_HEADS // 2, 2 * T, T), F32),
        ],
        compiler_params=_params(("parallel", "arbitrary")),
        name="dsa",
    )(pab, pab, pab, pab, pab, pab, pab, bias_tiles)


def _gla_kernel(q_ref, k_ref, v_ref, g_ref, a_ref, w2_ref, gb_ref, o_ref, st_ref, *, nchunk):
    C = CHUNK
    T = LANES

    @pl.when(pl.program_id(1) == 0)
    def _():
        st_ref[...] = jnp.zeros(st_ref.shape, F32)

    r_i = lax.broadcasted_iota(I32, (C, C), 0)
    c_i = lax.broadcasted_iota(I32, (C, C), 1)
    causal = c_i <= r_i
    tri = jnp.where(causal, 1.0, 0.0).astype(BF16)
    lane = lax.broadcasted_iota(I32, (C, T), 1)
    sd_r = lax.broadcasted_iota(I32, (2 * B_DV, T), 0)
    sd_c = lax.broadcasted_iota(I32, (2 * B_DV, T), 1)
    blockdiag = (sd_r >= B_DV) == (sd_c >= B_DK)

    pairs = range(B_HEADS // 2)
    lanes = [slice(hp * T, (hp + 1) * T) for hp in pairs]
    G = math.gcd(nchunk, CHUNK_GROUP)

    def group(gi, carry):
        rows = [pl.ds(pl.multiple_of((gi * G + c) * C, C), C) for c in range(G)]
        log_a = [_softplus_parts(jnp.dot(a_ref[r, :], w2_ref[...], preferred_element_type=F32)
                                 + gb_ref[...])[1] * (1.0 / GLA_TAU) for r in rows]
        terms = [_split3(x) for x in log_a]
        bcum = [sum(jnp.dot(tri, t, preferred_element_type=F32) for t in ts) for ts in terms]
        q_t, k_t, k_d, decay = [], [], [], []
        for r, b in zip(rows, bcum):
            b_last = b[C - 1:C, :]
            q = q_ref[r, :].astype(F32) * (B_DK ** -0.5)
            k = k_ref[r, :].astype(F32)
            q_t.append((q * jnp.exp(b)).astype(BF16))
            k_t.append((k * jnp.exp(-b)).astype(BF16))
            k_d.append((k * jnp.exp(b_last - b)).astype(BF16))
            decay.append(jnp.exp(b_last))
        att = {}
        for c in range(G):
            for h in range(B_HEADS):
                ls = lanes[h // 2]
                mine = (lane < B_DK) if h % 2 == 0 else (lane >= B_DK)
                qm = jnp.where(mine, q_t[c][:, ls], jnp.zeros((C, T), BF16))
                s = lax.dot_general(qm, k_t[c][:, ls], NT_DIMS, preferred_element_type=F32)
                att[c, h] = jnp.where(causal, s, 0.0).astype(BF16)
        contrib = {}
        for c in range(G):
            for hp in pairs:
                vp = v_ref[rows[c], hp * 2 * B_DV:(hp + 1) * 2 * B_DV]
                raw = lax.dot_general(vp, k_d[c][:, lanes[hp]], TN_DIMS, preferred_element_type=F32)
                contrib[c, hp] = jnp.where(blockdiag, raw, 0.0)
        before = {}
        for hp in pairs:
            st = st_ref[hp]
            for c in range(G):
                before[c, hp] = st
                st = st * decay[c][:, lanes[hp]] + contrib[c, hp]
            st_ref[hp] = st
        for c in range(G):
            for hp in pairs:
                o_inter = lax.dot_general(q_t[c][:, lanes[hp]], before[c, hp].astype(BF16), NT_DIMS,
                                          preferred_element_type=F32)
                for hh in range(2):
                    h = 2 * hp + hh
                    vs = slice(h * B_DV, (h + 1) * B_DV)
                    o = jnp.dot(att[c, h], v_ref[rows[c], vs], preferred_element_type=F32)
                    o = o + o_inter[:, hh * B_DV:(hh + 1) * B_DV]
                    o = o * lax.rsqrt(jnp.mean(o * o, axis=-1, keepdims=True) + NORM_EPS)
                    o = o * _silu(g_ref[rows[c], vs].astype(F32))
                    o_ref[rows[c], vs] = o.astype(o_ref.dtype)
        return carry

    lax.fori_loop(0, nchunk // G, group, 0)


def _gla(pab, w2, gb):
    bsz, p, _ = pab.shape
    tc = _pick_tile(p, (640, 128))
    hv = B_HEADS * B_DV
    hk = B_HEADS * B_DK
    return pl.pallas_call(
        functools.partial(_gla_kernel, nchunk=tc // CHUNK),
        grid=(bsz, p // tc),
        in_specs=[
            pl.BlockSpec((None, tc, hk), lambda b, c: (b, c, 14)),
            pl.BlockSpec((None, tc, hk), lambda b, c: (b, c, 15)),
            pl.BlockSpec((None, tc, hv), lambda b, c: (b, c, 5)),
            pl.BlockSpec((None, tc, hv), lambda b, c: (b, c, 6)),
            pl.BlockSpec((None, tc, LANES), lambda b, c: (b, c, 34)),
            pl.BlockSpec((LANES, hk), lambda b, c: (0, 0)),
            pl.BlockSpec((1, hk), lambda b, c: (0, 0)),
        ],
        out_specs=pl.BlockSpec((None, tc, hv), lambda b, c: (b, c, 0)),
        out_shape=jax.ShapeDtypeStruct((bsz, p, hv), BF16),
        scratch_shapes=[pltpu.VMEM((B_HEADS // 2, 2 * B_DV, 2 * B_DK), F32)],
        compiler_params=_params(("parallel", "arbitrary")),
        name="gla",
    )(pab, pab, pab, pab, pab, w2, gb)


def _ret_kernel(q_ref, k_ref, v_ref, g_ref, cos_ref, sin_ref, dmat_ref, zeta_ref, xi_ref,
                cdec_ref, o_ref, st_ref, *, nchunk):
    C = CHUNK
    T = LANES
    W = C_HEADS * C_DK

    @pl.when(pl.program_id(1) == 0)
    def _():
        st_ref[...] = jnp.zeros(st_ref.shape, F32)

    lane_w = lax.broadcasted_iota(I32, (C, W), 1)
    first_half = (lane_w & (C_DK - 1)) < (C_DK // 2)
    lane = lax.broadcasted_iota(I32, (C, T), 1)
    sd_r = lax.broadcasted_iota(I32, (2 * C_DV, T), 0)
    sd_c = lax.broadcasted_iota(I32, (2 * C_DV, T), 1)
    blockdiag = (sd_r >= C_DV) == (sd_c >= C_DK)

    def rotate(x, cos, sin_signed):
        swapped = jnp.where(first_half, pltpu.roll(x, W - C_DK // 2, 1), pltpu.roll(x, C_DK // 2, 1))
        return x * cos + swapped * sin_signed

    pairs = range(C_HEADS // 2)
    lanes = [slice(hp * T, (hp + 1) * T) for hp in pairs]
    G = math.gcd(nchunk, CHUNK_GROUP)

    def group(gi, carry):
        rows = [pl.ds(pl.multiple_of((gi * G + c) * C, C), C) for c in range(G)]
        q_b, k_b, q_x, k_z = [], [], [], []
        for r in rows:
            cos = cos_ref[r, :]
            sin = sin_ref[r, :]
            q = rotate(q_ref[r, :].astype(F32), cos, sin)
            k = rotate(k_ref[r, :].astype(F32), cos, sin) * (C_DK ** -0.5)
            q_b.append(q.astype(BF16))
            k_b.append(k.astype(BF16))
            q_x.append((q * xi_ref[...]).astype(BF16))
            k_z.append((k * zeta_ref[...]).astype(BF16))
        att = {}
        for c in range(G):
            for h in range(C_HEADS):
                ls = lanes[h // 2]
                mine = (lane < C_DK) if h % 2 == 0 else (lane >= C_DK)
                qm = jnp.where(mine, q_b[c][:, ls], jnp.zeros((C, T), BF16))
                s = lax.dot_general(qm, k_b[c][:, ls], NT_DIMS, preferred_element_type=F32)
                att[c, h] = (s * dmat_ref[h]).astype(BF16)
        contrib = {}
        for c in range(G):
            for hp in pairs:
                vp = v_ref[rows[c], hp * 2 * C_DV:(hp + 1) * 2 * C_DV]
                raw = lax.dot_general(vp, k_z[c][:, lanes[hp]], TN_DIMS, preferred_element_type=F32)
                contrib[c, hp] = jnp.where(blockdiag, raw, 0.0)
        before = {}
        for hp in pairs:
            st = st_ref[hp]
            for c in range(G):
                before[c, hp] = st
                st = st * cdec_ref[:, lanes[hp]] + contrib[c, hp]
            st_ref[hp] = st
        for c in range(G):
            for hp in pairs:
                o_inter = lax.dot_general(q_x[c][:, lanes[hp]], before[c, hp].astype(BF16), NT_DIMS,
                                          preferred_element_type=F32)
                for hh in range(2):
                    h = 2 * hp + hh
                    vs = slice(h * C_DV, (h + 1) * C_DV)
                    o = jnp.dot(att[c, h], v_ref[rows[c], vs], preferred_element_type=F32)
                    o = o + o_inter[:, hh * C_DV:(hh + 1) * C_DV]
                    o = o - jnp.mean(o, axis=-1, keepdims=True)
                    o = o * lax.rsqrt(jnp.mean(o * o, axis=-1, keepdims=True) + NORM_EPS)
                    o = o * _silu(g_ref[rows[c], vs].astype(F32))
                    o_ref[rows[c], vs] = o.astype(o_ref.dtype)
        return carry

    lax.fori_loop(0, nchunk // G, group, 0)


def _retention_tables(p):
    log_gamma = np.log(1.0 - np.exp2(-5.0 - np.arange(C_HEADS, dtype=np.float64)))
    i = np.arange(CHUNK, dtype=np.float64)
    diff = i[:, None] - i[None, :]
    dmat = np.where(diff >= 0, np.exp(log_gamma[:, None, None] * np.maximum(diff, 0.0)), 0.0)
    zeta = np.exp(log_gamma[:, None] * (CHUNK - 1 - i))
    xi = np.exp(log_gamma[:, None] * (i + 1))
    cdec = np.exp(log_gamma * CHUNK)
    widen = lambda t: np.repeat(t.T[:, :, None], C_DK, axis=2).reshape(CHUNK, C_HEADS * C_DK)
    half = C_DK // 2
    inv = jnp.asarray(ROPE_BASE, F32) ** (-jnp.arange(half, dtype=F32) / half)
    ang = jnp.arange(p, dtype=jnp.int32).astype(F32)[:, None] * inv[None, :]
    cos = jnp.tile(jnp.cos(ang), (1, 2 * C_HEADS))
    sin = jnp.sin(ang)
    sin_signed = jnp.tile(jnp.concatenate([-sin, sin], axis=1), (1, C_HEADS))
    return (cos, sin_signed, jnp.asarray(dmat, F32), jnp.asarray(widen(zeta), F32),
            jnp.asarray(widen(xi), F32),
            jnp.asarray(np.repeat(cdec, C_DK)[None, :], F32))


def _retention(pcd, tables):
    bsz, p, _ = pcd.shape
    cos, sin, dmat, zeta, xi, cdec = tables
    tc = _pick_tile(p, (640, 128))
    hv = C_HEADS * C_DV
    hk = C_HEADS * C_DK
    full = lambda shape: pl.BlockSpec(shape, lambda b, c: (0,) * len(shape))
    return pl.pallas_call(
        functools.partial(_ret_kernel, nchunk=tc // CHUNK),
        grid=(bsz, p // tc),
        in_specs=[
            pl.BlockSpec((None, tc, hk), lambda b, c: (b, c, 0)),
            pl.BlockSpec((None, tc, hk), lambda b, c: (b, c, 1)),
            pl.BlockSpec((None, tc, hv), lambda b, c: (b, c, 1)),
            pl.BlockSpec((None, tc, hv), lambda b, c: (b, c, 2)),
            pl.BlockSpec((tc, hk), lambda b, c: (c, 0)),
            pl.BlockSpec((tc, hk), lambda b, c: (c, 0)),
            full((C_HEADS, CHUNK, CHUNK)),
            full((CHUNK, hk)),
            full((CHUNK, hk)),
            full((1, hk)),
        ],
        out_specs=pl.BlockSpec((None, tc, hv), lambda b, c: (b, c, 0)),
        out_shape=jax.ShapeDtypeStruct((bsz, p, hv), BF16),
        scratch_shapes=[pltpu.VMEM((C_HEADS // 2, 2 * C_DV, 2 * C_DK), F32)],
        compiler_params=_params(("parallel", "arbitrary")),
        name="retention",
    )(pcd, pcd, pcd, pcd, cos, sin, dmat, zeta, xi, cdec)


def _sb_kernel(q_ref, g_ref, k_ref, v_ref, o_ref, qm_ref, run_ref, acc_ref, hi_ref, lo_ref, ls_ref,
               w_ref):
    T = LANES
    i = pl.program_id(1)
    row = lax.broadcasted_iota(I32, (T, T), 0)
    col = lax.broadcasted_iota(I32, (T, T), 1)
    qpos = i * T + row
    low = col < D_DIM
    u_r = lax.broadcasted_iota(I32, (T, 2 * T), 0)
    u_c = lax.broadcasted_iota(I32, (T, 2 * T), 1)
    suffix = jnp.where((u_c >= T) | (u_r > u_c), 1.0, 0.0).astype(BF16)

    for h in range(D_HEADS):
        pair = q_ref[:, (h // 2) * T:(h // 2 + 1) * T]
        mine = low if h % 2 == 0 else ~low
        qm_ref[h] = jnp.where(mine, pair * (D_DIM ** -0.5), jnp.zeros_like(pair))
    run_ref[...] = jnp.zeros(run_ref.shape, F32)
    acc_ref[...] = jnp.zeros(acc_ref.shape, F32)

    def walk(t0, count):
        kts = [i - t0 - j for j in range(count)]
        rows = [pl.ds(pl.multiple_of(jnp.maximum(kt, 0) * T, T), T) for kt in kts]
        for j, kt in enumerate(kts):
            kpos = kt * T + col
            ok = (kpos < qpos) & (kpos >= PAD_FRONT)
            for h in range(D_HEADS):
                ps = slice((h // 2) * T, (h // 2 + 1) * T)
                z = lax.dot_general(qm_ref[h], k_ref[rows[j], ps], NT_DIMS, preferred_element_type=F32)
                sp, logsig = _softplus_parts(z)
                log_1m = jnp.where(ok, -sp, 0.0)
                hi = lax.bitcast_convert_type(lax.bitcast_convert_type(log_1m, I32) & BF16_BITS, F32)
                hi_ref[j, h] = hi.astype(BF16)
                lo_ref[j, h] = (log_1m - hi).astype(BF16)
                ls_ref[j, h] = jnp.where(ok, logsig, NEG_INF)
        sums = {}
        for j in range(count):
            for h in range(D_HEADS):
                sums[j, h] = (jnp.dot(hi_ref[j, h], suffix, preferred_element_type=F32)
                              + jnp.dot(lo_ref[j, h], suffix, preferred_element_type=F32))
        slowest = None
        for h in range(D_HEADS):
            run = run_ref[h]
            for j in range(count):
                w_ref[j, h] = jnp.exp(ls_ref[j, h] + run + sums[j, h][:, :T]).astype(BF16)
                run = run + sums[j, h][:, T:]
            run_ref[h] = run
            slowest = run if slowest is None else jnp.maximum(slowest, run)
        for h in range(D_HEADS):
            ps = slice((h // 2) * T, (h // 2 + 1) * T)
            acc = acc_ref[h]
            for j in range(count):
                acc = acc + jnp.dot(w_ref[j, h], v_ref[rows[j], ps], preferred_element_type=F32)
            acc_ref[h] = acc
        return (jnp.max(slowest) > LOG_F32_UNDERFLOW).astype(I32)

    def one_more(state):
        t, _ = state
        return t + 1, walk(t, 1)

    lax.while_loop(lambda s: (s[0] <= i) & (s[1] > 0), one_more, (jnp.int32(SB_FIRST), walk(0, SB_FIRST)))

    for hp in range(D_HEADS // 2):
        ps = slice(hp * T, (hp + 1) * T)
        o = jnp.where(low, acc_ref[2 * hp], acc_ref[2 * hp + 1]) * _silu(g_ref[:, ps].astype(F32))
        o_ref[:, ps] = o.astype(o_ref.dtype)


def _stick_breaking(pcd):
    bsz, p, _ = pcd.shape
    T = LANES
    hw = D_HEADS * D_DIM
    return pl.pallas_call(
        _sb_kernel,
        grid=(bsz, p // T),
        in_specs=[
            pl.BlockSpec((None, T, hw), lambda b, i: (b, i, 3)),
            pl.BlockSpec((None, T, hw), lambda b, i: (b, i, 6)),
            pl.BlockSpec((None, p, hw), lambda b, i: (b, 0, 4)),
            pl.BlockSpec((None, p, hw), lambda b, i: (b, 0, 5)),
        ],
        out_specs=pl.BlockSpec((None, T, hw), lambda b, i: (b, i, 0)),
        out_shape=jax.ShapeDtypeStruct((bsz, p, hw), BF16),
        scratch_shapes=[
            pltpu.VMEM((D_HEADS, T, T), BF16),
            pltpu.VMEM((D_HEADS, T, T), F32),
            pltpu.VMEM((D_HEADS, T, T), F32),
            pltpu.VMEM((SB_FIRST, D_HEADS, T, T), BF16),
            pltpu.VMEM((SB_FIRST, D_HEADS, T, T), BF16),
            pltpu.VMEM((SB_FIRST, D_HEADS, T, T), F32),
            pltpu.VMEM((SB_FIRST, D_HEADS, T, T), BF16),
        ],
        compiler_params=_params(("parallel", "arbitrary")),
        name="stick_breaking",
    )(pcd, pcd, pcd, pcd)


def _rel_bucket_np(rel):
    half = REL_BUCKETS // 2
    max_exact = half // 2
    n = -rel
    ret = np.where(n < 0, half, 0)
    n = np.abs(n)
    edges = [math.ceil(max_exact * (REL_MAX_DIST / max_exact) ** (j / (half - max_exact)) - 1e-9)
             for j in range(1, half - max_exact)]
    large = max_exact + sum((n >= e).astype(np.int64) for e in edges)
    return ret + np.where(n < max_exact, n, large)


def _bias_tiles(rel_bias):
    i = np.arange(LANES)[:, None]
    j = np.arange(LANES)[None, :]
    idx = np.stack([_rel_bucket_np(j - i - LANES * d) for d in range(3)])
    assert (idx[2] == idx[2, 0, 0]).all()
    table = rel_bias.astype(F32) - rel_bias.astype(F32)[idx[2, 0, 0]]
    near = jnp.asarray(idx[:2], I32)[None]
    tiles = jnp.zeros((table.shape[1],) + near.shape[1:], F32)
    for bucket in range(REL_BUCKETS):
        tiles = jnp.where(near == bucket, table[bucket][:, None, None, None], tiles)
    return tiles


def _layout_ab(w):
    aq, ak, av, ag, iq, ik, iw, bq, bk, bv, bg, ba = jnp.split(w, np.cumsum(SPLIT_AB)[:-1].tolist(), axis=1)
    pad = lambda t: jnp.pad(t, ((0, 0), (0, LANES - t.shape[1])))
    cols = [aq, ak, av, ag, iq, bv, bg, bq, bk, jnp.concatenate([ik, ik], axis=1), pad(iw), pad(ba),
            jnp.zeros((w.shape[0], W_AB_PAD - W_AB_USED), w.dtype)]
    return jnp.concatenate(cols, axis=1).astype(BF16)


def kernel(x, meta_tokens, rel_bias, norm_g, final_g, w_in_ab, gla_gate_w2, gla_gate_b, w_out_ab,
           w_in_cd, w_out_cd):
    bsz, seq, d = x.shape
    p = seq + PAD_FRONT + N_META
    depth = norm_g.shape[0]
    topk = min(TOPK_MAX, seq // 4)
    h = jnp.concatenate([jnp.zeros((bsz, PAD_FRONT, d), x.dtype),
                         jnp.broadcast_to(meta_tokens.astype(x.dtype)[None], (bsz, N_META, d)),
                         x], axis=1)
    bias_tiles = _bias_tiles(rel_bias)
    tables = _retention_tables(p)
    assert depth >= 1
    hn = None
    for layer in range(depth):
        j = layer // 2
        w_in = _layout_ab(w_in_ab[j]) if layer % 2 == 0 else w_in_cd[j].astype(BF16)
        proj = _inproj(h, norm_g[layer], w_in) if hn is None else _proj(hn, w_in)
        if layer % 2 == 0:
            first = _dsa(proj, bias_tiles, topk)
            w2 = jnp.pad(gla_gate_w2[j], ((0, LANES - GLA_GATE_RANK), (0, 0))).astype(BF16)
            second = _gla(proj, w2, gla_gate_b[j].reshape(1, -1).astype(F32))
            w_out = w_out_ab[j]
        else:
            first = _retention(proj, tables)
            second = _stick_breaking(proj)
            w_out = w_out_cd[j]
        if layer + 1 < depth:
            h, hn = _outproj(first, second, w_out, h, norm_g[layer + 1])
    return _outproj_norm(first, second, w_out, h, final_g, seq)
```

```python
import functools
import math

import numpy as np
import jax
import jax.numpy as jnp
from jax import lax
from jax.experimental import pallas as pl
from jax.experimental.pallas import tpu as pltpu

F32 = jnp.float32
BF16 = jnp.bfloat16
I32 = jnp.int32

CHUNK = 64
CHUNK_SHIFT = CHUNK.bit_length() - 1
N_META = 16
PAD_FRONT = 128 - N_META
NORM_EPS = 1e-6
NEG_INF = -1e30
A_HEADS, A_DIM = 8, 64
IDX_HEADS, IDX_DIM = 8, 64
IDX_SCALE = (IDX_DIM ** -0.5) * (IDX_HEADS ** -0.5)
TOPK_MAX = 256
B_HEADS, B_DK, B_DV = 4, 64, 128
GLA_GATE_RANK = 16
GLA_TAU = 16.0
C_HEADS, C_DK, C_DV = 4, 64, 128
ROPE_BASE = 10000.0
D_HEADS, D_DIM = 8, 64
REL_BUCKETS = 32
REL_MAX_DIST = 128
SPLIT_AB = (512, 512, 512, 512, 512, 64, 8, 256, 256, 512, 512, 16)

LANES = 128
W_AB_USED = 4480
W_AB_PAD = 4608
I32_MIN = -2 ** 31
DUMMY = 7
IDX_BITS = 14
SETTLE_FROM_BIT = 22
LOG_F32_UNDERFLOW = -104.0
LOGIT_LIMIT = 40.0
NORM_SLACK = 1.05
TINY = 1e-30
BF16_BITS = -65536
CHUNK_GROUP = 10
SB_FIRST = 3
VMEM_LIMIT = 56 * 1024 * 1024

NT_DIMS = (((1,), (1,)), ((), ()))
TN_DIMS = (((0,), (0,)), ((), ()))


def _pick_tile(n, candidates):
    for c in candidates:
        if n % c == 0:
            return c
    raise ValueError(f"no tile for {n}")


def _params(sem):
    return pltpu.CompilerParams(dimension_semantics=sem, vmem_limit_bytes=VMEM_LIMIT)


def _silu(x):
    return x / (1.0 + jnp.exp(-x))


def _softplus_parts(z):
    t = jnp.log(1.0 + jnp.exp(-jnp.abs(z)))
    return jnp.maximum(z, 0.0) + t, jnp.minimum(z, 0.0) - t


def _split3(x):
    a = x.astype(BF16)
    r = x - a.astype(F32)
    b = r.astype(BF16)
    c = (r - b.astype(F32)).astype(BF16)
    return a, b, c


def _inproj_kernel(x_ref, g_ref, w_ref, o_ref, hn_ref):
    @pl.when(pl.program_id(2) == 0)
    def _():
        x = x_ref[...]
        ms = jnp.mean(x * x, axis=-1, keepdims=True)
        hn_ref[...] = (x * lax.rsqrt(ms + NORM_EPS) * g_ref[...]).astype(BF16)

    o_ref[...] = jnp.dot(hn_ref[...], w_ref[...], preferred_element_type=F32).astype(o_ref.dtype)


def _inproj(h, g, w):
    bsz, p, d = h.shape
    n = w.shape[1]
    tp = _pick_tile(p, (1664, 1280, 640, 128))
    tn = _pick_tile(n, (1792, 1536, 896, 512, 128))
    return pl.pallas_call(
        _inproj_kernel,
        grid=(bsz, p // tp, n // tn),
        in_specs=[
            pl.BlockSpec((None, tp, d), lambda b, i, j: (b, i, 0)),
            pl.BlockSpec((1, d), lambda b, i, j: (0, 0)),
            pl.BlockSpec((d, tn), lambda b, i, j: (0, j)),
        ],
        out_specs=pl.BlockSpec((None, tp, tn), lambda b, i, j: (b, i, j)),
        out_shape=jax.ShapeDtypeStruct((bsz, p, n), BF16),
        scratch_shapes=[pltpu.VMEM((tp, d), BF16)],
        compiler_params=_params(("parallel", "parallel", "arbitrary")),
        name="inproj",
    )(h, g.reshape(1, d), w)


def _proj_kernel(x_ref, w_ref, o_ref):
    o_ref[...] = jnp.dot(x_ref[...], w_ref[...], preferred_element_type=F32).astype(o_ref.dtype)


def _proj(hn, w):
    bsz, p, d = hn.shape
    n = w.shape[1]
    tp = _pick_tile(p, (1664, 1280, 640, 128))
    tn = _pick_tile(n, (1792, 1536, 896, 512, 128))
    return pl.pallas_call(
        _proj_kernel,
        grid=(bsz, p // tp, n // tn),
        in_specs=[
            pl.BlockSpec((None, tp, d), lambda b, i, j: (b, i, 0)),
            pl.BlockSpec((d, tn), lambda b, i, j: (0, j)),
        ],
        out_specs=pl.BlockSpec((None, tp, tn), lambda b, i, j: (b, i, j)),
        out_shape=jax.ShapeDtypeStruct((bsz, p, n), BF16),
        compiler_params=_params(("parallel", "parallel", "parallel")),
        name="proj",
    )(hn, w)


def _outproj_kernel(m1_ref, m2_ref, w1_ref, w2_ref, h_ref, g_ref, o_ref, hn_ref, *, tp):
    y = jnp.dot(m1_ref[...], w1_ref[...], preferred_element_type=F32)
    y = y + jnp.dot(m2_ref[...], w2_ref[...], preferred_element_type=F32)
    pos = pl.program_id(1) * tp + lax.broadcasted_iota(I32, y.shape, 0)
    x = h_ref[...] + jnp.where(pos >= PAD_FRONT, y, 0.0)
    o_ref[...] = x
    ms = jnp.mean(x * x, axis=-1, keepdims=True)
    hn_ref[...] = (x * lax.rsqrt(ms + NORM_EPS) * g_ref[...]).astype(BF16)


def _outproj(m1, m2, w_out, h, g_next):
    bsz, p, d = h.shape
    k1 = m1.shape[-1]
    k2 = m2.shape[-1]
    tp = _pick_tile(p, (832, 640, 128))
    w = w_out.astype(BF16)
    return pl.pallas_call(
        functools.partial(_outproj_kernel, tp=tp),
        grid=(bsz, p // tp),
        in_specs=[
            pl.BlockSpec((None, tp, k1), lambda b, i: (b, i, 0)),
            pl.BlockSpec((None, tp, k2), lambda b, i: (b, i, 0)),
            pl.BlockSpec((k1, d), lambda b, i: (0, 0)),
            pl.BlockSpec((k2, d), lambda b, i: (0, 0)),
            pl.BlockSpec((None, tp, d), lambda b, i: (b, i, 0)),
            pl.BlockSpec((1, d), lambda b, i: (0, 0)),
        ],
        out_specs=[pl.BlockSpec((None, tp, d), lambda b, i: (b, i, 0)),
                   pl.BlockSpec((None, tp, d), lambda b, i: (b, i, 0))],
        out_shape=[jax.ShapeDtypeStruct((bsz, p, d), F32), jax.ShapeDtypeStruct((bsz, p, d), BF16)],
        compiler_params=_params(("parallel", "parallel")),
        name="outproj",
    )(m1, m2, w[:k1], w[k1:], h, g_next.reshape(1, d))


def _outproj_norm_kernel(*refs, fold):
    m1_refs, m2_refs, h_refs = refs[:fold], refs[fold:2 * fold], refs[2 * fold:3 * fold]
    w1_ref, w2_ref, g_ref, o_ref = refs[3 * fold:]
    stack = lambda tiles: jnp.concatenate([t[...] for t in tiles], axis=0)
    y = jnp.dot(stack(m1_refs), w1_ref[...], preferred_element_type=F32)
    y = y + jnp.dot(stack(m2_refs), w2_ref[...], preferred_element_type=F32)
    x = stack(h_refs) + y
    ms = jnp.mean(x * x, axis=-1, keepdims=True)
    o_ref[...] = x * lax.rsqrt(ms + NORM_EPS) * g_ref[...]


def _outproj_norm(m1, m2, w_out, h, g, seq):
    bsz, p, d = h.shape
    k1 = m1.shape[-1]
    k2 = m2.shape[-1]
    tp = LANES
    skip = (p - seq) // tp
    assert skip * tp == p - seq
    fold = _pick_tile(seq // tp, (4, 2, 1))
    w = w_out.astype(BF16)

    def tiles(width):
        return [pl.BlockSpec((None, tp, width), lambda b, i, t=t: (b, fold * i + skip + t, 0))
                for t in range(fold)]

    return pl.pallas_call(
        functools.partial(_outproj_norm_kernel, fold=fold),
        grid=(bsz, seq // (fold * tp)),
        in_specs=tiles(k1) + tiles(k2) + tiles(d) + [
            pl.BlockSpec((k1, d), lambda b, i: (0, 0)),
            pl.BlockSpec((k2, d), lambda b, i: (0, 0)),
            pl.BlockSpec((1, d), lambda b, i: (0, 0)),
        ],
        out_specs=pl.BlockSpec((None, fold * tp, d), lambda b, i: (b, i, 0)),
        out_shape=jax.ShapeDtypeStruct((bsz, seq, d), F32),
        compiler_params=_params(("parallel", "parallel")),
        name="outproj_norm",
    )(*([m1] * fold + [m2] * fold + [h] * fold), w[:k1], w[k1:], g.reshape(1, d))


def _dsa_kernel(aq_ref, ag_ref, iq_ref, iw_ref, k_ref, v_ref, ik_ref, bias_ref, o_ref,
                key_ref, sa_ref, sb_ref, iqs_ref, wts_ref, qm_ref,
                jdx_ref, kmax_ref, mx_ref, ls_ref, acc_ref, *, topk):
    T = LANES
    i = pl.program_id(1)
    last = i + DUMMY

    def num_groups(group):
        return (i + group) // group

    def first_slot(group):
        return last + 1 - group * num_groups(group)

    row = lax.broadcasted_iota(I32, (T, T), 0)
    col = lax.broadcasted_iota(I32, (T, T), 1)
    qchunk = (i * T + row) >> CHUNK_SHIFT
    low = col < A_DIM

    def admissible(kt):
        kpos = kt * T + col
        return (kpos >= PAD_FRONT) & ((kpos >> CHUNK_SHIFT) <= qchunk)

    def tile_rows(slot):
        return pl.ds(pl.multiple_of(jnp.maximum(slot - DUMMY, 0) * T, T), T)

    iw = iw_ref[...].astype(F32)
    for h in range(IDX_HEADS):
        pair = iq_ref[:, (h // 2) * T:(h // 2 + 1) * T]
        mine = low if h % 2 == 0 else ~low
        iqs_ref[0, h * T:(h + 1) * T, :] = jnp.where(mine, pair, jnp.zeros_like(pair))
        wts_ref[h] = jnp.broadcast_to(iw[:, h:h + 1], (T, T))
    for h in range(A_HEADS):
        pair = aq_ref[:, (h // 2) * T:(h // 2 + 1) * T]
        mine = low if h % 2 == 0 else ~low
        qm_ref[h // 2, (h % 2) * T:(h % 2 + 1) * T, :] = jnp.where(
            mine, pair * (A_DIM ** -0.5), jnp.zeros_like(pair))

    def sweep(lhs_ref, rhs_ref, consume, consume_last):
        group_rows = lhs_ref.shape[1]

        def issue(s0, stage):
            rows = [tile_rows(s0), tile_rows(s0 + 1)]
            for g in range(lhs_ref.shape[0]):
                ps = slice(g * T, (g + 1) * T)
                kw = jnp.concatenate([rhs_ref[rows[0], ps], rhs_ref[rows[1], ps]], axis=0)
                stage[g * group_rows:(g + 1) * group_rows, :] = lax.dot_general(
                    lhs_ref[g], kw, NT_DIMS, preferred_element_type=F32)

        first = first_slot(4)
        issue(first, sa_ref)

        def quad(s):
            issue(s + 2, sb_ref)
            consume(s, sa_ref)
            issue(s + 4, sa_ref)
            consume(s + 2, sb_ref)

        inner = num_groups(4) - 1

        @pl.when(inner % 2 == 1)
        def _():
            quad(first)

        def body(q, carry):
            s = first + 4 * (inner % 2) + 8 * q
            quad(s)
            quad(s + 4)
            return carry

        lax.fori_loop(0, inner // 2, body, 0)
        issue(last - 1, sb_ref)
        consume(last - 3, sa_ref)
        consume_last(last - 1, sb_ref)

    def score_pair(s0, stage):
        for u in range(2):
            tot = jnp.zeros((T, T), F32)
            for h in range(IDX_HEADS):
                tot = tot + jnp.maximum(stage[h * T:(h + 1) * T, u * T:(u + 1) * T], 0.0) * wts_ref[h]
            score = jnp.where(admissible(s0 + u - DUMMY), tot * IDX_SCALE, NEG_INF)
            bits = lax.bitcast_convert_type(score, I32)
            key_ref[s0 + u] = bits ^ ((bits >> 31) & 0x7FFFFFFF)

    sweep(iqs_ref, ik_ref, score_pair, score_pair)
    for d in range(DUMMY):
        key_ref[d] = jnp.full((T, T), I32_MIN, I32)

    def row_count(pred):
        one = jnp.ones((T, T), I32)
        zero = jnp.zeros((T, T), I32)
        first = first_slot(8)

        def body(g, acc):
            hits = [jnp.where(pred(first + 8 * g + u), one, zero) for u in range(8)]
            while len(hits) > 1:
                hits = [a + b for a, b in zip(hits[::2], hits[1::2])]
            return acc + hits[0]

        acc = lax.fori_loop(0, num_groups(8), body, zero)
        return jnp.sum(acc.astype(F32), axis=-1, keepdims=True)

    def value_bit(it, state):
        t, reach = state
        cand = t + (jnp.int32(1) << (31 - it))
        wide = jnp.broadcast_to(cand, (T, T))
        cnt = row_count(lambda s: key_ref[s] >= wide)
        fits = cnt >= topk
        return jnp.where(fits, cand, t), jnp.where(fits, cnt, reach)

    everything = (8 * T * num_groups(8)).astype(F32)
    state = lax.fori_loop(0, SETTLE_FROM_BIT, value_bit, (jnp.full((T, 1), I32_MIN, I32),
                                                          jnp.full((T, 1), everything, F32)))

    wide = jnp.broadcast_to(state[0], (T, T))
    final = row_count(lambda s: key_ref[s] > wide) < topk

    def unsettled(carry):
        it, (_, reach) = carry
        return (it < 32) & (jnp.max(jnp.where(final, topk, reach)) > topk)

    def two_bits(carry):
        it, state = carry
        return it + 2, value_bit(it + 1, value_bit(it, state))

    _, (thr, reach) = lax.while_loop(unsettled, two_bits, (jnp.int32(SETTLE_FROM_BIT), state))
    thr = jnp.broadcast_to(thr, (T, T))
    jdx_ref[...] = jnp.full((T, T), 2 ** IDX_BITS, I32)

    @pl.when(jnp.max(reach) > topk)
    def _():
        need = topk - (reach - row_count(lambda s: key_ref[s] == thr))
        p_r = lax.broadcasted_iota(I32, (T, 2 * T), 0)
        p_c = lax.broadcasted_iota(I32, (T, 2 * T), 1)
        prefix_ones = jnp.where((p_c >= T) | (p_r <= p_c), 1.0, 0.0).astype(BF16)
        first = first_slot(8)

        def body(g, state):
            seen, best = state
            slots = [first + 8 * g + u for u in range(8)]
            tied = [key_ref[s] == thr for s in slots]
            sums = [jnp.dot(jnp.where(t, 1.0, 0.0).astype(BF16), prefix_ones,
                            preferred_element_type=F32) for t in tied]
            for s, t, sm in zip(slots, tied, sums):
                keep = t & (seen + sm[:, :T] <= need)
                best = jnp.maximum(best, jnp.where(keep, (s - DUMMY) * T + col, -1))
                seen = seen + sm[:, T:]
            return seen, best

        _, best = lax.fori_loop(0, num_groups(8), body,
                                (jnp.zeros((T, T), F32), jnp.full((T, T), -1, I32)))
        jdx_ref[...] = jnp.broadcast_to(jnp.max(best, axis=-1, keepdims=True), (T, T))

    jdx = jdx_ref[...]

    def selection_mask(s):
        key = key_ref[s]
        sel = (key > thr) | ((key == thr) & ((s - DUMMY) * T + col <= jdx))
        return jnp.where(sel & admissible(s - DUMMY), 0.0, NEG_INF)

    head_of_lane = (lax.broadcasted_iota(I32, (A_HEADS * A_DIM, T), 0) // A_DIM
                    == lax.broadcasted_iota(I32, (A_HEADS * A_DIM, T), 1))
    head_sum = jnp.where(head_of_lane, 1.0, 0.0).astype(BF16)

    def head_norms2(t):
        t = t.astype(F32)
        return jnp.dot((t * t).astype(BF16), head_sum, preferred_element_type=F32)

    @pl.when(i == 0)
    def _():
        def widest(t, best):
            return jnp.maximum(best, head_norms2(k_ref[pl.ds(pl.multiple_of(t * T, T), T), :]))
        best = lax.fori_loop(0, k_ref.shape[0] // T, widest, jnp.zeros((T, T), F32))
        kmax_ref[...] = jnp.broadcast_to(jnp.max(best, axis=0, keepdims=True), kmax_ref.shape)

    bound2 = head_norms2(aq_ref[...]) * kmax_ref[0:1, :] * (NORM_SLACK / A_DIM)
    room = LOGIT_LIMIT - jnp.max(jnp.abs(bias_ref[...]))
    small_logits = (room > 0.0) & (jnp.max(bound2) <= room * room)

    ls_ref[...] = jnp.zeros(ls_ref.shape, F32)
    acc_ref[...] = jnp.zeros(acc_ref.shape, F32)

    def attend(s0, stage, near, final):
        madd = [selection_mask(s0), selection_mask(s0 + 1)]
        for hp in range(A_HEADS // 2):
            weights = []
            for h in (2 * hp, 2 * hp + 1):
                parts = [stage[h * T:(h + 1) * T, u * T:(u + 1) * T] + madd[u] for u in range(2)]
                if near:
                    parts = [parts[u] + bias_ref[h, 1 - u] for u in range(2)]
                if not final:
                    mx_ref[h] = jnp.maximum(mx_ref[h], jnp.maximum(parts[0], parts[1]))
                else:
                    es = [jnp.exp(part - mx_ref[h]) for part in parts]
                    ls_ref[h] = ls_ref[h] + (es[0] + es[1])
                    weights.append(jnp.concatenate(es, axis=1).astype(BF16))
            if final:
                ps = slice(hp * T, (hp + 1) * T)
                vw = jnp.concatenate([v_ref[tile_rows(s0), ps], v_ref[tile_rows(s0 + 1), ps]], axis=0)
                acc_ref[hp] = acc_ref[hp] + jnp.dot(jnp.concatenate(weights, axis=0), vw,
                                                    preferred_element_type=F32)

    def attention_sweep(final):
        sweep(qm_ref, k_ref,
              functools.partial(attend, near=False, final=final),
              functools.partial(attend, near=True, final=final))

    @pl.when(small_logits)
    def _():
        mx_ref[...] = jnp.zeros(mx_ref.shape, F32)

    @pl.when(jnp.logical_not(small_logits))
    def _():
        mx_ref[...] = jnp.full(mx_ref.shape, NEG_INF, F32)
        attention_sweep(False)
        for h in range(A_HEADS):
            mx_ref[h] = jnp.broadcast_to(jnp.max(mx_ref[h], axis=-1, keepdims=True), (T, T))

    attention_sweep(True)

    def row_total(h):
        return jnp.maximum(jnp.sum(ls_ref[h], axis=-1, keepdims=True), TINY)

    for hp in range(A_HEADS // 2):
        o0 = acc_ref[hp, :T, :] / row_total(2 * hp)
        o1 = acc_ref[hp, T:, :] / row_total(2 * hp + 1)
        g = ag_ref[:, hp * T:(hp + 1) * T].astype(F32)
        o_ref[:, hp * T:(hp + 1) * T] = (jnp.where(low, o0, o1) * _silu(g)).astype(o_ref.dtype)


def _dsa(pab, bias_tiles, topk):
    bsz, p, _ = pab.shape
    T = LANES
    nq = p // T
    hw = A_HEADS * A_DIM
    one = pl.Buffered(1)
    return pl.pallas_call(
        functools.partial(_dsa_kernel, topk=float(topk)),
        grid=(bsz, nq),
        in_specs=[
            pl.BlockSpec((None, T, hw), lambda b, i: (b, i, 0)),
            pl.BlockSpec((None, T, hw), lambda b, i: (b, i, 3)),
            pl.BlockSpec((None, T, hw), lambda b, i: (b, i, 4)),
            pl.BlockSpec((None, T, T), lambda b, i: (b, i, 33)),
            pl.BlockSpec((None, p, hw), lambda b, i: (b, 0, 1)),
            pl.BlockSpec((None, p, hw), lambda b, i: (b, 0, 2)),
            pl.BlockSpec((None, p, T), lambda b, i: (b, 0, 32)),
            pl.BlockSpec((A_HEADS, 2, T, T), lambda b, i: (0, 0, 0, 0), pipeline_mode=one),
        ],
        out_specs=pl.BlockSpec((None, T, hw), lambda b, i: (b, i, 0)),
        out_shape=jax.ShapeDtypeStruct((bsz, p, hw), BF16),
        scratch_shapes=[
            pltpu.VMEM((nq + DUMMY, T, T), I32),
            pltpu.VMEM((A_HEADS * T, 2 * T), F32),
            pltpu.VMEM((A_HEADS * T, 2 * T), F32),
            pltpu.VMEM((1, IDX_HEADS * T, T), BF16),
            pltpu.VMEM((IDX_HEADS, T, T), F32),
            pltpu.VMEM((A_HEADS // 2, 2 * T, T), BF16),
            pltpu.VMEM((T, T), I32),
            pltpu.VMEM((8, T), F32),
            pltpu.VMEM((A_HEADS, T, T), F32),
            pltpu.VMEM((A_HEADS, T, T), F32),
            pltpu.VMEM((A_HEADS // 2, 2 * T, T), F32),
        ],
        compiler_params=_params(("parallel", "arbitrary")),
        name="dsa",
    )(pab, pab, pab, pab, pab, pab, pab, bias_tiles)


def _gla_kernel(q_ref, k_ref, v_ref, g_ref, a_ref, w2_ref, gb_ref, o_ref, st_ref, *, nchunk):
    C = CHUNK
    T = LANES

    @pl.when(pl.program_id(1) == 0)
    def _():
        st_ref[...] = jnp.zeros(st_ref.shape, F32)

    r_i = lax.broadcasted_iota(I32, (C, C), 0)
    c_i = lax.broadcasted_iota(I32, (C, C), 1)
    causal = c_i <= r_i
    tri = jnp.where(causal, 1.0, 0.0).astype(BF16)
    lane = lax.broadcasted_iota(I32, (C, T), 1)
    sd_r = lax.broadcasted_iota(I32, (2 * B_DV, T), 0)
    sd_c = lax.broadcasted_iota(I32, (2 * B_DV, T), 1)
    blockdiag = (sd_r >= B_DV) == (sd_c >= B_DK)

    pairs = range(B_HEADS // 2)
    lanes = [slice(hp * T, (hp + 1) * T) for hp in pairs]
    G = math.gcd(nchunk, CHUNK_GROUP)

    def group(gi, carry):
        rows = [pl.ds(pl.multiple_of((gi * G + c) * C, C), C) for c in range(G)]
        log_a = [_softplus_parts(jnp.dot(a_ref[r, :], w2_ref[...], preferred_element_type=F32)
                                 + gb_ref[...])[1] * (1.0 / GLA_TAU) for r in rows]
        terms = [_split3(x) for x in log_a]
        bcum = [sum(jnp.dot(tri, t, preferred_element_type=F32) for t in ts) for ts in terms]
        q_t, k_t, k_d, decay = [], [], [], []
        for r, b in zip(rows, bcum):
            b_last = b[C - 1:C, :]
            q = q_ref[r, :].astype(F32) * (B_DK ** -0.5)
            k = k_ref[r, :].astype(F32)
            q_t.append((q * jnp.exp(b)).astype(BF16))
            k_t.append((k * jnp.exp(-b)).astype(BF16))
            k_d.append((k * jnp.exp(b_last - b)).astype(BF16))
            decay.append(jnp.exp(b_last))
        att = {}
        for c in range(G):
            for h in range(B_HEADS):
                ls = lanes[h // 2]
                mine = (lane < B_DK) if h % 2 == 0 else (lane >= B_DK)
                qm = jnp.where(mine, q_t[c][:, ls], jnp.zeros((C, T), BF16))
                s = lax.dot_general(qm, k_t[c][:, ls], NT_DIMS, preferred_element_type=F32)
                att[c, h] = jnp.where(causal, s, 0.0).astype(BF16)
        contrib = {}
        for c in range(G):
            for hp in pairs:
                vp = v_ref[rows[c], hp * 2 * B_DV:(hp + 1) * 2 * B_DV]
                raw = lax.dot_general(vp, k_d[c][:, lanes[hp]], TN_DIMS, preferred_element_type=F32)
                contrib[c, hp] = jnp.where(blockdiag, raw, 0.0)
        before = {}
        for hp in pairs:
            st = st_ref[hp]
            for c in range(G):
                before[c, hp] = st
                st = st * decay[c][:, lanes[hp]] + contrib[c, hp]
            st_ref[hp] = st
        for c in range(G):
            for hp in pairs:
                o_inter = lax.dot_general(q_t[c][:, lanes[hp]], before[c, hp].astype(BF16), NT_DIMS,
                                          preferred_element_type=F32)
                for hh in range(2):
                    h = 2 * hp + hh
                    vs = slice(h * B_DV, (h + 1) * B_DV)
                    o = jnp.dot(att[c, h], v_ref[rows[c], vs], preferred_element_type=F32)
                    o = o + o_inter[:, hh * B_DV:(hh + 1) * B_DV]
                    o = o * lax.rsqrt(jnp.mean(o * o, axis=-1, keepdims=True) + NORM_EPS)
                    o = o * _silu(g_ref[rows[c], vs].astype(F32))
                    o_ref[rows[c], vs] = o.astype(o_ref.dtype)
        return carry

    lax.fori_loop(0, nchunk // G, group, 0)


def _gla(pab, w2, gb):
    bsz, p, _ = pab.shape
    tc = _pick_tile(p, (640, 128))
    hv = B_HEADS * B_DV
    hk = B_HEADS * B_DK
    return pl.pallas_call(
        functools.partial(_gla_kernel, nchunk=tc // CHUNK),
        grid=(bsz, p // tc),
        in_specs=[
            pl.BlockSpec((None, tc, hk), lambda b, c: (b, c, 14)),
            pl.BlockSpec((None, tc, hk), lambda b, c: (b, c, 15)),
            pl.BlockSpec((None, tc, hv), lambda b, c: (b, c, 5)),
            pl.BlockSpec((None, tc, hv), lambda b, c: (b, c, 6)),
            pl.BlockSpec((None, tc, LANES), lambda b, c: (b, c, 34)),
            pl.BlockSpec((LANES, hk), lambda b, c: (0, 0)),
            pl.BlockSpec((1, hk), lambda b, c: (0, 0)),
        ],
        out_specs=pl.BlockSpec((None, tc, hv), lambda b, c: (b, c, 0)),
        out_shape=jax.ShapeDtypeStruct((bsz, p, hv), BF16),
        scratch_shapes=[pltpu.VMEM((B_HEADS // 2, 2 * B_DV, 2 * B_DK), F32)],
        compiler_params=_params(("parallel", "arbitrary")),
        name="gla",
    )(pab, pab, pab, pab, pab, w2, gb)


def _ret_kernel(q_ref, k_ref, v_ref, g_ref, cos_ref, sin_ref, dmat_ref, zeta_ref, xi_ref,
                cdec_ref, o_ref, st_ref, *, nchunk):
    C = CHUNK
    T = LANES
    W = C_HEADS * C_DK

    @pl.when(pl.program_id(1) == 0)
    def _():
        st_ref[...] = jnp.zeros(st_ref.shape, F32)

    lane_w = lax.broadcasted_iota(I32, (C, W), 1)
    first_half = (lane_w & (C_DK - 1)) < (C_DK // 2)
    lane = lax.broadcasted_iota(I32, (C, T), 1)
    sd_r = lax.broadcasted_iota(I32, (2 * C_DV, T), 0)
    sd_c = lax.broadcasted_iota(I32, (2 * C_DV, T), 1)
    blockdiag = (sd_r >= C_DV) == (sd_c >= C_DK)

    def rotate(x, cos, sin_signed):
        swapped = jnp.where(first_half, pltpu.roll(x, W - C_DK // 2, 1), pltpu.roll(x, C_DK // 2, 1))
        return x * cos + swapped * sin_signed

    pairs = range(C_HEADS // 2)
    lanes = [slice(hp * T, (hp + 1) * T) for hp in pairs]
    G = math.gcd(nchunk, CHUNK_GROUP)

    def group(gi, carry):
        rows = [pl.ds(pl.multiple_of((gi * G + c) * C, C), C) for c in range(G)]
        q_b, k_b, q_x, k_z = [], [], [], []
        for r in rows:
            cos = cos_ref[r, :]
            sin = sin_ref[r, :]
            q = rotate(q_ref[r, :].astype(F32), cos, sin)
            k = rotate(k_ref[r, :].astype(F32), cos, sin) * (C_DK ** -0.5)
            q_b.append(q.astype(BF16))
            k_b.append(k.astype(BF16))
            q_x.append((q * xi_ref[...]).astype(BF16))
            k_z.append((k * zeta_ref[...]).astype(BF16))
        att = {}
        for c in range(G):
            for h in range(C_HEADS):
                ls = lanes[h // 2]
                mine = (lane < C_DK) if h % 2 == 0 else (lane >= C_DK)
                qm = jnp.where(mine, q_b[c][:, ls], jnp.zeros((C, T), BF16))
                s = lax.dot_general(qm, k_b[c][:, ls], NT_DIMS, preferred_element_type=F32)
                att[c, h] = (s * dmat_ref[h]).astype(BF16)
        contrib = {}
        for c in range(G):
            for hp in pairs:
                vp = v_ref[rows[c], hp * 2 * C_DV:(hp + 1) * 2 * C_DV]
                raw = lax.dot_general(vp, k_z[c][:, lanes[hp]], TN_DIMS, preferred_element_type=F32)
                contrib[c, hp] = jnp.where(blockdiag, raw, 0.0)
        before = {}
        for hp in pairs:
            st = st_ref[hp]
            for c in range(G):
                before[c, hp] = st
                st = st * cdec_ref[:, lanes[hp]] + contrib[c, hp]
            st_ref[hp] = st
        for c in range(G):
            for hp in pairs:
                o_inter = lax.dot_general(q_x[c][:, lanes[hp]], before[c, hp].astype(BF16), NT_DIMS,
                                          preferred_element_type=F32)
                for hh in range(2):
                    h = 2 * hp + hh
                    vs = slice(h * C_DV, (h + 1) * C_DV)
                    o = jnp.dot(att[c, h], v_ref[rows[c], vs], preferred_element_type=F32)
                    o = o + o_inter[:, hh * C_DV:(hh + 1) * C_DV]
                    o = o - jnp.mean(o, axis=-1, keepdims=True)
                    o = o * lax.rsqrt(jnp.mean(o * o, axis=-1, keepdims=True) + NORM_EPS)
                    o = o * _silu(g_ref[rows[c], vs].astype(F32))
                    o_ref[rows[c], vs] = o.astype(o_ref.dtype)
        return carry

    lax.fori_loop(0, nchunk // G, group, 0)


def _retention_tables(p):
    log_gamma = np.log(1.0 - np.exp2(-5.0 - np.arange(C_HEADS, dtype=np.float64)))
    i = np.arange(CHUNK, dtype=np.float64)
    diff = i[:, None] - i[None, :]
    dmat = np.where(diff >= 0, np.exp(log_gamma[:, None, None] * np.maximum(diff, 0.0)), 0.0)
    zeta = np.exp(log_gamma[:, None] * (CHUNK - 1 - i))
    xi = np.exp(log_gamma[:, None] * (i + 1))
    cdec = np.exp(log_gamma * CHUNK)
    widen = lambda t: np.repeat(t.T[:, :, None], C_DK, axis=2).reshape(CHUNK, C_HEADS * C_DK)
    half = C_DK // 2
    inv = jnp.asarray(ROPE_BASE, F32) ** (-jnp.arange(half, dtype=F32) / half)
    ang = jnp.arange(p, dtype=jnp.int32).astype(F32)[:, None] * inv[None, :]
    cos = jnp.tile(jnp.cos(ang), (1, 2 * C_HEADS))
    sin = jnp.sin(ang)
    sin_signed = jnp.tile(jnp.concatenate([-sin, sin], axis=1), (1, C_HEADS))
    return (cos, sin_signed, jnp.asarray(dmat, F32), jnp.asarray(widen(zeta), F32),
            jnp.asarray(widen(xi), F32),
            jnp.asarray(np.repeat(cdec, C_DK)[None, :], F32))


def _retention(pcd, tables):
    bsz, p, _ = pcd.shape
    cos, sin, dmat, zeta, xi, cdec = tables
    tc = _pick_tile(p, (640, 128))
    hv = C_HEADS * C_DV
    hk = C_HEADS * C_DK
    full = lambda shape: pl.BlockSpec(shape, lambda b, c: (0,) * len(shape))
    return pl.pallas_call(
        functools.partial(_ret_kernel, nchunk=tc // CHUNK),
        grid=(bsz, p // tc),
        in_specs=[
            pl.BlockSpec((None, tc, hk), lambda b, c: (b, c, 0)),
            pl.BlockSpec((None, tc, hk), lambda b, c: (b, c, 1)),
            pl.BlockSpec((None, tc, hv), lambda b, c: (b, c, 1)),
            pl.BlockSpec((None, tc, hv), lambda b, c: (b, c, 2)),
            pl.BlockSpec((tc, hk), lambda b, c: (c, 0)),
            pl.BlockSpec((tc, hk), lambda b, c: (c, 0)),
            full((C_HEADS, CHUNK, CHUNK)),
            full((CHUNK, hk)),
            full((CHUNK, hk)),
            full((1, hk)),
        ],
        out_specs=pl.BlockSpec((None, tc, hv), lambda b, c: (b, c, 0)),
        out_shape=jax.ShapeDtypeStruct((bsz, p, hv), BF16),
        scratch_shapes=[pltpu.VMEM((C_HEADS // 2, 2 * C_DV, 2 * C_DK), F32)],
        compiler_params=_params(("parallel", "arbitrary")),
        name="retention",
    )(pcd, pcd, pcd, pcd, cos, sin, dmat, zeta, xi, cdec)


def _sb_kernel(q_ref, g_ref, k_ref, v_ref, o_ref, qm_ref, run_ref, acc_ref, hi_ref, lo_ref, ls_ref,
               w_ref):
    T = LANES
    i = pl.program_id(1)
    row = lax.broadcasted_iota(I32, (T, T), 0)
    col = lax.broadcasted_iota(I32, (T, T), 1)
    qpos = i * T + row
    low = col < D_DIM
    u_r = lax.broadcasted_iota(I32, (T, 2 * T), 0)
    u_c = lax.broadcasted_iota(I32, (T, 2 * T), 1)
    suffix = jnp.where((u_c >= T) | (u_r > u_c), 1.0, 0.0).astype(BF16)

    for h in range(D_HEADS):
        pair = q_ref[:, (h // 2) * T:(h // 2 + 1) * T]
        mine = low if h % 2 == 0 else ~low
        qm_ref[h] = jnp.where(mine, pair * (D_DIM ** -0.5), jnp.zeros_like(pair))
    run_ref[...] = jnp.zeros(run_ref.shape, F32)
    acc_ref[...] = jnp.zeros(acc_ref.shape, F32)

    def walk(t0, count):
        kts = [i - t0 - j for j in range(count)]
        rows = [pl.ds(pl.multiple_of(jnp.maximum(kt, 0) * T, T), T) for kt in kts]
        for j, kt in enumerate(kts):
            kpos = kt * T + col
            ok = (kpos < qpos) & (kpos >= PAD_FRONT)
            for h in range(D_HEADS):
                ps = slice((h // 2) * T, (h // 2 + 1) * T)
                z = lax.dot_general(qm_ref[h], k_ref[rows[j], ps], NT_DIMS, preferred_element_type=F32)
                sp, logsig = _softplus_parts(z)
                log_1m = jnp.where(ok, -sp, 0.0)
                hi = lax.bitcast_convert_type(lax.bitcast_convert_type(log_1m, I32) & BF16_BITS, F32)
                hi_ref[j, h] = hi.astype(BF16)
                lo_ref[j, h] = (log_1m - hi).astype(BF16)
                ls_ref[j, h] = jnp.where(ok, logsig, NEG_INF)
        sums = {}
        for j in range(count):
            for h in range(D_HEADS):
                sums[j, h] = (jnp.dot(hi_ref[j, h], suffix, preferred_element_type=F32)
                              + jnp.dot(lo_ref[j, h], suffix, preferred_element_type=F32))
        slowest = None
        for h in range(D_HEADS):
            run = run_ref[h]
            for j in range(count):
                w_ref[j, h] = jnp.exp(ls_ref[j, h] + run + sums[j, h][:, :T]).astype(BF16)
                run = run + sums[j, h][:, T:]
            run_ref[h] = run
            slowest = run if slowest is None else jnp.maximum(slowest, run)
        for h in range(D_HEADS):
            ps = slice((h // 2) * T, (h // 2 + 1) * T)
            acc = acc_ref[h]
            for j in range(count):
                acc = acc + jnp.dot(w_ref[j, h], v_ref[rows[j], ps], preferred_element_type=F32)
            acc_ref[h] = acc
        return (jnp.max(slowest) > LOG_F32_UNDERFLOW).astype(I32)

    def one_more(state):
        t, _ = state
        return t + 1, walk(t, 1)

    lax.while_loop(lambda s: (s[0] <= i) & (s[1] > 0), one_more, (jnp.int32(SB_FIRST), walk(0, SB_FIRST)))

    for hp in range(D_HEADS // 2):
        ps = slice(hp * T, (hp + 1) * T)
        o = jnp.where(low, acc_ref[2 * hp], acc_ref[2 * hp + 1]) * _silu(g_ref[:, ps].astype(F32))
        o_ref[:, ps] = o.astype(o_ref.dtype)


def _stick_breaking(pcd):
    bsz, p, _ = pcd.shape
    T = LANES
    hw = D_HEADS * D_DIM
    return pl.pallas_call(
        _sb_kernel,
        grid=(bsz, p // T),
        in_specs=[
            pl.BlockSpec((None, T, hw), lambda b, i: (b, i, 3)),
            pl.BlockSpec((None, T, hw), lambda b, i: (b, i, 6)),
            pl.BlockSpec((None, p, hw), lambda b, i: (b, 0, 4)),
            pl.BlockSpec((None, p, hw), lambda b, i: (b, 0, 5)),
        ],
        out_specs=pl.BlockSpec((None, T, hw), lambda b, i: (b, i, 0)),
        out_shape=jax.ShapeDtypeStruct((bsz, p, hw), BF16),
        scratch_shapes=[
            pltpu.VMEM((D_HEADS, T, T), BF16),
            pltpu.VMEM((D_HEADS, T, T), F32),
            pltpu.VMEM((D_HEADS, T, T), F32),
            pltpu.VMEM((SB_FIRST, D_HEADS, T, T), BF16),
            pltpu.VMEM((SB_FIRST, D_HEADS, T, T), BF16),
            pltpu.VMEM((SB_FIRST, D_HEADS, T, T), F32),
            pltpu.VMEM((SB_FIRST, D_HEADS, T, T), BF16),
        ],
        compiler_params=_params(("parallel", "arbitrary")),
        name="stick_breaking",
    )(pcd, pcd, pcd, pcd)


def _rel_bucket_np(rel):
    half = REL_BUCKETS // 2
    max_exact = half // 2
    n = -rel
    ret = np.where(n < 0, half, 0)
    n = np.abs(n)
    edges = [math.ceil(max_exact * (REL_MAX_DIST / max_exact) ** (j / (half - max_exact)) - 1e-9)
             for j in range(1, half - max_exact)]
    large = max_exact + sum((n >= e).astype(np.int64) for e in edges)
    return ret + np.where(n < max_exact, n, large)


def _bias_tiles(rel_bias):
    i = np.arange(LANES)[:, None]
    j = np.arange(LANES)[None, :]
    idx = np.stack([_rel_bucket_np(j - i - LANES * d) for d in range(3)])
    assert (idx[2] == idx[2, 0, 0]).all()
    table = rel_bias.astype(F32) - rel_bias.astype(F32)[idx[2, 0, 0]]
    near = jnp.asarray(idx[:2], I32)[None]
    tiles = jnp.zeros((table.shape[1],) + near.shape[1:], F32)
    for bucket in range(REL_BUCKETS):
        tiles = jnp.where(near == bucket, table[bucket][:, None, None, None], tiles)
    return tiles


def _layout_ab(w):
    aq, ak, av, ag, iq, ik, iw, bq, bk, bv, bg, ba = jnp.split(w, np.cumsum(SPLIT_AB)[:-1].tolist(), axis=1)
    pad = lambda t: jnp.pad(t, ((0, 0), (0, LANES - t.shape[1])))
    cols = [aq, ak, av, ag, iq, bv, bg, bq, bk, jnp.concatenate([ik, ik], axis=1), pad(iw), pad(ba),
            jnp.zeros((w.shape[0], W_AB_PAD - W_AB_USED), w.dtype)]
    return jnp.concatenate(cols, axis=1).astype(BF16)


def kernel(x, meta_tokens, rel_bias, norm_g, final_g, w_in_ab, gla_gate_w2, gla_gate_b, w_out_ab,
           w_in_cd, w_out_cd):
    bsz, seq, d = x.shape
    p = seq + PAD_FRONT + N_META
    depth = norm_g.shape[0]
    topk = min(TOPK_MAX, seq // 4)
    h = jnp.concatenate([jnp.zeros((bsz, PAD_FRONT, d), x.dtype),
                         jnp.broadcast_to(meta_tokens.astype(x.dtype)[None], (bsz, N_META, d)),
                         x], axis=1)
    bias_tiles = _bias_tiles(rel_bias)
    tables = _retention_tables(p)
    assert depth >= 1
    hn = None
    for layer in range(depth):
        j = layer // 2
        w_in = _layout_ab(w_in_ab[j]) if layer % 2 == 0 else w_in_cd[j].astype(BF16)
        proj = _inproj(h, norm_g[layer], w_in) if hn is None else _proj(hn, w_in)
        if layer % 2 == 0:
            first = _dsa(proj, bias_tiles, topk)
            w2 = jnp.pad(gla_gate_w2[j], ((0, LANES - GLA_GATE_RANK), (0, 0))).astype(BF16)
            second = _gla(proj, w2, gla_gate_b[j].reshape(1, -1).astype(F32))
            w_out = w_out_ab[j]
        else:
            first = _retention(proj, tables)
            second = _stick_breaking(proj)
            w_out = w_out_cd[j]
        if layer + 1 < depth:
            h, hn = _outproj(first, second, w_out, h, norm_g[layer + 1])
    return _outproj_norm(first, second, w_out, h, final_g, seq)
```
